```python
import jax
import jax.numpy as jnp
from jax import lax
import numpy as np

D_MODEL = 2048
BATCH = 1
SEQ = 8192
DEPTH = 2

MIX_WIDTH = D_MODEL
RET_WIDTH = D_MODEL // 4
RET_HEAD_DIM = 128
RET_HEADS = RET_WIDTH // RET_HEAD_DIM
RET_CHUNK = 128
RWKV_WIDTH = D_MODEL // 4
RWKV_HEAD_DIM = 64
RWKV_HEADS = RWKV_WIDTH // RWKV_HEAD_DIM
RWKV_DECAY_RANK = max(32, int(round(1.8 * D_MODEL ** 0.5 / 32)) * 32)
RWKV_A_RANK = max(32, int(round(1.8 * D_MODEL ** 0.5 / 32)) * 32)
RWKV_V_RANK = max(32, int(round(1.3 * D_MODEL ** 0.5 / 32)) * 32)
RWKV_GATE_RANK = max(32, int(round(0.6 * D_MODEL ** 0.8 / 32)) * 32)
RWKV_GN_EPS = 64e-5
NSA_WIDTH = MIX_WIDTH - RET_WIDTH - RWKV_WIDTH
NSA_HEAD_DIM = 64
NSA_HEADS = NSA_WIDTH // NSA_HEAD_DIM
NSA_KV_GROUPS = 4
NSA_KV_WIDTH = NSA_KV_GROUPS * NSA_HEAD_DIM
CMP_BLOCK = 32
CMP_STRIDE = 16
CMP_HIDDEN = 4 * NSA_HEAD_DIM
SLC_BLOCK = 64
SLC_TOPK = 16
WINDOW = 512
Q_BLOCK = 128
ROPE_THETA = 10000.0
MLP_HIDDEN = 4 * D_MODEL
RMS_EPS = 1e-6
NEG_INF = -1e30
FORCED_SCORE = 1e9

RET_IN_SIZES = (RET_WIDTH,) * 4
RWKV_IN_SIZES = (RWKV_WIDTH,) * 3 + (RWKV_DECAY_RANK, RWKV_A_RANK, RWKV_GATE_RANK)
NSA_IN_SIZES = (NSA_WIDTH,) + (NSA_KV_WIDTH,) * 6 + (3 * NSA_HEADS,)
RET_IN = sum(RET_IN_SIZES)
RWKV_IN = sum(RWKV_IN_SIZES)
NSA_IN = sum(NSA_IN_SIZES)
IN_WIDTH = RET_IN + RWKV_IN + NSA_IN

kernel_name = 'hybrid_retention_rwkv7_nsa_block'


def _split(a, sizes):
    return jnp.split(a, np.cumsum(np.array(sizes))[:-1].tolist(), axis=-1)


def _rms_norm(x, gain=None, eps=RMS_EPS):
    xf = x.astype(jnp.float32)
    y = xf * lax.rsqrt(jnp.mean(xf * xf, axis=-1, keepdims=True) + eps)
    if gain is not None:
        y = y * gain.astype(jnp.float32)
    return y.astype(x.dtype)


def _rope(x, pos):
    half = x.shape[-1] // 2
    inv_freq = ROPE_THETA ** (-jnp.arange(half, dtype=jnp.float32) / half)
    ang = pos.astype(jnp.float32)[:, None] * inv_freq[None, :]
    cos = jnp.cos(ang)[None, :, None, :]
    sin = jnp.sin(ang)[None, :, None, :]
    xf = x.astype(jnp.float32)
    x1, x2 = xf[..., :half], xf[..., half:]
    return jnp.concatenate([x1 * cos - x2 * sin, x1 * sin + x2 * cos], axis=-1).astype(x.dtype)


def _token_shift(p):
    return jnp.pad(p, ((0, 0), (1, 0), (0, 0)))[:, :-1]


def _retention(q, k, v, gate):
    B, T, _ = q.shape
    H, Dh, C = RET_HEADS, RET_HEAD_DIM, RET_CHUNK
    n_chunks = T // C
    pos = jnp.arange(T)
    qf = _rope(q.reshape(B, T, H, Dh), pos).astype(jnp.float32)
    kf = _rope(k.reshape(B, T, H, Dh), pos).astype(jnp.float32) * Dh ** -0.5
    vf = v.reshape(B, T, H, Dh).astype(jnp.float32)
    log_gamma = jnp.log(1.0 - 2.0 ** (-5.0 - jnp.arange(H, dtype=jnp.float32)))
    n = jnp.arange(C, dtype=jnp.float32)
    lag = n[:, None] - n[None, :]
    decay_mat = jnp.where(lag >= 0, jnp.exp(jnp.maximum(lag, 0.0)[None] * log_gamma[:, None, None]), 0.0)
    q_decay = jnp.exp((n + 1.0)[None, :] * log_gamma[:, None])[None, :, :, None]
    k_decay = jnp.exp((C - 1.0 - n)[None, :] * log_gamma[:, None])[None, :, :, None]
    chunk_decay = jnp.exp(C * log_gamma)[None, :, None, None]

    def to_chunks(a):
        return a.reshape(B, n_chunks, C, H, Dh).transpose(1, 0, 3, 2, 4)

    def chunk_step(state, inp):
        qc, kc, vc = inp
        inner = jnp.einsum('bhnd,bhmd->bhnm', qc, kc) * decay_mat[None]
        out = jnp.einsum('bhnm,bhmd->bhnd', inner, vc) + jnp.einsum('bhnd,bhde->bhne', qc, state) * q_decay
        state = chunk_decay * state + jnp.einsum('bhmd,bhme->bhde', kc * k_decay, vc)
        return state, out

    state0 = jnp.zeros((B, H, Dh, Dh), jnp.float32)
    _, out = lax.scan(chunk_step, state0, (to_chunks(qf), to_chunks(kf), to_chunks(vf)))
    out = out.transpose(1, 0, 3, 2, 4).reshape(B, T, H, Dh)
    out = _rms_norm(out).reshape(B, T, H * Dh)
    return (jax.nn.silu(gate.astype(jnp.float32)) * out).astype(q.dtype)


def _wkv7(r, w, k, v, a, b):
    def step(S, inp):
        r_t, w_t, k_t, v_t, a_t, b_t = inp
        sa = jnp.einsum('bhij,bhj->bhi', S, a_t)
        S = S * w_t[:, :, None, :] + sa[..., :, None] * b_t[..., None, :] + v_t[..., :, None] * k_t[..., None, :]
        return S, jnp.einsum('bhij,bhj->bhi', S, r_t)

    B, T, H, N = r.shape
    seq_major = tuple(t.transpose(1, 0, 2, 3) for t in (r, w, k, v, a, b))
    _, y = lax.scan(step, jnp.zeros((B, H, N, N), jnp.float32), seq_major)
    return y.transpose(1, 0, 2, 3)


def _rwkv7(p, vres_p, v_first, mix, vres_mix, w0, w2, a0, a2, v0, v2, g2, k_k, k_a, r_k, ln_g, ln_b):
    B, T, _ = p.shape
    H, N = RWKV_HEADS, RWKV_HEAD_DIM
    f32 = jnp.float32
    dt = p.dtype
    p = (p + mix * (_token_shift(p) - p)).astype(f32)
    r, k, v, xw, xa, xg = _split(p, RWKV_IN_SIZES)
    w = -jax.nn.softplus(-(w0 + jnp.tanh(xw) @ w2)) - 0.5
    decay = jnp.exp(-jnp.exp(w))
    if vres_p is None:
        v_first = v
    else:
        xv = (vres_p + vres_mix * (_token_shift(vres_p) - vres_p)).astype(f32)
        v = v + (v_first - v) * jax.nn.sigmoid(v0 + xv @ v2)
    a = jax.nn.sigmoid(a0 + xa @ a2)
    g = jax.nn.sigmoid(xg) @ g2
    kk = (k * k_k).reshape(B, T, H, N)
    kk = kk / jnp.maximum(jnp.sqrt(jnp.sum(kk * kk, axis=-1, keepdims=True)), 1e-12)
    k = k * (1.0 + (a - 1.0) * k_a)
    rh = r.reshape(B, T, H, N)
    kh = k.reshape(B, T, H, N)
    vh = v.reshape(B, T, H, N)
    ah = a.reshape(B, T, H, N)
    y = _wkv7(rh, decay.reshape(B, T, H, N), kh, vh, -kk, kk * ah)
    mu = jnp.mean(y, axis=-1, keepdims=True)
    var = jnp.mean(jnp.square(y - mu), axis=-1, keepdims=True)
    y = (y - mu) * lax.rsqrt(var + RWKV_GN_EPS) * ln_g.reshape(H, N) + ln_b.reshape(H, N)
    y = y + jnp.sum(rh * kh * r_k, axis=-1, keepdims=True) * vh
    return (y.reshape(B, T, H * N) * g).astype(dt), v_first


def _nsa(q, kc, vc, ks, vs, kw, vw, gate_logits, q_norm, k_norm, cmp_pe, cmp_k_w1, cmp_k_w2, cmp_v_w1, cmp_v_w2):
    B, T, _ = q.shape
    H, G, Dh = NSA_HEADS, NSA_KV_GROUPS, NSA_HEAD_DIM
    Hg = H // G
    f32 = jnp.float32
    n_q_blocks = T // Q_BLOCK
    n_cmp = (T - CMP_BLOCK) // CMP_STRIDE + 1
    n_slc = T // SLC_BLOCK
    n_sel = min(SLC_TOPK, n_slc)
    pos = jnp.arange(T)

    def kv(t):
        return t.reshape(B, T, G, Dh)

    q = _rope(_rms_norm(q.reshape(B, T, H, Dh), q_norm), pos) * Dh ** -0.5

    cmp_start = jnp.arange(n_cmp) * CMP_STRIDE
    cmp_end = cmp_start + CMP_BLOCK - 1
    gather_ix = cmp_start[:, None] + jnp.arange(CMP_BLOCK)[None, :]

    def compress(t, pe, w1, w2):
        blocks = kv(t)[:, gather_ix] + pe[None, None, :, None, :]
        blocks = blocks.transpose(0, 1, 3, 2, 4).reshape(B, n_cmp, G, CMP_BLOCK * Dh)
        return jax.nn.gelu(blocks @ w1) @ w2

    k_cmp = _rope(_rms_norm(compress(kc, cmp_pe[0], cmp_k_w1, cmp_k_w2), k_norm[0]), cmp_end)
    v_cmp = compress(vc, cmp_pe[1], cmp_v_w1, cmp_v_w2)

    slc_start = jnp.arange(n_slc) * SLC_BLOCK
    cmp_in_slc = ((cmp_start[:, None] <= slc_start[None, :] + SLC_BLOCK - 1)
                  & (cmp_end[:, None] >= slc_start[None, :])).astype(f32)

    def blockify(t):
        return t.reshape(B, n_slc, SLC_BLOCK, G, Dh).transpose(0, 3, 1, 2, 4)

    k_slc = blockify(_rope(_rms_norm(kv(ks), k_norm[1]), pos))
    v_slc = blockify(kv(vs))

    pad = ((0, 0), (WINDOW, 0), (0, 0), (0, 0))
    k_win = jnp.pad(_rope(_rms_norm(kv(kw), k_norm[2]), pos), pad)
    v_win = jnp.pad(kv(vw), pad)

    b_ix = jnp.arange(B)[:, None, None, None]
    g_ix = jnp.arange(G)[None, None, :, None]
    blk_ids = jnp.arange(n_slc)[None, :]

    def masked_softmax(s, visible):
        return jax.nn.softmax(jnp.where(visible, s.astype(f32), NEG_INF), axis=-1)

    def query_block(args):
        qb, gb, qi = args
        t = qi * Q_BLOCK + jnp.arange(Q_BLOCK)
        vis_c = (cmp_end[None, :] <= t[:, None])[None, :, None, None, :]
        p_c = masked_softmax(jnp.einsum('bqghd,bngd->bqghn', qb, k_cmp), vis_c) * vis_c
        o_c = jnp.einsum('bqghn,bngd->bqghd', p_c.astype(v_cmp.dtype), v_cmp)
        score = jnp.einsum('bqghn,ns->bqgs', p_c, cmp_in_slc)
        cur = (t // SLC_BLOCK)[:, None]
        forced = (blk_ids == 0) | (blk_ids == cur) | (blk_ids == cur - 1)
        score = jnp.where(forced[None, :, None, :], FORCED_SCORE, score)
        score = jnp.where((blk_ids <= cur)[None, :, None, :], score, -jnp.inf)
        _, sel = lax.top_k(score, n_sel)
        k_sel = k_slc[b_ix, g_ix, sel]
        v_sel = v_slc[b_ix, g_ix, sel]
        tok = sel[..., None] * SLC_BLOCK + jnp.arange(SLC_BLOCK)
        vis_s = (tok <= t[None, :, None, None, None]).reshape(B, Q_BLOCK, G, 1, n_sel * SLC_BLOCK)
        s_s = jnp.einsum('bqghd,bqgnkd->bqghnk', qb, k_sel)
        p_s = masked_softmax(s_s.reshape(B, Q_BLOCK, G, Hg, n_sel * SLC_BLOCK), vis_s)
        o_s = jnp.einsum('bqghnk,bqgnkd->bqghd', p_s.reshape(s_s.shape).astype(v_sel.dtype), v_sel)
        start = qi * Q_BLOCK
        k_w = lax.dynamic_slice_in_dim(k_win, start, Q_BLOCK + WINDOW, axis=1)
        v_w = lax.dynamic_slice_in_dim(v_win, start, Q_BLOCK + WINDOW, axis=1)
        kpos = start - WINDOW + jnp.arange(Q_BLOCK + WINDOW)
        lag = t[:, None] - kpos[None, :]
        vis_w = ((lag >= 0) & (lag < WINDOW) & (kpos[None, :] >= 0))[None, :, None, None, :]
        p_w = masked_softmax(jnp.einsum('bqghd,bkgd->bqghk', qb, k_w), vis_w)
        o_w = jnp.einsum('bqghk,bkgd->bqghd', p_w.astype(v_w.dtype), v_w)
        gate = jax.nn.sigmoid(gb.astype(f32))
        out = gate[..., 0:1] * o_c + gate[..., 1:2] * o_s + gate[..., 2:3] * o_w
        return out.astype(qb.dtype)

    q_blocks = q.reshape(B, n_q_blocks, Q_BLOCK, G, Hg, Dh).transpose(1, 0, 2, 3, 4, 5)
    g_blocks = gate_logits.reshape(B, n_q_blocks, Q_BLOCK, G, Hg, 3).transpose(1, 0, 2, 3, 4, 5)
    out = lax.map(query_block, (q_blocks, g_blocks, jnp.arange(n_q_blocks)))
    return out.transpose(1, 0, 2, 3, 4, 5).reshape(B, T, NSA_WIDTH)


def setup_inputs(seed: int = 0) -> dict:
    key = jax.random.key(seed)
    keys = iter(jax.random.split(key, 40))

    def nrm(shape, scale):
        return jax.random.normal(next(keys), shape, jnp.float32) * scale

    def uni(shape, lo, hi):
        return jax.random.uniform(next(keys), shape, jnp.float32, lo, hi)

    L, Lv = DEPTH, DEPTH - 1
    RW, H, N, Dh = RWKV_WIDTH, RWKV_HEADS, RWKV_HEAD_DIM, NSA_HEAD_DIM
    return {
        'x': nrm((BATCH, SEQ, D_MODEL), 1.0),
        'ln1_g': 1.0 + nrm((L, D_MODEL), 0.02),
        'w_in': nrm((L, D_MODEL, IN_WIDTH), D_MODEL ** -0.5),
        'w_in_vres': nrm((Lv, D_MODEL, RWKV_V_RANK), D_MODEL ** -0.5),
        'rwkv_mix': uni((L, RWKV_IN), 0.0, 1.0),
        'rwkv_vres_mix': uni((Lv, RWKV_V_RANK), 0.0, 1.0),
        'rwkv_w0': uni((L, RW), -6.0, -1.0),
        'rwkv_w2': nrm((L, RWKV_DECAY_RANK, RW), RWKV_DECAY_RANK ** -0.5),
        'rwkv_a0': nrm((L, RW), 0.01),
        'rwkv_a2': nrm((L, RWKV_A_RANK, RW), RWKV_A_RANK ** -0.5),
        'rwkv_v0': 1.0 + nrm((Lv, RW), 0.1),
        'rwkv_v2': nrm((Lv, RWKV_V_RANK, RW), RWKV_V_RANK ** -0.5),
        'rwkv_g2': nrm((L, RWKV_GATE_RANK, RW), RWKV_GATE_RANK ** -0.5),
        'rwkv_k_k': 0.85 + nrm((L, RW), 0.02),
        'rwkv_k_a': 1.0 + nrm((L, RW), 0.02),
        'rwkv_r_k': nrm((L, H, N), 0.1),
        'rwkv_ln_g': 1.0 + nrm((L, RW), 0.02),
        'rwkv_ln_b': nrm((L, RW), 0.01),
        'nsa_q_norm': 1.0 + nrm((L, Dh), 0.02),
        'nsa_k_norm': 1.0 + nrm((L, 3, Dh), 0.02),
        'nsa_cmp_pe': nrm((L, 2, CMP_BLOCK, Dh), 0.02),
        'nsa_cmp_k_w1': nrm((L, CMP_BLOCK * Dh, CMP_HIDDEN), (CMP_BLOCK * Dh) ** -0.5),
        'nsa_cmp_k_w2': nrm((L, CMP_HIDDEN, Dh), CMP_HIDDEN ** -0.5),
        'nsa_cmp_v_w1': nrm((L, CMP_BLOCK * Dh, CMP_HIDDEN), (CMP_BLOCK * Dh) ** -0.5),
        'nsa_cmp_v_w2': nrm((L, CMP_HIDDEN, Dh), CMP_HIDDEN ** -0.5),
        'w_out': nrm((L, MIX_WIDTH, D_MODEL), MIX_WIDTH ** -0.5),
        'ln2_g': 1.0 + nrm((L, D_MODEL), 0.02),
        'w_up': nrm((L, D_MODEL, MLP_HIDDEN), D_MODEL ** -0.5),
        'w_down': nrm((L, MLP_HIDDEN, D_MODEL), MLP_HIDDEN ** -0.5),
    }


def reference(x, ln1_g, w_in, w_in_vres, rwkv_mix, rwkv_vres_mix, rwkv_w0, rwkv_w2, rwkv_a0, rwkv_a2,
              rwkv_v0, rwkv_v2, rwkv_g2, rwkv_k_k, rwkv_k_a, rwkv_r_k, rwkv_ln_g, rwkv_ln_b,
              nsa_q_norm, nsa_k_norm, nsa_cmp_pe, nsa_cmp_k_w1, nsa_cmp_k_w2, nsa_cmp_v_w1, nsa_cmp_v_w2,
              w_out, ln2_g, w_up, w_down):
    v_first = None
    for l in range(DEPTH):
        h = _rms_norm(x, ln1_g[l])
        if l == 0:
            w_l = w_in[l]
            group_sizes = (RET_IN, RWKV_IN, NSA_IN)
            vres_mix, v0, v2 = None, None, None
        else:
            w_l = jnp.concatenate([w_in[l], w_in_vres[l - 1]], axis=1)
            group_sizes = (RET_IN, RWKV_IN, NSA_IN, RWKV_V_RANK)
            vres_mix, v0, v2 = rwkv_vres_mix[l - 1], rwkv_v0[l - 1], rwkv_v2[l - 1]
        proj = jnp.einsum('btd,dn->btn', h, w_l)
        parts = _split(proj, group_sizes)
        vres_p = parts[3] if l > 0 else None
        o_ret = _retention(*_split(parts[0], RET_IN_SIZES))
        o_rwkv, v_first = _rwkv7(parts[1], vres_p, v_first, rwkv_mix[l], vres_mix, rwkv_w0[l], rwkv_w2[l],
                                 rwkv_a0[l], rwkv_a2[l], v0, v2, rwkv_g2[l], rwkv_k_k[l], rwkv_k_a[l],
                                 rwkv_r_k[l], rwkv_ln_g[l], rwkv_ln_b[l])
        o_nsa = _nsa(*_split(parts[2], NSA_IN_SIZES), nsa_q_norm[l], nsa_k_norm[l], nsa_cmp_pe[l],
                     nsa_cmp_k_w1[l], nsa_cmp_k_w2[l], nsa_cmp_v_w1[l], nsa_cmp_v_w2[l])
        mixed = jnp.concatenate([o_ret, o_rwkv, o_nsa], axis=-1)
        x = x + jnp.einsum('btm,md->btd', mixed, w_out[l])
        h = _rms_norm(x, ln2_g[l])
        up = jnp.square(jax.nn.relu(jnp.einsum('btd,df->btf', h, w_up[l])))
        x = x + jnp.einsum('btf,fd->btd', up, w_down[l])
    return x
```

```python
import functools

import numpy as np
import jax
import jax.numpy as jnp
from jax import lax
from jax.experimental import pallas as pl
from jax.experimental.pallas import tpu as pltpu

F32 = jnp.float32
BF16 = jnp.bfloat16

D_MODEL = 2048
RET_HEADS, RET_DH = 4, 128
RET_W = RET_HEADS * RET_DH
RW_HEADS, RW_N = 8, 64
RW_W = RW_HEADS * RW_N
RW_DECAY_RANK, RW_A_RANK, RW_V_RANK, RW_GATE_RANK = 96, 96, 64, 256
RW_GN_EPS = 64e-5
NSA_HEADS, NSA_G, NSA_DH = 16, 4, 64
NSA_HG = NSA_HEADS // NSA_G
NSA_W = NSA_HEADS * NSA_DH
NSA_KV_W = NSA_G * NSA_DH
CMP_BLOCK, CMP_STRIDE, CMP_HIDDEN = 32, 16, 256
SLC_BLOCK, SLC_TOPK, WINDOW, Q_BLOCK = 64, 16, 512, 128
ROPE_THETA = 10000.0
MLP_HIDDEN = 4 * D_MODEL
RMS_EPS = 1e-6
NEG_INF = -1e30
FORCED_SCORE = 1e9

LANES = 128
VMEM_LIMIT = 56 * 1024 * 1024

C_RET = 0
C_RW_RKV = 2048
C_RW_XW = 3584
C_RW_XA = 3712
C_RW_XG = 3840
C_RW_XV = 4096
C_NSA_Q = 4224
C_NSA_KS = 5248
C_NSA_KW = 5504
C_NSA_KC = 5760
C_NSA_VC = 6016
C_NSA_VS = 6272
C_NSA_VW = 6528
C_NSA_GATE = 6784
PROJ_W = 6912


def _cparams(sem):
    return pltpu.CompilerParams(dimension_semantics=sem, vmem_limit_bytes=VMEM_LIMIT)


def _dot(a, b):
    return jnp.dot(a, b, preferred_element_type=F32)


def _dot_t(a, b):
    return lax.dot_general(a, b, (((1,), (1,)), ((), ())), preferred_element_type=F32)


def _dot_tl(a, b):
    return lax.dot_general(a, b, (((0,), (0,)), ((), ())), preferred_element_type=F32)


def _split_dot(x, m_bf16, passes):
    hi = x.astype(BF16)
    acc = _dot(hi, m_bf16)
    rem = x - hi.astype(F32)
    for _ in range(passes - 1):
        piece = rem.astype(BF16)
        acc = acc + _dot(piece, m_bf16)
        rem = rem - piece.astype(F32)
    return acc


def _rmsnorm_kernel(x_ref, g_ref, o_ref):
    x = x_ref[...]
    ms = jnp.mean(x * x, axis=-1, keepdims=True)
    o_ref[...] = (x * lax.rsqrt(ms + RMS_EPS) * g_ref[...]).astype(o_ref.dtype)


def _rmsnorm(x, g, tm=512):
    t, d = x.shape
    return pl.pallas_call(
        _rmsnorm_kernel,
        out_shape=jax.ShapeDtypeStruct((t, d), BF16),
        grid=(t // tm,),
        in_specs=[pl.BlockSpec((tm, d), lambda i: (i, 0)), pl.BlockSpec((1, d), lambda i: (0, 0))],
        out_specs=pl.BlockSpec((tm, d), lambda i: (i, 0)),
        compiler_params=_cparams(("parallel",)),
        name="rmsnorm",
    )(x, g.reshape(1, d))


def _mm_kernel(*refs, nk, act, has_res):
    if has_res:
        a_ref, b_ref, r_ref, o_ref, acc_ref = refs
    else:
        a_ref, b_ref, o_ref, acc_ref = refs
        r_ref = None
    k = pl.program_id(2)

    @pl.when(k == 0)
    def _():
        acc_ref[...] = jnp.zeros_like(acc_ref)

    acc_ref[...] += _dot(a_ref[...], b_ref[...])

    @pl.when(k == nk - 1)
    def _():
        y = acc_ref[...]
        if act == "relu2":
            y = jnp.square(jnp.maximum(y, 0.0))
        if has_res:
            y = y + r_ref[...]
        o_ref[...] = y.astype(o_ref.dtype)


def _pick(n, cands):
    for c in cands:
        if n % c == 0:
            return c
    raise ValueError(f"no tile for {n}")


def _matmul(a, b, *, res=None, act=None, out_dtype=F32, name="matmul"):
    m, kd = a.shape
    _, n = b.shape
    tm = _pick(m, (1024, 512, 256, 128))
    tn = _pick(n, (1024, 768, 512, 256, 128))
    tk = _pick(kd, (2048, 1024, 512))
    nk = kd // tk
    in_specs = [pl.BlockSpec((tm, tk), lambda i, j, k: (i, k)), pl.BlockSpec((tk, tn), lambda i, j, k: (k, j))]
    args = [a, b]
    if res is not None:
        in_specs.append(pl.BlockSpec((tm, tn), lambda i, j, k: (i, j)))
        args.append(res)
    return pl.pallas_call(
        functools.partial(_mm_kernel, nk=nk, act=act, has_res=res is not None),
        out_shape=jax.ShapeDtypeStruct((m, n), out_dtype),
        grid=(m // tm, n // tn, nk),
        in_specs=in_specs,
        out_specs=pl.BlockSpec((tm, tn), lambda i, j, k: (i, j)),
        scratch_shapes=[pltpu.VMEM((tm, tn), F32)],
        compiler_params=_cparams(("parallel", "parallel", "arbitrary")),
        name=name,
    )(*args)


RET_CHUNK = 256


def _retention_kernel(q_ref, k_ref, v_ref, g_ref, cos_ref, sin_ref, lg_ref, o_ref, state_ref, *, chunk):
    c = pl.program_id(1)

    @pl.when(c == 0)
    def _():
        state_ref[...] = jnp.zeros_like(state_ref)

    lg_row = lg_ref[0]
    lg_dh = lg_row[:, :RET_DH]
    cos = cos_ref[...]
    sin = sin_ref[...]

    def rope(x):
        return x * cos + pltpu.roll(x, RET_DH // 2, axis=1) * sin

    q = rope(q_ref[...])
    k = rope(k_ref[...]) * (RET_DH ** -0.5)
    v = v_ref[...]
    n_col = lax.broadcasted_iota(jnp.int32, (chunk, RET_DH), 0).astype(F32)
    row = lax.broadcasted_iota(jnp.int32, (chunk, chunk), 0)
    col = lax.broadcasted_iota(jnp.int32, (chunk, chunk), 1)
    lag = (row - col).astype(F32)
    decay_mat = jnp.where(lag >= 0, jnp.exp(jnp.maximum(lag, 0.0) * lg_row), 0.0)
    q_decay = jnp.exp((n_col + 1.0) * lg_dh)
    k_decay = jnp.exp((chunk - 1.0 - n_col) * lg_dh)
    chunk_decay = jnp.exp(chunk * lg_dh)

    qb = q.astype(BF16)
    kb = k.astype(BF16)
    vb = v.astype(BF16)
    state = state_ref[...]
    inner = _dot_t(qb, kb) * decay_mat
    out = _dot(inner.astype(BF16), vb) + _dot(qb, state.astype(BF16)) * q_decay
    state_ref[...] = chunk_decay * state + _dot_tl((k * k_decay).astype(BF16), vb)

    out = out * lax.rsqrt(jnp.mean(out * out, axis=-1, keepdims=True) + RMS_EPS)
    gate = g_ref[...]
    o_ref[...] = (gate * jax.nn.sigmoid(gate) * out).astype(o_ref.dtype)


def _retention(proj, cos, sin):
    t = proj.shape[0]
    chunk = RET_CHUNK
    hb = C_RET // RET_DH
    log_gamma = np.log(1.0 - 2.0 ** (-5.0 - np.arange(RET_HEADS, dtype=np.float64)))
    lg = jnp.asarray(np.broadcast_to(log_gamma[:, None, None], (RET_HEADS, 1, chunk)), F32)

    def col(off):
        return pl.BlockSpec((chunk, RET_DH), lambda h, c, off=off: (c, hb + off * RET_HEADS + h))

    tab = pl.BlockSpec((chunk, RET_DH), lambda h, c: (c, 0))
    return pl.pallas_call(
        functools.partial(_retention_kernel, chunk=chunk),
        out_shape=jax.ShapeDtypeStruct((t, RET_W), BF16),
        grid=(RET_HEADS, t // chunk),
        in_specs=[col(0), col(1), col(2), col(3), tab, tab,
                  pl.BlockSpec((1, 1, chunk), lambda h, c: (h, 0, 0))],
        out_specs=pl.BlockSpec((chunk, RET_DH), lambda h, c: (c, h)),
        scratch_shapes=[pltpu.VMEM((RET_DH, RET_DH), F32)],
        compiler_params=_cparams(("parallel", "arbitrary")),
        name="retention",
    )(proj, proj, proj, proj, cos, sin, lg)


def _dot_f32(a, b):
    a_hi = a.astype(BF16)
    b_hi = b.astype(BF16)
    a_lo = (a - a_hi.astype(F32)).astype(BF16)
    b_lo = (b - b_hi.astype(F32)).astype(BF16)
    return _dot(a_hi, b_hi) + (_dot(a_hi, b_lo) + _dot(a_lo, b_hi))


RW_PRE_TM = 256
_MIX_OFF = (0, 512, 1024, 1536, 1664, 1792, 2048, 2176)


def _rwkv_pre_kernel(r_ref, k_ref, v_ref, xw_ref, xa_ref, xg_ref, xv_ref,
                     rp_ref, kp_ref, vp_ref, xwp_ref, xap_ref, xgp_ref, xvp_ref,
                     mix_ref, w0_ref, w2_ref, a0_ref, a2_ref, v0_ref, v2_ref, g2_ref, kk_ref, ka_ref,
                     gsum_ref, vfirst_ref,
                     r_out, lw_out, k_out, v_out, an_out, b_out, g_out, *, use_vres):
    i = pl.program_id(0)

    def mixed(cur_ref, prev_ref, seg):
        cur = cur_ref[...]
        mix = mix_ref[:, _MIX_OFF[seg]:_MIX_OFF[seg + 1]]
        prev_row = jnp.where(i > 0, prev_ref[7:8, :], 0.0)
        rid = lax.broadcasted_iota(jnp.int32, cur.shape, 0)
        shifted = jnp.where(rid == 0, prev_row, pltpu.roll(cur, 1, axis=0))
        return cur + mix * (shifted - cur)

    r = mixed(r_ref, rp_ref, 0)
    k = mixed(k_ref, kp_ref, 1)
    v = mixed(v_ref, vp_ref, 2)
    xw = mixed(xw_ref, xwp_ref, 3)
    xa = mixed(xa_ref, xap_ref, 4)
    xg = mixed(xg_ref, xgp_ref, 5)

    z = -(w0_ref[...] + _dot_f32(jnp.tanh(xw), w2_ref[...]))
    softplus = jnp.maximum(z, 0.0) + jnp.log(1.0 + jnp.exp(-jnp.abs(z)))
    w = -softplus - 0.5
    lw_out[...] = -jnp.exp(w)
    a = jax.nn.sigmoid(a0_ref[...] + _dot_f32(xa, a2_ref[...]))
    g_out[...] = _dot_f32(jax.nn.sigmoid(xg), g2_ref[...])
    if use_vres:
        xv = mixed(xv_ref, xvp_ref, 6)
        v = v + (vfirst_ref[...] - v) * jax.nn.sigmoid(v0_ref[...] + _dot_f32(xv, v2_ref[...]))
    kk = k * kk_ref[...]
    ss = _split_dot(kk * kk, gsum_ref[...], 3)
    kk = kk / jnp.maximum(jnp.sqrt(ss), 1e-12)
    r_out[...] = r
    k_out[...] = k * (1.0 + (a - 1.0) * ka_ref[...])
    v_out[...] = v
    an_out[...] = -kk
    b_out[...] = kk * a


def _rwkv_pre(proj, mix, w0, w2, a0, a2, v0, v2, g2, k_k, k_a, gsum, v_first, use_vres):
    t = proj.shape[0]
    tm = RW_PRE_TM

    def cur(width, off):
        return pl.BlockSpec((tm, width), lambda i: (i, off // width))

    def prev(width, off):
        return pl.BlockSpec((8, width), lambda i: (jnp.maximum(i * (tm // 8) - 1, 0), off // width))

    def full(a):
        return pl.BlockSpec(a.shape, lambda i: (0,) * a.ndim)

    segs = [(RW_W, C_RW_RKV), (RW_W, C_RW_RKV + RW_W), (RW_W, C_RW_RKV + 2 * RW_W),
            (128, C_RW_XW), (128, C_RW_XA), (256, C_RW_XG), (128, C_RW_XV)]
    params = [mix, w0, w2, a0, a2, v0, v2, g2, k_k, k_a, gsum]
    row = pl.BlockSpec((tm, RW_W), lambda i: (i, 0))
    outs = pl.pallas_call(
        functools.partial(_rwkv_pre_kernel, use_vres=use_vres),
        out_shape=[jax.ShapeDtypeStruct((t, RW_W), F32)] * 7,
        grid=(t // tm,),
        in_specs=[cur(w, o) for w, o in segs] + [prev(w, o) for w, o in segs] + [full(p) for p in params] + [row],
        out_specs=[row] * 7,
        compiler_params=_cparams(("parallel",)),
        name="rwkv_pre",
    )(*([proj] * 14), *params, v_first)
    return outs


RW_CHUNK = 64


def _wkv_kernel(r_ref, lw_ref, k_ref, v_ref, an_ref, b_ref, g_ref, lng_ref, lnb_ref, rk_ref,
                gsum_ref, tril_ref, o_ref, s_ref):
    c = pl.program_id(0)
    C = RW_CHUNK
    N = RW_N

    @pl.when(c == 0)
    def _():
        s_ref[...] = jnp.zeros_like(s_ref)

    r = r_ref[...]
    lw = lw_ref[...]
    k = k_ref[...]
    v = v_ref[...]
    lw_hi = lw.astype(BF16)
    rem = lw - lw_hi.astype(F32)
    lw_mid = rem.astype(BF16)
    lw_lo = (rem - lw_mid.astype(F32)).astype(BF16)
    tril = tril_ref[...]
    lg = _dot(tril, lw_hi) + (_dot(tril, lw_mid) + _dot(tril, lw_lo))
    gam = jnp.exp(lg)
    ginv = jnp.exp(-lg)
    at = (an_ref[...] * jnp.exp(lg - lw)).astype(BF16)
    bt = (b_ref[...] * ginv).astype(BF16)
    kt = (k * ginv).astype(BF16)
    rt_f = r * gam
    rt = rt_f.astype(BF16)
    vb = v.astype(BF16)
    g_last = gam[C - 1:C, :]

    rowi = lax.broadcasted_iota(jnp.int32, (C, C), 0)
    coli = lax.broadcasted_iota(jnp.int32, (C, C), 1)
    strict = rowi > coli
    incl = rowi >= coli
    eye_c = (rowi == coli).astype(F32)

    ys = []
    for h in range(RW_HEADS):
        sl = slice(h * N, (h + 1) * N)
        ath, bth, kth, rth, vh = at[:, sl], bt[:, sl], kt[:, sl], rt[:, sl], vb[:, sl]
        a_ab = jnp.where(strict, _dot_t(ath, bth), 0.0)
        a_ak = jnp.where(strict, _dot_t(ath, kth), 0.0)
        a_rb = jnp.where(incl, _dot_t(rth, bth), 0.0).astype(BF16)
        a_rk = jnp.where(incl, _dot_t(rth, kth), 0.0).astype(BF16)
        tinv = eye_c + a_ab
        pw = a_ab
        n = 1
        while 2 * n < C:
            pwb = pw.astype(BF16)
            pw = _dot(pwb, pwb)
            tinv = tinv + _dot(tinv.astype(BF16), pw.astype(BF16))
            n *= 2
        tinvb = tinv.astype(BF16)
        p = _dot(tinvb, ath)
        q = _dot(tinvb, _dot(a_ak.astype(BF16), vh).astype(BF16))
        pb = p.astype(BF16)
        qb = q.astype(BF16)
        y1 = rt_f[:, sl] + _dot(a_rb, pb)
        y2 = _dot(a_rb, qb) + _dot(a_rk, vh)
        gl = g_last[:, sl]
        m_mat = _dot_tl(pb, bth) * gl
        n_mat = (_dot_tl(qb, bth) + _dot_tl(vh, kth)) * gl
        s0 = s_ref[h]
        s0b = s0.astype(BF16)
        ys.append(_dot_t(y1.astype(BF16), s0b) + y2)
        s_ref[h] = s0 * gl + _dot(s0b, m_mat.astype(BF16)) + n_mat
    y = jnp.concatenate(ys, axis=1)

    gsum = gsum_ref[...]
    inv_n = 1.0 / N
    mu = _split_dot(y, gsum, 3) * inv_n
    yc = y - mu
    var = _split_dot(yc * yc, gsum, 3) * inv_n
    yn = yc * lax.rsqrt(var + RW_GN_EPS) * lng_ref[...] + lnb_ref[...]
    bonus = _split_dot(r * k * rk_ref[...], gsum, 3)
    o_ref[...] = ((yn + bonus * v) * g_ref[...]).astype(o_ref.dtype)


def _wkv(r, lw, k, v, an, b, g, ln_g, ln_b, r_k, gsum):
    t = r.shape[0]
    C = RW_CHUNK
    tril = jnp.asarray(np.tril(np.ones((C, C), np.float32)), BF16)
    row = pl.BlockSpec((C, RW_W), lambda c: (c, 0))
    vec = pl.BlockSpec((1, RW_W), lambda c: (0, 0))
    return pl.pallas_call(
        _wkv_kernel,
        out_shape=jax.ShapeDtypeStruct((t, RW_W), BF16),
        grid=(t // C,),
        in_specs=[row] * 7 + [vec] * 3 + [pl.BlockSpec((RW_W, RW_W), lambda c: (0, 0)),
                                          pl.BlockSpec((C, C), lambda c: (0, 0))],
        out_specs=row,
        scratch_shapes=[pltpu.VMEM((RW_HEADS, RW_N, RW_N), F32)],
        compiler_params=_cparams(("arbitrary",)),
        name="wkv7",
    )(r, lw, k, v, an, b, g, ln_g, ln_b, r_k, gsum, tril)


NSA_PREP_TM = 512


def _nsa_prep_kernel(x_ref, cos_ref, sin_ref, w_ref, gmean_ref, o_ref):
    x = x_ref[...]
    ms = _split_dot(x * x, gmean_ref[...], 3)
    y = x * lax.rsqrt(ms + RMS_EPS) * w_ref[0]
    lane = lax.broadcasted_iota(jnp.int32, y.shape, 1)
    half = NSA_DH // 2
    rot = jnp.where((lane % NSA_DH) < half, pltpu.roll(y, LANES - half, axis=1), pltpu.roll(y, half, axis=1))
    res = (y * cos_ref[...] + rot * sin_ref[...]).astype(o_ref.dtype)
    o_ref[0] = res[:, :NSA_DH]
    o_ref[1] = res[:, NSA_DH:]


def _nsa_prep(proj, col_off, n_slabs, w_slabs, cos, sin, gmean):
    t = proj.shape[0]
    tm = NSA_PREP_TM
    base = col_off // LANES
    return pl.pallas_call(
        _nsa_prep_kernel,
        out_shape=jax.ShapeDtypeStruct((2 * n_slabs, t, NSA_DH), BF16),
        grid=(t // tm, n_slabs),
        in_specs=[pl.BlockSpec((tm, LANES), lambda i, s: (i, base + s)),
                  pl.BlockSpec((tm, LANES), lambda i, s: (i, 0)),
                  pl.BlockSpec((tm, LANES), lambda i, s: (i, 0)),
                  pl.BlockSpec((1, 1, LANES), lambda i, s: (s, 0, 0)),
                  pl.BlockSpec((LANES, LANES), lambda i, s: (0, 0))],
        out_specs=pl.BlockSpec((2, tm, NSA_DH), lambda i, s: (s, i, 0)),
        compiler_params=_cparams(("parallel", "parallel")),
        name="nsa_prep",
    )(proj, cos, sin, w_slabs, gmean)


def _compress_kernel(h_ref, w1_ref, w2_ref, pe_ref, nw_ref, cos_ref, sin_ref, o_ref):
    kind = pl.program_id(0)
    nc = h_ref.shape[2]
    half_in = CMP_STRIDE * NSA_DH
    hm = h_ref[0, 0]
    w1 = w1_ref[0]
    first = _dot(hm, w1[:half_in])
    second = _dot(hm, w1[half_in:])
    const = _dot(pe_ref[0], w1)[0:1, :]
    pre = first + pltpu.roll(second, nc - 1, axis=0) + const
    hid = 0.5 * pre * (1.0 + jnp.tanh(0.7978845608028654 * (pre + 0.044715 * pre * pre * pre)))
    out = _dot(hid.astype(BF16), w2_ref[0])
    y = out * lax.rsqrt(jnp.mean(out * out, axis=-1, keepdims=True) + RMS_EPS) * nw_ref[...]
    half = NSA_DH // 2
    rot = jnp.concatenate([y[:, half:], y[:, :half]], axis=1)
    roped = y * cos_ref[...] + rot * sin_ref[...]
    o_ref[0, 0] = jnp.where(kind == 0, roped, out).astype(o_ref.dtype)


def _compress(hmat, w1, w2, pe, nw, cos_c, sin_c):
    _, g, nc, width = hmat.shape
    return pl.pallas_call(
        _compress_kernel,
        out_shape=jax.ShapeDtypeStruct((2, g, nc, NSA_DH), BF16),
        grid=(2, g),
        in_specs=[pl.BlockSpec((1, 1, nc, width), lambda a, b: (a, b, 0, 0)),
                  pl.BlockSpec((1, 2 * width, CMP_HIDDEN), lambda a, b: (a, 0, 0)),
                  pl.BlockSpec((1, CMP_HIDDEN, NSA_DH), lambda a, b: (a, 0, 0)),
                  pl.BlockSpec((1, 8, 2 * width), lambda a, b: (a, 0, 0)),
                  pl.BlockSpec((1, NSA_DH), lambda a, b: (0, 0)),
                  pl.BlockSpec((nc, NSA_DH), lambda a, b: (0, 0)),
                  pl.BlockSpec((nc, NSA_DH), lambda a, b: (0, 0))],
        out_specs=pl.BlockSpec((1, 1, nc, NSA_DH), lambda a, b: (a, b, 0, 0)),
        compiler_params=_cparams(("parallel", "parallel")),
        name="nsa_compress",
    )(hmat, w1, w2, pe, nw, cos_c, sin_c)


SLC_KT = 512
WIN_KEYS = WINDOW + Q_BLOCK


def _nsa_attn_kernel(q_ref, gate_ref, kc_ref, vc_ref, ks_ref, vs_ref, kw_ref, vw_ref, cis_ref, o_ref, scoret_ref):
    qi = pl.program_id(1)
    nc = kc_ref.shape[2]
    nb = cis_ref.shape[1]
    rows = NSA_HG * Q_BLOCK
    start = qi * Q_BLOCK

    q4 = q_ref[...].reshape(rows, NSA_DH)
    tok = start + lax.broadcasted_iota(jnp.int32, (Q_BLOCK, 1), 0)
    tok4 = jnp.concatenate([tok] * NSA_HG, axis=0)

    def rep4(x):
        return jnp.concatenate([x] * NSA_HG, axis=0)

    kc = kc_ref[0, 0]
    vc = vc_ref[0, 0]
    s = _dot_t(q4, kc)
    cend = lax.broadcasted_iota(jnp.int32, (1, nc), 1) * CMP_STRIDE + (CMP_BLOCK - 1)
    vis = cend <= tok4
    sm = jnp.where(vis, s, NEG_INF)
    e = jnp.exp(sm - jnp.max(sm, axis=-1, keepdims=True))
    p = jnp.where(vis, e / jnp.sum(e, axis=-1, keepdims=True), 0.0)
    o_c = _dot(p.astype(BF16), vc)

    psum = p[0:Q_BLOCK]
    for hh in range(1, NSA_HG):
        psum = psum + p[hh * Q_BLOCK:(hh + 1) * Q_BLOCK]
    score = _split_dot(psum, cis_ref[...], 3)
    blk = lax.broadcasted_iota(jnp.int32, (1, nb), 1)
    cur = tok // SLC_BLOCK
    forced = (blk == 0) | (blk == cur) | (blk == cur - 1)
    score = jnp.where(forced, FORCED_SCORE, score)
    score = jnp.where(blk <= cur, score, -jnp.inf)
    scoret_ref[...] = score.T
    sub = lax.broadcasted_iota(jnp.int32, (nb, Q_BLOCK), 0)

    def rank_body(sp, rank):
        row = scoret_ref[pl.ds(sp, 1), :]
        mat = scoret_ref[...]
        ahead = (row > mat) | ((row == mat) & (sub > sp))
        return rank + jnp.where(ahead, 1.0, 0.0)

    rank = lax.fori_loop(0, nb, rank_body, jnp.zeros((nb, Q_BLOCK), F32))
    sel = jnp.where(rank < SLC_TOPK, 1.0, 0.0).T.astype(BF16)

    n_tiles = (start + Q_BLOCK + SLC_KT - 1) // SLC_KT
    blk_col = lax.broadcasted_iota(jnp.int32, (nb, 1), 0)

    def sel_body(j, carry):
        m_i, l_i, acc = carry
        k0 = pl.multiple_of(j * SLC_KT, SLC_KT)
        kt = ks_ref[0, pl.ds(k0, SLC_KT), :]
        vt = vs_ref[0, pl.ds(k0, SLC_KT), :]
        sj = _dot_t(q4, kt)
        kpos = k0 + lax.broadcasted_iota(jnp.int32, (1, SLC_KT), 1)
        expand = jnp.where((kpos // SLC_BLOCK) == blk_col, 1.0, 0.0).astype(BF16)
        member = _dot(sel, expand)
        ok = rep4(jnp.where((member > 0.5) & (kpos <= tok), 1.0, 0.0)) > 0.5
        smj = jnp.where(ok, sj, NEG_INF)
        m_new = jnp.maximum(m_i, jnp.max(smj, axis=-1, keepdims=True))
        alpha = jnp.exp(m_i - m_new)
        pj = jnp.where(ok, jnp.exp(smj - m_new), 0.0)
        l_new = alpha * l_i + jnp.sum(pj, axis=-1, keepdims=True)
        acc = alpha * acc + _dot(pj.astype(BF16), vt)
        return m_new, l_new, acc

    m0 = jnp.full((rows, 1), NEG_INF, F32)
    l0 = jnp.zeros((rows, 1), F32)
    a0 = jnp.zeros((rows, NSA_DH), F32)
    _, l_s, acc_s = lax.fori_loop(0, n_tiles, sel_body, (m0, l0, a0))
    o_s = acc_s / l_s

    kbase = pl.multiple_of(jnp.maximum(start - WINDOW, 0), Q_BLOCK)
    kw = kw_ref[0, pl.ds(kbase, WIN_KEYS), :]
    vw = vw_ref[0, pl.ds(kbase, WIN_KEYS), :]
    sw = _dot_t(q4, kw)
    lag = tok - (kbase + lax.broadcasted_iota(jnp.int32, (1, WIN_KEYS), 1))
    okw = rep4(jnp.where((lag >= 0) & (lag < WINDOW), 1.0, 0.0)) > 0.5
    smw = jnp.where(okw, sw, NEG_INF)
    ew = jnp.where(okw, jnp.exp(smw - jnp.max(smw, axis=-1, keepdims=True)), 0.0)
    pw = ew / jnp.sum(ew, axis=-1, keepdims=True)
    o_w = _dot(pw.astype(BF16), vw)

    gt = jax.nn.sigmoid(gate_ref[0])

    def gate_col(br):
        return jnp.concatenate([gt[:, 3 * hh + br:3 * hh + br + 1] for hh in range(NSA_HG)], axis=0)

    out = gate_col(0) * o_c + gate_col(1) * o_s + gate_col(2) * o_w
    o_ref[...] = jnp.concatenate([out[hh * Q_BLOCK:(hh + 1) * Q_BLOCK] for hh in range(NSA_HG)],
                                 axis=1).astype(o_ref.dtype)


def _nsa_attn(q_hm, gates, cmp_kv, ks, vs, kw, vw, cis):
    t = q_hm.shape[1]
    nq = t // Q_BLOCK
    nc = cmp_kv.shape[2]
    nb = cis.shape[1]
    kv = pl.BlockSpec((1, t, NSA_DH), lambda g, i: (g, 0, 0))
    return pl.pallas_call(
        _nsa_attn_kernel,
        out_shape=jax.ShapeDtypeStruct((t, NSA_W), BF16),
        grid=(NSA_G, nq),
        in_specs=[pl.BlockSpec((NSA_HG, Q_BLOCK, NSA_DH), lambda g, i: (g, i, 0)),
                  pl.BlockSpec((1, Q_BLOCK, LANES), lambda g, i: (g, i, 0)),
                  pl.BlockSpec((1, 1, nc, NSA_DH), lambda g, i: (0, g, 0, 0)),
                  pl.BlockSpec((1, 1, nc, NSA_DH), lambda g, i: (1, g, 0, 0)),
                  kv, kv, kv, kv,
                  pl.BlockSpec((nc, nb), lambda g, i: (0, 0))],
        out_specs=pl.BlockSpec((Q_BLOCK, NSA_HG * NSA_DH), lambda g, i: (i, g)),
        scratch_shapes=[pltpu.VMEM((nb, Q_BLOCK), F32)],
        compiler_params=_cparams(("parallel", "arbitrary")),
        name="nsa_attention",
    )(q_hm, gates, cmp_kv, cmp_kv, ks, vs, kw, vw, cis)


def _rope_tables(pos, dh, reps):
    half = dh // 2
    inv_freq = ROPE_THETA ** (-jnp.arange(half, dtype=F32) / half)
    ang = pos.astype(F32)[:, None] * inv_freq[None, :]
    cos = jnp.cos(ang)
    sin = jnp.sin(ang)
    cos_t = jnp.tile(jnp.concatenate([cos, cos], axis=1), (1, reps))
    sin_t = jnp.tile(jnp.concatenate([-sin, sin], axis=1), (1, reps))
    return cos_t, sin_t


def _pad_cols(a, width):
    return jnp.pad(a, ((0, 0), (0, width - a.shape[1])))


def _pad_rows(a, height):
    return jnp.pad(a, ((0, height - a.shape[0]), (0, 0)))


def _pack_w_in(w_l, w_vres):
    d = w_l.shape[0]
    o = 0
    ret = w_l[:, o:o + 4 * RET_W]; o += 4 * RET_W
    rkv = w_l[:, o:o + 3 * RW_W]; o += 3 * RW_W
    xw = w_l[:, o:o + RW_DECAY_RANK]; o += RW_DECAY_RANK
    xa = w_l[:, o:o + RW_A_RANK]; o += RW_A_RANK
    xg = w_l[:, o:o + RW_GATE_RANK]; o += RW_GATE_RANK
    q = w_l[:, o:o + NSA_W]; o += NSA_W
    kc, vc, ks, vs, kw, vw = (w_l[:, o + i * NSA_KV_W:o + (i + 1) * NSA_KV_W] for i in range(6))
    o += 6 * NSA_KV_W
    gates = w_l[:, o:o + 3 * NSA_HEADS]
    xv = jnp.zeros((d, LANES), w_l.dtype) if w_vres is None else _pad_cols(w_vres, LANES)
    return jnp.concatenate([ret, rkv, _pad_cols(xw, LANES), _pad_cols(xa, LANES), xg, xv,
                            q, ks, kw, kc, vc, vs, vw, _pad_cols(gates, LANES)], axis=1)


def _pack_mix(mix, vres_mix):
    o = 3 * RW_W
    xw = mix[o:o + RW_DECAY_RANK]; o += RW_DECAY_RANK
    xa = mix[o:o + RW_A_RANK]; o += RW_A_RANK
    xg = mix[o:o + RW_GATE_RANK]
    z = lambda n: jnp.zeros((n,), mix.dtype)
    xv = z(LANES) if vres_mix is None else jnp.concatenate([vres_mix, z(LANES - RW_V_RANK)])
    return jnp.concatenate([mix[:3 * RW_W], xw, z(LANES - RW_DECAY_RANK), xa, z(LANES - RW_A_RANK), xg, xv])[None, :]


def _head_major(a, g):
    t = a.shape[0]
    return a.reshape(t, g, NSA_DH).transpose(1, 0, 2)


def kernel(x, ln1_g, w_in, w_in_vres, rwkv_mix, rwkv_vres_mix, rwkv_w0, rwkv_w2, rwkv_a0, rwkv_a2, rwkv_v0, rwkv_v2, rwkv_g2, rwkv_k_k, rwkv_k_a, rwkv_r_k, rwkv_ln_g, rwkv_ln_b, nsa_q_norm, nsa_k_norm, nsa_cmp_pe, nsa_cmp_k_w1, nsa_cmp_k_w2, nsa_cmp_v_w1, nsa_cmp_v_w2, w_out, ln2_g, w_up, w_down):
    bsz, t, d = x.shape
    assert bsz == 1
    depth = w_in.shape[0]
    nc = t // CMP_STRIDE
    nb = t // SLC_BLOCK
    xs = x.reshape(t, d)

    pos = jnp.arange(t)
    cos_r, sin_r = _rope_tables(pos, RET_DH, 1)
    cos_n, sin_n = _rope_tables(pos, NSA_DH, 2)
    cos_c, sin_c = _rope_tables(jnp.arange(nc) * CMP_STRIDE + (CMP_BLOCK - 1), NSA_DH, 1)
    lane_head = np.arange(RW_W) // RW_N
    gsum = jnp.asarray(lane_head[:, None] == lane_head[None, :], BF16)
    lane_h2 = np.arange(LANES) // NSA_DH
    gmean = jnp.asarray((lane_h2[:, None] == lane_h2[None, :]) / float(NSA_DH), BF16)
    cstart = np.arange(nc) * CMP_STRIDE
    sstart = np.arange(nb) * SLC_BLOCK
    cis = jnp.asarray((cstart[:, None] <= sstart[None, :] + SLC_BLOCK - 1)
                      & (cstart[:, None] + CMP_BLOCK - 1 >= sstart[None, :]), BF16)

    v_first = jnp.zeros((t, RW_W), F32)
    for l in range(depth):
        vres = l > 0
        w_cat = _pack_w_in(w_in[l], w_in_vres[l - 1] if vres else None).astype(BF16)
        h = _rmsnorm(xs, ln1_g[l])
        proj = _matmul(h, w_cat, name="in_proj")

        o_ret = _retention(proj, cos_r, sin_r)

        row = lambda a: a.reshape(1, -1)
        mix = _pack_mix(rwkv_mix[l], rwkv_vres_mix[l - 1] if vres else None)
        v0 = row(rwkv_v0[l - 1]) if vres else jnp.zeros((1, RW_W), F32)
        v2 = _pad_rows(rwkv_v2[l - 1], LANES) if vres else jnp.zeros((LANES, RW_W), F32)
        r_, lw_, k_, v_, an_, b_, g_ = _rwkv_pre(
            proj, mix, row(rwkv_w0[l]), _pad_rows(rwkv_w2[l], LANES), row(rwkv_a0[l]), _pad_rows(rwkv_a2[l], LANES),
            v0, v2, rwkv_g2[l], row(rwkv_k_k[l]), row(rwkv_k_a[l]), gsum, v_first, vres)
        if not vres:
            v_first = v_
        o_rwkv = _wkv(r_, lw_, k_, v_, an_, b_, g_, row(rwkv_ln_g[l]), row(rwkv_ln_b[l]), row(rwkv_r_k[l]), gsum)

        qw = jnp.tile(nsa_q_norm[l] * (NSA_DH ** -0.5), 2)
        q_hm = _nsa_prep(proj, C_NSA_Q, NSA_W // LANES, jnp.broadcast_to(qw, (NSA_W // LANES, 1, LANES)),
                         cos_n, sin_n, gmean)
        kw_slabs = jnp.stack([jnp.tile(nsa_k_norm[l, 1], 2)] * 2 + [jnp.tile(nsa_k_norm[l, 2], 2)] * 2)[:, None, :]
        k_hm = _nsa_prep(proj, C_NSA_KS, 2 * NSA_KV_W // LANES, kw_slabs, cos_n, sin_n, gmean)
        raw = lambda off: _head_major(proj[:, off:off + NSA_KV_W], NSA_G).astype(BF16)
        hmat = jnp.stack([raw(C_NSA_KC), raw(C_NSA_VC)]).reshape(2, NSA_G, nc, CMP_STRIDE * NSA_DH)
        w1 = jnp.stack([nsa_cmp_k_w1[l], nsa_cmp_v_w1[l]]).astype(BF16)
        w2 = jnp.stack([nsa_cmp_k_w2[l], nsa_cmp_v_w2[l]]).astype(BF16)
        pe = jnp.broadcast_to(nsa_cmp_pe[l].reshape(2, 1, CMP_BLOCK * NSA_DH), (2, 8, CMP_BLOCK * NSA_DH)).astype(BF16)
        cmp_kv = _compress(hmat, w1, w2, pe, row(nsa_k_norm[l, 0]), cos_c, sin_c)
        gates = jnp.pad(proj[:, C_NSA_GATE:C_NSA_GATE + 3 * NSA_HEADS].reshape(t, NSA_G, 3 * NSA_HG).transpose(1, 0, 2),
                        ((0, 0), (0, 0), (0, LANES - 3 * NSA_HG)))
        o_nsa = _nsa_attn(q_hm, gates, cmp_kv, k_hm[:NSA_G], raw(C_NSA_VS), k_hm[NSA_G:], raw(C_NSA_VW), cis)

        mixed = jnp.concatenate([o_ret, o_rwkv, o_nsa], axis=1)
        xs = _matmul(mixed, w_out[l].astype(BF16), res=xs, name="out_proj")

        h2 = _rmsnorm(xs, ln2_g[l])
        up = _matmul(h2, w_up[l].astype(BF16), act="relu2", out_dtype=BF16, name="mlp_up")
        xs = _matmul(up, w_down[l].astype(BF16), res=xs, name="mlp_down")
    return xs.reshape(bsz, t, d)
```

```python
import functools

import numpy as np
import jax
import jax.numpy as jnp
from jax import lax
from jax.experimental import pallas as pl
from jax.experimental.pallas import tpu as pltpu

F32 = jnp.float32
BF16 = jnp.bfloat16

D_MODEL = 2048
RET_HEADS, RET_DH = 4, 128
RET_W = RET_HEADS * RET_DH
RW_HEADS, RW_N = 8, 64
RW_W = RW_HEADS * RW_N
RW_DECAY_RANK, RW_A_RANK, RW_V_RANK, RW_GATE_RANK = 96, 96, 64, 256
RW_GN_EPS = 64e-5
NSA_HEADS, NSA_G, NSA_DH = 16, 4, 64
NSA_HG = NSA_HEADS // NSA_G
NSA_W = NSA_HEADS * NSA_DH
NSA_KV_W = NSA_G * NSA_DH
CMP_BLOCK, CMP_STRIDE, CMP_HIDDEN = 32, 16, 256
SLC_BLOCK, SLC_TOPK, WINDOW, Q_BLOCK = 64, 16, 512, 128
ROPE_THETA = 10000.0
MLP_HIDDEN = 4 * D_MODEL
RMS_EPS = 1e-6
NEG_INF = -1e30
FORCED_SCORE = 1e9

LANES = 128
VMEM_LIMIT = 56 * 1024 * 1024

C_RET = 0
C_RW_RKV = 2048
C_RW_XW = 3584
C_RW_XA = 3712
C_RW_XG = 3840
C_RW_XV = 4096
C_NSA_Q = 4224
C_NSA_KS = 5248
C_NSA_KW = 5504
C_NSA_KC = 5760
C_NSA_VC = 6016
C_NSA_VS = 6272
C_NSA_VW = 6528
C_NSA_GATE = 6784
PROJ_W = 6912


def _cparams(sem):
    return pltpu.CompilerParams(dimension_semantics=sem, vmem_limit_bytes=VMEM_LIMIT)


def _dot(a, b):
    return jnp.dot(a, b, preferred_element_type=F32)


def _dot_t(a, b):
    return lax.dot_general(a, b, (((1,), (1,)), ((), ())), preferred_element_type=F32)


def _dot_tl(a, b):
    return lax.dot_general(a, b, (((0,), (0,)), ((), ())), preferred_element_type=F32)


def _split_dot(x, m_bf16, passes):
    hi = x.astype(BF16)
    acc = _dot(hi, m_bf16)
    rem = x - hi.astype(F32)
    for _ in range(passes - 1):
        piece = rem.astype(BF16)
        acc = acc + _dot(piece, m_bf16)
        rem = rem - piece.astype(F32)
    return acc


def _rmsnorm_kernel(x_ref, g_ref, o_ref):
    x = x_ref[...]
    ms = jnp.mean(x * x, axis=-1, keepdims=True)
    o_ref[...] = (x * lax.rsqrt(ms + RMS_EPS) * g_ref[...]).astype(o_ref.dtype)


def _rmsnorm(x, g, tm=512):
    t, d = x.shape
    return pl.pallas_call(
        _rmsnorm_kernel,
        out_shape=jax.ShapeDtypeStruct((t, d), BF16),
        grid=(t // tm,),
        in_specs=[pl.BlockSpec((tm, d), lambda i: (i, 0)), pl.BlockSpec((1, d), lambda i: (0, 0))],
        out_specs=pl.BlockSpec((tm, d), lambda i: (i, 0)),
        compiler_params=_cparams(("parallel",)),
        name="rmsnorm",
    )(x, g.reshape(1, d))


def _mm_kernel(*refs, nk, act, has_res):
    if has_res:
        a_ref, b_ref, r_ref, o_ref, acc_ref = refs
    else:
        a_ref, b_ref, o_ref, acc_ref = refs
        r_ref = None
    k = pl.program_id(2)

    @pl.when(k == 0)
    def _():
        acc_ref[...] = jnp.zeros_like(acc_ref)

    acc_ref[...] += _dot(a_ref[...], b_ref[...])

    @pl.when(k == nk - 1)
    def _():
        y = acc_ref[...]
        if act == "relu2":
            y = jnp.square(jnp.maximum(y, 0.0))
        if has_res:
            y = y + r_ref[...]
        o_ref[...] = y.astype(o_ref.dtype)


def _pick(n, cands):
    for c in cands:
        if n % c == 0:
            return c
    raise ValueError(f"no tile for {n}")


def _matmul(a, b, *, res=None, act=None, out_dtype=F32, name="matmul"):
    m, kd = a.shape
    _, n = b.shape
    tm = _pick(m, (1024, 512, 256, 128))
    tn = _pick(n, (1024, 768, 512, 256, 128))
    tk = _pick(kd, (2048, 1024, 512))
    nk = kd // tk
    in_specs = [pl.BlockSpec((tm, tk), lambda i, j, k: (i, k)), pl.BlockSpec((tk, tn), lambda i, j, k: (k, j))]
    args = [a, b]
    if res is not None:
        in_specs.append(pl.BlockSpec((tm, tn), lambda i, j, k: (i, j)))
        args.append(res)
    return pl.pallas_call(
        functools.partial(_mm_kernel, nk=nk, act=act, has_res=res is not None),
        out_shape=jax.ShapeDtypeStruct((m, n), out_dtype),
        grid=(m // tm, n // tn, nk),
        in_specs=in_specs,
        out_specs=pl.BlockSpec((tm, tn), lambda i, j, k: (i, j)),
        scratch_shapes=[pltpu.VMEM((tm, tn), F32)],
        compiler_params=_cparams(("parallel", "parallel", "arbitrary")),
        name=name,
    )(*args)


RET_CHUNK = 256


def _retention_kernel(q_ref, k_ref, v_ref, g_ref, cos_ref, sin_ref, lg_ref, o_ref, state_ref, *, chunk):
    c = pl.program_id(1)

    @pl.when(c == 0)
    def _():
        state_ref[...] = jnp.zeros_like(state_ref)

    lg_row = lg_ref[0]
    lg_dh = lg_row[:, :RET_DH]
    cos = cos_ref[...]
    sin = sin_ref[...]

    def rope(x):
        return x * cos + pltpu.roll(x, RET_DH // 2, axis=1) * sin

    q = rope(q_ref[...])
    k = rope(k_ref[...]) * (RET_DH ** -0.5)
    v = v_ref[...]
    n_col = lax.broadcasted_iota(jnp.int32, (chunk, RET_DH), 0).astype(F32)
    row = lax.broadcasted_iota(jnp.int32, (chunk, chunk), 0)
    col = lax.broadcasted_iota(jnp.int32, (chunk, chunk), 1)
    lag = (row - col).astype(F32)
    decay_mat = jnp.where(lag >= 0, jnp.exp(jnp.maximum(lag, 0.0) * lg_row), 0.0)
    q_decay = jnp.exp((n_col + 1.0) * lg_dh)
    k_decay = jnp.exp((chunk - 1.0 - n_col) * lg_dh)
    chunk_decay = jnp.exp(chunk * lg_dh)

    qb = q.astype(BF16)
    kb = k.astype(BF16)
    vb = v.astype(BF16)
    state = state_ref[...]
    inner = _dot_t(qb, kb) * decay_mat
    out = _dot(inner.astype(BF16), vb) + _dot(qb, state.astype(BF16)) * q_decay
    state_ref[...] = chunk_decay * state + _dot_tl((k * k_decay).astype(BF16), vb)

    out = out * lax.rsqrt(jnp.mean(out * out, axis=-1, keepdims=True) + RMS_EPS)
    gate = g_ref[...]
    o_ref[...] = (gate * jax.nn.sigmoid(gate) * out).astype(o_ref.dtype)


def _retention(proj, cos, sin):
    t = proj.shape[0]
    chunk = RET_CHUNK
    hb = C_RET // RET_DH
    log_gamma = np.log(1.0 - 2.0 ** (-5.0 - np.arange(RET_HEADS, dtype=np.float64)))
    lg = jnp.asarray(np.broadcast_to(log_gamma[:, None, None], (RET_HEADS, 1, chunk)), F32)

    def col(off):
        return pl.BlockSpec((chunk, RET_DH), lambda h, c, off=off: (c, hb + off * RET_HEADS + h))

    tab = pl.BlockSpec((chunk, RET_DH), lambda h, c: (c, 0))
    return pl.pallas_call(
        functools.partial(_retention_kernel, chunk=chunk),
        out_shape=jax.ShapeDtypeStruct((t, RET_W), BF16),
        grid=(RET_HEADS, t // chunk),
        in_specs=[col(0), col(1), col(2), col(3), tab, tab,
                  pl.BlockSpec((1, 1, chunk), lambda h, c: (h, 0, 0))],
        out_specs=pl.BlockSpec((chunk, RET_DH), lambda h, c: (c, h)),
        scratch_shapes=[pltpu.VMEM((RET_DH, RET_DH), F32)],
        compiler_params=_cparams(("parallel", "arbitrary")),
        name="retention",
    )(proj, proj, proj, proj, cos, sin, lg)


def _dot_f32(a, b):
    a_hi = a.astype(BF16)
    b_hi = b.astype(BF16)
    a_lo = (a - a_hi.astype(F32)).astype(BF16)
    b_lo = (b - b_hi.astype(F32)).astype(BF16)
    return _dot(a_hi, b_hi) + (_dot(a_hi, b_lo) + _dot(a_lo, b_hi))


RW_PRE_TM = 256
_MIX_OFF = (0, 512, 1024, 1536, 1664, 1792, 2048, 2176)


def _rwkv_pre_kernel(r_ref, k_ref, v_ref, xw_ref, xa_ref, xg_ref, xv_ref,
                     rp_ref, kp_ref, vp_ref, xwp_ref, xap_ref, xgp_ref, xvp_ref,
                     mix_ref, w0_ref, w2_ref, a0_ref, a2_ref, v0_ref, v2_ref, g2_ref, kk_ref, ka_ref,
                     gsum_ref, vfirst_ref,
                     r_out, lw_out, k_out, v_out, an_out, b_out, g_out, *, use_vres):
    i = pl.program_id(0)

    def mixed(cur_ref, prev_ref, seg):
        cur = cur_ref[...]
        mix = mix_ref[:, _MIX_OFF[seg]:_MIX_OFF[seg + 1]]
        prev_row = jnp.where(i > 0, prev_ref[7:8, :], 0.0)
        rid = lax.broadcasted_iota(jnp.int32, cur.shape, 0)
        shifted = jnp.where(rid == 0, prev_row, pltpu.roll(cur, 1, axis=0))
        return cur + mix * (shifted - cur)

    r = mixed(r_ref, rp_ref, 0)
    k = mixed(k_ref, kp_ref, 1)
    v = mixed(v_ref, vp_ref, 2)
    xw = mixed(xw_ref, xwp_ref, 3)
    xa = mixed(xa_ref, xap_ref, 4)
    xg = mixed(xg_ref, xgp_ref, 5)

    z = -(w0_ref[...] + _dot_f32(jnp.tanh(xw), w2_ref[...]))
    softplus = jnp.maximum(z, 0.0) + jnp.log(1.0 + jnp.exp(-jnp.abs(z)))
    w = -softplus - 0.5
    lw_out[...] = -jnp.exp(w)
    a = jax.nn.sigmoid(a0_ref[...] + _dot_f32(xa, a2_ref[...]))
    g_out[...] = _dot_f32(jax.nn.sigmoid(xg), g2_ref[...])
    if use_vres:
        xv = mixed(xv_ref, xvp_ref, 6)
        v = v + (vfirst_ref[...] - v) * jax.nn.sigmoid(v0_ref[...] + _dot_f32(xv, v2_ref[...]))
    kk = k * kk_ref[...]
    ss = _split_dot(kk * kk, gsum_ref[...], 3)
    kk = kk / jnp.maximum(jnp.sqrt(ss), 1e-12)
    r_out[...] = r
    k_out[...] = k * (1.0 + (a - 1.0) * ka_ref[...])
    v_out[...] = v
    an_out[...] = -kk
    b_out[...] = kk * a


def _rwkv_pre(proj, mix, w0, w2, a0, a2, v0, v2, g2, k_k, k_a, gsum, v_first, use_vres):
    t = proj.shape[0]
    tm = RW_PRE_TM

    def cur(width, off):
        return pl.BlockSpec((tm, width), lambda i: (i, off // width))

    def prev(width, off):
        return pl.BlockSpec((8, width), lambda i: (jnp.maximum(i * (tm // 8) - 1, 0), off // width))

    def full(a):
        return pl.BlockSpec(a.shape, lambda i: (0,) * a.ndim)

    segs = [(RW_W, C_RW_RKV), (RW_W, C_RW_RKV + RW_W), (RW_W, C_RW_RKV + 2 * RW_W),
            (128, C_RW_XW), (128, C_RW_XA), (256, C_RW_XG), (128, C_RW_XV)]
    params = [mix, w0, w2, a0, a2, v0, v2, g2, k_k, k_a, gsum]
    row = pl.BlockSpec((tm, RW_W), lambda i: (i, 0))
    outs = pl.pallas_call(
        functools.partial(_rwkv_pre_kernel, use_vres=use_vres),
        out_shape=[jax.ShapeDtypeStruct((t, RW_W), F32)] * 7,
        grid=(t // tm,),
        in_specs=[cur(w, o) for w, o in segs] + [prev(w, o) for w, o in segs] + [full(p) for p in params] + [row],
        out_specs=[row] * 7,
        compiler_params=_cparams(("parallel",)),
        name="rwkv_pre",
    )(*([proj] * 14), *params, v_first)
    return outs


RW_CHUNK = 128


def _wkv_kernel(r_ref, lw_ref, k_ref, v_ref, an_ref, b_ref, g_ref, lng_ref, lnb_ref, rk_ref,
                gsum_ref, tril_ref, o_ref, s_ref):
    c = pl.program_id(0)
    C = RW_CHUNK
    N = RW_N

    @pl.when(c == 0)
    def _():
        s_ref[...] = jnp.zeros_like(s_ref)

    r = r_ref[...]
    lw = lw_ref[...]
    k = k_ref[...]
    v = v_ref[...]
    lw_hi = lw.astype(BF16)
    rem = lw - lw_hi.astype(F32)
    lw_mid = rem.astype(BF16)
    lw_lo = (rem - lw_mid.astype(F32)).astype(BF16)
    tril = tril_ref[...]
    lg = _dot(tril, lw_hi) + (_dot(tril, lw_mid) + _dot(tril, lw_lo))
    gam = jnp.exp(lg)
    ginv = jnp.exp(-lg)
    at = (an_ref[...] * jnp.exp(lg - lw)).astype(BF16)
    bt = (b_ref[...] * ginv).astype(BF16)
    kt = (k * ginv).astype(BF16)
    rt_f = r * gam
    rt = rt_f.astype(BF16)
    vb = v.astype(BF16)
    g_last = gam[C - 1:C, :]

    rowi = lax.broadcasted_iota(jnp.int32, (C, C), 0)
    coli = lax.broadcasted_iota(jnp.int32, (C, C), 1)
    strict = rowi > coli
    incl = rowi >= coli

    heads = range(RW_HEADS)
    hs = [slice(h * N, (h + 1) * N) for h in heads]
    bth = [bt[:, s] for s in hs]
    kth = [kt[:, s] for s in hs]
    vh = [vb[:, s] for s in hs]
    big = [_dot_t(jnp.concatenate([at[:, s], rt[:, s]], axis=0), jnp.concatenate([bth[h], kth[h]], axis=0))
           for h, s in zip(heads, hs)]
    a_ab = [jnp.where(strict, big[h][:C, :C], 0.0) for h in heads]
    a_ak = [jnp.where(strict, big[h][:C, C:], 0.0).astype(BF16) for h in heads]
    a_rb = [jnp.where(incl, big[h][C:, :C], 0.0).astype(BF16) for h in heads]
    a_rk = [jnp.where(incl, big[h][C:, C:], 0.0).astype(BF16) for h in heads]
    akv = [_dot(a_ak[h], vh[h]) for h in heads]
    def same_block(size):
        shift = size.bit_length() - 1
        return (rowi >> shift) == (coli >> shift)

    tinv = [jnp.where(same_block(2), a_ab[h], 0.0) + jnp.where(rowi == coli, 1.0, 0.0) for h in heads]
    size = 2
    while size < C:
        lower_left = same_block(2 * size) & jnp.logical_not(same_block(size))
        off = [jnp.where(lower_left, a_ab[h], 0.0).astype(BF16) for h in heads]
        tb = [tinv[h].astype(BF16) for h in heads]
        half = [_dot(tb[h], off[h]).astype(BF16) for h in heads]
        tinv = [tinv[h] + _dot(half[h], tb[h]) for h in heads]
        size *= 2
    xb = [_dot(tinv[h].astype(BF16), jnp.concatenate([at[:, hs[h]], akv[h].astype(BF16)], axis=1)).astype(BF16)
          for h in heads]
    yx = [_dot(a_rb[h], xb[h]) for h in heads]
    ykv = [_dot(a_rk[h], vh[h]) for h in heads]
    xtb = [_dot_tl(xb[h], bth[h]) for h in heads]
    vtk = [_dot_tl(vh[h], kth[h]) for h in heads]
    ys = []
    for h in heads:
        gl = g_last[:, hs[h]]
        y1 = (rt_f[:, hs[h]] + yx[h][:, :N]).astype(BF16)
        s0 = s_ref[h]
        s0b = s0.astype(BF16)
        ys.append(_dot_t(y1, s0b) + (yx[h][:, N:] + ykv[h]))
        s_ref[h] = (s0 + _dot(s0b, xtb[h][:N].astype(BF16)) + (xtb[h][N:] + vtk[h])) * gl
    y = jnp.concatenate(ys, axis=1)

    gsum = gsum_ref[...]
    inv_n = 1.0 / N
    mu = _split_dot(y, gsum, 3) * inv_n
    yc = y - mu
    var = _split_dot(yc * yc, gsum, 3) * inv_n
    yn = yc * lax.rsqrt(var + RW_GN_EPS) * lng_ref[...] + lnb_ref[...]
    bonus = _split_dot(r * k * rk_ref[...], gsum, 3)
    o_ref[...] = ((yn + bonus * v) * g_ref[...]).astype(o_ref.dtype)


def _wkv(r, lw, k, v, an, b, g, ln_g, ln_b, r_k, gsum):
    t = r.shape[0]
    C = RW_CHUNK
    tril = jnp.asarray(np.tril(np.ones((C, C), np.float32)), BF16)
    row = pl.BlockSpec((C, RW_W), lambda c: (c, 0))
    vec = pl.BlockSpec((1, RW_W), lambda c: (0, 0))
    return pl.pallas_call(
        _wkv_kernel,
        out_shape=jax.ShapeDtypeStruct((t, RW_W), BF16),
        grid=(t // C,),
        in_specs=[row] * 7 + [vec] * 3 + [pl.BlockSpec((RW_W, RW_W), lambda c: (0, 0)),
                                          pl.BlockSpec((C, C), lambda c: (0, 0))],
        out_specs=row,
        scratch_shapes=[pltpu.VMEM((RW_HEADS, RW_N, RW_N), F32)],
        compiler_params=_cparams(("arbitrary",)),
        name="wkv7",
    )(r, lw, k, v, an, b, g, ln_g, ln_b, r_k, gsum, tril)


NSA_PREP_TM = 512


def _nsa_prep_kernel(x_ref, cos_ref, sin_ref, w_ref, gmean_ref, o_ref):
    x = x_ref[...]
    ms = _split_dot(x * x, gmean_ref[...], 3)
    y = x * lax.rsqrt(ms + RMS_EPS) * w_ref[0]
    lane = lax.broadcasted_iota(jnp.int32, y.shape, 1)
    half = NSA_DH // 2
    rot = jnp.where((lane % NSA_DH) < half, pltpu.roll(y, LANES - half, axis=1), pltpu.roll(y, half, axis=1))
    res = (y * cos_ref[...] + rot * sin_ref[...]).astype(o_ref.dtype)
    o_ref[0] = res[:, :NSA_DH]
    o_ref[1] = res[:, NSA_DH:]


def _nsa_prep(proj, col_off, n_slabs, w_slabs, cos, sin, gmean):
    t = proj.shape[0]
    tm = NSA_PREP_TM
    base = col_off // LANES
    return pl.pallas_call(
        _nsa_prep_kernel,
        out_shape=jax.ShapeDtypeStruct((2 * n_slabs, t, NSA_DH), BF16),
        grid=(t // tm, n_slabs),
        in_specs=[pl.BlockSpec((tm, LANES), lambda i, s: (i, base + s)),
                  pl.BlockSpec((tm, LANES), lambda i, s: (i, 0)),
                  pl.BlockSpec((tm, LANES), lambda i, s: (i, 0)),
                  pl.BlockSpec((1, 1, LANES), lambda i, s: (s, 0, 0)),
                  pl.BlockSpec((LANES, LANES), lambda i, s: (0, 0))],
        out_specs=pl.BlockSpec((2, tm, NSA_DH), lambda i, s: (s, i, 0)),
        compiler_params=_cparams(("parallel", "parallel")),
        name="nsa_prep",
    )(proj, cos, sin, w_slabs, gmean)


def _compress_kernel(h_ref, w1_ref, w2_ref, pe_ref, nw_ref, cos_ref, sin_ref, o_ref):
    kind = pl.program_id(0)
    nc = h_ref.shape[2]
    half_in = CMP_STRIDE * NSA_DH
    hm = h_ref[0, 0]
    w1 = w1_ref[0]
    first = _dot(hm, w1[:half_in])
    second = _dot(hm, w1[half_in:])
    const = _dot(pe_ref[0], w1)[0:1, :]
    pre = first + pltpu.roll(second, nc - 1, axis=0) + const
    hid = 0.5 * pre * (1.0 + jnp.tanh(0.7978845608028654 * (pre + 0.044715 * pre * pre * pre)))
    out = _dot(hid.astype(BF16), w2_ref[0])
    y = out * lax.rsqrt(jnp.mean(out * out, axis=-1, keepdims=True) + RMS_EPS) * nw_ref[...]
    half = NSA_DH // 2
    rot = jnp.concatenate([y[:, half:], y[:, :half]], axis=1)
    roped = y * cos_ref[...] + rot * sin_ref[...]
    o_ref[0, 0] = jnp.where(kind == 0, roped, out).astype(o_ref.dtype)


def _compress(hmat, w1, w2, pe, nw, cos_c, sin_c):
    _, g, nc, width = hmat.shape
    return pl.pallas_call(
        _compress_kernel,
        out_shape=jax.ShapeDtypeStruct((2, g, nc, NSA_DH), BF16),
        grid=(2, g),
        in_specs=[pl.BlockSpec((1, 1, nc, width), lambda a, b: (a, b, 0, 0)),
                  pl.BlockSpec((1, 2 * width, CMP_HIDDEN), lambda a, b: (a, 0, 0)),
                  pl.BlockSpec((1, CMP_HIDDEN, NSA_DH), lambda a, b: (a, 0, 0)),
                  pl.BlockSpec((1, 8, 2 * width), lambda a, b: (a, 0, 0)),
                  pl.BlockSpec((1, NSA_DH), lambda a, b: (0, 0)),
                  pl.BlockSpec((nc, NSA_DH), lambda a, b: (0, 0)),
                  pl.BlockSpec((nc, NSA_DH), lambda a, b: (0, 0))],
        out_specs=pl.BlockSpec((1, 1, nc, NSA_DH), lambda a, b: (a, b, 0, 0)),
        compiler_params=_cparams(("parallel", "parallel")),
        name="nsa_compress",
    )(hmat, w1, w2, pe, nw, cos_c, sin_c)


SLC_KT = 512
WIN_KEYS = WINDOW + Q_BLOCK


INT32_MIN = -2 ** 31


def _nsa_attn_kernel(q_ref, gate_ref, kc_ref, vc_ref, ks_ref, vs_ref, kw_ref, vw_ref, cis_ref, expand_ref, ltri_ref,
                     o_ref):
    qi = pl.program_id(1)
    nc = kc_ref.shape[2]
    nb = cis_ref.shape[1]
    rows = NSA_HG * Q_BLOCK
    start = qi * Q_BLOCK

    q4 = q_ref[...].reshape(rows, NSA_DH)
    tok = start + lax.broadcasted_iota(jnp.int32, (Q_BLOCK, 1), 0)
    tok4 = start + (lax.broadcasted_iota(jnp.int32, (rows, 1), 0) & (Q_BLOCK - 1))

    def add_rows(s, bias):
        n = s.shape[-1]
        return (s.reshape(NSA_HG, Q_BLOCK, n) + bias[None]).reshape(rows, n)

    kc = kc_ref[0, 0]
    vc = vc_ref[0, 0]
    cend = lax.broadcasted_iota(jnp.int32, (1, nc), 1) * CMP_STRIDE + (CMP_BLOCK - 1)
    sm = add_rows(_dot_t(q4, kc), jnp.where(cend <= tok, 0.0, NEG_INF))
    e = jnp.exp(sm - jnp.max(sm, axis=-1, keepdims=True))
    rden = jnp.where(tok4 >= CMP_BLOCK - 1, 1.0 / jnp.sum(e, axis=-1, keepdims=True), 0.0)
    p = e * rden
    o_c = _dot(p.astype(BF16), vc)

    psum = p[0:Q_BLOCK]
    for hh in range(1, NSA_HG):
        psum = psum + p[hh * Q_BLOCK:(hh + 1) * Q_BLOCK]
    score = _split_dot(psum, cis_ref[...], 3)
    blk = lax.broadcasted_iota(jnp.int32, (1, nb), 1)
    cur = tok // SLC_BLOCK
    forced = (blk == 0) | (blk == cur) | (blk == cur - 1)
    score = jnp.where(forced, FORCED_SCORE, score)
    score = jnp.where(blk <= cur, score, -jnp.inf)

    bits = lax.bitcast_convert_type(score.T, jnp.int32)
    key = bits ^ ((bits >> 31) & 0x7FFFFFFF)
    n_sel = float(min(SLC_TOPK, nb))

    def count_ge(c):
        return jnp.sum(jnp.where(key >= c, 1.0, 0.0), axis=0, keepdims=True)

    zero_row = jnp.zeros((1, Q_BLOCK), jnp.int32)
    thr = jnp.where(count_ge(zero_row) >= n_sel, zero_row, INT32_MIN)
    for bit in range(30, -1, -1):
        cand = thr + (1 << bit)
        thr = jnp.where(count_ge(cand) >= n_sel, cand, thr)
    above = key > thr
    tied = key == thr
    need = n_sel - jnp.sum(jnp.where(above, 1.0, 0.0), axis=0, keepdims=True)
    tied_before = _dot(ltri_ref[...], jnp.where(tied, 1.0, 0.0).astype(BF16))
    sel_t = above | (tied & (tied_before < need))
    sel = jnp.where(sel_t, 1.0, 0.0).T.astype(BF16)

    n_tiles = (start + Q_BLOCK + SLC_KT - 1) // SLC_KT

    def sel_body(j, carry):
        m_i, acc = carry
        k0 = pl.multiple_of(j * SLC_KT, SLC_KT)
        kt = ks_ref[0, pl.ds(k0, SLC_KT), :]
        vt = vs_ref[0, pl.ds(k0, SLC_KT), :]
        member = _dot(sel, expand_ref[:, pl.ds(k0, SLC_KT)])
        kpos = k0 + lax.broadcasted_iota(jnp.int32, (1, SLC_KT), 1)
        bias = jnp.where((member > 0.5) & (kpos <= tok), 0.0, NEG_INF)
        smj = add_rows(_dot_t(q4, kt), bias)
        m_new = jnp.maximum(m_i, jnp.max(smj, axis=-1, keepdims=True))
        pj = jnp.exp(smj - m_new).astype(BF16)
        return m_new, jnp.exp(m_i - m_new) * acc + _dot(pj, vt)

    m0 = jnp.full((rows, 1), NEG_INF, F32)
    a0 = jnp.zeros((rows, LANES), F32)
    _, acc_s = lax.fori_loop(0, n_tiles, sel_body, (m0, a0))
    o_s = acc_s[:, :NSA_DH] / acc_s[:, NSA_DH:NSA_DH + 1]

    kbase = pl.multiple_of(jnp.maximum(start - WINDOW, 0), Q_BLOCK)
    kw = kw_ref[0, pl.ds(kbase, WIN_KEYS), :]
    vw = vw_ref[0, pl.ds(kbase, WIN_KEYS), :]
    lag = tok - (kbase + lax.broadcasted_iota(jnp.int32, (1, WIN_KEYS), 1))
    smw = add_rows(_dot_t(q4, kw), jnp.where((lag >= 0) & (lag < WINDOW), 0.0, NEG_INF))
    ew = jnp.exp(smw - jnp.max(smw, axis=-1, keepdims=True)).astype(BF16)
    acc_w = _dot(ew, vw)
    o_w = acc_w[:, :NSA_DH] / acc_w[:, NSA_DH:NSA_DH + 1]

    gt = jax.nn.sigmoid(gate_ref[0])

    def gate_col(br):
        return jnp.concatenate([gt[:, 3 * hh + br:3 * hh + br + 1] for hh in range(NSA_HG)], axis=0)

    out = gate_col(0) * o_c + gate_col(1) * o_s + gate_col(2) * o_w
    o_ref[...] = jnp.concatenate([out[hh * Q_BLOCK:(hh + 1) * Q_BLOCK] for hh in range(NSA_HG)],
                                 axis=1).astype(o_ref.dtype)


def _nsa_attn(q_hm, gates, cmp_kv, ks, vs1, kw, vw1, cis, expand, ltri):
    t = q_hm.shape[1]
    nq = t // Q_BLOCK
    nc = cmp_kv.shape[2]
    nb = cis.shape[1]
    kspec = pl.BlockSpec((1, t, NSA_DH), lambda g, i: (g, 0, 0))
    vspec = pl.BlockSpec((1, t, LANES), lambda g, i: (g, 0, 0))
    return pl.pallas_call(
        _nsa_attn_kernel,
        out_shape=jax.ShapeDtypeStruct((t, NSA_W), BF16),
        grid=(NSA_G, nq),
        in_specs=[pl.BlockSpec((NSA_HG, Q_BLOCK, NSA_DH), lambda g, i: (g, i, 0)),
                  pl.BlockSpec((1, Q_BLOCK, LANES), lambda g, i: (g, i, 0)),
                  pl.BlockSpec((1, 1, nc, NSA_DH), lambda g, i: (0, g, 0, 0)),
                  pl.BlockSpec((1, 1, nc, NSA_DH), lambda g, i: (1, g, 0, 0)),
                  kspec, vspec, kspec, vspec,
                  pl.BlockSpec((nc, nb), lambda g, i: (0, 0)),
                  pl.BlockSpec((nb, t), lambda g, i: (0, 0)),
                  pl.BlockSpec((nb, nb), lambda g, i: (0, 0))],
        out_specs=pl.BlockSpec((Q_BLOCK, NSA_HG * NSA_DH), lambda g, i: (i, g)),
        compiler_params=_cparams(("parallel", "arbitrary")),
        name="nsa_attention",
    )(q_hm, gates, cmp_kv, cmp_kv, ks, vs1, kw, vw1, cis, expand, ltri)


def _rope_tables(pos, dh, reps):
    half = dh // 2
    inv_freq = ROPE_THETA ** (-jnp.arange(half, dtype=F32) / half)
    ang = pos.astype(F32)[:, None] * inv_freq[None, :]
    cos = jnp.cos(ang)
    sin = jnp.sin(ang)
    cos_t = jnp.tile(jnp.concatenate([cos, cos], axis=1), (1, reps))
    sin_t = jnp.tile(jnp.concatenate([-sin, sin], axis=1), (1, reps))
    return cos_t, sin_t


def _pad_cols(a, width):
    return jnp.pad(a, ((0, 0), (0, width - a.shape[1])))


def _pad_rows(a, height):
    return jnp.pad(a, ((0, height - a.shape[0]), (0, 0)))


def _pack_w_in(w_l, w_vres):
    d = w_l.shape[0]
    o = 0
    ret = w_l[:, o:o + 4 * RET_W]; o += 4 * RET_W
    rkv = w_l[:, o:o + 3 * RW_W]; o += 3 * RW_W
    xw = w_l[:, o:o + RW_DECAY_RANK]; o += RW_DECAY_RANK
    xa = w_l[:, o:o + RW_A_RANK]; o += RW_A_RANK
    xg = w_l[:, o:o + RW_GATE_RANK]; o += RW_GATE_RANK
    q = w_l[:, o:o + NSA_W]; o += NSA_W
    kc, vc, ks, vs, kw, vw = (w_l[:, o + i * NSA_KV_W:o + (i + 1) * NSA_KV_W] for i in range(6))
    o += 6 * NSA_KV_W
    gates = w_l[:, o:o + 3 * NSA_HEADS]
    xv = jnp.zeros((d, LANES), w_l.dtype) if w_vres is None else _pad_cols(w_vres, LANES)
    return jnp.concatenate([ret, rkv, _pad_cols(xw, LANES), _pad_cols(xa, LANES), xg, xv,
                            q, ks, kw, kc, vc, vs, vw, _pad_cols(gates, LANES)], axis=1)


def _pack_mix(mix, vres_mix):
    o = 3 * RW_W
    xw = mix[o:o + RW_DECAY_RANK]; o += RW_DECAY_RANK
    xa = mix[o:o + RW_A_RANK]; o += RW_A_RANK
    xg = mix[o:o + RW_GATE_RANK]
    z = lambda n: jnp.zeros((n,), mix.dtype)
    xv = z(LANES) if vres_mix is None else jnp.concatenate([vres_mix, z(LANES - RW_V_RANK)])
    return jnp.concatenate([mix[:3 * RW_W], xw, z(LANES - RW_DECAY_RANK), xa, z(LANES - RW_A_RANK), xg, xv])[None, :]


def _head_major(a, g):
    t = a.shape[0]
    return a.reshape(t, g, NSA_DH).transpose(1, 0, 2)


def kernel(x, ln1_g, w_in, w_in_vres, rwkv_mix, rwkv_vres_mix, rwkv_w0, rwkv_w2, rwkv_a0, rwkv_a2, rwkv_v0, rwkv_v2, rwkv_g2, rwkv_k_k, rwkv_k_a, rwkv_r_k, rwkv_ln_g, rwkv_ln_b, nsa_q_norm, nsa_k_norm, nsa_cmp_pe, nsa_cmp_k_w1, nsa_cmp_k_w2, nsa_cmp_v_w1, nsa_cmp_v_w2, w_out, ln2_g, w_up, w_down):
    bsz, t, d = x.shape
    assert bsz == 1
    depth = w_in.shape[0]
    nc = t // CMP_STRIDE
    nb = t // SLC_BLOCK
    xs = x.reshape(t, d)

    pos = jnp.arange(t)
    cos_r, sin_r = _rope_tables(pos, RET_DH, 1)
    cos_n, sin_n = _rope_tables(pos, NSA_DH, 2)
    cos_c, sin_c = _rope_tables(jnp.arange(nc) * CMP_STRIDE + (CMP_BLOCK - 1), NSA_DH, 1)
    lane_head = np.arange(RW_W) // RW_N
    gsum = jnp.asarray(lane_head[:, None] == lane_head[None, :], BF16)
    lane_h2 = np.arange(LANES) // NSA_DH
    gmean = jnp.asarray((lane_h2[:, None] == lane_h2[None, :]) / float(NSA_DH), BF16)
    cstart = np.arange(nc) * CMP_STRIDE
    sstart = np.arange(nb) * SLC_BLOCK
    cis = jnp.asarray((cstart[:, None] <= sstart[None, :] + SLC_BLOCK - 1)
                      & (cstart[:, None] + CMP_BLOCK - 1 >= sstart[None, :]), BF16)

    expand = jnp.asarray((np.arange(t)[None, :] // SLC_BLOCK) == np.arange(nb)[:, None], BF16)
    ltri = jnp.asarray(np.tril(np.ones((nb, nb), np.float32), -1), BF16)

    def with_ones(v_hm):
        pad = jnp.zeros(v_hm.shape[:-1] + (LANES - NSA_DH,), v_hm.dtype).at[..., 0].set(1)
        return jnp.concatenate([v_hm, pad], axis=-1)

    v_first = jnp.zeros((t, RW_W), F32)
    for l in range(depth):
        vres = l > 0
        w_cat = _pack_w_in(w_in[l], w_in_vres[l - 1] if vres else None).astype(BF16)
        h = _rmsnorm(xs, ln1_g[l])
        proj = _matmul(h, w_cat, name="in_proj")

        o_ret = _retention(proj, cos_r, sin_r)

        row = lambda a: a.reshape(1, -1)
        mix = _pack_mix(rwkv_mix[l], rwkv_vres_mix[l - 1] if vres else None)
        v0 = row(rwkv_v0[l - 1]) if vres else jnp.zeros((1, RW_W), F32)
        v2 = _pad_rows(rwkv_v2[l - 1], LANES) if vres else jnp.zeros((LANES, RW_W), F32)
        r_, lw_, k_, v_, an_, b_, g_ = _rwkv_pre(
            proj, mix, row(rwkv_w0[l]), _pad_rows(rwkv_w2[l], LANES), row(rwkv_a0[l]), _pad_rows(rwkv_a2[l], LANES),
            v0, v2, rwkv_g2[l], row(rwkv_k_k[l]), row(rwkv_k_a[l]), gsum, v_first, vres)
        if not vres:
            v_first = v_
        o_rwkv = _wkv(r_, lw_, k_, v_, an_, b_, g_, row(rwkv_ln_g[l]), row(rwkv_ln_b[l]), row(rwkv_r_k[l]), gsum)

        qw = jnp.tile(nsa_q_norm[l] * (NSA_DH ** -0.5), 2)
        q_hm = _nsa_prep(proj, C_NSA_Q, NSA_W // LANES, jnp.broadcast_to(qw, (NSA_W // LANES, 1, LANES)),
                         cos_n, sin_n, gmean)
        kw_slabs = jnp.stack([jnp.tile(nsa_k_norm[l, 1], 2)] * 2 + [jnp.tile(nsa_k_norm[l, 2], 2)] * 2)[:, None, :]
        k_hm = _nsa_prep(proj, C_NSA_KS, 2 * NSA_KV_W // LANES, kw_slabs, cos_n, sin_n, gmean)
        raw = lambda off: _head_major(proj[:, off:off + NSA_KV_W], NSA_G).astype(BF16)
        hmat = jnp.stack([raw(C_NSA_KC), raw(C_NSA_VC)]).reshape(2, NSA_G, nc, CMP_STRIDE * NSA_DH)
        w1 = jnp.stack([nsa_cmp_k_w1[l], nsa_cmp_v_w1[l]]).astype(BF16)
        w2 = jnp.stack([nsa_cmp_k_w2[l], nsa_cmp_v_w2[l]]).astype(BF16)
        pe = jnp.broadcast_to(nsa_cmp_pe[l].reshape(2, 1, CMP_BLOCK * NSA_DH), (2, 8, CMP_BLOCK * NSA_DH)).astype(BF16)
        cmp_kv = _compress(hmat, w1, w2, pe, row(nsa_k_norm[l, 0]), cos_c, sin_c)
        gates = jnp.pad(proj[:, C_NSA_GATE:C_NSA_GATE + 3 * NSA_HEADS].reshape(t, NSA_G, 3 * NSA_HG).transpose(1, 0, 2),
                        ((0, 0), (0, 0), (0, LANES - 3 * NSA_HG)))
        o_nsa = _nsa_attn(q_hm, gates, cmp_kv, k_hm[:NSA_G], with_ones(raw(C_NSA_VS)), k_hm[NSA_G:],
                          with_ones(raw(C_NSA_VW)), cis, expand, ltri)

        mixed = jnp.concatenate([o_ret, o_rwkv, o_nsa], axis=1)
        xs = _matmul(mixed, w_out[l].astype(BF16), res=xs, name="out_proj")

        h2 = _rmsnorm(xs, ln2_g[l])
        up = _matmul(h2, w_up[l].astype(BF16), act="relu2", out_dtype=BF16, name="mlp_up")
        xs = _matmul(up, w_down[l].astype(BF16), res=xs, name="mlp_down")
    return xs.reshape(bsz, t, d)
```

```python
import functools

import numpy as np
import jax
import jax.numpy as jnp
from jax import lax
from jax.experimental import pallas as pl
from jax.experimental.pallas import tpu as pltpu

F32 = jnp.float32
BF16 = jnp.bfloat16

D_MODEL = 2048
RET_HEADS, RET_DH = 4, 128
RET_W = RET_HEADS * RET_DH
RW_HEADS, RW_N = 8, 64
RW_W = RW_HEADS * RW_N
RW_DECAY_RANK, RW_A_RANK, RW_V_RANK, RW_GATE_RANK = 96, 96, 64, 256
RW_GN_EPS = 64e-5
NSA_HEADS, NSA_G, NSA_DH = 16, 4, 64
NSA_HG = NSA_HEADS // NSA_G
NSA_W = NSA_HEADS * NSA_DH
NSA_KV_W = NSA_G * NSA_DH
CMP_BLOCK, CMP_STRIDE, CMP_HIDDEN = 32, 16, 256
SLC_BLOCK, SLC_TOPK, WINDOW, Q_BLOCK = 64, 16, 512, 128
ROPE_THETA = 10000.0
MLP_HIDDEN = 4 * D_MODEL
RMS_EPS = 1e-6
NEG_INF = -1e30
FORCED_SCORE = 1e9

LANES = 128
VMEM_LIMIT = 56 * 1024 * 1024

C_RET = 0
C_RW_RKV = 2048
C_RW_XW = 3584
C_RW_XA = 3712
C_RW_XG = 3840
C_RW_XV = 4096
C_NSA_Q = 4224
C_NSA_KS = 5248
C_NSA_KW = 5504
C_NSA_KC = 5760
C_NSA_VC = 6016
C_NSA_VS = 6272
C_NSA_VW = 6528
C_NSA_GATE = 6784
PROJ_W = 6912


def _cparams(sem):
    return pltpu.CompilerParams(dimension_semantics=sem, vmem_limit_bytes=VMEM_LIMIT)


def _dot(a, b):
    return jnp.dot(a, b, preferred_element_type=F32)


def _dot_t(a, b):
    return lax.dot_general(a, b, (((1,), (1,)), ((), ())), preferred_element_type=F32)


def _dot_tl(a, b):
    return lax.dot_general(a, b, (((0,), (0,)), ((), ())), preferred_element_type=F32)


def _split_dot(x, m_bf16, passes):
    hi = x.astype(BF16)
    acc = _dot(hi, m_bf16)
    rem = x - hi.astype(F32)
    for _ in range(passes - 1):
        piece = rem.astype(BF16)
        acc = acc + _dot(piece, m_bf16)
        rem = rem - piece.astype(F32)
    return acc


def _rmsnorm_kernel(x_ref, g_ref, o_ref):
    x = x_ref[...]
    ms = jnp.mean(x * x, axis=-1, keepdims=True)
    o_ref[...] = (x * lax.rsqrt(ms + RMS_EPS) * g_ref[...]).astype(o_ref.dtype)


def _rmsnorm(x, g, tm=512):
    t, d = x.shape
    return pl.pallas_call(
        _rmsnorm_kernel,
        out_shape=jax.ShapeDtypeStruct((t, d), BF16),
        grid=(t // tm,),
        in_specs=[pl.BlockSpec((tm, d), lambda i: (i, 0)), pl.BlockSpec((1, d), lambda i: (0, 0))],
        out_specs=pl.BlockSpec((tm, d), lambda i: (i, 0)),
        compiler_params=_cparams(("parallel",)),
        name="rmsnorm",
    )(x, g.reshape(1, d))


def _mm_kernel(*refs, nk, act, has_res):
    if has_res:
        a_ref, b_ref, r_ref, o_ref, acc_ref = refs
    else:
        a_ref, b_ref, o_ref, acc_ref = refs
        r_ref = None
    k = pl.program_id(2)

    @pl.when(k == 0)
    def _():
        acc_ref[...] = jnp.zeros_like(acc_ref)

    acc_ref[...] += _dot(a_ref[...], b_ref[...].astype(BF16))

    @pl.when(k == nk - 1)
    def _():
        y = acc_ref[...]
        if act == "relu2":
            y = jnp.square(jnp.maximum(y, 0.0))
        if has_res:
            y = y + r_ref[...]
        o_ref[...] = y.astype(o_ref.dtype)


def _pick(n, cands):
    for c in cands:
        if n % c == 0:
            return c
    raise ValueError(f"no tile for {n}")


def _matmul(a, b, *, res=None, act=None, out_dtype=F32, name="matmul"):
    m, kd = a.shape
    _, n = b.shape
    tm = _pick(m, (1024, 512, 256, 128))
    tn = _pick(n, (1024, 768, 512, 256, 128))
    tk = _pick(kd, (2048, 1024, 512))
    nk = kd // tk
    in_specs = [pl.BlockSpec((tm, tk), lambda i, j, k: (i, k)), pl.BlockSpec((tk, tn), lambda i, j, k: (k, j))]
    args = [a, b]
    if res is not None:
        in_specs.append(pl.BlockSpec((tm, tn), lambda i, j, k: (i, j)))
        args.append(res)
    return pl.pallas_call(
        functools.partial(_mm_kernel, nk=nk, act=act, has_res=res is not None),
        out_shape=jax.ShapeDtypeStruct((m, n), out_dtype),
        grid=(m // tm, n // tn, nk),
        in_specs=in_specs,
        out_specs=pl.BlockSpec((tm, tn), lambda i, j, k: (i, j)),
        scratch_shapes=[pltpu.VMEM((tm, tn), F32)],
        compiler_params=_cparams(("parallel", "parallel", "arbitrary")),
        name=name,
    )(*args)


RET_CHUNK = 256


def _retention_kernel(q_ref, k_ref, v_ref, g_ref, cos_ref, sin_ref, dmat_ref, qd_ref, kd_ref, cd_ref, o_ref, state_ref):
    @pl.when(pl.program_id(0) == 0)
    def _():
        state_ref[...] = jnp.zeros_like(state_ref)

    cos = cos_ref[...]
    sin = sin_ref[...]

    def rope(x):
        return x * cos + pltpu.roll(x, RET_DH // 2, axis=1) * sin

    heads = range(RET_HEADS)
    hs = [slice(h * RET_DH, (h + 1) * RET_DH) for h in heads]
    q_all, k_all, v_all, gate = q_ref[...], k_ref[...], v_ref[...], g_ref[...]
    qb = [rope(q_all[:, s]).astype(BF16) for s in hs]
    kf = [rope(k_all[:, s]) * (RET_DH ** -0.5) for s in hs]
    kb = [kf[h].astype(BF16) for h in heads]
    vb = [v_all[:, s].astype(BF16) for s in hs]
    inner = [(_dot_t(qb[h], kb[h]) * dmat_ref[h]).astype(BF16) for h in heads]
    cross = [_dot(qb[h], state_ref[h].astype(BF16)) * qd_ref[h] for h in heads]
    kv = [_dot_tl((kf[h] * kd_ref[h]).astype(BF16), vb[h]) for h in heads]
    outs = []
    for h in heads:
        out = _dot(inner[h], vb[h]) + cross[h]
        state_ref[h] = cd_ref[h] * state_ref[h] + kv[h]
        outs.append(out * lax.rsqrt(jnp.mean(out * out, axis=-1, keepdims=True) + RMS_EPS))
    o_ref[...] = (gate * jax.nn.sigmoid(gate) * jnp.concatenate(outs, axis=1)).astype(o_ref.dtype)


def _retention(proj, cos, sin):
    t = proj.shape[0]
    chunk = RET_CHUNK
    log_gamma = np.log(1.0 - 2.0 ** (-5.0 - np.arange(RET_HEADS, dtype=np.float64)))[:, None, None]
    n = np.arange(chunk, dtype=np.float64)
    lag = n[:, None] - n[None, :]
    dmat = np.where(lag >= 0, np.exp(np.maximum(lag, 0.0)[None] * log_gamma), 0.0)
    ones = np.ones((1, 1, RET_DH))
    qd = np.exp((n + 1.0)[None, :, None] * log_gamma) * ones
    kd = np.exp((chunk - 1.0 - n)[None, :, None] * log_gamma) * ones
    cd = np.exp(chunk * log_gamma) * ones
    tables = [jnp.asarray(a, F32) for a in (dmat, qd, kd, cd)]
    base = C_RET // RET_W

    def col(off):
        return pl.BlockSpec((chunk, RET_W), lambda c, off=off: (c, base + off))

    def full(a):
        return pl.BlockSpec(a.shape, lambda c: (0, 0, 0))

    tab = pl.BlockSpec((chunk, RET_DH), lambda c: (c, 0))
    return pl.pallas_call(
        _retention_kernel,
        out_shape=jax.ShapeDtypeStruct((t, RET_W), BF16),
        grid=(t // chunk,),
        in_specs=[col(0), col(1), col(2), col(3), tab, tab] + [full(a) for a in tables],
        out_specs=pl.BlockSpec((chunk, RET_W), lambda c: (c, 0)),
        scratch_shapes=[pltpu.VMEM((RET_HEADS, RET_DH, RET_DH), F32)],
        compiler_params=_cparams(("arbitrary",)),
        name="retention",
    )(proj, proj, proj, proj, cos, sin, *tables)


def _dot_f32(a, b):
    a_hi = a.astype(BF16)
    b_hi = b.astype(BF16)
    a_lo = (a - a_hi.astype(F32)).astype(BF16)
    b_lo = (b - b_hi.astype(F32)).astype(BF16)
    return _dot(a_hi, b_hi) + (_dot(a_hi, b_lo) + _dot(a_lo, b_hi))


RW_PRE_TM = 256
_MIX_OFF = (0, 512, 1024, 1536, 1664, 1792, 2048, 2176)


def _rwkv_pre_kernel(r_ref, k_ref, v_ref, xw_ref, xa_ref, xg_ref, xv_ref,
                     rp_ref, kp_ref, vp_ref, xwp_ref, xap_ref, xgp_ref, xvp_ref,
                     mix_ref, w0_ref, w2_ref, a0_ref, a2_ref, v0_ref, v2_ref, g2_ref, kk_ref, ka_ref,
                     gsum_ref, vfirst_ref,
                     r_out, lw_out, k_out, v_out, an_out, b_out, g_out, *, use_vres):
    i = pl.program_id(0)

    def mixed(cur_ref, prev_ref, seg):
        cur = cur_ref[...]
        mix = mix_ref[:, _MIX_OFF[seg]:_MIX_OFF[seg + 1]]
        prev_row = jnp.where(i > 0, prev_ref[7:8, :], 0.0)
        rid = lax.broadcasted_iota(jnp.int32, cur.shape, 0)
        shifted = jnp.where(rid == 0, prev_row, pltpu.roll(cur, 1, axis=0))
        return cur + mix * (shifted - cur)

    r = mixed(r_ref, rp_ref, 0)
    k = mixed(k_ref, kp_ref, 1)
    v = mixed(v_ref, vp_ref, 2)
    xw = mixed(xw_ref, xwp_ref, 3)
    xa = mixed(xa_ref, xap_ref, 4)
    xg = mixed(xg_ref, xgp_ref, 5)

    z = -(w0_ref[...] + _dot_f32(jnp.tanh(xw), w2_ref[...]))
    softplus = jnp.maximum(z, 0.0) + jnp.log(1.0 + jnp.exp(-jnp.abs(z)))
    w = -softplus - 0.5
    lw_out[...] = -jnp.exp(w)
    a = jax.nn.sigmoid(a0_ref[...] + _dot_f32(xa, a2_ref[...]))
    g_out[...] = _dot_f32(jax.nn.sigmoid(xg), g2_ref[...])
    if use_vres:
        xv = mixed(xv_ref, xvp_ref, 6)
        v = v + (vfirst_ref[...] - v) * jax.nn.sigmoid(v0_ref[...] + _dot_f32(xv, v2_ref[...]))
    kk = k * kk_ref[...]
    ss = _split_dot(kk * kk, gsum_ref[...], 3)
    kk = kk / jnp.maximum(jnp.sqrt(ss), 1e-12)
    r_out[...] = r
    k_out[...] = k * (1.0 + (a - 1.0) * ka_ref[...])
    v_out[...] = v
    an_out[...] = -kk
    b_out[...] = kk * a


def _rwkv_pre(proj, mix, w0, w2, a0, a2, v0, v2, g2, k_k, k_a, gsum, v_first, use_vres):
    t = proj.shape[0]
    tm = RW_PRE_TM

    def cur(width, off):
        return pl.BlockSpec((tm, width), lambda i: (i, off // width))

    def prev(width, off):
        return pl.BlockSpec((8, width), lambda i: (jnp.maximum(i * (tm // 8) - 1, 0), off // width))

    def full(a):
        return pl.BlockSpec(a.shape, lambda i: (0,) * a.ndim)

    segs = [(RW_W, C_RW_RKV), (RW_W, C_RW_RKV + RW_W), (RW_W, C_RW_RKV + 2 * RW_W),
            (128, C_RW_XW), (128, C_RW_XA), (256, C_RW_XG), (128, C_RW_XV)]
    params = [mix, w0, w2, a0, a2, v0, v2, g2, k_k, k_a, gsum]
    row = pl.BlockSpec((tm, RW_W), lambda i: (i, 0))
    outs = pl.pallas_call(
        functools.partial(_rwkv_pre_kernel, use_vres=use_vres),
        out_shape=[jax.ShapeDtypeStruct((t, RW_W), F32)] * 7,
        grid=(t // tm,),
        in_specs=[cur(w, o) for w, o in segs] + [prev(w, o) for w, o in segs] + [full(p) for p in params] + [row],
        out_specs=[row] * 7,
        compiler_params=_cparams(("parallel",)),
        name="rwkv_pre",
    )(*([proj] * 14), *params, v_first)
    return outs


RW_CHUNK = 128


def _wkv_kernel(r_ref, lw_ref, k_ref, v_ref, an_ref, b_ref, g_ref, lng_ref, lnb_ref, rk_ref,
                gsum_ref, tril_ref, o_ref, s_ref):
    c = pl.program_id(0)
    C = RW_CHUNK
    N = RW_N

    @pl.when(c == 0)
    def _():
        s_ref[...] = jnp.zeros_like(s_ref)

    r = r_ref[...]
    lw = lw_ref[...]
    k = k_ref[...]
    v = v_ref[...]
    lw_hi = lw.astype(BF16)
    rem = lw - lw_hi.astype(F32)
    lw_mid = rem.astype(BF16)
    lw_lo = (rem - lw_mid.astype(F32)).astype(BF16)
    tril = tril_ref[...]
    lg = _dot(tril, lw_hi) + (_dot(tril, lw_mid) + _dot(tril, lw_lo))
    gam = jnp.exp(lg)
    ginv = jnp.exp(-lg)
    at = (an_ref[...] * jnp.exp(lg - lw)).astype(BF16)
    bt = (b_ref[...] * ginv).astype(BF16)
    kt = (k * ginv).astype(BF16)
    rt_f = r * gam
    rt = rt_f.astype(BF16)
    vb = v.astype(BF16)
    g_last = gam[C - 1:C, :]

    rowi = lax.broadcasted_iota(jnp.int32, (C, C), 0)
    coli = lax.broadcasted_iota(jnp.int32, (C, C), 1)
    strict = rowi > coli
    incl = rowi >= coli

    heads = range(RW_HEADS)
    hs = [slice(h * N, (h + 1) * N) for h in heads]
    bth = [bt[:, s] for s in hs]
    kth = [kt[:, s] for s in hs]
    vh = [vb[:, s] for s in hs]
    big = [_dot_t(jnp.concatenate([at[:, s], rt[:, s]], axis=0), jnp.concatenate([bth[h], kth[h]], axis=0))
           for h, s in zip(heads, hs)]
    a_ab = [jnp.where(strict, big[h][:C, :C], 0.0) for h in heads]
    a_ak = [jnp.where(strict, big[h][:C, C:], 0.0).astype(BF16) for h in heads]
    a_rb = [jnp.where(incl, big[h][C:, :C], 0.0).astype(BF16) for h in heads]
    a_rk = [jnp.where(incl, big[h][C:, C:], 0.0).astype(BF16) for h in heads]
    akv = [_dot(a_ak[h], vh[h]) for h in heads]
    def same_block(size):
        shift = size.bit_length() - 1
        return (rowi >> shift) == (coli >> shift)

    tinv = [jnp.where(same_block(2), a_ab[h], 0.0) + jnp.where(rowi == coli, 1.0, 0.0) for h in heads]
    size = 2
    while size < C:
        lower_left = same_block(2 * size) & jnp.logical_not(same_block(size))
        off = [jnp.where(lower_left, a_ab[h], 0.0).astype(BF16) for h in heads]
        tb = [tinv[h].astype(BF16) for h in heads]
        half = [_dot(tb[h], off[h]).astype(BF16) for h in heads]
        tinv = [tinv[h] + _dot(half[h], tb[h]) for h in heads]
        size *= 2
    xb = [_dot(tinv[h].astype(BF16), jnp.concatenate([at[:, hs[h]], akv[h].astype(BF16)], axis=1)).astype(BF16)
          for h in heads]
    yx = [_dot(a_rb[h], xb[h]) for h in heads]
    ykv = [_dot(a_rk[h], vh[h]) for h in heads]
    xtb = [_dot_tl(xb[h], bth[h]) for h in heads]
    vtk = [_dot_tl(vh[h], kth[h]) for h in heads]
    ys = []
    for h in heads:
        gl = g_last[:, hs[h]]
        y1 = (rt_f[:, hs[h]] + yx[h][:, :N]).astype(BF16)
        s0 = s_ref[h]
        s0b = s0.astype(BF16)
        ys.append(_dot_t(y1, s0b) + (yx[h][:, N:] + ykv[h]))
        s_ref[h] = (s0 + _dot(s0b, xtb[h][:N].astype(BF16)) + (xtb[h][N:] + vtk[h])) * gl
    y = jnp.concatenate(ys, axis=1)

    gsum = gsum_ref[...]
    inv_n = 1.0 / N
    mu = _split_dot(y, gsum, 3) * inv_n
    yc = y - mu
    var = _split_dot(yc * yc, gsum, 3) * inv_n
    yn = yc * lax.rsqrt(var + RW_GN_EPS) * lng_ref[...] + lnb_ref[...]
    bonus = _split_dot(r * k * rk_ref[...], gsum, 3)
    o_ref[...] = ((yn + bonus * v) * g_ref[...]).astype(o_ref.dtype)


def _wkv(r, lw, k, v, an, b, g, ln_g, ln_b, r_k, gsum):
    t = r.shape[0]
    C = RW_CHUNK
    tril = jnp.asarray(np.tril(np.ones((C, C), np.float32)), BF16)
    row = pl.BlockSpec((C, RW_W), lambda c: (c, 0))
    vec = pl.BlockSpec((1, RW_W), lambda c: (0, 0))
    return pl.pallas_call(
        _wkv_kernel,
        out_shape=jax.ShapeDtypeStruct((t, RW_W), BF16),
        grid=(t // C,),
        in_specs=[row] * 7 + [vec] * 3 + [pl.BlockSpec((RW_W, RW_W), lambda c: (0, 0)),
                                          pl.BlockSpec((C, C), lambda c: (0, 0))],
        out_specs=row,
        scratch_shapes=[pltpu.VMEM((RW_HEADS, RW_N, RW_N), F32)],
        compiler_params=_cparams(("arbitrary",)),
        name="wkv7",
    )(r, lw, k, v, an, b, g, ln_g, ln_b, r_k, gsum, tril)


NSA_PREP_TM = 512


def _nsa_prep_kernel(*refs, n_slabs):
    x_refs = refs[:n_slabs]
    cos_ref, sin_ref, w_ref, gmean_ref, o_ref = refs[n_slabs:]
    cos = cos_ref[...]
    sin = sin_ref[...]
    gmean = gmean_ref[...]
    half = NSA_DH // 2
    first_half = (lax.broadcasted_iota(jnp.int32, cos.shape, 1) % NSA_DH) < half
    for s in range(n_slabs):
        x = x_refs[s][...]
        y = x * lax.rsqrt(_split_dot(x * x, gmean, 3) + RMS_EPS) * w_ref[s]
        rot = jnp.where(first_half, pltpu.roll(y, LANES - half, axis=1), pltpu.roll(y, half, axis=1))
        res = (y * cos + rot * sin).astype(o_ref.dtype)
        o_ref[2 * s] = res[:, :NSA_DH]
        o_ref[2 * s + 1] = res[:, NSA_DH:]


def _nsa_prep(proj, col_off, n_slabs, w_slabs, cos, sin, gmean):
    t = proj.shape[0]
    tm = NSA_PREP_TM
    base = col_off // LANES
    tab = pl.BlockSpec((tm, LANES), lambda i: (i, 0))
    return pl.pallas_call(
        functools.partial(_nsa_prep_kernel, n_slabs=n_slabs),
        out_shape=jax.ShapeDtypeStruct((2 * n_slabs, t, NSA_DH), BF16),
        grid=(t // tm,),
        in_specs=[pl.BlockSpec((tm, LANES), lambda i, s=s: (i, base + s)) for s in range(n_slabs)]
        + [tab, tab, pl.BlockSpec((n_slabs, 1, LANES), lambda i: (0, 0, 0)),
           pl.BlockSpec((LANES, LANES), lambda i: (0, 0))],
        out_specs=pl.BlockSpec((2 * n_slabs, tm, NSA_DH), lambda i: (0, i, 0)),
        compiler_params=_cparams(("parallel",)),
        name="nsa_prep",
    )(*([proj] * n_slabs), cos, sin, w_slabs, gmean)


def _compress_kernel(h_ref, w1_ref, w2_ref, pe_ref, nw_ref, cos_ref, sin_ref, o_ref):
    kind = pl.program_id(0)
    nc = h_ref.shape[2]
    half_in = CMP_STRIDE * NSA_DH
    hm = h_ref[0, 0]
    w1 = w1_ref[0]
    first = _dot(hm, w1[:half_in])
    second = _dot(hm, w1[half_in:])
    const = _dot(pe_ref[0], w1)[0:1, :]
    pre = first + pltpu.roll(second, nc - 1, axis=0) + const
    hid = 0.5 * pre * (1.0 + jnp.tanh(0.7978845608028654 * (pre + 0.044715 * pre * pre * pre)))
    out = _dot(hid.astype(BF16), w2_ref[0])
    y = out * lax.rsqrt(jnp.mean(out * out, axis=-1, keepdims=True) + RMS_EPS) * nw_ref[...]
    half = NSA_DH // 2
    rot = jnp.concatenate([y[:, half:], y[:, :half]], axis=1)
    roped = y * cos_ref[...] + rot * sin_ref[...]
    o_ref[0, 0] = jnp.where(kind == 0, roped, out).astype(o_ref.dtype)


def _compress(hmat, w1, w2, pe, nw, cos_c, sin_c):
    _, g, nc, width = hmat.shape
    return pl.pallas_call(
        _compress_kernel,
        out_shape=jax.ShapeDtypeStruct((2, g, nc, NSA_DH), BF16),
        grid=(2, g),
        in_specs=[pl.BlockSpec((1, 1, nc, width), lambda a, b: (a, b, 0, 0)),
                  pl.BlockSpec((1, 2 * width, CMP_HIDDEN), lambda a, b: (a, 0, 0)),
                  pl.BlockSpec((1, CMP_HIDDEN, NSA_DH), lambda a, b: (a, 0, 0)),
                  pl.BlockSpec((1, 8, 2 * width), lambda a, b: (a, 0, 0)),
                  pl.BlockSpec((1, NSA_DH), lambda a, b: (0, 0)),
                  pl.BlockSpec((nc, NSA_DH), lambda a, b: (0, 0)),
                  pl.BlockSpec((nc, NSA_DH), lambda a, b: (0, 0))],
        out_specs=pl.BlockSpec((1, 1, nc, NSA_DH), lambda a, b: (a, b, 0, 0)),
        compiler_params=_cparams(("parallel", "parallel")),
        name="nsa_compress",
    )(hmat, w1, w2, pe, nw, cos_c, sin_c)


SLC_KT = 1024
WIN_KEYS = WINDOW + Q_BLOCK


INT32_MIN = -2 ** 31


def _nsa_attn_kernel(q_ref, gate_ref, kc_ref, vc_ref, ks_ref, vs_ref, kw_ref, vw_ref, cis_ref, expand_ref, ltri_ref,
                     o_ref):
    qi = pl.program_id(1)
    nc = kc_ref.shape[2]
    nb = cis_ref.shape[1]
    heads = range(NSA_HG)
    start = qi * Q_BLOCK
    qh = [q_ref[hh] for hh in heads]
    tok = start + lax.broadcasted_iota(jnp.int32, (Q_BLOCK, 1), 0)

    def softmax_terms(scores):
        return [jnp.exp(s - jnp.max(s, axis=-1, keepdims=True)) for s in scores]

    kc = kc_ref[0, 0]
    vc = vc_ref[0, 0]
    cend = lax.broadcasted_iota(jnp.int32, (1, nc), 1) * CMP_STRIDE + (CMP_BLOCK - 1)
    bias_c = jnp.where(cend <= tok, 0.0, NEG_INF)
    ec = softmax_terms([_dot_t(qh[hh], kc) + bias_c for hh in heads])
    sees_any = tok >= CMP_BLOCK - 1
    pc = [ec[hh] * jnp.where(sees_any, 1.0 / jnp.sum(ec[hh], axis=-1, keepdims=True), 0.0) for hh in heads]
    o_c = [_dot(pc[hh].astype(BF16), vc) for hh in heads]

    kbase = pl.multiple_of(jnp.maximum(start - WINDOW, 0), Q_BLOCK)
    kw = kw_ref[0, pl.ds(kbase, WIN_KEYS), :]
    vw = vw_ref[0, pl.ds(kbase, WIN_KEYS), :]
    lag = tok - (kbase + lax.broadcasted_iota(jnp.int32, (1, WIN_KEYS), 1))
    bias_w = jnp.where((lag >= 0) & (lag < WINDOW), 0.0, NEG_INF)
    ew = softmax_terms([_dot_t(qh[hh], kw) + bias_w for hh in heads])
    acc_w = [_dot(ew[hh].astype(BF16), vw) for hh in heads]

    psum = pc[0]
    for hh in range(1, NSA_HG):
        psum = psum + pc[hh]
    score = _split_dot(psum, cis_ref[...], 3)
    blk = lax.broadcasted_iota(jnp.int32, (1, nb), 1)
    cur = tok // SLC_BLOCK
    forced = (blk == 0) | (blk == cur) | (blk == cur - 1)
    score = jnp.where(forced, FORCED_SCORE, score)
    score = jnp.where(blk <= cur, score, -jnp.inf)

    bits = lax.bitcast_convert_type(score.T, jnp.int32)
    key = bits ^ ((bits >> 31) & 0x7FFFFFFF)
    n_sel = float(min(SLC_TOPK, nb))

    def enough(c):
        return jnp.sum(jnp.where(key >= c, 1.0, 0.0), axis=0, keepdims=True) >= n_sel

    zero_row = jnp.zeros((1, Q_BLOCK), jnp.int32)
    thr = jnp.where(enough(zero_row), zero_row, INT32_MIN)
    for hi in range(30, 0, -2):
        c1 = thr + (1 << (hi - 1))
        c2 = thr + (1 << hi)
        c3 = c2 + (1 << (hi - 1))
        thr = jnp.where(enough(c3), c3, jnp.where(enough(c2), c2, jnp.where(enough(c1), c1, thr)))
    c1 = thr + 1
    thr = jnp.where(enough(c1), c1, thr)
    above = key > thr
    tied = key == thr
    need = n_sel - jnp.sum(jnp.where(above, 1.0, 0.0), axis=0, keepdims=True)
    tied_before = _dot(ltri_ref[...], jnp.where(tied, 1.0, 0.0).astype(BF16))
    sel_t = above | (tied & (tied_before < need))
    sel = jnp.where(sel_t, 1.0, 0.0).T.astype(BF16)

    n_tiles = (start + Q_BLOCK + SLC_KT - 1) // SLC_KT

    def sel_body(j, carry):
        ms, accs = carry
        k0 = pl.multiple_of(j * SLC_KT, SLC_KT)
        kt = ks_ref[0, pl.ds(k0, SLC_KT), :]
        vt = vs_ref[0, pl.ds(k0, SLC_KT), :]
        member = _dot(sel, expand_ref[:, pl.ds(k0, SLC_KT)])
        kpos = k0 + lax.broadcasted_iota(jnp.int32, (1, SLC_KT), 1)
        bias = jnp.where((member > 0.5) & (kpos <= tok), 0.0, NEG_INF)
        sj = [_dot_t(qh[hh], kt) + bias for hh in heads]
        m_new = [jnp.maximum(ms[hh], jnp.max(sj[hh], axis=-1, keepdims=True)) for hh in heads]
        pj = [jnp.exp(sj[hh] - m_new[hh]).astype(BF16) for hh in heads]
        accs = [jnp.exp(ms[hh] - m_new[hh]) * accs[hh] + _dot(pj[hh], vt) for hh in heads]
        return tuple(m_new), tuple(accs)

    m0 = tuple(jnp.full((Q_BLOCK, 1), NEG_INF, F32) for _ in heads)
    a0 = tuple(jnp.zeros((Q_BLOCK, LANES), F32) for _ in heads)
    _, acc_s = lax.fori_loop(0, n_tiles, sel_body, (m0, a0))

    gt = jax.nn.sigmoid(gate_ref[0])
    outs = []
    for hh in heads:
        o_s = acc_s[hh][:, :NSA_DH] / acc_s[hh][:, NSA_DH:NSA_DH + 1]
        o_w = acc_w[hh][:, :NSA_DH] / acc_w[hh][:, NSA_DH:NSA_DH + 1]
        g0, g1, g2 = (gt[:, 3 * hh + br:3 * hh + br + 1] for br in range(3))
        outs.append(g0 * o_c[hh] + g1 * o_s + g2 * o_w)
    o_ref[...] = jnp.concatenate(outs, axis=1).astype(o_ref.dtype)


def _nsa_attn(qk_hm, gates, cmp_kv, vs1, vw1, cis, expand, ltri):
    t = qk_hm.shape[1]
    nq = t // Q_BLOCK
    nc = cmp_kv.shape[2]
    nb = cis.shape[1]
    ks_spec = pl.BlockSpec((1, t, NSA_DH), lambda g, i: (NSA_HEADS + g, 0, 0))
    kw_spec = pl.BlockSpec((1, t, NSA_DH), lambda g, i: (NSA_HEADS + NSA_G + g, 0, 0))
    vspec = pl.BlockSpec((1, t, LANES), lambda g, i: (g, 0, 0))
    return pl.pallas_call(
        _nsa_attn_kernel,
        out_shape=jax.ShapeDtypeStruct((t, NSA_W), BF16),
        grid=(NSA_G, nq),
        in_specs=[pl.BlockSpec((NSA_HG, Q_BLOCK, NSA_DH), lambda g, i: (g, i, 0)),
                  pl.BlockSpec((1, Q_BLOCK, LANES), lambda g, i: (g, i, 0)),
                  pl.BlockSpec((1, 1, nc, NSA_DH), lambda g, i: (0, g, 0, 0)),
                  pl.BlockSpec((1, 1, nc, NSA_DH), lambda g, i: (1, g, 0, 0)),
                  ks_spec, vspec, kw_spec, vspec,
                  pl.BlockSpec((nc, nb), lambda g, i: (0, 0)),
                  pl.BlockSpec((nb, t), lambda g, i: (0, 0)),
                  pl.BlockSpec((nb, nb), lambda g, i: (0, 0))],
        out_specs=pl.BlockSpec((Q_BLOCK, NSA_HG * NSA_DH), lambda g, i: (i, g)),
        compiler_params=_cparams(("parallel", "arbitrary")),
        name="nsa_attention",
    )(qk_hm, gates, cmp_kv, cmp_kv, qk_hm, vs1, qk_hm, vw1, cis, expand, ltri)


def _rope_tables(pos, dh, reps):
    half = dh // 2
    inv_freq = ROPE_THETA ** (-jnp.arange(half, dtype=F32) / half)
    ang = pos.astype(F32)[:, None] * inv_freq[None, :]
    cos = jnp.cos(ang)
    sin = jnp.sin(ang)
    cos_t = jnp.tile(jnp.concatenate([cos, cos], axis=1), (1, reps))
    sin_t = jnp.tile(jnp.concatenate([-sin, sin], axis=1), (1, reps))
    return cos_t, sin_t


def _pad_cols(a, width):
    return jnp.pad(a, ((0, 0), (0, width - a.shape[1])))


def _pad_rows(a, height):
    return jnp.pad(a, ((0, height - a.shape[0]), (0, 0)))


def _pack_w_in(w_l, w_vres):
    d = w_l.shape[0]
    o = 0
    ret = w_l[:, o:o + 4 * RET_W]; o += 4 * RET_W
    rkv = w_l[:, o:o + 3 * RW_W]; o += 3 * RW_W
    xw = w_l[:, o:o + RW_DECAY_RANK]; o += RW_DECAY_RANK
    xa = w_l[:, o:o + RW_A_RANK]; o += RW_A_RANK
    xg = w_l[:, o:o + RW_GATE_RANK]; o += RW_GATE_RANK
    q = w_l[:, o:o + NSA_W]; o += NSA_W
    kc, vc, ks, vs, kw, vw = (w_l[:, o + i * NSA_KV_W:o + (i + 1) * NSA_KV_W] for i in range(6))
    o += 6 * NSA_KV_W
    gates = w_l[:, o:o + 3 * NSA_HEADS]
    xv = jnp.zeros((d, LANES), w_l.dtype) if w_vres is None else _pad_cols(w_vres, LANES)
    return jnp.concatenate([ret, rkv, _pad_cols(xw, LANES), _pad_cols(xa, LANES), xg, xv,
                            q, ks, kw, kc, vc, vs, vw, _pad_cols(gates, LANES)], axis=1)


def _pack_mix(mix, vres_mix):
    o = 3 * RW_W
    xw = mix[o:o + RW_DECAY_RANK]; o += RW_DECAY_RANK
    xa = mix[o:o + RW_A_RANK]; o += RW_A_RANK
    xg = mix[o:o + RW_GATE_RANK]
    z = lambda n: jnp.zeros((n,), mix.dtype)
    xv = z(LANES) if vres_mix is None else jnp.concatenate([vres_mix, z(LANES - RW_V_RANK)])
    return jnp.concatenate([mix[:3 * RW_W], xw, z(LANES - RW_DECAY_RANK), xa, z(LANES - RW_A_RANK), xg, xv])[None, :]


def _head_major(a, g):
    t = a.shape[0]
    return a.reshape(t, g, NSA_DH).transpose(1, 0, 2)


def kernel(x, ln1_g, w_in, w_in_vres, rwkv_mix, rwkv_vres_mix, rwkv_w0, rwkv_w2, rwkv_a0, rwkv_a2, rwkv_v0, rwkv_v2, rwkv_g2, rwkv_k_k, rwkv_k_a, rwkv_r_k, rwkv_ln_g, rwkv_ln_b, nsa_q_norm, nsa_k_norm, nsa_cmp_pe, nsa_cmp_k_w1, nsa_cmp_k_w2, nsa_cmp_v_w1, nsa_cmp_v_w2, w_out, ln2_g, w_up, w_down):
    bsz, t, d = x.shape
    assert bsz == 1
    depth = w_in.shape[0]
    nc = t // CMP_STRIDE
    nb = t // SLC_BLOCK
    xs = x.reshape(t, d)

    pos = jnp.arange(t)
    cos_r, sin_r = _rope_tables(pos, RET_DH, 1)
    cos_n, sin_n = _rope_tables(pos, NSA_DH, 2)
    cos_c, sin_c = _rope_tables(jnp.arange(nc) * CMP_STRIDE + (CMP_BLOCK - 1), NSA_DH, 1)
    lane_head = np.arange(RW_W) // RW_N
    gsum = jnp.asarray(lane_head[:, None] == lane_head[None, :], BF16)
    lane_h2 = np.arange(LANES) // NSA_DH
    gmean = jnp.asarray((lane_h2[:, None] == lane_h2[None, :]) / float(NSA_DH), BF16)
    cstart = np.arange(nc) * CMP_STRIDE
    sstart = np.arange(nb) * SLC_BLOCK
    cis = jnp.asarray((cstart[:, None] <= sstart[None, :] + SLC_BLOCK - 1)
                      & (cstart[:, None] + CMP_BLOCK - 1 >= sstart[None, :]), BF16)

    expand = jnp.asarray((np.arange(t)[None, :] // SLC_BLOCK) == np.arange(nb)[:, None], BF16)
    ltri = jnp.asarray(np.tril(np.ones((nb, nb), np.float32), -1), BF16)

    def with_ones(v_hm):
        pad = jnp.zeros(v_hm.shape[:-1] + (LANES - NSA_DH,), v_hm.dtype).at[..., 0].set(1)
        return jnp.concatenate([v_hm, pad], axis=-1)

    v_first = jnp.zeros((t, RW_W), F32)
    for l in range(depth):
        vres = l > 0
        w_cat = _pack_w_in(w_in[l], w_in_vres[l - 1] if vres else None).astype(BF16)
        h = _rmsnorm(xs, ln1_g[l])
        proj = _matmul(h, w_cat, name="in_proj")

        o_ret = _retention(proj, cos_r, sin_r)

        row = lambda a: a.reshape(1, -1)
        mix = _pack_mix(rwkv_mix[l], rwkv_vres_mix[l - 1] if vres else None)
        v0 = row(rwkv_v0[l - 1]) if vres else jnp.zeros((1, RW_W), F32)
        v2 = _pad_rows(rwkv_v2[l - 1], LANES) if vres else jnp.zeros((LANES, RW_W), F32)
        r_, lw_, k_, v_, an_, b_, g_ = _rwkv_pre(
            proj, mix, row(rwkv_w0[l]), _pad_rows(rwkv_w2[l], LANES), row(rwkv_a0[l]), _pad_rows(rwkv_a2[l], LANES),
            v0, v2, rwkv_g2[l], row(rwkv_k_k[l]), row(rwkv_k_a[l]), gsum, v_first, vres)
        if not vres:
            v_first = v_
        o_rwkv = _wkv(r_, lw_, k_, v_, an_, b_, g_, row(rwkv_ln_g[l]), row(rwkv_ln_b[l]), row(rwkv_r_k[l]), gsum)

        n_q = NSA_W // LANES
        n_k = NSA_KV_W // LANES
        w_slabs = jnp.stack([jnp.tile(nsa_q_norm[l] * (NSA_DH ** -0.5), 2)] * n_q
                            + [jnp.tile(nsa_k_norm[l, 1], 2)] * n_k + [jnp.tile(nsa_k_norm[l, 2], 2)] * n_k)[:, None, :]
        qk_hm = _nsa_prep(proj, C_NSA_Q, n_q + 2 * n_k, w_slabs, cos_n, sin_n, gmean)
        raw = lambda off: _head_major(proj[:, off:off + NSA_KV_W], NSA_G).astype(BF16)
        hmat = jnp.stack([raw(C_NSA_KC), raw(C_NSA_VC)]).reshape(2, NSA_G, nc, CMP_STRIDE * NSA_DH)
        w1 = jnp.stack([nsa_cmp_k_w1[l], nsa_cmp_v_w1[l]]).astype(BF16)
        w2 = jnp.stack([nsa_cmp_k_w2[l], nsa_cmp_v_w2[l]]).astype(BF16)
        pe = jnp.broadcast_to(nsa_cmp_pe[l].reshape(2, 1, CMP_BLOCK * NSA_DH), (2, 8, CMP_BLOCK * NSA_DH)).astype(BF16)
        cmp_kv = _compress(hmat, w1, w2, pe, row(nsa_k_norm[l, 0]), cos_c, sin_c)
        gates = jnp.pad(proj[:, C_NSA_GATE:C_NSA_GATE + 3 * NSA_HEADS].reshape(t, NSA_G, 3 * NSA_HG).transpose(1, 0, 2),
                        ((0, 0), (0, 0), (0, LANES - 3 * NSA_HG)))
        o_nsa = _nsa_attn(qk_hm, gates, cmp_kv, with_ones(raw(C_NSA_VS)), with_ones(raw(C_NSA_VW)), cis, expand, ltri)

        mixed = jnp.concatenate([o_ret, o_rwkv, o_nsa], axis=1)
        xs = _matmul(mixed, w_out[l], res=xs, name="out_proj")

        h2 = _rmsnorm(xs, ln2_g[l])
        up = _matmul(h2, w_up[l], act="relu2", out_dtype=BF16, name="mlp_up")
        xs = _matmul(up, w_down[l], res=xs, name="mlp_down")
    return xs.reshape(bsz, t, d)
```

```python
import functools

import numpy as np
import jax
import jax.numpy as jnp
from jax import lax
from jax.experimental import pallas as pl
from jax.experimental.pallas import tpu as pltpu

F32 = jnp.float32
BF16 = jnp.bfloat16

D_MODEL = 2048
RET_HEADS, RET_DH = 4, 128
RET_W = RET_HEADS * RET_DH
RW_HEADS, RW_N = 8, 64
RW_W = RW_HEADS * RW_N
RW_DECAY_RANK, RW_A_RANK, RW_V_RANK, RW_GATE_RANK = 96, 96, 64, 256
RW_GN_EPS = 64e-5
NSA_HEADS, NSA_G, NSA_DH = 16, 4, 64
NSA_HG = NSA_HEADS // NSA_G
NSA_W = NSA_HEADS * NSA_DH
NSA_KV_W = NSA_G * NSA_DH
CMP_BLOCK, CMP_STRIDE, CMP_HIDDEN = 32, 16, 256
SLC_BLOCK, SLC_TOPK, WINDOW, Q_BLOCK = 64, 16, 512, 128
ROPE_THETA = 10000.0
MLP_HIDDEN = 4 * D_MODEL
RMS_EPS = 1e-6
NEG_INF = -1e30
FORCED_SCORE = 1e9

LANES = 128
VMEM_LIMIT = 56 * 1024 * 1024

C_RET = 0
C_RW_RKV = 2048
C_RW_XW = 3584
C_RW_XA = 3712
C_RW_XG = 3840
C_RW_XV = 4096
C_NSA_Q = 4224
C_NSA_KS = 5248
C_NSA_KW = 5504
C_NSA_KC = 5760
C_NSA_VC = 6016
C_NSA_VS = 6272
C_NSA_VW = 6528
C_NSA_GATE = 6784
PROJ_W = 6912


def _cparams(sem):
    return pltpu.CompilerParams(dimension_semantics=sem, vmem_limit_bytes=VMEM_LIMIT)


def _dot(a, b):
    return jnp.dot(a, b, preferred_element_type=F32)


def _dot_t(a, b):
    return lax.dot_general(a, b, (((1,), (1,)), ((), ())), preferred_element_type=F32)


def _dot_tl(a, b):
    return lax.dot_general(a, b, (((0,), (0,)), ((), ())), preferred_element_type=F32)


def _split_dot(x, m_bf16, passes):
    hi = x.astype(BF16)
    acc = _dot(hi, m_bf16)
    rem = x - hi.astype(F32)
    for _ in range(passes - 1):
        piece = rem.astype(BF16)
        acc = acc + _dot(piece, m_bf16)
        rem = rem - piece.astype(F32)
    return acc


def _rmsnorm_kernel(x_ref, g_ref, o_ref):
    x = x_ref[...]
    ms = jnp.mean(x * x, axis=-1, keepdims=True)
    o_ref[...] = (x * lax.rsqrt(ms + RMS_EPS) * g_ref[...]).astype(o_ref.dtype)


def _rmsnorm(x, g, tm=512):
    t, d = x.shape
    return pl.pallas_call(
        _rmsnorm_kernel,
        out_shape=jax.ShapeDtypeStruct((t, d), BF16),
        grid=(t // tm,),
        in_specs=[pl.BlockSpec((tm, d), lambda i: (i, 0)), pl.BlockSpec((1, d), lambda i: (0, 0))],
        out_specs=pl.BlockSpec((tm, d), lambda i: (i, 0)),
        compiler_params=_cparams(("parallel",)),
        name="rmsnorm",
    )(x, g.reshape(1, d))


def _mm_kernel(*refs, nk, act, has_res):
    if has_res:
        a_ref, b_ref, r_ref, o_ref, acc_ref = refs
    else:
        a_ref, b_ref, o_ref, acc_ref = refs
        r_ref = None
    k = pl.program_id(2)

    @pl.when(k == 0)
    def _():
        acc_ref[...] = jnp.zeros_like(acc_ref)

    acc_ref[...] += _dot(a_ref[...], b_ref[...].astype(BF16))

    @pl.when(k == nk - 1)
    def _():
        y = acc_ref[...]
        if act == "relu2":
            y = jnp.square(jnp.maximum(y, 0.0))
        if has_res:
            y = y + r_ref[...]
        o_ref[...] = y.astype(o_ref.dtype)


def _pick(n, cands):
    for c in cands:
        if n % c == 0:
            return c
    raise ValueError(f"no tile for {n}")


def _matmul(a, b, *, res=None, act=None, out_dtype=F32, name="matmul"):
    m, kd = a.shape
    _, n = b.shape
    tm = _pick(m, (1024, 512, 256, 128))
    tn = _pick(n, (1024, 768, 512, 256, 128))
    tk = _pick(kd, (2048, 1024, 512))
    nk = kd // tk
    in_specs = [pl.BlockSpec((tm, tk), lambda i, j, k: (i, k)), pl.BlockSpec((tk, tn), lambda i, j, k: (k, j))]
    args = [a, b]
    if res is not None:
        in_specs.append(pl.BlockSpec((tm, tn), lambda i, j, k: (i, j)))
        args.append(res)
    return pl.pallas_call(
        functools.partial(_mm_kernel, nk=nk, act=act, has_res=res is not None),
        out_shape=jax.ShapeDtypeStruct((m, n), out_dtype),
        grid=(m // tm, n // tn, nk),
        in_specs=in_specs,
        out_specs=pl.BlockSpec((tm, tn), lambda i, j, k: (i, j)),
        scratch_shapes=[pltpu.VMEM((tm, tn), F32)],
        compiler_params=_cparams(("parallel", "parallel", "arbitrary")),
        name=name,
    )(*args)


RET_CHUNK = 256


def _retention_kernel(q_ref, k_ref, v_ref, g_ref, cos_ref, sin_ref, dmat_ref, qd_ref, kd_ref, cd_ref, o_ref, state_ref):
    @pl.when(pl.program_id(0) == 0)
    def _():
        state_ref[...] = jnp.zeros_like(state_ref)

    cos = cos_ref[...]
    sin = sin_ref[...]

    def rope(x):
        return x * cos + pltpu.roll(x, RET_DH // 2, axis=1) * sin

    heads = range(RET_HEADS)
    hs = [slice(h * RET_DH, (h + 1) * RET_DH) for h in heads]
    q_all, k_all, v_all, gate = q_ref[...], k_ref[...], v_ref[...], g_ref[...]
    qb = [rope(q_all[:, s]).astype(BF16) for s in hs]
    kf = [rope(k_all[:, s]) * (RET_DH ** -0.5) for s in hs]
    kb = [kf[h].astype(BF16) for h in heads]
    vb = [v_all[:, s].astype(BF16) for s in hs]
    inner = [(_dot_t(qb[h], kb[h]) * dmat_ref[h]).astype(BF16) for h in heads]
    cross = [_dot(qb[h], state_ref[h].astype(BF16)) * qd_ref[h] for h in heads]
    kv = [_dot_tl((kf[h] * kd_ref[h]).astype(BF16), vb[h]) for h in heads]
    outs = []
    for h in heads:
        out = _dot(inner[h], vb[h]) + cross[h]
        state_ref[h] = cd_ref[h] * state_ref[h] + kv[h]
        outs.append(out * lax.rsqrt(jnp.mean(out * out, axis=-1, keepdims=True) + RMS_EPS))
    o_ref[...] = (gate * jax.nn.sigmoid(gate) * jnp.concatenate(outs, axis=1)).astype(o_ref.dtype)


def _retention(proj, cos, sin):
    t = proj.shape[0]
    chunk = RET_CHUNK
    log_gamma = np.log(1.0 - 2.0 ** (-5.0 - np.arange(RET_HEADS, dtype=np.float64)))[:, None, None]
    n = np.arange(chunk, dtype=np.float64)
    lag = n[:, None] - n[None, :]
    dmat = np.where(lag >= 0, np.exp(np.maximum(lag, 0.0)[None] * log_gamma), 0.0)
    ones = np.ones((1, 1, RET_DH))
    qd = np.exp((n + 1.0)[None, :, None] * log_gamma) * ones
    kd = np.exp((chunk - 1.0 - n)[None, :, None] * log_gamma) * ones
    cd = np.exp(chunk * log_gamma) * ones
    tables = [jnp.asarray(a, F32) for a in (dmat, qd, kd, cd)]
    base = C_RET // RET_W

    def col(off):
        return pl.BlockSpec((chunk, RET_W), lambda c, off=off: (c, base + off))

    def full(a):
        return pl.BlockSpec(a.shape, lambda c: (0, 0, 0))

    tab = pl.BlockSpec((chunk, RET_DH), lambda c: (c, 0))
    return pl.pallas_call(
        _retention_kernel,
        out_shape=jax.ShapeDtypeStruct((t, RET_W), BF16),
        grid=(t // chunk,),
        in_specs=[col(0), col(1), col(2), col(3), tab, tab] + [full(a) for a in tables],
        out_specs=pl.BlockSpec((chunk, RET_W), lambda c: (c, 0)),
        scratch_shapes=[pltpu.VMEM((RET_HEADS, RET_DH, RET_DH), F32)],
        compiler_params=_cparams(("arbitrary",)),
        name="retention",
    )(proj, proj, proj, proj, cos, sin, *tables)


def _dot_f32(a, b):
    a_hi = a.astype(BF16)
    b_hi = b.astype(BF16)
    a_lo = (a - a_hi.astype(F32)).astype(BF16)
    b_lo = (b - b_hi.astype(F32)).astype(BF16)
    return _dot(a_hi, b_hi) + (_dot(a_hi, b_lo) + _dot(a_lo, b_hi))


RW_PRE_TM = 256
_MIX_OFF = (0, 512, 1024, 1536, 1664, 1792, 2048, 2176)


def _rwkv_pre_kernel(r_ref, k_ref, v_ref, xw_ref, xa_ref, xg_ref, xv_ref,
                     rp_ref, kp_ref, vp_ref, xwp_ref, xap_ref, xgp_ref, xvp_ref,
                     mix_ref, w0_ref, w2_ref, a0_ref, a2_ref, v0_ref, v2_ref, g2_ref, kk_ref, ka_ref,
                     gsum_ref, vfirst_ref,
                     r_out, lw_out, k_out, v_out, an_out, b_out, g_out, *, use_vres):
    i = pl.program_id(0)

    def mixed(cur_ref, prev_ref, seg):
        cur = cur_ref[...]
        mix = mix_ref[:, _MIX_OFF[seg]:_MIX_OFF[seg + 1]]
        prev_row = jnp.where(i > 0, prev_ref[7:8, :], 0.0)
        rid = lax.broadcasted_iota(jnp.int32, cur.shape, 0)
        shifted = jnp.where(rid == 0, prev_row, pltpu.roll(cur, 1, axis=0))
        return cur + mix * (shifted - cur)

    r = mixed(r_ref, rp_ref, 0)
    k = mixed(k_ref, kp_ref, 1)
    v = mixed(v_ref, vp_ref, 2)
    xw = mixed(xw_ref, xwp_ref, 3)
    xa = mixed(xa_ref, xap_ref, 4)
    xg = mixed(xg_ref, xgp_ref, 5)

    z = -(w0_ref[...] + _dot_f32(jnp.tanh(xw), w2_ref[...]))
    softplus = jnp.maximum(z, 0.0) + jnp.log(1.0 + jnp.exp(-jnp.abs(z)))
    w = -softplus - 0.5
    lw_out[...] = -jnp.exp(w)
    a = jax.nn.sigmoid(a0_ref[...] + _dot_f32(xa, a2_ref[...]))
    g_out[...] = _dot_f32(jax.nn.sigmoid(xg), g2_ref[...])
    if use_vres:
        xv = mixed(xv_ref, xvp_ref, 6)
        v = v + (vfirst_ref[...] - v) * jax.nn.sigmoid(v0_ref[...] + _dot_f32(xv, v2_ref[...]))
    kk = k * kk_ref[...]
    ss = _split_dot(kk * kk, gsum_ref[...], 3)
    kk = kk / jnp.maximum(jnp.sqrt(ss), 1e-12)
    r_out[...] = r
    k_out[...] = k * (1.0 + (a - 1.0) * ka_ref[...])
    v_out[...] = v
    an_out[...] = -kk
    b_out[...] = kk * a


def _rwkv_pre(proj, mix, w0, w2, a0, a2, v0, v2, g2, k_k, k_a, gsum, v_first, use_vres):
    t = proj.shape[0]
    tm = RW_PRE_TM

    def cur(width, off):
        return pl.BlockSpec((tm, width), lambda i: (i, off // width))

    def prev(width, off):
        return pl.BlockSpec((8, width), lambda i: (jnp.maximum(i * (tm // 8) - 1, 0), off // width))

    def full(a):
        return pl.BlockSpec(a.shape, lambda i: (0,) * a.ndim)

    segs = [(RW_W, C_RW_RKV), (RW_W, C_RW_RKV + RW_W), (RW_W, C_RW_RKV + 2 * RW_W),
            (128, C_RW_XW), (128, C_RW_XA), (256, C_RW_XG), (128, C_RW_XV)]
    params = [mix, w0, w2, a0, a2, v0, v2, g2, k_k, k_a, gsum]
    row = pl.BlockSpec((tm, RW_W), lambda i: (i, 0))
    outs = pl.pallas_call(
        functools.partial(_rwkv_pre_kernel, use_vres=use_vres),
        out_shape=[jax.ShapeDtypeStruct((t, RW_W), F32)] * 7,
        grid=(t // tm,),
        in_specs=[cur(w, o) for w, o in segs] + [prev(w, o) for w, o in segs] + [full(p) for p in params] + [row],
        out_specs=[row] * 7,
        compiler_params=_cparams(("parallel",)),
        name="rwkv_pre",
    )(*([proj] * 14), *params, v_first)
    return outs


RW_CHUNK = 128


def _wkv_kernel(r_ref, lw_ref, k_ref, v_ref, an_ref, b_ref, g_ref, lng_ref, lnb_ref, rk_ref,
                gsum_ref, tril_ref, o_ref, s_ref):
    c = pl.program_id(0)
    C = RW_CHUNK
    N = RW_N

    @pl.when(c == 0)
    def _():
        s_ref[...] = jnp.zeros_like(s_ref)

    r = r_ref[...]
    lw = lw_ref[...]
    k = k_ref[...]
    v = v_ref[...]
    lw_hi = lw.astype(BF16)
    rem = lw - lw_hi.astype(F32)
    lw_mid = rem.astype(BF16)
    lw_lo = (rem - lw_mid.astype(F32)).astype(BF16)
    tril = tril_ref[...]
    lg = _dot(tril, lw_hi) + (_dot(tril, lw_mid) + _dot(tril, lw_lo))
    gam = jnp.exp(lg)
    ginv = jnp.exp(-lg)
    at = (an_ref[...] * jnp.exp(lg - lw)).astype(BF16)
    bt = (b_ref[...] * ginv).astype(BF16)
    kt = (k * ginv).astype(BF16)
    rt_f = r * gam
    rt = rt_f.astype(BF16)
    vb = v.astype(BF16)
    g_last = gam[C - 1:C, :]

    rowi = lax.broadcasted_iota(jnp.int32, (C, C), 0)
    coli = lax.broadcasted_iota(jnp.int32, (C, C), 1)
    strict = rowi > coli
    incl = rowi >= coli

    heads = range(RW_HEADS)
    hs = [slice(h * N, (h + 1) * N) for h in heads]
    bth = [bt[:, s] for s in hs]
    kth = [kt[:, s] for s in hs]
    vh = [vb[:, s] for s in hs]
    big = [_dot_t(jnp.concatenate([at[:, s], rt[:, s]], axis=0), jnp.concatenate([bth[h], kth[h]], axis=0))
           for h, s in zip(heads, hs)]
    a_ab = [jnp.where(strict, big[h][:C, :C], 0.0) for h in heads]
    a_ak = [jnp.where(strict, big[h][:C, C:], 0.0).astype(BF16) for h in heads]
    a_rb = [jnp.where(incl, big[h][C:, :C], 0.0).astype(BF16) for h in heads]
    a_rk = [jnp.where(incl, big[h][C:, C:], 0.0).astype(BF16) for h in heads]
    akv = [_dot(a_ak[h], vh[h]) for h in heads]
    def same_block(size):
        shift = size.bit_length() - 1
        return (rowi >> shift) == (coli >> shift)

    tinv = [jnp.where(same_block(2), a_ab[h], 0.0) + jnp.where(rowi == coli, 1.0, 0.0) for h in heads]
    size = 2
    while size < C:
        lower_left = same_block(2 * size) & jnp.logical_not(same_block(size))
        off = [jnp.where(lower_left, a_ab[h], 0.0).astype(BF16) for h in heads]
        tb = [tinv[h].astype(BF16) for h in heads]
        half = [_dot(tb[h], off[h]).astype(BF16) for h in heads]
        tinv = [tinv[h] + _dot(half[h], tb[h]) for h in heads]
        size *= 2
    xb = [_dot(tinv[h].astype(BF16), jnp.concatenate([at[:, hs[h]], akv[h].astype(BF16)], axis=1)).astype(BF16)
          for h in heads]
    yx = [_dot(a_rb[h], xb[h]) for h in heads]
    ykv = [_dot(a_rk[h], vh[h]) for h in heads]
    xtb = [_dot_tl(xb[h], bth[h]) for h in heads]
    vtk = [_dot_tl(vh[h], kth[h]) for h in heads]
    ys = []
    for h in heads:
        gl = g_last[:, hs[h]]
        y1 = (rt_f[:, hs[h]] + yx[h][:, :N]).astype(BF16)
        s0 = s_ref[h]
        s0b = s0.astype(BF16)
        ys.append(_dot_t(y1, s0b) + (yx[h][:, N:] + ykv[h]))
        s_ref[h] = (s0 + _dot(s0b, xtb[h][:N].astype(BF16)) + (xtb[h][N:] + vtk[h])) * gl
    y = jnp.concatenate(ys, axis=1)

    gsum = gsum_ref[...]
    inv_n = 1.0 / N
    mu = _split_dot(y, gsum, 3) * inv_n
    yc = y - mu
    var = _split_dot(yc * yc, gsum, 3) * inv_n
    yn = yc * lax.rsqrt(var + RW_GN_EPS) * lng_ref[...] + lnb_ref[...]
    bonus = _split_dot(r * k * rk_ref[...], gsum, 3)
    o_ref[...] = ((yn + bonus * v) * g_ref[...]).astype(o_ref.dtype)


def _wkv(r, lw, k, v, an, b, g, ln_g, ln_b, r_k, gsum):
    t = r.shape[0]
    C = RW_CHUNK
    tril = jnp.asarray(np.tril(np.ones((C, C), np.float32)), BF16)
    row = pl.BlockSpec((C, RW_W), lambda c: (c, 0))
    vec = pl.BlockSpec((1, RW_W), lambda c: (0, 0))
    return pl.pallas_call(
        _wkv_kernel,
        out_shape=jax.ShapeDtypeStruct((t, RW_W), BF16),
        grid=(t // C,),
        in_specs=[row] * 7 + [vec] * 3 + [pl.BlockSpec((RW_W, RW_W), lambda c: (0, 0)),
                                          pl.BlockSpec((C, C), lambda c: (0, 0))],
        out_specs=row,
        scratch_shapes=[pltpu.VMEM((RW_HEADS, RW_N, RW_N), F32)],
        compiler_params=_cparams(("arbitrary",)),
        name="wkv7",
    )(r, lw, k, v, an, b, g, ln_g, ln_b, r_k, gsum, tril)


NSA_PREP_TM = 512


def _nsa_prep_kernel(*refs, n_slabs):
    x_refs = refs[:n_slabs]
    cos_ref, sin_ref, w_ref, gmean_ref, o_ref = refs[n_slabs:]
    cos = cos_ref[...]
    sin = sin_ref[...]
    gmean = gmean_ref[...]
    half = NSA_DH // 2
    first_half = (lax.broadcasted_iota(jnp.int32, cos.shape, 1) % NSA_DH) < half
    for s in range(n_slabs):
        x = x_refs[s][...]
        y = x * lax.rsqrt(_split_dot(x * x, gmean, 3) + RMS_EPS) * w_ref[s]
        rot = jnp.where(first_half, pltpu.roll(y, LANES - half, axis=1), pltpu.roll(y, half, axis=1))
        res = (y * cos + rot * sin).astype(o_ref.dtype)
        o_ref[2 * s] = res[:, :NSA_DH]
        o_ref[2 * s + 1] = res[:, NSA_DH:]


def _nsa_prep(proj, col_off, n_slabs, w_slabs, cos, sin, gmean):
    t = proj.shape[0]
    tm = NSA_PREP_TM
    base = col_off // LANES
    tab = pl.BlockSpec((tm, LANES), lambda i: (i, 0))
    return pl.pallas_call(
        functools.partial(_nsa_prep_kernel, n_slabs=n_slabs),
        out_shape=jax.ShapeDtypeStruct((2 * n_slabs, t, NSA_DH), BF16),
        grid=(t // tm,),
        in_specs=[pl.BlockSpec((tm, LANES), lambda i, s=s: (i, base + s)) for s in range(n_slabs)]
        + [tab, tab, pl.BlockSpec((n_slabs, 1, LANES), lambda i: (0, 0, 0)),
           pl.BlockSpec((LANES, LANES), lambda i: (0, 0))],
        out_specs=pl.BlockSpec((2 * n_slabs, tm, NSA_DH), lambda i: (0, i, 0)),
        compiler_params=_cparams(("parallel",)),
        name="nsa_prep",
    )(*([proj] * n_slabs), cos, sin, w_slabs, gmean)


def _compress_kernel(h_ref, w1_ref, w2_ref, pe_ref, nw_ref, cos_ref, sin_ref, o_ref):
    kind = pl.program_id(0)
    nc = h_ref.shape[2]
    half_in = CMP_STRIDE * NSA_DH
    hm = h_ref[0, 0]
    w1 = w1_ref[0]
    first = _dot(hm, w1[:half_in])
    second = _dot(hm, w1[half_in:])
    const = _dot(pe_ref[0], w1)[0:1, :]
    pre = first + pltpu.roll(second, nc - 1, axis=0) + const
    hid = 0.5 * pre * (1.0 + jnp.tanh(0.7978845608028654 * (pre + 0.044715 * pre * pre * pre)))
    out = _dot(hid.astype(BF16), w2_ref[0])
    y = out * lax.rsqrt(jnp.mean(out * out, axis=-1, keepdims=True) + RMS_EPS) * nw_ref[...]
    half = NSA_DH // 2
    rot = jnp.concatenate([y[:, half:], y[:, :half]], axis=1)
    roped = y * cos_ref[...] + rot * sin_ref[...]
    o_ref[0, 0] = jnp.where(kind == 0, roped, out).astype(o_ref.dtype)


def _compress(hmat, w1, w2, pe, nw, cos_c, sin_c):
    _, g, nc, width = hmat.shape
    return pl.pallas_call(
        _compress_kernel,
        out_shape=jax.ShapeDtypeStruct((2, g, nc, NSA_DH), BF16),
        grid=(2, g),
        in_specs=[pl.BlockSpec((1, 1, nc, width), lambda a, b: (a, b, 0, 0)),
                  pl.BlockSpec((1, 2 * width, CMP_HIDDEN), lambda a, b: (a, 0, 0)),
                  pl.BlockSpec((1, CMP_HIDDEN, NSA_DH), lambda a, b: (a, 0, 0)),
                  pl.BlockSpec((1, 8, 2 * width), lambda a, b: (a, 0, 0)),
                  pl.BlockSpec((1, NSA_DH), lambda a, b: (0, 0)),
                  pl.BlockSpec((nc, NSA_DH), lambda a, b: (0, 0)),
                  pl.BlockSpec((nc, NSA_DH), lambda a, b: (0, 0))],
        out_specs=pl.BlockSpec((1, 1, nc, NSA_DH), lambda a, b: (a, b, 0, 0)),
        compiler_params=_cparams(("parallel", "parallel")),
        name="nsa_compress",
    )(hmat, w1, w2, pe, nw, cos_c, sin_c)


SLC_KT = 1024
WIN_KEYS = WINDOW + Q_BLOCK


INT32_MIN = -2 ** 31


def _nsa_attn_kernel(q_ref, gate_ref, kc_ref, vc_ref, ks_ref, vs_ref, kw_ref, vw_ref, cis_ref, expand_ref, ltri_ref,
                     o_ref):
    qi = pl.program_id(1)
    nc = kc_ref.shape[2]
    nb = cis_ref.shape[1]
    heads = range(NSA_HG)
    start = qi * Q_BLOCK
    qh = [q_ref[hh] for hh in heads]
    tok = start + lax.broadcasted_iota(jnp.int32, (Q_BLOCK, 1), 0)

    def softmax_terms(scores):
        return [jnp.exp(s - jnp.max(s, axis=-1, keepdims=True)) for s in scores]

    kc = kc_ref[0, 0]
    vc = vc_ref[0, 0]
    cend = lax.broadcasted_iota(jnp.int32, (1, nc), 1) * CMP_STRIDE + (CMP_BLOCK - 1)
    bias_c = jnp.where(cend <= tok, 0.0, NEG_INF)
    ec = softmax_terms([_dot_t(qh[hh], kc) + bias_c for hh in heads])
    sees_any = tok >= CMP_BLOCK - 1
    pc = [ec[hh] * jnp.where(sees_any, 1.0 / jnp.sum(ec[hh], axis=-1, keepdims=True), 0.0) for hh in heads]
    o_c = [_dot(pc[hh].astype(BF16), vc) for hh in heads]

    psum = pc[0]
    for hh in range(1, NSA_HG):
        psum = psum + pc[hh]
    score = _split_dot(psum, cis_ref[...], 3)
    blk = lax.broadcasted_iota(jnp.int32, (1, nb), 1)
    cur = tok // SLC_BLOCK
    forced = (blk == 0) | (blk == cur) | (blk == cur - 1)
    score = jnp.where(forced, FORCED_SCORE, score)
    score = jnp.where(blk <= cur, score, -jnp.inf)

    kbase = pl.multiple_of(jnp.maximum(start - WINDOW, 0), Q_BLOCK)
    kw = kw_ref[0, pl.ds(kbase, WIN_KEYS), :]
    lag = tok - (kbase + lax.broadcasted_iota(jnp.int32, (1, WIN_KEYS), 1))
    bias_w = jnp.where((lag >= 0) & (lag < WINDOW), 0.0, NEG_INF)
    sw = [_dot_t(qh[hh], kw) for hh in heads]

    bits = lax.bitcast_convert_type(score.T, jnp.int32)
    key = bits ^ ((bits >> 31) & 0x7FFFFFFF)
    n_sel = min(SLC_TOPK, nb)

    def enough(c):
        return jnp.sum(jnp.where(key >= c, 1.0, 0.0), axis=0, keepdims=True) >= n_sel

    def enough3(c1, c2, c3):
        packed = jnp.where(key >= c3, 65793.0, jnp.where(key >= c2, 257.0, jnp.where(key >= c1, 1.0, 0.0)))
        tot = jnp.sum(packed, axis=0, keepdims=True).astype(jnp.int32)
        return (tot & 255) >= n_sel, ((tot >> 8) & 255) >= n_sel, (tot >> 16) >= n_sel

    zero_row = jnp.zeros((1, Q_BLOCK), jnp.int32)
    thr = jnp.where(enough(zero_row), zero_row, INT32_MIN)
    for hi in range(30, 0, -2):
        c1 = thr + (1 << (hi - 1))
        c2 = thr + (1 << hi)
        c3 = c2 + (1 << (hi - 1))
        e1, e2, e3 = enough3(c1, c2, c3)
        thr = jnp.where(e3, c3, jnp.where(e2, c2, jnp.where(e1, c1, thr)))
    c1 = thr + 1
    thr = jnp.where(enough(c1), c1, thr)
    above = key > thr
    tied = key == thr
    need = n_sel - jnp.sum(jnp.where(above, 1.0, 0.0), axis=0, keepdims=True)
    tied_before = _dot(ltri_ref[...], jnp.where(tied, 1.0, 0.0).astype(BF16))
    sel_t = above | (tied & (tied_before < need))
    sel = jnp.where(sel_t, 1.0, 0.0).T.astype(BF16)

    ew = softmax_terms([sw[hh] + bias_w for hh in heads])
    acc_w = [_dot(ew[hh].astype(BF16), vw_ref[0, pl.ds(kbase, WIN_KEYS), :]) for hh in heads]

    n_tiles = (start + Q_BLOCK + SLC_KT - 1) // SLC_KT

    def sel_body(j, carry):
        ms, accs = carry
        k0 = pl.multiple_of(j * SLC_KT, SLC_KT)
        kt = ks_ref[0, pl.ds(k0, SLC_KT), :]
        vt = vs_ref[0, pl.ds(k0, SLC_KT), :]
        member = _dot(sel, expand_ref[:, pl.ds(k0, SLC_KT)])
        kpos = k0 + lax.broadcasted_iota(jnp.int32, (1, SLC_KT), 1)
        bias = jnp.where((member > 0.5) & (kpos <= tok), 0.0, NEG_INF)
        sj = [_dot_t(qh[hh], kt) + bias for hh in heads]
        m_new = [jnp.maximum(ms[hh], jnp.max(sj[hh], axis=-1, keepdims=True)) for hh in heads]
        pj = [jnp.exp((sj[hh] - m_new[hh]).astype(BF16)) for hh in heads]
        accs = [jnp.exp(ms[hh] - m_new[hh]) * accs[hh] + _dot(pj[hh], vt) for hh in heads]
        return tuple(m_new), tuple(accs)

    m0 = tuple(jnp.full((Q_BLOCK, 1), NEG_INF, F32) for _ in heads)
    a0 = tuple(jnp.zeros((Q_BLOCK, LANES), F32) for _ in heads)
    _, acc_s = lax.fori_loop(0, n_tiles, sel_body, (m0, a0))

    gt = jax.nn.sigmoid(gate_ref[0])
    outs = []
    for hh in heads:
        g0, g1, g2 = (gt[:, 3 * hh + br:3 * hh + br + 1] for br in range(3))
        scale_s = g1 / acc_s[hh][:, NSA_DH:NSA_DH + 1]
        scale_w = g2 / acc_w[hh][:, NSA_DH:NSA_DH + 1]
        outs.append(g0 * o_c[hh] + scale_s * acc_s[hh][:, :NSA_DH] + scale_w * acc_w[hh][:, :NSA_DH])
    o_ref[...] = jnp.concatenate(outs, axis=1).astype(o_ref.dtype)


def _nsa_attn(qk_hm, gates, cmp_kv, vs1, vw1, cis, expand, ltri):
    t = qk_hm.shape[1]
    nq = t // Q_BLOCK
    nc = cmp_kv.shape[2]
    nb = cis.shape[1]
    ks_spec = pl.BlockSpec((1, t, NSA_DH), lambda g, i: (NSA_HEADS + g, 0, 0))
    kw_spec = pl.BlockSpec((1, t, NSA_DH), lambda g, i: (NSA_HEADS + NSA_G + g, 0, 0))
    vspec = pl.BlockSpec((1, t, LANES), lambda g, i: (g, 0, 0))
    return pl.pallas_call(
        _nsa_attn_kernel,
        out_shape=jax.ShapeDtypeStruct((t, NSA_W), BF16),
        grid=(NSA_G, nq),
        in_specs=[pl.BlockSpec((NSA_HG, Q_BLOCK, NSA_DH), lambda g, i: (g, i, 0)),
                  pl.BlockSpec((1, Q_BLOCK, LANES), lambda g, i: (g, i, 0)),
                  pl.BlockSpec((1, 1, nc, NSA_DH), lambda g, i: (0, g, 0, 0)),
                  pl.BlockSpec((1, 1, nc, NSA_DH), lambda g, i: (1, g, 0, 0)),
                  ks_spec, vspec, kw_spec, vspec,
                  pl.BlockSpec((nc, nb), lambda g, i: (0, 0)),
                  pl.BlockSpec((nb, t), lambda g, i: (0, 0)),
                  pl.BlockSpec((nb, nb), lambda g, i: (0, 0))],
        out_specs=pl.BlockSpec((Q_BLOCK, NSA_HG * NSA_DH), lambda g, i: (i, g)),
        compiler_params=_cparams(("parallel", "arbitrary")),
        name="nsa_attention",
    )(qk_hm, gates, cmp_kv, cmp_kv, qk_hm, vs1, qk_hm, vw1, cis, expand, ltri)


def _rope_tables(pos, dh, reps):
    half = dh // 2
    inv_freq = ROPE_THETA ** (-jnp.arange(half, dtype=F32) / half)
    ang = pos.astype(F32)[:, None] * inv_freq[None, :]
    cos = jnp.cos(ang)
    sin = jnp.sin(ang)
    cos_t = jnp.tile(jnp.concatenate([cos, cos], axis=1), (1, reps))
    sin_t = jnp.tile(jnp.concatenate([-sin, sin], axis=1), (1, reps))
    return cos_t, sin_t


def _pad_cols(a, width):
    return jnp.pad(a, ((0, 0), (0, width - a.shape[1])))


def _pad_rows(a, height):
    return jnp.pad(a, ((0, height - a.shape[0]), (0, 0)))


def _pack_w_in(w_l, w_vres):
    d = w_l.shape[0]
    o = 0
    ret = w_l[:, o:o + 4 * RET_W]; o += 4 * RET_W
    rkv = w_l[:, o:o + 3 * RW_W]; o += 3 * RW_W
    xw = w_l[:, o:o + RW_DECAY_RANK]; o += RW_DECAY_RANK
    xa = w_l[:, o:o + RW_A_RANK]; o += RW_A_RANK
    xg = w_l[:, o:o + RW_GATE_RANK]; o += RW_GATE_RANK
    q = w_l[:, o:o + NSA_W]; o += NSA_W
    kc, vc, ks, vs, kw, vw = (w_l[:, o + i * NSA_KV_W:o + (i + 1) * NSA_KV_W] for i in range(6))
    o += 6 * NSA_KV_W
    gates = w_l[:, o:o + 3 * NSA_HEADS]
    xv = jnp.zeros((d, LANES), w_l.dtype) if w_vres is None else _pad_cols(w_vres, LANES)
    return jnp.concatenate([ret, rkv, _pad_cols(xw, LANES), _pad_cols(xa, LANES), xg, xv,
                            q, ks, kw, kc, vc, vs, vw, _pad_cols(gates, LANES)], axis=1)


def _pack_mix(mix, vres_mix):
    o = 3 * RW_W
    xw = mix[o:o + RW_DECAY_RANK]; o += RW_DECAY_RANK
    xa = mix[o:o + RW_A_RANK]; o += RW_A_RANK
    xg = mix[o:o + RW_GATE_RANK]
    z = lambda n: jnp.zeros((n,), mix.dtype)
    xv = z(LANES) if vres_mix is None else jnp.concatenate([vres_mix, z(LANES - RW_V_RANK)])
    return jnp.concatenate([mix[:3 * RW_W], xw, z(LANES - RW_DECAY_RANK), xa, z(LANES - RW_A_RANK), xg, xv])[None, :]


def _head_major(a, g):
    t = a.shape[0]
    return a.reshape(t, g, NSA_DH).transpose(1, 0, 2)


def kernel(x, ln1_g, w_in, w_in_vres, rwkv_mix, rwkv_vres_mix, rwkv_w0, rwkv_w2, rwkv_a0, rwkv_a2, rwkv_v0, rwkv_v2, rwkv_g2, rwkv_k_k, rwkv_k_a, rwkv_r_k, rwkv_ln_g, rwkv_ln_b, nsa_q_norm, nsa_k_norm, nsa_cmp_pe, nsa_cmp_k_w1, nsa_cmp_k_w2, nsa_cmp_v_w1, nsa_cmp_v_w2, w_out, ln2_g, w_up, w_down):
    bsz, t, d = x.shape
    assert bsz == 1
    depth = w_in.shape[0]
    nc = t // CMP_STRIDE
    nb = t // SLC_BLOCK
    xs = x.reshape(t, d)

    pos = jnp.arange(t)
    cos_r, sin_r = _rope_tables(pos, RET_DH, 1)
    cos_n, sin_n = _rope_tables(pos, NSA_DH, 2)
    cos_c, sin_c = _rope_tables(jnp.arange(nc) * CMP_STRIDE + (CMP_BLOCK - 1), NSA_DH, 1)
    lane_head = np.arange(RW_W) // RW_N
    gsum = jnp.asarray(lane_head[:, None] == lane_head[None, :], BF16)
    lane_h2 = np.arange(LANES) // NSA_DH
    gmean = jnp.asarray((lane_h2[:, None] == lane_h2[None, :]) / float(NSA_DH), BF16)
    cstart = np.arange(nc) * CMP_STRIDE
    sstart = np.arange(nb) * SLC_BLOCK
    cis = jnp.asarray((cstart[:, None] <= sstart[None, :] + SLC_BLOCK - 1)
                      & (cstart[:, None] + CMP_BLOCK - 1 >= sstart[None, :]), BF16)

    expand = jnp.asarray((np.arange(t)[None, :] // SLC_BLOCK) == np.arange(nb)[:, None], BF16)
    ltri = jnp.asarray(np.tril(np.ones((nb, nb), np.float32), -1), BF16)

    def with_ones(v_hm):
        pad = jnp.zeros(v_hm.shape[:-1] + (LANES - NSA_DH,), v_hm.dtype).at[..., 0].set(1)
        return jnp.concatenate([v_hm, pad], axis=-1)

    v_first = jnp.zeros((t, RW_W), F32)
    for l in range(depth):
        vres = l > 0
        w_cat = _pack_w_in(w_in[l], w_in_vres[l - 1] if vres else None).astype(BF16)
        h = _rmsnorm(xs, ln1_g[l])
        proj = _matmul(h, w_cat, name="in_proj")

        o_ret = _retention(proj, cos_r, sin_r)

        row = lambda a: a.reshape(1, -1)
        mix = _pack_mix(rwkv_mix[l], rwkv_vres_mix[l - 1] if vres else None)
        v0 = row(rwkv_v0[l - 1]) if vres else jnp.zeros((1, RW_W), F32)
        v2 = _pad_rows(rwkv_v2[l - 1], LANES) if vres else jnp.zeros((LANES, RW_W), F32)
        r_, lw_, k_, v_, an_, b_, g_ = _rwkv_pre(
            proj, mix, row(rwkv_w0[l]), _pad_rows(rwkv_w2[l], LANES), row(rwkv_a0[l]), _pad_rows(rwkv_a2[l], LANES),
            v0, v2, rwkv_g2[l], row(rwkv_k_k[l]), row(rwkv_k_a[l]), gsum, v_first, vres)
        if not vres:
            v_first = v_
        o_rwkv = _wkv(r_, lw_, k_, v_, an_, b_, g_, row(rwkv_ln_g[l]), row(rwkv_ln_b[l]), row(rwkv_r_k[l]), gsum)

        n_q = NSA_W // LANES
        n_k = NSA_KV_W // LANES
        w_slabs = jnp.stack([jnp.tile(nsa_q_norm[l] * (NSA_DH ** -0.5), 2)] * n_q
                            + [jnp.tile(nsa_k_norm[l, 1], 2)] * n_k + [jnp.tile(nsa_k_norm[l, 2], 2)] * n_k)[:, None, :]
        qk_hm = _nsa_prep(proj, C_NSA_Q, n_q + 2 * n_k, w_slabs, cos_n, sin_n, gmean)
        raw = lambda off: _head_major(proj[:, off:off + NSA_KV_W], NSA_G).astype(BF16)
        hmat = jnp.stack([raw(C_NSA_KC), raw(C_NSA_VC)]).reshape(2, NSA_G, nc, CMP_STRIDE * NSA_DH)
        w1 = jnp.stack([nsa_cmp_k_w1[l], nsa_cmp_v_w1[l]]).astype(BF16)
        w2 = jnp.stack([nsa_cmp_k_w2[l], nsa_cmp_v_w2[l]]).astype(BF16)
        pe = jnp.broadcast_to(nsa_cmp_pe[l].reshape(2, 1, CMP_BLOCK * NSA_DH), (2, 8, CMP_BLOCK * NSA_DH)).astype(BF16)
        cmp_kv = _compress(hmat, w1, w2, pe, row(nsa_k_norm[l, 0]), cos_c, sin_c)
        gates = jnp.pad(proj[:, C_NSA_GATE:C_NSA_GATE + 3 * NSA_HEADS].reshape(t, NSA_G, 3 * NSA_HG).transpose(1, 0, 2),
                        ((0, 0), (0, 0), (0, LANES - 3 * NSA_HG)))
        o_nsa = _nsa_attn(qk_hm, gates, cmp_kv, with_ones(raw(C_NSA_VS)), with_ones(raw(C_NSA_VW)), cis, expand, ltri)

        mixed = jnp.concatenate([o_ret, o_rwkv, o_nsa], axis=1)
        xs = _matmul(mixed, w_out[l], res=xs, name="out_proj")

        h2 = _rmsnorm(xs, ln2_g[l])
        up = _matmul(h2, w_up[l], act="relu2", out_dtype=BF16, name="mlp_up")
        xs = _matmul(up, w_down[l], res=xs, name="mlp_down")
    return xs.reshape(bsz, t, d)
```

```python
import functools

import numpy as np
import jax
import jax.numpy as jnp
from jax import lax
from jax.experimental import pallas as pl
from jax.experimental.pallas import tpu as pltpu

F32 = jnp.float32
BF16 = jnp.bfloat16

D_MODEL = 2048
RET_HEADS, RET_DH = 4, 128
RET_W = RET_HEADS * RET_DH
RW_HEADS, RW_N = 8, 64
RW_W = RW_HEADS * RW_N
RW_DECAY_RANK, RW_A_RANK, RW_V_RANK, RW_GATE_RANK = 96, 96, 64, 256
RW_GN_EPS = 64e-5
NSA_HEADS, NSA_G, NSA_DH = 16, 4, 64
NSA_HG = NSA_HEADS // NSA_G
NSA_W = NSA_HEADS * NSA_DH
NSA_KV_W = NSA_G * NSA_DH
CMP_BLOCK, CMP_STRIDE, CMP_HIDDEN = 32, 16, 256
SLC_BLOCK, SLC_TOPK, WINDOW, Q_BLOCK = 64, 16, 512, 128
ROPE_THETA = 10000.0
MLP_HIDDEN = 4 * D_MODEL
RMS_EPS = 1e-6
NEG_INF = -1e30
FORCED_SCORE = 1e9

LANES = 128
VMEM_LIMIT = 56 * 1024 * 1024

C_RET = 0
C_RW_RKV = 2048
C_RW_XW = 3584
C_RW_XA = 3712
C_RW_XG = 3840
C_RW_XV = 4096
C_NSA_Q = 4224
C_NSA_KS = 5248
C_NSA_KW = 5504
C_NSA_KC = 5760
C_NSA_VC = 6016
C_NSA_VS = 6272
C_NSA_VW = 6528
C_NSA_GATE = 6784
PROJ_W = 6912


def _cparams(sem):
    return pltpu.CompilerParams(dimension_semantics=sem, vmem_limit_bytes=VMEM_LIMIT)


def _dot(a, b):
    return jnp.dot(a, b, preferred_element_type=F32)


def _dot_t(a, b):
    return lax.dot_general(a, b, (((1,), (1,)), ((), ())), preferred_element_type=F32)


def _dot_tl(a, b):
    return lax.dot_general(a, b, (((0,), (0,)), ((), ())), preferred_element_type=F32)


def _split_dot(x, m_bf16, passes):
    hi = x.astype(BF16)
    acc = _dot(hi, m_bf16)
    rem = x - hi.astype(F32)
    for _ in range(passes - 1):
        piece = rem.astype(BF16)
        acc = acc + _dot(piece, m_bf16)
        rem = rem - piece.astype(F32)
    return acc


def _rmsnorm_kernel(x_ref, g_ref, o_ref):
    x = x_ref[...]
    ms = jnp.mean(x * x, axis=-1, keepdims=True)
    o_ref[...] = (x * lax.rsqrt(ms + RMS_EPS) * g_ref[...]).astype(o_ref.dtype)


def _rmsnorm(x, g, tm=512):
    t, d = x.shape
    return pl.pallas_call(
        _rmsnorm_kernel,
        out_shape=jax.ShapeDtypeStruct((t, d), BF16),
        grid=(t // tm,),
        in_specs=[pl.BlockSpec((tm, d), lambda i: (i, 0)), pl.BlockSpec((1, d), lambda i: (0, 0))],
        out_specs=pl.BlockSpec((tm, d), lambda i: (i, 0)),
        compiler_params=_cparams(("parallel",)),
        name="rmsnorm",
    )(x, g.reshape(1, d))


def _mm_kernel(*refs, nk, act, has_res):
    if has_res:
        a_ref, b_ref, r_ref, o_ref, acc_ref = refs
    else:
        a_ref, b_ref, o_ref, acc_ref = refs
        r_ref = None
    k = pl.program_id(2)

    @pl.when(k == 0)
    def _():
        acc_ref[...] = jnp.zeros_like(acc_ref)

    acc_ref[...] += _dot(a_ref[...], b_ref[...].astype(BF16))

    @pl.when(k == nk - 1)
    def _():
        y = acc_ref[...]
        if act == "relu2":
            y = jnp.square(jnp.maximum(y, 0.0))
        if has_res:
            y = y + r_ref[...]
        o_ref[...] = y.astype(o_ref.dtype)


def _pick(n, cands):
    for c in cands:
        if n % c == 0:
            return c
    raise ValueError(f"no tile for {n}")


def _matmul(a, b, *, res=None, act=None, out_dtype=F32, name="matmul"):
    m, kd = a.shape
    _, n = b.shape
    tm = _pick(m, (1024, 512, 256, 128))
    tn = _pick(n, (1024, 768, 512, 256, 128))
    tk = _pick(kd, (2048, 1024, 512))
    nk = kd // tk
    in_specs = [pl.BlockSpec((tm, tk), lambda i, j, k: (i, k)), pl.BlockSpec((tk, tn), lambda i, j, k: (k, j))]
    args = [a, b]
    if res is not None:
        in_specs.append(pl.BlockSpec((tm, tn), lambda i, j, k: (i, j)))
        args.append(res)
    return pl.pallas_call(
        functools.partial(_mm_kernel, nk=nk, act=act, has_res=res is not None),
        out_shape=jax.ShapeDtypeStruct((m, n), out_dtype),
        grid=(m // tm, n // tn, nk),
        in_specs=in_specs,
        out_specs=pl.BlockSpec((tm, tn), lambda i, j, k: (i, j)),
        scratch_shapes=[pltpu.VMEM((tm, tn), F32)],
        compiler_params=_cparams(("parallel", "parallel", "arbitrary")),
        name=name,
    )(*args)


def _out_proj_kernel(ret_ref, rw_ref, nsa_ref, w_ref, res_ref, o_ref, acc_ref):
    k = pl.program_id(2)
    w = w_ref[...].astype(BF16)

    @pl.when(k == 0)
    def _():
        acc_ref[...] = _dot(ret_ref[...], w)

    @pl.when(k == 1)
    def _():
        acc_ref[...] += _dot(rw_ref[...], w)

    @pl.when(k >= 2)
    def _():
        acc_ref[...] += _dot(nsa_ref[...], w)

    @pl.when(k == pl.num_programs(2) - 1)
    def _():
        o_ref[...] = acc_ref[...] + res_ref[...]


def _out_proj(o_ret, o_rwkv, o_nsa, w, res):
    m = o_ret.shape[0]
    n = w.shape[1]
    tk = RET_W
    assert o_rwkv.shape[1] == tk and o_nsa.shape[1] % tk == 0 and w.shape[0] == 2 * tk + o_nsa.shape[1]
    nk = w.shape[0] // tk
    tm = _pick(m, (1024, 512, 256, 128))
    tn = _pick(n, (1024, 512, 256, 128))
    return pl.pallas_call(
        _out_proj_kernel,
        out_shape=jax.ShapeDtypeStruct((m, n), F32),
        grid=(m // tm, n // tn, nk),
        in_specs=[pl.BlockSpec((tm, tk), lambda i, j, k: (i, 0)),
                  pl.BlockSpec((tm, tk), lambda i, j, k: (i, 0)),
                  pl.BlockSpec((tm, tk), lambda i, j, k: (i, jnp.maximum(k - 2, 0))),
                  pl.BlockSpec((tk, tn), lambda i, j, k: (k, j)),
                  pl.BlockSpec((tm, tn), lambda i, j, k: (i, j))],
        out_specs=pl.BlockSpec((tm, tn), lambda i, j, k: (i, j)),
        scratch_shapes=[pltpu.VMEM((tm, tn), F32)],
        compiler_params=_cparams(("parallel", "parallel", "arbitrary")),
        name="out_proj",
    )(o_ret, o_rwkv, o_nsa, w, res)


RET_CHUNK = 256


def _retention_kernel(q_ref, k_ref, v_ref, g_ref, cos_ref, sin_ref, dmat_ref, qd_ref, kd_ref, cd_ref, o_ref, state_ref):
    @pl.when(pl.program_id(0) == 0)
    def _():
        state_ref[...] = jnp.zeros_like(state_ref)

    cos = cos_ref[...]
    sin = sin_ref[...]

    def rope(x):
        return x * cos + pltpu.roll(x, RET_DH // 2, axis=1) * sin

    heads = range(RET_HEADS)
    hs = [slice(h * RET_DH, (h + 1) * RET_DH) for h in heads]
    q_all, k_all, v_all, gate = q_ref[...], k_ref[...], v_ref[...], g_ref[...]
    qb = [rope(q_all[:, s]).astype(BF16) for s in hs]
    kf = [rope(k_all[:, s]) * (RET_DH ** -0.5) for s in hs]
    kb = [kf[h].astype(BF16) for h in heads]
    vb = [v_all[:, s].astype(BF16) for s in hs]
    inner = [(_dot_t(qb[h], kb[h]) * dmat_ref[h]).astype(BF16) for h in heads]
    cross = [_dot(qb[h], state_ref[h].astype(BF16)) * qd_ref[h] for h in heads]
    kv = [_dot_tl((kf[h] * kd_ref[h]).astype(BF16), vb[h]) for h in heads]
    outs = []
    for h in heads:
        out = _dot(inner[h], vb[h]) + cross[h]
        state_ref[h] = cd_ref[h] * state_ref[h] + kv[h]
        outs.append(out * lax.rsqrt(jnp.mean(out * out, axis=-1, keepdims=True) + RMS_EPS))
    o_ref[...] = (gate * jax.nn.sigmoid(gate) * jnp.concatenate(outs, axis=1)).astype(o_ref.dtype)


def _retention(proj, cos, sin):
    t = proj.shape[0]
    chunk = RET_CHUNK
    log_gamma = np.log(1.0 - 2.0 ** (-5.0 - np.arange(RET_HEADS, dtype=np.float64)))[:, None, None]
    n = np.arange(chunk, dtype=np.float64)
    lag = n[:, None] - n[None, :]
    dmat = np.where(lag >= 0, np.exp(np.maximum(lag, 0.0)[None] * log_gamma), 0.0)
    ones = np.ones((1, 1, RET_DH))
    qd = np.exp((n + 1.0)[None, :, None] * log_gamma) * ones
    kd = np.exp((chunk - 1.0 - n)[None, :, None] * log_gamma) * ones
    cd = np.exp(chunk * log_gamma) * ones
    tables = [jnp.asarray(a, F32) for a in (dmat, qd, kd, cd)]
    base = C_RET // RET_W

    def col(off):
        return pl.BlockSpec((chunk, RET_W), lambda c, off=off: (c, base + off))

    def full(a):
        return pl.BlockSpec(a.shape, lambda c: (0, 0, 0))

    tab = pl.BlockSpec((chunk, RET_DH), lambda c: (c, 0))
    return pl.pallas_call(
        _retention_kernel,
        out_shape=jax.ShapeDtypeStruct((t, RET_W), BF16),
        grid=(t // chunk,),
        in_specs=[col(0), col(1), col(2), col(3), tab, tab] + [full(a) for a in tables],
        out_specs=pl.BlockSpec((chunk, RET_W), lambda c: (c, 0)),
        scratch_shapes=[pltpu.VMEM((RET_HEADS, RET_DH, RET_DH), F32)],
        compiler_params=_cparams(("arbitrary",)),
        name="retention",
    )(proj, proj, proj, proj, cos, sin, *tables)


def _dot_f32(a, b):
    a_hi = a.astype(BF16)
    b_hi = b.astype(BF16)
    a_lo = (a - a_hi.astype(F32)).astype(BF16)
    b_lo = (b - b_hi.astype(F32)).astype(BF16)
    return _dot(a_hi, b_hi) + (_dot(a_hi, b_lo) + _dot(a_lo, b_hi))


RW_PRE_TM = 256
_MIX_OFF = (0, 512, 1024, 1536, 1664, 1792, 2048, 2176)


def _rwkv_pre_kernel(r_ref, k_ref, v_ref, xw_ref, xa_ref, xg_ref, xv_ref,
                     rp_ref, kp_ref, vp_ref, xwp_ref, xap_ref, xgp_ref, xvp_ref,
                     mix_ref, w0_ref, w2_ref, a0_ref, a2_ref, v0_ref, v2_ref, g2_ref, kk_ref, ka_ref,
                     gsum_ref, vfirst_ref,
                     r_out, lw_out, k_out, v_out, an_out, b_out, g_out, *, use_vres):
    i = pl.program_id(0)

    def mixed(cur_ref, prev_ref, seg):
        cur = cur_ref[...]
        mix = mix_ref[:, _MIX_OFF[seg]:_MIX_OFF[seg + 1]]
        prev_row = jnp.where(i > 0, prev_ref[7:8, :], 0.0)
        rid = lax.broadcasted_iota(jnp.int32, cur.shape, 0)
        shifted = jnp.where(rid == 0, prev_row, pltpu.roll(cur, 1, axis=0))
        return cur + mix * (shifted - cur)

    r = mixed(r_ref, rp_ref, 0)
    k = mixed(k_ref, kp_ref, 1)
    v = mixed(v_ref, vp_ref, 2)
    xw = mixed(xw_ref, xwp_ref, 3)
    xa = mixed(xa_ref, xap_ref, 4)
    xg = mixed(xg_ref, xgp_ref, 5)

    z = -(w0_ref[...] + _dot_f32(jnp.tanh(xw), w2_ref[...]))
    softplus = jnp.maximum(z, 0.0) + jnp.log(1.0 + jnp.exp(-jnp.abs(z)))
    w = -softplus - 0.5
    lw_out[...] = -jnp.exp(w)
    a = jax.nn.sigmoid(a0_ref[...] + _dot_f32(xa, a2_ref[...]))
    g_out[...] = _dot_f32(jax.nn.sigmoid(xg), g2_ref[...])
    if use_vres:
        xv = mixed(xv_ref, xvp_ref, 6)
        v = v + (vfirst_ref[...] - v) * jax.nn.sigmoid(v0_ref[...] + _dot_f32(xv, v2_ref[...]))
    kk = k * kk_ref[...]
    ss = _split_dot(kk * kk, gsum_ref[...], 3)
    kk = kk / jnp.maximum(jnp.sqrt(ss), 1e-12)
    r_out[...] = r
    k_out[...] = k * (1.0 + (a - 1.0) * ka_ref[...])
    v_out[...] = v
    an_out[...] = -kk
    b_out[...] = kk * a


def _rwkv_pre(proj, mix, w0, w2, a0, a2, v0, v2, g2, k_k, k_a, gsum, v_first, use_vres):
    t = proj.shape[0]
    tm = RW_PRE_TM

    def cur(width, off):
        return pl.BlockSpec((tm, width), lambda i: (i, off // width))

    def prev(width, off):
        return pl.BlockSpec((8, width), lambda i: (jnp.maximum(i * (tm // 8) - 1, 0), off // width))

    def full(a):
        return pl.BlockSpec(a.shape, lambda i: (0,) * a.ndim)

    segs = [(RW_W, C_RW_RKV), (RW_W, C_RW_RKV + RW_W), (RW_W, C_RW_RKV + 2 * RW_W),
            (128, C_RW_XW), (128, C_RW_XA), (256, C_RW_XG), (128, C_RW_XV)]
    params = [mix, w0, w2, a0, a2, v0, v2, g2, k_k, k_a, gsum]
    row = pl.BlockSpec((tm, RW_W), lambda i: (i, 0))
    outs = pl.pallas_call(
        functools.partial(_rwkv_pre_kernel, use_vres=use_vres),
        out_shape=[jax.ShapeDtypeStruct((t, RW_W), F32)] * 7,
        grid=(t // tm,),
        in_specs=[cur(w, o) for w, o in segs] + [prev(w, o) for w, o in segs] + [full(p) for p in params] + [row],
        out_specs=[row] * 7,
        compiler_params=_cparams(("parallel",)),
        name="rwkv_pre",
    )(*([proj] * 14), *params, v_first)
    return outs


RW_CHUNK = 128


def _wkv_kernel(r_ref, lw_ref, k_ref, v_ref, an_ref, b_ref, g_ref, lng_ref, lnb_ref, rk_ref,
                gsum_ref, tril_ref, o_ref, s_ref):
    c = pl.program_id(0)
    C = RW_CHUNK
    N = RW_N

    @pl.when(c == 0)
    def _():
        s_ref[...] = jnp.zeros_like(s_ref)

    r = r_ref[...]
    lw = lw_ref[...]
    k = k_ref[...]
    v = v_ref[...]
    lw_hi = lw.astype(BF16)
    rem = lw - lw_hi.astype(F32)
    lw_mid = rem.astype(BF16)
    lw_lo = (rem - lw_mid.astype(F32)).astype(BF16)
    tril = tril_ref[...]
    lg = _dot(tril, lw_hi) + (_dot(tril, lw_mid) + _dot(tril, lw_lo))
    gam = jnp.exp(lg)
    ginv = jnp.exp(-lg)
    at = (an_ref[...] * jnp.exp(lg - lw)).astype(BF16)
    bt = (b_ref[...] * ginv).astype(BF16)
    kt = (k * ginv).astype(BF16)
    rt_f = r * gam
    rt = rt_f.astype(BF16)
    vb = v.astype(BF16)
    g_last = gam[C - 1:C, :]

    rowi = lax.broadcasted_iota(jnp.int32, (C, C), 0)
    coli = lax.broadcasted_iota(jnp.int32, (C, C), 1)
    strict = rowi > coli
    incl = rowi >= coli

    heads = range(RW_HEADS)
    hs = [slice(h * N, (h + 1) * N) for h in heads]
    bth = [bt[:, s] for s in hs]
    kth = [kt[:, s] for s in hs]
    vh = [vb[:, s] for s in hs]
    big = [_dot_t(jnp.concatenate([at[:, s], rt[:, s]], axis=0), jnp.concatenate([bth[h], kth[h]], axis=0))
           for h, s in zip(heads, hs)]
    a_ab = [jnp.where(strict, big[h][:C, :C], 0.0) for h in heads]
    a_ak = [jnp.where(strict, big[h][:C, C:], 0.0).astype(BF16) for h in heads]
    a_rb = [jnp.where(incl, big[h][C:, :C], 0.0).astype(BF16) for h in heads]
    a_rk = [jnp.where(incl, big[h][C:, C:], 0.0).astype(BF16) for h in heads]
    akv = [_dot(a_ak[h], vh[h]) for h in heads]
    def same_block(size):
        shift = size.bit_length() - 1
        return (rowi >> shift) == (coli >> shift)

    tinv = [jnp.where(same_block(2), a_ab[h], 0.0) + jnp.where(rowi == coli, 1.0, 0.0) for h in heads]
    size = 2
    while size < C:
        lower_left = same_block(2 * size) & jnp.logical_not(same_block(size))
        off = [jnp.where(lower_left, a_ab[h], 0.0).astype(BF16) for h in heads]
        tb = [tinv[h].astype(BF16) for h in heads]
        half = [_dot(tb[h], off[h]).astype(BF16) for h in heads]
        tinv = [tinv[h] + _dot(half[h], tb[h]) for h in heads]
        size *= 2
    xb = [_dot(tinv[h].astype(BF16), jnp.concatenate([at[:, hs[h]], akv[h].astype(BF16)], axis=1)).astype(BF16)
          for h in heads]
    yx = [_dot(a_rb[h], xb[h]) for h in heads]
    ykv = [_dot(a_rk[h], vh[h]) for h in heads]
    xtb = [_dot_tl(xb[h], bth[h]) for h in heads]
    vtk = [_dot_tl(vh[h], kth[h]) for h in heads]
    ys = []
    for h in heads:
        gl = g_last[:, hs[h]]
        y1 = (rt_f[:, hs[h]] + yx[h][:, :N]).astype(BF16)
        s0 = s_ref[h]
        s0b = s0.astype(BF16)
        ys.append(_dot_t(y1, s0b) + (yx[h][:, N:] + ykv[h]))
        s_ref[h] = (s0 + _dot(s0b, xtb[h][:N].astype(BF16)) + (xtb[h][N:] + vtk[h])) * gl
    y = jnp.concatenate(ys, axis=1)

    gsum = gsum_ref[...]
    inv_n = 1.0 / N
    mu = _split_dot(y, gsum, 3) * inv_n
    yc = y - mu
    var = _split_dot(yc * yc, gsum, 3) * inv_n
    yn = yc * lax.rsqrt(var + RW_GN_EPS) * lng_ref[...] + lnb_ref[...]
    bonus = _split_dot(r * k * rk_ref[...], gsum, 3)
    o_ref[...] = ((yn + bonus * v) * g_ref[...]).astype(o_ref.dtype)


def _wkv(r, lw, k, v, an, b, g, ln_g, ln_b, r_k, gsum):
    t = r.shape[0]
    C = RW_CHUNK
    tril = jnp.asarray(np.tril(np.ones((C, C), np.float32)), BF16)
    row = pl.BlockSpec((C, RW_W), lambda c: (c, 0))
    vec = pl.BlockSpec((1, RW_W), lambda c: (0, 0))
    return pl.pallas_call(
        _wkv_kernel,
        out_shape=jax.ShapeDtypeStruct((t, RW_W), BF16),
        grid=(t // C,),
        in_specs=[row] * 7 + [vec] * 3 + [pl.BlockSpec((RW_W, RW_W), lambda c: (0, 0)),
                                          pl.BlockSpec((C, C), lambda c: (0, 0))],
        out_specs=row,
        scratch_shapes=[pltpu.VMEM((RW_HEADS, RW_N, RW_N), F32)],
        compiler_params=_cparams(("arbitrary",)),
        name="wkv7",
    )(r, lw, k, v, an, b, g, ln_g, ln_b, r_k, gsum, tril)


NSA_PREP_TM = 512


N_NORM_SLABS = (NSA_W + 2 * NSA_KV_W) // LANES
N_RAW_SLABS = 4 * NSA_KV_W // LANES
N_NSA_SLABS = N_NORM_SLABS + N_RAW_SLABS + 1


def _nsa_prep_kernel(*refs):
    x_refs = refs[:N_NSA_SLABS]
    cos_ref, sin_ref, w_ref, gmean_ref, qk_ref, hmat_ref, v1_ref, gate_ref = refs[N_NSA_SLABS:]
    cos = cos_ref[...]
    sin = sin_ref[...]
    gmean = gmean_ref[...]
    half = NSA_DH // 2
    lane = lax.broadcasted_iota(jnp.int32, cos.shape, 1)
    first_half = (lane % NSA_DH) < half
    low = lane < NSA_DH
    for s in range(N_NORM_SLABS):
        x = x_refs[s][...]
        y = x * lax.rsqrt(_split_dot(x * x, gmean, 3) + RMS_EPS) * w_ref[s]
        rot = jnp.where(first_half, pltpu.roll(y, LANES - half, axis=1), pltpu.roll(y, half, axis=1))
        res = (y * cos + rot * sin).astype(qk_ref.dtype)
        qk_ref[2 * s] = res[:, :NSA_DH]
        qk_ref[2 * s + 1] = res[:, NSA_DH:]

    rows = hmat_ref.shape[2]
    low_r = lax.broadcasted_iota(jnp.int32, (rows, LANES), 1) < NSA_DH
    for s in range(NSA_KV_W // LANES * 2):
        kind, pair = divmod(s, NSA_KV_W // LANES)
        x_ref = x_refs[N_NORM_SLABS + s]
        for t2 in range(CMP_STRIDE // 2):
            r0 = x_ref[pl.ds(2 * t2, rows, stride=CMP_STRIDE), :]
            r1 = x_ref[pl.ds(2 * t2 + 1, rows, stride=CMP_STRIDE), :]
            cols = slice(t2 * LANES, (t2 + 1) * LANES)
            hmat_ref[kind, 2 * pair, :, cols] = jnp.where(low_r, r0, pltpu.roll(r1, NSA_DH, axis=1)).astype(BF16)
            hmat_ref[kind, 2 * pair + 1, :, cols] = jnp.where(low_r, pltpu.roll(r0, NSA_DH, axis=1), r1).astype(BF16)

    one_col = jnp.where(lane == NSA_DH, 1.0, 0.0)
    for s in range(NSA_KV_W // LANES * 2):
        x = x_refs[N_NORM_SLABS + N_RAW_SLABS // 2 + s][...]
        v1_ref[2 * s] = jnp.where(low, x, one_col).astype(BF16)
        v1_ref[2 * s + 1] = jnp.where(low, pltpu.roll(x, NSA_DH, axis=1), one_col).astype(BF16)

    xg = x_refs[N_NSA_SLABS - 1][...]
    per_group = 3 * NSA_HG
    for g in range(NSA_G):
        shifted = xg if g == 0 else pltpu.roll(xg, LANES - per_group * g, axis=1)
        gate_ref[g] = jnp.where(lane < per_group, shifted, 0.0)


def _nsa_prep(proj, w_slabs, cos, sin, gmean):
    t = proj.shape[0]
    tm = NSA_PREP_TM
    base = C_NSA_Q // LANES
    tab = pl.BlockSpec((tm, LANES), lambda i: (i, 0))
    n_heads_out = 2 * N_NORM_SLABS
    return pl.pallas_call(
        _nsa_prep_kernel,
        out_shape=[jax.ShapeDtypeStruct((n_heads_out, t, NSA_DH), BF16),
                   jax.ShapeDtypeStruct((2, NSA_G, t // CMP_STRIDE, CMP_STRIDE * NSA_DH), BF16),
                   jax.ShapeDtypeStruct((2 * NSA_G, t, LANES), BF16),
                   jax.ShapeDtypeStruct((NSA_G, t, LANES), F32)],
        grid=(t // tm,),
        in_specs=[pl.BlockSpec((tm, LANES), lambda i, s=s: (i, base + s)) for s in range(N_NSA_SLABS)]
        + [tab, tab, pl.BlockSpec((N_NORM_SLABS, 1, LANES), lambda i: (0, 0, 0)),
           pl.BlockSpec((LANES, LANES), lambda i: (0, 0))],
        out_specs=[pl.BlockSpec((n_heads_out, tm, NSA_DH), lambda i: (0, i, 0)),
                   pl.BlockSpec((2, NSA_G, tm // CMP_STRIDE, CMP_STRIDE * NSA_DH), lambda i: (0, 0, i, 0)),
                   pl.BlockSpec((2 * NSA_G, tm, LANES), lambda i: (0, i, 0)),
                   pl.BlockSpec((NSA_G, tm, LANES), lambda i: (0, i, 0))],
        compiler_params=_cparams(("parallel",)),
        name="nsa_prep",
    )(*([proj] * N_NSA_SLABS), cos, sin, w_slabs, gmean)


def _compress_kernel(h_ref, w1_ref, w2_ref, pe_ref, nw_ref, cos_ref, sin_ref, o_ref):
    kind = pl.program_id(0)
    nc = h_ref.shape[2]
    half_in = CMP_STRIDE * NSA_DH
    hm = h_ref[0, 0]
    w1 = w1_ref[0]
    first = _dot(hm, w1[:half_in])
    second = _dot(hm, w1[half_in:])
    const = _dot(pe_ref[0], w1)[0:1, :]
    pre = first + pltpu.roll(second, nc - 1, axis=0) + const
    hid = 0.5 * pre * (1.0 + jnp.tanh(0.7978845608028654 * (pre + 0.044715 * pre * pre * pre)))
    out = _dot(hid.astype(BF16), w2_ref[0])
    y = out * lax.rsqrt(jnp.mean(out * out, axis=-1, keepdims=True) + RMS_EPS) * nw_ref[...]
    half = NSA_DH // 2
    rot = jnp.concatenate([y[:, half:], y[:, :half]], axis=1)
    roped = y * cos_ref[...] + rot * sin_ref[...]
    o_ref[0, 0] = jnp.where(kind == 0, roped, out).astype(o_ref.dtype)


def _compress(hmat, w1, w2, pe, nw, cos_c, sin_c):
    _, g, nc, width = hmat.shape
    return pl.pallas_call(
        _compress_kernel,
        out_shape=jax.ShapeDtypeStruct((2, g, nc, NSA_DH), BF16),
        grid=(2, g),
        in_specs=[pl.BlockSpec((1, 1, nc, width), lambda a, b: (a, b, 0, 0)),
                  pl.BlockSpec((1, 2 * width, CMP_HIDDEN), lambda a, b: (a, 0, 0)),
                  pl.BlockSpec((1, CMP_HIDDEN, NSA_DH), lambda a, b: (a, 0, 0)),
                  pl.BlockSpec((1, 8, 2 * width), lambda a, b: (a, 0, 0)),
                  pl.BlockSpec((1, NSA_DH), lambda a, b: (0, 0)),
                  pl.BlockSpec((nc, NSA_DH), lambda a, b: (0, 0)),
                  pl.BlockSpec((nc, NSA_DH), lambda a, b: (0, 0))],
        out_specs=pl.BlockSpec((1, 1, nc, NSA_DH), lambda a, b: (a, b, 0, 0)),
        compiler_params=_cparams(("parallel", "parallel")),
        name="nsa_compress",
    )(hmat, w1, w2, pe, nw, cos_c, sin_c)


SLC_KT = 1024
WIN_KEYS = WINDOW + Q_BLOCK


INT32_MIN = -2 ** 31


def _nsa_attn_kernel(q_ref, gate_ref, kc_ref, vc_ref, ks_ref, vs_ref, kw_ref, vw_ref, cis_ref, expand_ref, ltri_ref,
                     o_ref):
    qi = pl.program_id(1)
    nc = kc_ref.shape[2]
    nb = cis_ref.shape[1]
    heads = range(NSA_HG)
    start = qi * Q_BLOCK
    qh = [q_ref[hh] for hh in heads]
    tok = start + lax.broadcasted_iota(jnp.int32, (Q_BLOCK, 1), 0)

    def softmax_terms(scores):
        return [jnp.exp(s - jnp.max(s, axis=-1, keepdims=True)) for s in scores]

    kc = kc_ref[0, 0]
    vc = vc_ref[0, 0]
    cend = lax.broadcasted_iota(jnp.int32, (1, nc), 1) * CMP_STRIDE + (CMP_BLOCK - 1)
    bias_c = jnp.where(cend <= tok, 0.0, NEG_INF)
    ec = softmax_terms([_dot_t(qh[hh], kc) + bias_c for hh in heads])
    sees_any = tok >= CMP_BLOCK - 1
    pc = [ec[hh] * jnp.where(sees_any, 1.0 / jnp.sum(ec[hh], axis=-1, keepdims=True), 0.0) for hh in heads]
    o_c = [_dot(pc[hh].astype(BF16), vc) for hh in heads]

    psum = pc[0]
    for hh in range(1, NSA_HG):
        psum = psum + pc[hh]
    score = _split_dot(psum, cis_ref[...], 3)
    blk = lax.broadcasted_iota(jnp.int32, (1, nb), 1)
    cur = tok // SLC_BLOCK
    forced = (blk == 0) | (blk == cur) | (blk == cur - 1)
    score = jnp.where(forced, FORCED_SCORE, score)
    score = jnp.where(blk <= cur, score, -jnp.inf)

    kbase = pl.multiple_of(jnp.maximum(start - WINDOW, 0), Q_BLOCK)
    kw = kw_ref[0, pl.ds(kbase, WIN_KEYS), :]
    lag = tok - (kbase + lax.broadcasted_iota(jnp.int32, (1, WIN_KEYS), 1))
    bias_w = jnp.where((lag >= 0) & (lag < WINDOW), 0.0, NEG_INF)
    sw = [_dot_t(qh[hh], kw) for hh in heads]

    bits = lax.bitcast_convert_type(score.T, jnp.int32)
    key = bits ^ ((bits >> 31) & 0x7FFFFFFF)
    n_sel = min(SLC_TOPK, nb)

    def enough(c):
        return jnp.sum(jnp.where(key >= c, 1.0, 0.0), axis=0, keepdims=True) >= n_sel

    def enough3(c1, c2, c3):
        packed = jnp.where(key >= c3, 65793.0, jnp.where(key >= c2, 257.0, jnp.where(key >= c1, 1.0, 0.0)))
        tot = jnp.sum(packed, axis=0, keepdims=True).astype(jnp.int32)
        return (tot & 255) >= n_sel, ((tot >> 8) & 255) >= n_sel, (tot >> 16) >= n_sel

    zero_row = jnp.zeros((1, Q_BLOCK), jnp.int32)
    thr = jnp.where(enough(zero_row), zero_row, INT32_MIN)
    for hi in range(30, 0, -2):
        c1 = thr + (1 << (hi - 1))
        c2 = thr + (1 << hi)
        c3 = c2 + (1 << (hi - 1))
        e1, e2, e3 = enough3(c1, c2, c3)
        thr = jnp.where(e3, c3, jnp.where(e2, c2, jnp.where(e1, c1, thr)))
    c1 = thr + 1
    thr = jnp.where(enough(c1), c1, thr)
    above = key > thr
    tied = key == thr
    need = n_sel - jnp.sum(jnp.where(above, 1.0, 0.0), axis=0, keepdims=True)
    tied_before = _dot(ltri_ref[...], jnp.where(tied, 1.0, 0.0).astype(BF16))
    sel_t = above | (tied & (tied_before < need))
    sel = jnp.where(sel_t, 1.0, 0.0).T.astype(BF16)

    ew = softmax_terms([sw[hh] + bias_w for hh in heads])
    acc_w = [_dot(ew[hh].astype(BF16), vw_ref[0, pl.ds(kbase, WIN_KEYS), :]) for hh in heads]

    n_tiles = (start + Q_BLOCK + SLC_KT - 1) // SLC_KT

    def sel_body(j, carry):
        ms, accs = carry
        k0 = pl.multiple_of(j * SLC_KT, SLC_KT)
        kt = ks_ref[0, pl.ds(k0, SLC_KT), :]
        vt = vs_ref[0, pl.ds(k0, SLC_KT), :]
        member = _dot(sel, expand_ref[:, pl.ds(k0, SLC_KT)])
        kpos = k0 + lax.broadcasted_iota(jnp.int32, (1, SLC_KT), 1)
        bias = jnp.where((member > 0.5) & (kpos <= tok), 0.0, NEG_INF)
        sj = [_dot_t(qh[hh], kt) + bias for hh in heads]
        m_new = [jnp.maximum(ms[hh], jnp.max(sj[hh], axis=-1, keepdims=True)) for hh in heads]
        pj = [jnp.exp((sj[hh] - m_new[hh]).astype(BF16)) for hh in heads]
        accs = [jnp.exp(ms[hh] - m_new[hh]) * accs[hh] + _dot(pj[hh], vt) for hh in heads]
        return tuple(m_new), tuple(accs)

    m0 = tuple(jnp.full((Q_BLOCK, 1), NEG_INF, F32) for _ in heads)
    a0 = tuple(jnp.zeros((Q_BLOCK, LANES), F32) for _ in heads)
    _, acc_s = lax.fori_loop(0, n_tiles, sel_body, (m0, a0))

    gt = jax.nn.sigmoid(gate_ref[0])
    outs = []
    for hh in heads:
        g0, g1, g2 = (gt[:, 3 * hh + br:3 * hh + br + 1] for br in range(3))
        scale_s = g1 / acc_s[hh][:, NSA_DH:NSA_DH + 1]
        scale_w = g2 / acc_w[hh][:, NSA_DH:NSA_DH + 1]
        outs.append(g0 * o_c[hh] + scale_s * acc_s[hh][:, :NSA_DH] + scale_w * acc_w[hh][:, :NSA_DH])
    o_ref[...] = jnp.concatenate(outs, axis=1).astype(o_ref.dtype)


def _nsa_attn(qk_hm, gates, cmp_kv, v1, cis, expand, ltri):
    t = qk_hm.shape[1]
    nq = t // Q_BLOCK
    nc = cmp_kv.shape[2]
    nb = cis.shape[1]
    ks_spec = pl.BlockSpec((1, t, NSA_DH), lambda g, i: (NSA_HEADS + g, 0, 0))
    kw_spec = pl.BlockSpec((1, t, NSA_DH), lambda g, i: (NSA_HEADS + NSA_G + g, 0, 0))
    vs_spec = pl.BlockSpec((1, t, LANES), lambda g, i: (g, 0, 0))
    vw_spec = pl.BlockSpec((1, t, LANES), lambda g, i: (NSA_G + g, 0, 0))
    return pl.pallas_call(
        _nsa_attn_kernel,
        out_shape=jax.ShapeDtypeStruct((t, NSA_W), BF16),
        grid=(NSA_G, nq),
        in_specs=[pl.BlockSpec((NSA_HG, Q_BLOCK, NSA_DH), lambda g, i: (g, i, 0)),
                  pl.BlockSpec((1, Q_BLOCK, LANES), lambda g, i: (g, i, 0)),
                  pl.BlockSpec((1, 1, nc, NSA_DH), lambda g, i: (0, g, 0, 0)),
                  pl.BlockSpec((1, 1, nc, NSA_DH), lambda g, i: (1, g, 0, 0)),
                  ks_spec, vs_spec, kw_spec, vw_spec,
                  pl.BlockSpec((nc, nb), lambda g, i: (0, 0)),
                  pl.BlockSpec((nb, t), lambda g, i: (0, 0)),
                  pl.BlockSpec((nb, nb), lambda g, i: (0, 0))],
        out_specs=pl.BlockSpec((Q_BLOCK, NSA_HG * NSA_DH), lambda g, i: (i, g)),
        compiler_params=_cparams(("parallel", "arbitrary")),
        name="nsa_attention",
    )(qk_hm, gates, cmp_kv, cmp_kv, qk_hm, v1, qk_hm, v1, cis, expand, ltri)


def _rope_tables(pos, dh, reps):
    half = dh // 2
    inv_freq = ROPE_THETA ** (-jnp.arange(half, dtype=F32) / half)
    ang = pos.astype(F32)[:, None] * inv_freq[None, :]
    cos = jnp.cos(ang)
    sin = jnp.sin(ang)
    cos_t = jnp.tile(jnp.concatenate([cos, cos], axis=1), (1, reps))
    sin_t = jnp.tile(jnp.concatenate([-sin, sin], axis=1), (1, reps))
    return cos_t, sin_t


def _pad_cols(a, width):
    return jnp.pad(a, ((0, 0), (0, width - a.shape[1])))


def _pad_rows(a, height):
    return jnp.pad(a, ((0, height - a.shape[0]), (0, 0)))


def _pack_w_in(w_l, w_vres):
    d = w_l.shape[0]
    o = 0
    ret = w_l[:, o:o + 4 * RET_W]; o += 4 * RET_W
    rkv = w_l[:, o:o + 3 * RW_W]; o += 3 * RW_W
    xw = w_l[:, o:o + RW_DECAY_RANK]; o += RW_DECAY_RANK
    xa = w_l[:, o:o + RW_A_RANK]; o += RW_A_RANK
    xg = w_l[:, o:o + RW_GATE_RANK]; o += RW_GATE_RANK
    q = w_l[:, o:o + NSA_W]; o += NSA_W
    kc, vc, ks, vs, kw, vw = (w_l[:, o + i * NSA_KV_W:o + (i + 1) * NSA_KV_W] for i in range(6))
    o += 6 * NSA_KV_W
    gates = w_l[:, o:o + 3 * NSA_HEADS]
    xv = jnp.zeros((d, LANES), w_l.dtype) if w_vres is None else _pad_cols(w_vres, LANES)
    return jnp.concatenate([ret, rkv, _pad_cols(xw, LANES), _pad_cols(xa, LANES), xg, xv,
                            q, ks, kw, kc, vc, vs, vw, _pad_cols(gates, LANES)], axis=1)


def _pack_mix(mix, vres_mix):
    o = 3 * RW_W
    xw = mix[o:o + RW_DECAY_RANK]; o += RW_DECAY_RANK
    xa = mix[o:o + RW_A_RANK]; o += RW_A_RANK
    xg = mix[o:o + RW_GATE_RANK]
    z = lambda n: jnp.zeros((n,), mix.dtype)
    xv = z(LANES) if vres_mix is None else jnp.concatenate([vres_mix, z(LANES - RW_V_RANK)])
    return jnp.concatenate([mix[:3 * RW_W], xw, z(LANES - RW_DECAY_RANK), xa, z(LANES - RW_A_RANK), xg, xv])[None, :]


def _head_major(a, g):
    t = a.shape[0]
    return a.reshape(t, g, NSA_DH).transpose(1, 0, 2)


def kernel(x, ln1_g, w_in, w_in_vres, rwkv_mix, rwkv_vres_mix, rwkv_w0, rwkv_w2, rwkv_a0, rwkv_a2, rwkv_v0, rwkv_v2, rwkv_g2, rwkv_k_k, rwkv_k_a, rwkv_r_k, rwkv_ln_g, rwkv_ln_b, nsa_q_norm, nsa_k_norm, nsa_cmp_pe, nsa_cmp_k_w1, nsa_cmp_k_w2, nsa_cmp_v_w1, nsa_cmp_v_w2, w_out, ln2_g, w_up, w_down):
    bsz, t, d = x.shape
    assert bsz == 1
    depth = w_in.shape[0]
    nc = t // CMP_STRIDE
    nb = t // SLC_BLOCK
    xs = x.reshape(t, d)

    pos = jnp.arange(t)
    cos_r, sin_r = _rope_tables(pos, RET_DH, 1)
    cos_n, sin_n = _rope_tables(pos, NSA_DH, 2)
    cos_c, sin_c = _rope_tables(jnp.arange(nc) * CMP_STRIDE + (CMP_BLOCK - 1), NSA_DH, 1)
    lane_head = np.arange(RW_W) // RW_N
    gsum = jnp.asarray(lane_head[:, None] == lane_head[None, :], BF16)
    lane_h2 = np.arange(LANES) // NSA_DH
    gmean = jnp.asarray((lane_h2[:, None] == lane_h2[None, :]) / float(NSA_DH), BF16)
    cstart = np.arange(nc) * CMP_STRIDE
    sstart = np.arange(nb) * SLC_BLOCK
    cis = jnp.asarray((cstart[:, None] <= sstart[None, :] + SLC_BLOCK - 1)
                      & (cstart[:, None] + CMP_BLOCK - 1 >= sstart[None, :]), BF16)

    expand = jnp.asarray((np.arange(t)[None, :] // SLC_BLOCK) == np.arange(nb)[:, None], BF16)
    ltri = jnp.asarray(np.tril(np.ones((nb, nb), np.float32), -1), BF16)

    v_first = jnp.zeros((t, RW_W), F32)
    for l in range(depth):
        vres = l > 0
        w_cat = _pack_w_in(w_in[l], w_in_vres[l - 1] if vres else None).astype(BF16)
        h = _rmsnorm(xs, ln1_g[l])
        proj = _matmul(h, w_cat, name="in_proj")

        o_ret = _retention(proj, cos_r, sin_r)

        row = lambda a: a.reshape(1, -1)
        mix = _pack_mix(rwkv_mix[l], rwkv_vres_mix[l - 1] if vres else None)
        v0 = row(rwkv_v0[l - 1]) if vres else jnp.zeros((1, RW_W), F32)
        v2 = _pad_rows(rwkv_v2[l - 1], LANES) if vres else jnp.zeros((LANES, RW_W), F32)
        r_, lw_, k_, v_, an_, b_, g_ = _rwkv_pre(
            proj, mix, row(rwkv_w0[l]), _pad_rows(rwkv_w2[l], LANES), row(rwkv_a0[l]), _pad_rows(rwkv_a2[l], LANES),
            v0, v2, rwkv_g2[l], row(rwkv_k_k[l]), row(rwkv_k_a[l]), gsum, v_first, vres)
        if not vres:
            v_first = v_
        o_rwkv = _wkv(r_, lw_, k_, v_, an_, b_, g_, row(rwkv_ln_g[l]), row(rwkv_ln_b[l]), row(rwkv_r_k[l]), gsum)

        n_q = NSA_W // LANES
        n_k = NSA_KV_W // LANES
        w_slabs = jnp.stack([jnp.tile(nsa_q_norm[l] * (NSA_DH ** -0.5), 2)] * n_q
                            + [jnp.tile(nsa_k_norm[l, 1], 2)] * n_k + [jnp.tile(nsa_k_norm[l, 2], 2)] * n_k)[:, None, :]
        qk_hm, hmat, v1, gates = _nsa_prep(proj, w_slabs, cos_n, sin_n, gmean)
        w1 = jnp.stack([nsa_cmp_k_w1[l], nsa_cmp_v_w1[l]]).astype(BF16)
        w2 = jnp.stack([nsa_cmp_k_w2[l], nsa_cmp_v_w2[l]]).astype(BF16)
        pe = jnp.broadcast_to(nsa_cmp_pe[l].reshape(2, 1, CMP_BLOCK * NSA_DH), (2, 8, CMP_BLOCK * NSA_DH)).astype(BF16)
        cmp_kv = _compress(hmat, w1, w2, pe, row(nsa_k_norm[l, 0]), cos_c, sin_c)
        o_nsa = _nsa_attn(qk_hm, gates, cmp_kv, v1, cis, expand, ltri)

        xs = _out_proj(o_ret, o_rwkv, o_nsa, w_out[l], xs)

        h2 = _rmsnorm(xs, ln2_g[l])
        up = _matmul(h2, w_up[l], act="relu2", out_dtype=BF16, name="mlp_up")
        xs = _matmul(up, w_down[l], res=xs, name="mlp_down")
    return xs.reshape(bsz, t, d)
```

```python
import functools

import numpy as np
import jax
import jax.numpy as jnp
from jax import lax
from jax.experimental import pallas as pl
from jax.experimental.pallas import tpu as pltpu

F32 = jnp.float32
BF16 = jnp.bfloat16

D_MODEL = 2048
RET_HEADS, RET_DH = 4, 128
RET_W = RET_HEADS * RET_DH
RW_HEADS, RW_N = 8, 64
RW_W = RW_HEADS * RW_N
RW_DECAY_RANK, RW_A_RANK, RW_V_RANK, RW_GATE_RANK = 96, 96, 64, 256
RW_GN_EPS = 64e-5
NSA_HEADS, NSA_G, NSA_DH = 16, 4, 64
NSA_HG = NSA_HEADS // NSA_G
NSA_W = NSA_HEADS * NSA_DH
NSA_KV_W = NSA_G * NSA_DH
CMP_BLOCK, CMP_STRIDE, CMP_HIDDEN = 32, 16, 256
SLC_BLOCK, SLC_TOPK, WINDOW, Q_BLOCK = 64, 16, 512, 128
ROPE_THETA = 10000.0
MLP_HIDDEN = 4 * D_MODEL
RMS_EPS = 1e-6
NEG_INF = -1e30
FORCED_SCORE = 1e9

LANES = 128
VMEM_LIMIT = 56 * 1024 * 1024

C_RET = 0
C_RW_RKV = 2048
C_RW_XW = 3584
C_RW_XA = 3712
C_RW_XG = 3840
C_RW_XV = 4096
C_NSA_Q = 4224
C_NSA_KS = 5248
C_NSA_KW = 5504
C_NSA_KC = 5760
C_NSA_VC = 6016
C_NSA_VS = 6272
C_NSA_VW = 6528
C_NSA_GATE = 6784
PROJ_W = 6912


def _cparams(sem):
    return pltpu.CompilerParams(dimension_semantics=sem, vmem_limit_bytes=VMEM_LIMIT)


def _dot(a, b):
    return jnp.dot(a, b, preferred_element_type=F32)


def _dot_t(a, b):
    return lax.dot_general(a, b, (((1,), (1,)), ((), ())), preferred_element_type=F32)


def _dot_tl(a, b):
    return lax.dot_general(a, b, (((0,), (0,)), ((), ())), preferred_element_type=F32)


def _split_dot(x, m_bf16, passes):
    hi = x.astype(BF16)
    acc = _dot(hi, m_bf16)
    rem = x - hi.astype(F32)
    for _ in range(passes - 1):
        piece = rem.astype(BF16)
        acc = acc + _dot(piece, m_bf16)
        rem = rem - piece.astype(F32)
    return acc


def _mm_kernel(*refs, nk, act, has_res):
    a_ref, b_ref = refs[:2]
    r_ref = refs[2] if has_res else None
    o_ref = refs[3] if has_res else refs[2]

    def finish(y):
        if act == "relu2":
            y = jnp.square(jnp.maximum(y, 0.0))
        if has_res:
            y = y + r_ref[...]
        o_ref[...] = y.astype(o_ref.dtype)

    part = _dot(a_ref[...], b_ref[...].astype(BF16))
    if nk == 1:
        finish(part)
        return
    acc_ref = refs[-1]
    k = pl.program_id(2)

    @pl.when(k == 0)
    def _():
        acc_ref[...] = part

    @pl.when(k > 0)
    def _():
        acc_ref[...] += part

    @pl.when(k == nk - 1)
    def _():
        finish(acc_ref[...])


def _pick(n, cands):
    for c in cands:
        if n % c == 0:
            return c
    raise ValueError(f"no tile for {n}")


def _matmul(a, b, *, res=None, act=None, out_dtype=F32, name="matmul"):
    m, kd = a.shape
    _, n = b.shape
    tm = _pick(m, (1024, 512, 256, 128))
    tn = _pick(n, (1024, 768, 512, 256, 128))
    tk = _pick(kd, (2048, 1024, 512))
    nk = kd // tk
    in_specs = [pl.BlockSpec((tm, tk), lambda i, j, k: (i, k)), pl.BlockSpec((tk, tn), lambda i, j, k: (k, j))]
    args = [a, b]
    if res is not None:
        in_specs.append(pl.BlockSpec((tm, tn), lambda i, j, k: (i, j)))
        args.append(res)
    return pl.pallas_call(
        functools.partial(_mm_kernel, nk=nk, act=act, has_res=res is not None),
        out_shape=jax.ShapeDtypeStruct((m, n), out_dtype),
        grid=(m // tm, n // tn, nk),
        in_specs=in_specs,
        out_specs=pl.BlockSpec((tm, tn), lambda i, j, k: (i, j)),
        scratch_shapes=[pltpu.VMEM((tm, tn), F32)] if nk > 1 else [],
        compiler_params=_cparams(("parallel", "parallel", "arbitrary")),
        name=name,
    )(*args)


def _norm_mm_kernel(x_ref, g_ref, b_ref, o_ref, h_ref, *, act):
    @pl.when(pl.program_id(1) == 0)
    def _():
        x = x_ref[...]
        ms = jnp.mean(x * x, axis=-1, keepdims=True)
        h_ref[...] = (x * lax.rsqrt(ms + RMS_EPS) * g_ref[...]).astype(BF16)

    y = _dot(h_ref[...], b_ref[...].astype(BF16))
    if act == "relu2":
        y = jnp.square(jnp.maximum(y, 0.0))
    o_ref[...] = y.astype(o_ref.dtype)


def _norm_matmul(x, g, b, *, act=None, out_dtype=F32, name="norm_matmul"):
    m, kd = x.shape
    n = b.shape[1]
    tm = _pick(m, (1024, 512, 256, 128))
    tn = _pick(n, (1024, 768, 512, 256, 128))
    return pl.pallas_call(
        functools.partial(_norm_mm_kernel, act=act),
        out_shape=jax.ShapeDtypeStruct((m, n), out_dtype),
        grid=(m // tm, n // tn),
        in_specs=[pl.BlockSpec((tm, kd), lambda i, j: (i, 0)),
                  pl.BlockSpec((1, kd), lambda i, j: (0, 0)),
                  pl.BlockSpec((kd, tn), lambda i, j: (0, j))],
        out_specs=pl.BlockSpec((tm, tn), lambda i, j: (i, j)),
        scratch_shapes=[pltpu.VMEM((tm, kd), BF16)],
        compiler_params=_cparams(("parallel", "arbitrary")),
        name=name,
    )(x, g.reshape(1, kd), b)


def _out_proj_kernel(ret_ref, rw_ref, nsa_ref, w_ref, res_ref, o_ref):
    k0 = ret_ref.shape[1]
    k1 = k0 + rw_ref.shape[1]
    acc = _dot(ret_ref[...], w_ref[:k0, :].astype(BF16))
    acc = acc + _dot(rw_ref[...], w_ref[k0:k1, :].astype(BF16))
    acc = acc + _dot(nsa_ref[...], w_ref[k1:, :].astype(BF16))
    o_ref[...] = acc + res_ref[...]


def _out_proj(o_ret, o_rwkv, o_nsa, w, res):
    m = o_ret.shape[0]
    kd, n = w.shape
    assert kd == o_ret.shape[1] + o_rwkv.shape[1] + o_nsa.shape[1]
    tm = _pick(m, (1024, 512, 256, 128))
    tn = _pick(n, (1024, 512, 256, 128))

    def rows(a):
        return pl.BlockSpec((tm, a.shape[1]), lambda i, j: (i, 0))

    return pl.pallas_call(
        _out_proj_kernel,
        out_shape=jax.ShapeDtypeStruct((m, n), F32),
        grid=(m // tm, n // tn),
        in_specs=[rows(o_ret), rows(o_rwkv), rows(o_nsa),
                  pl.BlockSpec((kd, tn), lambda i, j: (0, j)),
                  pl.BlockSpec((tm, tn), lambda i, j: (i, j))],
        out_specs=pl.BlockSpec((tm, tn), lambda i, j: (i, j)),
        compiler_params=_cparams(("parallel", "parallel")),
        name="out_proj",
    )(o_ret, o_rwkv, o_nsa, w, res)


RET_CHUNK = 256


def _retention_kernel(q_ref, k_ref, v_ref, g_ref, cos_ref, sin_ref, dmat_ref, qd_ref, kd_ref, cd_ref, o_ref, state_ref):
    @pl.when(pl.program_id(0) == 0)
    def _():
        state_ref[...] = jnp.zeros_like(state_ref)

    cos = cos_ref[...]
    sin = sin_ref[...]

    def rope(x):
        return x * cos + pltpu.roll(x, RET_DH // 2, axis=1) * sin

    heads = range(RET_HEADS)
    hs = [slice(h * RET_DH, (h + 1) * RET_DH) for h in heads]
    q_all, k_all, v_all, gate = q_ref[...], k_ref[...], v_ref[...], g_ref[...]
    qb = [rope(q_all[:, s]).astype(BF16) for s in hs]
    kf = [rope(k_all[:, s]) * (RET_DH ** -0.5) for s in hs]
    kb = [kf[h].astype(BF16) for h in heads]
    vb = [v_all[:, s].astype(BF16) for s in hs]
    inner = [(_dot_t(qb[h], kb[h]) * dmat_ref[h]).astype(BF16) for h in heads]
    cross = [_dot(qb[h], state_ref[h].astype(BF16)) * qd_ref[h] for h in heads]
    kv = [_dot_tl((kf[h] * kd_ref[h]).astype(BF16), vb[h]) for h in heads]
    outs = []
    for h in heads:
        out = _dot(inner[h], vb[h]) + cross[h]
        state_ref[h] = cd_ref[h] * state_ref[h] + kv[h]
        outs.append(out * lax.rsqrt(jnp.mean(out * out, axis=-1, keepdims=True) + RMS_EPS))
    o_ref[...] = (gate * jax.nn.sigmoid(gate) * jnp.concatenate(outs, axis=1)).astype(o_ref.dtype)


def _retention(proj, cos, sin):
    t = proj.shape[0]
    chunk = RET_CHUNK
    log_gamma = np.log(1.0 - 2.0 ** (-5.0 - np.arange(RET_HEADS, dtype=np.float64)))[:, None, None]
    n = np.arange(chunk, dtype=np.float64)
    lag = n[:, None] - n[None, :]
    dmat = np.where(lag >= 0, np.exp(np.maximum(lag, 0.0)[None] * log_gamma), 0.0)
    ones = np.ones((1, 1, RET_DH))
    qd = np.exp((n + 1.0)[None, :, None] * log_gamma) * ones
    kd = np.exp((chunk - 1.0 - n)[None, :, None] * log_gamma) * ones
    cd = np.exp(chunk * log_gamma) * ones
    tables = [jnp.asarray(a, F32) for a in (dmat, qd, kd, cd)]
    base = C_RET // RET_W

    def col(off):
        return pl.BlockSpec((chunk, RET_W), lambda c, off=off: (c, base + off))

    def full(a):
        return pl.BlockSpec(a.shape, lambda c: (0, 0, 0))

    tab = pl.BlockSpec((chunk, RET_DH), lambda c: (c, 0))
    return pl.pallas_call(
        _retention_kernel,
        out_shape=jax.ShapeDtypeStruct((t, RET_W), BF16),
        grid=(t // chunk,),
        in_specs=[col(0), col(1), col(2), col(3), tab, tab] + [full(a) for a in tables],
        out_specs=pl.BlockSpec((chunk, RET_W), lambda c: (c, 0)),
        scratch_shapes=[pltpu.VMEM((RET_HEADS, RET_DH, RET_DH), F32)],
        compiler_params=_cparams(("arbitrary",)),
        name="retention",
    )(proj, proj, proj, proj, cos, sin, *tables)


def _dot_f32(a, b):
    a_hi = a.astype(BF16)
    b_hi = b.astype(BF16)
    a_lo = (a - a_hi.astype(F32)).astype(BF16)
    b_lo = (b - b_hi.astype(F32)).astype(BF16)
    return _dot(a_hi, b_hi) + (_dot(a_hi, b_lo) + _dot(a_lo, b_hi))


RW_PRE_TM = 256
_MIX_OFF = (0, 512, 1024, 1536, 1664, 1792, 2048, 2176)


def _rwkv_pre_kernel(r_ref, k_ref, v_ref, xw_ref, xa_ref, xg_ref, xv_ref,
                     rp_ref, kp_ref, vp_ref, xwp_ref, xap_ref, xgp_ref, xvp_ref,
                     mix_ref, w0_ref, w2_ref, a0_ref, a2_ref, v0_ref, v2_ref, g2_ref, kk_ref, ka_ref,
                     gsum_ref, vfirst_ref,
                     r_out, lw_out, k_out, v_out, an_out, b_out, g_out, *, use_vres):
    i = pl.program_id(0)

    def mixed(cur_ref, prev_ref, seg):
        cur = cur_ref[...]
        mix = mix_ref[:, _MIX_OFF[seg]:_MIX_OFF[seg + 1]]
        prev_row = jnp.where(i > 0, prev_ref[7:8, :], 0.0)
        rid = lax.broadcasted_iota(jnp.int32, cur.shape, 0)
        shifted = jnp.where(rid == 0, prev_row, pltpu.roll(cur, 1, axis=0))
        return cur + mix * (shifted - cur)

    r = mixed(r_ref, rp_ref, 0)
    k = mixed(k_ref, kp_ref, 1)
    v = mixed(v_ref, vp_ref, 2)
    xw = mixed(xw_ref, xwp_ref, 3)
    xa = mixed(xa_ref, xap_ref, 4)
    xg = mixed(xg_ref, xgp_ref, 5)

    z = -(w0_ref[...] + _dot_f32(jnp.tanh(xw), w2_ref[...]))
    softplus = jnp.maximum(z, 0.0) + jnp.log(1.0 + jnp.exp(-jnp.abs(z)))
    w = -softplus - 0.5
    lw_out[...] = -jnp.exp(w)
    a = jax.nn.sigmoid(a0_ref[...] + _dot_f32(xa, a2_ref[...]))
    g_out[...] = _dot_f32(jax.nn.sigmoid(xg), g2_ref[...])
    if use_vres:
        xv = mixed(xv_ref, xvp_ref, 6)
        v = v + (vfirst_ref[...] - v) * jax.nn.sigmoid(v0_ref[...] + _dot_f32(xv, v2_ref[...]))
    kk = k * kk_ref[...]
    ss = _split_dot(kk * kk, gsum_ref[...], 3)
    kk = kk / jnp.maximum(jnp.sqrt(ss), 1e-12)
    r_out[...] = r
    k_out[...] = k * (1.0 + (a - 1.0) * ka_ref[...])
    v_out[...] = v
    an_out[...] = -kk
    b_out[...] = kk * a


def _rwkv_pre(proj, mix, w0, w2, a0, a2, v0, v2, g2, k_k, k_a, gsum, v_first, use_vres):
    t = proj.shape[0]
    tm = RW_PRE_TM

    def cur(width, off):
        return pl.BlockSpec((tm, width), lambda i: (i, off // width))

    def prev(width, off):
        return pl.BlockSpec((8, width), lambda i: (jnp.maximum(i * (tm // 8) - 1, 0), off // width))

    def full(a):
        return pl.BlockSpec(a.shape, lambda i: (0,) * a.ndim)

    segs = [(RW_W, C_RW_RKV), (RW_W, C_RW_RKV + RW_W), (RW_W, C_RW_RKV + 2 * RW_W),
            (128, C_RW_XW), (128, C_RW_XA), (256, C_RW_XG), (128, C_RW_XV)]
    params = [mix, w0, w2, a0, a2, v0, v2, g2, k_k, k_a, gsum]
    row = pl.BlockSpec((tm, RW_W), lambda i: (i, 0))
    outs = pl.pallas_call(
        functools.partial(_rwkv_pre_kernel, use_vres=use_vres),
        out_shape=[jax.ShapeDtypeStruct((t, RW_W), F32)] * 7,
        grid=(t // tm,),
        in_specs=[cur(w, o) for w, o in segs] + [prev(w, o) for w, o in segs] + [full(p) for p in params] + [row],
        out_specs=[row] * 7,
        compiler_params=_cparams(("parallel",)),
        name="rwkv_pre",
    )(*([proj] * 14), *params, v_first)
    return outs


RW_CHUNK = 128


def _wkv_kernel(r_ref, lw_ref, k_ref, v_ref, an_ref, b_ref, g_ref, lng_ref, lnb_ref, rk_ref,
                gsum_ref, tril_ref, o_ref, s_ref):
    c = pl.program_id(0)
    C = RW_CHUNK
    N = RW_N

    @pl.when(c == 0)
    def _():
        s_ref[...] = jnp.zeros_like(s_ref)

    r = r_ref[...]
    lw = lw_ref[...]
    k = k_ref[...]
    v = v_ref[...]
    lw_hi = lw.astype(BF16)
    rem = lw - lw_hi.astype(F32)
    lw_mid = rem.astype(BF16)
    lw_lo = (rem - lw_mid.astype(F32)).astype(BF16)
    tril = tril_ref[...]
    lg = _dot(tril, lw_hi) + (_dot(tril, lw_mid) + _dot(tril, lw_lo))
    gam = jnp.exp(lg)
    ginv = jnp.exp(-lg)
    at = (an_ref[...] * jnp.exp(lg - lw)).astype(BF16)
    bt = (b_ref[...] * ginv).astype(BF16)
    kt = (k * ginv).astype(BF16)
    rt_f = r * gam
    rt = rt_f.astype(BF16)
    vb = v.astype(BF16)
    g_last = gam[C - 1:C, :]

    rowi = lax.broadcasted_iota(jnp.int32, (C, C), 0)
    coli = lax.broadcasted_iota(jnp.int32, (C, C), 1)
    strict = rowi > coli
    incl = rowi >= coli

    heads = range(RW_HEADS)
    hs = [slice(h * N, (h + 1) * N) for h in heads]
    bth = [bt[:, s] for s in hs]
    kth = [kt[:, s] for s in hs]
    vh = [vb[:, s] for s in hs]
    big = [_dot_t(jnp.concatenate([at[:, s], rt[:, s]], axis=0), jnp.concatenate([bth[h], kth[h]], axis=0))
           for h, s in zip(heads, hs)]
    a_ab = [jnp.where(strict, big[h][:C, :C], 0.0) for h in heads]
    a_ak = [jnp.where(strict, big[h][:C, C:], 0.0).astype(BF16) for h in heads]
    a_rb = [jnp.where(incl, big[h][C:, :C], 0.0).astype(BF16) for h in heads]
    a_rk = [jnp.where(incl, big[h][C:, C:], 0.0).astype(BF16) for h in heads]
    akv = [_dot(a_ak[h], vh[h]) for h in heads]
    def same_block(size):
        shift = size.bit_length() - 1
        return (rowi >> shift) == (coli >> shift)

    tinv = [jnp.where(same_block(2), a_ab[h], 0.0) + jnp.where(rowi == coli, 1.0, 0.0) for h in heads]
    size = 2
    while size < C:
        lower_left = same_block(2 * size) & jnp.logical_not(same_block(size))
        off = [jnp.where(lower_left, a_ab[h], 0.0).astype(BF16) for h in heads]
        tb = [tinv[h].astype(BF16) for h in heads]
        half = [_dot(tb[h], off[h]).astype(BF16) for h in heads]
        tinv = [tinv[h] + _dot(half[h], tb[h]) for h in heads]
        size *= 2
    xb = [_dot(tinv[h].astype(BF16), jnp.concatenate([at[:, hs[h]], akv[h].astype(BF16)], axis=1)).astype(BF16)
          for h in heads]
    yx = [_dot(a_rb[h], xb[h]) for h in heads]
    ykv = [_dot(a_rk[h], vh[h]) for h in heads]
    xtb = [_dot_tl(xb[h], bth[h]) for h in heads]
    vtk = [_dot_tl(vh[h], kth[h]) for h in heads]
    ys = []
    for h in heads:
        gl = g_last[:, hs[h]]
        y1 = (rt_f[:, hs[h]] + yx[h][:, :N]).astype(BF16)
        s0 = s_ref[h]
        s0b = s0.astype(BF16)
        ys.append(_dot_t(y1, s0b) + (yx[h][:, N:] + ykv[h]))
        s_ref[h] = (s0 + _dot(s0b, xtb[h][:N].astype(BF16)) + (xtb[h][N:] + vtk[h])) * gl
    y = jnp.concatenate(ys, axis=1)

    gsum = gsum_ref[...]
    inv_n = 1.0 / N
    mu = _split_dot(y, gsum, 3) * inv_n
    yc = y - mu
    var = _split_dot(yc * yc, gsum, 3) * inv_n
    yn = yc * lax.rsqrt(var + RW_GN_EPS) * lng_ref[...] + lnb_ref[...]
    bonus = _split_dot(r * k * rk_ref[...], gsum, 3)
    o_ref[...] = ((yn + bonus * v) * g_ref[...]).astype(o_ref.dtype)


def _wkv(r, lw, k, v, an, b, g, ln_g, ln_b, r_k, gsum):
    t = r.shape[0]
    C = RW_CHUNK
    tril = jnp.asarray(np.tril(np.ones((C, C), np.float32)), BF16)
    row = pl.BlockSpec((C, RW_W), lambda c: (c, 0))
    vec = pl.BlockSpec((1, RW_W), lambda c: (0, 0))
    return pl.pallas_call(
        _wkv_kernel,
        out_shape=jax.ShapeDtypeStruct((t, RW_W), BF16),
        grid=(t // C,),
        in_specs=[row] * 7 + [vec] * 3 + [pl.BlockSpec((RW_W, RW_W), lambda c: (0, 0)),
                                          pl.BlockSpec((C, C), lambda c: (0, 0))],
        out_specs=row,
        scratch_shapes=[pltpu.VMEM((RW_HEADS, RW_N, RW_N), F32)],
        compiler_params=_cparams(("arbitrary",)),
        name="wkv7",
    )(r, lw, k, v, an, b, g, ln_g, ln_b, r_k, gsum, tril)


NSA_PREP_TM = 512


N_NORM_SLABS = (NSA_W + 2 * NSA_KV_W) // LANES
N_RAW_SLABS = 4 * NSA_KV_W // LANES
N_NSA_SLABS = N_NORM_SLABS + N_RAW_SLABS + 1


def _nsa_prep_kernel(*refs):
    x_refs = refs[:N_NSA_SLABS]
    cos_ref, sin_ref, w_ref, gmean_ref, qk_ref, hmat_ref, v1_ref, gate_ref = refs[N_NSA_SLABS:]
    cos = cos_ref[...]
    sin = sin_ref[...]
    gmean = gmean_ref[...]
    half = NSA_DH // 2
    lane = lax.broadcasted_iota(jnp.int32, cos.shape, 1)
    first_half = (lane % NSA_DH) < half
    low = lane < NSA_DH
    for s in range(N_NORM_SLABS):
        x = x_refs[s][...]
        y = x * lax.rsqrt(_split_dot(x * x, gmean, 3) + RMS_EPS) * w_ref[s]
        rot = jnp.where(first_half, pltpu.roll(y, LANES - half, axis=1), pltpu.roll(y, half, axis=1))
        res = (y * cos + rot * sin).astype(qk_ref.dtype)
        qk_ref[2 * s] = res[:, :NSA_DH]
        qk_ref[2 * s + 1] = res[:, NSA_DH:]

    rows = hmat_ref.shape[2]
    low_r = lax.broadcasted_iota(jnp.int32, (rows, LANES), 1) < NSA_DH
    for s in range(NSA_KV_W // LANES * 2):
        kind, pair = divmod(s, NSA_KV_W // LANES)
        x_ref = x_refs[N_NORM_SLABS + s]
        for t2 in range(CMP_STRIDE // 2):
            r0 = x_ref[pl.ds(2 * t2, rows, stride=CMP_STRIDE), :]
            r1 = x_ref[pl.ds(2 * t2 + 1, rows, stride=CMP_STRIDE), :]
            cols = slice(t2 * LANES, (t2 + 1) * LANES)
            hmat_ref[kind, 2 * pair, :, cols] = jnp.where(low_r, r0, pltpu.roll(r1, NSA_DH, axis=1)).astype(BF16)
            hmat_ref[kind, 2 * pair + 1, :, cols] = jnp.where(low_r, pltpu.roll(r0, NSA_DH, axis=1), r1).astype(BF16)

    one_col = jnp.where(lane == NSA_DH, 1.0, 0.0)
    for s in range(NSA_KV_W // LANES * 2):
        x = x_refs[N_NORM_SLABS + N_RAW_SLABS // 2 + s][...]
        v1_ref[2 * s] = jnp.where(low, x, one_col).astype(BF16)
        v1_ref[2 * s + 1] = jnp.where(low, pltpu.roll(x, NSA_DH, axis=1), one_col).astype(BF16)

    xg = x_refs[N_NSA_SLABS - 1][...]
    per_group = 3 * NSA_HG
    for g in range(NSA_G):
        shifted = xg if g == 0 else pltpu.roll(xg, LANES - per_group * g, axis=1)
        gate_ref[g] = jnp.where(lane < per_group, shifted, 0.0)


def _nsa_prep(proj, w_slabs, cos, sin, gmean):
    t = proj.shape[0]
    tm = NSA_PREP_TM
    base = C_NSA_Q // LANES
    tab = pl.BlockSpec((tm, LANES), lambda i: (i, 0))
    n_heads_out = 2 * N_NORM_SLABS
    return pl.pallas_call(
        _nsa_prep_kernel,
        out_shape=[jax.ShapeDtypeStruct((n_heads_out, t, NSA_DH), BF16),
                   jax.ShapeDtypeStruct((2, NSA_G, t // CMP_STRIDE, CMP_STRIDE * NSA_DH), BF16),
                   jax.ShapeDtypeStruct((2 * NSA_G, t, LANES), BF16),
                   jax.ShapeDtypeStruct((NSA_G, t, LANES), F32)],
        grid=(t // tm,),
        in_specs=[pl.BlockSpec((tm, LANES), lambda i, s=s: (i, base + s)) for s in range(N_NSA_SLABS)]
        + [tab, tab, pl.BlockSpec((N_NORM_SLABS, 1, LANES), lambda i: (0, 0, 0)),
           pl.BlockSpec((LANES, LANES), lambda i: (0, 0))],
        out_specs=[pl.BlockSpec((n_heads_out, tm, NSA_DH), lambda i: (0, i, 0)),
                   pl.BlockSpec((2, NSA_G, tm // CMP_STRIDE, CMP_STRIDE * NSA_DH), lambda i: (0, 0, i, 0)),
                   pl.BlockSpec((2 * NSA_G, tm, LANES), lambda i: (0, i, 0)),
                   pl.BlockSpec((NSA_G, tm, LANES), lambda i: (0, i, 0))],
        compiler_params=_cparams(("parallel",)),
        name="nsa_prep",
    )(*([proj] * N_NSA_SLABS), cos, sin, w_slabs, gmean)


def _compress_kernel(h_ref, w1_ref, w2_ref, pe_ref, nw_ref, cos_ref, sin_ref, o_ref):
    kind = pl.program_id(0)
    nc = h_ref.shape[2]
    half_in = CMP_STRIDE * NSA_DH
    hm = h_ref[0, 0]
    w1 = w1_ref[0]
    first = _dot(hm, w1[:half_in])
    second = _dot(hm, w1[half_in:])
    const = _dot(pe_ref[0], w1)[0:1, :]
    pre = first + pltpu.roll(second, nc - 1, axis=0) + const
    hid = 0.5 * pre * (1.0 + jnp.tanh(0.7978845608028654 * (pre + 0.044715 * pre * pre * pre)))
    out = _dot(hid.astype(BF16), w2_ref[0])
    y = out * lax.rsqrt(jnp.mean(out * out, axis=-1, keepdims=True) + RMS_EPS) * nw_ref[...]
    half = NSA_DH // 2
    rot = jnp.concatenate([y[:, half:], y[:, :half]], axis=1)
    roped = y * cos_ref[...] + rot * sin_ref[...]
    o_ref[0, 0] = jnp.where(kind == 0, roped, out).astype(o_ref.dtype)


def _compress(hmat, w1, w2, pe, nw, cos_c, sin_c):
    _, g, nc, width = hmat.shape
    return pl.pallas_call(
        _compress_kernel,
        out_shape=jax.ShapeDtypeStruct((2, g, nc, NSA_DH), BF16),
        grid=(2, g),
        in_specs=[pl.BlockSpec((1, 1, nc, width), lambda a, b: (a, b, 0, 0)),
                  pl.BlockSpec((1, 2 * width, CMP_HIDDEN), lambda a, b: (a, 0, 0)),
                  pl.BlockSpec((1, CMP_HIDDEN, NSA_DH), lambda a, b: (a, 0, 0)),
                  pl.BlockSpec((1, 8, 2 * width), lambda a, b: (a, 0, 0)),
                  pl.BlockSpec((1, NSA_DH), lambda a, b: (0, 0)),
                  pl.BlockSpec((nc, NSA_DH), lambda a, b: (0, 0)),
                  pl.BlockSpec((nc, NSA_DH), lambda a, b: (0, 0))],
        out_specs=pl.BlockSpec((1, 1, nc, NSA_DH), lambda a, b: (a, b, 0, 0)),
        compiler_params=_cparams(("parallel", "parallel")),
        name="nsa_compress",
    )(hmat, w1, w2, pe, nw, cos_c, sin_c)


SLC_KT = 1024
WIN_KEYS = WINDOW + Q_BLOCK


INT32_MIN = -2 ** 31


def _nsa_attn_kernel(q_ref, gate_ref, kc_ref, vc_ref, ks_ref, vs_ref, kw_ref, vw_ref, cis_ref, expand_ref, ltri_ref,
                     o_ref):
    qi = pl.program_id(1)
    nc = kc_ref.shape[2]
    nb = cis_ref.shape[1]
    heads = range(NSA_HG)
    start = qi * Q_BLOCK
    qh = [q_ref[hh] for hh in heads]
    tok = start + lax.broadcasted_iota(jnp.int32, (Q_BLOCK, 1), 0)

    def softmax_terms(scores):
        return [jnp.exp(s - jnp.max(s, axis=-1, keepdims=True)) for s in scores]

    kc = kc_ref[0, 0]
    vc = vc_ref[0, 0]
    cend = lax.broadcasted_iota(jnp.int32, (1, nc), 1) * CMP_STRIDE + (CMP_BLOCK - 1)
    bias_c = jnp.where(cend <= tok, 0.0, NEG_INF)
    ec = softmax_terms([_dot_t(qh[hh], kc) + bias_c for hh in heads])
    sees_any = tok >= CMP_BLOCK - 1
    pc = [ec[hh] * jnp.where(sees_any, 1.0 / jnp.sum(ec[hh], axis=-1, keepdims=True), 0.0) for hh in heads]
    o_c = [_dot(pc[hh].astype(BF16), vc) for hh in heads]

    psum = pc[0]
    for hh in range(1, NSA_HG):
        psum = psum + pc[hh]
    score = _split_dot(psum, cis_ref[...], 3)
    blk = lax.broadcasted_iota(jnp.int32, (1, nb), 1)
    cur = tok // SLC_BLOCK
    forced = (blk == 0) | (blk == cur) | (blk == cur - 1)
    score = jnp.where(forced, FORCED_SCORE, score)
    score = jnp.where(blk <= cur, score, -jnp.inf)

    kbase = pl.multiple_of(jnp.maximum(start - WINDOW, 0), Q_BLOCK)
    kw = kw_ref[0, pl.ds(kbase, WIN_KEYS), :]
    lag = tok - (kbase + lax.broadcasted_iota(jnp.int32, (1, WIN_KEYS), 1))
    bias_w = jnp.where((lag >= 0) & (lag < WINDOW), 0.0, NEG_INF)
    sw = [_dot_t(qh[hh], kw) for hh in heads]

    bits = lax.bitcast_convert_type(score.T, jnp.int32)
    key = bits ^ ((bits >> 31) & 0x7FFFFFFF)
    n_sel = min(SLC_TOPK, nb)

    def enough(c):
        return jnp.sum(jnp.where(key >= c, 1.0, 0.0), axis=0, keepdims=True) >= n_sel

    def enough3(c1, c2, c3):
        packed = jnp.where(key >= c3, 65793.0, jnp.where(key >= c2, 257.0, jnp.where(key >= c1, 1.0, 0.0)))
        tot = jnp.sum(packed, axis=0, keepdims=True).astype(jnp.int32)
        return (tot & 255) >= n_sel, ((tot >> 8) & 255) >= n_sel, (tot >> 16) >= n_sel

    zero_row = jnp.zeros((1, Q_BLOCK), jnp.int32)
    thr = jnp.where(enough(zero_row), zero_row, INT32_MIN)
    for hi in range(30, 0, -2):
        c1 = thr + (1 << (hi - 1))
        c2 = thr + (1 << hi)
        c3 = c2 + (1 << (hi - 1))
        e1, e2, e3 = enough3(c1, c2, c3)
        thr = jnp.where(e3, c3, jnp.where(e2, c2, jnp.where(e1, c1, thr)))
    c1 = thr + 1
    thr = jnp.where(enough(c1), c1, thr)
    above = key > thr
    tied = key == thr
    need = n_sel - jnp.sum(jnp.where(above, 1.0, 0.0), axis=0, keepdims=True)
    tied_before = _dot(ltri_ref[...], jnp.where(tied, 1.0, 0.0).astype(BF16))
    sel_t = above | (tied & (tied_before < need))
    sel = jnp.where(sel_t, 1.0, 0.0).T.astype(BF16)

    ew = softmax_terms([sw[hh] + bias_w for hh in heads])
    acc_w = [_dot(ew[hh].astype(BF16), vw_ref[0, pl.ds(kbase, WIN_KEYS), :]) for hh in heads]

    n_tiles = (start + Q_BLOCK + SLC_KT - 1) // SLC_KT

    def sel_body(j, carry):
        ms, accs = carry
        k0 = pl.multiple_of(j * SLC_KT, SLC_KT)
        kt = ks_ref[0, pl.ds(k0, SLC_KT), :]
        vt = vs_ref[0, pl.ds(k0, SLC_KT), :]
        member = _dot(sel, expand_ref[:, pl.ds(k0, SLC_KT)])
        kpos = k0 + lax.broadcasted_iota(jnp.int32, (1, SLC_KT), 1)
        bias = jnp.where((member > 0.5) & (kpos <= tok), 0.0, NEG_INF)
        sj = [_dot_t(qh[hh], kt) + bias for hh in heads]
        m_new = [jnp.maximum(ms[hh], jnp.max(sj[hh], axis=-1, keepdims=True)) for hh in heads]
        pj = [jnp.exp((sj[hh] - m_new[hh]).astype(BF16)) for hh in heads]
        accs = [jnp.exp(ms[hh] - m_new[hh]) * accs[hh] + _dot(pj[hh], vt) for hh in heads]
        return tuple(m_new), tuple(accs)

    m0 = tuple(jnp.full((Q_BLOCK, 1), NEG_INF, F32) for _ in heads)
    a0 = tuple(jnp.zeros((Q_BLOCK, LANES), F32) for _ in heads)
    _, acc_s = lax.fori_loop(0, n_tiles, sel_body, (m0, a0))

    gt = jax.nn.sigmoid(gate_ref[0])
    outs = []
    for hh in heads:
        g0, g1, g2 = (gt[:, 3 * hh + br:3 * hh + br + 1] for br in range(3))
        scale_s = g1 / acc_s[hh][:, NSA_DH:NSA_DH + 1]
        scale_w = g2 / acc_w[hh][:, NSA_DH:NSA_DH + 1]
        outs.append(g0 * o_c[hh] + scale_s * acc_s[hh][:, :NSA_DH] + scale_w * acc_w[hh][:, :NSA_DH])
    o_ref[...] = jnp.concatenate(outs, axis=1).astype(o_ref.dtype)


def _nsa_attn(qk_hm, gates, cmp_kv, v1, cis, expand, ltri):
    t = qk_hm.shape[1]
    nq = t // Q_BLOCK
    nc = cmp_kv.shape[2]
    nb = cis.shape[1]
    ks_spec = pl.BlockSpec((1, t, NSA_DH), lambda g, i: (NSA_HEADS + g, 0, 0))
    kw_spec = pl.BlockSpec((1, t, NSA_DH), lambda g, i: (NSA_HEADS + NSA_G + g, 0, 0))
    vs_spec = pl.BlockSpec((1, t, LANES), lambda g, i: (g, 0, 0))
    vw_spec = pl.BlockSpec((1, t, LANES), lambda g, i: (NSA_G + g, 0, 0))
    return pl.pallas_call(
        _nsa_attn_kernel,
        out_shape=jax.ShapeDtypeStruct((t, NSA_W), BF16),
        grid=(NSA_G, nq),
        in_specs=[pl.BlockSpec((NSA_HG, Q_BLOCK, NSA_DH), lambda g, i: (g, i, 0)),
                  pl.BlockSpec((1, Q_BLOCK, LANES), lambda g, i: (g, i, 0)),
                  pl.BlockSpec((1, 1, nc, NSA_DH), lambda g, i: (0, g, 0, 0)),
                  pl.BlockSpec((1, 1, nc, NSA_DH), lambda g, i: (1, g, 0, 0)),
                  ks_spec, vs_spec, kw_spec, vw_spec,
                  pl.BlockSpec((nc, nb), lambda g, i: (0, 0)),
                  pl.BlockSpec((nb, t), lambda g, i: (0, 0)),
                  pl.BlockSpec((nb, nb), lambda g, i: (0, 0))],
        out_specs=pl.BlockSpec((Q_BLOCK, NSA_HG * NSA_DH), lambda g, i: (i, g)),
        compiler_params=_cparams(("parallel", "arbitrary")),
        name="nsa_attention",
    )(qk_hm, gates, cmp_kv, cmp_kv, qk_hm, v1, qk_hm, v1, cis, expand, ltri)


def _rope_tables(pos, dh, reps):
    half = dh // 2
    inv_freq = ROPE_THETA ** (-jnp.arange(half, dtype=F32) / half)
    ang = pos.astype(F32)[:, None] * inv_freq[None, :]
    cos = jnp.cos(ang)
    sin = jnp.sin(ang)
    cos_t = jnp.tile(jnp.concatenate([cos, cos], axis=1), (1, reps))
    sin_t = jnp.tile(jnp.concatenate([-sin, sin], axis=1), (1, reps))
    return cos_t, sin_t


def _pad_cols(a, width):
    return jnp.pad(a, ((0, 0), (0, width - a.shape[1])))


def _pad_rows(a, height):
    return jnp.pad(a, ((0, height - a.shape[0]), (0, 0)))


def _pack_w_in(w_l, w_vres):
    d = w_l.shape[0]
    o = 0
    ret = w_l[:, o:o + 4 * RET_W]; o += 4 * RET_W
    rkv = w_l[:, o:o + 3 * RW_W]; o += 3 * RW_W
    xw = w_l[:, o:o + RW_DECAY_RANK]; o += RW_DECAY_RANK
    xa = w_l[:, o:o + RW_A_RANK]; o += RW_A_RANK
    xg = w_l[:, o:o + RW_GATE_RANK]; o += RW_GATE_RANK
    q = w_l[:, o:o + NSA_W]; o += NSA_W
    kc, vc, ks, vs, kw, vw = (w_l[:, o + i * NSA_KV_W:o + (i + 1) * NSA_KV_W] for i in range(6))
    o += 6 * NSA_KV_W
    gates = w_l[:, o:o + 3 * NSA_HEADS]
    xv = jnp.zeros((d, LANES), w_l.dtype) if w_vres is None else _pad_cols(w_vres, LANES)
    return jnp.concatenate([ret, rkv, _pad_cols(xw, LANES), _pad_cols(xa, LANES), xg, xv,
                            q, ks, kw, kc, vc, vs, vw, _pad_cols(gates, LANES)], axis=1)


def _pack_mix(mix, vres_mix):
    o = 3 * RW_W
    xw = mix[o:o + RW_DECAY_RANK]; o += RW_DECAY_RANK
    xa = mix[o:o + RW_A_RANK]; o += RW_A_RANK
    xg = mix[o:o + RW_GATE_RANK]
    z = lambda n: jnp.zeros((n,), mix.dtype)
    xv = z(LANES) if vres_mix is None else jnp.concatenate([vres_mix, z(LANES - RW_V_RANK)])
    return jnp.concatenate([mix[:3 * RW_W], xw, z(LANES - RW_DECAY_RANK), xa, z(LANES - RW_A_RANK), xg, xv])[None, :]


def kernel(x, ln1_g, w_in, w_in_vres, rwkv_mix, rwkv_vres_mix, rwkv_w0, rwkv_w2, rwkv_a0, rwkv_a2, rwkv_v0, rwkv_v2, rwkv_g2, rwkv_k_k, rwkv_k_a, rwkv_r_k, rwkv_ln_g, rwkv_ln_b, nsa_q_norm, nsa_k_norm, nsa_cmp_pe, nsa_cmp_k_w1, nsa_cmp_k_w2, nsa_cmp_v_w1, nsa_cmp_v_w2, w_out, ln2_g, w_up, w_down):
    bsz, t, d = x.shape
    assert bsz == 1
    depth = w_in.shape[0]
    nc = t // CMP_STRIDE
    nb = t // SLC_BLOCK
    xs = x.reshape(t, d)

    pos = jnp.arange(t)
    cos_r, sin_r = _rope_tables(pos, RET_DH, 1)
    cos_n, sin_n = _rope_tables(pos, NSA_DH, 2)
    cos_c, sin_c = _rope_tables(jnp.arange(nc) * CMP_STRIDE + (CMP_BLOCK - 1), NSA_DH, 1)
    lane_head = np.arange(RW_W) // RW_N
    gsum = jnp.asarray(lane_head[:, None] == lane_head[None, :], BF16)
    lane_h2 = np.arange(LANES) // NSA_DH
    gmean = jnp.asarray((lane_h2[:, None] == lane_h2[None, :]) / float(NSA_DH), BF16)
    cstart = np.arange(nc) * CMP_STRIDE
    sstart = np.arange(nb) * SLC_BLOCK
    cis = jnp.asarray((cstart[:, None] <= sstart[None, :] + SLC_BLOCK - 1)
                      & (cstart[:, None] + CMP_BLOCK - 1 >= sstart[None, :]), BF16)

    expand = jnp.asarray((np.arange(t)[None, :] // SLC_BLOCK) == np.arange(nb)[:, None], BF16)
    ltri = jnp.asarray(np.tril(np.ones((nb, nb), np.float32), -1), BF16)

    v_first = jnp.zeros((t, RW_W), F32)
    for l in range(depth):
        vres = l > 0
        w_cat = _pack_w_in(w_in[l], w_in_vres[l - 1] if vres else None).astype(BF16)
        proj = _norm_matmul(xs, ln1_g[l], w_cat, name="in_proj")

        o_ret = _retention(proj, cos_r, sin_r)

        row = lambda a: a.reshape(1, -1)
        mix = _pack_mix(rwkv_mix[l], rwkv_vres_mix[l - 1] if vres else None)
        v0 = row(rwkv_v0[l - 1]) if vres else jnp.zeros((1, RW_W), F32)
        v2 = _pad_rows(rwkv_v2[l - 1], LANES) if vres else jnp.zeros((LANES, RW_W), F32)
        r_, lw_, k_, v_, an_, b_, g_ = _rwkv_pre(
            proj, mix, row(rwkv_w0[l]), _pad_rows(rwkv_w2[l], LANES), row(rwkv_a0[l]), _pad_rows(rwkv_a2[l], LANES),
            v0, v2, rwkv_g2[l], row(rwkv_k_k[l]), row(rwkv_k_a[l]), gsum, v_first, vres)
        if not vres:
            v_first = v_
        o_rwkv = _wkv(r_, lw_, k_, v_, an_, b_, g_, row(rwkv_ln_g[l]), row(rwkv_ln_b[l]), row(rwkv_r_k[l]), gsum)

        n_q = NSA_W // LANES
        n_k = NSA_KV_W // LANES
        w_slabs = jnp.stack([jnp.tile(nsa_q_norm[l] * (NSA_DH ** -0.5), 2)] * n_q
                            + [jnp.tile(nsa_k_norm[l, 1], 2)] * n_k + [jnp.tile(nsa_k_norm[l, 2], 2)] * n_k)[:, None, :]
        qk_hm, hmat, v1, gates = _nsa_prep(proj, w_slabs, cos_n, sin_n, gmean)
        w1 = jnp.stack([nsa_cmp_k_w1[l], nsa_cmp_v_w1[l]]).astype(BF16)
        w2 = jnp.stack([nsa_cmp_k_w2[l], nsa_cmp_v_w2[l]]).astype(BF16)
        pe = jnp.broadcast_to(nsa_cmp_pe[l].reshape(2, 1, CMP_BLOCK * NSA_DH), (2, 8, CMP_BLOCK * NSA_DH)).astype(BF16)
        cmp_kv = _compress(hmat, w1, w2, pe, row(nsa_k_norm[l, 0]), cos_c, sin_c)
        o_nsa = _nsa_attn(qk_hm, gates, cmp_kv, v1, cis, expand, ltri)

        xs = _out_proj(o_ret, o_rwkv, o_nsa, w_out[l], xs)

        up = _norm_matmul(xs, ln2_g[l], w_up[l], act="relu2", out_dtype=BF16, name="mlp_up")
        xs = _matmul(up, w_down[l], res=xs, name="mlp_down")
    return xs.reshape(bsz, t, d)
```

```python
import functools

import numpy as np
import jax
import jax.numpy as jnp
from jax import lax
from jax.experimental import pallas as pl
from jax.experimental.pallas import tpu as pltpu

F32 = jnp.float32
BF16 = jnp.bfloat16

D_MODEL = 2048
RET_HEADS, RET_DH = 4, 128
RET_W = RET_HEADS * RET_DH
RW_HEADS, RW_N = 8, 64
RW_W = RW_HEADS * RW_N
RW_DECAY_RANK, RW_A_RANK, RW_V_RANK, RW_GATE_RANK = 96, 96, 64, 256
RW_GN_EPS = 64e-5
NSA_HEADS, NSA_G, NSA_DH = 16, 4, 64
NSA_HG = NSA_HEADS // NSA_G
NSA_W = NSA_HEADS * NSA_DH
NSA_KV_W = NSA_G * NSA_DH
CMP_BLOCK, CMP_STRIDE, CMP_HIDDEN = 32, 16, 256
SLC_BLOCK, SLC_TOPK, WINDOW, Q_BLOCK = 64, 16, 512, 128
ROPE_THETA = 10000.0
MLP_HIDDEN = 4 * D_MODEL
RMS_EPS = 1e-6
NEG_INF = -1e30
FORCED_SCORE = 1e9

LANES = 128
VMEM_LIMIT = 56 * 1024 * 1024

C_RET = 0
C_RW_RKV = 2048
C_RW_XW = 3584
C_RW_XA = 3712
C_RW_XG = 3840
C_RW_XV = 4096
C_NSA_Q = 4224
C_NSA_KS = 5248
C_NSA_KW = 5504
C_NSA_KC = 5760
C_NSA_VC = 6016
C_NSA_VS = 6272
C_NSA_VW = 6528
C_NSA_GATE = 6784
PROJ_W = 6912


def _cparams(sem):
    return pltpu.CompilerParams(dimension_semantics=sem, vmem_limit_bytes=VMEM_LIMIT)


def _dot(a, b):
    return jnp.dot(a, b, preferred_element_type=F32)


def _dot_t(a, b):
    return lax.dot_general(a, b, (((1,), (1,)), ((), ())), preferred_element_type=F32)


def _dot_tl(a, b):
    return lax.dot_general(a, b, (((0,), (0,)), ((), ())), preferred_element_type=F32)


def _split_dot(x, m_bf16, passes):
    hi = x.astype(BF16)
    acc = _dot(hi, m_bf16)
    rem = x - hi.astype(F32)
    for _ in range(passes - 1):
        piece = rem.astype(BF16)
        acc = acc + _dot(piece, m_bf16)
        rem = rem - piece.astype(F32)
    return acc


def _mm_kernel(*refs, nk, act, has_res):
    a_ref, b_ref = refs[:2]
    r_ref = refs[2] if has_res else None
    o_ref = refs[3] if has_res else refs[2]

    def finish(y):
        if act == "relu2":
            y = jnp.square(jnp.maximum(y, 0.0))
        if has_res:
            y = y + r_ref[...]
        o_ref[...] = y.astype(o_ref.dtype)

    part = _dot(a_ref[...], b_ref[...].astype(BF16))
    if nk == 1:
        finish(part)
        return
    acc_ref = refs[-1]
    k = pl.program_id(2)

    @pl.when(k == 0)
    def _():
        acc_ref[...] = part

    @pl.when(k > 0)
    def _():
        acc_ref[...] += part

    @pl.when(k == nk - 1)
    def _():
        finish(acc_ref[...])


def _pick(n, cands):
    for c in cands:
        if n % c == 0:
            return c
    raise ValueError(f"no tile for {n}")


def _matmul(a, b, *, res=None, act=None, out_dtype=F32, name="matmul"):
    m, kd = a.shape
    _, n = b.shape
    tm = _pick(m, (1024, 512, 256, 128))
    tn = _pick(n, (1024, 768, 512, 256, 128))
    tk = _pick(kd, (2048, 1024, 512))
    nk = kd // tk
    in_specs = [pl.BlockSpec((tm, tk), lambda i, j, k: (i, k)), pl.BlockSpec((tk, tn), lambda i, j, k: (k, j))]
    args = [a, b]
    if res is not None:
        in_specs.append(pl.BlockSpec((tm, tn), lambda i, j, k: (i, j)))
        args.append(res)
    return pl.pallas_call(
        functools.partial(_mm_kernel, nk=nk, act=act, has_res=res is not None),
        out_shape=jax.ShapeDtypeStruct((m, n), out_dtype),
        grid=(m // tm, n // tn, nk),
        in_specs=in_specs,
        out_specs=pl.BlockSpec((tm, tn), lambda i, j, k: (i, j)),
        scratch_shapes=[pltpu.VMEM((tm, tn), F32)] if nk > 1 else [],
        compiler_params=_cparams(("parallel", "parallel", "arbitrary")),
        name=name,
    )(*args)


def _norm_mm_kernel(x_ref, g_ref, b_ref, o_ref, h_ref, *, act):
    @pl.when(pl.program_id(1) == 0)
    def _():
        x = x_ref[...]
        ms = jnp.mean(x * x, axis=-1, keepdims=True)
        h_ref[...] = (x * lax.rsqrt(ms + RMS_EPS) * g_ref[...]).astype(BF16)

    y = _dot(h_ref[...], b_ref[...].astype(BF16))
    if act == "relu2":
        y = jnp.square(jnp.maximum(y, 0.0))
    o_ref[...] = y.astype(o_ref.dtype)


def _norm_matmul(x, g, b, *, act=None, out_dtype=F32, name="norm_matmul"):
    m, kd = x.shape
    n = b.shape[1]
    tm = _pick(m, (2048, 1024, 512, 256, 128))
    tn = _pick(n, (512, 256, 128) if b.dtype == F32 else (768, 512, 256, 128))
    return pl.pallas_call(
        functools.partial(_norm_mm_kernel, act=act),
        out_shape=jax.ShapeDtypeStruct((m, n), out_dtype),
        grid=(m // tm, n // tn),
        in_specs=[pl.BlockSpec((tm, kd), lambda i, j: (i, 0), pipeline_mode=pl.Buffered(1)),
                  pl.BlockSpec((1, kd), lambda i, j: (0, 0)),
                  pl.BlockSpec((kd, tn), lambda i, j: (0, j))],
        out_specs=pl.BlockSpec((tm, tn), lambda i, j: (i, j)),
        scratch_shapes=[pltpu.VMEM((tm, kd), BF16)],
        compiler_params=_cparams(("parallel", "arbitrary")),
        name=name,
    )(x, g.reshape(1, kd), b)


def _out_proj_kernel(ret_ref, rw_ref, nsa_ref, w_ref, res_ref, o_ref):
    k0 = ret_ref.shape[1]
    k1 = k0 + rw_ref.shape[1]
    acc = _dot(ret_ref[...], w_ref[:k0, :].astype(BF16))
    acc = acc + _dot(rw_ref[...], w_ref[k0:k1, :].astype(BF16))
    acc = acc + _dot(nsa_ref[...], w_ref[k1:, :].astype(BF16))
    o_ref[...] = acc + res_ref[...]


def _out_proj(o_ret, o_rwkv, o_nsa, w, res):
    m = o_ret.shape[0]
    kd, n = w.shape
    assert kd == o_ret.shape[1] + o_rwkv.shape[1] + o_nsa.shape[1]
    tm = _pick(m, (1024, 512, 256, 128))
    tn = _pick(n, (1024, 512, 256, 128))

    def rows(a):
        return pl.BlockSpec((tm, a.shape[1]), lambda i, j: (i, 0))

    return pl.pallas_call(
        _out_proj_kernel,
        out_shape=jax.ShapeDtypeStruct((m, n), F32),
        grid=(m // tm, n // tn),
        in_specs=[rows(o_ret), rows(o_rwkv), rows(o_nsa),
                  pl.BlockSpec((kd, tn), lambda i, j: (0, j)),
                  pl.BlockSpec((tm, tn), lambda i, j: (i, j))],
        out_specs=pl.BlockSpec((tm, tn), lambda i, j: (i, j)),
        compiler_params=_cparams(("parallel", "parallel")),
        name="out_proj",
    )(o_ret, o_rwkv, o_nsa, w, res)


RET_CHUNK = 256


def _retention_kernel(q_ref, k_ref, v_ref, g_ref, cos_ref, sin_ref, dmat_ref, qd_ref, kd_ref, cd_ref, o_ref, state_ref):
    @pl.when(pl.program_id(0) == 0)
    def _():
        state_ref[...] = jnp.zeros_like(state_ref)

    cos = cos_ref[...]
    sin = sin_ref[...]

    def rope(x):
        return x * cos + pltpu.roll(x, RET_DH // 2, axis=1) * sin

    heads = range(RET_HEADS)
    hs = [slice(h * RET_DH, (h + 1) * RET_DH) for h in heads]
    q_all, k_all, v_all, gate = q_ref[...], k_ref[...], v_ref[...], g_ref[...]
    qb = [rope(q_all[:, s]).astype(BF16) for s in hs]
    kf = [rope(k_all[:, s]) * (RET_DH ** -0.5) for s in hs]
    kb = [kf[h].astype(BF16) for h in heads]
    vb = [v_all[:, s].astype(BF16) for s in hs]
    inner = [(_dot_t(qb[h], kb[h]) * dmat_ref[h]).astype(BF16) for h in heads]
    cross = [_dot(qb[h], state_ref[h].astype(BF16)) * qd_ref[h] for h in heads]
    kv = [_dot_tl((kf[h] * kd_ref[h]).astype(BF16), vb[h]) for h in heads]
    outs = []
    for h in heads:
        out = _dot(inner[h], vb[h]) + cross[h]
        state_ref[h] = cd_ref[h] * state_ref[h] + kv[h]
        outs.append(out * lax.rsqrt(jnp.mean(out * out, axis=-1, keepdims=True) + RMS_EPS))
    o_ref[...] = (gate * jax.nn.sigmoid(gate) * jnp.concatenate(outs, axis=1)).astype(o_ref.dtype)


def _retention(proj, cos, sin):
    t = proj.shape[0]
    chunk = RET_CHUNK
    log_gamma = np.log(1.0 - 2.0 ** (-5.0 - np.arange(RET_HEADS, dtype=np.float64)))[:, None, None]
    n = np.arange(chunk, dtype=np.float64)
    lag = n[:, None] - n[None, :]
    dmat = np.where(lag >= 0, np.exp(np.maximum(lag, 0.0)[None] * log_gamma), 0.0)
    ones = np.ones((1, 1, RET_DH))
    qd = np.exp((n + 1.0)[None, :, None] * log_gamma) * ones
    kd = np.exp((chunk - 1.0 - n)[None, :, None] * log_gamma) * ones
    cd = np.exp(chunk * log_gamma) * ones
    tables = [jnp.asarray(a, F32) for a in (dmat, qd, kd, cd)]
    base = C_RET // RET_W

    def col(off):
        return pl.BlockSpec((chunk, RET_W), lambda c, off=off: (c, base + off))

    def full(a):
        return pl.BlockSpec(a.shape, lambda c: (0, 0, 0))

    tab = pl.BlockSpec((chunk, RET_DH), lambda c: (c, 0))
    return pl.pallas_call(
        _retention_kernel,
        out_shape=jax.ShapeDtypeStruct((t, RET_W), BF16),
        grid=(t // chunk,),
        in_specs=[col(0), col(1), col(2), col(3), tab, tab] + [full(a) for a in tables],
        out_specs=pl.BlockSpec((chunk, RET_W), lambda c: (c, 0)),
        scratch_shapes=[pltpu.VMEM((RET_HEADS, RET_DH, RET_DH), F32)],
        compiler_params=_cparams(("arbitrary",)),
        name="retention",
    )(proj, proj, proj, proj, cos, sin, *tables)


def _dot_f32(a, b):
    a_hi = a.astype(BF16)
    b_hi = b.astype(BF16)
    a_lo = (a - a_hi.astype(F32)).astype(BF16)
    b_lo = (b - b_hi.astype(F32)).astype(BF16)
    return _dot(a_hi, b_hi) + (_dot(a_hi, b_lo) + _dot(a_lo, b_hi))


RW_PRE_TM = 256
_MIX_OFF = (0, 512, 1024, 1536, 1664, 1792, 2048, 2176)


def _rwkv_pre_kernel(r_ref, k_ref, v_ref, xw_ref, xa_ref, xg_ref, xv_ref,
                     rp_ref, kp_ref, vp_ref, xwp_ref, xap_ref, xgp_ref, xvp_ref,
                     mix_ref, w0_ref, w2_ref, a0_ref, a2_ref, v0_ref, v2_ref, g2_ref, kk_ref, ka_ref,
                     gsum_ref, vfirst_ref,
                     r_out, lw_out, k_out, v_out, an_out, b_out, g_out, *, use_vres):
    i = pl.program_id(0)

    def mixed(cur_ref, prev_ref, seg):
        cur = cur_ref[...]
        mix = mix_ref[:, _MIX_OFF[seg]:_MIX_OFF[seg + 1]]
        prev_row = jnp.where(i > 0, prev_ref[7:8, :], 0.0)
        rid = lax.broadcasted_iota(jnp.int32, cur.shape, 0)
        shifted = jnp.where(rid == 0, prev_row, pltpu.roll(cur, 1, axis=0))
        return cur + mix * (shifted - cur)

    r = mixed(r_ref, rp_ref, 0)
    k = mixed(k_ref, kp_ref, 1)
    v = mixed(v_ref, vp_ref, 2)
    xw = mixed(xw_ref, xwp_ref, 3)
    xa = mixed(xa_ref, xap_ref, 4)
    xg = mixed(xg_ref, xgp_ref, 5)

    z = -(w0_ref[...] + _dot_f32(jnp.tanh(xw), w2_ref[...]))
    softplus = jnp.maximum(z, 0.0) + jnp.log(1.0 + jnp.exp(-jnp.abs(z)))
    w = -softplus - 0.5
    lw_out[...] = -jnp.exp(w)
    a = jax.nn.sigmoid(a0_ref[...] + _dot_f32(xa, a2_ref[...]))
    g_out[...] = _dot_f32(jax.nn.sigmoid(xg), g2_ref[...])
    if use_vres:
        xv = mixed(xv_ref, xvp_ref, 6)
        v = v + (vfirst_ref[...] - v) * jax.nn.sigmoid(v0_ref[...] + _dot_f32(xv, v2_ref[...]))
    kk = k * kk_ref[...]
    ss = _split_dot(kk * kk, gsum_ref[...], 3)
    kk = kk / jnp.maximum(jnp.sqrt(ss), 1e-12)
    r_out[...] = r
    k_out[...] = k * (1.0 + (a - 1.0) * ka_ref[...])
    v_out[...] = v
    an_out[...] = -kk
    b_out[...] = kk * a


def _rwkv_pre(proj, mix, w0, w2, a0, a2, v0, v2, g2, k_k, k_a, gsum, v_first, use_vres):
    t = proj.shape[0]
    tm = RW_PRE_TM

    def cur(width, off):
        return pl.BlockSpec((tm, width), lambda i: (i, off // width))

    def prev(width, off):
        return pl.BlockSpec((8, width), lambda i: (jnp.maximum(i * (tm // 8) - 1, 0), off // width))

    def full(a):
        return pl.BlockSpec(a.shape, lambda i: (0,) * a.ndim)

    segs = [(RW_W, C_RW_RKV), (RW_W, C_RW_RKV + RW_W), (RW_W, C_RW_RKV + 2 * RW_W),
            (128, C_RW_XW), (128, C_RW_XA), (256, C_RW_XG), (128, C_RW_XV)]
    params = [mix, w0, w2, a0, a2, v0, v2, g2, k_k, k_a, gsum]
    row = pl.BlockSpec((tm, RW_W), lambda i: (i, 0))
    outs = pl.pallas_call(
        functools.partial(_rwkv_pre_kernel, use_vres=use_vres),
        out_shape=[jax.ShapeDtypeStruct((t, RW_W), F32)] * 7,
        grid=(t // tm,),
        in_specs=[cur(w, o) for w, o in segs] + [prev(w, o) for w, o in segs] + [full(p) for p in params] + [row],
        out_specs=[row] * 7,
        compiler_params=_cparams(("parallel",)),
        name="rwkv_pre",
    )(*([proj] * 14), *params, v_first)
    return outs


RW_CHUNK = 128


def _wkv_kernel(r_ref, lw_ref, k_ref, v_ref, an_ref, b_ref, g_ref, lng_ref, lnb_ref, rk_ref,
                gsum_ref, tril_ref, o_ref, s_ref):
    c = pl.program_id(0)
    C = RW_CHUNK
    N = RW_N

    @pl.when(c == 0)
    def _():
        s_ref[...] = jnp.zeros_like(s_ref)

    r = r_ref[...]
    lw = lw_ref[...]
    k = k_ref[...]
    v = v_ref[...]
    lw_hi = lw.astype(BF16)
    rem = lw - lw_hi.astype(F32)
    lw_mid = rem.astype(BF16)
    lw_lo = (rem - lw_mid.astype(F32)).astype(BF16)
    tril = tril_ref[...]
    lg = _dot(tril, lw_hi) + (_dot(tril, lw_mid) + _dot(tril, lw_lo))
    gam = jnp.exp(lg)
    ginv = jnp.exp(-lg)
    at = (an_ref[...] * jnp.exp(lg - lw)).astype(BF16)
    bt = (b_ref[...] * ginv).astype(BF16)
    kt = (k * ginv).astype(BF16)
    rt_f = r * gam
    rt = rt_f.astype(BF16)
    vb = v.astype(BF16)
    g_last = gam[C - 1:C, :]

    rowi = lax.broadcasted_iota(jnp.int32, (C, C), 0)
    coli = lax.broadcasted_iota(jnp.int32, (C, C), 1)
    strict = rowi > coli
    incl = rowi >= coli

    heads = range(RW_HEADS)
    hs = [slice(h * N, (h + 1) * N) for h in heads]
    bth = [bt[:, s] for s in hs]
    kth = [kt[:, s] for s in hs]
    vh = [vb[:, s] for s in hs]
    big = [_dot_t(jnp.concatenate([at[:, s], rt[:, s]], axis=0), jnp.concatenate([bth[h], kth[h]], axis=0))
           for h, s in zip(heads, hs)]
    a_ab = [jnp.where(strict, big[h][:C, :C], 0.0) for h in heads]
    a_ak = [jnp.where(strict, big[h][:C, C:], 0.0).astype(BF16) for h in heads]
    a_rb = [jnp.where(incl, big[h][C:, :C], 0.0).astype(BF16) for h in heads]
    a_rk = [jnp.where(incl, big[h][C:, C:], 0.0).astype(BF16) for h in heads]
    akv = [_dot(a_ak[h], vh[h]) for h in heads]
    def same_block(size):
        shift = size.bit_length() - 1
        return (rowi >> shift) == (coli >> shift)

    tinv = [jnp.where(same_block(2), a_ab[h], 0.0) + jnp.where(rowi == coli, 1.0, 0.0) for h in heads]
    size = 2
    while size < C:
        lower_left = same_block(2 * size) & jnp.logical_not(same_block(size))
        off = [jnp.where(lower_left, a_ab[h], 0.0).astype(BF16) for h in heads]
        tb = [tinv[h].astype(BF16) for h in heads]
        half = [_dot(tb[h], off[h]).astype(BF16) for h in heads]
        tinv = [tinv[h] + _dot(half[h], tb[h]) for h in heads]
        size *= 2
    xb = [_dot(tinv[h].astype(BF16), jnp.concatenate([at[:, hs[h]], akv[h].astype(BF16)], axis=1)).astype(BF16)
          for h in heads]
    yx = [_dot(a_rb[h], xb[h]) for h in heads]
    ykv = [_dot(a_rk[h], vh[h]) for h in heads]
    xtb = [_dot_tl(xb[h], bth[h]) for h in heads]
    vtk = [_dot_tl(vh[h], kth[h]) for h in heads]
    ys = []
    for h in heads:
        gl = g_last[:, hs[h]]
        y1 = (rt_f[:, hs[h]] + yx[h][:, :N]).astype(BF16)
        s0 = s_ref[h]
        s0b = s0.astype(BF16)
        ys.append(_dot_t(y1, s0b) + (yx[h][:, N:] + ykv[h]))
        s_ref[h] = (s0 + _dot(s0b, xtb[h][:N].astype(BF16)) + (xtb[h][N:] + vtk[h])) * gl
    y = jnp.concatenate(ys, axis=1)

    gsum = gsum_ref[...]
    inv_n = 1.0 / N
    mu = _split_dot(y, gsum, 3) * inv_n
    yc = y - mu
    var = _split_dot(yc * yc, gsum, 3) * inv_n
    yn = yc * lax.rsqrt(var + RW_GN_EPS) * lng_ref[...] + lnb_ref[...]
    bonus = _split_dot(r * k * rk_ref[...], gsum, 3)
    o_ref[...] = ((yn + bonus * v) * g_ref[...]).astype(o_ref.dtype)


def _wkv(r, lw, k, v, an, b, g, ln_g, ln_b, r_k, gsum):
    t = r.shape[0]
    C = RW_CHUNK
    tril = jnp.asarray(np.tril(np.ones((C, C), np.float32)), BF16)
    row = pl.BlockSpec((C, RW_W), lambda c: (c, 0))
    vec = pl.BlockSpec((1, RW_W), lambda c: (0, 0))
    return pl.pallas_call(
        _wkv_kernel,
        out_shape=jax.ShapeDtypeStruct((t, RW_W), BF16),
        grid=(t // C,),
        in_specs=[row] * 7 + [vec] * 3 + [pl.BlockSpec((RW_W, RW_W), lambda c: (0, 0)),
                                          pl.BlockSpec((C, C), lambda c: (0, 0))],
        out_specs=row,
        scratch_shapes=[pltpu.VMEM((RW_HEADS, RW_N, RW_N), F32)],
        compiler_params=_cparams(("arbitrary",)),
        name="wkv7",
    )(r, lw, k, v, an, b, g, ln_g, ln_b, r_k, gsum, tril)


NSA_PREP_TM = 512


N_NORM_SLABS = (NSA_W + 2 * NSA_KV_W) // LANES
N_RAW_SLABS = 4 * NSA_KV_W // LANES
N_NSA_SLABS = N_NORM_SLABS + N_RAW_SLABS + 1


def _nsa_prep_kernel(*refs):
    x_refs = refs[:N_NSA_SLABS]
    cos_ref, sin_ref, w_ref, gmean_ref, qk_ref, hmat_ref, v1_ref, gate_ref = refs[N_NSA_SLABS:]
    cos = cos_ref[...]
    sin = sin_ref[...]
    gmean = gmean_ref[...]
    half = NSA_DH // 2
    lane = lax.broadcasted_iota(jnp.int32, cos.shape, 1)
    first_half = (lane % NSA_DH) < half
    low = lane < NSA_DH
    for s in range(N_NORM_SLABS):
        x = x_refs[s][...]
        y = x * lax.rsqrt(_split_dot(x * x, gmean, 3) + RMS_EPS) * w_ref[s]
        rot = jnp.where(first_half, pltpu.roll(y, LANES - half, axis=1), pltpu.roll(y, half, axis=1))
        res = (y * cos + rot * sin).astype(qk_ref.dtype)
        qk_ref[2 * s] = res[:, :NSA_DH]
        qk_ref[2 * s + 1] = res[:, NSA_DH:]

    rows = hmat_ref.shape[2]
    low_r = lax.broadcasted_iota(jnp.int32, (rows, LANES), 1) < NSA_DH
    for s in range(NSA_KV_W // LANES * 2):
        kind, pair = divmod(s, NSA_KV_W // LANES)
        x_ref = x_refs[N_NORM_SLABS + s]
        for t2 in range(CMP_STRIDE // 2):
            r0 = x_ref[pl.ds(2 * t2, rows, stride=CMP_STRIDE), :]
            r1 = x_ref[pl.ds(2 * t2 + 1, rows, stride=CMP_STRIDE), :]
            cols = slice(t2 * LANES, (t2 + 1) * LANES)
            hmat_ref[kind, 2 * pair, :, cols] = jnp.where(low_r, r0, pltpu.roll(r1, NSA_DH, axis=1)).astype(BF16)
            hmat_ref[kind, 2 * pair + 1, :, cols] = jnp.where(low_r, pltpu.roll(r0, NSA_DH, axis=1), r1).astype(BF16)

    one_col = jnp.where(lane == NSA_DH, 1.0, 0.0)
    for s in range(NSA_KV_W // LANES * 2):
        x = x_refs[N_NORM_SLABS + N_RAW_SLABS // 2 + s][...]
        v1_ref[2 * s] = jnp.where(low, x, one_col).astype(BF16)
        v1_ref[2 * s + 1] = jnp.where(low, pltpu.roll(x, NSA_DH, axis=1), one_col).astype(BF16)

    xg = x_refs[N_NSA_SLABS - 1][...]
    per_group = 3 * NSA_HG
    for g in range(NSA_G):
        shifted = xg if g == 0 else pltpu.roll(xg, LANES - per_group * g, axis=1)
        gate_ref[g] = jnp.where(lane < per_group, shifted, 0.0)


def _nsa_prep(proj, w_slabs, cos, sin, gmean):
    t = proj.shape[0]
    tm = NSA_PREP_TM
    base = C_NSA_Q // LANES
    tab = pl.BlockSpec((tm, LANES), lambda i: (i, 0))
    n_heads_out = 2 * N_NORM_SLABS
    return pl.pallas_call(
        _nsa_prep_kernel,
        out_shape=[jax.ShapeDtypeStruct((n_heads_out, t, NSA_DH), BF16),
                   jax.ShapeDtypeStruct((2, NSA_G, t // CMP_STRIDE, CMP_STRIDE * NSA_DH), BF16),
                   jax.ShapeDtypeStruct((2 * NSA_G, t, LANES), BF16),
                   jax.ShapeDtypeStruct((NSA_G, t, LANES), F32)],
        grid=(t // tm,),
        in_specs=[pl.BlockSpec((tm, LANES), lambda i, s=s: (i, base + s)) for s in range(N_NSA_SLABS)]
        + [tab, tab, pl.BlockSpec((N_NORM_SLABS, 1, LANES), lambda i: (0, 0, 0)),
           pl.BlockSpec((LANES, LANES), lambda i: (0, 0))],
        out_specs=[pl.BlockSpec((n_heads_out, tm, NSA_DH), lambda i: (0, i, 0)),
                   pl.BlockSpec((2, NSA_G, tm // CMP_STRIDE, CMP_STRIDE * NSA_DH), lambda i: (0, 0, i, 0)),
                   pl.BlockSpec((2 * NSA_G, tm, LANES), lambda i: (0, i, 0)),
                   pl.BlockSpec((NSA_G, tm, LANES), lambda i: (0, i, 0))],
        compiler_params=_cparams(("parallel",)),
        name="nsa_prep",
    )(*([proj] * N_NSA_SLABS), cos, sin, w_slabs, gmean)


def _compress_kernel(h_ref, w1_ref, w2_ref, pe_ref, nw_ref, cos_ref, sin_ref, o_ref):
    kind = pl.program_id(0)
    nc = h_ref.shape[2]
    half_in = CMP_STRIDE * NSA_DH
    hm = h_ref[0, 0]
    w1 = w1_ref[0]
    first = _dot(hm, w1[:half_in])
    second = _dot(hm, w1[half_in:])
    const = _dot(pe_ref[0], w1)[0:1, :]
    pre = first + pltpu.roll(second, nc - 1, axis=0) + const
    hid = 0.5 * pre * (1.0 + jnp.tanh(0.7978845608028654 * (pre + 0.044715 * pre * pre * pre)))
    out = _dot(hid.astype(BF16), w2_ref[0])
    y = out * lax.rsqrt(jnp.mean(out * out, axis=-1, keepdims=True) + RMS_EPS) * nw_ref[...]
    half = NSA_DH // 2
    rot = jnp.concatenate([y[:, half:], y[:, :half]], axis=1)
    roped = y * cos_ref[...] + rot * sin_ref[...]
    o_ref[0, 0] = jnp.where(kind == 0, roped, out).astype(o_ref.dtype)


def _compress(hmat, w1, w2, pe, nw, cos_c, sin_c):
    _, g, nc, width = hmat.shape
    return pl.pallas_call(
        _compress_kernel,
        out_shape=jax.ShapeDtypeStruct((2, g, nc, NSA_DH), BF16),
        grid=(2, g),
        in_specs=[pl.BlockSpec((1, 1, nc, width), lambda a, b: (a, b, 0, 0)),
                  pl.BlockSpec((1, 2 * width, CMP_HIDDEN), lambda a, b: (a, 0, 0)),
                  pl.BlockSpec((1, CMP_HIDDEN, NSA_DH), lambda a, b: (a, 0, 0)),
                  pl.BlockSpec((1, 8, 2 * width), lambda a, b: (a, 0, 0)),
                  pl.BlockSpec((1, NSA_DH), lambda a, b: (0, 0)),
                  pl.BlockSpec((nc, NSA_DH), lambda a, b: (0, 0)),
                  pl.BlockSpec((nc, NSA_DH), lambda a, b: (0, 0))],
        out_specs=pl.BlockSpec((1, 1, nc, NSA_DH), lambda a, b: (a, b, 0, 0)),
        compiler_params=_cparams(("parallel", "parallel")),
        name="nsa_compress",
    )(hmat, w1, w2, pe, nw, cos_c, sin_c)


SLC_KT = 1024
WIN_KEYS = WINDOW + Q_BLOCK


INT32_MIN = -2 ** 31


def _nsa_attn_kernel(q_ref, gate_ref, kc_ref, vc_ref, ks_ref, vs_ref, kw_ref, vw_ref, cis_ref, expand_ref, ltri_ref,
                     o_ref):
    qi = pl.program_id(1)
    nc = kc_ref.shape[2]
    nb = cis_ref.shape[1]
    heads = range(NSA_HG)
    start = qi * Q_BLOCK
    qh = [q_ref[hh] for hh in heads]
    tok = start + lax.broadcasted_iota(jnp.int32, (Q_BLOCK, 1), 0)

    def softmax_terms(scores):
        return [jnp.exp(s - jnp.max(s, axis=-1, keepdims=True)) for s in scores]

    kc = kc_ref[0, 0]
    vc = vc_ref[0, 0]
    cend = lax.broadcasted_iota(jnp.int32, (1, nc), 1) * CMP_STRIDE + (CMP_BLOCK - 1)
    bias_c = jnp.where(cend <= tok, 0.0, NEG_INF)
    ec = softmax_terms([_dot_t(qh[hh], kc) + bias_c for hh in heads])
    sees_any = tok >= CMP_BLOCK - 1
    pc = [ec[hh] * jnp.where(sees_any, 1.0 / jnp.sum(ec[hh], axis=-1, keepdims=True), 0.0) for hh in heads]
    o_c = [_dot(pc[hh].astype(BF16), vc) for hh in heads]

    psum = pc[0]
    for hh in range(1, NSA_HG):
        psum = psum + pc[hh]
    score = _split_dot(psum, cis_ref[...], 3)
    blk = lax.broadcasted_iota(jnp.int32, (1, nb), 1)
    cur = tok // SLC_BLOCK
    forced = (blk == 0) | (blk == cur) | (blk == cur - 1)
    score = jnp.where(forced, FORCED_SCORE, score)
    score = jnp.where(blk <= cur, score, -jnp.inf)

    kbase = pl.multiple_of(jnp.maximum(start - WINDOW, 0), Q_BLOCK)
    kw = kw_ref[0, pl.ds(kbase, WIN_KEYS), :]
    lag = tok - (kbase + lax.broadcasted_iota(jnp.int32, (1, WIN_KEYS), 1))
    bias_w = jnp.where((lag >= 0) & (lag < WINDOW), 0.0, NEG_INF)
    sw = [_dot_t(qh[hh], kw) for hh in heads]

    bits = lax.bitcast_convert_type(score.T, jnp.int32)
    key = bits ^ ((bits >> 31) & 0x7FFFFFFF)
    n_sel = min(SLC_TOPK, nb)

    def enough(c):
        return jnp.sum(jnp.where(key >= c, 1.0, 0.0), axis=0, keepdims=True) >= n_sel

    def enough3(c1, c2, c3):
        packed = jnp.where(key >= c3, 65793.0, jnp.where(key >= c2, 257.0, jnp.where(key >= c1, 1.0, 0.0)))
        tot = jnp.sum(packed, axis=0, keepdims=True).astype(jnp.int32)
        return (tot & 255) >= n_sel, ((tot >> 8) & 255) >= n_sel, (tot >> 16) >= n_sel

    zero_row = jnp.zeros((1, Q_BLOCK), jnp.int32)
    thr = jnp.where(enough(zero_row), zero_row, INT32_MIN)
    for hi in range(30, 0, -2):
        c1 = thr + (1 << (hi - 1))
        c2 = thr + (1 << hi)
        c3 = c2 + (1 << (hi - 1))
        e1, e2, e3 = enough3(c1, c2, c3)
        thr = jnp.where(e3, c3, jnp.where(e2, c2, jnp.where(e1, c1, thr)))
    c1 = thr + 1
    thr = jnp.where(enough(c1), c1, thr)
    above = key > thr
    tied = key == thr
    need = n_sel - jnp.sum(jnp.where(above, 1.0, 0.0), axis=0, keepdims=True)
    tied_before = _dot(ltri_ref[...], jnp.where(tied, 1.0, 0.0).astype(BF16))
    sel_t = above | (tied & (tied_before < need))
    sel = jnp.where(sel_t, 1.0, 0.0).T.astype(BF16)

    ew = softmax_terms([sw[hh] + bias_w for hh in heads])
    acc_w = [_dot(ew[hh].astype(BF16), vw_ref[0, pl.ds(kbase, WIN_KEYS), :]) for hh in heads]

    n_tiles = (start + Q_BLOCK + SLC_KT - 1) // SLC_KT

    def sel_body(j, carry):
        ms, accs = carry
        k0 = pl.multiple_of(j * SLC_KT, SLC_KT)
        kt = ks_ref[0, pl.ds(k0, SLC_KT), :]
        vt = vs_ref[0, pl.ds(k0, SLC_KT), :]
        member = _dot(sel, expand_ref[:, pl.ds(k0, SLC_KT)])
        kpos = k0 + lax.broadcasted_iota(jnp.int32, (1, SLC_KT), 1)
        bias = jnp.where((member > 0.5) & (kpos <= tok), 0.0, NEG_INF)
        sj = [_dot_t(qh[hh], kt) + bias for hh in heads]
        m_new = [jnp.maximum(ms[hh], jnp.max(sj[hh], axis=-1, keepdims=True)) for hh in heads]
        pj = [jnp.exp((sj[hh] - m_new[hh]).astype(BF16)) for hh in heads]
        accs = [jnp.exp(ms[hh] - m_new[hh]) * accs[hh] + _dot(pj[hh], vt) for hh in heads]
        return tuple(m_new), tuple(accs)

    m0 = tuple(jnp.full((Q_BLOCK, 1), NEG_INF, F32) for _ in heads)
    a0 = tuple(jnp.zeros((Q_BLOCK, LANES), F32) for _ in heads)
    _, acc_s = lax.fori_loop(0, n_tiles, sel_body, (m0, a0))

    gt = jax.nn.sigmoid(gate_ref[0])
    outs = []
    for hh in heads:
        g0, g1, g2 = (gt[:, 3 * hh + br:3 * hh + br + 1] for br in range(3))
        scale_s = g1 / acc_s[hh][:, NSA_DH:NSA_DH + 1]
        scale_w = g2 / acc_w[hh][:, NSA_DH:NSA_DH + 1]
        outs.append(g0 * o_c[hh] + scale_s * acc_s[hh][:, :NSA_DH] + scale_w * acc_w[hh][:, :NSA_DH])
    o_ref[...] = jnp.concatenate(outs, axis=1).astype(o_ref.dtype)


def _nsa_attn(qk_hm, gates, cmp_kv, v1, cis, expand, ltri):
    t = qk_hm.shape[1]
    nq = t // Q_BLOCK
    nc = cmp_kv.shape[2]
    nb = cis.shape[1]
    ks_spec = pl.BlockSpec((1, t, NSA_DH), lambda g, i: (NSA_HEADS + g, 0, 0))
    kw_spec = pl.BlockSpec((1, t, NSA_DH), lambda g, i: (NSA_HEADS + NSA_G + g, 0, 0))
    vs_spec = pl.BlockSpec((1, t, LANES), lambda g, i: (g, 0, 0))
    vw_spec = pl.BlockSpec((1, t, LANES), lambda g, i: (NSA_G + g, 0, 0))
    return pl.pallas_call(
        _nsa_attn_kernel,
        out_shape=jax.ShapeDtypeStruct((t, NSA_W), BF16),
        grid=(NSA_G, nq),
        in_specs=[pl.BlockSpec((NSA_HG, Q_BLOCK, NSA_DH), lambda g, i: (g, i, 0)),
                  pl.BlockSpec((1, Q_BLOCK, LANES), lambda g, i: (g, i, 0)),
                  pl.BlockSpec((1, 1, nc, NSA_DH), lambda g, i: (0, g, 0, 0)),
                  pl.BlockSpec((1, 1, nc, NSA_DH), lambda g, i: (1, g, 0, 0)),
                  ks_spec, vs_spec, kw_spec, vw_spec,
                  pl.BlockSpec((nc, nb), lambda g, i: (0, 0)),
                  pl.BlockSpec((nb, t), lambda g, i: (0, 0)),
                  pl.BlockSpec((nb, nb), lambda g, i: (0, 0))],
        out_specs=pl.BlockSpec((Q_BLOCK, NSA_HG * NSA_DH), lambda g, i: (i, g)),
        compiler_params=_cparams(("parallel", "arbitrary")),
        name="nsa_attention",
    )(qk_hm, gates, cmp_kv, cmp_kv, qk_hm, v1, qk_hm, v1, cis, expand, ltri)


def _rope_tables(pos, dh, reps):
    half = dh // 2
    inv_freq = ROPE_THETA ** (-jnp.arange(half, dtype=F32) / half)
    ang = pos.astype(F32)[:, None] * inv_freq[None, :]
    cos = jnp.cos(ang)
    sin = jnp.sin(ang)
    cos_t = jnp.tile(jnp.concatenate([cos, cos], axis=1), (1, reps))
    sin_t = jnp.tile(jnp.concatenate([-sin, sin], axis=1), (1, reps))
    return cos_t, sin_t


def _pad_cols(a, width):
    return jnp.pad(a, ((0, 0), (0, width - a.shape[1])))


def _pad_rows(a, height):
    return jnp.pad(a, ((0, height - a.shape[0]), (0, 0)))


def _pack_w_in(w_l, w_vres):
    d = w_l.shape[0]
    o = 0
    ret = w_l[:, o:o + 4 * RET_W]; o += 4 * RET_W
    rkv = w_l[:, o:o + 3 * RW_W]; o += 3 * RW_W
    xw = w_l[:, o:o + RW_DECAY_RANK]; o += RW_DECAY_RANK
    xa = w_l[:, o:o + RW_A_RANK]; o += RW_A_RANK
    xg = w_l[:, o:o + RW_GATE_RANK]; o += RW_GATE_RANK
    q = w_l[:, o:o + NSA_W]; o += NSA_W
    kc, vc, ks, vs, kw, vw = (w_l[:, o + i * NSA_KV_W:o + (i + 1) * NSA_KV_W] for i in range(6))
    o += 6 * NSA_KV_W
    gates = w_l[:, o:o + 3 * NSA_HEADS]
    xv = jnp.zeros((d, LANES), w_l.dtype) if w_vres is None else _pad_cols(w_vres, LANES)
    return jnp.concatenate([ret, rkv, _pad_cols(xw, LANES), _pad_cols(xa, LANES), xg, xv,
                            q, ks, kw, kc, vc, vs, vw, _pad_cols(gates, LANES)], axis=1)


def _pack_mix(mix, vres_mix):
    o = 3 * RW_W
    xw = mix[o:o + RW_DECAY_RANK]; o += RW_DECAY_RANK
    xa = mix[o:o + RW_A_RANK]; o += RW_A_RANK
    xg = mix[o:o + RW_GATE_RANK]
    z = lambda n: jnp.zeros((n,), mix.dtype)
    xv = z(LANES) if vres_mix is None else jnp.concatenate([vres_mix, z(LANES - RW_V_RANK)])
    return jnp.concatenate([mix[:3 * RW_W], xw, z(LANES - RW_DECAY_RANK), xa, z(LANES - RW_A_RANK), xg, xv])[None, :]


def kernel(x, ln1_g, w_in, w_in_vres, rwkv_mix, rwkv_vres_mix, rwkv_w0, rwkv_w2, rwkv_a0, rwkv_a2, rwkv_v0, rwkv_v2, rwkv_g2, rwkv_k_k, rwkv_k_a, rwkv_r_k, rwkv_ln_g, rwkv_ln_b, nsa_q_norm, nsa_k_norm, nsa_cmp_pe, nsa_cmp_k_w1, nsa_cmp_k_w2, nsa_cmp_v_w1, nsa_cmp_v_w2, w_out, ln2_g, w_up, w_down):
    bsz, t, d = x.shape
    assert bsz == 1
    depth = w_in.shape[0]
    nc = t // CMP_STRIDE
    nb = t // SLC_BLOCK
    xs = x.reshape(t, d)

    pos = jnp.arange(t)
    cos_r, sin_r = _rope_tables(pos, RET_DH, 1)
    cos_n, sin_n = _rope_tables(pos, NSA_DH, 2)
    cos_c, sin_c = _rope_tables(jnp.arange(nc) * CMP_STRIDE + (CMP_BLOCK - 1), NSA_DH, 1)
    lane_head = np.arange(RW_W) // RW_N
    gsum = jnp.asarray(lane_head[:, None] == lane_head[None, :], BF16)
    lane_h2 = np.arange(LANES) // NSA_DH
    gmean = jnp.asarray((lane_h2[:, None] == lane_h2[None, :]) / float(NSA_DH), BF16)
    cstart = np.arange(nc) * CMP_STRIDE
    sstart = np.arange(nb) * SLC_BLOCK
    cis = jnp.asarray((cstart[:, None] <= sstart[None, :] + SLC_BLOCK - 1)
                      & (cstart[:, None] + CMP_BLOCK - 1 >= sstart[None, :]), BF16)

    expand = jnp.asarray((np.arange(t)[None, :] // SLC_BLOCK) == np.arange(nb)[:, None], BF16)
    ltri = jnp.asarray(np.tril(np.ones((nb, nb), np.float32), -1), BF16)

    v_first = jnp.zeros((t, RW_W), F32)
    for l in range(depth):
        vres = l > 0
        w_cat = _pack_w_in(w_in[l], w_in_vres[l - 1] if vres else None).astype(BF16)
        proj = _norm_matmul(xs, ln1_g[l], w_cat, name="in_proj")

        o_ret = _retention(proj, cos_r, sin_r)

        row = lambda a: a.reshape(1, -1)
        mix = _pack_mix(rwkv_mix[l], rwkv_vres_mix[l - 1] if vres else None)
        v0 = row(rwkv_v0[l - 1]) if vres else jnp.zeros((1, RW_W), F32)
        v2 = _pad_rows(rwkv_v2[l - 1], LANES) if vres else jnp.zeros((LANES, RW_W), F32)
        r_, lw_, k_, v_, an_, b_, g_ = _rwkv_pre(
            proj, mix, row(rwkv_w0[l]), _pad_rows(rwkv_w2[l], LANES), row(rwkv_a0[l]), _pad_rows(rwkv_a2[l], LANES),
            v0, v2, rwkv_g2[l], row(rwkv_k_k[l]), row(rwkv_k_a[l]), gsum, v_first, vres)
        if not vres:
            v_first = v_
        o_rwkv = _wkv(r_, lw_, k_, v_, an_, b_, g_, row(rwkv_ln_g[l]), row(rwkv_ln_b[l]), row(rwkv_r_k[l]), gsum)

        n_q = NSA_W // LANES
        n_k = NSA_KV_W // LANES
        w_slabs = jnp.stack([jnp.tile(nsa_q_norm[l] * (NSA_DH ** -0.5), 2)] * n_q
                            + [jnp.tile(nsa_k_norm[l, 1], 2)] * n_k + [jnp.tile(nsa_k_norm[l, 2], 2)] * n_k)[:, None, :]
        qk_hm, hmat, v1, gates = _nsa_prep(proj, w_slabs, cos_n, sin_n, gmean)
        w1 = jnp.stack([nsa_cmp_k_w1[l], nsa_cmp_v_w1[l]]).astype(BF16)
        w2 = jnp.stack([nsa_cmp_k_w2[l], nsa_cmp_v_w2[l]]).astype(BF16)
        pe = jnp.broadcast_to(nsa_cmp_pe[l].reshape(2, 1, CMP_BLOCK * NSA_DH), (2, 8, CMP_BLOCK * NSA_DH)).astype(BF16)
        cmp_kv = _compress(hmat, w1, w2, pe, row(nsa_k_norm[l, 0]), cos_c, sin_c)
        o_nsa = _nsa_attn(qk_hm, gates, cmp_kv, v1, cis, expand, ltri)

        xs = _out_proj(o_ret, o_rwkv, o_nsa, w_out[l], xs)

        up = _norm_matmul(xs, ln2_g[l], w_up[l], act="relu2", out_dtype=BF16, name="mlp_up")
        xs = _matmul(up, w_down[l], res=xs, name="mlp_down")
    return xs.reshape(bsz, t, d)
```

```python
import functools

import numpy as np
import jax
import jax.numpy as jnp
from jax import lax
from jax.experimental import pallas as pl
from jax.experimental.pallas import tpu as pltpu

F32 = jnp.float32
BF16 = jnp.bfloat16

D_MODEL = 2048
RET_HEADS, RET_DH = 4, 128
RET_W = RET_HEADS * RET_DH
RW_HEADS, RW_N = 8, 64
RW_W = RW_HEADS * RW_N
RW_DECAY_RANK, RW_A_RANK, RW_V_RANK, RW_GATE_RANK = 96, 96, 64, 256
RW_GN_EPS = 64e-5
NSA_HEADS, NSA_G, NSA_DH = 16, 4, 64
NSA_HG = NSA_HEADS // NSA_G
NSA_W = NSA_HEADS * NSA_DH
NSA_KV_W = NSA_G * NSA_DH
CMP_BLOCK, CMP_STRIDE, CMP_HIDDEN = 32, 16, 256
SLC_BLOCK, SLC_TOPK, WINDOW = 64, 16, 512
NSA_TQ = 256
ROPE_THETA = 10000.0
MLP_HIDDEN = 4 * D_MODEL
RMS_EPS = 1e-6
NEG_INF = -1e30
FORCED_SCORE = 1e9

LANES = 128
VMEM_LIMIT = 56 * 1024 * 1024

C_RET = 0
C_RW_RKV = 2048
C_RW_XW = 3584
C_RW_XA = 3712
C_RW_XG = 3840
C_RW_XV = 4096
C_NSA_Q = 4224
C_NSA_KS = 5248
C_NSA_KW = 5504
C_NSA_KC = 5760
C_NSA_VC = 6016
C_NSA_VS = 6272
C_NSA_VW = 6528
C_NSA_GATE = 6784
PROJ_W = 6912


def _cparams(sem):
    return pltpu.CompilerParams(dimension_semantics=sem, vmem_limit_bytes=VMEM_LIMIT)


def _dot(a, b):
    return jnp.dot(a, b, preferred_element_type=F32)


def _dot_t(a, b):
    return lax.dot_general(a, b, (((1,), (1,)), ((), ())), preferred_element_type=F32)


def _dot_tl(a, b):
    return lax.dot_general(a, b, (((0,), (0,)), ((), ())), preferred_element_type=F32)


def _split_dot(x, m_bf16, passes):
    hi = x.astype(BF16)
    acc = _dot(hi, m_bf16)
    rem = x - hi.astype(F32)
    for _ in range(passes - 1):
        piece = rem.astype(BF16)
        acc = acc + _dot(piece, m_bf16)
        rem = rem - piece.astype(F32)
    return acc


def _mm_kernel(*refs, nk, act, has_res):
    a_ref, b_ref = refs[:2]
    r_ref = refs[2] if has_res else None
    o_ref = refs[3] if has_res else refs[2]

    def finish(y):
        if act == "relu2":
            y = jnp.square(jnp.maximum(y, 0.0))
        if has_res:
            y = y + r_ref[...]
        o_ref[...] = y.astype(o_ref.dtype)

    part = _dot(a_ref[...], b_ref[...].astype(BF16))
    if nk == 1:
        finish(part)
        return
    acc_ref = refs[-1]
    k = pl.program_id(2)

    @pl.when(k == 0)
    def _():
        acc_ref[...] = part

    @pl.when(k > 0)
    def _():
        acc_ref[...] += part

    @pl.when(k == nk - 1)
    def _():
        finish(acc_ref[...])


def _pick(n, cands):
    for c in cands:
        if n % c == 0:
            return c
    raise ValueError(f"no tile for {n}")


def _matmul(a, b, *, res=None, act=None, out_dtype=F32, name="matmul"):
    m, kd = a.shape
    _, n = b.shape
    tm = _pick(m, (1024, 512, 256, 128))
    tn = _pick(n, (1024, 768, 512, 256, 128))
    tk = _pick(kd, (2048, 1024, 512))
    nk = kd // tk
    in_specs = [pl.BlockSpec((tm, tk), lambda i, j, k: (i, k)), pl.BlockSpec((tk, tn), lambda i, j, k: (k, j))]
    args = [a, b]
    if res is not None:
        in_specs.append(pl.BlockSpec((tm, tn), lambda i, j, k: (i, j)))
        args.append(res)
    return pl.pallas_call(
        functools.partial(_mm_kernel, nk=nk, act=act, has_res=res is not None),
        out_shape=jax.ShapeDtypeStruct((m, n), out_dtype),
        grid=(m // tm, n // tn, nk),
        in_specs=in_specs,
        out_specs=pl.BlockSpec((tm, tn), lambda i, j, k: (i, j)),
        scratch_shapes=[pltpu.VMEM((tm, tn), F32)] if nk > 1 else [],
        compiler_params=_cparams(("parallel", "parallel", "arbitrary")),
        name=name,
    )(*args)


def _norm_mm_kernel(x_ref, g_ref, b_ref, o_ref, h_ref, *, act):
    @pl.when(pl.program_id(1) == 0)
    def _():
        x = x_ref[...]
        ms = jnp.mean(x * x, axis=-1, keepdims=True)
        h_ref[...] = (x * lax.rsqrt(ms + RMS_EPS) * g_ref[...]).astype(BF16)

    y = _dot(h_ref[...], b_ref[...].astype(BF16))
    if act == "relu2":
        y = jnp.square(jnp.maximum(y, 0.0))
    o_ref[...] = y.astype(o_ref.dtype)


def _norm_matmul(x, g, b, *, act=None, out_dtype=F32, name="norm_matmul"):
    m, kd = x.shape
    n = b.shape[1]
    tm = _pick(m, (1024, 512, 256, 128))
    tn = _pick(n, (1024, 768, 512, 256, 128))
    return pl.pallas_call(
        functools.partial(_norm_mm_kernel, act=act),
        out_shape=jax.ShapeDtypeStruct((m, n), out_dtype),
        grid=(m // tm, n // tn),
        in_specs=[pl.BlockSpec((tm, kd), lambda i, j: (i, 0)),
                  pl.BlockSpec((1, kd), lambda i, j: (0, 0)),
                  pl.BlockSpec((kd, tn), lambda i, j: (0, j))],
        out_specs=pl.BlockSpec((tm, tn), lambda i, j: (i, j)),
        scratch_shapes=[pltpu.VMEM((tm, kd), BF16)],
        compiler_params=_cparams(("parallel", "arbitrary")),
        name=name,
    )(x, g.reshape(1, kd), b)


def _out_proj_kernel(ret_ref, rw_ref, nsa_ref, w_ref, res_ref, o_ref):
    k0 = ret_ref.shape[1]
    k1 = k0 + rw_ref.shape[1]
    acc = _dot(ret_ref[...], w_ref[:k0, :].astype(BF16))
    acc = acc + _dot(rw_ref[...], w_ref[k0:k1, :].astype(BF16))
    acc = acc + _dot(nsa_ref[...], w_ref[k1:, :].astype(BF16))
    o_ref[...] = acc + res_ref[...]


def _out_proj(o_ret, o_rwkv, o_nsa, w, res):
    m = o_ret.shape[0]
    kd, n = w.shape
    assert kd == o_ret.shape[1] + o_rwkv.shape[1] + o_nsa.shape[1]
    tm = _pick(m, (1024, 512, 256, 128))
    tn = _pick(n, (1024, 512, 256, 128))

    def rows(a):
        return pl.BlockSpec((tm, a.shape[1]), lambda i, j: (i, 0))

    return pl.pallas_call(
        _out_proj_kernel,
        out_shape=jax.ShapeDtypeStruct((m, n), F32),
        grid=(m // tm, n // tn),
        in_specs=[rows(o_ret), rows(o_rwkv), rows(o_nsa),
                  pl.BlockSpec((kd, tn), lambda i, j: (0, j)),
                  pl.BlockSpec((tm, tn), lambda i, j: (i, j))],
        out_specs=pl.BlockSpec((tm, tn), lambda i, j: (i, j)),
        compiler_params=_cparams(("parallel", "parallel")),
        name="out_proj",
    )(o_ret, o_rwkv, o_nsa, w, res)


RET_CHUNK = 256


def _retention_kernel(q_ref, k_ref, v_ref, g_ref, cos_ref, sin_ref, dmat_ref, qd_ref, kd_ref, cd_ref, o_ref, state_ref):
    @pl.when(pl.program_id(0) == 0)
    def _():
        state_ref[...] = jnp.zeros_like(state_ref)

    cos = cos_ref[...]
    sin = sin_ref[...]

    def rope(x):
        return x * cos + pltpu.roll(x, RET_DH // 2, axis=1) * sin

    heads = range(RET_HEADS)
    hs = [slice(h * RET_DH, (h + 1) * RET_DH) for h in heads]
    q_all, k_all, v_all, gate = q_ref[...], k_ref[...], v_ref[...], g_ref[...]
    qb = [rope(q_all[:, s]).astype(BF16) for s in hs]
    kf = [rope(k_all[:, s]) * (RET_DH ** -0.5) for s in hs]
    kb = [kf[h].astype(BF16) for h in heads]
    vb = [v_all[:, s].astype(BF16) for s in hs]
    inner = [(_dot_t(qb[h], kb[h]) * dmat_ref[h]).astype(BF16) for h in heads]
    cross = [_dot(qb[h], state_ref[h].astype(BF16)) * qd_ref[h] for h in heads]
    kv = [_dot_tl((kf[h] * kd_ref[h]).astype(BF16), vb[h]) for h in heads]
    outs = []
    for h in heads:
        out = _dot(inner[h], vb[h]) + cross[h]
        state_ref[h] = cd_ref[h] * state_ref[h] + kv[h]
        outs.append(out * lax.rsqrt(jnp.mean(out * out, axis=-1, keepdims=True) + RMS_EPS))
    o_ref[...] = (gate * jax.nn.sigmoid(gate) * jnp.concatenate(outs, axis=1)).astype(o_ref.dtype)


def _retention(proj, cos, sin):
    t = proj.shape[0]
    chunk = RET_CHUNK
    log_gamma = np.log(1.0 - 2.0 ** (-5.0 - np.arange(RET_HEADS, dtype=np.float64)))[:, None, None]
    n = np.arange(chunk, dtype=np.float64)
    lag = n[:, None] - n[None, :]
    dmat = np.where(lag >= 0, np.exp(np.maximum(lag, 0.0)[None] * log_gamma), 0.0)
    ones = np.ones((1, 1, RET_DH))
    qd = np.exp((n + 1.0)[None, :, None] * log_gamma) * ones
    kd = np.exp((chunk - 1.0 - n)[None, :, None] * log_gamma) * ones
    cd = np.exp(chunk * log_gamma) * ones
    tables = [jnp.asarray(a, F32) for a in (dmat, qd, kd, cd)]
    base = C_RET // RET_W

    def col(off):
        return pl.BlockSpec((chunk, RET_W), lambda c, off=off: (c, base + off))

    def full(a):
        return pl.BlockSpec(a.shape, lambda c: (0, 0, 0))

    tab = pl.BlockSpec((chunk, RET_DH), lambda c: (c, 0))
    return pl.pallas_call(
        _retention_kernel,
        out_shape=jax.ShapeDtypeStruct((t, RET_W), BF16),
        grid=(t // chunk,),
        in_specs=[col(0), col(1), col(2), col(3), tab, tab] + [full(a) for a in tables],
        out_specs=pl.BlockSpec((chunk, RET_W), lambda c: (c, 0)),
        scratch_shapes=[pltpu.VMEM((RET_HEADS, RET_DH, RET_DH), F32)],
        compiler_params=_cparams(("arbitrary",)),
        name="retention",
    )(proj, proj, proj, proj, cos, sin, *tables)


def _dot_f32(a, b):
    a_hi = a.astype(BF16)
    b_hi = b.astype(BF16)
    a_lo = (a - a_hi.astype(F32)).astype(BF16)
    b_lo = (b - b_hi.astype(F32)).astype(BF16)
    return _dot(a_hi, b_hi) + (_dot(a_hi, b_lo) + _dot(a_lo, b_hi))


RW_PRE_TM = 256
_MIX_OFF = (0, 512, 1024, 1536, 1664, 1792, 2048, 2176)


def _rwkv_pre_kernel(r_ref, k_ref, v_ref, xw_ref, xa_ref, xg_ref, xv_ref,
                     rp_ref, kp_ref, vp_ref, xwp_ref, xap_ref, xgp_ref, xvp_ref,
                     mix_ref, w0_ref, w2_ref, a0_ref, a2_ref, v0_ref, v2_ref, g2_ref, kk_ref, ka_ref,
                     gsum_ref, vfirst_ref,
                     r_out, lw_out, k_out, v_out, an_out, b_out, g_out, *, use_vres):
    i = pl.program_id(0)

    def mixed(cur_ref, prev_ref, seg):
        cur = cur_ref[...]
        mix = mix_ref[:, _MIX_OFF[seg]:_MIX_OFF[seg + 1]]
        prev_row = jnp.where(i > 0, prev_ref[7:8, :], 0.0)
        rid = lax.broadcasted_iota(jnp.int32, cur.shape, 0)
        shifted = jnp.where(rid == 0, prev_row, pltpu.roll(cur, 1, axis=0))
        return cur + mix * (shifted - cur)

    r = mixed(r_ref, rp_ref, 0)
    k = mixed(k_ref, kp_ref, 1)
    v = mixed(v_ref, vp_ref, 2)
    xw = mixed(xw_ref, xwp_ref, 3)
    xa = mixed(xa_ref, xap_ref, 4)
    xg = mixed(xg_ref, xgp_ref, 5)

    z = -(w0_ref[...] + _dot_f32(jnp.tanh(xw), w2_ref[...]))
    softplus = jnp.maximum(z, 0.0) + jnp.log(1.0 + jnp.exp(-jnp.abs(z)))
    w = -softplus - 0.5
    lw_out[...] = -jnp.exp(w)
    a = jax.nn.sigmoid(a0_ref[...] + _dot_f32(xa, a2_ref[...]))
    g_out[...] = _dot_f32(jax.nn.sigmoid(xg), g2_ref[...])
    if use_vres:
        xv = mixed(xv_ref, xvp_ref, 6)
        v = v + (vfirst_ref[...] - v) * jax.nn.sigmoid(v0_ref[...] + _dot_f32(xv, v2_ref[...]))
    kk = k * kk_ref[...]
    ss = _split_dot(kk * kk, gsum_ref[...], 3)
    kk = kk / jnp.maximum(jnp.sqrt(ss), 1e-12)
    r_out[...] = r
    k_out[...] = k * (1.0 + (a - 1.0) * ka_ref[...])
    v_out[...] = v
    an_out[...] = -kk
    b_out[...] = kk * a


def _rwkv_pre(proj, mix, w0, w2, a0, a2, v0, v2, g2, k_k, k_a, gsum, v_first, use_vres):
    t = proj.shape[0]
    tm = RW_PRE_TM

    def cur(width, off):
        return pl.BlockSpec((tm, width), lambda i: (i, off // width))

    def prev(width, off):
        return pl.BlockSpec((8, width), lambda i: (jnp.maximum(i * (tm // 8) - 1, 0), off // width))

    def full(a):
        return pl.BlockSpec(a.shape, lambda i: (0,) * a.ndim)

    segs = [(RW_W, C_RW_RKV), (RW_W, C_RW_RKV + RW_W), (RW_W, C_RW_RKV + 2 * RW_W),
            (128, C_RW_XW), (128, C_RW_XA), (256, C_RW_XG), (128, C_RW_XV)]
    params = [mix, w0, w2, a0, a2, v0, v2, g2, k_k, k_a, gsum]
    row = pl.BlockSpec((tm, RW_W), lambda i: (i, 0))
    outs = pl.pallas_call(
        functools.partial(_rwkv_pre_kernel, use_vres=use_vres),
        out_shape=[jax.ShapeDtypeStruct((t, RW_W), F32)] * 7,
        grid=(t // tm,),
        in_specs=[cur(w, o) for w, o in segs] + [prev(w, o) for w, o in segs] + [full(p) for p in params] + [row],
        out_specs=[row] * 7,
        compiler_params=_cparams(("parallel",)),
        name="rwkv_pre",
    )(*([proj] * 14), *params, v_first)
    return outs


RW_CHUNK = 128


def _wkv_kernel(r_ref, lw_ref, k_ref, v_ref, an_ref, b_ref, g_ref, lng_ref, lnb_ref, rk_ref,
                gsum_ref, tril_ref, o_ref, s_ref):
    c = pl.program_id(0)
    C = RW_CHUNK
    N = RW_N

    @pl.when(c == 0)
    def _():
        s_ref[...] = jnp.zeros_like(s_ref)

    r = r_ref[...]
    lw = lw_ref[...]
    k = k_ref[...]
    v = v_ref[...]
    lw_hi = lw.astype(BF16)
    rem = lw - lw_hi.astype(F32)
    lw_mid = rem.astype(BF16)
    lw_lo = (rem - lw_mid.astype(F32)).astype(BF16)
    tril = tril_ref[...]
    lg = _dot(tril, lw_hi) + (_dot(tril, lw_mid) + _dot(tril, lw_lo))
    gam = jnp.exp(lg)
    ginv = jnp.exp(-lg)
    at = (an_ref[...] * jnp.exp(lg - lw)).astype(BF16)
    bt = (b_ref[...] * ginv).astype(BF16)
    kt = (k * ginv).astype(BF16)
    rt_f = r * gam
    rt = rt_f.astype(BF16)
    vb = v.astype(BF16)
    g_last = gam[C - 1:C, :]

    rowi = lax.broadcasted_iota(jnp.int32, (C, C), 0)
    coli = lax.broadcasted_iota(jnp.int32, (C, C), 1)
    strict = rowi > coli
    incl = rowi >= coli

    heads = range(RW_HEADS)
    hs = [slice(h * N, (h + 1) * N) for h in heads]
    bth = [bt[:, s] for s in hs]
    kth = [kt[:, s] for s in hs]
    vh = [vb[:, s] for s in hs]
    big = [_dot_t(jnp.concatenate([at[:, s], rt[:, s]], axis=0), jnp.concatenate([bth[h], kth[h]], axis=0))
           for h, s in zip(heads, hs)]
    a_ab = [jnp.where(strict, big[h][:C, :C], 0.0) for h in heads]
    a_ak = [jnp.where(strict, big[h][:C, C:], 0.0).astype(BF16) for h in heads]
    a_rb = [jnp.where(incl, big[h][C:, :C], 0.0).astype(BF16) for h in heads]
    a_rk = [jnp.where(incl, big[h][C:, C:], 0.0).astype(BF16) for h in heads]
    akv = [_dot(a_ak[h], vh[h]) for h in heads]
    def same_block(size):
        shift = size.bit_length() - 1
        return (rowi >> shift) == (coli >> shift)

    tinv = [jnp.where(same_block(2), a_ab[h], 0.0) + jnp.where(rowi == coli, 1.0, 0.0) for h in heads]
    size = 2
    while size < C:
        lower_left = same_block(2 * size) & jnp.logical_not(same_block(size))
        off = [jnp.where(lower_left, a_ab[h], 0.0).astype(BF16) for h in heads]
        tb = [tinv[h].astype(BF16) for h in heads]
        half = [_dot(tb[h], off[h]).astype(BF16) for h in heads]
        tinv = [tinv[h] + _dot(half[h], tb[h]) for h in heads]
        size *= 2
    xb = [_dot(tinv[h].astype(BF16), jnp.concatenate([at[:, hs[h]], akv[h].astype(BF16)], axis=1)).astype(BF16)
          for h in heads]
    yx = [_dot(a_rb[h], xb[h]) for h in heads]
    ykv = [_dot(a_rk[h], vh[h]) for h in heads]
    xtb = [_dot_tl(xb[h], bth[h]) for h in heads]
    vtk = [_dot_tl(vh[h], kth[h]) for h in heads]
    ys = []
    for h in heads:
        gl = g_last[:, hs[h]]
        y1 = (rt_f[:, hs[h]] + yx[h][:, :N]).astype(BF16)
        s0 = s_ref[h]
        s0b = s0.astype(BF16)
        ys.append(_dot_t(y1, s0b) + (yx[h][:, N:] + ykv[h]))
        s_ref[h] = (s0 + _dot(s0b, xtb[h][:N].astype(BF16)) + (xtb[h][N:] + vtk[h])) * gl
    y = jnp.concatenate(ys, axis=1)

    gsum = gsum_ref[...]
    inv_n = 1.0 / N
    mu = _split_dot(y, gsum, 3) * inv_n
    yc = y - mu
    var = _split_dot(yc * yc, gsum, 3) * inv_n
    yn = yc * lax.rsqrt(var + RW_GN_EPS) * lng_ref[...] + lnb_ref[...]
    bonus = _split_dot(r * k * rk_ref[...], gsum, 3)
    o_ref[...] = ((yn + bonus * v) * g_ref[...]).astype(o_ref.dtype)


def _wkv(r, lw, k, v, an, b, g, ln_g, ln_b, r_k, gsum):
    t = r.shape[0]
    C = RW_CHUNK
    tril = jnp.asarray(np.tril(np.ones((C, C), np.float32)), BF16)
    row = pl.BlockSpec((C, RW_W), lambda c: (c, 0))
    vec = pl.BlockSpec((1, RW_W), lambda c: (0, 0))
    return pl.pallas_call(
        _wkv_kernel,
        out_shape=jax.ShapeDtypeStruct((t, RW_W), BF16),
        grid=(t // C,),
        in_specs=[row] * 7 + [vec] * 3 + [pl.BlockSpec((RW_W, RW_W), lambda c: (0, 0)),
                                          pl.BlockSpec((C, C), lambda c: (0, 0))],
        out_specs=row,
        scratch_shapes=[pltpu.VMEM((RW_HEADS, RW_N, RW_N), F32)],
        compiler_params=_cparams(("arbitrary",)),
        name="wkv7",
    )(r, lw, k, v, an, b, g, ln_g, ln_b, r_k, gsum, tril)


NSA_PREP_TM = 512


N_NORM_SLABS = (NSA_W + 2 * NSA_KV_W) // LANES
N_RAW_SLABS = 4 * NSA_KV_W // LANES
N_NSA_SLABS = N_NORM_SLABS + N_RAW_SLABS + 1


def _nsa_prep_kernel(*refs):
    x_refs = refs[:N_NSA_SLABS]
    cos_ref, sin_ref, w_ref, gmean_ref, qk_ref, hmat_ref, v1_ref, gate_ref = refs[N_NSA_SLABS:]
    cos = cos_ref[...]
    sin = sin_ref[...]
    gmean = gmean_ref[...]
    half = NSA_DH // 2
    lane = lax.broadcasted_iota(jnp.int32, cos.shape, 1)
    first_half = (lane % NSA_DH) < half
    low = lane < NSA_DH
    for s in range(N_NORM_SLABS):
        x = x_refs[s][...]
        y = x * lax.rsqrt(_split_dot(x * x, gmean, 3) + RMS_EPS) * w_ref[s]
        rot = jnp.where(first_half, pltpu.roll(y, LANES - half, axis=1), pltpu.roll(y, half, axis=1))
        res = (y * cos + rot * sin).astype(qk_ref.dtype)
        qk_ref[2 * s] = res[:, :NSA_DH]
        qk_ref[2 * s + 1] = res[:, NSA_DH:]

    rows = hmat_ref.shape[2]
    low_r = lax.broadcasted_iota(jnp.int32, (rows, LANES), 1) < NSA_DH
    for s in range(NSA_KV_W // LANES * 2):
        kind, pair = divmod(s, NSA_KV_W // LANES)
        x_ref = x_refs[N_NORM_SLABS + s]
        for t2 in range(CMP_STRIDE // 2):
            r0 = x_ref[pl.ds(2 * t2, rows, stride=CMP_STRIDE), :]
            r1 = x_ref[pl.ds(2 * t2 + 1, rows, stride=CMP_STRIDE), :]
            cols = slice(t2 * LANES, (t2 + 1) * LANES)
            hmat_ref[kind, 2 * pair, :, cols] = jnp.where(low_r, r0, pltpu.roll(r1, NSA_DH, axis=1)).astype(BF16)
            hmat_ref[kind, 2 * pair + 1, :, cols] = jnp.where(low_r, pltpu.roll(r0, NSA_DH, axis=1), r1).astype(BF16)

    one_col = jnp.where(lane == NSA_DH, 1.0, 0.0)
    for s in range(NSA_KV_W // LANES * 2):
        x = x_refs[N_NORM_SLABS + N_RAW_SLABS // 2 + s][...]
        v1_ref[2 * s] = jnp.where(low, x, one_col).astype(BF16)
        v1_ref[2 * s + 1] = jnp.where(low, pltpu.roll(x, NSA_DH, axis=1), one_col).astype(BF16)

    xg = x_refs[N_NSA_SLABS - 1][...]
    per_group = 3 * NSA_HG
    for g in range(NSA_G):
        shifted = xg if g == 0 else pltpu.roll(xg, LANES - per_group * g, axis=1)
        gate_ref[g] = jnp.where(lane < per_group, shifted, 0.0)


def _nsa_prep(proj, w_slabs, cos, sin, gmean):
    t = proj.shape[0]
    tm = NSA_PREP_TM
    base = C_NSA_Q // LANES
    tab = pl.BlockSpec((tm, LANES), lambda i: (i, 0))
    n_heads_out = 2 * N_NORM_SLABS
    return pl.pallas_call(
        _nsa_prep_kernel,
        out_shape=[jax.ShapeDtypeStruct((n_heads_out, t, NSA_DH), BF16),
                   jax.ShapeDtypeStruct((2, NSA_G, t // CMP_STRIDE, CMP_STRIDE * NSA_DH), BF16),
                   jax.ShapeDtypeStruct((2 * NSA_G, t, LANES), BF16),
                   jax.ShapeDtypeStruct((NSA_G, t, LANES), F32)],
        grid=(t // tm,),
        in_specs=[pl.BlockSpec((tm, LANES), lambda i, s=s: (i, base + s)) for s in range(N_NSA_SLABS)]
        + [tab, tab, pl.BlockSpec((N_NORM_SLABS, 1, LANES), lambda i: (0, 0, 0)),
           pl.BlockSpec((LANES, LANES), lambda i: (0, 0))],
        out_specs=[pl.BlockSpec((n_heads_out, tm, NSA_DH), lambda i: (0, i, 0)),
                   pl.BlockSpec((2, NSA_G, tm // CMP_STRIDE, CMP_STRIDE * NSA_DH), lambda i: (0, 0, i, 0)),
                   pl.BlockSpec((2 * NSA_G, tm, LANES), lambda i: (0, i, 0)),
                   pl.BlockSpec((NSA_G, tm, LANES), lambda i: (0, i, 0))],
        compiler_params=_cparams(("parallel",)),
        name="nsa_prep",
    )(*([proj] * N_NSA_SLABS), cos, sin, w_slabs, gmean)


def _compress_kernel(h_ref, w1_ref, w2_ref, pe_ref, nw_ref, cos_ref, sin_ref, o_ref):
    kind = pl.program_id(0)
    nc = h_ref.shape[2]
    half_in = CMP_STRIDE * NSA_DH
    hm = h_ref[0, 0]
    w1 = w1_ref[0]
    first = _dot(hm, w1[:half_in])
    second = _dot(hm, w1[half_in:])
    const = _dot(pe_ref[0], w1)[0:1, :]
    pre = first + pltpu.roll(second, nc - 1, axis=0) + const
    hid = 0.5 * pre * (1.0 + jnp.tanh(0.7978845608028654 * (pre + 0.044715 * pre * pre * pre)))
    out = _dot(hid.astype(BF16), w2_ref[0])
    y = out * lax.rsqrt(jnp.mean(out * out, axis=-1, keepdims=True) + RMS_EPS) * nw_ref[...]
    half = NSA_DH // 2
    rot = jnp.concatenate([y[:, half:], y[:, :half]], axis=1)
    roped = y * cos_ref[...] + rot * sin_ref[...]
    o_ref[0, 0] = jnp.where(kind == 0, roped, out).astype(o_ref.dtype)


def _compress(hmat, w1, w2, pe, nw, cos_c, sin_c):
    _, g, nc, width = hmat.shape
    return pl.pallas_call(
        _compress_kernel,
        out_shape=jax.ShapeDtypeStruct((2, g, nc, NSA_DH), BF16),
        grid=(2, g),
        in_specs=[pl.BlockSpec((1, 1, nc, width), lambda a, b: (a, b, 0, 0)),
                  pl.BlockSpec((1, 2 * width, CMP_HIDDEN), lambda a, b: (a, 0, 0)),
                  pl.BlockSpec((1, CMP_HIDDEN, NSA_DH), lambda a, b: (a, 0, 0)),
                  pl.BlockSpec((1, 8, 2 * width), lambda a, b: (a, 0, 0)),
                  pl.BlockSpec((1, NSA_DH), lambda a, b: (0, 0)),
                  pl.BlockSpec((nc, NSA_DH), lambda a, b: (0, 0)),
                  pl.BlockSpec((nc, NSA_DH), lambda a, b: (0, 0))],
        out_specs=pl.BlockSpec((1, 1, nc, NSA_DH), lambda a, b: (a, b, 0, 0)),
        compiler_params=_cparams(("parallel", "parallel")),
        name="nsa_compress",
    )(hmat, w1, w2, pe, nw, cos_c, sin_c)


SLC_KT = 1024
WIN_KEYS = WINDOW + NSA_TQ


INT32_MIN = -2 ** 31


def _nsa_attn_kernel(q_ref, gate_ref, kc_ref, vc_ref, ks_ref, vs_ref, kw_ref, vw_ref, cis_ref, expand_ref, ltri_ref,
                     o_ref):
    qi = pl.program_id(1)
    nc = kc_ref.shape[2]
    nb = cis_ref.shape[1]
    heads = range(NSA_HG)
    start = qi * NSA_TQ
    qh = [q_ref[hh] for hh in heads]
    tok = start + lax.broadcasted_iota(jnp.int32, (NSA_TQ, 1), 0)

    def softmax_terms(scores):
        return [jnp.exp(s - jnp.max(s, axis=-1, keepdims=True)) for s in scores]

    kc = kc_ref[0, 0]
    vc = vc_ref[0, 0]
    cend = lax.broadcasted_iota(jnp.int32, (1, nc), 1) * CMP_STRIDE + (CMP_BLOCK - 1)
    bias_c = jnp.where(cend <= tok, 0.0, NEG_INF)
    ec = softmax_terms([_dot_t(qh[hh], kc) + bias_c for hh in heads])
    sees_any = tok >= CMP_BLOCK - 1
    pc = [ec[hh] * jnp.where(sees_any, 1.0 / jnp.sum(ec[hh], axis=-1, keepdims=True), 0.0) for hh in heads]
    o_c = [_dot(pc[hh].astype(BF16), vc) for hh in heads]

    psum = pc[0]
    for hh in range(1, NSA_HG):
        psum = psum + pc[hh]
    score = _split_dot(psum, cis_ref[...], 3)
    blk = lax.broadcasted_iota(jnp.int32, (1, nb), 1)
    cur = tok // SLC_BLOCK
    forced = (blk == 0) | (blk == cur) | (blk == cur - 1)
    score = jnp.where(forced, FORCED_SCORE, score)
    score = jnp.where(blk <= cur, score, -jnp.inf)

    kbase = pl.multiple_of(jnp.maximum(start - WINDOW, 0), NSA_TQ)
    kw = kw_ref[0, pl.ds(kbase, WIN_KEYS), :]
    lag = tok - (kbase + lax.broadcasted_iota(jnp.int32, (1, WIN_KEYS), 1))
    bias_w = jnp.where((lag >= 0) & (lag < WINDOW), 0.0, NEG_INF)
    sw = [_dot_t(qh[hh], kw) for hh in heads]

    bits = lax.bitcast_convert_type(score.T, jnp.int32)
    key = bits ^ ((bits >> 31) & 0x7FFFFFFF)
    n_sel = min(SLC_TOPK, nb)

    def enough(c):
        return jnp.sum(jnp.where(key >= c, 1.0, 0.0), axis=0, keepdims=True) >= n_sel

    def enough3(c1, c2, c3):
        packed = jnp.where(key >= c3, 65793.0, jnp.where(key >= c2, 257.0, jnp.where(key >= c1, 1.0, 0.0)))
        tot = jnp.sum(packed, axis=0, keepdims=True).astype(jnp.int32)
        return (tot & 255) >= n_sel, ((tot >> 8) & 255) >= n_sel, (tot >> 16) >= n_sel

    zero_row = jnp.zeros((1, NSA_TQ), jnp.int32)
    thr = jnp.where(enough(zero_row), zero_row, INT32_MIN)
    for hi in range(30, 0, -2):
        c1 = thr + (1 << (hi - 1))
        c2 = thr + (1 << hi)
        c3 = c2 + (1 << (hi - 1))
        e1, e2, e3 = enough3(c1, c2, c3)
        thr = jnp.where(e3, c3, jnp.where(e2, c2, jnp.where(e1, c1, thr)))
    c1 = thr + 1
    thr = jnp.where(enough(c1), c1, thr)
    above = key > thr
    tied = key == thr
    need = n_sel - jnp.sum(jnp.where(above, 1.0, 0.0), axis=0, keepdims=True)
    tied_before = _dot(ltri_ref[...], jnp.where(tied, 1.0, 0.0).astype(BF16))
    sel_t = above | (tied & (tied_before < need))
    sel = jnp.where(sel_t, 1.0, 0.0).T.astype(BF16)

    ew = softmax_terms([sw[hh] + bias_w for hh in heads])
    acc_w = [_dot(ew[hh].astype(BF16), vw_ref[0, pl.ds(kbase, WIN_KEYS), :]) for hh in heads]

    n_tiles = (start + NSA_TQ + SLC_KT - 1) // SLC_KT

    def sel_body(j, carry):
        ms, accs = carry
        k0 = pl.multiple_of(j * SLC_KT, SLC_KT)
        kt = ks_ref[0, pl.ds(k0, SLC_KT), :]
        vt = vs_ref[0, pl.ds(k0, SLC_KT), :]
        member = _dot(sel, expand_ref[:, pl.ds(k0, SLC_KT)])
        kpos = k0 + lax.broadcasted_iota(jnp.int32, (1, SLC_KT), 1)
        bias = jnp.where((member > 0.5) & (kpos <= tok), 0.0, NEG_INF)
        sj = [_dot_t(qh[hh], kt) + bias for hh in heads]
        m_new = [jnp.maximum(ms[hh], jnp.max(sj[hh], axis=-1, keepdims=True)) for hh in heads]
        pj = [jnp.exp((sj[hh] - m_new[hh]).astype(BF16)) for hh in heads]
        accs = [jnp.exp(ms[hh] - m_new[hh]) * accs[hh] + _dot(pj[hh], vt) for hh in heads]
        return tuple(m_new), tuple(accs)

    m0 = tuple(jnp.full((NSA_TQ, 1), NEG_INF, F32) for _ in heads)
    a0 = tuple(jnp.zeros((NSA_TQ, LANES), F32) for _ in heads)
    _, acc_s = lax.fori_loop(0, n_tiles, sel_body, (m0, a0))

    gt = jax.nn.sigmoid(gate_ref[0])
    outs = []
    for hh in heads:
        g0, g1, g2 = (gt[:, 3 * hh + br:3 * hh + br + 1] for br in range(3))
        scale_s = g1 / acc_s[hh][:, NSA_DH:NSA_DH + 1]
        scale_w = g2 / acc_w[hh][:, NSA_DH:NSA_DH + 1]
        outs.append(g0 * o_c[hh] + scale_s * acc_s[hh][:, :NSA_DH] + scale_w * acc_w[hh][:, :NSA_DH])
    o_ref[...] = jnp.concatenate(outs, axis=1).astype(o_ref.dtype)


def _nsa_attn(qk_hm, gates, cmp_kv, v1, cis, expand, ltri):
    t = qk_hm.shape[1]
    nq = t // NSA_TQ
    nc = cmp_kv.shape[2]
    nb = cis.shape[1]
    ks_spec = pl.BlockSpec((1, t, NSA_DH), lambda g, i: (NSA_HEADS + g, 0, 0))
    kw_spec = pl.BlockSpec((1, t, NSA_DH), lambda g, i: (NSA_HEADS + NSA_G + g, 0, 0))
    vs_spec = pl.BlockSpec((1, t, LANES), lambda g, i: (g, 0, 0))
    vw_spec = pl.BlockSpec((1, t, LANES), lambda g, i: (NSA_G + g, 0, 0))
    return pl.pallas_call(
        _nsa_attn_kernel,
        out_shape=jax.ShapeDtypeStruct((t, NSA_W), BF16),
        grid=(NSA_G, nq),
        in_specs=[pl.BlockSpec((NSA_HG, NSA_TQ, NSA_DH), lambda g, i: (g, i, 0)),
                  pl.BlockSpec((1, NSA_TQ, LANES), lambda g, i: (g, i, 0)),
                  pl.BlockSpec((1, 1, nc, NSA_DH), lambda g, i: (0, g, 0, 0)),
                  pl.BlockSpec((1, 1, nc, NSA_DH), lambda g, i: (1, g, 0, 0)),
                  ks_spec, vs_spec, kw_spec, vw_spec,
                  pl.BlockSpec((nc, nb), lambda g, i: (0, 0)),
                  pl.BlockSpec((nb, t), lambda g, i: (0, 0)),
                  pl.BlockSpec((nb, nb), lambda g, i: (0, 0))],
        out_specs=pl.BlockSpec((NSA_TQ, NSA_HG * NSA_DH), lambda g, i: (i, g)),
        compiler_params=_cparams(("parallel", "arbitrary")),
        name="nsa_attention",
    )(qk_hm, gates, cmp_kv, cmp_kv, qk_hm, v1, qk_hm, v1, cis, expand, ltri)


def _rope_tables(pos, dh, reps):
    half = dh // 2
    inv_freq = ROPE_THETA ** (-jnp.arange(half, dtype=F32) / half)
    ang = pos.astype(F32)[:, None] * inv_freq[None, :]
    cos = jnp.cos(ang)
    sin = jnp.sin(ang)
    cos_t = jnp.tile(jnp.concatenate([cos, cos], axis=1), (1, reps))
    sin_t = jnp.tile(jnp.concatenate([-sin, sin], axis=1), (1, reps))
    return cos_t, sin_t


def _pad_cols(a, width):
    return jnp.pad(a, ((0, 0), (0, width - a.shape[1])))


def _pad_rows(a, height):
    return jnp.pad(a, ((0, height - a.shape[0]), (0, 0)))


def _pack_w_in(w_l, w_vres):
    d = w_l.shape[0]
    o = 0
    ret = w_l[:, o:o + 4 * RET_W]; o += 4 * RET_W
    rkv = w_l[:, o:o + 3 * RW_W]; o += 3 * RW_W
    xw = w_l[:, o:o + RW_DECAY_RANK]; o += RW_DECAY_RANK
    xa = w_l[:, o:o + RW_A_RANK]; o += RW_A_RANK
    xg = w_l[:, o:o + RW_GATE_RANK]; o += RW_GATE_RANK
    q = w_l[:, o:o + NSA_W]; o += NSA_W
    kc, vc, ks, vs, kw, vw = (w_l[:, o + i * NSA_KV_W:o + (i + 1) * NSA_KV_W] for i in range(6))
    o += 6 * NSA_KV_W
    gates = w_l[:, o:o + 3 * NSA_HEADS]
    xv = jnp.zeros((d, LANES), w_l.dtype) if w_vres is None else _pad_cols(w_vres, LANES)
    return jnp.concatenate([ret, rkv, _pad_cols(xw, LANES), _pad_cols(xa, LANES), xg, xv,
                            q, ks, kw, kc, vc, vs, vw, _pad_cols(gates, LANES)], axis=1)


def _pack_mix(mix, vres_mix):
    o = 3 * RW_W
    xw = mix[o:o + RW_DECAY_RANK]; o += RW_DECAY_RANK
    xa = mix[o:o + RW_A_RANK]; o += RW_A_RANK
    xg = mix[o:o + RW_GATE_RANK]
    z = lambda n: jnp.zeros((n,), mix.dtype)
    xv = z(LANES) if vres_mix is None else jnp.concatenate([vres_mix, z(LANES - RW_V_RANK)])
    return jnp.concatenate([mix[:3 * RW_W], xw, z(LANES - RW_DECAY_RANK), xa, z(LANES - RW_A_RANK), xg, xv])[None, :]


def kernel(x, ln1_g, w_in, w_in_vres, rwkv_mix, rwkv_vres_mix, rwkv_w0, rwkv_w2, rwkv_a0, rwkv_a2, rwkv_v0, rwkv_v2, rwkv_g2, rwkv_k_k, rwkv_k_a, rwkv_r_k, rwkv_ln_g, rwkv_ln_b, nsa_q_norm, nsa_k_norm, nsa_cmp_pe, nsa_cmp_k_w1, nsa_cmp_k_w2, nsa_cmp_v_w1, nsa_cmp_v_w2, w_out, ln2_g, w_up, w_down):
    bsz, t, d = x.shape
    assert bsz == 1
    depth = w_in.shape[0]
    nc = t // CMP_STRIDE
    nb = t // SLC_BLOCK
    xs = x.reshape(t, d)

    pos = jnp.arange(t)
    cos_r, sin_r = _rope_tables(pos, RET_DH, 1)
    cos_n, sin_n = _rope_tables(pos, NSA_DH, 2)
    cos_c, sin_c = _rope_tables(jnp.arange(nc) * CMP_STRIDE + (CMP_BLOCK - 1), NSA_DH, 1)
    lane_head = np.arange(RW_W) // RW_N
    gsum = jnp.asarray(lane_head[:, None] == lane_head[None, :], BF16)
    lane_h2 = np.arange(LANES) // NSA_DH
    gmean = jnp.asarray((lane_h2[:, None] == lane_h2[None, :]) / float(NSA_DH), BF16)
    cstart = np.arange(nc) * CMP_STRIDE
    sstart = np.arange(nb) * SLC_BLOCK
    cis = jnp.asarray((cstart[:, None] <= sstart[None, :] + SLC_BLOCK - 1)
                      & (cstart[:, None] + CMP_BLOCK - 1 >= sstart[None, :]), BF16)

    expand = jnp.asarray((np.arange(t)[None, :] // SLC_BLOCK) == np.arange(nb)[:, None], BF16)
    ltri = jnp.asarray(np.tril(np.ones((nb, nb), np.float32), -1), BF16)

    v_first = jnp.zeros((t, RW_W), F32)
    for l in range(depth):
        vres = l > 0
        w_cat = _pack_w_in(w_in[l], w_in_vres[l - 1] if vres else None).astype(BF16)
        proj = _norm_matmul(xs, ln1_g[l], w_cat, name="in_proj")

        o_ret = _retention(proj, cos_r, sin_r)

        row = lambda a: a.reshape(1, -1)
        mix = _pack_mix(rwkv_mix[l], rwkv_vres_mix[l - 1] if vres else None)
        v0 = row(rwkv_v0[l - 1]) if vres else jnp.zeros((1, RW_W), F32)
        v2 = _pad_rows(rwkv_v2[l - 1], LANES) if vres else jnp.zeros((LANES, RW_W), F32)
        r_, lw_, k_, v_, an_, b_, g_ = _rwkv_pre(
            proj, mix, row(rwkv_w0[l]), _pad_rows(rwkv_w2[l], LANES), row(rwkv_a0[l]), _pad_rows(rwkv_a2[l], LANES),
            v0, v2, rwkv_g2[l], row(rwkv_k_k[l]), row(rwkv_k_a[l]), gsum, v_first, vres)
        if not vres:
            v_first = v_
        o_rwkv = _wkv(r_, lw_, k_, v_, an_, b_, g_, row(rwkv_ln_g[l]), row(rwkv_ln_b[l]), row(rwkv_r_k[l]), gsum)

        n_q = NSA_W // LANES
        n_k = NSA_KV_W // LANES
        w_slabs = jnp.stack([jnp.tile(nsa_q_norm[l] * (NSA_DH ** -0.5), 2)] * n_q
                            + [jnp.tile(nsa_k_norm[l, 1], 2)] * n_k + [jnp.tile(nsa_k_norm[l, 2], 2)] * n_k)[:, None, :]
        qk_hm, hmat, v1, gates = _nsa_prep(proj, w_slabs, cos_n, sin_n, gmean)
        w1 = jnp.stack([nsa_cmp_k_w1[l], nsa_cmp_v_w1[l]]).astype(BF16)
        w2 = jnp.stack([nsa_cmp_k_w2[l], nsa_cmp_v_w2[l]]).astype(BF16)
        pe = jnp.broadcast_to(nsa_cmp_pe[l].reshape(2, 1, CMP_BLOCK * NSA_DH), (2, 8, CMP_BLOCK * NSA_DH)).astype(BF16)
        cmp_kv = _compress(hmat, w1, w2, pe, row(nsa_k_norm[l, 0]), cos_c, sin_c)
        o_nsa = _nsa_attn(qk_hm, gates, cmp_kv, v1, cis, expand, ltri)

        xs = _out_proj(o_ret, o_rwkv, o_nsa, w_out[l], xs)

        up = _norm_matmul(xs, ln2_g[l], w_up[l], act="relu2", out_dtype=BF16, name="mlp_up")
        xs = _matmul(up, w_down[l], res=xs, name="mlp_down")
    return xs.reshape(bsz, t, d)
```

```python
import functools

import numpy as np
import jax
import jax.numpy as jnp
from jax import lax
from jax.experimental import pallas as pl
from jax.experimental.pallas import tpu as pltpu

F32 = jnp.float32
BF16 = jnp.bfloat16

D_MODEL = 2048
RET_HEADS, RET_DH = 4, 128
RET_W = RET_HEADS * RET_DH
RW_HEADS, RW_N = 8, 64
RW_W = RW_HEADS * RW_N
RW_DECAY_RANK, RW_A_RANK, RW_V_RANK, RW_GATE_RANK = 96, 96, 64, 256
RW_GN_EPS = 64e-5
NSA_HEADS, NSA_G, NSA_DH = 16, 4, 64
NSA_HG = NSA_HEADS // NSA_G
NSA_W = NSA_HEADS * NSA_DH
NSA_KV_W = NSA_G * NSA_DH
CMP_BLOCK, CMP_STRIDE, CMP_HIDDEN = 32, 16, 256
SLC_BLOCK, SLC_TOPK, WINDOW = 64, 16, 512
NSA_TQ = 256
ROPE_THETA = 10000.0
MLP_HIDDEN = 4 * D_MODEL
RMS_EPS = 1e-6
NEG_INF = -1e30
FORCED_SCORE = 1e9

LANES = 128
VMEM_LIMIT = 56 * 1024 * 1024

C_RET = 0
C_RW_RKV = 2048
C_RW_XW = 3584
C_RW_XA = 3712
C_RW_XG = 3840
C_RW_XV = 4096
C_NSA_Q = 4224
C_NSA_KS = 5248
C_NSA_KW = 5504
C_NSA_KC = 5760
C_NSA_VC = 6016
C_NSA_VS = 6272
C_NSA_VW = 6528
C_NSA_GATE = 6784
PROJ_W = 6912


def _cparams(sem):
    return pltpu.CompilerParams(dimension_semantics=sem, vmem_limit_bytes=VMEM_LIMIT)


def _dot(a, b):
    return jnp.dot(a, b, preferred_element_type=F32)


def _dot_t(a, b):
    return lax.dot_general(a, b, (((1,), (1,)), ((), ())), preferred_element_type=F32)


def _dot_tl(a, b):
    return lax.dot_general(a, b, (((0,), (0,)), ((), ())), preferred_element_type=F32)


def _split_dot(x, m_bf16, passes):
    hi = x.astype(BF16)
    acc = _dot(hi, m_bf16)
    rem = x - hi.astype(F32)
    for _ in range(passes - 1):
        piece = rem.astype(BF16)
        acc = acc + _dot(piece, m_bf16)
        rem = rem - piece.astype(F32)
    return acc


def _mm_kernel(*refs, nk, act, has_res):
    a_ref, b_ref = refs[:2]
    r_ref = refs[2] if has_res else None
    o_ref = refs[3] if has_res else refs[2]

    def finish(y):
        if act == "relu2":
            y = jnp.square(jnp.maximum(y, 0.0))
        if has_res:
            y = y + r_ref[...]
        o_ref[...] = y.astype(o_ref.dtype)

    part = _dot(a_ref[...], b_ref[...].astype(BF16))
    if nk == 1:
        finish(part)
        return
    acc_ref = refs[-1]
    k = pl.program_id(2)

    @pl.when(k == 0)
    def _():
        acc_ref[...] = part

    @pl.when(k > 0)
    def _():
        acc_ref[...] += part

    @pl.when(k == nk - 1)
    def _():
        finish(acc_ref[...])


def _pick(n, cands):
    for c in cands:
        if n % c == 0:
            return c
    raise ValueError(f"no tile for {n}")


def _matmul(a, b, *, res=None, act=None, out_dtype=F32, name="matmul"):
    m, kd = a.shape
    _, n = b.shape
    tm = _pick(m, (1024, 512, 256, 128))
    tn = _pick(n, (1024, 768, 512, 256, 128))
    tk = _pick(kd, (2048, 1024, 512))
    nk = kd // tk
    in_specs = [pl.BlockSpec((tm, tk), lambda i, j, k: (i, k)), pl.BlockSpec((tk, tn), lambda i, j, k: (k, j))]
    args = [a, b]
    if res is not None:
        in_specs.append(pl.BlockSpec((tm, tn), lambda i, j, k: (i, j)))
        args.append(res)
    return pl.pallas_call(
        functools.partial(_mm_kernel, nk=nk, act=act, has_res=res is not None),
        out_shape=jax.ShapeDtypeStruct((m, n), out_dtype),
        grid=(m // tm, n // tn, nk),
        in_specs=in_specs,
        out_specs=pl.BlockSpec((tm, tn), lambda i, j, k: (i, j)),
        scratch_shapes=[pltpu.VMEM((tm, tn), F32)] if nk > 1 else [],
        compiler_params=_cparams(("parallel", "parallel", "arbitrary")),
        name=name,
    )(*args)


def _norm_mm_kernel(x_ref, g_ref, b_ref, o_ref, h_ref, *, act):
    @pl.when(pl.program_id(1) == 0)
    def _():
        x = x_ref[...]
        ms = jnp.mean(x * x, axis=-1, keepdims=True)
        h_ref[...] = (x * lax.rsqrt(ms + RMS_EPS) * g_ref[...]).astype(BF16)

    y = _dot(h_ref[...], b_ref[...].astype(BF16))
    if act == "relu2":
        y = jnp.square(jnp.maximum(y, 0.0))
    o_ref[...] = y.astype(o_ref.dtype)


def _norm_matmul(x, g, b, *, act=None, out_dtype=F32, name="norm_matmul"):
    m, kd = x.shape
    n = b.shape[1]
    tm = _pick(m, (1024, 512, 256, 128))
    tn = _pick(n, (1024, 768, 512, 256, 128))
    return pl.pallas_call(
        functools.partial(_norm_mm_kernel, act=act),
        out_shape=jax.ShapeDtypeStruct((m, n), out_dtype),
        grid=(m // tm, n // tn),
        in_specs=[pl.BlockSpec((tm, kd), lambda i, j: (i, 0)),
                  pl.BlockSpec((1, kd), lambda i, j: (0, 0)),
                  pl.BlockSpec((kd, tn), lambda i, j: (0, j))],
        out_specs=pl.BlockSpec((tm, tn), lambda i, j: (i, j)),
        scratch_shapes=[pltpu.VMEM((tm, kd), BF16)],
        compiler_params=_cparams(("parallel", "arbitrary")),
        name=name,
    )(x, g.reshape(1, kd), b)


def _out_proj_kernel(ret_ref, rw_ref, nsa_ref, w_ref, res_ref, o_ref):
    k0 = ret_ref.shape[1]
    k1 = k0 + rw_ref.shape[1]
    acc = _dot(ret_ref[...], w_ref[:k0, :].astype(BF16))
    acc = acc + _dot(rw_ref[...], w_ref[k0:k1, :].astype(BF16))
    acc = acc + _dot(nsa_ref[...], w_ref[k1:, :].astype(BF16))
    o_ref[...] = acc + res_ref[...]


def _out_proj(o_ret, o_rwkv, o_nsa, w, res):
    m = o_ret.shape[0]
    kd, n = w.shape
    assert kd == o_ret.shape[1] + o_rwkv.shape[1] + o_nsa.shape[1]
    tm = _pick(m, (1024, 512, 256, 128))
    tn = _pick(n, (1024, 512, 256, 128))

    def rows(a):
        return pl.BlockSpec((tm, a.shape[1]), lambda i, j: (i, 0))

    return pl.pallas_call(
        _out_proj_kernel,
        out_shape=jax.ShapeDtypeStruct((m, n), F32),
        grid=(m // tm, n // tn),
        in_specs=[rows(o_ret), rows(o_rwkv), rows(o_nsa),
                  pl.BlockSpec((kd, tn), lambda i, j: (0, j)),
                  pl.BlockSpec((tm, tn), lambda i, j: (i, j))],
        out_specs=pl.BlockSpec((tm, tn), lambda i, j: (i, j)),
        compiler_params=_cparams(("parallel", "parallel")),
        name="out_proj",
    )(o_ret, o_rwkv, o_nsa, w, res)


RET_CHUNK = 256


def _retention_kernel(q_ref, k_ref, v_ref, g_ref, cos_ref, sin_ref, dmat_ref, qd_ref, kd_ref, cd_ref, o_ref, state_ref):
    @pl.when(pl.program_id(0) == 0)
    def _():
        state_ref[...] = jnp.zeros_like(state_ref)

    cos = cos_ref[...]
    sin = sin_ref[...]

    def rope(x):
        return x * cos + pltpu.roll(x, RET_DH // 2, axis=1) * sin

    heads = range(RET_HEADS)
    hs = [slice(h * RET_DH, (h + 1) * RET_DH) for h in heads]
    q_all, k_all, v_all, gate = q_ref[...], k_ref[...], v_ref[...], g_ref[...]
    qb = [rope(q_all[:, s]).astype(BF16) for s in hs]
    kf = [rope(k_all[:, s]) * (RET_DH ** -0.5) for s in hs]
    kb = [kf[h].astype(BF16) for h in heads]
    vb = [v_all[:, s].astype(BF16) for s in hs]
    inner = [(_dot_t(qb[h], kb[h]) * dmat_ref[h]).astype(BF16) for h in heads]
    cross = [_dot(qb[h], state_ref[h].astype(BF16)) * qd_ref[h] for h in heads]
    kv = [_dot_tl((kf[h] * kd_ref[h]).astype(BF16), vb[h]) for h in heads]
    outs = []
    for h in heads:
        out = _dot(inner[h], vb[h]) + cross[h]
        state_ref[h] = cd_ref[h] * state_ref[h] + kv[h]
        outs.append(out * lax.rsqrt(jnp.mean(out * out, axis=-1, keepdims=True) + RMS_EPS))
    o_ref[...] = (gate * jax.nn.sigmoid(gate) * jnp.concatenate(outs, axis=1)).astype(o_ref.dtype)


def _retention(proj, cos, sin):
    t = proj.shape[0]
    chunk = RET_CHUNK
    log_gamma = np.log(1.0 - 2.0 ** (-5.0 - np.arange(RET_HEADS, dtype=np.float64)))[:, None, None]
    n = np.arange(chunk, dtype=np.float64)
    lag = n[:, None] - n[None, :]
    dmat = np.where(lag >= 0, np.exp(np.maximum(lag, 0.0)[None] * log_gamma), 0.0)
    ones = np.ones((1, 1, RET_DH))
    qd = np.exp((n + 1.0)[None, :, None] * log_gamma) * ones
    kd = np.exp((chunk - 1.0 - n)[None, :, None] * log_gamma) * ones
    cd = np.exp(chunk * log_gamma) * ones
    tables = [jnp.asarray(a, F32) for a in (dmat, qd, kd, cd)]
    base = C_RET // RET_W

    def col(off):
        return pl.BlockSpec((chunk, RET_W), lambda c, off=off: (c, base + off))

    def full(a):
        return pl.BlockSpec(a.shape, lambda c: (0, 0, 0))

    tab = pl.BlockSpec((chunk, RET_DH), lambda c: (c, 0))
    return pl.pallas_call(
        _retention_kernel,
        out_shape=jax.ShapeDtypeStruct((t, RET_W), BF16),
        grid=(t // chunk,),
        in_specs=[col(0), col(1), col(2), col(3), tab, tab] + [full(a) for a in tables],
        out_specs=pl.BlockSpec((chunk, RET_W), lambda c: (c, 0)),
        scratch_shapes=[pltpu.VMEM((RET_HEADS, RET_DH, RET_DH), F32)],
        compiler_params=_cparams(("arbitrary",)),
        name="retention",
    )(proj, proj, proj, proj, cos, sin, *tables)


def _dot_f32(a, b):
    a_hi = a.astype(BF16)
    b_hi = b.astype(BF16)
    a_lo = (a - a_hi.astype(F32)).astype(BF16)
    b_lo = (b - b_hi.astype(F32)).astype(BF16)
    return _dot(a_hi, b_hi) + (_dot(a_hi, b_lo) + _dot(a_lo, b_hi))


RW_PRE_TM = 256
_MIX_OFF = (0, 512, 1024, 1536, 1664, 1792, 2048, 2176)


def _rwkv_pre_kernel(r_ref, k_ref, v_ref, xw_ref, xa_ref, xg_ref, xv_ref,
                     rp_ref, kp_ref, vp_ref, xwp_ref, xap_ref, xgp_ref, xvp_ref,
                     mix_ref, w0_ref, w2_ref, a0_ref, a2_ref, v0_ref, v2_ref, g2_ref, kk_ref, ka_ref,
                     gsum_ref, vfirst_ref,
                     r_out, lw_out, k_out, v_out, an_out, b_out, g_out, *, use_vres):
    i = pl.program_id(0)

    def mixed(cur_ref, prev_ref, seg):
        cur = cur_ref[...]
        mix = mix_ref[:, _MIX_OFF[seg]:_MIX_OFF[seg + 1]]
        prev_row = jnp.where(i > 0, prev_ref[7:8, :], 0.0)
        rid = lax.broadcasted_iota(jnp.int32, cur.shape, 0)
        shifted = jnp.where(rid == 0, prev_row, pltpu.roll(cur, 1, axis=0))
        return cur + mix * (shifted - cur)

    r = mixed(r_ref, rp_ref, 0)
    k = mixed(k_ref, kp_ref, 1)
    v = mixed(v_ref, vp_ref, 2)
    xw = mixed(xw_ref, xwp_ref, 3)
    xa = mixed(xa_ref, xap_ref, 4)
    xg = mixed(xg_ref, xgp_ref, 5)

    z = -(w0_ref[...] + _dot_f32(jnp.tanh(xw), w2_ref[...]))
    softplus = jnp.maximum(z, 0.0) + jnp.log(1.0 + jnp.exp(-jnp.abs(z)))
    w = -softplus - 0.5
    lw_out[...] = -jnp.exp(w)
    a = jax.nn.sigmoid(a0_ref[...] + _dot_f32(xa, a2_ref[...]))
    g_out[...] = _dot_f32(jax.nn.sigmoid(xg), g2_ref[...])
    if use_vres:
        xv = mixed(xv_ref, xvp_ref, 6)
        v = v + (vfirst_ref[...] - v) * jax.nn.sigmoid(v0_ref[...] + _dot_f32(xv, v2_ref[...]))
    kk = k * kk_ref[...]
    ss = _split_dot(kk * kk, gsum_ref[...], 3)
    kk = kk / jnp.maximum(jnp.sqrt(ss), 1e-12)
    r_out[...] = r
    k_out[...] = k * (1.0 + (a - 1.0) * ka_ref[...])
    v_out[...] = v
    an_out[...] = -kk
    b_out[...] = kk * a


def _rwkv_pre(proj, mix, w0, w2, a0, a2, v0, v2, g2, k_k, k_a, gsum, v_first, use_vres):
    t = proj.shape[0]
    tm = RW_PRE_TM

    def cur(width, off):
        return pl.BlockSpec((tm, width), lambda i: (i, off // width))

    def prev(width, off):
        return pl.BlockSpec((8, width), lambda i: (jnp.maximum(i * (tm // 8) - 1, 0), off // width))

    def full(a):
        return pl.BlockSpec(a.shape, lambda i: (0,) * a.ndim)

    segs = [(RW_W, C_RW_RKV), (RW_W, C_RW_RKV + RW_W), (RW_W, C_RW_RKV + 2 * RW_W),
            (128, C_RW_XW), (128, C_RW_XA), (256, C_RW_XG), (128, C_RW_XV)]
    params = [mix, w0, w2, a0, a2, v0, v2, g2, k_k, k_a, gsum]
    row = pl.BlockSpec((tm, RW_W), lambda i: (i, 0))
    outs = pl.pallas_call(
        functools.partial(_rwkv_pre_kernel, use_vres=use_vres),
        out_shape=[jax.ShapeDtypeStruct((t, RW_W), F32)] * 7,
        grid=(t // tm,),
        in_specs=[cur(w, o) for w, o in segs] + [prev(w, o) for w, o in segs] + [full(p) for p in params] + [row],
        out_specs=[row] * 7,
        compiler_params=_cparams(("parallel",)),
        name="rwkv_pre",
    )(*([proj] * 14), *params, v_first)
    return outs


RW_CHUNK = 128


def _wkv_kernel(r_ref, lw_ref, k_ref, v_ref, an_ref, b_ref, g_ref, lng_ref, lnb_ref, rk_ref,
                gsum_ref, tril_ref, o_ref, s_ref):
    c = pl.program_id(0)
    C = RW_CHUNK
    N = RW_N

    @pl.when(c == 0)
    def _():
        s_ref[...] = jnp.zeros_like(s_ref)

    r = r_ref[...]
    lw = lw_ref[...]
    k = k_ref[...]
    v = v_ref[...]
    lw_hi = lw.astype(BF16)
    rem = lw - lw_hi.astype(F32)
    lw_mid = rem.astype(BF16)
    lw_lo = (rem - lw_mid.astype(F32)).astype(BF16)
    tril = tril_ref[...]
    lg = _dot(tril, lw_hi) + (_dot(tril, lw_mid) + _dot(tril, lw_lo))
    gam = jnp.exp(lg)
    ginv = jnp.exp(-lg)
    at = (an_ref[...] * jnp.exp(lg - lw)).astype(BF16)
    bt = (b_ref[...] * ginv).astype(BF16)
    kt = (k * ginv).astype(BF16)
    rt_f = r * gam
    rt = rt_f.astype(BF16)
    vb = v.astype(BF16)
    g_last = gam[C - 1:C, :]

    rowi = lax.broadcasted_iota(jnp.int32, (C, C), 0)
    coli = lax.broadcasted_iota(jnp.int32, (C, C), 1)
    strict = rowi > coli
    incl = rowi >= coli

    heads = range(RW_HEADS)
    hs = [slice(h * N, (h + 1) * N) for h in heads]
    bth = [bt[:, s] for s in hs]
    kth = [kt[:, s] for s in hs]
    vh = [vb[:, s] for s in hs]
    big = [_dot_t(jnp.concatenate([at[:, s], rt[:, s]], axis=0), jnp.concatenate([bth[h], kth[h]], axis=0))
           for h, s in zip(heads, hs)]
    a_ab = [jnp.where(strict, big[h][:C, :C], 0.0) for h in heads]
    a_ak = [jnp.where(strict, big[h][:C, C:], 0.0).astype(BF16) for h in heads]
    a_rb = [jnp.where(incl, big[h][C:, :C], 0.0).astype(BF16) for h in heads]
    a_rk = [jnp.where(incl, big[h][C:, C:], 0.0).astype(BF16) for h in heads]
    akv = [_dot(a_ak[h], vh[h]) for h in heads]
    def same_block(size):
        shift = size.bit_length() - 1
        return (rowi >> shift) == (coli >> shift)

    tinv = [jnp.where(same_block(2), a_ab[h], 0.0) + jnp.where(rowi == coli, 1.0, 0.0) for h in heads]
    size = 2
    while size < C:
        lower_left = same_block(2 * size) & jnp.logical_not(same_block(size))
        off = [jnp.where(lower_left, a_ab[h], 0.0).astype(BF16) for h in heads]
        tb = [tinv[h].astype(BF16) for h in heads]
        half = [_dot(tb[h], off[h]).astype(BF16) for h in heads]
        tinv = [tinv[h] + _dot(half[h], tb[h]) for h in heads]
        size *= 2
    xb = [_dot(tinv[h].astype(BF16), jnp.concatenate([at[:, hs[h]], akv[h].astype(BF16)], axis=1)).astype(BF16)
          for h in heads]
    yx = [_dot(a_rb[h], xb[h]) for h in heads]
    ykv = [_dot(a_rk[h], vh[h]) for h in heads]
    xtb = [_dot_tl(xb[h], bth[h]) for h in heads]
    vtk = [_dot_tl(vh[h], kth[h]) for h in heads]
    ys = []
    for h in heads:
        gl = g_last[:, hs[h]]
        y1 = (rt_f[:, hs[h]] + yx[h][:, :N]).astype(BF16)
        s0 = s_ref[h]
        s0b = s0.astype(BF16)
        ys.append(_dot_t(y1, s0b) + (yx[h][:, N:] + ykv[h]))
        s_ref[h] = (s0 + _dot(s0b, xtb[h][:N].astype(BF16)) + (xtb[h][N:] + vtk[h])) * gl
    y = jnp.concatenate(ys, axis=1)

    gsum = gsum_ref[...]
    inv_n = 1.0 / N
    mu = _split_dot(y, gsum, 3) * inv_n
    yc = y - mu
    var = _split_dot(yc * yc, gsum, 3) * inv_n
    yn = yc * lax.rsqrt(var + RW_GN_EPS) * lng_ref[...] + lnb_ref[...]
    bonus = _split_dot(r * k * rk_ref[...], gsum, 3)
    o_ref[...] = ((yn + bonus * v) * g_ref[...]).astype(o_ref.dtype)


def _wkv(r, lw, k, v, an, b, g, ln_g, ln_b, r_k, gsum):
    t = r.shape[0]
    C = RW_CHUNK
    tril = jnp.asarray(np.tril(np.ones((C, C), np.float32)), BF16)
    row = pl.BlockSpec((C, RW_W), lambda c: (c, 0))
    vec = pl.BlockSpec((1, RW_W), lambda c: (0, 0))
    return pl.pallas_call(
        _wkv_kernel,
        out_shape=jax.ShapeDtypeStruct((t, RW_W), BF16),
        grid=(t // C,),
        in_specs=[row] * 7 + [vec] * 3 + [pl.BlockSpec((RW_W, RW_W), lambda c: (0, 0)),
                                          pl.BlockSpec((C, C), lambda c: (0, 0))],
        out_specs=row,
        scratch_shapes=[pltpu.VMEM((RW_HEADS, RW_N, RW_N), F32)],
        compiler_params=_cparams(("arbitrary",)),
        name="wkv7",
    )(r, lw, k, v, an, b, g, ln_g, ln_b, r_k, gsum, tril)


NSA_PREP_TM = 512


N_NORM_SLABS = (NSA_W + 2 * NSA_KV_W) // LANES
N_RAW_SLABS = 4 * NSA_KV_W // LANES
N_NSA_SLABS = N_NORM_SLABS + N_RAW_SLABS + 1


def _nsa_prep_kernel(*refs):
    x_refs = refs[:N_NSA_SLABS]
    (cos_ref, sin_ref, w_ref, gmean_ref, big_ref,
     qk_ref, hmat_ref, v1_ref, gate_ref, qpad_ref, ksx_ref) = refs[N_NSA_SLABS:]
    cos = cos_ref[...]
    sin = sin_ref[...]
    gmean = gmean_ref[...]
    half = NSA_DH // 2
    tm = cos.shape[0]
    lane = lax.broadcasted_iota(jnp.int32, cos.shape, 1)
    first_half = (lane % NSA_DH) < half
    low = lane < NSA_DH
    blk_row = (pl.program_id(0) * tm + lax.broadcasted_iota(jnp.int32, cos.shape, 0)) // SLC_BLOCK
    big = big_ref[...]
    mark_lo = jnp.where(blk_row == lane - NSA_DH, big, 0.0)
    mark_hi = jnp.where(low & (blk_row == lane + NSA_DH), big, 0.0).astype(BF16)
    n_q = NSA_W // LANES
    n_k = NSA_KV_W // LANES
    for s in range(N_NORM_SLABS):
        x = x_refs[s][...]
        y = x * lax.rsqrt(_split_dot(x * x, gmean, 3) + RMS_EPS) * w_ref[s]
        rot = jnp.where(first_half, pltpu.roll(y, LANES - half, axis=1), pltpu.roll(y, half, axis=1))
        res32 = y * cos + rot * sin
        res = res32.astype(qk_ref.dtype)
        qk_ref[2 * s] = res[:, :NSA_DH]
        qk_ref[2 * s + 1] = res[:, NSA_DH:]
        if s >= n_q + n_k:
            continue
        swapped = pltpu.roll(res32, NSA_DH, axis=1)
        if s < n_q:
            qpad_ref[2 * s] = jnp.where(low, res32, 0.0).astype(BF16)
            qpad_ref[2 * s + 1] = jnp.where(low, swapped, 0.0).astype(BF16)
        else:
            g0 = 2 * (s - n_q)
            ksx_ref[g0, :, :LANES] = jnp.where(low, res32, mark_lo).astype(BF16)
            ksx_ref[g0 + 1, :, :LANES] = jnp.where(low, swapped, mark_lo).astype(BF16)
            ksx_ref[g0, :, LANES:] = mark_hi
            ksx_ref[g0 + 1, :, LANES:] = mark_hi

    rows = hmat_ref.shape[2]
    low_r = lax.broadcasted_iota(jnp.int32, (rows, LANES), 1) < NSA_DH
    for s in range(NSA_KV_W // LANES * 2):
        kind, pair = divmod(s, NSA_KV_W // LANES)
        x_ref = x_refs[N_NORM_SLABS + s]
        for t2 in range(CMP_STRIDE // 2):
            r0 = x_ref[pl.ds(2 * t2, rows, stride=CMP_STRIDE), :]
            r1 = x_ref[pl.ds(2 * t2 + 1, rows, stride=CMP_STRIDE), :]
            cols = slice(t2 * LANES, (t2 + 1) * LANES)
            hmat_ref[kind, 2 * pair, :, cols] = jnp.where(low_r, r0, pltpu.roll(r1, NSA_DH, axis=1)).astype(BF16)
            hmat_ref[kind, 2 * pair + 1, :, cols] = jnp.where(low_r, pltpu.roll(r0, NSA_DH, axis=1), r1).astype(BF16)

    one_col = jnp.where(lane == NSA_DH, 1.0, 0.0)
    for s in range(NSA_KV_W // LANES * 2):
        x = x_refs[N_NORM_SLABS + N_RAW_SLABS // 2 + s][...]
        v1_ref[2 * s] = jnp.where(low, x, one_col).astype(BF16)
        v1_ref[2 * s + 1] = jnp.where(low, pltpu.roll(x, NSA_DH, axis=1), one_col).astype(BF16)

    xg = x_refs[N_NSA_SLABS - 1][...]
    per_group = 3 * NSA_HG
    for g in range(NSA_G):
        shifted = xg if g == 0 else pltpu.roll(xg, LANES - per_group * g, axis=1)
        gate_ref[g] = jnp.where(lane < per_group, shifted, 0.0)


def _nsa_prep(proj, w_slabs, cos, sin, gmean, big):
    t = proj.shape[0]
    assert t // SLC_BLOCK <= LANES
    tm = NSA_PREP_TM
    base = C_NSA_Q // LANES
    tab = pl.BlockSpec((tm, LANES), lambda i: (i, 0))
    n_heads_out = 2 * N_NORM_SLABS
    return pl.pallas_call(
        _nsa_prep_kernel,
        out_shape=[jax.ShapeDtypeStruct((n_heads_out, t, NSA_DH), BF16),
                   jax.ShapeDtypeStruct((2, NSA_G, t // CMP_STRIDE, CMP_STRIDE * NSA_DH), BF16),
                   jax.ShapeDtypeStruct((2 * NSA_G, t, LANES), BF16),
                   jax.ShapeDtypeStruct((NSA_G, t, LANES), F32),
                   jax.ShapeDtypeStruct((NSA_HEADS, t, LANES), BF16),
                   jax.ShapeDtypeStruct((NSA_G, t, 2 * LANES), BF16)],
        grid=(t // tm,),
        in_specs=[pl.BlockSpec((tm, LANES), lambda i, s=s: (i, base + s)) for s in range(N_NSA_SLABS)]
        + [tab, tab, pl.BlockSpec((N_NORM_SLABS, 1, LANES), lambda i: (0, 0, 0)),
           pl.BlockSpec((LANES, LANES), lambda i: (0, 0)), pl.BlockSpec((1, LANES), lambda i: (0, 0))],
        out_specs=[pl.BlockSpec((n_heads_out, tm, NSA_DH), lambda i: (0, i, 0)),
                   pl.BlockSpec((2, NSA_G, tm // CMP_STRIDE, CMP_STRIDE * NSA_DH), lambda i: (0, 0, i, 0)),
                   pl.BlockSpec((2 * NSA_G, tm, LANES), lambda i: (0, i, 0)),
                   pl.BlockSpec((NSA_G, tm, LANES), lambda i: (0, i, 0)),
                   pl.BlockSpec((NSA_HEADS, tm, LANES), lambda i: (0, i, 0)),
                   pl.BlockSpec((NSA_G, tm, 2 * LANES), lambda i: (0, i, 0))],
        compiler_params=_cparams(("parallel",)),
        name="nsa_prep",
    )(*([proj] * N_NSA_SLABS), cos, sin, w_slabs, gmean, big)


def _compress_kernel(h_ref, w1_ref, w2_ref, pe_ref, nw_ref, cos_ref, sin_ref, o_ref):
    kind = pl.program_id(0)
    nc = h_ref.shape[2]
    half_in = CMP_STRIDE * NSA_DH
    hm = h_ref[0, 0]
    w1 = w1_ref[0]
    first = _dot(hm, w1[:half_in])
    second = _dot(hm, w1[half_in:])
    const = _dot(pe_ref[0], w1)[0:1, :]
    pre = first + pltpu.roll(second, nc - 1, axis=0) + const
    hid = 0.5 * pre * (1.0 + jnp.tanh(0.7978845608028654 * (pre + 0.044715 * pre * pre * pre)))
    out = _dot(hid.astype(BF16), w2_ref[0])
    y = out * lax.rsqrt(jnp.mean(out * out, axis=-1, keepdims=True) + RMS_EPS) * nw_ref[...]
    half = NSA_DH // 2
    rot = jnp.concatenate([y[:, half:], y[:, :half]], axis=1)
    roped = y * cos_ref[...] + rot * sin_ref[...]
    o_ref[0, 0] = jnp.where(kind == 0, roped, out).astype(o_ref.dtype)


def _compress(hmat, w1, w2, pe, nw, cos_c, sin_c):
    _, g, nc, width = hmat.shape
    return pl.pallas_call(
        _compress_kernel,
        out_shape=jax.ShapeDtypeStruct((2, g, nc, NSA_DH), BF16),
        grid=(2, g),
        in_specs=[pl.BlockSpec((1, 1, nc, width), lambda a, b: (a, b, 0, 0)),
                  pl.BlockSpec((1, 2 * width, CMP_HIDDEN), lambda a, b: (a, 0, 0)),
                  pl.BlockSpec((1, CMP_HIDDEN, NSA_DH), lambda a, b: (a, 0, 0)),
                  pl.BlockSpec((1, 8, 2 * width), lambda a, b: (a, 0, 0)),
                  pl.BlockSpec((1, NSA_DH), lambda a, b: (0, 0)),
                  pl.BlockSpec((nc, NSA_DH), lambda a, b: (0, 0)),
                  pl.BlockSpec((nc, NSA_DH), lambda a, b: (0, 0))],
        out_specs=pl.BlockSpec((1, 1, nc, NSA_DH), lambda a, b: (a, b, 0, 0)),
        compiler_params=_cparams(("parallel", "parallel")),
        name="nsa_compress",
    )(hmat, w1, w2, pe, nw, cos_c, sin_c)


SLC_KT = 1024
WIN_KEYS = WINDOW + NSA_TQ


INT32_MIN = -2 ** 31


def _nsa_attn_kernel(q_ref, qpad_ref, gate_ref, kc_ref, vc_ref, ks_ref, vs_ref, kw_ref, vw_ref, cis_ref, ltri_ref,
                     o_ref):
    qi = pl.program_id(1)
    nc = kc_ref.shape[2]
    nb = cis_ref.shape[1]
    heads = range(NSA_HG)
    start = qi * NSA_TQ
    qh = [q_ref[hh] for hh in heads]
    tok = start + lax.broadcasted_iota(jnp.int32, (NSA_TQ, 1), 0)

    def softmax_terms(scores):
        return [jnp.exp(s - jnp.max(s, axis=-1, keepdims=True)) for s in scores]

    kc = kc_ref[0, 0]
    vc = vc_ref[0, 0]
    cend = lax.broadcasted_iota(jnp.int32, (1, nc), 1) * CMP_STRIDE + (CMP_BLOCK - 1)
    bias_c = jnp.where(cend <= tok, 0.0, NEG_INF)
    ec = softmax_terms([_dot_t(qh[hh], kc) + bias_c for hh in heads])
    sees_any = tok >= CMP_BLOCK - 1
    pc = [ec[hh] * jnp.where(sees_any, 1.0 / jnp.sum(ec[hh], axis=-1, keepdims=True), 0.0) for hh in heads]
    o_c = [_dot(pc[hh].astype(BF16), vc) for hh in heads]

    psum = pc[0]
    for hh in range(1, NSA_HG):
        psum = psum + pc[hh]
    score = _split_dot(psum, cis_ref[...], 3)
    blk = lax.broadcasted_iota(jnp.int32, (1, nb), 1)
    cur = tok // SLC_BLOCK
    forced = (blk == 0) | (blk == cur) | (blk == cur - 1)
    score = jnp.where(forced, FORCED_SCORE, score)
    score = jnp.where(blk <= cur, score, -jnp.inf)

    kbase = pl.multiple_of(jnp.maximum(start - WINDOW, 0), NSA_TQ)
    kw = kw_ref[0, pl.ds(kbase, WIN_KEYS), :]
    lag = tok - (kbase + lax.broadcasted_iota(jnp.int32, (1, WIN_KEYS), 1))
    bias_w = jnp.where((lag >= 0) & (lag < WINDOW), 0.0, NEG_INF)
    sw = [_dot_t(qh[hh], kw) for hh in heads]

    bits = lax.bitcast_convert_type(score.T, jnp.int32)
    key = bits ^ ((bits >> 31) & 0x7FFFFFFF)
    n_sel = min(SLC_TOPK, nb)

    def enough(c):
        return jnp.sum(jnp.where(key >= c, 1.0, 0.0), axis=0, keepdims=True) >= n_sel

    def enough3(c1, c2, c3):
        packed = jnp.where(key >= c3, 65793.0, jnp.where(key >= c2, 257.0, jnp.where(key >= c1, 1.0, 0.0)))
        tot = jnp.sum(packed, axis=0, keepdims=True).astype(jnp.int32)
        return (tot & 255) >= n_sel, ((tot >> 8) & 255) >= n_sel, (tot >> 16) >= n_sel

    zero_row = jnp.zeros((1, NSA_TQ), jnp.int32)
    thr = jnp.where(enough(zero_row), zero_row, INT32_MIN)
    for hi in range(30, 0, -2):
        c1 = thr + (1 << (hi - 1))
        c2 = thr + (1 << hi)
        c3 = c2 + (1 << (hi - 1))
        e1, e2, e3 = enough3(c1, c2, c3)
        thr = jnp.where(e3, c3, jnp.where(e2, c2, jnp.where(e1, c1, thr)))
    c1 = thr + 1
    thr = jnp.where(enough(c1), c1, thr)
    above = key > thr
    tied = key == thr
    need = n_sel - jnp.sum(jnp.where(above, 1.0, 0.0), axis=0, keepdims=True)
    tied_before = _dot(ltri_ref[...], jnp.where(tied, 1.0, 0.0).astype(BF16))
    sel_t = above | (tied & (tied_before < need))
    sel = jnp.where(sel_t, 1.0, 0.0).T
    if nb < LANES:
        sel = jnp.concatenate([sel, jnp.zeros((NSA_TQ, LANES - nb), F32)], axis=1)
    lane_q = lax.broadcasted_iota(jnp.int32, (NSA_TQ, LANES), 1)
    sel_sw = pltpu.roll(sel, NSA_DH, axis=1)
    sel_lo = jnp.where(lane_q >= NSA_DH, sel_sw, 0.0)
    sel_hi = jnp.where(lane_q < NSA_DH, sel_sw, 0.0)
    qx = [jnp.concatenate([qpad_ref[hh].astype(F32) + sel_lo, sel_hi], axis=1).astype(BF16) for hh in heads]

    sw = [sw[hh] + bias_w for hh in heads]
    ew = [jnp.exp((s - jnp.max(s, axis=-1, keepdims=True)).astype(BF16)) for s in sw]
    acc_w = [_dot(ew[hh], vw_ref[0, pl.ds(kbase, WIN_KEYS), :]) for hh in heads]

    n_tiles = (start + NSA_TQ + SLC_KT - 1) // SLC_KT

    def sel_tile(j, carry, causal):
        ms, accs = carry
        k0 = pl.multiple_of(j * SLC_KT, SLC_KT)
        kt = ks_ref[0, pl.ds(k0, SLC_KT), :]
        vt = vs_ref[0, pl.ds(k0, SLC_KT), :]
        sj = [_dot_t(qx[hh], kt) for hh in heads]
        if causal:
            future = (k0 + lax.broadcasted_iota(jnp.int32, (1, SLC_KT), 1)) > tok
            sj = [jnp.where(future, NEG_INF, s) for s in sj]
        m_new = [jnp.maximum(ms[hh], jnp.max(sj[hh], axis=-1, keepdims=True)) for hh in heads]
        pj = [jnp.exp((sj[hh] - m_new[hh]).astype(BF16)) for hh in heads]
        accs = [jnp.exp(ms[hh] - m_new[hh]) * accs[hh] + _dot(pj[hh], vt) for hh in heads]
        return tuple(m_new), tuple(accs)

    m0 = tuple(jnp.full((NSA_TQ, 1), NEG_INF, F32) for _ in heads)
    a0 = tuple(jnp.zeros((NSA_TQ, LANES), F32) for _ in heads)
    carry = lax.fori_loop(0, n_tiles - 1, functools.partial(sel_tile, causal=False), (m0, a0))
    _, acc_s = sel_tile(n_tiles - 1, carry, causal=True)

    gt = jax.nn.sigmoid(gate_ref[0])
    outs = []
    for hh in heads:
        g0, g1, g2 = (gt[:, 3 * hh + br:3 * hh + br + 1] for br in range(3))
        scale_s = g1 / acc_s[hh][:, NSA_DH:NSA_DH + 1]
        scale_w = g2 / acc_w[hh][:, NSA_DH:NSA_DH + 1]
        outs.append(g0 * o_c[hh] + scale_s * acc_s[hh][:, :NSA_DH] + scale_w * acc_w[hh][:, :NSA_DH])
    o_ref[...] = jnp.concatenate(outs, axis=1).astype(o_ref.dtype)


def _nsa_attn(qk_hm, qpad, ksx, gates, cmp_kv, v1, cis, ltri):
    t = qk_hm.shape[1]
    nq = t // NSA_TQ
    nc = cmp_kv.shape[2]
    nb = cis.shape[1]
    ks_spec = pl.BlockSpec((1, t, 2 * LANES), lambda g, i: (g, 0, 0))
    kw_spec = pl.BlockSpec((1, t, NSA_DH), lambda g, i: (NSA_HEADS + NSA_G + g, 0, 0))
    vs_spec = pl.BlockSpec((1, t, LANES), lambda g, i: (g, 0, 0))
    vw_spec = pl.BlockSpec((1, t, LANES), lambda g, i: (NSA_G + g, 0, 0))
    return pl.pallas_call(
        _nsa_attn_kernel,
        out_shape=jax.ShapeDtypeStruct((t, NSA_W), BF16),
        grid=(NSA_G, nq),
        in_specs=[pl.BlockSpec((NSA_HG, NSA_TQ, NSA_DH), lambda g, i: (g, i, 0)),
                  pl.BlockSpec((NSA_HG, NSA_TQ, LANES), lambda g, i: (g, i, 0)),
                  pl.BlockSpec((1, NSA_TQ, LANES), lambda g, i: (g, i, 0)),
                  pl.BlockSpec((1, 1, nc, NSA_DH), lambda g, i: (0, g, 0, 0)),
                  pl.BlockSpec((1, 1, nc, NSA_DH), lambda g, i: (1, g, 0, 0)),
                  ks_spec, vs_spec, kw_spec, vw_spec,
                  pl.BlockSpec((nc, nb), lambda g, i: (0, 0)),
                  pl.BlockSpec((nb, nb), lambda g, i: (0, 0))],
        out_specs=pl.BlockSpec((NSA_TQ, NSA_HG * NSA_DH), lambda g, i: (i, g)),
        compiler_params=_cparams(("parallel", "arbitrary")),
        name="nsa_attention",
    )(qk_hm, qpad, gates, cmp_kv, cmp_kv, ksx, v1, qk_hm, v1, cis, ltri)


def _rope_tables(pos, dh, reps):
    half = dh // 2
    inv_freq = ROPE_THETA ** (-np.arange(half, dtype=np.float64) / half)
    ang = np.asarray(pos, np.float64)[:, None] * inv_freq[None, :]
    cos = np.cos(ang)
    sin = np.sin(ang)
    cos_t = np.tile(np.concatenate([cos, cos], axis=1), (1, reps))
    sin_t = np.tile(np.concatenate([-sin, sin], axis=1), (1, reps))
    return jnp.asarray(cos_t, F32), jnp.asarray(sin_t, F32)


def _pad_cols(a, width):
    return jnp.pad(a, ((0, 0), (0, width - a.shape[1])))


def _pad_rows(a, height):
    return jnp.pad(a, ((0, height - a.shape[0]), (0, 0)))


def _pack_w_in(w_l, w_vres):
    d = w_l.shape[0]
    o = 0
    ret = w_l[:, o:o + 4 * RET_W]; o += 4 * RET_W
    rkv = w_l[:, o:o + 3 * RW_W]; o += 3 * RW_W
    xw = w_l[:, o:o + RW_DECAY_RANK]; o += RW_DECAY_RANK
    xa = w_l[:, o:o + RW_A_RANK]; o += RW_A_RANK
    xg = w_l[:, o:o + RW_GATE_RANK]; o += RW_GATE_RANK
    q = w_l[:, o:o + NSA_W]; o += NSA_W
    kc, vc, ks, vs, kw, vw = (w_l[:, o + i * NSA_KV_W:o + (i + 1) * NSA_KV_W] for i in range(6))
    o += 6 * NSA_KV_W
    gates = w_l[:, o:o + 3 * NSA_HEADS]
    xv = jnp.zeros((d, LANES), w_l.dtype) if w_vres is None else _pad_cols(w_vres, LANES)
    return jnp.concatenate([ret, rkv, _pad_cols(xw, LANES), _pad_cols(xa, LANES), xg, xv,
                            q, ks, kw, kc, vc, vs, vw, _pad_cols(gates, LANES)], axis=1)


def _pack_mix(mix, vres_mix):
    o = 3 * RW_W
    xw = mix[o:o + RW_DECAY_RANK]; o += RW_DECAY_RANK
    xa = mix[o:o + RW_A_RANK]; o += RW_A_RANK
    xg = mix[o:o + RW_GATE_RANK]
    z = lambda n: jnp.zeros((n,), mix.dtype)
    xv = z(LANES) if vres_mix is None else jnp.concatenate([vres_mix, z(LANES - RW_V_RANK)])
    return jnp.concatenate([mix[:3 * RW_W], xw, z(LANES - RW_DECAY_RANK), xa, z(LANES - RW_A_RANK), xg, xv])[None, :]


def kernel(x, ln1_g, w_in, w_in_vres, rwkv_mix, rwkv_vres_mix, rwkv_w0, rwkv_w2, rwkv_a0, rwkv_a2, rwkv_v0, rwkv_v2, rwkv_g2, rwkv_k_k, rwkv_k_a, rwkv_r_k, rwkv_ln_g, rwkv_ln_b, nsa_q_norm, nsa_k_norm, nsa_cmp_pe, nsa_cmp_k_w1, nsa_cmp_k_w2, nsa_cmp_v_w1, nsa_cmp_v_w2, w_out, ln2_g, w_up, w_down):
    bsz, t, d = x.shape
    assert bsz == 1
    depth = w_in.shape[0]
    nc = t // CMP_STRIDE
    nb = t // SLC_BLOCK
    xs = x.reshape(t, d)

    pos = np.arange(t)
    cos_r, sin_r = _rope_tables(pos, RET_DH, 1)
    cos_n, sin_n = _rope_tables(pos, NSA_DH, 2)
    cos_c, sin_c = _rope_tables(np.arange(nc) * CMP_STRIDE + (CMP_BLOCK - 1), NSA_DH, 1)
    lane_head = np.arange(RW_W) // RW_N
    gsum = jnp.asarray(lane_head[:, None] == lane_head[None, :], BF16)
    lane_h2 = np.arange(LANES) // NSA_DH
    gmean = jnp.asarray((lane_h2[:, None] == lane_h2[None, :]) / float(NSA_DH), BF16)
    cstart = np.arange(nc) * CMP_STRIDE
    sstart = np.arange(nb) * SLC_BLOCK
    cis = jnp.asarray((cstart[:, None] <= sstart[None, :] + SLC_BLOCK - 1)
                      & (cstart[:, None] + CMP_BLOCK - 1 >= sstart[None, :]), BF16)

    ltri = jnp.asarray(np.tril(np.ones((nb, nb), np.float32), -1), BF16)

    v_first = jnp.zeros((t, RW_W), F32)
    for l in range(depth):
        vres = l > 0
        w_cat = _pack_w_in(w_in[l], w_in_vres[l - 1] if vres else None).astype(BF16)
        proj = _norm_matmul(xs, ln1_g[l], w_cat, name="in_proj")

        o_ret = _retention(proj, cos_r, sin_r)

        row = lambda a: a.reshape(1, -1)
        mix = _pack_mix(rwkv_mix[l], rwkv_vres_mix[l - 1] if vres else None)
        v0 = row(rwkv_v0[l - 1]) if vres else jnp.zeros((1, RW_W), F32)
        v2 = _pad_rows(rwkv_v2[l - 1], LANES) if vres else jnp.zeros((LANES, RW_W), F32)
        r_, lw_, k_, v_, an_, b_, g_ = _rwkv_pre(
            proj, mix, row(rwkv_w0[l]), _pad_rows(rwkv_w2[l], LANES), row(rwkv_a0[l]), _pad_rows(rwkv_a2[l], LANES),
            v0, v2, rwkv_g2[l], row(rwkv_k_k[l]), row(rwkv_k_a[l]), gsum, v_first, vres)
        if not vres:
            v_first = v_
        o_rwkv = _wkv(r_, lw_, k_, v_, an_, b_, g_, row(rwkv_ln_g[l]), row(rwkv_ln_b[l]), row(rwkv_r_k[l]), gsum)

        n_q = NSA_W // LANES
        n_k = NSA_KV_W // LANES
        w_slabs = jnp.stack([jnp.tile(nsa_q_norm[l] * (NSA_DH ** -0.5), 2)] * n_q
                            + [jnp.tile(nsa_k_norm[l, 1], 2)] * n_k + [jnp.tile(nsa_k_norm[l, 2], 2)] * n_k)[:, None, :]
        s_max = 1.02 * jnp.max(jnp.abs(nsa_q_norm[l])) * jnp.max(jnp.abs(nsa_k_norm[l, 1])) * (NSA_DH ** 0.5)
        big = jnp.full((1, LANES), jnp.exp2(jnp.ceil(jnp.log2(2.0 * s_max + 128.0))), F32)
        qk_hm, hmat, v1, gates, qpad, ksx = _nsa_prep(proj, w_slabs, cos_n, sin_n, gmean, big)
        w1 = jnp.stack([nsa_cmp_k_w1[l], nsa_cmp_v_w1[l]]).astype(BF16)
        w2 = jnp.stack([nsa_cmp_k_w2[l], nsa_cmp_v_w2[l]]).astype(BF16)
        pe = jnp.broadcast_to(nsa_cmp_pe[l].reshape(2, 1, CMP_BLOCK * NSA_DH), (2, 8, CMP_BLOCK * NSA_DH)).astype(BF16)
        cmp_kv = _compress(hmat, w1, w2, pe, row(nsa_k_norm[l, 0]), cos_c, sin_c)
        o_nsa = _nsa_attn(qk_hm, qpad, ksx, gates, cmp_kv, v1, cis, ltri)

        xs = _out_proj(o_ret, o_rwkv, o_nsa, w_out[l], xs)

        up = _norm_matmul(xs, ln2_g[l], w_up[l], act="relu2", out_dtype=BF16, name="mlp_up")
        xs = _matmul(up, w_down[l], res=xs, name="mlp_down")
    return xs.reshape(bsz, t, d)
```

```python
import functools

import numpy as np
import jax
import jax.numpy as jnp
from jax import lax
from jax.experimental import pallas as pl
from jax.experimental.pallas import tpu as pltpu

F32 = jnp.float32
BF16 = jnp.bfloat16

D_MODEL = 2048
RET_HEADS, RET_DH = 4, 128
RET_W = RET_HEADS * RET_DH
RW_HEADS, RW_N = 8, 64
RW_W = RW_HEADS * RW_N
RW_DECAY_RANK, RW_A_RANK, RW_V_RANK, RW_GATE_RANK = 96, 96, 64, 256
RW_GN_EPS = 64e-5
NSA_HEADS, NSA_G, NSA_DH = 16, 4, 64
NSA_HG = NSA_HEADS // NSA_G
NSA_W = NSA_HEADS * NSA_DH
NSA_KV_W = NSA_G * NSA_DH
CMP_BLOCK, CMP_STRIDE, CMP_HIDDEN = 32, 16, 256
SLC_BLOCK, SLC_TOPK, WINDOW = 64, 16, 512
NSA_TQ = 256
ROPE_THETA = 10000.0
MLP_HIDDEN = 4 * D_MODEL
RMS_EPS = 1e-6
NEG_INF = -1e30
FORCED_SCORE = 1e9

LANES = 128
VMEM_LIMIT = 56 * 1024 * 1024

C_RET = 0
C_RW_RKV = 2048
C_RW_XW = 3584
C_RW_XA = 3712
C_RW_XG = 3840
C_RW_XV = 4096
C_NSA_Q = 4224
C_NSA_KS = 5248
C_NSA_KW = 5504
C_NSA_KC = 5760
C_NSA_VC = 6016
C_NSA_VS = 6272
C_NSA_VW = 6528
C_NSA_GATE = 6784
PROJ_W = 6912


def _cparams(sem):
    return pltpu.CompilerParams(dimension_semantics=sem, vmem_limit_bytes=VMEM_LIMIT)


def _dot(a, b):
    return jnp.dot(a, b, preferred_element_type=F32)


def _dot_t(a, b):
    return lax.dot_general(a, b, (((1,), (1,)), ((), ())), preferred_element_type=F32)


def _dot_tl(a, b):
    return lax.dot_general(a, b, (((0,), (0,)), ((), ())), preferred_element_type=F32)


def _split_dot(x, m_bf16, passes):
    hi = x.astype(BF16)
    acc = _dot(hi, m_bf16)
    rem = x - hi.astype(F32)
    for _ in range(passes - 1):
        piece = rem.astype(BF16)
        acc = acc + _dot(piece, m_bf16)
        rem = rem - piece.astype(F32)
    return acc


def _mm_kernel(*refs, nk, act, has_res):
    a_ref, b_ref = refs[:2]
    r_ref = refs[2] if has_res else None
    o_ref = refs[3] if has_res else refs[2]

    def finish(y):
        if act == "relu2":
            y = jnp.square(jnp.maximum(y, 0.0))
        if has_res:
            y = y + r_ref[...]
        o_ref[...] = y.astype(o_ref.dtype)

    part = _dot(a_ref[...], b_ref[...].astype(BF16))
    if nk == 1:
        finish(part)
        return
    acc_ref = refs[-1]
    k = pl.program_id(2)

    @pl.when(k == 0)
    def _():
        acc_ref[...] = part

    @pl.when(k > 0)
    def _():
        acc_ref[...] += part

    @pl.when(k == nk - 1)
    def _():
        finish(acc_ref[...])


def _pick(n, cands):
    for c in cands:
        if n % c == 0:
            return c
    raise ValueError(f"no tile for {n}")


def _matmul(a, b, *, res=None, act=None, out_dtype=F32, name="matmul"):
    m, kd = a.shape
    _, n = b.shape
    tm = _pick(m, (1024, 512, 256, 128))
    tn = _pick(n, (1024, 768, 512, 256, 128))
    tk = _pick(kd, (2048, 1024, 512))
    nk = kd // tk
    in_specs = [pl.BlockSpec((tm, tk), lambda i, j, k: (i, k)), pl.BlockSpec((tk, tn), lambda i, j, k: (k, j))]
    args = [a, b]
    if res is not None:
        in_specs.append(pl.BlockSpec((tm, tn), lambda i, j, k: (i, j)))
        args.append(res)
    return pl.pallas_call(
        functools.partial(_mm_kernel, nk=nk, act=act, has_res=res is not None),
        out_shape=jax.ShapeDtypeStruct((m, n), out_dtype),
        grid=(m // tm, n // tn, nk),
        in_specs=in_specs,
        out_specs=pl.BlockSpec((tm, tn), lambda i, j, k: (i, j)),
        scratch_shapes=[pltpu.VMEM((tm, tn), F32)] if nk > 1 else [],
        compiler_params=_cparams(("parallel", "parallel", "arbitrary")),
        name=name,
    )(*args)


def _norm_mm_kernel(x_ref, g_ref, b_ref, o_ref, h_ref, *, act):
    @pl.when(pl.program_id(1) == 0)
    def _():
        x = x_ref[...]
        ms = jnp.mean(x * x, axis=-1, keepdims=True)
        h_ref[...] = (x * lax.rsqrt(ms + RMS_EPS) * g_ref[...]).astype(BF16)

    y = _dot(h_ref[...], b_ref[...].astype(BF16))
    if act == "relu2":
        y = jnp.square(jnp.maximum(y, 0.0))
    o_ref[...] = y.astype(o_ref.dtype)


def _norm_matmul(x, g, b, *, act=None, out_dtype=F32, name="norm_matmul"):
    m, kd = x.shape
    n = b.shape[1]
    tm = _pick(m, (1024, 512, 256, 128))
    tn = _pick(n, (1024, 768, 512, 256, 128))
    return pl.pallas_call(
        functools.partial(_norm_mm_kernel, act=act),
        out_shape=jax.ShapeDtypeStruct((m, n), out_dtype),
        grid=(m // tm, n // tn),
        in_specs=[pl.BlockSpec((tm, kd), lambda i, j: (i, 0)),
                  pl.BlockSpec((1, kd), lambda i, j: (0, 0)),
                  pl.BlockSpec((kd, tn), lambda i, j: (0, j))],
        out_specs=pl.BlockSpec((tm, tn), lambda i, j: (i, j)),
        scratch_shapes=[pltpu.VMEM((tm, kd), BF16)],
        compiler_params=_cparams(("parallel", "arbitrary")),
        name=name,
    )(x, g.reshape(1, kd), b)


def _out_proj_kernel(ret_ref, rw_ref, nsa_ref, w_ref, res_ref, o_ref):
    k0 = ret_ref.shape[1]
    k1 = k0 + rw_ref.shape[1]
    acc = _dot(ret_ref[...], w_ref[:k0, :].astype(BF16))
    acc = acc + _dot(rw_ref[...], w_ref[k0:k1, :].astype(BF16))
    acc = acc + _dot(nsa_ref[...], w_ref[k1:, :].astype(BF16))
    o_ref[...] = acc + res_ref[...]


def _out_proj(o_ret, o_rwkv, o_nsa, w, res):
    m = o_ret.shape[0]
    kd, n = w.shape
    assert kd == o_ret.shape[1] + o_rwkv.shape[1] + o_nsa.shape[1]
    tm = _pick(m, (1024, 512, 256, 128))
    tn = _pick(n, (1024, 512, 256, 128))

    def rows(a):
        return pl.BlockSpec((tm, a.shape[1]), lambda i, j: (i, 0))

    return pl.pallas_call(
        _out_proj_kernel,
        out_shape=jax.ShapeDtypeStruct((m, n), F32),
        grid=(m // tm, n // tn),
        in_specs=[rows(o_ret), rows(o_rwkv), rows(o_nsa),
                  pl.BlockSpec((kd, tn), lambda i, j: (0, j)),
                  pl.BlockSpec((tm, tn), lambda i, j: (i, j))],
        out_specs=pl.BlockSpec((tm, tn), lambda i, j: (i, j)),
        compiler_params=_cparams(("parallel", "parallel")),
        name="out_proj",
    )(o_ret, o_rwkv, o_nsa, w, res)


RET_CHUNK = 256


def _retention_kernel(q_ref, k_ref, v_ref, g_ref, cos_ref, sin_ref, dmat_ref, qd_ref, kd_ref, cd_ref, o_ref, state_ref):
    @pl.when(pl.program_id(0) == 0)
    def _():
        state_ref[...] = jnp.zeros_like(state_ref)

    cos = cos_ref[...]
    sin = sin_ref[...]

    def rope(x):
        return x * cos + pltpu.roll(x, RET_DH // 2, axis=1) * sin

    heads = range(RET_HEADS)
    hs = [slice(h * RET_DH, (h + 1) * RET_DH) for h in heads]
    q_all, k_all, v_all, gate = q_ref[...], k_ref[...], v_ref[...], g_ref[...]
    qb = [rope(q_all[:, s]).astype(BF16) for s in hs]
    kf = [rope(k_all[:, s]) * (RET_DH ** -0.5) for s in hs]
    kb = [kf[h].astype(BF16) for h in heads]
    vb = [v_all[:, s].astype(BF16) for s in hs]
    inner = [(_dot_t(qb[h], kb[h]) * dmat_ref[h]).astype(BF16) for h in heads]
    cross = [_dot(qb[h], state_ref[h].astype(BF16)) * qd_ref[h] for h in heads]
    kv = [_dot_tl((kf[h] * kd_ref[h]).astype(BF16), vb[h]) for h in heads]
    outs = []
    for h in heads:
        out = _dot(inner[h], vb[h]) + cross[h]
        state_ref[h] = cd_ref[h] * state_ref[h] + kv[h]
        outs.append(out * lax.rsqrt(jnp.mean(out * out, axis=-1, keepdims=True) + RMS_EPS))
    o_ref[...] = (gate * jax.nn.sigmoid(gate) * jnp.concatenate(outs, axis=1)).astype(o_ref.dtype)


def _retention(proj, cos, sin):
    t = proj.shape[0]
    chunk = RET_CHUNK
    log_gamma = np.log(1.0 - 2.0 ** (-5.0 - np.arange(RET_HEADS, dtype=np.float64)))[:, None, None]
    n = np.arange(chunk, dtype=np.float64)
    lag = n[:, None] - n[None, :]
    dmat = np.where(lag >= 0, np.exp(np.maximum(lag, 0.0)[None] * log_gamma), 0.0)
    ones = np.ones((1, 1, RET_DH))
    qd = np.exp((n + 1.0)[None, :, None] * log_gamma) * ones
    kd = np.exp((chunk - 1.0 - n)[None, :, None] * log_gamma) * ones
    cd = np.exp(chunk * log_gamma) * ones
    tables = [jnp.asarray(a, F32) for a in (dmat, qd, kd, cd)]
    base = C_RET // RET_W

    def col(off):
        return pl.BlockSpec((chunk, RET_W), lambda c, off=off: (c, base + off))

    def full(a):
        return pl.BlockSpec(a.shape, lambda c: (0, 0, 0))

    tab = pl.BlockSpec((chunk, RET_DH), lambda c: (c, 0))
    return pl.pallas_call(
        _retention_kernel,
        out_shape=jax.ShapeDtypeStruct((t, RET_W), BF16),
        grid=(t // chunk,),
        in_specs=[col(0), col(1), col(2), col(3), tab, tab] + [full(a) for a in tables],
        out_specs=pl.BlockSpec((chunk, RET_W), lambda c: (c, 0)),
        scratch_shapes=[pltpu.VMEM((RET_HEADS, RET_DH, RET_DH), F32)],
        compiler_params=_cparams(("arbitrary",)),
        name="retention",
    )(proj, proj, proj, proj, cos, sin, *tables)


def _dot_f32(a, b):
    a_hi = a.astype(BF16)
    b_hi = b.astype(BF16)
    a_lo = (a - a_hi.astype(F32)).astype(BF16)
    b_lo = (b - b_hi.astype(F32)).astype(BF16)
    return _dot(a_hi, b_hi) + (_dot(a_hi, b_lo) + _dot(a_lo, b_hi))


RW_PRE_TM = 256
_MIX_OFF = (0, 512, 1024, 1536, 1664, 1792, 2048, 2176)


def _rwkv_pre_kernel(r_ref, k_ref, v_ref, xw_ref, xa_ref, xg_ref, xv_ref,
                     rp_ref, kp_ref, vp_ref, xwp_ref, xap_ref, xgp_ref, xvp_ref,
                     mix_ref, w0_ref, w2_ref, a0_ref, a2_ref, v0_ref, v2_ref, g2_ref, kk_ref, ka_ref,
                     gsum_ref, vfirst_ref,
                     r_out, lw_out, k_out, v_out, an_out, b_out, g_out, *, use_vres):
    i = pl.program_id(0)

    def mixed(cur_ref, prev_ref, seg):
        cur = cur_ref[...]
        mix = mix_ref[:, _MIX_OFF[seg]:_MIX_OFF[seg + 1]]
        prev_row = jnp.where(i > 0, prev_ref[7:8, :], 0.0)
        rid = lax.broadcasted_iota(jnp.int32, cur.shape, 0)
        shifted = jnp.where(rid == 0, prev_row, pltpu.roll(cur, 1, axis=0))
        return cur + mix * (shifted - cur)

    r = mixed(r_ref, rp_ref, 0)
    k = mixed(k_ref, kp_ref, 1)
    v = mixed(v_ref, vp_ref, 2)
    xw = mixed(xw_ref, xwp_ref, 3)
    xa = mixed(xa_ref, xap_ref, 4)
    xg = mixed(xg_ref, xgp_ref, 5)

    z = -(w0_ref[...] + _dot_f32(jnp.tanh(xw), w2_ref[...]))
    softplus = jnp.maximum(z, 0.0) + jnp.log(1.0 + jnp.exp(-jnp.abs(z)))
    w = -softplus - 0.5
    lw_out[...] = -jnp.exp(w)
    a = jax.nn.sigmoid(a0_ref[...] + _dot_f32(xa, a2_ref[...]))
    g_out[...] = _dot_f32(jax.nn.sigmoid(xg), g2_ref[...])
    if use_vres:
        xv = mixed(xv_ref, xvp_ref, 6)
        v = v + (vfirst_ref[...] - v) * jax.nn.sigmoid(v0_ref[...] + _dot_f32(xv, v2_ref[...]))
    kk = k * kk_ref[...]
    ss = _split_dot(kk * kk, gsum_ref[...], 3)
    kk = kk / jnp.maximum(jnp.sqrt(ss), 1e-12)
    r_out[...] = r
    k_out[...] = k * (1.0 + (a - 1.0) * ka_ref[...])
    v_out[...] = v
    an_out[...] = -kk
    b_out[...] = kk * a


def _rwkv_pre(proj, mix, w0, w2, a0, a2, v0, v2, g2, k_k, k_a, gsum, v_first, use_vres):
    t = proj.shape[0]
    tm = RW_PRE_TM

    def cur(width, off):
        return pl.BlockSpec((tm, width), lambda i: (i, off // width))

    def prev(width, off):
        return pl.BlockSpec((8, width), lambda i: (jnp.maximum(i * (tm // 8) - 1, 0), off // width))

    def full(a):
        return pl.BlockSpec(a.shape, lambda i: (0,) * a.ndim)

    segs = [(RW_W, C_RW_RKV), (RW_W, C_RW_RKV + RW_W), (RW_W, C_RW_RKV + 2 * RW_W),
            (128, C_RW_XW), (128, C_RW_XA), (256, C_RW_XG), (128, C_RW_XV)]
    params = [mix, w0, w2, a0, a2, v0, v2, g2, k_k, k_a, gsum]
    row = pl.BlockSpec((tm, RW_W), lambda i: (i, 0))
    outs = pl.pallas_call(
        functools.partial(_rwkv_pre_kernel, use_vres=use_vres),
        out_shape=[jax.ShapeDtypeStruct((t, RW_W), F32)] * 7,
        grid=(t // tm,),
        in_specs=[cur(w, o) for w, o in segs] + [prev(w, o) for w, o in segs] + [full(p) for p in params] + [row],
        out_specs=[row] * 7,
        compiler_params=_cparams(("parallel",)),
        name="rwkv_pre",
    )(*([proj] * 14), *params, v_first)
    return outs


RW_CHUNK = 128


def _wkv_kernel(r_ref, lw_ref, k_ref, v_ref, an_ref, b_ref, g_ref, lng_ref, lnb_ref, rk_ref,
                gsum_ref, tril_ref, o_ref, s_ref):
    c = pl.program_id(0)
    C = RW_CHUNK
    N = RW_N

    @pl.when(c == 0)
    def _():
        s_ref[...] = jnp.zeros_like(s_ref)

    r = r_ref[...]
    lw = lw_ref[...]
    k = k_ref[...]
    v = v_ref[...]
    lw_hi = lw.astype(BF16)
    rem = lw - lw_hi.astype(F32)
    lw_mid = rem.astype(BF16)
    lw_lo = (rem - lw_mid.astype(F32)).astype(BF16)
    tril = tril_ref[...]
    lg = _dot(tril, lw_hi) + (_dot(tril, lw_mid) + _dot(tril, lw_lo))
    gam = jnp.exp(lg)
    ginv = jnp.exp(-lg)
    at = (an_ref[...] * jnp.exp(lg - lw)).astype(BF16)
    bt = (b_ref[...] * ginv).astype(BF16)
    kt = (k * ginv).astype(BF16)
    rt_f = r * gam
    rt = rt_f.astype(BF16)
    vb = v.astype(BF16)
    g_last = gam[C - 1:C, :]

    rowi = lax.broadcasted_iota(jnp.int32, (C, C), 0)
    coli = lax.broadcasted_iota(jnp.int32, (C, C), 1)
    strict = rowi > coli
    incl = rowi >= coli

    heads = range(RW_HEADS)
    hs = [slice(h * N, (h + 1) * N) for h in heads]
    bth = [bt[:, s] for s in hs]
    kth = [kt[:, s] for s in hs]
    vh = [vb[:, s] for s in hs]
    big = [_dot_t(jnp.concatenate([at[:, s], rt[:, s]], axis=0), jnp.concatenate([bth[h], kth[h]], axis=0))
           for h, s in zip(heads, hs)]
    a_ab = [jnp.where(strict, big[h][:C, :C], 0.0) for h in heads]
    a_ak = [jnp.where(strict, big[h][:C, C:], 0.0).astype(BF16) for h in heads]
    a_rb = [jnp.where(incl, big[h][C:, :C], 0.0).astype(BF16) for h in heads]
    a_rk = [jnp.where(incl, big[h][C:, C:], 0.0).astype(BF16) for h in heads]
    akv = [_dot(a_ak[h], vh[h]) for h in heads]
    def same_block(size):
        shift = size.bit_length() - 1
        return (rowi >> shift) == (coli >> shift)

    tinv = [jnp.where(same_block(2), a_ab[h], 0.0) + jnp.where(rowi == coli, 1.0, 0.0) for h in heads]
    size = 2
    while size < C:
        lower_left = same_block(2 * size) & jnp.logical_not(same_block(size))
        off = [jnp.where(lower_left, a_ab[h], 0.0).astype(BF16) for h in heads]
        tb = [tinv[h].astype(BF16) for h in heads]
        half = [_dot(tb[h], off[h]).astype(BF16) for h in heads]
        tinv = [tinv[h] + _dot(half[h], tb[h]) for h in heads]
        size *= 2
    xb = [_dot(tinv[h].astype(BF16), jnp.concatenate([at[:, hs[h]], akv[h].astype(BF16)], axis=1)).astype(BF16)
          for h in heads]
    yx = [_dot(a_rb[h], xb[h]) for h in heads]
    ykv = [_dot(a_rk[h], vh[h]) for h in heads]
    xtb = [_dot_tl(xb[h], bth[h]) for h in heads]
    vtk = [_dot_tl(vh[h], kth[h]) for h in heads]
    ys = []
    for h in heads:
        gl = g_last[:, hs[h]]
        y1 = (rt_f[:, hs[h]] + yx[h][:, :N]).astype(BF16)
        s0 = s_ref[h]
        s0b = s0.astype(BF16)
        ys.append(_dot_t(y1, s0b) + (yx[h][:, N:] + ykv[h]))
        s_ref[h] = (s0 + _dot(s0b, xtb[h][:N].astype(BF16)) + (xtb[h][N:] + vtk[h])) * gl
    y = jnp.concatenate(ys, axis=1)

    gsum = gsum_ref[...]
    inv_n = 1.0 / N
    mu = _split_dot(y, gsum, 3) * inv_n
    yc = y - mu
    var = _split_dot(yc * yc, gsum, 3) * inv_n
    yn = yc * lax.rsqrt(var + RW_GN_EPS) * lng_ref[...] + lnb_ref[...]
    bonus = _split_dot(r * k * rk_ref[...], gsum, 3)
    o_ref[...] = ((yn + bonus * v) * g_ref[...]).astype(o_ref.dtype)


def _wkv(r, lw, k, v, an, b, g, ln_g, ln_b, r_k, gsum):
    t = r.shape[0]
    C = RW_CHUNK
    tril = jnp.asarray(np.tril(np.ones((C, C), np.float32)), BF16)
    row = pl.BlockSpec((C, RW_W), lambda c: (c, 0))
    vec = pl.BlockSpec((1, RW_W), lambda c: (0, 0))
    return pl.pallas_call(
        _wkv_kernel,
        out_shape=jax.ShapeDtypeStruct((t, RW_W), BF16),
        grid=(t // C,),
        in_specs=[row] * 7 + [vec] * 3 + [pl.BlockSpec((RW_W, RW_W), lambda c: (0, 0)),
                                          pl.BlockSpec((C, C), lambda c: (0, 0))],
        out_specs=row,
        scratch_shapes=[pltpu.VMEM((RW_HEADS, RW_N, RW_N), F32)],
        compiler_params=_cparams(("arbitrary",)),
        name="wkv7",
    )(r, lw, k, v, an, b, g, ln_g, ln_b, r_k, gsum, tril)


NSA_PREP_TM = 512


N_NORM_SLABS = (NSA_W + 2 * NSA_KV_W) // LANES
N_RAW_SLABS = 4 * NSA_KV_W // LANES
N_NSA_SLABS = N_NORM_SLABS + N_RAW_SLABS + 1


def _nsa_prep_kernel(*refs):
    x_refs = refs[:N_NSA_SLABS]
    (cos_ref, sin_ref, w_ref, gmean_ref, big_ref,
     qk_ref, hmat_ref, v1_ref, gate_ref, qpad_ref, ksx_ref) = refs[N_NSA_SLABS:]
    cos = cos_ref[...]
    sin = sin_ref[...]
    gmean = gmean_ref[...]
    half = NSA_DH // 2
    tm = cos.shape[0]
    lane = lax.broadcasted_iota(jnp.int32, cos.shape, 1)
    first_half = (lane % NSA_DH) < half
    low = lane < NSA_DH
    blk_row = (pl.program_id(0) * tm + lax.broadcasted_iota(jnp.int32, cos.shape, 0)) // SLC_BLOCK
    big = big_ref[...]
    mark_lo = jnp.where(blk_row == lane - NSA_DH, big, 0.0)
    mark_hi = jnp.where(low & (blk_row == lane + NSA_DH), big, 0.0).astype(BF16)
    n_q = NSA_W // LANES
    n_k = NSA_KV_W // LANES
    for s in range(N_NORM_SLABS):
        x = x_refs[s][...]
        y = x * lax.rsqrt(_split_dot(x * x, gmean, 3) + RMS_EPS) * w_ref[s]
        rot = jnp.where(first_half, pltpu.roll(y, LANES - half, axis=1), pltpu.roll(y, half, axis=1))
        res32 = y * cos + rot * sin
        res = res32.astype(qk_ref.dtype)
        qk_ref[2 * s] = res[:, :NSA_DH]
        qk_ref[2 * s + 1] = res[:, NSA_DH:]
        if s >= n_q + n_k:
            continue
        swapped = pltpu.roll(res32, NSA_DH, axis=1)
        if s < n_q:
            qpad_ref[2 * s] = jnp.where(low, res32, 0.0).astype(BF16)
            qpad_ref[2 * s + 1] = jnp.where(low, swapped, 0.0).astype(BF16)
        else:
            g0 = 2 * (s - n_q)
            ksx_ref[g0, :, :LANES] = jnp.where(low, res32, mark_lo).astype(BF16)
            ksx_ref[g0 + 1, :, :LANES] = jnp.where(low, swapped, mark_lo).astype(BF16)
            ksx_ref[g0, :, LANES:] = mark_hi
            ksx_ref[g0 + 1, :, LANES:] = mark_hi

    rows = hmat_ref.shape[2]
    low_r = lax.broadcasted_iota(jnp.int32, (rows, LANES), 1) < NSA_DH
    for s in range(NSA_KV_W // LANES * 2):
        kind, pair = divmod(s, NSA_KV_W // LANES)
        x_ref = x_refs[N_NORM_SLABS + s]
        for t2 in range(CMP_STRIDE // 2):
            r0 = x_ref[pl.ds(2 * t2, rows, stride=CMP_STRIDE), :]
            r1 = x_ref[pl.ds(2 * t2 + 1, rows, stride=CMP_STRIDE), :]
            cols = slice(t2 * LANES, (t2 + 1) * LANES)
            hmat_ref[kind, 2 * pair, :, cols] = jnp.where(low_r, r0, pltpu.roll(r1, NSA_DH, axis=1)).astype(BF16)
            hmat_ref[kind, 2 * pair + 1, :, cols] = jnp.where(low_r, pltpu.roll(r0, NSA_DH, axis=1), r1).astype(BF16)

    one_col = jnp.where(lane == NSA_DH, 1.0, 0.0)
    for s in range(NSA_KV_W // LANES * 2):
        x = x_refs[N_NORM_SLABS + N_RAW_SLABS // 2 + s][...]
        v1_ref[2 * s] = jnp.where(low, x, one_col).astype(BF16)
        v1_ref[2 * s + 1] = jnp.where(low, pltpu.roll(x, NSA_DH, axis=1), one_col).astype(BF16)

    xg = x_refs[N_NSA_SLABS - 1][...]
    per_group = 3 * NSA_HG
    for g in range(NSA_G):
        shifted = xg if g == 0 else pltpu.roll(xg, LANES - per_group * g, axis=1)
        gate_ref[g] = jnp.where(lane < per_group, shifted, 0.0)


def _nsa_prep(proj, w_slabs, cos, sin, gmean, big):
    t = proj.shape[0]
    assert t // SLC_BLOCK <= LANES
    tm = NSA_PREP_TM
    base = C_NSA_Q // LANES
    tab = pl.BlockSpec((tm, LANES), lambda i: (i, 0))
    n_heads_out = 2 * N_NORM_SLABS
    return pl.pallas_call(
        _nsa_prep_kernel,
        out_shape=[jax.ShapeDtypeStruct((n_heads_out, t, NSA_DH), BF16),
                   jax.ShapeDtypeStruct((2, NSA_G, t // CMP_STRIDE, CMP_STRIDE * NSA_DH), BF16),
                   jax.ShapeDtypeStruct((2 * NSA_G, t, LANES), BF16),
                   jax.ShapeDtypeStruct((NSA_G, t, LANES), F32),
                   jax.ShapeDtypeStruct((NSA_HEADS, t, LANES), BF16),
                   jax.ShapeDtypeStruct((NSA_G, t, 2 * LANES), BF16)],
        grid=(t // tm,),
        in_specs=[pl.BlockSpec((tm, LANES), lambda i, s=s: (i, base + s)) for s in range(N_NSA_SLABS)]
        + [tab, tab, pl.BlockSpec((N_NORM_SLABS, 1, LANES), lambda i: (0, 0, 0)),
           pl.BlockSpec((LANES, LANES), lambda i: (0, 0)), pl.BlockSpec((1, LANES), lambda i: (0, 0))],
        out_specs=[pl.BlockSpec((n_heads_out, tm, NSA_DH), lambda i: (0, i, 0)),
                   pl.BlockSpec((2, NSA_G, tm // CMP_STRIDE, CMP_STRIDE * NSA_DH), lambda i: (0, 0, i, 0)),
                   pl.BlockSpec((2 * NSA_G, tm, LANES), lambda i: (0, i, 0)),
                   pl.BlockSpec((NSA_G, tm, LANES), lambda i: (0, i, 0)),
                   pl.BlockSpec((NSA_HEADS, tm, LANES), lambda i: (0, i, 0)),
                   pl.BlockSpec((NSA_G, tm, 2 * LANES), lambda i: (0, i, 0))],
        compiler_params=_cparams(("parallel",)),
        name="nsa_prep",
    )(*([proj] * N_NSA_SLABS), cos, sin, w_slabs, gmean, big)


def _compress_kernel(h_ref, w1_ref, w2_ref, pe_ref, nw_ref, cos_ref, sin_ref, o_ref):
    kind = pl.program_id(0)
    nc = h_ref.shape[2]
    half_in = CMP_STRIDE * NSA_DH
    hm = h_ref[0, 0]
    w1 = w1_ref[0]
    first = _dot(hm, w1[:half_in])
    second = _dot(hm, w1[half_in:])
    const = _dot(pe_ref[0], w1)[0:1, :]
    pre = first + pltpu.roll(second, nc - 1, axis=0) + const
    hid = 0.5 * pre * (1.0 + jnp.tanh(0.7978845608028654 * (pre + 0.044715 * pre * pre * pre)))
    out = _dot(hid.astype(BF16), w2_ref[0])
    y = out * lax.rsqrt(jnp.mean(out * out, axis=-1, keepdims=True) + RMS_EPS) * nw_ref[...]
    half = NSA_DH // 2
    rot = jnp.concatenate([y[:, half:], y[:, :half]], axis=1)
    roped = y * cos_ref[...] + rot * sin_ref[...]
    o_ref[0, 0] = jnp.where(kind == 0, roped, out).astype(o_ref.dtype)


def _compress(hmat, w1, w2, pe, nw, cos_c, sin_c):
    _, g, nc, width = hmat.shape
    return pl.pallas_call(
        _compress_kernel,
        out_shape=jax.ShapeDtypeStruct((2, g, nc, NSA_DH), BF16),
        grid=(2, g),
        in_specs=[pl.BlockSpec((1, 1, nc, width), lambda a, b: (a, b, 0, 0)),
                  pl.BlockSpec((1, 2 * width, CMP_HIDDEN), lambda a, b: (a, 0, 0)),
                  pl.BlockSpec((1, CMP_HIDDEN, NSA_DH), lambda a, b: (a, 0, 0)),
                  pl.BlockSpec((1, 8, 2 * width), lambda a, b: (a, 0, 0)),
                  pl.BlockSpec((1, NSA_DH), lambda a, b: (0, 0)),
                  pl.BlockSpec((nc, NSA_DH), lambda a, b: (0, 0)),
                  pl.BlockSpec((nc, NSA_DH), lambda a, b: (0, 0))],
        out_specs=pl.BlockSpec((1, 1, nc, NSA_DH), lambda a, b: (a, b, 0, 0)),
        compiler_params=_cparams(("parallel", "parallel")),
        name="nsa_compress",
    )(hmat, w1, w2, pe, nw, cos_c, sin_c)


SLC_KT = 1024
WIN_KEYS = WINDOW + NSA_TQ


INT32_MIN = -2 ** 31


def _nsa_attn_kernel(q_ref, qpad_ref, gate_ref, kc_ref, vc_ref, ks_ref, vs_ref, kw_ref, vw_ref, cis_ref, ltri_ref,
                     o_ref):
    qi = pl.program_id(1)
    nc = kc_ref.shape[2]
    nb = cis_ref.shape[1]
    heads = range(NSA_HG)
    start = qi * NSA_TQ
    qh = [q_ref[hh] for hh in heads]
    tok = start + lax.broadcasted_iota(jnp.int32, (NSA_TQ, 1), 0)

    def softmax_terms(scores):
        return [jnp.exp(s - jnp.max(s, axis=-1, keepdims=True)) for s in scores]

    kc = kc_ref[0, 0]
    vc = vc_ref[0, 0]
    cend = lax.broadcasted_iota(jnp.int32, (1, nc), 1) * CMP_STRIDE + (CMP_BLOCK - 1)
    bias_c = jnp.where(cend <= tok, 0.0, NEG_INF)
    ec = softmax_terms([_dot_t(qh[hh], kc) + bias_c for hh in heads])
    sees_any = tok >= CMP_BLOCK - 1
    pc = [ec[hh] * jnp.where(sees_any, 1.0 / jnp.sum(ec[hh], axis=-1, keepdims=True), 0.0) for hh in heads]
    o_c = [_dot(pc[hh].astype(BF16), vc) for hh in heads]

    psum = pc[0]
    for hh in range(1, NSA_HG):
        psum = psum + pc[hh]
    score = _split_dot(psum, cis_ref[...], 3)
    blk = lax.broadcasted_iota(jnp.int32, (1, nb), 1)
    cur = tok // SLC_BLOCK
    forced = (blk == 0) | (blk == cur) | (blk == cur - 1)
    score = jnp.where(forced, FORCED_SCORE, score)
    score = jnp.where(blk <= cur, score, -jnp.inf)

    kbase = pl.multiple_of(jnp.maximum(start - WINDOW, 0), NSA_TQ)
    kw = kw_ref[0, pl.ds(kbase, WIN_KEYS), :]
    lag = tok - (kbase + lax.broadcasted_iota(jnp.int32, (1, WIN_KEYS), 1))
    bias_w = jnp.where((lag >= 0) & (lag < WINDOW), 0.0, NEG_INF)
    sw = [_dot_t(qh[hh], kw) for hh in heads]

    bits = lax.bitcast_convert_type(score.T, jnp.int32)
    key = bits ^ ((bits >> 31) & 0x7FFFFFFF)
    n_sel = min(SLC_TOPK, nb)

    def enough(c):
        return jnp.sum(jnp.where(key >= c, 1.0, 0.0), axis=0, keepdims=True) >= n_sel

    def enough3(c1, c2, c3):
        packed = jnp.where(key >= c3, 65793.0, jnp.where(key >= c2, 257.0, jnp.where(key >= c1, 1.0, 0.0)))
        tot = jnp.sum(packed, axis=0, keepdims=True).astype(jnp.int32)
        return (tot & 255) >= n_sel, ((tot >> 8) & 255) >= n_sel, (tot >> 16) >= n_sel

    zero_row = jnp.zeros((1, NSA_TQ), jnp.int32)
    thr = jnp.where(enough(zero_row), zero_row, INT32_MIN)
    for hi in range(30, 0, -2):
        c1 = thr + (1 << (hi - 1))
        c2 = thr + (1 << hi)
        c3 = c2 + (1 << (hi - 1))
        e1, e2, e3 = enough3(c1, c2, c3)
        thr = jnp.where(e3, c3, jnp.where(e2, c2, jnp.where(e1, c1, thr)))
    c1 = thr + 1
    thr = jnp.where(enough(c1), c1, thr)
    above = key > thr
    tied = key == thr
    need = n_sel - jnp.sum(jnp.where(above, 1.0, 0.0), axis=0, keepdims=True)
    tied_before = _dot(ltri_ref[...], jnp.where(tied, 1.0, 0.0).astype(BF16))
    sel_t = above | (tied & (tied_before < need))
    sel = jnp.where(sel_t, 1.0, 0.0).T
    if nb < LANES:
        sel = jnp.concatenate([sel, jnp.zeros((NSA_TQ, LANES - nb), F32)], axis=1)
    lane_q = lax.broadcasted_iota(jnp.int32, (NSA_TQ, LANES), 1)
    sel_sw = pltpu.roll(sel, NSA_DH, axis=1)
    sel_lo = jnp.where(lane_q >= NSA_DH, sel_sw, 0.0)
    sel_hi = jnp.where(lane_q < NSA_DH, sel_sw, 0.0)
    qx = [jnp.concatenate([qpad_ref[hh].astype(F32) + sel_lo, sel_hi], axis=1).astype(BF16) for hh in heads]

    sw = [sw[hh] + bias_w for hh in heads]
    ew = [jnp.exp((s - jnp.max(s, axis=-1, keepdims=True)).astype(BF16)) for s in sw]
    acc_w = [_dot(ew[hh], vw_ref[0, pl.ds(kbase, WIN_KEYS), :]) for hh in heads]

    n_tiles = (start + NSA_TQ + SLC_KT - 1) // SLC_KT

    def sel_tile(j, carry, causal):
        ms, accs = carry
        k0 = pl.multiple_of(j * SLC_KT, SLC_KT)
        kt = ks_ref[0, pl.ds(k0, SLC_KT), :]
        vt = vs_ref[0, pl.ds(k0, SLC_KT), :]
        sj = [_dot_t(qx[hh], kt) for hh in heads]
        if causal:
            future = (k0 + lax.broadcasted_iota(jnp.int32, (1, SLC_KT), 1)) > tok
            sj = [jnp.where(future, NEG_INF, s) for s in sj]
        m_new = [jnp.maximum(ms[hh], jnp.max(sj[hh], axis=-1, keepdims=True)) for hh in heads]
        pj = [jnp.exp((sj[hh] - m_new[hh]).astype(BF16)) for hh in heads]
        accs = [jnp.exp(ms[hh] - m_new[hh]) * accs[hh] + _dot(pj[hh], vt) for hh in heads]
        return tuple(m_new), tuple(accs)

    m0 = tuple(jnp.full((NSA_TQ, 1), NEG_INF, F32) for _ in heads)
    a0 = tuple(jnp.zeros((NSA_TQ, LANES), F32) for _ in heads)
    carry = lax.fori_loop(0, n_tiles - 1, functools.partial(sel_tile, causal=False), (m0, a0))
    _, acc_s = sel_tile(n_tiles - 1, carry, causal=True)

    gt = jax.nn.sigmoid(gate_ref[0])
    outs = []
    for hh in heads:
        g0, g1, g2 = (gt[:, 3 * hh + br:3 * hh + br + 1] for br in range(3))
        scale_s = g1 / acc_s[hh][:, NSA_DH:NSA_DH + 1]
        scale_w = g2 / acc_w[hh][:, NSA_DH:NSA_DH + 1]
        outs.append(g0 * o_c[hh] + scale_s * acc_s[hh][:, :NSA_DH] + scale_w * acc_w[hh][:, :NSA_DH])
    o_ref[...] = jnp.concatenate(outs, axis=1).astype(o_ref.dtype)


def _nsa_attn(qk_hm, qpad, ksx, gates, cmp_kv, v1, cis, ltri):
    t = qk_hm.shape[1]
    nq = t // NSA_TQ
    nc = cmp_kv.shape[2]
    nb = cis.shape[1]
    ks_spec = pl.BlockSpec((1, t, 2 * LANES), lambda g, i: (g, 0, 0))
    kw_spec = pl.BlockSpec((1, t, NSA_DH), lambda g, i: (NSA_HEADS + NSA_G + g, 0, 0))
    vs_spec = pl.BlockSpec((1, t, LANES), lambda g, i: (g, 0, 0))
    vw_spec = pl.BlockSpec((1, t, LANES), lambda g, i: (NSA_G + g, 0, 0))
    return pl.pallas_call(
        _nsa_attn_kernel,
        out_shape=jax.ShapeDtypeStruct((t, NSA_W), BF16),
        grid=(NSA_G, nq),
        in_specs=[pl.BlockSpec((NSA_HG, NSA_TQ, NSA_DH), lambda g, i: (g, i, 0)),
                  pl.BlockSpec((NSA_HG, NSA_TQ, LANES), lambda g, i: (g, i, 0)),
                  pl.BlockSpec((1, NSA_TQ, LANES), lambda g, i: (g, i, 0)),
                  pl.BlockSpec((1, 1, nc, NSA_DH), lambda g, i: (0, g, 0, 0)),
                  pl.BlockSpec((1, 1, nc, NSA_DH), lambda g, i: (1, g, 0, 0)),
                  ks_spec, vs_spec, kw_spec, vw_spec,
                  pl.BlockSpec((nc, nb), lambda g, i: (0, 0)),
                  pl.BlockSpec((nb, nb), lambda g, i: (0, 0))],
        out_specs=pl.BlockSpec((NSA_TQ, NSA_HG * NSA_DH), lambda g, i: (i, g)),
        compiler_params=_cparams(("parallel", "arbitrary")),
        name="nsa_attention",
    )(qk_hm, qpad, gates, cmp_kv, cmp_kv, ksx, v1, qk_hm, v1, cis, ltri)


def _rope_tables(pos, dh, reps):
    half = dh // 2
    inv_freq = ROPE_THETA ** (-np.arange(half, dtype=np.float64) / half)
    ang = np.asarray(pos, np.float64)[:, None] * inv_freq[None, :]
    cos = np.cos(ang)
    sin = np.sin(ang)
    cos_t = np.tile(np.concatenate([cos, cos], axis=1), (1, reps))
    sin_t = np.tile(np.concatenate([-sin, sin], axis=1), (1, reps))
    return jnp.asarray(cos_t, F32), jnp.asarray(sin_t, F32)


def _pad_cols(a, width):
    return jnp.pad(a, ((0, 0), (0, width - a.shape[1])))


def _pad_rows(a, height):
    return jnp.pad(a, ((0, height - a.shape[0]), (0, 0)))


def _pack_w_in_kernel(w_ref, xv_ref, o_ref):
    o_ref[...] = _pack_w_in(w_ref[...], xv_ref[...]).astype(o_ref.dtype)


def _pack_w_in_call(w_in, layer, xv):
    _, d, n_in = w_in.shape
    tk = 256
    return pl.pallas_call(
        _pack_w_in_kernel,
        out_shape=jax.ShapeDtypeStruct((d, PROJ_W), BF16),
        grid=(d // tk,),
        in_specs=[pl.BlockSpec((None, tk, n_in), lambda i: (layer, i, 0)),
                  pl.BlockSpec((tk, LANES), lambda i: (i, 0))],
        out_specs=pl.BlockSpec((tk, PROJ_W), lambda i: (i, 0)),
        compiler_params=_cparams(("parallel",)),
        name="pack_w_in",
    )(w_in, xv)


def _pack_w_in(w_l, xv):
    d = w_l.shape[0]
    o = 0
    ret = w_l[:, o:o + 4 * RET_W]; o += 4 * RET_W
    rkv = w_l[:, o:o + 3 * RW_W]; o += 3 * RW_W
    xw = w_l[:, o:o + RW_DECAY_RANK]; o += RW_DECAY_RANK
    xa = w_l[:, o:o + RW_A_RANK]; o += RW_A_RANK
    xg = w_l[:, o:o + RW_GATE_RANK]; o += RW_GATE_RANK
    q = w_l[:, o:o + NSA_W]; o += NSA_W
    kc, vc, ks, vs, kw, vw = (w_l[:, o + i * NSA_KV_W:o + (i + 1) * NSA_KV_W] for i in range(6))
    o += 6 * NSA_KV_W
    gates = w_l[:, o:o + 3 * NSA_HEADS]
    return jnp.concatenate([ret, rkv, _pad_cols(xw, LANES), _pad_cols(xa, LANES), xg, xv,
                            q, ks, kw, kc, vc, vs, vw, _pad_cols(gates, LANES)], axis=1)


def _pack_mix(mix, vres_mix):
    o = 3 * RW_W
    xw = mix[o:o + RW_DECAY_RANK]; o += RW_DECAY_RANK
    xa = mix[o:o + RW_A_RANK]; o += RW_A_RANK
    xg = mix[o:o + RW_GATE_RANK]
    z = lambda n: jnp.zeros((n,), mix.dtype)
    xv = z(LANES) if vres_mix is None else jnp.concatenate([vres_mix, z(LANES - RW_V_RANK)])
    return jnp.concatenate([mix[:3 * RW_W], xw, z(LANES - RW_DECAY_RANK), xa, z(LANES - RW_A_RANK), xg, xv])[None, :]


def kernel(x, ln1_g, w_in, w_in_vres, rwkv_mix, rwkv_vres_mix, rwkv_w0, rwkv_w2, rwkv_a0, rwkv_a2, rwkv_v0, rwkv_v2, rwkv_g2, rwkv_k_k, rwkv_k_a, rwkv_r_k, rwkv_ln_g, rwkv_ln_b, nsa_q_norm, nsa_k_norm, nsa_cmp_pe, nsa_cmp_k_w1, nsa_cmp_k_w2, nsa_cmp_v_w1, nsa_cmp_v_w2, w_out, ln2_g, w_up, w_down):
    bsz, t, d = x.shape
    assert bsz == 1
    depth = w_in.shape[0]
    nc = t // CMP_STRIDE
    nb = t // SLC_BLOCK
    xs = x.reshape(t, d)

    pos = np.arange(t)
    cos_r, sin_r = _rope_tables(pos, RET_DH, 1)
    cos_n, sin_n = _rope_tables(pos, NSA_DH, 2)
    cos_c, sin_c = _rope_tables(np.arange(nc) * CMP_STRIDE + (CMP_BLOCK - 1), NSA_DH, 1)
    lane_head = np.arange(RW_W) // RW_N
    gsum = jnp.asarray(lane_head[:, None] == lane_head[None, :], BF16)
    lane_h2 = np.arange(LANES) // NSA_DH
    gmean = jnp.asarray((lane_h2[:, None] == lane_h2[None, :]) / float(NSA_DH), BF16)
    cstart = np.arange(nc) * CMP_STRIDE
    sstart = np.arange(nb) * SLC_BLOCK
    cis = jnp.asarray((cstart[:, None] <= sstart[None, :] + SLC_BLOCK - 1)
                      & (cstart[:, None] + CMP_BLOCK - 1 >= sstart[None, :]), BF16)

    ltri = jnp.asarray(np.tril(np.ones((nb, nb), np.float32), -1), BF16)

    v_first = jnp.zeros((t, RW_W), F32)
    for l in range(depth):
        vres = l > 0
        xv_w = _pad_cols(w_in_vres[l - 1], LANES) if vres else jnp.zeros((d, LANES), F32)
        w_cat = _pack_w_in_call(w_in, l, xv_w)
        proj = _norm_matmul(xs, ln1_g[l], w_cat, name="in_proj")

        o_ret = _retention(proj, cos_r, sin_r)

        row = lambda a: a.reshape(1, -1)
        mix = _pack_mix(rwkv_mix[l], rwkv_vres_mix[l - 1] if vres else None)
        v0 = row(rwkv_v0[l - 1]) if vres else jnp.zeros((1, RW_W), F32)
        v2 = _pad_rows(rwkv_v2[l - 1], LANES) if vres else jnp.zeros((LANES, RW_W), F32)
        r_, lw_, k_, v_, an_, b_, g_ = _rwkv_pre(
            proj, mix, row(rwkv_w0[l]), _pad_rows(rwkv_w2[l], LANES), row(rwkv_a0[l]), _pad_rows(rwkv_a2[l], LANES),
            v0, v2, rwkv_g2[l], row(rwkv_k_k[l]), row(rwkv_k_a[l]), gsum, v_first, vres)
        if not vres:
            v_first = v_
        o_rwkv = _wkv(r_, lw_, k_, v_, an_, b_, g_, row(rwkv_ln_g[l]), row(rwkv_ln_b[l]), row(rwkv_r_k[l]), gsum)

        n_q = NSA_W // LANES
        n_k = NSA_KV_W // LANES
        w_slabs = jnp.stack([jnp.tile(nsa_q_norm[l] * (NSA_DH ** -0.5), 2)] * n_q
                            + [jnp.tile(nsa_k_norm[l, 1], 2)] * n_k + [jnp.tile(nsa_k_norm[l, 2], 2)] * n_k)[:, None, :]
        s_max = 1.02 * jnp.max(jnp.abs(nsa_q_norm[l])) * jnp.max(jnp.abs(nsa_k_norm[l, 1])) * (NSA_DH ** 0.5)
        big = jnp.full((1, LANES), jnp.exp2(jnp.ceil(jnp.log2(2.0 * s_max + 128.0))), F32)
        qk_hm, hmat, v1, gates, qpad, ksx = _nsa_prep(proj, w_slabs, cos_n, sin_n, gmean, big)
        w1 = jnp.stack([nsa_cmp_k_w1[l], nsa_cmp_v_w1[l]]).astype(BF16)
        w2 = jnp.stack([nsa_cmp_k_w2[l], nsa_cmp_v_w2[l]]).astype(BF16)
        pe = jnp.broadcast_to(nsa_cmp_pe[l].reshape(2, 1, CMP_BLOCK * NSA_DH), (2, 8, CMP_BLOCK * NSA_DH)).astype(BF16)
        cmp_kv = _compress(hmat, w1, w2, pe, row(nsa_k_norm[l, 0]), cos_c, sin_c)
        o_nsa = _nsa_attn(qk_hm, qpad, ksx, gates, cmp_kv, v1, cis, ltri)

        xs = _out_proj(o_ret, o_rwkv, o_nsa, w_out[l], xs)

        up = _norm_matmul(xs, ln2_g[l], w_up[l], act="relu2", out_dtype=BF16, name="mlp_up")
        xs = _matmul(up, w_down[l], res=xs, name="mlp_down")
    return xs.reshape(bsz, t, d)
```

```python
import functools

import numpy as np
import jax
import jax.numpy as jnp
from jax import lax
from jax.experimental import pallas as pl
from jax.experimental.pallas import tpu as pltpu

F32 = jnp.float32
BF16 = jnp.bfloat16

D_MODEL = 2048
RET_HEADS, RET_DH = 4, 128
RET_W = RET_HEADS * RET_DH
RW_HEADS, RW_N = 8, 64
RW_W = RW_HEADS * RW_N
RW_DECAY_RANK, RW_A_RANK, RW_V_RANK, RW_GATE_RANK = 96, 96, 64, 256
RW_GN_EPS = 64e-5
NSA_HEADS, NSA_G, NSA_DH = 16, 4, 64
NSA_HG = NSA_HEADS // NSA_G
NSA_W = NSA_HEADS * NSA_DH
NSA_KV_W = NSA_G * NSA_DH
CMP_BLOCK, CMP_STRIDE, CMP_HIDDEN = 32, 16, 256
SLC_BLOCK, SLC_TOPK, WINDOW = 64, 16, 512
NSA_TQ = 256
ROPE_THETA = 10000.0
MLP_HIDDEN = 4 * D_MODEL
RMS_EPS = 1e-6
NEG_INF = -1e30
FORCED_SCORE = 1e9

LANES = 128
VMEM_LIMIT = 56 * 1024 * 1024

C_RET = 0
C_RW_RKV = 2048
C_RW_XW = 3584
C_RW_XA = 3712
C_RW_XG = 3840
C_RW_XV = 4096
C_NSA_Q = 4224
C_NSA_KS = 5248
C_NSA_KW = 5504
C_NSA_KC = 5760
C_NSA_VC = 6016
C_NSA_VS = 6272
C_NSA_VW = 6528
C_NSA_GATE = 6784
PROJ_W = 6912


def _cparams(sem):
    return pltpu.CompilerParams(dimension_semantics=sem, vmem_limit_bytes=VMEM_LIMIT)


def _dot(a, b):
    return jnp.dot(a, b, preferred_element_type=F32)


def _dot_t(a, b):
    return lax.dot_general(a, b, (((1,), (1,)), ((), ())), preferred_element_type=F32)


def _dot_tl(a, b):
    return lax.dot_general(a, b, (((0,), (0,)), ((), ())), preferred_element_type=F32)


def _split_dot(x, m_bf16, passes):
    hi = x.astype(BF16)
    acc = _dot(hi, m_bf16)
    rem = x - hi.astype(F32)
    for _ in range(passes - 1):
        piece = rem.astype(BF16)
        acc = acc + _dot(piece, m_bf16)
        rem = rem - piece.astype(F32)
    return acc


def _mm_kernel(*refs, nk, act, has_res):
    a_ref, b_ref = refs[:2]
    r_ref = refs[2] if has_res else None
    o_ref = refs[3] if has_res else refs[2]

    def finish(y):
        if act == "relu2":
            y = jnp.square(jnp.maximum(y, 0.0))
        if has_res:
            y = y + r_ref[...]
        o_ref[...] = y.astype(o_ref.dtype)

    part = _dot(a_ref[...], b_ref[...].astype(BF16))
    if nk == 1:
        finish(part)
        return
    acc_ref = refs[-1]
    k = pl.program_id(2)

    @pl.when(k == 0)
    def _():
        acc_ref[...] = part

    @pl.when(k > 0)
    def _():
        acc_ref[...] += part

    @pl.when(k == nk - 1)
    def _():
        finish(acc_ref[...])


def _pick(n, cands):
    for c in cands:
        if n % c == 0:
            return c
    raise ValueError(f"no tile for {n}")


def _weight_spec(b, layer, block, index_map):
    if layer is None:
        return pl.BlockSpec(block, index_map)
    return pl.BlockSpec((None,) + block, lambda *idx: (layer,) + index_map(*idx))


def _matmul(a, b, *, layer=None, res=None, act=None, out_dtype=F32, name="matmul"):
    m, kd = a.shape
    n = b.shape[-1]
    tm = _pick(m, (1024, 512, 256, 128))
    tn = _pick(n, (1024, 768, 512, 256, 128))
    tk = _pick(kd, (2048, 1024, 512))
    nk = kd // tk
    in_specs = [pl.BlockSpec((tm, tk), lambda i, j, k: (i, k)),
                _weight_spec(b, layer, (tk, tn), lambda i, j, k: (k, j))]
    args = [a, b]
    if res is not None:
        in_specs.append(pl.BlockSpec((tm, tn), lambda i, j, k: (i, j)))
        args.append(res)
    return pl.pallas_call(
        functools.partial(_mm_kernel, nk=nk, act=act, has_res=res is not None),
        out_shape=jax.ShapeDtypeStruct((m, n), out_dtype),
        grid=(m // tm, n // tn, nk),
        in_specs=in_specs,
        out_specs=pl.BlockSpec((tm, tn), lambda i, j, k: (i, j)),
        scratch_shapes=[pltpu.VMEM((tm, tn), F32)] if nk > 1 else [],
        compiler_params=_cparams(("parallel", "parallel", "arbitrary")),
        name=name,
    )(*args)


def _norm_mm_kernel(x_ref, g_ref, b_ref, o_ref, h_ref, *, act):
    @pl.when(pl.program_id(1) == 0)
    def _():
        x = x_ref[...]
        ms = jnp.mean(x * x, axis=-1, keepdims=True)
        h_ref[...] = (x * lax.rsqrt(ms + RMS_EPS) * g_ref[...]).astype(BF16)

    y = _dot(h_ref[...], b_ref[...].astype(BF16))
    if act == "relu2":
        y = jnp.square(jnp.maximum(y, 0.0))
    o_ref[...] = y.astype(o_ref.dtype)


def _norm_matmul(x, g, b, *, layer=None, act=None, out_dtype=F32, name="norm_matmul"):
    m, kd = x.shape
    n = b.shape[-1]
    tm = _pick(m, (1024, 512, 256, 128))
    tn = _pick(n, (1024, 768, 512, 256, 128))
    return pl.pallas_call(
        functools.partial(_norm_mm_kernel, act=act),
        out_shape=jax.ShapeDtypeStruct((m, n), out_dtype),
        grid=(m // tm, n // tn),
        in_specs=[pl.BlockSpec((tm, kd), lambda i, j: (i, 0)),
                  pl.BlockSpec((1, kd), lambda i, j: (0, 0)),
                  _weight_spec(b, layer, (kd, tn), lambda i, j: (0, j))],
        out_specs=pl.BlockSpec((tm, tn), lambda i, j: (i, j)),
        scratch_shapes=[pltpu.VMEM((tm, kd), BF16)],
        compiler_params=_cparams(("parallel", "arbitrary")),
        name=name,
    )(x, g.reshape(1, kd), b)


def _out_proj_kernel(ret_ref, rw_ref, nsa_ref, w_ref, res_ref, o_ref):
    k0 = ret_ref.shape[1]
    k1 = k0 + rw_ref.shape[1]
    acc = _dot(ret_ref[...], w_ref[:k0, :].astype(BF16))
    acc = acc + _dot(rw_ref[...], w_ref[k0:k1, :].astype(BF16))
    acc = acc + _dot(nsa_ref[...], w_ref[k1:, :].astype(BF16))
    o_ref[...] = acc + res_ref[...]


def _out_proj(o_ret, o_rwkv, o_nsa, w, layer, res):
    m = o_ret.shape[0]
    kd, n = w.shape[-2:]
    assert kd == o_ret.shape[1] + o_rwkv.shape[1] + o_nsa.shape[1]
    tm = _pick(m, (1024, 512, 256, 128))
    tn = _pick(n, (1024, 512, 256, 128))

    def rows(a):
        return pl.BlockSpec((tm, a.shape[1]), lambda i, j: (i, 0))

    return pl.pallas_call(
        _out_proj_kernel,
        out_shape=jax.ShapeDtypeStruct((m, n), F32),
        grid=(m // tm, n // tn),
        in_specs=[rows(o_ret), rows(o_rwkv), rows(o_nsa),
                  _weight_spec(w, layer, (kd, tn), lambda i, j: (0, j)),
                  pl.BlockSpec((tm, tn), lambda i, j: (i, j))],
        out_specs=pl.BlockSpec((tm, tn), lambda i, j: (i, j)),
        compiler_params=_cparams(("parallel", "parallel")),
        name="out_proj",
    )(o_ret, o_rwkv, o_nsa, w, res)


RET_CHUNK = 256


def _retention_kernel(q_ref, k_ref, v_ref, g_ref, cos_ref, sin_ref, dmat_ref, qd_ref, kd_ref, cd_ref, o_ref, state_ref):
    @pl.when(pl.program_id(0) == 0)
    def _():
        state_ref[...] = jnp.zeros_like(state_ref)

    cos = cos_ref[...]
    sin = sin_ref[...]

    def rope(x):
        return x * cos + pltpu.roll(x, RET_DH // 2, axis=1) * sin

    heads = range(RET_HEADS)
    hs = [slice(h * RET_DH, (h + 1) * RET_DH) for h in heads]
    q_all, k_all, v_all, gate = q_ref[...], k_ref[...], v_ref[...], g_ref[...]
    qb = [rope(q_all[:, s]).astype(BF16) for s in hs]
    kf = [rope(k_all[:, s]) * (RET_DH ** -0.5) for s in hs]
    kb = [kf[h].astype(BF16) for h in heads]
    vb = [v_all[:, s].astype(BF16) for s in hs]
    inner = [(_dot_t(qb[h], kb[h]) * dmat_ref[h]).astype(BF16) for h in heads]
    cross = [_dot(qb[h], state_ref[h].astype(BF16)) * qd_ref[h] for h in heads]
    kv = [_dot_tl((kf[h] * kd_ref[h]).astype(BF16), vb[h]) for h in heads]
    outs = []
    for h in heads:
        out = _dot(inner[h], vb[h]) + cross[h]
        state_ref[h] = cd_ref[h] * state_ref[h] + kv[h]
        outs.append(out * lax.rsqrt(jnp.mean(out * out, axis=-1, keepdims=True) + RMS_EPS))
    o_ref[...] = (gate * jax.nn.sigmoid(gate) * jnp.concatenate(outs, axis=1)).astype(o_ref.dtype)


def _retention(proj, cos, sin):
    t = proj.shape[0]
    chunk = RET_CHUNK
    log_gamma = np.log(1.0 - 2.0 ** (-5.0 - np.arange(RET_HEADS, dtype=np.float64)))[:, None, None]
    n = np.arange(chunk, dtype=np.float64)
    lag = n[:, None] - n[None, :]
    dmat = np.where(lag >= 0, np.exp(np.maximum(lag, 0.0)[None] * log_gamma), 0.0)
    ones = np.ones((1, 1, RET_DH))
    qd = np.exp((n + 1.0)[None, :, None] * log_gamma) * ones
    kd = np.exp((chunk - 1.0 - n)[None, :, None] * log_gamma) * ones
    cd = np.exp(chunk * log_gamma) * ones
    tables = [jnp.asarray(a, F32) for a in (dmat, qd, kd, cd)]
    base = C_RET // RET_W

    def col(off):
        return pl.BlockSpec((chunk, RET_W), lambda c, off=off: (c, base + off))

    def full(a):
        return pl.BlockSpec(a.shape, lambda c: (0, 0, 0))

    tab = pl.BlockSpec((chunk, RET_DH), lambda c: (c, 0))
    return pl.pallas_call(
        _retention_kernel,
        out_shape=jax.ShapeDtypeStruct((t, RET_W), BF16),
        grid=(t // chunk,),
        in_specs=[col(0), col(1), col(2), col(3), tab, tab] + [full(a) for a in tables],
        out_specs=pl.BlockSpec((chunk, RET_W), lambda c: (c, 0)),
        scratch_shapes=[pltpu.VMEM((RET_HEADS, RET_DH, RET_DH), F32)],
        compiler_params=_cparams(("arbitrary",)),
        name="retention",
    )(proj, proj, proj, proj, cos, sin, *tables)


def _dot_f32(a, b):
    a_hi = a.astype(BF16)
    b_hi = b.astype(BF16)
    a_lo = (a - a_hi.astype(F32)).astype(BF16)
    b_lo = (b - b_hi.astype(F32)).astype(BF16)
    return _dot(a_hi, b_hi) + (_dot(a_hi, b_lo) + _dot(a_lo, b_hi))


RW_PRE_TM = 256
_MIX_OFF = (0, 512, 1024, 1536, 1664, 1792, 2048, 2176)


def _rwkv_pre_kernel(r_ref, k_ref, v_ref, xw_ref, xa_ref, xg_ref, xv_ref,
                     rp_ref, kp_ref, vp_ref, xwp_ref, xap_ref, xgp_ref, xvp_ref,
                     mix_ref, w0_ref, w2_ref, a0_ref, a2_ref, v0_ref, v2_ref, g2_ref, kk_ref, ka_ref,
                     gsum_ref, vfirst_ref,
                     r_out, lw_out, k_out, v_out, an_out, b_out, g_out, *, use_vres):
    i = pl.program_id(0)

    def mixed(cur_ref, prev_ref, seg):
        cur = cur_ref[...]
        mix = mix_ref[:, _MIX_OFF[seg]:_MIX_OFF[seg + 1]]
        prev_row = jnp.where(i > 0, prev_ref[7:8, :], 0.0)
        rid = lax.broadcasted_iota(jnp.int32, cur.shape, 0)
        shifted = jnp.where(rid == 0, prev_row, pltpu.roll(cur, 1, axis=0))
        return cur + mix * (shifted - cur)

    r = mixed(r_ref, rp_ref, 0)
    k = mixed(k_ref, kp_ref, 1)
    v = mixed(v_ref, vp_ref, 2)
    xw = mixed(xw_ref, xwp_ref, 3)
    xa = mixed(xa_ref, xap_ref, 4)
    xg = mixed(xg_ref, xgp_ref, 5)

    z = -(w0_ref[...] + _dot_f32(jnp.tanh(xw), w2_ref[...]))
    softplus = jnp.maximum(z, 0.0) + jnp.log(1.0 + jnp.exp(-jnp.abs(z)))
    w = -softplus - 0.5
    lw_out[...] = -jnp.exp(w)
    a = jax.nn.sigmoid(a0_ref[...] + _dot_f32(xa, a2_ref[...]))
    g_out[...] = _dot_f32(jax.nn.sigmoid(xg), g2_ref[...])
    if use_vres:
        xv = mixed(xv_ref, xvp_ref, 6)
        v = v + (vfirst_ref[...] - v) * jax.nn.sigmoid(v0_ref[...] + _dot_f32(xv, v2_ref[...]))
    kk = k * kk_ref[...]
    ss = _split_dot(kk * kk, gsum_ref[...], 3)
    kk = kk / jnp.maximum(jnp.sqrt(ss), 1e-12)
    r_out[...] = r
    k_out[...] = k * (1.0 + (a - 1.0) * ka_ref[...])
    v_out[...] = v
    an_out[...] = -kk
    b_out[...] = kk * a


def _rwkv_pre(proj, mix, w0, w2, a0, a2, v0, v2, g2, k_k, k_a, gsum, v_first, use_vres):
    t = proj.shape[0]
    tm = RW_PRE_TM

    def cur(width, off):
        return pl.BlockSpec((tm, width), lambda i: (i, off // width))

    def prev(width, off):
        return pl.BlockSpec((8, width), lambda i: (jnp.maximum(i * (tm // 8) - 1, 0), off // width))

    def full(a):
        return pl.BlockSpec(a.shape, lambda i: (0,) * a.ndim)

    segs = [(RW_W, C_RW_RKV), (RW_W, C_RW_RKV + RW_W), (RW_W, C_RW_RKV + 2 * RW_W),
            (128, C_RW_XW), (128, C_RW_XA), (256, C_RW_XG), (128, C_RW_XV)]
    params = [mix, w0, w2, a0, a2, v0, v2, g2, k_k, k_a, gsum]
    row = pl.BlockSpec((tm, RW_W), lambda i: (i, 0))
    outs = pl.pallas_call(
        functools.partial(_rwkv_pre_kernel, use_vres=use_vres),
        out_shape=[jax.ShapeDtypeStruct((t, RW_W), F32)] * 7,
        grid=(t // tm,),
        in_specs=[cur(w, o) for w, o in segs] + [prev(w, o) for w, o in segs] + [full(p) for p in params] + [row],
        out_specs=[row] * 7,
        compiler_params=_cparams(("parallel",)),
        name="rwkv_pre",
    )(*([proj] * 14), *params, v_first)
    return outs


RW_CHUNK = 128


def _wkv_kernel(r_ref, lw_ref, k_ref, v_ref, an_ref, b_ref, g_ref, lng_ref, lnb_ref, rk_ref,
                gsum_ref, tril_ref, o_ref, s_ref):
    c = pl.program_id(0)
    C = RW_CHUNK
    N = RW_N

    @pl.when(c == 0)
    def _():
        s_ref[...] = jnp.zeros_like(s_ref)

    r = r_ref[...]
    lw = lw_ref[...]
    k = k_ref[...]
    v = v_ref[...]
    lw_hi = lw.astype(BF16)
    rem = lw - lw_hi.astype(F32)
    lw_mid = rem.astype(BF16)
    lw_lo = (rem - lw_mid.astype(F32)).astype(BF16)
    tril = tril_ref[...]
    lg = _dot(tril, lw_hi) + (_dot(tril, lw_mid) + _dot(tril, lw_lo))
    gam = jnp.exp(lg)
    ginv = jnp.exp(-lg)
    at = (an_ref[...] * jnp.exp(lg - lw)).astype(BF16)
    bt = (b_ref[...] * ginv).astype(BF16)
    kt = (k * ginv).astype(BF16)
    rt_f = r * gam
    rt = rt_f.astype(BF16)
    vb = v.astype(BF16)
    g_last = gam[C - 1:C, :]

    rowi = lax.broadcasted_iota(jnp.int32, (C, C), 0)
    coli = lax.broadcasted_iota(jnp.int32, (C, C), 1)
    strict = rowi > coli
    incl = rowi >= coli

    heads = range(RW_HEADS)
    hs = [slice(h * N, (h + 1) * N) for h in heads]
    bth = [bt[:, s] for s in hs]
    kth = [kt[:, s] for s in hs]
    vh = [vb[:, s] for s in hs]
    big = [_dot_t(jnp.concatenate([at[:, s], rt[:, s]], axis=0), jnp.concatenate([bth[h], kth[h]], axis=0))
           for h, s in zip(heads, hs)]
    a_ab = [jnp.where(strict, big[h][:C, :C], 0.0) for h in heads]
    a_ak = [jnp.where(strict, big[h][:C, C:], 0.0).astype(BF16) for h in heads]
    a_rb = [jnp.where(incl, big[h][C:, :C], 0.0).astype(BF16) for h in heads]
    a_rk = [jnp.where(incl, big[h][C:, C:], 0.0).astype(BF16) for h in heads]
    akv = [_dot(a_ak[h], vh[h]) for h in heads]
    def same_block(size):
        shift = size.bit_length() - 1
        return (rowi >> shift) == (coli >> shift)

    tinv = [jnp.where(same_block(2), a_ab[h], 0.0) + jnp.where(rowi == coli, 1.0, 0.0) for h in heads]
    size = 2
    while size < C:
        lower_left = same_block(2 * size) & jnp.logical_not(same_block(size))
        off = [jnp.where(lower_left, a_ab[h], 0.0).astype(BF16) for h in heads]
        tb = [tinv[h].astype(BF16) for h in heads]
        half = [_dot(tb[h], off[h]).astype(BF16) for h in heads]
        tinv = [tinv[h] + _dot(half[h], tb[h]) for h in heads]
        size *= 2
    xb = [_dot(tinv[h].astype(BF16), jnp.concatenate([at[:, hs[h]], akv[h].astype(BF16)], axis=1)).astype(BF16)
          for h in heads]
    yx = [_dot(a_rb[h], xb[h]) for h in heads]
    ykv = [_dot(a_rk[h], vh[h]) for h in heads]
    xtb = [_dot_tl(xb[h], bth[h]) for h in heads]
    vtk = [_dot_tl(vh[h], kth[h]) for h in heads]
    ys = []
    for h in heads:
        gl = g_last[:, hs[h]]
        y1 = (rt_f[:, hs[h]] + yx[h][:, :N]).astype(BF16)
        s0 = s_ref[h]
        s0b = s0.astype(BF16)
        ys.append(_dot_t(y1, s0b) + (yx[h][:, N:] + ykv[h]))
        s_ref[h] = (s0 + _dot(s0b, xtb[h][:N].astype(BF16)) + (xtb[h][N:] + vtk[h])) * gl
    y = jnp.concatenate(ys, axis=1)

    gsum = gsum_ref[...]
    inv_n = 1.0 / N
    mu = _split_dot(y, gsum, 3) * inv_n
    yc = y - mu
    var = _split_dot(yc * yc, gsum, 3) * inv_n
    yn = yc * lax.rsqrt(var + RW_GN_EPS) * lng_ref[...] + lnb_ref[...]
    bonus = _split_dot(r * k * rk_ref[...], gsum, 3)
    o_ref[...] = ((yn + bonus * v) * g_ref[...]).astype(o_ref.dtype)


def _wkv(r, lw, k, v, an, b, g, ln_g, ln_b, r_k, gsum):
    t = r.shape[0]
    C = RW_CHUNK
    tril = jnp.asarray(np.tril(np.ones((C, C), np.float32)), BF16)
    row = pl.BlockSpec((C, RW_W), lambda c: (c, 0))
    vec = pl.BlockSpec((1, RW_W), lambda c: (0, 0))
    return pl.pallas_call(
        _wkv_kernel,
        out_shape=jax.ShapeDtypeStruct((t, RW_W), BF16),
        grid=(t // C,),
        in_specs=[row] * 7 + [vec] * 3 + [pl.BlockSpec((RW_W, RW_W), lambda c: (0, 0)),
                                          pl.BlockSpec((C, C), lambda c: (0, 0))],
        out_specs=row,
        scratch_shapes=[pltpu.VMEM((RW_HEADS, RW_N, RW_N), F32)],
        compiler_params=_cparams(("arbitrary",)),
        name="wkv7",
    )(r, lw, k, v, an, b, g, ln_g, ln_b, r_k, gsum, tril)


NSA_PREP_TM = 512


N_NORM_SLABS = (NSA_W + 2 * NSA_KV_W) // LANES
N_RAW_SLABS = 4 * NSA_KV_W // LANES
N_NSA_SLABS = N_NORM_SLABS + N_RAW_SLABS + 1


def _nsa_prep_kernel(*refs):
    x_refs = refs[:N_NSA_SLABS]
    (cos_ref, sin_ref, w_ref, gmean_ref, big_ref,
     qk_ref, hmat_ref, v1_ref, gate_ref, qpad_ref, ksx_ref) = refs[N_NSA_SLABS:]
    cos = cos_ref[...]
    sin = sin_ref[...]
    gmean = gmean_ref[...]
    half = NSA_DH // 2
    tm = cos.shape[0]
    lane = lax.broadcasted_iota(jnp.int32, cos.shape, 1)
    first_half = (lane % NSA_DH) < half
    low = lane < NSA_DH
    blk_row = (pl.program_id(0) * tm + lax.broadcasted_iota(jnp.int32, cos.shape, 0)) // SLC_BLOCK
    big = big_ref[...]
    mark_lo = jnp.where(blk_row == lane - NSA_DH, big, 0.0)
    mark_hi = jnp.where(low & (blk_row == lane + NSA_DH), big, 0.0).astype(BF16)
    n_q = NSA_W // LANES
    n_k = NSA_KV_W // LANES
    for s in range(N_NORM_SLABS):
        x = x_refs[s][...]
        y = x * lax.rsqrt(_split_dot(x * x, gmean, 3) + RMS_EPS) * w_ref[s]
        rot = jnp.where(first_half, pltpu.roll(y, LANES - half, axis=1), pltpu.roll(y, half, axis=1))
        res32 = y * cos + rot * sin
        res = res32.astype(qk_ref.dtype)
        qk_ref[2 * s] = res[:, :NSA_DH]
        qk_ref[2 * s + 1] = res[:, NSA_DH:]
        if s >= n_q + n_k:
            continue
        swapped = pltpu.roll(res32, NSA_DH, axis=1)
        if s < n_q:
            qpad_ref[2 * s] = jnp.where(low, res32, 0.0).astype(BF16)
            qpad_ref[2 * s + 1] = jnp.where(low, swapped, 0.0).astype(BF16)
        else:
            g0 = 2 * (s - n_q)
            ksx_ref[g0, :, :LANES] = jnp.where(low, res32, mark_lo).astype(BF16)
            ksx_ref[g0 + 1, :, :LANES] = jnp.where(low, swapped, mark_lo).astype(BF16)
            ksx_ref[g0, :, LANES:] = mark_hi
            ksx_ref[g0 + 1, :, LANES:] = mark_hi

    rows = hmat_ref.shape[2]
    low_r = lax.broadcasted_iota(jnp.int32, (rows, LANES), 1) < NSA_DH
    for s in range(NSA_KV_W // LANES * 2):
        kind, pair = divmod(s, NSA_KV_W // LANES)
        x_ref = x_refs[N_NORM_SLABS + s]
        for t2 in range(CMP_STRIDE // 2):
            r0 = x_ref[pl.ds(2 * t2, rows, stride=CMP_STRIDE), :]
            r1 = x_ref[pl.ds(2 * t2 + 1, rows, stride=CMP_STRIDE), :]
            cols = slice(t2 * LANES, (t2 + 1) * LANES)
            hmat_ref[kind, 2 * pair, :, cols] = jnp.where(low_r, r0, pltpu.roll(r1, NSA_DH, axis=1)).astype(BF16)
            hmat_ref[kind, 2 * pair + 1, :, cols] = jnp.where(low_r, pltpu.roll(r0, NSA_DH, axis=1), r1).astype(BF16)

    one_col = jnp.where(lane == NSA_DH, 1.0, 0.0)
    for s in range(NSA_KV_W // LANES * 2):
        x = x_refs[N_NORM_SLABS + N_RAW_SLABS // 2 + s][...]
        v1_ref[2 * s] = jnp.where(low, x, one_col).astype(BF16)
        v1_ref[2 * s + 1] = jnp.where(low, pltpu.roll(x, NSA_DH, axis=1), one_col).astype(BF16)

    xg = x_refs[N_NSA_SLABS - 1][...]
    per_group = 3 * NSA_HG
    for g in range(NSA_G):
        shifted = xg if g == 0 else pltpu.roll(xg, LANES - per_group * g, axis=1)
        gate_ref[g] = jnp.where(lane < per_group, shifted, 0.0)


def _nsa_prep(proj, w_slabs, cos, sin, gmean, big):
    t = proj.shape[0]
    assert t // SLC_BLOCK <= LANES
    tm = NSA_PREP_TM
    base = C_NSA_Q // LANES
    tab = pl.BlockSpec((tm, LANES), lambda i: (i, 0))
    n_heads_out = 2 * N_NORM_SLABS
    return pl.pallas_call(
        _nsa_prep_kernel,
        out_shape=[jax.ShapeDtypeStruct((n_heads_out, t, NSA_DH), BF16),
                   jax.ShapeDtypeStruct((2, NSA_G, t // CMP_STRIDE, CMP_STRIDE * NSA_DH), BF16),
                   jax.ShapeDtypeStruct((2 * NSA_G, t, LANES), BF16),
                   jax.ShapeDtypeStruct((NSA_G, t, LANES), F32),
                   jax.ShapeDtypeStruct((NSA_HEADS, t, LANES), BF16),
                   jax.ShapeDtypeStruct((NSA_G, t, 2 * LANES), BF16)],
        grid=(t // tm,),
        in_specs=[pl.BlockSpec((tm, LANES), lambda i, s=s: (i, base + s)) for s in range(N_NSA_SLABS)]
        + [tab, tab, pl.BlockSpec((N_NORM_SLABS, 1, LANES), lambda i: (0, 0, 0)),
           pl.BlockSpec((LANES, LANES), lambda i: (0, 0)), pl.BlockSpec((1, LANES), lambda i: (0, 0))],
        out_specs=[pl.BlockSpec((n_heads_out, tm, NSA_DH), lambda i: (0, i, 0)),
                   pl.BlockSpec((2, NSA_G, tm // CMP_STRIDE, CMP_STRIDE * NSA_DH), lambda i: (0, 0, i, 0)),
                   pl.BlockSpec((2 * NSA_G, tm, LANES), lambda i: (0, i, 0)),
                   pl.BlockSpec((NSA_G, tm, LANES), lambda i: (0, i, 0)),
                   pl.BlockSpec((NSA_HEADS, tm, LANES), lambda i: (0, i, 0)),
                   pl.BlockSpec((NSA_G, tm, 2 * LANES), lambda i: (0, i, 0))],
        compiler_params=_cparams(("parallel",)),
        name="nsa_prep",
    )(*([proj] * N_NSA_SLABS), cos, sin, w_slabs, gmean, big)


def _compress_kernel(h_ref, w1_ref, w2_ref, pe_ref, nw_ref, cos_ref, sin_ref, o_ref):
    kind = pl.program_id(0)
    nc = h_ref.shape[2]
    half_in = CMP_STRIDE * NSA_DH
    hm = h_ref[0, 0]
    w1 = w1_ref[0]
    first = _dot(hm, w1[:half_in])
    second = _dot(hm, w1[half_in:])
    const = _dot(pe_ref[0], w1)[0:1, :]
    pre = first + pltpu.roll(second, nc - 1, axis=0) + const
    hid = 0.5 * pre * (1.0 + jnp.tanh(0.7978845608028654 * (pre + 0.044715 * pre * pre * pre)))
    out = _dot(hid.astype(BF16), w2_ref[0])
    y = out * lax.rsqrt(jnp.mean(out * out, axis=-1, keepdims=True) + RMS_EPS) * nw_ref[...]
    half = NSA_DH // 2
    rot = jnp.concatenate([y[:, half:], y[:, :half]], axis=1)
    roped = y * cos_ref[...] + rot * sin_ref[...]
    o_ref[0, 0] = jnp.where(kind == 0, roped, out).astype(o_ref.dtype)


def _compress(hmat, w1, w2, pe, nw, cos_c, sin_c):
    _, g, nc, width = hmat.shape
    return pl.pallas_call(
        _compress_kernel,
        out_shape=jax.ShapeDtypeStruct((2, g, nc, NSA_DH), BF16),
        grid=(2, g),
        in_specs=[pl.BlockSpec((1, 1, nc, width), lambda a, b: (a, b, 0, 0)),
                  pl.BlockSpec((1, 2 * width, CMP_HIDDEN), lambda a, b: (a, 0, 0)),
                  pl.BlockSpec((1, CMP_HIDDEN, NSA_DH), lambda a, b: (a, 0, 0)),
                  pl.BlockSpec((1, 8, 2 * width), lambda a, b: (a, 0, 0)),
                  pl.BlockSpec((1, NSA_DH), lambda a, b: (0, 0)),
                  pl.BlockSpec((nc, NSA_DH), lambda a, b: (0, 0)),
                  pl.BlockSpec((nc, NSA_DH), lambda a, b: (0, 0))],
        out_specs=pl.BlockSpec((1, 1, nc, NSA_DH), lambda a, b: (a, b, 0, 0)),
        compiler_params=_cparams(("parallel", "parallel")),
        name="nsa_compress",
    )(hmat, w1, w2, pe, nw, cos_c, sin_c)


SLC_KT = 1024
WIN_KEYS = WINDOW + NSA_TQ


INT32_MIN = -2 ** 31


def _nsa_attn_kernel(q_ref, qpad_ref, gate_ref, kc_ref, vc_ref, ks_ref, vs_ref, kw_ref, vw_ref, cis_ref, ltri_ref,
                     o_ref):
    qi = pl.program_id(1)
    nc = kc_ref.shape[2]
    nb = cis_ref.shape[1]
    heads = range(NSA_HG)
    start = qi * NSA_TQ
    qh = [q_ref[hh] for hh in heads]
    tok = start + lax.broadcasted_iota(jnp.int32, (NSA_TQ, 1), 0)

    def softmax_terms(scores):
        return [jnp.exp(s - jnp.max(s, axis=-1, keepdims=True)) for s in scores]

    kc = kc_ref[0, 0]
    vc = vc_ref[0, 0]
    cend = lax.broadcasted_iota(jnp.int32, (1, nc), 1) * CMP_STRIDE + (CMP_BLOCK - 1)
    bias_c = jnp.where(cend <= tok, 0.0, NEG_INF)
    ec = softmax_terms([_dot_t(qh[hh], kc) + bias_c for hh in heads])
    sees_any = tok >= CMP_BLOCK - 1
    pc = [ec[hh] * jnp.where(sees_any, 1.0 / jnp.sum(ec[hh], axis=-1, keepdims=True), 0.0) for hh in heads]
    o_c = [_dot(pc[hh].astype(BF16), vc) for hh in heads]

    psum = pc[0]
    for hh in range(1, NSA_HG):
        psum = psum + pc[hh]
    score = _split_dot(psum, cis_ref[...], 3)
    blk = lax.broadcasted_iota(jnp.int32, (1, nb), 1)
    cur = tok // SLC_BLOCK
    forced = (blk == 0) | (blk == cur) | (blk == cur - 1)
    score = jnp.where(forced, FORCED_SCORE, score)
    score = jnp.where(blk <= cur, score, -jnp.inf)

    kbase = pl.multiple_of(jnp.maximum(start - WINDOW, 0), NSA_TQ)
    kw = kw_ref[0, pl.ds(kbase, WIN_KEYS), :]
    lag = tok - (kbase + lax.broadcasted_iota(jnp.int32, (1, WIN_KEYS), 1))
    bias_w = jnp.where((lag >= 0) & (lag < WINDOW), 0.0, NEG_INF)
    sw = [_dot_t(qh[hh], kw) for hh in heads]

    bits = lax.bitcast_convert_type(score.T, jnp.int32)
    key = bits ^ ((bits >> 31) & 0x7FFFFFFF)
    n_sel = min(SLC_TOPK, nb)

    def enough(c):
        return jnp.sum(jnp.where(key >= c, 1.0, 0.0), axis=0, keepdims=True) >= n_sel

    def enough3(c1, c2, c3):
        packed = jnp.where(key >= c3, 65793.0, jnp.where(key >= c2, 257.0, jnp.where(key >= c1, 1.0, 0.0)))
        tot = jnp.sum(packed, axis=0, keepdims=True).astype(jnp.int32)
        return (tot & 255) >= n_sel, ((tot >> 8) & 255) >= n_sel, (tot >> 16) >= n_sel

    zero_row = jnp.zeros((1, NSA_TQ), jnp.int32)
    thr = jnp.where(enough(zero_row), zero_row, INT32_MIN)
    for hi in range(30, 0, -2):
        c1 = thr + (1 << (hi - 1))
        c2 = thr + (1 << hi)
        c3 = c2 + (1 << (hi - 1))
        e1, e2, e3 = enough3(c1, c2, c3)
        thr = jnp.where(e3, c3, jnp.where(e2, c2, jnp.where(e1, c1, thr)))
    c1 = thr + 1
    thr = jnp.where(enough(c1), c1, thr)
    above = key > thr
    tied = key == thr
    need = n_sel - jnp.sum(jnp.where(above, 1.0, 0.0), axis=0, keepdims=True)
    tied_before = _dot(ltri_ref[...], jnp.where(tied, 1.0, 0.0).astype(BF16))
    sel_t = above | (tied & (tied_before < need))
    sel = jnp.where(sel_t, 1.0, 0.0).T
    if nb < LANES:
        sel = jnp.concatenate([sel, jnp.zeros((NSA_TQ, LANES - nb), F32)], axis=1)
    lane_q = lax.broadcasted_iota(jnp.int32, (NSA_TQ, LANES), 1)
    sel_sw = pltpu.roll(sel, NSA_DH, axis=1)
    sel_lo = jnp.where(lane_q >= NSA_DH, sel_sw, 0.0)
    sel_hi = jnp.where(lane_q < NSA_DH, sel_sw, 0.0)
    qx = [jnp.concatenate([qpad_ref[hh].astype(F32) + sel_lo, sel_hi], axis=1).astype(BF16) for hh in heads]

    sw = [sw[hh] + bias_w for hh in heads]
    ew = [jnp.exp((s - jnp.max(s, axis=-1, keepdims=True)).astype(BF16)) for s in sw]
    acc_w = [_dot(ew[hh], vw_ref[0, pl.ds(kbase, WIN_KEYS), :]) for hh in heads]

    n_tiles = (start + NSA_TQ + SLC_KT - 1) // SLC_KT

    def sel_tile(j, carry, causal):
        ms, accs = carry
        k0 = pl.multiple_of(j * SLC_KT, SLC_KT)
        kt = ks_ref[0, pl.ds(k0, SLC_KT), :]
        vt = vs_ref[0, pl.ds(k0, SLC_KT), :]
        sj = [_dot_t(qx[hh], kt) for hh in heads]
        if causal:
            future = (k0 + lax.broadcasted_iota(jnp.int32, (1, SLC_KT), 1)) > tok
            sj = [jnp.where(future, NEG_INF, s) for s in sj]
        m_new = [jnp.maximum(ms[hh], jnp.max(sj[hh], axis=-1, keepdims=True)) for hh in heads]
        pj = [jnp.exp((sj[hh] - m_new[hh]).astype(BF16)) for hh in heads]
        accs = [jnp.exp(ms[hh] - m_new[hh]) * accs[hh] + _dot(pj[hh], vt) for hh in heads]
        return tuple(m_new), tuple(accs)

    m0 = tuple(jnp.full((NSA_TQ, 1), NEG_INF, F32) for _ in heads)
    a0 = tuple(jnp.zeros((NSA_TQ, LANES), F32) for _ in heads)
    carry = lax.fori_loop(0, n_tiles - 1, functools.partial(sel_tile, causal=False), (m0, a0))
    _, acc_s = sel_tile(n_tiles - 1, carry, causal=True)

    gt = jax.nn.sigmoid(gate_ref[0])
    outs = []
    for hh in heads:
        g0, g1, g2 = (gt[:, 3 * hh + br:3 * hh + br + 1] for br in range(3))
        scale_s = g1 / acc_s[hh][:, NSA_DH:NSA_DH + 1]
        scale_w = g2 / acc_w[hh][:, NSA_DH:NSA_DH + 1]
        outs.append(g0 * o_c[hh] + scale_s * acc_s[hh][:, :NSA_DH] + scale_w * acc_w[hh][:, :NSA_DH])
    o_ref[...] = jnp.concatenate(outs, axis=1).astype(o_ref.dtype)


def _nsa_attn(qk_hm, qpad, ksx, gates, cmp_kv, v1, cis, ltri):
    t = qk_hm.shape[1]
    nq = t // NSA_TQ
    nc = cmp_kv.shape[2]
    nb = cis.shape[1]
    ks_spec = pl.BlockSpec((1, t, 2 * LANES), lambda g, i: (g, 0, 0))
    kw_spec = pl.BlockSpec((1, t, NSA_DH), lambda g, i: (NSA_HEADS + NSA_G + g, 0, 0))
    vs_spec = pl.BlockSpec((1, t, LANES), lambda g, i: (g, 0, 0))
    vw_spec = pl.BlockSpec((1, t, LANES), lambda g, i: (NSA_G + g, 0, 0))
    return pl.pallas_call(
        _nsa_attn_kernel,
        out_shape=jax.ShapeDtypeStruct((t, NSA_W), BF16),
        grid=(NSA_G, nq),
        in_specs=[pl.BlockSpec((NSA_HG, NSA_TQ, NSA_DH), lambda g, i: (g, i, 0)),
                  pl.BlockSpec((NSA_HG, NSA_TQ, LANES), lambda g, i: (g, i, 0)),
                  pl.BlockSpec((1, NSA_TQ, LANES), lambda g, i: (g, i, 0)),
                  pl.BlockSpec((1, 1, nc, NSA_DH), lambda g, i: (0, g, 0, 0)),
                  pl.BlockSpec((1, 1, nc, NSA_DH), lambda g, i: (1, g, 0, 0)),
                  ks_spec, vs_spec, kw_spec, vw_spec,
                  pl.BlockSpec((nc, nb), lambda g, i: (0, 0)),
                  pl.BlockSpec((nb, nb), lambda g, i: (0, 0))],
        out_specs=pl.BlockSpec((NSA_TQ, NSA_HG * NSA_DH), lambda g, i: (i, g)),
        compiler_params=_cparams(("parallel", "arbitrary")),
        name="nsa_attention",
    )(qk_hm, qpad, gates, cmp_kv, cmp_kv, ksx, v1, qk_hm, v1, cis, ltri)


def _rope_tables(pos, dh, reps):
    half = dh // 2
    inv_freq = ROPE_THETA ** (-np.arange(half, dtype=np.float64) / half)
    ang = np.asarray(pos, np.float64)[:, None] * inv_freq[None, :]
    cos = np.cos(ang)
    sin = np.sin(ang)
    cos_t = np.tile(np.concatenate([cos, cos], axis=1), (1, reps))
    sin_t = np.tile(np.concatenate([-sin, sin], axis=1), (1, reps))
    return jnp.asarray(cos_t, F32), jnp.asarray(sin_t, F32)


def _pad_cols(a, width):
    return jnp.pad(a, ((0, 0), (0, width - a.shape[1])))


def _pad_rows(a, height):
    return jnp.pad(a, ((0, height - a.shape[0]), (0, 0)))


def _pack_w_in_kernel(w_ref, xv_ref, o_ref):
    o_ref[...] = _pack_w_in(w_ref[...], xv_ref[...]).astype(o_ref.dtype)


def _pack_w_in_call(w_in, layer, xv):
    _, d, n_in = w_in.shape
    tk = 256
    return pl.pallas_call(
        _pack_w_in_kernel,
        out_shape=jax.ShapeDtypeStruct((d, PROJ_W), BF16),
        grid=(d // tk,),
        in_specs=[pl.BlockSpec((None, tk, n_in), lambda i: (layer, i, 0)),
                  pl.BlockSpec((tk, LANES), lambda i: (i, 0))],
        out_specs=pl.BlockSpec((tk, PROJ_W), lambda i: (i, 0)),
        compiler_params=_cparams(("parallel",)),
        name="pack_w_in",
    )(w_in, xv)


def _pack_w_in(w_l, xv):
    d = w_l.shape[0]
    o = 0
    ret = w_l[:, o:o + 4 * RET_W]; o += 4 * RET_W
    rkv = w_l[:, o:o + 3 * RW_W]; o += 3 * RW_W
    xw = w_l[:, o:o + RW_DECAY_RANK]; o += RW_DECAY_RANK
    xa = w_l[:, o:o + RW_A_RANK]; o += RW_A_RANK
    xg = w_l[:, o:o + RW_GATE_RANK]; o += RW_GATE_RANK
    q = w_l[:, o:o + NSA_W]; o += NSA_W
    kc, vc, ks, vs, kw, vw = (w_l[:, o + i * NSA_KV_W:o + (i + 1) * NSA_KV_W] for i in range(6))
    o += 6 * NSA_KV_W
    gates = w_l[:, o:o + 3 * NSA_HEADS]
    return jnp.concatenate([ret, rkv, _pad_cols(xw, LANES), _pad_cols(xa, LANES), xg, xv,
                            q, ks, kw, kc, vc, vs, vw, _pad_cols(gates, LANES)], axis=1)


def _pack_mix(mix, vres_mix):
    o = 3 * RW_W
    xw = mix[o:o + RW_DECAY_RANK]; o += RW_DECAY_RANK
    xa = mix[o:o + RW_A_RANK]; o += RW_A_RANK
    xg = mix[o:o + RW_GATE_RANK]
    z = lambda n: jnp.zeros((n,), mix.dtype)
    xv = z(LANES) if vres_mix is None else jnp.concatenate([vres_mix, z(LANES - RW_V_RANK)])
    return jnp.concatenate([mix[:3 * RW_W], xw, z(LANES - RW_DECAY_RANK), xa, z(LANES - RW_A_RANK), xg, xv])[None, :]


def kernel(x, ln1_g, w_in, w_in_vres, rwkv_mix, rwkv_vres_mix, rwkv_w0, rwkv_w2, rwkv_a0, rwkv_a2, rwkv_v0, rwkv_v2, rwkv_g2, rwkv_k_k, rwkv_k_a, rwkv_r_k, rwkv_ln_g, rwkv_ln_b, nsa_q_norm, nsa_k_norm, nsa_cmp_pe, nsa_cmp_k_w1, nsa_cmp_k_w2, nsa_cmp_v_w1, nsa_cmp_v_w2, w_out, ln2_g, w_up, w_down):
    bsz, t, d = x.shape
    assert bsz == 1
    depth = w_in.shape[0]
    nc = t // CMP_STRIDE
    nb = t // SLC_BLOCK
    xs = x.reshape(t, d)

    pos = np.arange(t)
    cos_r, sin_r = _rope_tables(pos, RET_DH, 1)
    cos_n, sin_n = _rope_tables(pos, NSA_DH, 2)
    cos_c, sin_c = _rope_tables(np.arange(nc) * CMP_STRIDE + (CMP_BLOCK - 1), NSA_DH, 1)
    lane_head = np.arange(RW_W) // RW_N
    gsum = jnp.asarray(lane_head[:, None] == lane_head[None, :], BF16)
    lane_h2 = np.arange(LANES) // NSA_DH
    gmean = jnp.asarray((lane_h2[:, None] == lane_h2[None, :]) / float(NSA_DH), BF16)
    cstart = np.arange(nc) * CMP_STRIDE
    sstart = np.arange(nb) * SLC_BLOCK
    cis = jnp.asarray((cstart[:, None] <= sstart[None, :] + SLC_BLOCK - 1)
                      & (cstart[:, None] + CMP_BLOCK - 1 >= sstart[None, :]), BF16)

    ltri = jnp.asarray(np.tril(np.ones((nb, nb), np.float32), -1), BF16)

    v_first = jnp.zeros((t, RW_W), F32)
    for l in range(depth):
        vres = l > 0
        xv_w = _pad_cols(w_in_vres[l - 1], LANES) if vres else jnp.zeros((d, LANES), F32)
        w_cat = _pack_w_in_call(w_in, l, xv_w)
        proj = _norm_matmul(xs, ln1_g[l], w_cat, name="in_proj")

        o_ret = _retention(proj, cos_r, sin_r)

        row = lambda a: a.reshape(1, -1)
        mix = _pack_mix(rwkv_mix[l], rwkv_vres_mix[l - 1] if vres else None)
        v0 = row(rwkv_v0[l - 1]) if vres else jnp.zeros((1, RW_W), F32)
        v2 = _pad_rows(rwkv_v2[l - 1], LANES) if vres else jnp.zeros((LANES, RW_W), F32)
        r_, lw_, k_, v_, an_, b_, g_ = _rwkv_pre(
            proj, mix, row(rwkv_w0[l]), _pad_rows(rwkv_w2[l], LANES), row(rwkv_a0[l]), _pad_rows(rwkv_a2[l], LANES),
            v0, v2, rwkv_g2[l], row(rwkv_k_k[l]), row(rwkv_k_a[l]), gsum, v_first, vres)
        if not vres:
            v_first = v_
        o_rwkv = _wkv(r_, lw_, k_, v_, an_, b_, g_, row(rwkv_ln_g[l]), row(rwkv_ln_b[l]), row(rwkv_r_k[l]), gsum)

        n_q = NSA_W // LANES
        n_k = NSA_KV_W // LANES
        w_slabs = jnp.stack([jnp.tile(nsa_q_norm[l] * (NSA_DH ** -0.5), 2)] * n_q
                            + [jnp.tile(nsa_k_norm[l, 1], 2)] * n_k + [jnp.tile(nsa_k_norm[l, 2], 2)] * n_k)[:, None, :]
        s_max = 1.02 * jnp.max(jnp.abs(nsa_q_norm[l])) * jnp.max(jnp.abs(nsa_k_norm[l, 1])) * (NSA_DH ** 0.5)
        big = jnp.full((1, LANES), jnp.exp2(jnp.ceil(jnp.log2(2.0 * s_max + 128.0))), F32)
        qk_hm, hmat, v1, gates, qpad, ksx = _nsa_prep(proj, w_slabs, cos_n, sin_n, gmean, big)
        w1 = jnp.stack([nsa_cmp_k_w1[l], nsa_cmp_v_w1[l]]).astype(BF16)
        w2 = jnp.stack([nsa_cmp_k_w2[l], nsa_cmp_v_w2[l]]).astype(BF16)
        pe = jnp.broadcast_to(nsa_cmp_pe[l].reshape(2, 1, CMP_BLOCK * NSA_DH), (2, 8, CMP_BLOCK * NSA_DH)).astype(BF16)
        cmp_kv = _compress(hmat, w1, w2, pe, row(nsa_k_norm[l, 0]), cos_c, sin_c)
        o_nsa = _nsa_attn(qk_hm, qpad, ksx, gates, cmp_kv, v1, cis, ltri)

        xs = _out_proj(o_ret, o_rwkv, o_nsa, w_out, l, xs)

        up = _norm_matmul(xs, ln2_g[l], w_up, layer=l, act="relu2", out_dtype=BF16, name="mlp_up")
        xs = _matmul(up, w_down, layer=l, res=xs, name="mlp_down")
    return xs.reshape(bsz, t, d)
```

```python
import functools

import numpy as np
import jax
import jax.numpy as jnp
from jax import lax
from jax.experimental import pallas as pl
from jax.experimental.pallas import tpu as pltpu

F32 = jnp.float32
BF16 = jnp.bfloat16

D_MODEL = 2048
RET_HEADS, RET_DH = 4, 128
RET_W = RET_HEADS * RET_DH
RW_HEADS, RW_N = 8, 64
RW_W = RW_HEADS * RW_N
RW_DECAY_RANK, RW_A_RANK, RW_V_RANK, RW_GATE_RANK = 96, 96, 64, 256
RW_GN_EPS = 64e-5
NSA_HEADS, NSA_G, NSA_DH = 16, 4, 64
NSA_HG = NSA_HEADS // NSA_G
NSA_W = NSA_HEADS * NSA_DH
NSA_KV_W = NSA_G * NSA_DH
CMP_BLOCK, CMP_STRIDE, CMP_HIDDEN = 32, 16, 256
SLC_BLOCK, SLC_TOPK, WINDOW = 64, 16, 512
NSA_TQ = 256
ROPE_THETA = 10000.0
MLP_HIDDEN = 4 * D_MODEL
RMS_EPS = 1e-6
NEG_INF = -1e30
FORCED_SCORE = 1e9

LANES = 128
VMEM_LIMIT = 56 * 1024 * 1024

C_RET = 0
C_RW_RKV = 2048
C_RW_XW = 3584
C_RW_XA = 3712
C_RW_XG = 3840
C_RW_XV = 4096
C_NSA_Q = 4224
C_NSA_KS = 5248
C_NSA_KW = 5504
C_NSA_KC = 5760
C_NSA_VC = 6016
C_NSA_VS = 6272
C_NSA_VW = 6528
C_NSA_GATE = 6784
PROJ_W = 6912


def _cparams(sem):
    return pltpu.CompilerParams(dimension_semantics=sem, vmem_limit_bytes=VMEM_LIMIT)


def _dot(a, b):
    return jnp.dot(a, b, preferred_element_type=F32)


def _dot_t(a, b):
    return lax.dot_general(a, b, (((1,), (1,)), ((), ())), preferred_element_type=F32)


def _dot_tl(a, b):
    return lax.dot_general(a, b, (((0,), (0,)), ((), ())), preferred_element_type=F32)


def _split_dot(x, m_bf16, passes):
    hi = x.astype(BF16)
    acc = _dot(hi, m_bf16)
    rem = x - hi.astype(F32)
    for _ in range(passes - 1):
        piece = rem.astype(BF16)
        acc = acc + _dot(piece, m_bf16)
        rem = rem - piece.astype(F32)
    return acc


def _mm_kernel(*refs, nk, act, has_res):
    a_ref, b_ref = refs[:2]
    r_ref = refs[2] if has_res else None
    o_ref = refs[3] if has_res else refs[2]

    def finish(y):
        if act == "relu2":
            y = jnp.square(jnp.maximum(y, 0.0))
        if has_res:
            y = y + r_ref[...]
        o_ref[...] = y.astype(o_ref.dtype)

    part = _dot(a_ref[...], b_ref[...].astype(BF16))
    if nk == 1:
        finish(part)
        return
    acc_ref = refs[-1]
    k = pl.program_id(2)

    @pl.when(k == 0)
    def _():
        acc_ref[...] = part

    @pl.when(k > 0)
    def _():
        acc_ref[...] += part

    @pl.when(k == nk - 1)
    def _():
        finish(acc_ref[...])


def _pick(n, cands):
    for c in cands:
        if n % c == 0:
            return c
    raise ValueError(f"no tile for {n}")


def _weight_spec(b, layer, block, index_map):
    if layer is None:
        return pl.BlockSpec(block, index_map)
    return pl.BlockSpec((None,) + block, lambda *idx: (layer,) + index_map(*idx))


def _matmul(a, b, *, layer=None, res=None, act=None, out_dtype=F32, name="matmul"):
    m, kd = a.shape
    n = b.shape[-1]
    tm = _pick(m, (1024, 512, 256, 128))
    tn = _pick(n, (1024, 768, 512, 256, 128))
    tk = _pick(kd, (2048, 1024, 512))
    nk = kd // tk
    in_specs = [pl.BlockSpec((tm, tk), lambda i, j, k: (i, k)),
                _weight_spec(b, layer, (tk, tn), lambda i, j, k: (k, j))]
    args = [a, b]
    if res is not None:
        in_specs.append(pl.BlockSpec((tm, tn), lambda i, j, k: (i, j)))
        args.append(res)
    return pl.pallas_call(
        functools.partial(_mm_kernel, nk=nk, act=act, has_res=res is not None),
        out_shape=jax.ShapeDtypeStruct((m, n), out_dtype),
        grid=(m // tm, n // tn, nk),
        in_specs=in_specs,
        out_specs=pl.BlockSpec((tm, tn), lambda i, j, k: (i, j)),
        scratch_shapes=[pltpu.VMEM((tm, tn), F32)] if nk > 1 else [],
        compiler_params=_cparams(("parallel", "parallel", "arbitrary")),
        name=name,
    )(*args)


def _norm_mm_kernel(x_ref, g_ref, b_ref, o_ref, h_ref, *, act):
    @pl.when(pl.program_id(1) == 0)
    def _():
        x = x_ref[...]
        ms = jnp.mean(x * x, axis=-1, keepdims=True)
        h_ref[...] = (x * lax.rsqrt(ms + RMS_EPS) * g_ref[...]).astype(BF16)

    y = _dot(h_ref[...], b_ref[...].astype(BF16))
    if act == "relu2":
        y = jnp.square(jnp.maximum(y, 0.0))
    o_ref[...] = y.astype(o_ref.dtype)


def _norm_matmul(x, g, b, *, layer=None, act=None, out_dtype=F32, name="norm_matmul"):
    m, kd = x.shape
    n = b.shape[-1]
    tm = _pick(m, (1024, 512, 256, 128))
    tn = _pick(n, (1024, 768, 512, 256, 128))
    return pl.pallas_call(
        functools.partial(_norm_mm_kernel, act=act),
        out_shape=jax.ShapeDtypeStruct((m, n), out_dtype),
        grid=(m // tm, n // tn),
        in_specs=[pl.BlockSpec((tm, kd), lambda i, j: (i, 0)),
                  pl.BlockSpec((1, kd), lambda i, j: (0, 0)),
                  _weight_spec(b, layer, (kd, tn), lambda i, j: (0, j))],
        out_specs=pl.BlockSpec((tm, tn), lambda i, j: (i, j)),
        scratch_shapes=[pltpu.VMEM((tm, kd), BF16)],
        compiler_params=_cparams(("parallel", "arbitrary")),
        name=name,
    )(x, g.reshape(1, kd), b)


def _out_proj_kernel(ret_ref, rw_ref, nsa_ref, w_ref, res_ref, o_ref, wb_ref):
    @pl.when(pl.program_id(1) == 0)
    def _():
        wb_ref[...] = w_ref[...].astype(BF16)

    k0 = ret_ref.shape[1]
    k1 = k0 + rw_ref.shape[1]
    acc = _dot(ret_ref[...], wb_ref[:k0, :])
    acc = acc + _dot(rw_ref[...], wb_ref[k0:k1, :])
    acc = acc + _dot(nsa_ref[...], wb_ref[k1:, :])
    o_ref[...] = acc + res_ref[...]


def _out_proj(o_ret, o_rwkv, o_nsa, w, layer, res):
    m = o_ret.shape[0]
    kd, n = w.shape[-2:]
    assert kd == o_ret.shape[1] + o_rwkv.shape[1] + o_nsa.shape[1]
    tm = _pick(m, (1024, 512, 256, 128))
    tn = _pick(n, (1024, 512, 256, 128))

    def rows(a):
        return pl.BlockSpec((tm, a.shape[1]), lambda j, i: (i, 0))

    return pl.pallas_call(
        _out_proj_kernel,
        out_shape=jax.ShapeDtypeStruct((m, n), F32),
        grid=(n // tn, m // tm),
        in_specs=[rows(o_ret), rows(o_rwkv), rows(o_nsa),
                  _weight_spec(w, layer, (kd, tn), lambda j, i: (0, j)),
                  pl.BlockSpec((tm, tn), lambda j, i: (i, j))],
        out_specs=pl.BlockSpec((tm, tn), lambda j, i: (i, j)),
        scratch_shapes=[pltpu.VMEM((kd, tn), BF16)],
        compiler_params=_cparams(("parallel", "arbitrary")),
        name="out_proj",
    )(o_ret, o_rwkv, o_nsa, w, res)


RET_CHUNK = 256


def _retention_kernel(q_ref, k_ref, v_ref, g_ref, cos_ref, sin_ref, dmat_ref, qd_ref, kd_ref, cd_ref, o_ref, state_ref):
    @pl.when(pl.program_id(0) == 0)
    def _():
        state_ref[...] = jnp.zeros_like(state_ref)

    cos = cos_ref[...]
    sin = sin_ref[...]

    def rope(x):
        return x * cos + pltpu.roll(x, RET_DH // 2, axis=1) * sin

    heads = range(RET_HEADS)
    hs = [slice(h * RET_DH, (h + 1) * RET_DH) for h in heads]
    q_all, k_all, v_all, gate = q_ref[...], k_ref[...], v_ref[...], g_ref[...]
    qb = [rope(q_all[:, s]).astype(BF16) for s in hs]
    kf = [rope(k_all[:, s]) * (RET_DH ** -0.5) for s in hs]
    kb = [kf[h].astype(BF16) for h in heads]
    vb = [v_all[:, s].astype(BF16) for s in hs]
    inner = [(_dot_t(qb[h], kb[h]) * dmat_ref[h]).astype(BF16) for h in heads]
    cross = [_dot(qb[h], state_ref[h].astype(BF16)) * qd_ref[h] for h in heads]
    kv = [_dot_tl((kf[h] * kd_ref[h]).astype(BF16), vb[h]) for h in heads]
    outs = []
    for h in heads:
        out = _dot(inner[h], vb[h]) + cross[h]
        state_ref[h] = cd_ref[h] * state_ref[h] + kv[h]
        outs.append(out * lax.rsqrt(jnp.mean(out * out, axis=-1, keepdims=True) + RMS_EPS))
    o_ref[...] = (gate * jax.nn.sigmoid(gate) * jnp.concatenate(outs, axis=1)).astype(o_ref.dtype)


def _retention(proj, cos, sin):
    t = proj.shape[0]
    chunk = RET_CHUNK
    log_gamma = np.log(1.0 - 2.0 ** (-5.0 - np.arange(RET_HEADS, dtype=np.float64)))[:, None, None]
    n = np.arange(chunk, dtype=np.float64)
    lag = n[:, None] - n[None, :]
    dmat = np.where(lag >= 0, np.exp(np.maximum(lag, 0.0)[None] * log_gamma), 0.0)
    ones = np.ones((1, 1, RET_DH))
    qd = np.exp((n + 1.0)[None, :, None] * log_gamma) * ones
    kd = np.exp((chunk - 1.0 - n)[None, :, None] * log_gamma) * ones
    cd = np.exp(chunk * log_gamma) * ones
    tables = [jnp.asarray(a, F32) for a in (dmat, qd, kd, cd)]
    base = C_RET // RET_W

    def col(off):
        return pl.BlockSpec((chunk, RET_W), lambda c, off=off: (c, base + off))

    def full(a):
        return pl.BlockSpec(a.shape, lambda c: (0, 0, 0))

    tab = pl.BlockSpec((chunk, RET_DH), lambda c: (c, 0))
    return pl.pallas_call(
        _retention_kernel,
        out_shape=jax.ShapeDtypeStruct((t, RET_W), BF16),
        grid=(t // chunk,),
        in_specs=[col(0), col(1), col(2), col(3), tab, tab] + [full(a) for a in tables],
        out_specs=pl.BlockSpec((chunk, RET_W), lambda c: (c, 0)),
        scratch_shapes=[pltpu.VMEM((RET_HEADS, RET_DH, RET_DH), F32)],
        compiler_params=_cparams(("arbitrary",)),
        name="retention",
    )(proj, proj, proj, proj, cos, sin, *tables)


def _dot_f32(a, b):
    a_hi = a.astype(BF16)
    b_hi = b.astype(BF16)
    a_lo = (a - a_hi.astype(F32)).astype(BF16)
    b_lo = (b - b_hi.astype(F32)).astype(BF16)
    return _dot(a_hi, b_hi) + (_dot(a_hi, b_lo) + _dot(a_lo, b_hi))


RW_PRE_TM = 256
_MIX_OFF = (0, 512, 1024, 1536, 1664, 1792, 2048, 2176)


def _rwkv_pre_kernel(r_ref, k_ref, v_ref, xw_ref, xa_ref, xg_ref, xv_ref,
                     rp_ref, kp_ref, vp_ref, xwp_ref, xap_ref, xgp_ref, xvp_ref,
                     mix_ref, w0_ref, w2_ref, a0_ref, a2_ref, v0_ref, v2_ref, g2_ref, kk_ref, ka_ref,
                     gsum_ref, vfirst_ref,
                     r_out, lw_out, k_out, v_out, an_out, b_out, g_out, *, use_vres):
    i = pl.program_id(0)

    def mixed(cur_ref, prev_ref, seg):
        cur = cur_ref[...]
        mix = mix_ref[:, _MIX_OFF[seg]:_MIX_OFF[seg + 1]]
        prev_row = jnp.where(i > 0, prev_ref[7:8, :], 0.0)
        rid = lax.broadcasted_iota(jnp.int32, cur.shape, 0)
        shifted = jnp.where(rid == 0, prev_row, pltpu.roll(cur, 1, axis=0))
        return cur + mix * (shifted - cur)

    r = mixed(r_ref, rp_ref, 0)
    k = mixed(k_ref, kp_ref, 1)
    v = mixed(v_ref, vp_ref, 2)
    xw = mixed(xw_ref, xwp_ref, 3)
    xa = mixed(xa_ref, xap_ref, 4)
    xg = mixed(xg_ref, xgp_ref, 5)

    z = -(w0_ref[...] + _dot_f32(jnp.tanh(xw), w2_ref[...]))
    softplus = jnp.maximum(z, 0.0) + jnp.log(1.0 + jnp.exp(-jnp.abs(z)))
    w = -softplus - 0.5
    lw_out[...] = -jnp.exp(w)
    a = jax.nn.sigmoid(a0_ref[...] + _dot_f32(xa, a2_ref[...]))
    g_out[...] = _dot_f32(jax.nn.sigmoid(xg), g2_ref[...])
    if use_vres:
        xv = mixed(xv_ref, xvp_ref, 6)
        v = v + (vfirst_ref[...] - v) * jax.nn.sigmoid(v0_ref[...] + _dot_f32(xv, v2_ref[...]))
    kk = k * kk_ref[...]
    ss = _split_dot(kk * kk, gsum_ref[...], 3)
    kk = kk / jnp.maximum(jnp.sqrt(ss), 1e-12)
    r_out[...] = r
    k_out[...] = k * (1.0 + (a - 1.0) * ka_ref[...])
    v_out[...] = v
    an_out[...] = -kk
    b_out[...] = kk * a


def _rwkv_pre(proj, mix, w0, w2, a0, a2, v0, v2, g2, k_k, k_a, gsum, v_first, use_vres):
    t = proj.shape[0]
    tm = RW_PRE_TM

    def cur(width, off):
        return pl.BlockSpec((tm, width), lambda i: (i, off // width))

    def prev(width, off):
        return pl.BlockSpec((8, width), lambda i: (jnp.maximum(i * (tm // 8) - 1, 0), off // width))

    def full(a):
        return pl.BlockSpec(a.shape, lambda i: (0,) * a.ndim)

    segs = [(RW_W, C_RW_RKV), (RW_W, C_RW_RKV + RW_W), (RW_W, C_RW_RKV + 2 * RW_W),
            (128, C_RW_XW), (128, C_RW_XA), (256, C_RW_XG), (128, C_RW_XV)]
    params = [mix, w0, w2, a0, a2, v0, v2, g2, k_k, k_a, gsum]
    row = pl.BlockSpec((tm, RW_W), lambda i: (i, 0))
    outs = pl.pallas_call(
        functools.partial(_rwkv_pre_kernel, use_vres=use_vres),
        out_shape=[jax.ShapeDtypeStruct((t, RW_W), F32)] * 7,
        grid=(t // tm,),
        in_specs=[cur(w, o) for w, o in segs] + [prev(w, o) for w, o in segs] + [full(p) for p in params] + [row],
        out_specs=[row] * 7,
        compiler_params=_cparams(("parallel",)),
        name="rwkv_pre",
    )(*([proj] * 14), *params, v_first)
    return outs


RW_CHUNK = 128


def _wkv_kernel(r_ref, lw_ref, k_ref, v_ref, an_ref, b_ref, g_ref, lng_ref, lnb_ref, rk_ref,
                gsum_ref, tril_ref, o_ref, s_ref):
    c = pl.program_id(0)
    C = RW_CHUNK
    N = RW_N

    @pl.when(c == 0)
    def _():
        s_ref[...] = jnp.zeros_like(s_ref)

    r = r_ref[...]
    lw = lw_ref[...]
    k = k_ref[...]
    v = v_ref[...]
    lw_hi = lw.astype(BF16)
    rem = lw - lw_hi.astype(F32)
    lw_mid = rem.astype(BF16)
    lw_lo = (rem - lw_mid.astype(F32)).astype(BF16)
    tril = tril_ref[...]
    lg = _dot(tril, lw_hi) + (_dot(tril, lw_mid) + _dot(tril, lw_lo))
    gam = jnp.exp(lg)
    ginv = jnp.exp(-lg)
    at = (an_ref[...] * jnp.exp(lg - lw)).astype(BF16)
    bt = (b_ref[...] * ginv).astype(BF16)
    kt = (k * ginv).astype(BF16)
    rt_f = r * gam
    rt = rt_f.astype(BF16)
    vb = v.astype(BF16)
    g_last = gam[C - 1:C, :]

    rowi = lax.broadcasted_iota(jnp.int32, (C, C), 0)
    coli = lax.broadcasted_iota(jnp.int32, (C, C), 1)
    strict = rowi > coli
    incl = rowi >= coli

    heads = range(RW_HEADS)
    hs = [slice(h * N, (h + 1) * N) for h in heads]
    bth = [bt[:, s] for s in hs]
    kth = [kt[:, s] for s in hs]
    vh = [vb[:, s] for s in hs]
    big = [_dot_t(jnp.concatenate([at[:, s], rt[:, s]], axis=0), jnp.concatenate([bth[h], kth[h]], axis=0))
           for h, s in zip(heads, hs)]
    a_ab = [jnp.where(strict, big[h][:C, :C], 0.0) for h in heads]
    a_ak = [jnp.where(strict, big[h][:C, C:], 0.0).astype(BF16) for h in heads]
    a_rb = [jnp.where(incl, big[h][C:, :C], 0.0).astype(BF16) for h in heads]
    a_rk = [jnp.where(incl, big[h][C:, C:], 0.0).astype(BF16) for h in heads]
    akv = [_dot(a_ak[h], vh[h]) for h in heads]
    def same_block(size):
        shift = size.bit_length() - 1
        return (rowi >> shift) == (coli >> shift)

    tinv = [jnp.where(same_block(2), a_ab[h], 0.0) + jnp.where(rowi == coli, 1.0, 0.0) for h in heads]
    size = 2
    while size < C:
        lower_left = same_block(2 * size) & jnp.logical_not(same_block(size))
        off = [jnp.where(lower_left, a_ab[h], 0.0).astype(BF16) for h in heads]
        tb = [tinv[h].astype(BF16) for h in heads]
        half = [_dot(tb[h], off[h]).astype(BF16) for h in heads]
        tinv = [tinv[h] + _dot(half[h], tb[h]) for h in heads]
        size *= 2
    xb = [_dot(tinv[h].astype(BF16), jnp.concatenate([at[:, hs[h]], akv[h].astype(BF16)], axis=1)).astype(BF16)
          for h in heads]
    yx = [_dot(a_rb[h], xb[h]) for h in heads]
    ykv = [_dot(a_rk[h], vh[h]) for h in heads]
    xtb = [_dot_tl(xb[h], bth[h]) for h in heads]
    vtk = [_dot_tl(vh[h], kth[h]) for h in heads]
    ys = []
    for h in heads:
        gl = g_last[:, hs[h]]
        y1 = (rt_f[:, hs[h]] + yx[h][:, :N]).astype(BF16)
        s0 = s_ref[h]
        s0b = s0.astype(BF16)
        ys.append(_dot_t(y1, s0b) + (yx[h][:, N:] + ykv[h]))
        s_ref[h] = (s0 + _dot(s0b, xtb[h][:N].astype(BF16)) + (xtb[h][N:] + vtk[h])) * gl
    y = jnp.concatenate(ys, axis=1)

    gsum = gsum_ref[...]
    inv_n = 1.0 / N
    mu = _split_dot(y, gsum, 3) * inv_n
    yc = y - mu
    var = _split_dot(yc * yc, gsum, 3) * inv_n
    yn = yc * lax.rsqrt(var + RW_GN_EPS) * lng_ref[...] + lnb_ref[...]
    bonus = _split_dot(r * k * rk_ref[...], gsum, 3)
    o_ref[...] = ((yn + bonus * v) * g_ref[...]).astype(o_ref.dtype)


def _wkv(r, lw, k, v, an, b, g, ln_g, ln_b, r_k, gsum):
    t = r.shape[0]
    C = RW_CHUNK
    tril = jnp.asarray(np.tril(np.ones((C, C), np.float32)), BF16)
    row = pl.BlockSpec((C, RW_W), lambda c: (c, 0))
    vec = pl.BlockSpec((1, RW_W), lambda c: (0, 0))
    return pl.pallas_call(
        _wkv_kernel,
        out_shape=jax.ShapeDtypeStruct((t, RW_W), BF16),
        grid=(t // C,),
        in_specs=[row] * 7 + [vec] * 3 + [pl.BlockSpec((RW_W, RW_W), lambda c: (0, 0)),
                                          pl.BlockSpec((C, C), lambda c: (0, 0))],
        out_specs=row,
        scratch_shapes=[pltpu.VMEM((RW_HEADS, RW_N, RW_N), F32)],
        compiler_params=_cparams(("arbitrary",)),
        name="wkv7",
    )(r, lw, k, v, an, b, g, ln_g, ln_b, r_k, gsum, tril)


NSA_PREP_TM = 512


N_NORM_SLABS = (NSA_W + 2 * NSA_KV_W) // LANES
N_RAW_SLABS = 4 * NSA_KV_W // LANES
N_NSA_SLABS = N_NORM_SLABS + N_RAW_SLABS + 1


def _nsa_prep_kernel(*refs):
    x_refs = refs[:N_NSA_SLABS]
    (cos_ref, sin_ref, w_ref, gmean_ref, big_ref,
     qk_ref, hmat_ref, v1_ref, gate_ref, qpad_ref, ksx_ref) = refs[N_NSA_SLABS:]
    cos = cos_ref[...]
    sin = sin_ref[...]
    gmean = gmean_ref[...]
    half = NSA_DH // 2
    tm = cos.shape[0]
    lane = lax.broadcasted_iota(jnp.int32, cos.shape, 1)
    first_half = (lane % NSA_DH) < half
    low = lane < NSA_DH
    blk_row = (pl.program_id(0) * tm + lax.broadcasted_iota(jnp.int32, cos.shape, 0)) // SLC_BLOCK
    big = big_ref[...]
    mark_lo = jnp.where(blk_row == lane - NSA_DH, big, 0.0)
    mark_hi = jnp.where(low & (blk_row == lane + NSA_DH), big, 0.0).astype(BF16)
    n_q = NSA_W // LANES
    n_k = NSA_KV_W // LANES
    for s in range(N_NORM_SLABS):
        x = x_refs[s][...]
        y = x * lax.rsqrt(_split_dot(x * x, gmean, 3) + RMS_EPS) * w_ref[s]
        rot = jnp.where(first_half, pltpu.roll(y, LANES - half, axis=1), pltpu.roll(y, half, axis=1))
        res32 = y * cos + rot * sin
        res = res32.astype(qk_ref.dtype)
        qk_ref[2 * s] = res[:, :NSA_DH]
        qk_ref[2 * s + 1] = res[:, NSA_DH:]
        if s >= n_q + n_k:
            continue
        swapped = pltpu.roll(res32, NSA_DH, axis=1)
        if s < n_q:
            qpad_ref[2 * s] = jnp.where(low, res32, 0.0).astype(BF16)
            qpad_ref[2 * s + 1] = jnp.where(low, swapped, 0.0).astype(BF16)
        else:
            g0 = 2 * (s - n_q)
            ksx_ref[g0, :, :LANES] = jnp.where(low, res32, mark_lo).astype(BF16)
            ksx_ref[g0 + 1, :, :LANES] = jnp.where(low, swapped, mark_lo).astype(BF16)
            ksx_ref[g0, :, LANES:] = mark_hi
            ksx_ref[g0 + 1, :, LANES:] = mark_hi

    rows = hmat_ref.shape[2]
    low_r = lax.broadcasted_iota(jnp.int32, (rows, LANES), 1) < NSA_DH
    for s in range(NSA_KV_W // LANES * 2):
        kind, pair = divmod(s, NSA_KV_W // LANES)
        x_ref = x_refs[N_NORM_SLABS + s]
        for t2 in range(CMP_STRIDE // 2):
            r0 = x_ref[pl.ds(2 * t2, rows, stride=CMP_STRIDE), :]
            r1 = x_ref[pl.ds(2 * t2 + 1, rows, stride=CMP_STRIDE), :]
            cols = slice(t2 * LANES, (t2 + 1) * LANES)
            hmat_ref[kind, 2 * pair, :, cols] = jnp.where(low_r, r0, pltpu.roll(r1, NSA_DH, axis=1)).astype(BF16)
            hmat_ref[kind, 2 * pair + 1, :, cols] = jnp.where(low_r, pltpu.roll(r0, NSA_DH, axis=1), r1).astype(BF16)

    one_col = jnp.where(lane == NSA_DH, 1.0, 0.0)
    for s in range(NSA_KV_W // LANES * 2):
        x = x_refs[N_NORM_SLABS + N_RAW_SLABS // 2 + s][...]
        v1_ref[2 * s] = jnp.where(low, x, one_col).astype(BF16)
        v1_ref[2 * s + 1] = jnp.where(low, pltpu.roll(x, NSA_DH, axis=1), one_col).astype(BF16)

    xg = x_refs[N_NSA_SLABS - 1][...]
    per_group = 3 * NSA_HG
    for g in range(NSA_G):
        shifted = xg if g == 0 else pltpu.roll(xg, LANES - per_group * g, axis=1)
        gate_ref[g] = jnp.where(lane < per_group, shifted, 0.0)


def _nsa_prep(proj, w_slabs, cos, sin, gmean, big):
    t = proj.shape[0]
    assert t // SLC_BLOCK <= LANES
    tm = NSA_PREP_TM
    base = C_NSA_Q // LANES
    tab = pl.BlockSpec((tm, LANES), lambda i: (i, 0))
    n_heads_out = 2 * N_NORM_SLABS
    return pl.pallas_call(
        _nsa_prep_kernel,
        out_shape=[jax.ShapeDtypeStruct((n_heads_out, t, NSA_DH), BF16),
                   jax.ShapeDtypeStruct((2, NSA_G, t // CMP_STRIDE, CMP_STRIDE * NSA_DH), BF16),
                   jax.ShapeDtypeStruct((2 * NSA_G, t, LANES), BF16),
                   jax.ShapeDtypeStruct((NSA_G, t, LANES), F32),
                   jax.ShapeDtypeStruct((NSA_HEADS, t, LANES), BF16),
                   jax.ShapeDtypeStruct((NSA_G, t, 2 * LANES), BF16)],
        grid=(t // tm,),
        in_specs=[pl.BlockSpec((tm, LANES), lambda i, s=s: (i, base + s)) for s in range(N_NSA_SLABS)]
        + [tab, tab, pl.BlockSpec((N_NORM_SLABS, 1, LANES), lambda i: (0, 0, 0)),
           pl.BlockSpec((LANES, LANES), lambda i: (0, 0)), pl.BlockSpec((1, LANES), lambda i: (0, 0))],
        out_specs=[pl.BlockSpec((n_heads_out, tm, NSA_DH), lambda i: (0, i, 0)),
                   pl.BlockSpec((2, NSA_G, tm // CMP_STRIDE, CMP_STRIDE * NSA_DH), lambda i: (0, 0, i, 0)),
                   pl.BlockSpec((2 * NSA_G, tm, LANES), lambda i: (0, i, 0)),
                   pl.BlockSpec((NSA_G, tm, LANES), lambda i: (0, i, 0)),
                   pl.BlockSpec((NSA_HEADS, tm, LANES), lambda i: (0, i, 0)),
                   pl.BlockSpec((NSA_G, tm, 2 * LANES), lambda i: (0, i, 0))],
        compiler_params=_cparams(("parallel",)),
        name="nsa_prep",
    )(*([proj] * N_NSA_SLABS), cos, sin, w_slabs, gmean, big)


def _compress_kernel(h_ref, w1_ref, w2_ref, pe_ref, nw_ref, cos_ref, sin_ref, o_ref):
    kind = pl.program_id(0)
    nc = h_ref.shape[2]
    half_in = CMP_STRIDE * NSA_DH
    hm = h_ref[0, 0]
    w1 = w1_ref[0]
    first = _dot(hm, w1[:half_in])
    second = _dot(hm, w1[half_in:])
    const = _dot(pe_ref[0], w1)[0:1, :]
    pre = first + pltpu.roll(second, nc - 1, axis=0) + const
    hid = 0.5 * pre * (1.0 + jnp.tanh(0.7978845608028654 * (pre + 0.044715 * pre * pre * pre)))
    out = _dot(hid.astype(BF16), w2_ref[0])
    y = out * lax.rsqrt(jnp.mean(out * out, axis=-1, keepdims=True) + RMS_EPS) * nw_ref[...]
    half = NSA_DH // 2
    rot = jnp.concatenate([y[:, half:], y[:, :half]], axis=1)
    roped = y * cos_ref[...] + rot * sin_ref[...]
    o_ref[0, 0] = jnp.where(kind == 0, roped, out).astype(o_ref.dtype)


def _compress(hmat, w1, w2, pe, nw, cos_c, sin_c):
    _, g, nc, width = hmat.shape
    return pl.pallas_call(
        _compress_kernel,
        out_shape=jax.ShapeDtypeStruct((2, g, nc, NSA_DH), BF16),
        grid=(2, g),
        in_specs=[pl.BlockSpec((1, 1, nc, width), lambda a, b: (a, b, 0, 0)),
                  pl.BlockSpec((1, 2 * width, CMP_HIDDEN), lambda a, b: (a, 0, 0)),
                  pl.BlockSpec((1, CMP_HIDDEN, NSA_DH), lambda a, b: (a, 0, 0)),
                  pl.BlockSpec((1, 8, 2 * width), lambda a, b: (a, 0, 0)),
                  pl.BlockSpec((1, NSA_DH), lambda a, b: (0, 0)),
                  pl.BlockSpec((nc, NSA_DH), lambda a, b: (0, 0)),
                  pl.BlockSpec((nc, NSA_DH), lambda a, b: (0, 0))],
        out_specs=pl.BlockSpec((1, 1, nc, NSA_DH), lambda a, b: (a, b, 0, 0)),
        compiler_params=_cparams(("parallel", "parallel")),
        name="nsa_compress",
    )(hmat, w1, w2, pe, nw, cos_c, sin_c)


SLC_KT = 1024
WIN_KEYS = WINDOW + NSA_TQ


INT32_MIN = -2 ** 31


def _nsa_attn_kernel(q_ref, qpad_ref, gate_ref, kc_ref, vc_ref, ks_ref, vs_ref, kw_ref, vw_ref, cis_ref, ltri_ref,
                     o_ref):
    qi = pl.program_id(1)
    nc = kc_ref.shape[2]
    nb = cis_ref.shape[1]
    heads = range(NSA_HG)
    start = qi * NSA_TQ
    qh = [q_ref[hh] for hh in heads]
    tok = start + lax.broadcasted_iota(jnp.int32, (NSA_TQ, 1), 0)

    def softmax_terms(scores):
        return [jnp.exp(s - jnp.max(s, axis=-1, keepdims=True)) for s in scores]

    def cmp_branch(width):
        kc = kc_ref[0, 0, :width, :]
        vc = vc_ref[0, 0, :width, :]
        cend = lax.broadcasted_iota(jnp.int32, (1, width), 1) * CMP_STRIDE + (CMP_BLOCK - 1)
        bias_c = jnp.where(cend <= tok, 0.0, NEG_INF)
        ec = softmax_terms([_dot_t(qh[hh], kc) + bias_c for hh in heads])
        sees_any = tok >= CMP_BLOCK - 1
        pc = [ec[hh] * jnp.where(sees_any, 1.0 / jnp.sum(ec[hh], axis=-1, keepdims=True), 0.0) for hh in heads]
        outs_c = tuple(_dot(pc[hh].astype(BF16), vc) for hh in heads)
        psum = pc[0]
        for hh in range(1, NSA_HG):
            psum = psum + pc[hh]
        return outs_c, _split_dot(psum, cis_ref[:width, :], 3)

    n_quarters = 4
    quarter = nc // n_quarters
    n_visible = (start + NSA_TQ - CMP_BLOCK) // CMP_STRIDE + 1
    which = jnp.clip((n_visible - 1) // quarter, 0, n_quarters - 1)
    o_c, score = lax.switch(which, [functools.partial(cmp_branch, (r + 1) * quarter) for r in range(n_quarters)])
    blk = lax.broadcasted_iota(jnp.int32, (1, nb), 1)
    cur = tok // SLC_BLOCK
    forced = (blk == 0) | (blk == cur) | (blk == cur - 1)
    score = jnp.where(forced, FORCED_SCORE, score)
    score = jnp.where(blk <= cur, score, -jnp.inf)

    kbase = pl.multiple_of(jnp.maximum(start - WINDOW, 0), NSA_TQ)
    kw = kw_ref[0, pl.ds(kbase, WIN_KEYS), :]
    lag = tok - (kbase + lax.broadcasted_iota(jnp.int32, (1, WIN_KEYS), 1))
    bias_w = jnp.where((lag >= 0) & (lag < WINDOW), 0.0, NEG_INF)
    sw = [_dot_t(qh[hh], kw) for hh in heads]

    bits = lax.bitcast_convert_type(score.T, jnp.int32)
    key = bits ^ ((bits >> 31) & 0x7FFFFFFF)
    n_sel = min(SLC_TOPK, nb)

    def enough(c):
        return jnp.sum(jnp.where(key >= c, 1.0, 0.0), axis=0, keepdims=True) >= n_sel

    def enough3(c1, c2, c3):
        packed = jnp.where(key >= c3, 65793.0, jnp.where(key >= c2, 257.0, jnp.where(key >= c1, 1.0, 0.0)))
        tot = jnp.sum(packed, axis=0, keepdims=True).astype(jnp.int32)
        return (tot & 255) >= n_sel, ((tot >> 8) & 255) >= n_sel, (tot >> 16) >= n_sel

    zero_row = jnp.zeros((1, NSA_TQ), jnp.int32)
    thr = jnp.where(enough(zero_row), zero_row, INT32_MIN)
    for hi in range(30, 0, -2):
        c1 = thr + (1 << (hi - 1))
        c2 = thr + (1 << hi)
        c3 = c2 + (1 << (hi - 1))
        e1, e2, e3 = enough3(c1, c2, c3)
        thr = jnp.where(e3, c3, jnp.where(e2, c2, jnp.where(e1, c1, thr)))
    c1 = thr + 1
    thr = jnp.where(enough(c1), c1, thr)
    above = key > thr
    tied = key == thr
    need = n_sel - jnp.sum(jnp.where(above, 1.0, 0.0), axis=0, keepdims=True)
    tied_before = _dot(ltri_ref[...], jnp.where(tied, 1.0, 0.0).astype(BF16))
    sel_t = above | (tied & (tied_before < need))
    sel = jnp.where(sel_t, 1.0, 0.0).T
    if nb < LANES:
        sel = jnp.concatenate([sel, jnp.zeros((NSA_TQ, LANES - nb), F32)], axis=1)
    lane_q = lax.broadcasted_iota(jnp.int32, (NSA_TQ, LANES), 1)
    sel_sw = pltpu.roll(sel, NSA_DH, axis=1)
    sel_lo = jnp.where(lane_q >= NSA_DH, sel_sw, 0.0)
    sel_hi = jnp.where(lane_q < NSA_DH, sel_sw, 0.0)
    qx = [jnp.concatenate([qpad_ref[hh].astype(F32) + sel_lo, sel_hi], axis=1).astype(BF16) for hh in heads]

    sw = [sw[hh] + bias_w for hh in heads]
    ew = [jnp.exp((s - jnp.max(s, axis=-1, keepdims=True)).astype(BF16)) for s in sw]
    acc_w = [_dot(ew[hh], vw_ref[0, pl.ds(kbase, WIN_KEYS), :]) for hh in heads]

    n_tiles = (start + NSA_TQ + SLC_KT - 1) // SLC_KT

    def sel_tile(j, carry, causal, width=SLC_KT):
        ms, accs = carry
        k0 = pl.multiple_of(j * SLC_KT, SLC_KT)
        kt = ks_ref[0, pl.ds(k0, width), :]
        vt = vs_ref[0, pl.ds(k0, width), :]
        sj = [_dot_t(qx[hh], kt) for hh in heads]
        if causal:
            future = (k0 + lax.broadcasted_iota(jnp.int32, (1, width), 1)) > tok
            sj = [jnp.where(future, NEG_INF, s) for s in sj]
        m_new = [jnp.maximum(ms[hh], jnp.max(sj[hh], axis=-1, keepdims=True)) for hh in heads]
        pj = [jnp.exp((sj[hh] - m_new[hh]).astype(BF16)) for hh in heads]
        accs = [jnp.exp(ms[hh] - m_new[hh]) * accs[hh] + _dot(pj[hh], vt) for hh in heads]
        return tuple(m_new), tuple(accs)

    m0 = tuple(jnp.full((NSA_TQ, 1), NEG_INF, F32) for _ in heads)
    a0 = tuple(jnp.zeros((NSA_TQ, LANES), F32) for _ in heads)
    carry = lax.fori_loop(0, n_tiles - 1, functools.partial(sel_tile, causal=False), (m0, a0))
    per_tile = SLC_KT // NSA_TQ
    last = [functools.partial(sel_tile, n_tiles - 1, causal=True, width=(r + 1) * NSA_TQ) for r in range(per_tile)]
    _, acc_s = lax.switch(qi % per_tile, last, carry)

    gt = jax.nn.sigmoid(gate_ref[0])
    outs = []
    for hh in heads:
        g0, g1, g2 = (gt[:, 3 * hh + br:3 * hh + br + 1] for br in range(3))
        scale_s = g1 / acc_s[hh][:, NSA_DH:NSA_DH + 1]
        scale_w = g2 / acc_w[hh][:, NSA_DH:NSA_DH + 1]
        outs.append(g0 * o_c[hh] + scale_s * acc_s[hh][:, :NSA_DH] + scale_w * acc_w[hh][:, :NSA_DH])
    o_ref[...] = jnp.concatenate(outs, axis=1).astype(o_ref.dtype)


def _nsa_attn(qk_hm, qpad, ksx, gates, cmp_kv, v1, cis, ltri):
    t = qk_hm.shape[1]
    nq = t // NSA_TQ
    nc = cmp_kv.shape[2]
    nb = cis.shape[1]
    ks_spec = pl.BlockSpec((1, t, 2 * LANES), lambda g, i: (g, 0, 0))
    kw_spec = pl.BlockSpec((1, t, NSA_DH), lambda g, i: (NSA_HEADS + NSA_G + g, 0, 0))
    vs_spec = pl.BlockSpec((1, t, LANES), lambda g, i: (g, 0, 0))
    vw_spec = pl.BlockSpec((1, t, LANES), lambda g, i: (NSA_G + g, 0, 0))
    return pl.pallas_call(
        _nsa_attn_kernel,
        out_shape=jax.ShapeDtypeStruct((t, NSA_W), BF16),
        grid=(NSA_G, nq),
        in_specs=[pl.BlockSpec((NSA_HG, NSA_TQ, NSA_DH), lambda g, i: (g, i, 0)),
                  pl.BlockSpec((NSA_HG, NSA_TQ, LANES), lambda g, i: (g, i, 0)),
                  pl.BlockSpec((1, NSA_TQ, LANES), lambda g, i: (g, i, 0)),
                  pl.BlockSpec((1, 1, nc, NSA_DH), lambda g, i: (0, g, 0, 0)),
                  pl.BlockSpec((1, 1, nc, NSA_DH), lambda g, i: (1, g, 0, 0)),
                  ks_spec, vs_spec, kw_spec, vw_spec,
                  pl.BlockSpec((nc, nb), lambda g, i: (0, 0)),
                  pl.BlockSpec((nb, nb), lambda g, i: (0, 0))],
        out_specs=pl.BlockSpec((NSA_TQ, NSA_HG * NSA_DH), lambda g, i: (i, g)),
        compiler_params=_cparams(("parallel", "arbitrary")),
        name="nsa_attention",
    )(qk_hm, qpad, gates, cmp_kv, cmp_kv, ksx, v1, qk_hm, v1, cis, ltri)


def _rope_tables(pos, dh, reps):
    half = dh // 2
    inv_freq = ROPE_THETA ** (-np.arange(half, dtype=np.float64) / half)
    ang = np.asarray(pos, np.float64)[:, None] * inv_freq[None, :]
    cos = np.cos(ang)
    sin = np.sin(ang)
    cos_t = np.tile(np.concatenate([cos, cos], axis=1), (1, reps))
    sin_t = np.tile(np.concatenate([-sin, sin], axis=1), (1, reps))
    return jnp.asarray(cos_t, F32), jnp.asarray(sin_t, F32)


def _pad_cols(a, width):
    return jnp.pad(a, ((0, 0), (0, width - a.shape[1])))


def _pad_rows(a, height):
    return jnp.pad(a, ((0, height - a.shape[0]), (0, 0)))


def _pack_w_in_kernel(w_ref, xv_ref, o_ref):
    o_ref[...] = _pack_w_in(w_ref[...], xv_ref[...]).astype(o_ref.dtype)


def _pack_w_in_call(w_in, layer, xv):
    _, d, n_in = w_in.shape
    tk = 256
    return pl.pallas_call(
        _pack_w_in_kernel,
        out_shape=jax.ShapeDtypeStruct((d, PROJ_W), BF16),
        grid=(d // tk,),
        in_specs=[pl.BlockSpec((None, tk, n_in), lambda i: (layer, i, 0)),
                  pl.BlockSpec((tk, LANES), lambda i: (i, 0))],
        out_specs=pl.BlockSpec((tk, PROJ_W), lambda i: (i, 0)),
        compiler_params=_cparams(("parallel",)),
        name="pack_w_in",
    )(w_in, xv)


def _pack_w_in(w_l, xv):
    d = w_l.shape[0]
    o = 0
    ret = w_l[:, o:o + 4 * RET_W]; o += 4 * RET_W
    rkv = w_l[:, o:o + 3 * RW_W]; o += 3 * RW_W
    xw = w_l[:, o:o + RW_DECAY_RANK]; o += RW_DECAY_RANK
    xa = w_l[:, o:o + RW_A_RANK]; o += RW_A_RANK
    xg = w_l[:, o:o + RW_GATE_RANK]; o += RW_GATE_RANK
    q = w_l[:, o:o + NSA_W]; o += NSA_W
    kc, vc, ks, vs, kw, vw = (w_l[:, o + i * NSA_KV_W:o + (i + 1) * NSA_KV_W] for i in range(6))
    o += 6 * NSA_KV_W
    gates = w_l[:, o:o + 3 * NSA_HEADS]
    return jnp.concatenate([ret, rkv, _pad_cols(xw, LANES), _pad_cols(xa, LANES), xg, xv,
                            q, ks, kw, kc, vc, vs, vw, _pad_cols(gates, LANES)], axis=1)


def _pack_mix(mix, vres_mix):
    o = 3 * RW_W
    xw = mix[o:o + RW_DECAY_RANK]; o += RW_DECAY_RANK
    xa = mix[o:o + RW_A_RANK]; o += RW_A_RANK
    xg = mix[o:o + RW_GATE_RANK]
    z = lambda n: jnp.zeros((n,), mix.dtype)
    xv = z(LANES) if vres_mix is None else jnp.concatenate([vres_mix, z(LANES - RW_V_RANK)])
    return jnp.concatenate([mix[:3 * RW_W], xw, z(LANES - RW_DECAY_RANK), xa, z(LANES - RW_A_RANK), xg, xv])[None, :]


def kernel(x, ln1_g, w_in, w_in_vres, rwkv_mix, rwkv_vres_mix, rwkv_w0, rwkv_w2, rwkv_a0, rwkv_a2, rwkv_v0, rwkv_v2, rwkv_g2, rwkv_k_k, rwkv_k_a, rwkv_r_k, rwkv_ln_g, rwkv_ln_b, nsa_q_norm, nsa_k_norm, nsa_cmp_pe, nsa_cmp_k_w1, nsa_cmp_k_w2, nsa_cmp_v_w1, nsa_cmp_v_w2, w_out, ln2_g, w_up, w_down):
    bsz, t, d = x.shape
    assert bsz == 1
    depth = w_in.shape[0]
    nc = t // CMP_STRIDE
    nb = t // SLC_BLOCK
    xs = x.reshape(t, d)

    pos = np.arange(t)
    cos_r, sin_r = _rope_tables(pos, RET_DH, 1)
    cos_n, sin_n = _rope_tables(pos, NSA_DH, 2)
    cos_c, sin_c = _rope_tables(np.arange(nc) * CMP_STRIDE + (CMP_BLOCK - 1), NSA_DH, 1)
    lane_head = np.arange(RW_W) // RW_N
    gsum = jnp.asarray(lane_head[:, None] == lane_head[None, :], BF16)
    lane_h2 = np.arange(LANES) // NSA_DH
    gmean = jnp.asarray((lane_h2[:, None] == lane_h2[None, :]) / float(NSA_DH), BF16)
    cstart = np.arange(nc) * CMP_STRIDE
    sstart = np.arange(nb) * SLC_BLOCK
    cis = jnp.asarray((cstart[:, None] <= sstart[None, :] + SLC_BLOCK - 1)
                      & (cstart[:, None] + CMP_BLOCK - 1 >= sstart[None, :]), BF16)

    ltri = jnp.asarray(np.tril(np.ones((nb, nb), np.float32), -1), BF16)

    v_first = jnp.zeros((t, RW_W), F32)
    for l in range(depth):
        vres = l > 0
        xv_w = _pad_cols(w_in_vres[l - 1], LANES) if vres else jnp.zeros((d, LANES), F32)
        w_cat = _pack_w_in_call(w_in, l, xv_w)
        proj = _norm_matmul(xs, ln1_g[l], w_cat, name="in_proj")

        o_ret = _retention(proj, cos_r, sin_r)

        row = lambda a: a.reshape(1, -1)
        mix = _pack_mix(rwkv_mix[l], rwkv_vres_mix[l - 1] if vres else None)
        v0 = row(rwkv_v0[l - 1]) if vres else jnp.zeros((1, RW_W), F32)
        v2 = _pad_rows(rwkv_v2[l - 1], LANES) if vres else jnp.zeros((LANES, RW_W), F32)
        r_, lw_, k_, v_, an_, b_, g_ = _rwkv_pre(
            proj, mix, row(rwkv_w0[l]), _pad_rows(rwkv_w2[l], LANES), row(rwkv_a0[l]), _pad_rows(rwkv_a2[l], LANES),
            v0, v2, rwkv_g2[l], row(rwkv_k_k[l]), row(rwkv_k_a[l]), gsum, v_first, vres)
        if not vres:
            v_first = v_
        o_rwkv = _wkv(r_, lw_, k_, v_, an_, b_, g_, row(rwkv_ln_g[l]), row(rwkv_ln_b[l]), row(rwkv_r_k[l]), gsum)

        n_q = NSA_W // LANES
        n_k = NSA_KV_W // LANES
        w_slabs = jnp.stack([jnp.tile(nsa_q_norm[l] * (NSA_DH ** -0.5), 2)] * n_q
                            + [jnp.tile(nsa_k_norm[l, 1], 2)] * n_k + [jnp.tile(nsa_k_norm[l, 2], 2)] * n_k)[:, None, :]
        s_max = 1.02 * jnp.max(jnp.abs(nsa_q_norm[l])) * jnp.max(jnp.abs(nsa_k_norm[l, 1])) * (NSA_DH ** 0.5)
        big = jnp.full((1, LANES), jnp.exp2(jnp.ceil(jnp.log2(2.0 * s_max + 128.0))), F32)
        qk_hm, hmat, v1, gates, qpad, ksx = _nsa_prep(proj, w_slabs, cos_n, sin_n, gmean, big)
        w1 = jnp.stack([nsa_cmp_k_w1[l], nsa_cmp_v_w1[l]]).astype(BF16)
        w2 = jnp.stack([nsa_cmp_k_w2[l], nsa_cmp_v_w2[l]]).astype(BF16)
        pe = jnp.broadcast_to(nsa_cmp_pe[l].reshape(2, 1, CMP_BLOCK * NSA_DH), (2, 8, CMP_BLOCK * NSA_DH)).astype(BF16)
        cmp_kv = _compress(hmat, w1, w2, pe, row(nsa_k_norm[l, 0]), cos_c, sin_c)
        o_nsa = _nsa_attn(qk_hm, qpad, ksx, gates, cmp_kv, v1, cis, ltri)

        xs = _out_proj(o_ret, o_rwkv, o_nsa, w_out, l, xs)

        up = _norm_matmul(xs, ln2_g[l], w_up, layer=l, act="relu2", out_dtype=BF16, name="mlp_up")
        xs = _matmul(up, w_down, layer=l, res=xs, name="mlp_down")
    return xs.reshape(bsz, t, d)
```

```python
import functools

import numpy as np
import jax
import jax.numpy as jnp
from jax import lax
from jax.experimental import pallas as pl
from jax.experimental.pallas import tpu as pltpu

F32 = jnp.float32
BF16 = jnp.bfloat16

D_MODEL = 2048
RET_HEADS, RET_DH = 4, 128
RET_W = RET_HEADS * RET_DH
RW_HEADS, RW_N = 8, 64
RW_W = RW_HEADS * RW_N
RW_DECAY_RANK, RW_A_RANK, RW_V_RANK, RW_GATE_RANK = 96, 96, 64, 256
RW_GN_EPS = 64e-5
NSA_HEADS, NSA_G, NSA_DH = 16, 4, 64
NSA_HG = NSA_HEADS // NSA_G
NSA_W = NSA_HEADS * NSA_DH
NSA_KV_W = NSA_G * NSA_DH
CMP_BLOCK, CMP_STRIDE, CMP_HIDDEN = 32, 16, 256
SLC_BLOCK, SLC_TOPK, WINDOW = 64, 16, 512
NSA_TQ = 512
ROPE_THETA = 10000.0
MLP_HIDDEN = 4 * D_MODEL
RMS_EPS = 1e-6
NEG_INF = -1e30
FORCED_SCORE = 1e9

LANES = 128
VMEM_LIMIT = 56 * 1024 * 1024

C_RET = 0
C_RW_RKV = 2048
C_RW_XW = 3584
C_RW_XA = 3712
C_RW_XG = 3840
C_RW_XV = 4096
C_NSA_Q = 4224
C_NSA_KS = 5248
C_NSA_KW = 5504
C_NSA_KC = 5760
C_NSA_VC = 6016
C_NSA_VS = 6272
C_NSA_VW = 6528
C_NSA_GATE = 6784
PROJ_W = 6912


def _cparams(sem):
    return pltpu.CompilerParams(dimension_semantics=sem, vmem_limit_bytes=VMEM_LIMIT)


def _dot(a, b):
    return jnp.dot(a, b, preferred_element_type=F32)


def _dot_t(a, b):
    return lax.dot_general(a, b, (((1,), (1,)), ((), ())), preferred_element_type=F32)


def _dot_tl(a, b):
    return lax.dot_general(a, b, (((0,), (0,)), ((), ())), preferred_element_type=F32)


def _split_dot(x, m_bf16, passes):
    hi = x.astype(BF16)
    acc = _dot(hi, m_bf16)
    rem = x - hi.astype(F32)
    for _ in range(passes - 1):
        piece = rem.astype(BF16)
        acc = acc + _dot(piece, m_bf16)
        rem = rem - piece.astype(F32)
    return acc


def _mm_kernel(*refs, nk, act, has_res):
    a_ref, b_ref = refs[:2]
    r_ref = refs[2] if has_res else None
    o_ref = refs[3] if has_res else refs[2]

    def finish(y):
        if act == "relu2":
            y = jnp.square(jnp.maximum(y, 0.0))
        if has_res:
            y = y + r_ref[...]
        o_ref[...] = y.astype(o_ref.dtype)

    part = _dot(a_ref[...], b_ref[...].astype(BF16))
    if nk == 1:
        finish(part)
        return
    acc_ref = refs[-1]
    k = pl.program_id(2)

    @pl.when(k == 0)
    def _():
        acc_ref[...] = part

    @pl.when(k > 0)
    def _():
        acc_ref[...] += part

    @pl.when(k == nk - 1)
    def _():
        finish(acc_ref[...])


def _pick(n, cands):
    for c in cands:
        if n % c == 0:
            return c
    raise ValueError(f"no tile for {n}")


def _weight_spec(b, layer, block, index_map):
    if layer is None:
        return pl.BlockSpec(block, index_map)
    return pl.BlockSpec((None,) + block, lambda *idx: (layer,) + index_map(*idx))


def _matmul(a, b, *, layer=None, res=None, act=None, out_dtype=F32, name="matmul"):
    m, kd = a.shape
    n = b.shape[-1]
    tm = _pick(m, (1024, 512, 256, 128))
    tn = _pick(n, (1024, 768, 512, 256, 128))
    tk = _pick(kd, (2048, 1024, 512))
    nk = kd // tk
    in_specs = [pl.BlockSpec((tm, tk), lambda i, j, k: (i, k)),
                _weight_spec(b, layer, (tk, tn), lambda i, j, k: (k, j))]
    args = [a, b]
    if res is not None:
        in_specs.append(pl.BlockSpec((tm, tn), lambda i, j, k: (i, j)))
        args.append(res)
    return pl.pallas_call(
        functools.partial(_mm_kernel, nk=nk, act=act, has_res=res is not None),
        out_shape=jax.ShapeDtypeStruct((m, n), out_dtype),
        grid=(m // tm, n // tn, nk),
        in_specs=in_specs,
        out_specs=pl.BlockSpec((tm, tn), lambda i, j, k: (i, j)),
        scratch_shapes=[pltpu.VMEM((tm, tn), F32)] if nk > 1 else [],
        compiler_params=_cparams(("parallel", "parallel", "arbitrary")),
        name=name,
    )(*args)


def _norm_mm_kernel(x_ref, g_ref, b_ref, o_ref, h_ref, *, act):
    @pl.when(pl.program_id(1) == 0)
    def _():
        x = x_ref[...]
        ms = jnp.mean(x * x, axis=-1, keepdims=True)
        h_ref[...] = (x * lax.rsqrt(ms + RMS_EPS) * g_ref[...]).astype(BF16)

    y = _dot(h_ref[...], b_ref[...].astype(BF16))
    if act == "relu2":
        y = jnp.square(jnp.maximum(y, 0.0))
    o_ref[...] = y.astype(o_ref.dtype)


def _norm_matmul(x, g, b, *, layer=None, act=None, out_dtype=F32, name="norm_matmul"):
    m, kd = x.shape
    n = b.shape[-1]
    tm = _pick(m, (1024, 512, 256, 128))
    tn = _pick(n, (1024, 768, 512, 256, 128))
    return pl.pallas_call(
        functools.partial(_norm_mm_kernel, act=act),
        out_shape=jax.ShapeDtypeStruct((m, n), out_dtype),
        grid=(m // tm, n // tn),
        in_specs=[pl.BlockSpec((tm, kd), lambda i, j: (i, 0)),
                  pl.BlockSpec((1, kd), lambda i, j: (0, 0)),
                  _weight_spec(b, layer, (kd, tn), lambda i, j: (0, j))],
        out_specs=pl.BlockSpec((tm, tn), lambda i, j: (i, j)),
        scratch_shapes=[pltpu.VMEM((tm, kd), BF16)],
        compiler_params=_cparams(("parallel", "arbitrary")),
        name=name,
    )(x, g.reshape(1, kd), b)


def _out_proj_kernel(ret_ref, rw_ref, nsa_ref, w_ref, res_ref, o_ref, wb_ref):
    @pl.when(pl.program_id(1) == 0)
    def _():
        wb_ref[...] = w_ref[...].astype(BF16)

    k0 = ret_ref.shape[1]
    k1 = k0 + rw_ref.shape[1]
    acc = _dot(ret_ref[...], wb_ref[:k0, :])
    acc = acc + _dot(rw_ref[...], wb_ref[k0:k1, :])
    acc = acc + _dot(nsa_ref[...], wb_ref[k1:, :])
    o_ref[...] = acc + res_ref[...]


def _out_proj(o_ret, o_rwkv, o_nsa, w, layer, res):
    m = o_ret.shape[0]
    kd, n = w.shape[-2:]
    assert kd == o_ret.shape[1] + o_rwkv.shape[1] + o_nsa.shape[1]
    tm = _pick(m, (1024, 512, 256, 128))
    tn = _pick(n, (1024, 512, 256, 128))

    def rows(a):
        return pl.BlockSpec((tm, a.shape[1]), lambda j, i: (i, 0))

    return pl.pallas_call(
        _out_proj_kernel,
        out_shape=jax.ShapeDtypeStruct((m, n), F32),
        grid=(n // tn, m // tm),
        in_specs=[rows(o_ret), rows(o_rwkv), rows(o_nsa),
                  _weight_spec(w, layer, (kd, tn), lambda j, i: (0, j)),
                  pl.BlockSpec((tm, tn), lambda j, i: (i, j))],
        out_specs=pl.BlockSpec((tm, tn), lambda j, i: (i, j)),
        scratch_shapes=[pltpu.VMEM((kd, tn), BF16)],
        compiler_params=_cparams(("parallel", "arbitrary")),
        name="out_proj",
    )(o_ret, o_rwkv, o_nsa, w, res)


RET_CHUNK = 256


def _retention_kernel(q_ref, k_ref, v_ref, g_ref, cos_ref, sin_ref, dmat_ref, qd_ref, kd_ref, cd_ref, o_ref, state_ref):
    @pl.when(pl.program_id(0) == 0)
    def _():
        state_ref[...] = jnp.zeros_like(state_ref)

    cos = cos_ref[...]
    sin = sin_ref[...]

    def rope(x):
        return x * cos + pltpu.roll(x, RET_DH // 2, axis=1) * sin

    heads = range(RET_HEADS)
    hs = [slice(h * RET_DH, (h + 1) * RET_DH) for h in heads]
    q_all, k_all, v_all, gate = q_ref[...], k_ref[...], v_ref[...], g_ref[...]
    qb = [rope(q_all[:, s]).astype(BF16) for s in hs]
    kf = [rope(k_all[:, s]) * (RET_DH ** -0.5) for s in hs]
    kb = [kf[h].astype(BF16) for h in heads]
    vb = [v_all[:, s].astype(BF16) for s in hs]
    inner = [(_dot_t(qb[h], kb[h]) * dmat_ref[h]).astype(BF16) for h in heads]
    cross = [_dot(qb[h], state_ref[h].astype(BF16)) * qd_ref[h] for h in heads]
    kv = [_dot_tl((kf[h] * kd_ref[h]).astype(BF16), vb[h]) for h in heads]
    outs = []
    for h in heads:
        out = _dot(inner[h], vb[h]) + cross[h]
        state_ref[h] = cd_ref[h] * state_ref[h] + kv[h]
        outs.append(out * lax.rsqrt(jnp.mean(out * out, axis=-1, keepdims=True) + RMS_EPS))
    o_ref[...] = (gate * jax.nn.sigmoid(gate) * jnp.concatenate(outs, axis=1)).astype(o_ref.dtype)


def _retention(proj, cos, sin):
    t = proj.shape[0]
    chunk = RET_CHUNK
    log_gamma = np.log(1.0 - 2.0 ** (-5.0 - np.arange(RET_HEADS, dtype=np.float64)))[:, None, None]
    n = np.arange(chunk, dtype=np.float64)
    lag = n[:, None] - n[None, :]
    dmat = np.where(lag >= 0, np.exp(np.maximum(lag, 0.0)[None] * log_gamma), 0.0)
    ones = np.ones((1, 1, RET_DH))
    qd = np.exp((n + 1.0)[None, :, None] * log_gamma) * ones
    kd = np.exp((chunk - 1.0 - n)[None, :, None] * log_gamma) * ones
    cd = np.exp(chunk * log_gamma) * ones
    tables = [jnp.asarray(a, F32) for a in (dmat, qd, kd, cd)]
    base = C_RET // RET_W

    def col(off):
        return pl.BlockSpec((chunk, RET_W), lambda c, off=off: (c, base + off))

    def full(a):
        return pl.BlockSpec(a.shape, lambda c: (0, 0, 0))

    tab = pl.BlockSpec((chunk, RET_DH), lambda c: (c, 0))
    return pl.pallas_call(
        _retention_kernel,
        out_shape=jax.ShapeDtypeStruct((t, RET_W), BF16),
        grid=(t // chunk,),
        in_specs=[col(0), col(1), col(2), col(3), tab, tab] + [full(a) for a in tables],
        out_specs=pl.BlockSpec((chunk, RET_W), lambda c: (c, 0)),
        scratch_shapes=[pltpu.VMEM((RET_HEADS, RET_DH, RET_DH), F32)],
        compiler_params=_cparams(("arbitrary",)),
        name="retention",
    )(proj, proj, proj, proj, cos, sin, *tables)


def _dot_f32(a, b):
    a_hi = a.astype(BF16)
    b_hi = b.astype(BF16)
    a_lo = (a - a_hi.astype(F32)).astype(BF16)
    b_lo = (b - b_hi.astype(F32)).astype(BF16)
    return _dot(a_hi, b_hi) + (_dot(a_hi, b_lo) + _dot(a_lo, b_hi))


RW_PRE_TM = 256
_MIX_OFF = (0, 512, 1024, 1536, 1664, 1792, 2048, 2176)


def _rwkv_pre_kernel(r_ref, k_ref, v_ref, xw_ref, xa_ref, xg_ref, xv_ref,
                     rp_ref, kp_ref, vp_ref, xwp_ref, xap_ref, xgp_ref, xvp_ref,
                     mix_ref, w0_ref, w2_ref, a0_ref, a2_ref, v0_ref, v2_ref, g2_ref, kk_ref, ka_ref,
                     gsum_ref, vfirst_ref,
                     r_out, lw_out, k_out, v_out, an_out, b_out, g_out, *, use_vres):
    i = pl.program_id(0)

    def mixed(cur_ref, prev_ref, seg):
        cur = cur_ref[...]
        mix = mix_ref[:, _MIX_OFF[seg]:_MIX_OFF[seg + 1]]
        prev_row = jnp.where(i > 0, prev_ref[7:8, :], 0.0)
        rid = lax.broadcasted_iota(jnp.int32, cur.shape, 0)
        shifted = jnp.where(rid == 0, prev_row, pltpu.roll(cur, 1, axis=0))
        return cur + mix * (shifted - cur)

    r = mixed(r_ref, rp_ref, 0)
    k = mixed(k_ref, kp_ref, 1)
    v = mixed(v_ref, vp_ref, 2)
    xw = mixed(xw_ref, xwp_ref, 3)
    xa = mixed(xa_ref, xap_ref, 4)
    xg = mixed(xg_ref, xgp_ref, 5)

    z = -(w0_ref[...] + _dot_f32(jnp.tanh(xw), w2_ref[...]))
    softplus = jnp.maximum(z, 0.0) + jnp.log(1.0 + jnp.exp(-jnp.abs(z)))
    w = -softplus - 0.5
    lw_out[...] = -jnp.exp(w)
    a = jax.nn.sigmoid(a0_ref[...] + _dot_f32(xa, a2_ref[...]))
    g_out[...] = _dot_f32(jax.nn.sigmoid(xg), g2_ref[...])
    if use_vres:
        xv = mixed(xv_ref, xvp_ref, 6)
        v = v + (vfirst_ref[...] - v) * jax.nn.sigmoid(v0_ref[...] + _dot_f32(xv, v2_ref[...]))
    kk = k * kk_ref[...]
    ss = _split_dot(kk * kk, gsum_ref[...], 3)
    kk = kk / jnp.maximum(jnp.sqrt(ss), 1e-12)
    r_out[...] = r
    k_out[...] = k * (1.0 + (a - 1.0) * ka_ref[...])
    v_out[...] = v
    an_out[...] = -kk
    b_out[...] = kk * a


def _rwkv_pre(proj, mix, w0, w2, a0, a2, v0, v2, g2, k_k, k_a, gsum, v_first, use_vres):
    t = proj.shape[0]
    tm = RW_PRE_TM

    def cur(width, off):
        return pl.BlockSpec((tm, width), lambda i: (i, off // width))

    def prev(width, off):
        return pl.BlockSpec((8, width), lambda i: (jnp.maximum(i * (tm // 8) - 1, 0), off // width))

    def full(a):
        return pl.BlockSpec(a.shape, lambda i: (0,) * a.ndim)

    segs = [(RW_W, C_RW_RKV), (RW_W, C_RW_RKV + RW_W), (RW_W, C_RW_RKV + 2 * RW_W),
            (128, C_RW_XW), (128, C_RW_XA), (256, C_RW_XG), (128, C_RW_XV)]
    params = [mix, w0, w2, a0, a2, v0, v2, g2, k_k, k_a, gsum]
    row = pl.BlockSpec((tm, RW_W), lambda i: (i, 0))
    outs = pl.pallas_call(
        functools.partial(_rwkv_pre_kernel, use_vres=use_vres),
        out_shape=[jax.ShapeDtypeStruct((t, RW_W), F32)] * 7,
        grid=(t // tm,),
        in_specs=[cur(w, o) for w, o in segs] + [prev(w, o) for w, o in segs] + [full(p) for p in params] + [row],
        out_specs=[row] * 7,
        compiler_params=_cparams(("parallel",)),
        name="rwkv_pre",
    )(*([proj] * 14), *params, v_first)
    return outs


RW_CHUNK = 128


def _wkv_kernel(r_ref, lw_ref, k_ref, v_ref, an_ref, b_ref, g_ref, lng_ref, lnb_ref, rk_ref,
                gsum_ref, tril_ref, o_ref, s_ref):
    c = pl.program_id(0)
    C = RW_CHUNK
    N = RW_N

    @pl.when(c == 0)
    def _():
        s_ref[...] = jnp.zeros_like(s_ref)

    r = r_ref[...]
    lw = lw_ref[...]
    k = k_ref[...]
    v = v_ref[...]
    lw_hi = lw.astype(BF16)
    rem = lw - lw_hi.astype(F32)
    lw_mid = rem.astype(BF16)
    lw_lo = (rem - lw_mid.astype(F32)).astype(BF16)
    tril = tril_ref[...]
    lg = _dot(tril, lw_hi) + (_dot(tril, lw_mid) + _dot(tril, lw_lo))
    gam = jnp.exp(lg)
    ginv = jnp.exp(-lg)
    at = (an_ref[...] * jnp.exp(lg - lw)).astype(BF16)
    bt = (b_ref[...] * ginv).astype(BF16)
    kt = (k * ginv).astype(BF16)
    rt_f = r * gam
    rt = rt_f.astype(BF16)
    vb = v.astype(BF16)
    g_last = gam[C - 1:C, :]

    rowi = lax.broadcasted_iota(jnp.int32, (C, C), 0)
    coli = lax.broadcasted_iota(jnp.int32, (C, C), 1)
    strict = rowi > coli
    incl = rowi >= coli

    heads = range(RW_HEADS)
    hs = [slice(h * N, (h + 1) * N) for h in heads]
    bth = [bt[:, s] for s in hs]
    kth = [kt[:, s] for s in hs]
    vh = [vb[:, s] for s in hs]
    big = [_dot_t(jnp.concatenate([at[:, s], rt[:, s]], axis=0), jnp.concatenate([bth[h], kth[h]], axis=0))
           for h, s in zip(heads, hs)]
    a_ab = [jnp.where(strict, big[h][:C, :C], 0.0) for h in heads]
    a_ak = [jnp.where(strict, big[h][:C, C:], 0.0).astype(BF16) for h in heads]
    a_rb = [jnp.where(incl, big[h][C:, :C], 0.0).astype(BF16) for h in heads]
    a_rk = [jnp.where(incl, big[h][C:, C:], 0.0).astype(BF16) for h in heads]
    akv = [_dot(a_ak[h], vh[h]) for h in heads]
    def same_block(size):
        shift = size.bit_length() - 1
        return (rowi >> shift) == (coli >> shift)

    tinv = [jnp.where(same_block(2), a_ab[h], 0.0) + jnp.where(rowi == coli, 1.0, 0.0) for h in heads]
    size = 2
    while size < C:
        lower_left = same_block(2 * size) & jnp.logical_not(same_block(size))
        off = [jnp.where(lower_left, a_ab[h], 0.0).astype(BF16) for h in heads]
        tb = [tinv[h].astype(BF16) for h in heads]
        half = [_dot(tb[h], off[h]).astype(BF16) for h in heads]
        tinv = [tinv[h] + _dot(half[h], tb[h]) for h in heads]
        size *= 2
    xb = [_dot(tinv[h].astype(BF16), jnp.concatenate([at[:, hs[h]], akv[h].astype(BF16)], axis=1)).astype(BF16)
          for h in heads]
    yx = [_dot(a_rb[h], xb[h]) for h in heads]
    ykv = [_dot(a_rk[h], vh[h]) for h in heads]
    xtb = [_dot_tl(xb[h], bth[h]) for h in heads]
    vtk = [_dot_tl(vh[h], kth[h]) for h in heads]
    ys = []
    for h in heads:
        gl = g_last[:, hs[h]]
        y1 = (rt_f[:, hs[h]] + yx[h][:, :N]).astype(BF16)
        s0 = s_ref[h]
        s0b = s0.astype(BF16)
        ys.append(_dot_t(y1, s0b) + (yx[h][:, N:] + ykv[h]))
        s_ref[h] = (s0 + _dot(s0b, xtb[h][:N].astype(BF16)) + (xtb[h][N:] + vtk[h])) * gl
    y = jnp.concatenate(ys, axis=1)

    gsum = gsum_ref[...]
    inv_n = 1.0 / N
    mu = _split_dot(y, gsum, 3) * inv_n
    yc = y - mu
    var = _split_dot(yc * yc, gsum, 3) * inv_n
    yn = yc * lax.rsqrt(var + RW_GN_EPS) * lng_ref[...] + lnb_ref[...]
    bonus = _split_dot(r * k * rk_ref[...], gsum, 3)
    o_ref[...] = ((yn + bonus * v) * g_ref[...]).astype(o_ref.dtype)


def _wkv(r, lw, k, v, an, b, g, ln_g, ln_b, r_k, gsum):
    t = r.shape[0]
    C = RW_CHUNK
    tril = jnp.asarray(np.tril(np.ones((C, C), np.float32)), BF16)
    row = pl.BlockSpec((C, RW_W), lambda c: (c, 0))
    vec = pl.BlockSpec((1, RW_W), lambda c: (0, 0))
    return pl.pallas_call(
        _wkv_kernel,
        out_shape=jax.ShapeDtypeStruct((t, RW_W), BF16),
        grid=(t // C,),
        in_specs=[row] * 7 + [vec] * 3 + [pl.BlockSpec((RW_W, RW_W), lambda c: (0, 0)),
                                          pl.BlockSpec((C, C), lambda c: (0, 0))],
        out_specs=row,
        scratch_shapes=[pltpu.VMEM((RW_HEADS, RW_N, RW_N), F32)],
        compiler_params=_cparams(("arbitrary",)),
        name="wkv7",
    )(r, lw, k, v, an, b, g, ln_g, ln_b, r_k, gsum, tril)


NSA_PREP_TM = 512


N_NORM_SLABS = (NSA_W + 2 * NSA_KV_W) // LANES
N_RAW_SLABS = 4 * NSA_KV_W // LANES
N_NSA_SLABS = N_NORM_SLABS + N_RAW_SLABS + 1


def _nsa_prep_kernel(*refs):
    x_refs = refs[:N_NSA_SLABS]
    (cos_ref, sin_ref, w_ref, gmean_ref, big_ref,
     qk_ref, hmat_ref, v1_ref, gate_ref, qpad_ref, ksx_ref) = refs[N_NSA_SLABS:]
    cos = cos_ref[...]
    sin = sin_ref[...]
    gmean = gmean_ref[...]
    half = NSA_DH // 2
    tm = cos.shape[0]
    lane = lax.broadcasted_iota(jnp.int32, cos.shape, 1)
    first_half = (lane % NSA_DH) < half
    low = lane < NSA_DH
    blk_row = (pl.program_id(0) * tm + lax.broadcasted_iota(jnp.int32, cos.shape, 0)) // SLC_BLOCK
    big = big_ref[...]
    mark_lo = jnp.where(blk_row == lane - NSA_DH, big, 0.0)
    mark_hi = jnp.where(low & (blk_row == lane + NSA_DH), big, 0.0).astype(BF16)
    n_q = NSA_W // LANES
    n_k = NSA_KV_W // LANES
    for s in range(N_NORM_SLABS):
        x = x_refs[s][...]
        y = x * lax.rsqrt(_split_dot(x * x, gmean, 3) + RMS_EPS) * w_ref[s]
        rot = jnp.where(first_half, pltpu.roll(y, LANES - half, axis=1), pltpu.roll(y, half, axis=1))
        res32 = y * cos + rot * sin
        res = res32.astype(qk_ref.dtype)
        qk_ref[2 * s] = res[:, :NSA_DH]
        qk_ref[2 * s + 1] = res[:, NSA_DH:]
        if s >= n_q + n_k:
            continue
        swapped = pltpu.roll(res32, NSA_DH, axis=1)
        if s < n_q:
            qpad_ref[2 * s] = jnp.where(low, res32, 0.0).astype(BF16)
            qpad_ref[2 * s + 1] = jnp.where(low, swapped, 0.0).astype(BF16)
        else:
            g0 = 2 * (s - n_q)
            ksx_ref[g0, :, :LANES] = jnp.where(low, res32, mark_lo).astype(BF16)
            ksx_ref[g0 + 1, :, :LANES] = jnp.where(low, swapped, mark_lo).astype(BF16)
            ksx_ref[g0, :, LANES:] = mark_hi
            ksx_ref[g0 + 1, :, LANES:] = mark_hi

    rows = hmat_ref.shape[2]
    low_r = lax.broadcasted_iota(jnp.int32, (rows, LANES), 1) < NSA_DH
    for s in range(NSA_KV_W // LANES * 2):
        kind, pair = divmod(s, NSA_KV_W // LANES)
        x_ref = x_refs[N_NORM_SLABS + s]
        for t2 in range(CMP_STRIDE // 2):
            r0 = x_ref[pl.ds(2 * t2, rows, stride=CMP_STRIDE), :]
            r1 = x_ref[pl.ds(2 * t2 + 1, rows, stride=CMP_STRIDE), :]
            cols = slice(t2 * LANES, (t2 + 1) * LANES)
            hmat_ref[kind, 2 * pair, :, cols] = jnp.where(low_r, r0, pltpu.roll(r1, NSA_DH, axis=1)).astype(BF16)
            hmat_ref[kind, 2 * pair + 1, :, cols] = jnp.where(low_r, pltpu.roll(r0, NSA_DH, axis=1), r1).astype(BF16)

    one_col = jnp.where(lane == NSA_DH, 1.0, 0.0)
    for s in range(NSA_KV_W // LANES * 2):
        x = x_refs[N_NORM_SLABS + N_RAW_SLABS // 2 + s][...]
        v1_ref[2 * s] = jnp.where(low, x, one_col).astype(BF16)
        v1_ref[2 * s + 1] = jnp.where(low, pltpu.roll(x, NSA_DH, axis=1), one_col).astype(BF16)

    xg = x_refs[N_NSA_SLABS - 1][...]
    per_group = 3 * NSA_HG
    for g in range(NSA_G):
        shifted = xg if g == 0 else pltpu.roll(xg, LANES - per_group * g, axis=1)
        gate_ref[g] = jnp.where(lane < per_group, shifted, 0.0)


def _nsa_prep(proj, w_slabs, cos, sin, gmean, big):
    t = proj.shape[0]
    assert t // SLC_BLOCK <= LANES
    tm = NSA_PREP_TM
    base = C_NSA_Q // LANES
    tab = pl.BlockSpec((tm, LANES), lambda i: (i, 0))
    n_heads_out = 2 * N_NORM_SLABS
    return pl.pallas_call(
        _nsa_prep_kernel,
        out_shape=[jax.ShapeDtypeStruct((n_heads_out, t, NSA_DH), BF16),
                   jax.ShapeDtypeStruct((2, NSA_G, t // CMP_STRIDE, CMP_STRIDE * NSA_DH), BF16),
                   jax.ShapeDtypeStruct((2 * NSA_G, t, LANES), BF16),
                   jax.ShapeDtypeStruct((NSA_G, t, LANES), F32),
                   jax.ShapeDtypeStruct((NSA_HEADS, t, LANES), BF16),
                   jax.ShapeDtypeStruct((NSA_G, t, 2 * LANES), BF16)],
        grid=(t // tm,),
        in_specs=[pl.BlockSpec((tm, LANES), lambda i, s=s: (i, base + s)) for s in range(N_NSA_SLABS)]
        + [tab, tab, pl.BlockSpec((N_NORM_SLABS, 1, LANES), lambda i: (0, 0, 0)),
           pl.BlockSpec((LANES, LANES), lambda i: (0, 0)), pl.BlockSpec((1, LANES), lambda i: (0, 0))],
        out_specs=[pl.BlockSpec((n_heads_out, tm, NSA_DH), lambda i: (0, i, 0)),
                   pl.BlockSpec((2, NSA_G, tm // CMP_STRIDE, CMP_STRIDE * NSA_DH), lambda i: (0, 0, i, 0)),
                   pl.BlockSpec((2 * NSA_G, tm, LANES), lambda i: (0, i, 0)),
                   pl.BlockSpec((NSA_G, tm, LANES), lambda i: (0, i, 0)),
                   pl.BlockSpec((NSA_HEADS, tm, LANES), lambda i: (0, i, 0)),
                   pl.BlockSpec((NSA_G, tm, 2 * LANES), lambda i: (0, i, 0))],
        compiler_params=_cparams(("parallel",)),
        name="nsa_prep",
    )(*([proj] * N_NSA_SLABS), cos, sin, w_slabs, gmean, big)


def _compress_kernel(h_ref, w1_ref, w2_ref, pe_ref, nw_ref, cos_ref, sin_ref, o_ref):
    kind = pl.program_id(0)
    nc = h_ref.shape[2]
    half_in = CMP_STRIDE * NSA_DH
    hm = h_ref[0, 0]
    w1 = w1_ref[0]
    first = _dot(hm, w1[:half_in])
    second = _dot(hm, w1[half_in:])
    const = _dot(pe_ref[0], w1)[0:1, :]
    pre = first + pltpu.roll(second, nc - 1, axis=0) + const
    hid = 0.5 * pre * (1.0 + jnp.tanh(0.7978845608028654 * (pre + 0.044715 * pre * pre * pre)))
    out = _dot(hid.astype(BF16), w2_ref[0])
    y = out * lax.rsqrt(jnp.mean(out * out, axis=-1, keepdims=True) + RMS_EPS) * nw_ref[...]
    half = NSA_DH // 2
    rot = jnp.concatenate([y[:, half:], y[:, :half]], axis=1)
    roped = y * cos_ref[...] + rot * sin_ref[...]
    o_ref[0, 0] = jnp.where(kind == 0, roped, out).astype(o_ref.dtype)


def _compress(hmat, w1, w2, pe, nw, cos_c, sin_c):
    _, g, nc, width = hmat.shape
    return pl.pallas_call(
        _compress_kernel,
        out_shape=jax.ShapeDtypeStruct((2, g, nc, NSA_DH), BF16),
        grid=(2, g),
        in_specs=[pl.BlockSpec((1, 1, nc, width), lambda a, b: (a, b, 0, 0)),
                  pl.BlockSpec((1, 2 * width, CMP_HIDDEN), lambda a, b: (a, 0, 0)),
                  pl.BlockSpec((1, CMP_HIDDEN, NSA_DH), lambda a, b: (a, 0, 0)),
                  pl.BlockSpec((1, 8, 2 * width), lambda a, b: (a, 0, 0)),
                  pl.BlockSpec((1, NSA_DH), lambda a, b: (0, 0)),
                  pl.BlockSpec((nc, NSA_DH), lambda a, b: (0, 0)),
                  pl.BlockSpec((nc, NSA_DH), lambda a, b: (0, 0))],
        out_specs=pl.BlockSpec((1, 1, nc, NSA_DH), lambda a, b: (a, b, 0, 0)),
        compiler_params=_cparams(("parallel", "parallel")),
        name="nsa_compress",
    )(hmat, w1, w2, pe, nw, cos_c, sin_c)


SLC_KT = 1024
WIN_KEYS = WINDOW + NSA_TQ


INT32_MIN = -2 ** 31


def _nsa_attn_kernel(q_ref, qpad_ref, gate_ref, kc_ref, vc_ref, ks_ref, vs_ref, kw_ref, vw_ref, cis_ref, ltri_ref,
                     o_ref):
    qi = pl.program_id(1)
    nc = kc_ref.shape[2]
    nb = cis_ref.shape[1]
    heads = range(NSA_HG)
    start = qi * NSA_TQ
    qh = [q_ref[hh] for hh in heads]
    tok = start + lax.broadcasted_iota(jnp.int32, (NSA_TQ, 1), 0)

    def softmax_terms(scores):
        return [jnp.exp(s - jnp.max(s, axis=-1, keepdims=True)) for s in scores]

    def cmp_branch(width):
        kc = kc_ref[0, 0, :width, :]
        vc = vc_ref[0, 0, :width, :]
        cend = lax.broadcasted_iota(jnp.int32, (1, width), 1) * CMP_STRIDE + (CMP_BLOCK - 1)
        bias_c = jnp.where(cend <= tok, 0.0, NEG_INF)
        ec = softmax_terms([_dot_t(qh[hh], kc) + bias_c for hh in heads])
        sees_any = tok >= CMP_BLOCK - 1
        pc = [ec[hh] * jnp.where(sees_any, 1.0 / jnp.sum(ec[hh], axis=-1, keepdims=True), 0.0) for hh in heads]
        outs_c = tuple(_dot(pc[hh].astype(BF16), vc) for hh in heads)
        psum = pc[0]
        for hh in range(1, NSA_HG):
            psum = psum + pc[hh]
        return outs_c, _split_dot(psum, cis_ref[:width, :], 3)

    n_quarters = 4
    quarter = nc // n_quarters
    n_visible = (start + NSA_TQ - CMP_BLOCK) // CMP_STRIDE + 1
    which = jnp.clip((n_visible - 1) // quarter, 0, n_quarters - 1)
    o_c, score = lax.switch(which, [functools.partial(cmp_branch, (r + 1) * quarter) for r in range(n_quarters)])
    blk = lax.broadcasted_iota(jnp.int32, (1, nb), 1)
    cur = tok // SLC_BLOCK
    forced = (blk == 0) | (blk == cur) | (blk == cur - 1)
    score = jnp.where(forced, FORCED_SCORE, score)
    score = jnp.where(blk <= cur, score, -jnp.inf)

    kbase = pl.multiple_of(jnp.maximum(start - WINDOW, 0), NSA_TQ)
    kw = kw_ref[0, pl.ds(kbase, WIN_KEYS), :]
    lag = tok - (kbase + lax.broadcasted_iota(jnp.int32, (1, WIN_KEYS), 1))
    bias_w = jnp.where((lag >= 0) & (lag < WINDOW), 0.0, NEG_INF)
    sw = [_dot_t(qh[hh], kw) for hh in heads]

    bits = lax.bitcast_convert_type(score.T, jnp.int32)
    key = bits ^ ((bits >> 31) & 0x7FFFFFFF)
    n_sel = min(SLC_TOPK, nb)

    def enough(c):
        return jnp.sum(jnp.where(key >= c, 1.0, 0.0), axis=0, keepdims=True) >= n_sel

    def enough3(c1, c2, c3):
        packed = jnp.where(key >= c3, 65793.0, jnp.where(key >= c2, 257.0, jnp.where(key >= c1, 1.0, 0.0)))
        tot = jnp.sum(packed, axis=0, keepdims=True).astype(jnp.int32)
        return (tot & 255) >= n_sel, ((tot >> 8) & 255) >= n_sel, (tot >> 16) >= n_sel

    zero_row = jnp.zeros((1, NSA_TQ), jnp.int32)
    thr = jnp.where(enough(zero_row), zero_row, INT32_MIN)
    for hi in range(30, 0, -2):
        c1 = thr + (1 << (hi - 1))
        c2 = thr + (1 << hi)
        c3 = c2 + (1 << (hi - 1))
        e1, e2, e3 = enough3(c1, c2, c3)
        thr = jnp.where(e3, c3, jnp.where(e2, c2, jnp.where(e1, c1, thr)))
    c1 = thr + 1
    thr = jnp.where(enough(c1), c1, thr)
    above = key > thr
    tied = key == thr
    need = n_sel - jnp.sum(jnp.where(above, 1.0, 0.0), axis=0, keepdims=True)
    tied_before = _dot(ltri_ref[...], jnp.where(tied, 1.0, 0.0).astype(BF16))
    sel_t = above | (tied & (tied_before < need))
    sel = jnp.where(sel_t, 1.0, 0.0).T
    if nb < LANES:
        sel = jnp.concatenate([sel, jnp.zeros((NSA_TQ, LANES - nb), F32)], axis=1)
    lane_q = lax.broadcasted_iota(jnp.int32, (NSA_TQ, LANES), 1)
    sel_sw = pltpu.roll(sel, NSA_DH, axis=1)
    sel_lo = jnp.where(lane_q >= NSA_DH, sel_sw, 0.0)
    sel_hi = jnp.where(lane_q < NSA_DH, sel_sw, 0.0)
    qx = [jnp.concatenate([qpad_ref[hh].astype(F32) + sel_lo, sel_hi], axis=1).astype(BF16) for hh in heads]

    sw = [sw[hh] + bias_w for hh in heads]
    ew = [jnp.exp((s - jnp.max(s, axis=-1, keepdims=True)).astype(BF16)) for s in sw]
    acc_w = [_dot(ew[hh], vw_ref[0, pl.ds(kbase, WIN_KEYS), :]) for hh in heads]

    n_tiles = (start + NSA_TQ + SLC_KT - 1) // SLC_KT

    def sel_tile(j, carry, causal, width=SLC_KT):
        ms, accs = carry
        k0 = pl.multiple_of(j * SLC_KT, SLC_KT)
        kt = ks_ref[0, pl.ds(k0, width), :]
        vt = vs_ref[0, pl.ds(k0, width), :]
        sj = [_dot_t(qx[hh], kt) for hh in heads]
        if causal:
            future = (k0 + lax.broadcasted_iota(jnp.int32, (1, width), 1)) > tok
            sj = [jnp.where(future, NEG_INF, s) for s in sj]
        m_new = [jnp.maximum(ms[hh], jnp.max(sj[hh], axis=-1, keepdims=True)) for hh in heads]
        pj = [jnp.exp((sj[hh] - m_new[hh]).astype(BF16)) for hh in heads]
        accs = [jnp.exp(ms[hh] - m_new[hh]) * accs[hh] + _dot(pj[hh], vt) for hh in heads]
        return tuple(m_new), tuple(accs)

    m0 = tuple(jnp.full((NSA_TQ, 1), NEG_INF, F32) for _ in heads)
    a0 = tuple(jnp.zeros((NSA_TQ, LANES), F32) for _ in heads)
    carry = lax.fori_loop(0, n_tiles - 1, functools.partial(sel_tile, causal=False), (m0, a0))
    per_tile = SLC_KT // NSA_TQ
    last = [functools.partial(sel_tile, n_tiles - 1, causal=True, width=(r + 1) * NSA_TQ) for r in range(per_tile)]
    _, acc_s = lax.switch(qi % per_tile, last, carry)

    gt = jax.nn.sigmoid(gate_ref[0])
    outs = []
    for hh in heads:
        g0, g1, g2 = (gt[:, 3 * hh + br:3 * hh + br + 1] for br in range(3))
        scale_s = g1 / acc_s[hh][:, NSA_DH:NSA_DH + 1]
        scale_w = g2 / acc_w[hh][:, NSA_DH:NSA_DH + 1]
        outs.append(g0 * o_c[hh] + scale_s * acc_s[hh][:, :NSA_DH] + scale_w * acc_w[hh][:, :NSA_DH])
    o_ref[...] = jnp.concatenate(outs, axis=1).astype(o_ref.dtype)


def _nsa_attn(qk_hm, qpad, ksx, gates, cmp_kv, v1, cis, ltri):
    t = qk_hm.shape[1]
    nq = t // NSA_TQ
    nc = cmp_kv.shape[2]
    nb = cis.shape[1]
    ks_spec = pl.BlockSpec((1, t, 2 * LANES), lambda g, i: (g, 0, 0))
    kw_spec = pl.BlockSpec((1, t, NSA_DH), lambda g, i: (NSA_HEADS + NSA_G + g, 0, 0))
    vs_spec = pl.BlockSpec((1, t, LANES), lambda g, i: (g, 0, 0))
    vw_spec = pl.BlockSpec((1, t, LANES), lambda g, i: (NSA_G + g, 0, 0))
    return pl.pallas_call(
        _nsa_attn_kernel,
        out_shape=jax.ShapeDtypeStruct((t, NSA_W), BF16),
        grid=(NSA_G, nq),
        in_specs=[pl.BlockSpec((NSA_HG, NSA_TQ, NSA_DH), lambda g, i: (g, i, 0)),
                  pl.BlockSpec((NSA_HG, NSA_TQ, LANES), lambda g, i: (g, i, 0)),
                  pl.BlockSpec((1, NSA_TQ, LANES), lambda g, i: (g, i, 0)),
                  pl.BlockSpec((1, 1, nc, NSA_DH), lambda g, i: (0, g, 0, 0)),
                  pl.BlockSpec((1, 1, nc, NSA_DH), lambda g, i: (1, g, 0, 0)),
                  ks_spec, vs_spec, kw_spec, vw_spec,
                  pl.BlockSpec((nc, nb), lambda g, i: (0, 0)),
                  pl.BlockSpec((nb, nb), lambda g, i: (0, 0))],
        out_specs=pl.BlockSpec((NSA_TQ, NSA_HG * NSA_DH), lambda g, i: (i, g)),
        compiler_params=_cparams(("parallel", "arbitrary")),
        name="nsa_attention",
    )(qk_hm, qpad, gates, cmp_kv, cmp_kv, ksx, v1, qk_hm, v1, cis, ltri)


def _rope_tables(pos, dh, reps):
    half = dh // 2
    inv_freq = ROPE_THETA ** (-np.arange(half, dtype=np.float64) / half)
    ang = np.asarray(pos, np.float64)[:, None] * inv_freq[None, :]
    cos = np.cos(ang)
    sin = np.sin(ang)
    cos_t = np.tile(np.concatenate([cos, cos], axis=1), (1, reps))
    sin_t = np.tile(np.concatenate([-sin, sin], axis=1), (1, reps))
    return jnp.asarray(cos_t, F32), jnp.asarray(sin_t, F32)


def _pad_cols(a, width):
    return jnp.pad(a, ((0, 0), (0, width - a.shape[1])))


def _pad_rows(a, height):
    return jnp.pad(a, ((0, height - a.shape[0]), (0, 0)))


def _pack_w_in_kernel(w_ref, xv_ref, o_ref):
    o_ref[...] = _pack_w_in(w_ref[...], xv_ref[...]).astype(o_ref.dtype)


def _pack_w_in_call(w_in, layer, xv):
    _, d, n_in = w_in.shape
    tk = 256
    return pl.pallas_call(
        _pack_w_in_kernel,
        out_shape=jax.ShapeDtypeStruct((d, PROJ_W), BF16),
        grid=(d // tk,),
        in_specs=[pl.BlockSpec((None, tk, n_in), lambda i: (layer, i, 0)),
                  pl.BlockSpec((tk, LANES), lambda i: (i, 0))],
        out_specs=pl.BlockSpec((tk, PROJ_W), lambda i: (i, 0)),
        compiler_params=_cparams(("parallel",)),
        name="pack_w_in",
    )(w_in, xv)


def _pack_w_in(w_l, xv):
    d = w_l.shape[0]
    o = 0
    ret = w_l[:, o:o + 4 * RET_W]; o += 4 * RET_W
    rkv = w_l[:, o:o + 3 * RW_W]; o += 3 * RW_W
    xw = w_l[:, o:o + RW_DECAY_RANK]; o += RW_DECAY_RANK
    xa = w_l[:, o:o + RW_A_RANK]; o += RW_A_RANK
    xg = w_l[:, o:o + RW_GATE_RANK]; o += RW_GATE_RANK
    q = w_l[:, o:o + NSA_W]; o += NSA_W
    kc, vc, ks, vs, kw, vw = (w_l[:, o + i * NSA_KV_W:o + (i + 1) * NSA_KV_W] for i in range(6))
    o += 6 * NSA_KV_W
    gates = w_l[:, o:o + 3 * NSA_HEADS]
    return jnp.concatenate([ret, rkv, _pad_cols(xw, LANES), _pad_cols(xa, LANES), xg, xv,
                            q, ks, kw, kc, vc, vs, vw, _pad_cols(gates, LANES)], axis=1)


def _pack_mix(mix, vres_mix):
    o = 3 * RW_W
    xw = mix[o:o + RW_DECAY_RANK]; o += RW_DECAY_RANK
    xa = mix[o:o + RW_A_RANK]; o += RW_A_RANK
    xg = mix[o:o + RW_GATE_RANK]
    z = lambda n: jnp.zeros((n,), mix.dtype)
    xv = z(LANES) if vres_mix is None else jnp.concatenate([vres_mix, z(LANES - RW_V_RANK)])
    return jnp.concatenate([mix[:3 * RW_W], xw, z(LANES - RW_DECAY_RANK), xa, z(LANES - RW_A_RANK), xg, xv])[None, :]


def kernel(x, ln1_g, w_in, w_in_vres, rwkv_mix, rwkv_vres_mix, rwkv_w0, rwkv_w2, rwkv_a0, rwkv_a2, rwkv_v0, rwkv_v2, rwkv_g2, rwkv_k_k, rwkv_k_a, rwkv_r_k, rwkv_ln_g, rwkv_ln_b, nsa_q_norm, nsa_k_norm, nsa_cmp_pe, nsa_cmp_k_w1, nsa_cmp_k_w2, nsa_cmp_v_w1, nsa_cmp_v_w2, w_out, ln2_g, w_up, w_down):
    bsz, t, d = x.shape
    assert bsz == 1
    depth = w_in.shape[0]
    nc = t // CMP_STRIDE
    nb = t // SLC_BLOCK
    xs = x.reshape(t, d)

    pos = np.arange(t)
    cos_r, sin_r = _rope_tables(pos, RET_DH, 1)
    cos_n, sin_n = _rope_tables(pos, NSA_DH, 2)
    cos_c, sin_c = _rope_tables(np.arange(nc) * CMP_STRIDE + (CMP_BLOCK - 1), NSA_DH, 1)
    lane_head = np.arange(RW_W) // RW_N
    gsum = jnp.asarray(lane_head[:, None] == lane_head[None, :], BF16)
    lane_h2 = np.arange(LANES) // NSA_DH
    gmean = jnp.asarray((lane_h2[:, None] == lane_h2[None, :]) / float(NSA_DH), BF16)
    cstart = np.arange(nc) * CMP_STRIDE
    sstart = np.arange(nb) * SLC_BLOCK
    cis = jnp.asarray((cstart[:, None] <= sstart[None, :] + SLC_BLOCK - 1)
                      & (cstart[:, None] + CMP_BLOCK - 1 >= sstart[None, :]), BF16)

    ltri = jnp.asarray(np.tril(np.ones((nb, nb), np.float32), -1), BF16)

    v_first = jnp.zeros((t, RW_W), F32)
    for l in range(depth):
        vres = l > 0
        xv_w = _pad_cols(w_in_vres[l - 1], LANES) if vres else jnp.zeros((d, LANES), F32)
        w_cat = _pack_w_in_call(w_in, l, xv_w)
        proj = _norm_matmul(xs, ln1_g[l], w_cat, name="in_proj")

        o_ret = _retention(proj, cos_r, sin_r)

        row = lambda a: a.reshape(1, -1)
        mix = _pack_mix(rwkv_mix[l], rwkv_vres_mix[l - 1] if vres else None)
        v0 = row(rwkv_v0[l - 1]) if vres else jnp.zeros((1, RW_W), F32)
        v2 = _pad_rows(rwkv_v2[l - 1], LANES) if vres else jnp.zeros((LANES, RW_W), F32)
        r_, lw_, k_, v_, an_, b_, g_ = _rwkv_pre(
            proj, mix, row(rwkv_w0[l]), _pad_rows(rwkv_w2[l], LANES), row(rwkv_a0[l]), _pad_rows(rwkv_a2[l], LANES),
            v0, v2, rwkv_g2[l], row(rwkv_k_k[l]), row(rwkv_k_a[l]), gsum, v_first, vres)
        if not vres:
            v_first = v_
        o_rwkv = _wkv(r_, lw_, k_, v_, an_, b_, g_, row(rwkv_ln_g[l]), row(rwkv_ln_b[l]), row(rwkv_r_k[l]), gsum)

        n_q = NSA_W // LANES
        n_k = NSA_KV_W // LANES
        w_slabs = jnp.stack([jnp.tile(nsa_q_norm[l] * (NSA_DH ** -0.5), 2)] * n_q
                            + [jnp.tile(nsa_k_norm[l, 1], 2)] * n_k + [jnp.tile(nsa_k_norm[l, 2], 2)] * n_k)[:, None, :]
        s_max = 1.02 * jnp.max(jnp.abs(nsa_q_norm[l])) * jnp.max(jnp.abs(nsa_k_norm[l, 1])) * (NSA_DH ** 0.5)
        big = jnp.full((1, LANES), jnp.exp2(jnp.ceil(jnp.log2(2.0 * s_max + 128.0))), F32)
        qk_hm, hmat, v1, gates, qpad, ksx = _nsa_prep(proj, w_slabs, cos_n, sin_n, gmean, big)
        w1 = jnp.stack([nsa_cmp_k_w1[l], nsa_cmp_v_w1[l]]).astype(BF16)
        w2 = jnp.stack([nsa_cmp_k_w2[l], nsa_cmp_v_w2[l]]).astype(BF16)
        pe = jnp.broadcast_to(nsa_cmp_pe[l].reshape(2, 1, CMP_BLOCK * NSA_DH), (2, 8, CMP_BLOCK * NSA_DH)).astype(BF16)
        cmp_kv = _compress(hmat, w1, w2, pe, row(nsa_k_norm[l, 0]), cos_c, sin_c)
        o_nsa = _nsa_attn(qk_hm, qpad, ksx, gates, cmp_kv, v1, cis, ltri)

        xs = _out_proj(o_ret, o_rwkv, o_nsa, w_out, l, xs)

        up = _norm_matmul(xs, ln2_g[l], w_up, layer=l, act="relu2", out_dtype=BF16, name="mlp_up")
        xs = _matmul(up, w_down, layer=l, res=xs, name="mlp_down")
    return xs.reshape(bsz, t, d)
```

```python
import functools

import numpy as np
import jax
import jax.numpy as jnp
from jax import lax
from jax.experimental import pallas as pl
from jax.experimental.pallas import tpu as pltpu

F32 = jnp.float32
BF16 = jnp.bfloat16

D_MODEL = 2048
RET_HEADS, RET_DH = 4, 128
RET_W = RET_HEADS * RET_DH
RW_HEADS, RW_N = 8, 64
RW_W = RW_HEADS * RW_N
RW_DECAY_RANK, RW_A_RANK, RW_V_RANK, RW_GATE_RANK = 96, 96, 64, 256
RW_GN_EPS = 64e-5
NSA_HEADS, NSA_G, NSA_DH = 16, 4, 64
NSA_HG = NSA_HEADS // NSA_G
NSA_W = NSA_HEADS * NSA_DH
NSA_KV_W = NSA_G * NSA_DH
CMP_BLOCK, CMP_STRIDE, CMP_HIDDEN = 32, 16, 256
SLC_BLOCK, SLC_TOPK, WINDOW = 64, 16, 512
NSA_TQ = 512
ROPE_THETA = 10000.0
MLP_HIDDEN = 4 * D_MODEL
RMS_EPS = 1e-6
NEG_INF = -1e30
FORCED_SCORE = 1e9

LANES = 128
VMEM_LIMIT = 56 * 1024 * 1024

C_RET = 0
C_RW_RKV = 2048
C_RW_XW = 3584
C_RW_XA = 3712
C_RW_XG = 3840
C_RW_XV = 4096
C_NSA_Q = 4224
C_NSA_KS = 5248
C_NSA_KW = 5504
C_NSA_KC = 5760
C_NSA_VC = 6016
C_NSA_VS = 6272
C_NSA_VW = 6528
C_NSA_GATE = 6784
PROJ_W = 6912


def _cparams(sem):
    return pltpu.CompilerParams(dimension_semantics=sem, vmem_limit_bytes=VMEM_LIMIT)


def _dot(a, b):
    return jnp.dot(a, b, preferred_element_type=F32)


def _dot_t(a, b):
    return lax.dot_general(a, b, (((1,), (1,)), ((), ())), preferred_element_type=F32)


def _dot_tl(a, b):
    return lax.dot_general(a, b, (((0,), (0,)), ((), ())), preferred_element_type=F32)


def _split_dot(x, m_bf16, passes):
    hi = x.astype(BF16)
    acc = _dot(hi, m_bf16)
    rem = x - hi.astype(F32)
    for _ in range(passes - 1):
        piece = rem.astype(BF16)
        acc = acc + _dot(piece, m_bf16)
        rem = rem - piece.astype(F32)
    return acc


def _mm_kernel(*refs, nk, act, has_res):
    a_ref, b_ref = refs[:2]
    r_ref = refs[2] if has_res else None
    o_ref = refs[3] if has_res else refs[2]

    def finish(y):
        if act == "relu2":
            y = jnp.square(jnp.maximum(y, 0.0))
        if has_res:
            y = y + r_ref[...]
        o_ref[...] = y.astype(o_ref.dtype)

    part = _dot(a_ref[...], b_ref[...].astype(BF16))
    if nk == 1:
        finish(part)
        return
    acc_ref = refs[-1]
    k = pl.program_id(2)

    @pl.when(k == 0)
    def _():
        acc_ref[...] = part

    @pl.when(k > 0)
    def _():
        acc_ref[...] += part

    @pl.when(k == nk - 1)
    def _():
        finish(acc_ref[...])


def _pick(n, cands):
    for c in cands:
        if n % c == 0:
            return c
    raise ValueError(f"no tile for {n}")


def _weight_spec(b, layer, block, index_map):
    if layer is None:
        return pl.BlockSpec(block, index_map)
    return pl.BlockSpec((None,) + block, lambda *idx: (layer,) + index_map(*idx))


def _matmul(a, b, *, layer=None, res=None, act=None, out_dtype=F32, name="matmul"):
    m, kd = a.shape
    n = b.shape[-1]
    tm = _pick(m, (1024, 512, 256, 128))
    tn = _pick(n, (1024, 768, 512, 256, 128))
    tk = _pick(kd, (2048, 1024, 512))
    nk = kd // tk
    in_specs = [pl.BlockSpec((tm, tk), lambda i, j, k: (i, k)),
                _weight_spec(b, layer, (tk, tn), lambda i, j, k: (k, j))]
    args = [a, b]
    if res is not None:
        in_specs.append(pl.BlockSpec((tm, tn), lambda i, j, k: (i, j)))
        args.append(res)
    return pl.pallas_call(
        functools.partial(_mm_kernel, nk=nk, act=act, has_res=res is not None),
        out_shape=jax.ShapeDtypeStruct((m, n), out_dtype),
        grid=(m // tm, n // tn, nk),
        in_specs=in_specs,
        out_specs=pl.BlockSpec((tm, tn), lambda i, j, k: (i, j)),
        scratch_shapes=[pltpu.VMEM((tm, tn), F32)] if nk > 1 else [],
        compiler_params=_cparams(("parallel", "parallel", "arbitrary")),
        name=name,
    )(*args)


def _norm_mm_kernel(x_ref, g_ref, b_ref, o_ref, h_ref, *, act):
    @pl.when(pl.program_id(1) == 0)
    def _():
        x = x_ref[...]
        ms = jnp.mean(x * x, axis=-1, keepdims=True)
        h_ref[...] = (x * lax.rsqrt(ms + RMS_EPS) * g_ref[...]).astype(BF16)

    y = _dot(h_ref[...], b_ref[...].astype(BF16))
    if act == "relu2":
        y = jnp.square(jnp.maximum(y, 0.0))
    o_ref[...] = y.astype(o_ref.dtype)


def _norm_matmul(x, g, b, *, layer=None, act=None, out_dtype=F32, name="norm_matmul"):
    m, kd = x.shape
    n = b.shape[-1]
    tm = _pick(m, (1024, 512, 256, 128))
    tn = _pick(n, (1024, 768, 512, 256, 128))
    return pl.pallas_call(
        functools.partial(_norm_mm_kernel, act=act),
        out_shape=jax.ShapeDtypeStruct((m, n), out_dtype),
        grid=(m // tm, n // tn),
        in_specs=[pl.BlockSpec((tm, kd), lambda i, j: (i, 0)),
                  pl.BlockSpec((1, kd), lambda i, j: (0, 0)),
                  _weight_spec(b, layer, (kd, tn), lambda i, j: (0, j))],
        out_specs=pl.BlockSpec((tm, tn), lambda i, j: (i, j)),
        scratch_shapes=[pltpu.VMEM((tm, kd), BF16)],
        compiler_params=_cparams(("parallel", "arbitrary")),
        name=name,
    )(x, g.reshape(1, kd), b)


def _out_proj_kernel(ret_ref, rw_ref, nsa_ref, w_ref, res_ref, o_ref, wb_ref):
    @pl.when(pl.program_id(1) == 0)
    def _():
        wb_ref[...] = w_ref[...].astype(BF16)

    k0 = ret_ref.shape[1]
    k1 = k0 + rw_ref.shape[1]
    acc = _dot(ret_ref[...], wb_ref[:k0, :])
    acc = acc + _dot(rw_ref[...], wb_ref[k0:k1, :])
    acc = acc + _dot(nsa_ref[...], wb_ref[k1:, :])
    o_ref[...] = acc + res_ref[...]


def _out_proj(o_ret, o_rwkv, o_nsa, w, layer, res):
    m = o_ret.shape[0]
    kd, n = w.shape[-2:]
    assert kd == o_ret.shape[1] + o_rwkv.shape[1] + o_nsa.shape[1]
    tm = _pick(m, (1024, 512, 256, 128))
    tn = _pick(n, (1024, 512, 256, 128))

    def rows(a):
        return pl.BlockSpec((tm, a.shape[1]), lambda j, i: (i, 0))

    return pl.pallas_call(
        _out_proj_kernel,
        out_shape=jax.ShapeDtypeStruct((m, n), F32),
        grid=(n // tn, m // tm),
        in_specs=[rows(o_ret), rows(o_rwkv), rows(o_nsa),
                  _weight_spec(w, layer, (kd, tn), lambda j, i: (0, j)),
                  pl.BlockSpec((tm, tn), lambda j, i: (i, j))],
        out_specs=pl.BlockSpec((tm, tn), lambda j, i: (i, j)),
        scratch_shapes=[pltpu.VMEM((kd, tn), BF16)],
        compiler_params=_cparams(("parallel", "arbitrary")),
        name="out_proj",
    )(o_ret, o_rwkv, o_nsa, w, res)


RET_CHUNK = 256


def _retention_kernel(q_ref, k_ref, v_ref, g_ref, cos_ref, sin_ref, dmat_ref, qd_ref, kd_ref, cd_ref, o_ref, state_ref):
    @pl.when(pl.program_id(0) == 0)
    def _():
        state_ref[...] = jnp.zeros_like(state_ref)

    cos = cos_ref[...]
    sin = sin_ref[...]

    def rope(x):
        return x * cos + pltpu.roll(x, RET_DH // 2, axis=1) * sin

    heads = range(RET_HEADS)
    hs = [slice(h * RET_DH, (h + 1) * RET_DH) for h in heads]
    q_all, k_all, v_all, gate = q_ref[...], k_ref[...], v_ref[...], g_ref[...]
    qb = [rope(q_all[:, s]).astype(BF16) for s in hs]
    kf = [rope(k_all[:, s]) * (RET_DH ** -0.5) for s in hs]
    kb = [kf[h].astype(BF16) for h in heads]
    vb = [v_all[:, s].astype(BF16) for s in hs]
    inner = [(_dot_t(qb[h], kb[h]) * dmat_ref[h]).astype(BF16) for h in heads]
    cross = [_dot(qb[h], state_ref[h].astype(BF16)) * qd_ref[h] for h in heads]
    kv = [_dot_tl((kf[h] * kd_ref[h]).astype(BF16), vb[h]) for h in heads]
    outs = []
    for h in heads:
        out = _dot(inner[h], vb[h]) + cross[h]
        state_ref[h] = cd_ref[h] * state_ref[h] + kv[h]
        outs.append(out * lax.rsqrt(jnp.mean(out * out, axis=-1, keepdims=True) + RMS_EPS))
    o_ref[...] = (gate * jax.nn.sigmoid(gate) * jnp.concatenate(outs, axis=1)).astype(o_ref.dtype)


def _retention(proj, cos, sin):
    t = proj.shape[0]
    chunk = RET_CHUNK
    log_gamma = np.log(1.0 - 2.0 ** (-5.0 - np.arange(RET_HEADS, dtype=np.float64)))[:, None, None]
    n = np.arange(chunk, dtype=np.float64)
    lag = n[:, None] - n[None, :]
    dmat = np.where(lag >= 0, np.exp(np.maximum(lag, 0.0)[None] * log_gamma), 0.0)
    ones = np.ones((1, 1, RET_DH))
    qd = np.exp((n + 1.0)[None, :, None] * log_gamma) * ones
    kd = np.exp((chunk - 1.0 - n)[None, :, None] * log_gamma) * ones
    cd = np.exp(chunk * log_gamma) * ones
    tables = [jnp.asarray(a, F32) for a in (dmat, qd, kd, cd)]
    base = C_RET // RET_W

    def col(off):
        return pl.BlockSpec((chunk, RET_W), lambda c, off=off: (c, base + off))

    def full(a):
        return pl.BlockSpec(a.shape, lambda c: (0, 0, 0))

    tab = pl.BlockSpec((chunk, RET_DH), lambda c: (c, 0))
    return pl.pallas_call(
        _retention_kernel,
        out_shape=jax.ShapeDtypeStruct((t, RET_W), BF16),
        grid=(t // chunk,),
        in_specs=[col(0), col(1), col(2), col(3), tab, tab] + [full(a) for a in tables],
        out_specs=pl.BlockSpec((chunk, RET_W), lambda c: (c, 0)),
        scratch_shapes=[pltpu.VMEM((RET_HEADS, RET_DH, RET_DH), F32)],
        compiler_params=_cparams(("arbitrary",)),
        name="retention",
    )(proj, proj, proj, proj, cos, sin, *tables)


def _dot_f32(a, b):
    a_hi = a.astype(BF16)
    b_hi = b.astype(BF16)
    a_lo = (a - a_hi.astype(F32)).astype(BF16)
    b_lo = (b - b_hi.astype(F32)).astype(BF16)
    return _dot(a_hi, b_hi) + (_dot(a_hi, b_lo) + _dot(a_lo, b_hi))


RW_PRE_TM = 256
_MIX_OFF = (0, 512, 1024, 1536, 1664, 1792, 2048, 2176)


def _rwkv_pre_kernel(r_ref, k_ref, v_ref, xw_ref, xa_ref, xg_ref, xv_ref,
                     rp_ref, kp_ref, vp_ref, xwp_ref, xap_ref, xgp_ref, xvp_ref,
                     mix_ref, w0_ref, w2_ref, a0_ref, a2_ref, v0_ref, v2_ref, g2_ref, kk_ref, ka_ref,
                     gsum_ref, vfirst_ref,
                     r_out, lw_out, k_out, v_out, an_out, b_out, g_out, *, use_vres):
    i = pl.program_id(0)

    def mixed(cur_ref, prev_ref, seg):
        cur = cur_ref[...]
        mix = mix_ref[:, _MIX_OFF[seg]:_MIX_OFF[seg + 1]]
        prev_row = jnp.where(i > 0, prev_ref[7:8, :], 0.0)
        rid = lax.broadcasted_iota(jnp.int32, cur.shape, 0)
        shifted = jnp.where(rid == 0, prev_row, pltpu.roll(cur, 1, axis=0))
        return cur + mix * (shifted - cur)

    r = mixed(r_ref, rp_ref, 0)
    k = mixed(k_ref, kp_ref, 1)
    v = mixed(v_ref, vp_ref, 2)
    xw = mixed(xw_ref, xwp_ref, 3)
    xa = mixed(xa_ref, xap_ref, 4)
    xg = mixed(xg_ref, xgp_ref, 5)

    z = -(w0_ref[...] + _dot_f32(jnp.tanh(xw), w2_ref[...]))
    softplus = jnp.maximum(z, 0.0) + jnp.log(1.0 + jnp.exp(-jnp.abs(z)))
    w = -softplus - 0.5
    lw_out[...] = -jnp.exp(w)
    a = jax.nn.sigmoid(a0_ref[...] + _dot_f32(xa, a2_ref[...]))
    g_out[...] = _dot_f32(jax.nn.sigmoid(xg), g2_ref[...])
    if use_vres:
        xv = mixed(xv_ref, xvp_ref, 6)
        v = v + (vfirst_ref[...] - v) * jax.nn.sigmoid(v0_ref[...] + _dot_f32(xv, v2_ref[...]))
    kk = k * kk_ref[...]
    ss = _split_dot(kk * kk, gsum_ref[...], 3)
    kk = kk / jnp.maximum(jnp.sqrt(ss), 1e-12)
    r_out[...] = r
    k_out[...] = k * (1.0 + (a - 1.0) * ka_ref[...])
    v_out[...] = v
    an_out[...] = -kk
    b_out[...] = kk * a


def _rwkv_pre(proj, mix, w0, w2, a0, a2, v0, v2, g2, k_k, k_a, gsum, v_first, use_vres):
    t = proj.shape[0]
    tm = RW_PRE_TM

    def cur(width, off):
        return pl.BlockSpec((tm, width), lambda i: (i, off // width))

    def prev(width, off):
        return pl.BlockSpec((8, width), lambda i: (jnp.maximum(i * (tm // 8) - 1, 0), off // width))

    def full(a):
        return pl.BlockSpec(a.shape, lambda i: (0,) * a.ndim)

    segs = [(RW_W, C_RW_RKV), (RW_W, C_RW_RKV + RW_W), (RW_W, C_RW_RKV + 2 * RW_W),
            (128, C_RW_XW), (128, C_RW_XA), (256, C_RW_XG), (128, C_RW_XV)]
    params = [mix, w0, w2, a0, a2, v0, v2, g2, k_k, k_a, gsum]
    row = pl.BlockSpec((tm, RW_W), lambda i: (i, 0))
    outs = pl.pallas_call(
        functools.partial(_rwkv_pre_kernel, use_vres=use_vres),
        out_shape=[jax.ShapeDtypeStruct((t, RW_W), F32)] * 7,
        grid=(t // tm,),
        in_specs=[cur(w, o) for w, o in segs] + [prev(w, o) for w, o in segs] + [full(p) for p in params] + [row],
        out_specs=[row] * 7,
        compiler_params=_cparams(("parallel",)),
        name="rwkv_pre",
    )(*([proj] * 14), *params, v_first)
    return outs


RW_CHUNK = 128


def _wkv_kernel(r_ref, lw_ref, k_ref, v_ref, an_ref, b_ref, g_ref, lng_ref, lnb_ref, rk_ref,
                gsum_ref, tril_ref, o_ref, s_ref):
    c = pl.program_id(0)
    C = RW_CHUNK
    N = RW_N

    @pl.when(c == 0)
    def _():
        s_ref[...] = jnp.zeros_like(s_ref)

    r = r_ref[...]
    lw = lw_ref[...]
    k = k_ref[...]
    v = v_ref[...]
    lw_hi = lw.astype(BF16)
    rem = lw - lw_hi.astype(F32)
    lw_mid = rem.astype(BF16)
    lw_lo = (rem - lw_mid.astype(F32)).astype(BF16)
    tril = tril_ref[...]
    lg = _dot(tril, lw_hi) + (_dot(tril, lw_mid) + _dot(tril, lw_lo))
    gam = jnp.exp(lg)
    ginv = jnp.exp(-lg)
    at = (an_ref[...] * jnp.exp(lg - lw)).astype(BF16)
    bt = (b_ref[...] * ginv).astype(BF16)
    kt = (k * ginv).astype(BF16)
    rt_f = r * gam
    rt = rt_f.astype(BF16)
    vb = v.astype(BF16)
    g_last = gam[C - 1:C, :]

    rowi = lax.broadcasted_iota(jnp.int32, (C, C), 0)
    coli = lax.broadcasted_iota(jnp.int32, (C, C), 1)
    strict = rowi > coli
    incl = rowi >= coli

    heads = range(RW_HEADS)
    hs = [slice(h * N, (h + 1) * N) for h in heads]
    bth = [bt[:, s] for s in hs]
    kth = [kt[:, s] for s in hs]
    vh = [vb[:, s] for s in hs]
    big = [_dot_t(jnp.concatenate([at[:, s], rt[:, s]], axis=0), jnp.concatenate([bth[h], kth[h]], axis=0))
           for h, s in zip(heads, hs)]
    a_ab = [jnp.where(strict, big[h][:C, :C], 0.0) for h in heads]
    a_ak = [jnp.where(strict, big[h][:C, C:], 0.0).astype(BF16) for h in heads]
    a_rb = [jnp.where(incl, big[h][C:, :C], 0.0).astype(BF16) for h in heads]
    a_rk = [jnp.where(incl, big[h][C:, C:], 0.0).astype(BF16) for h in heads]
    akv = [_dot(a_ak[h], vh[h]) for h in heads]
    def same_block(size):
        shift = size.bit_length() - 1
        return (rowi >> shift) == (coli >> shift)

    tinv = [jnp.where(same_block(2), a_ab[h], 0.0) + jnp.where(rowi == coli, 1.0, 0.0) for h in heads]
    size = 2
    while size < C:
        lower_left = same_block(2 * size) & jnp.logical_not(same_block(size))
        off = [jnp.where(lower_left, a_ab[h], 0.0).astype(BF16) for h in heads]
        tb = [tinv[h].astype(BF16) for h in heads]
        half = [_dot(tb[h], off[h]).astype(BF16) for h in heads]
        tinv = [tinv[h] + _dot(half[h], tb[h]) for h in heads]
        size *= 2
    xb = [_dot(tinv[h].astype(BF16), jnp.concatenate([at[:, hs[h]], akv[h].astype(BF16)], axis=1)).astype(BF16)
          for h in heads]
    yx = [_dot(a_rb[h], xb[h]) for h in heads]
    ykv = [_dot(a_rk[h], vh[h]) for h in heads]
    xtb = [_dot_tl(xb[h], bth[h]) for h in heads]
    vtk = [_dot_tl(vh[h], kth[h]) for h in heads]
    ys = []
    for h in heads:
        gl = g_last[:, hs[h]]
        y1 = (rt_f[:, hs[h]] + yx[h][:, :N]).astype(BF16)
        s0 = s_ref[h]
        s0b = s0.astype(BF16)
        ys.append(_dot_t(y1, s0b) + (yx[h][:, N:] + ykv[h]))
        s_ref[h] = (s0 + _dot(s0b, xtb[h][:N].astype(BF16)) + (xtb[h][N:] + vtk[h])) * gl
    y = jnp.concatenate(ys, axis=1)

    gsum = gsum_ref[...]
    inv_n = 1.0 / N
    mu = _split_dot(y, gsum, 3) * inv_n
    yc = y - mu
    var = _split_dot(yc * yc, gsum, 3) * inv_n
    yn = yc * lax.rsqrt(var + RW_GN_EPS) * lng_ref[...] + lnb_ref[...]
    bonus = _split_dot(r * k * rk_ref[...], gsum, 3)
    o_ref[...] = ((yn + bonus * v) * g_ref[...]).astype(o_ref.dtype)


def _wkv(r, lw, k, v, an, b, g, ln_g, ln_b, r_k, gsum):
    t = r.shape[0]
    C = RW_CHUNK
    tril = jnp.asarray(np.tril(np.ones((C, C), np.float32)), BF16)
    row = pl.BlockSpec((C, RW_W), lambda c: (c, 0))
    vec = pl.BlockSpec((1, RW_W), lambda c: (0, 0))
    return pl.pallas_call(
        _wkv_kernel,
        out_shape=jax.ShapeDtypeStruct((t, RW_W), BF16),
        grid=(t // C,),
        in_specs=[row] * 7 + [vec] * 3 + [pl.BlockSpec((RW_W, RW_W), lambda c: (0, 0)),
                                          pl.BlockSpec((C, C), lambda c: (0, 0))],
        out_specs=row,
        scratch_shapes=[pltpu.VMEM((RW_HEADS, RW_N, RW_N), F32)],
        compiler_params=_cparams(("arbitrary",)),
        name="wkv7",
    )(r, lw, k, v, an, b, g, ln_g, ln_b, r_k, gsum, tril)


NSA_PREP_TM = 512


N_NORM_SLABS = (NSA_W + 2 * NSA_KV_W) // LANES
N_RAW_SLABS = 4 * NSA_KV_W // LANES
N_NSA_SLABS = N_NORM_SLABS + N_RAW_SLABS + 1


def _nsa_prep_kernel(*refs):
    x_refs = refs[:N_NSA_SLABS]
    (cos_ref, sin_ref, w_ref, gmean_ref, big_ref,
     qk_ref, hmat_ref, v1_ref, gate_ref, qpad_ref, ksx_ref) = refs[N_NSA_SLABS:]
    cos = cos_ref[...]
    sin = sin_ref[...]
    gmean = gmean_ref[...]
    half = NSA_DH // 2
    tm = cos.shape[0]
    lane = lax.broadcasted_iota(jnp.int32, cos.shape, 1)
    first_half = (lane % NSA_DH) < half
    low = lane < NSA_DH
    blk_row = (pl.program_id(0) * tm + lax.broadcasted_iota(jnp.int32, cos.shape, 0)) // SLC_BLOCK
    big = big_ref[...]
    mark_lo = jnp.where(blk_row == lane - NSA_DH, big, 0.0)
    mark_hi = jnp.where(low & (blk_row == lane + NSA_DH), big, 0.0).astype(BF16)
    n_q = NSA_W // LANES
    n_k = NSA_KV_W // LANES
    for s in range(N_NORM_SLABS):
        x = x_refs[s][...]
        y = x * lax.rsqrt(_split_dot(x * x, gmean, 3) + RMS_EPS) * w_ref[s]
        rot = jnp.where(first_half, pltpu.roll(y, LANES - half, axis=1), pltpu.roll(y, half, axis=1))
        res32 = y * cos + rot * sin
        res = res32.astype(qk_ref.dtype)
        qk_ref[2 * s] = res[:, :NSA_DH]
        qk_ref[2 * s + 1] = res[:, NSA_DH:]
        if s >= n_q + n_k:
            continue
        swapped = pltpu.roll(res32, NSA_DH, axis=1)
        if s < n_q:
            qpad_ref[2 * s] = jnp.where(low, res32, 0.0).astype(BF16)
            qpad_ref[2 * s + 1] = jnp.where(low, swapped, 0.0).astype(BF16)
        else:
            g0 = 2 * (s - n_q)
            ksx_ref[g0, :, :LANES] = jnp.where(low, res32, mark_lo).astype(BF16)
            ksx_ref[g0 + 1, :, :LANES] = jnp.where(low, swapped, mark_lo).astype(BF16)
            ksx_ref[g0, :, LANES:] = mark_hi
            ksx_ref[g0 + 1, :, LANES:] = mark_hi

    rows = hmat_ref.shape[2]
    low_r = lax.broadcasted_iota(jnp.int32, (rows, LANES), 1) < NSA_DH
    for s in range(NSA_KV_W // LANES * 2):
        kind, pair = divmod(s, NSA_KV_W // LANES)
        x_ref = x_refs[N_NORM_SLABS + s]
        for t2 in range(CMP_STRIDE // 2):
            r0 = x_ref[pl.ds(2 * t2, rows, stride=CMP_STRIDE), :]
            r1 = x_ref[pl.ds(2 * t2 + 1, rows, stride=CMP_STRIDE), :]
            cols = slice(t2 * LANES, (t2 + 1) * LANES)
            hmat_ref[kind, 2 * pair, :, cols] = jnp.where(low_r, r0, pltpu.roll(r1, NSA_DH, axis=1)).astype(BF16)
            hmat_ref[kind, 2 * pair + 1, :, cols] = jnp.where(low_r, pltpu.roll(r0, NSA_DH, axis=1), r1).astype(BF16)

    one_col = jnp.where(lane == NSA_DH, 1.0, 0.0)
    for s in range(NSA_KV_W // LANES * 2):
        x = x_refs[N_NORM_SLABS + N_RAW_SLABS // 2 + s][...]
        v1_ref[2 * s] = jnp.where(low, x, one_col).astype(BF16)
        v1_ref[2 * s + 1] = jnp.where(low, pltpu.roll(x, NSA_DH, axis=1), one_col).astype(BF16)

    xg = x_refs[N_NSA_SLABS - 1][...]
    per_group = 3 * NSA_HG
    for g in range(NSA_G):
        shifted = xg if g == 0 else pltpu.roll(xg, LANES - per_group * g, axis=1)
        gate_ref[g] = jnp.where(lane < per_group, shifted, 0.0)


def _nsa_prep(proj, w_slabs, cos, sin, gmean, big):
    t = proj.shape[0]
    assert t // SLC_BLOCK <= LANES
    tm = NSA_PREP_TM
    base = C_NSA_Q // LANES
    tab = pl.BlockSpec((tm, LANES), lambda i: (i, 0))
    n_heads_out = 2 * N_NORM_SLABS
    return pl.pallas_call(
        _nsa_prep_kernel,
        out_shape=[jax.ShapeDtypeStruct((n_heads_out, t, NSA_DH), BF16),
                   jax.ShapeDtypeStruct((2, NSA_G, t // CMP_STRIDE, CMP_STRIDE * NSA_DH), BF16),
                   jax.ShapeDtypeStruct((2 * NSA_G, t, LANES), BF16),
                   jax.ShapeDtypeStruct((NSA_G, t, LANES), F32),
                   jax.ShapeDtypeStruct((NSA_HEADS, t, LANES), BF16),
                   jax.ShapeDtypeStruct((NSA_G, t, 2 * LANES), BF16)],
        grid=(t // tm,),
        in_specs=[pl.BlockSpec((tm, LANES), lambda i, s=s: (i, base + s)) for s in range(N_NSA_SLABS)]
        + [tab, tab, pl.BlockSpec((N_NORM_SLABS, 1, LANES), lambda i: (0, 0, 0)),
           pl.BlockSpec((LANES, LANES), lambda i: (0, 0)), pl.BlockSpec((1, LANES), lambda i: (0, 0))],
        out_specs=[pl.BlockSpec((n_heads_out, tm, NSA_DH), lambda i: (0, i, 0)),
                   pl.BlockSpec((2, NSA_G, tm // CMP_STRIDE, CMP_STRIDE * NSA_DH), lambda i: (0, 0, i, 0)),
                   pl.BlockSpec((2 * NSA_G, tm, LANES), lambda i: (0, i, 0)),
                   pl.BlockSpec((NSA_G, tm, LANES), lambda i: (0, i, 0)),
                   pl.BlockSpec((NSA_HEADS, tm, LANES), lambda i: (0, i, 0)),
                   pl.BlockSpec((NSA_G, tm, 2 * LANES), lambda i: (0, i, 0))],
        compiler_params=_cparams(("parallel",)),
        name="nsa_prep",
    )(*([proj] * N_NSA_SLABS), cos, sin, w_slabs, gmean, big)


def _compress_kernel(h_ref, w1_ref, w2_ref, pe_ref, nw_ref, cos_ref, sin_ref, o_ref):
    kind = pl.program_id(0)
    nc = h_ref.shape[2]
    half_in = CMP_STRIDE * NSA_DH
    hm = h_ref[0, 0]
    w1 = w1_ref[0]
    first = _dot(hm, w1[:half_in])
    second = _dot(hm, w1[half_in:])
    const = _dot(pe_ref[0], w1)[0:1, :]
    pre = first + pltpu.roll(second, nc - 1, axis=0) + const
    hid = 0.5 * pre * (1.0 + jnp.tanh(0.7978845608028654 * (pre + 0.044715 * pre * pre * pre)))
    out = _dot(hid.astype(BF16), w2_ref[0])
    y = out * lax.rsqrt(jnp.mean(out * out, axis=-1, keepdims=True) + RMS_EPS) * nw_ref[...]
    half = NSA_DH // 2
    rot = jnp.concatenate([y[:, half:], y[:, :half]], axis=1)
    roped = y * cos_ref[...] + rot * sin_ref[...]
    o_ref[0, 0] = jnp.where(kind == 0, roped, out).astype(o_ref.dtype)


def _compress(hmat, w1, w2, pe, nw, cos_c, sin_c):
    _, g, nc, width = hmat.shape
    return pl.pallas_call(
        _compress_kernel,
        out_shape=jax.ShapeDtypeStruct((2, g, nc, NSA_DH), BF16),
        grid=(2, g),
        in_specs=[pl.BlockSpec((1, 1, nc, width), lambda a, b: (a, b, 0, 0)),
                  pl.BlockSpec((1, 2 * width, CMP_HIDDEN), lambda a, b: (a, 0, 0)),
                  pl.BlockSpec((1, CMP_HIDDEN, NSA_DH), lambda a, b: (a, 0, 0)),
                  pl.BlockSpec((1, 8, 2 * width), lambda a, b: (a, 0, 0)),
                  pl.BlockSpec((1, NSA_DH), lambda a, b: (0, 0)),
                  pl.BlockSpec((nc, NSA_DH), lambda a, b: (0, 0)),
                  pl.BlockSpec((nc, NSA_DH), lambda a, b: (0, 0))],
        out_specs=pl.BlockSpec((1, 1, nc, NSA_DH), lambda a, b: (a, b, 0, 0)),
        compiler_params=_cparams(("parallel", "parallel")),
        name="nsa_compress",
    )(hmat, w1, w2, pe, nw, cos_c, sin_c)


SLC_KT = 2048
WIN_KEYS = WINDOW + NSA_TQ


INT32_MIN = -2 ** 31


def _nsa_attn_kernel(q_ref, qpad_ref, gate_ref, kc_ref, vc_ref, ks_ref, vs_ref, kw_ref, vw_ref, cis_ref, ltri_ref,
                     o_ref):
    qi = pl.program_id(1)
    nc = kc_ref.shape[2]
    nb = cis_ref.shape[1]
    heads = range(NSA_HG)
    start = qi * NSA_TQ
    qh = [q_ref[hh] for hh in heads]
    tok = start + lax.broadcasted_iota(jnp.int32, (NSA_TQ, 1), 0)

    def softmax_terms(scores):
        return [jnp.exp(s - jnp.max(s, axis=-1, keepdims=True)) for s in scores]

    def cmp_branch(width):
        kc = kc_ref[0, 0, :width, :]
        vc = vc_ref[0, 0, :width, :]
        cend = lax.broadcasted_iota(jnp.int32, (1, width), 1) * CMP_STRIDE + (CMP_BLOCK - 1)
        bias_c = jnp.where(cend <= tok, 0.0, NEG_INF)
        ec = softmax_terms([_dot_t(qh[hh], kc) + bias_c for hh in heads])
        sees_any = tok >= CMP_BLOCK - 1
        pc = [ec[hh] * jnp.where(sees_any, 1.0 / jnp.sum(ec[hh], axis=-1, keepdims=True), 0.0) for hh in heads]
        outs_c = tuple(_dot(pc[hh].astype(BF16), vc) for hh in heads)
        psum = pc[0]
        for hh in range(1, NSA_HG):
            psum = psum + pc[hh]
        return outs_c, _split_dot(psum, cis_ref[:width, :], 3)

    n_quarters = 4
    quarter = nc // n_quarters
    n_visible = (start + NSA_TQ - CMP_BLOCK) // CMP_STRIDE + 1
    which = jnp.clip((n_visible - 1) // quarter, 0, n_quarters - 1)
    o_c, score = lax.switch(which, [functools.partial(cmp_branch, (r + 1) * quarter) for r in range(n_quarters)])
    blk = lax.broadcasted_iota(jnp.int32, (1, nb), 1)
    cur = tok // SLC_BLOCK
    forced = (blk == 0) | (blk == cur) | (blk == cur - 1)
    score = jnp.where(forced, FORCED_SCORE, score)
    score = jnp.where(blk <= cur, score, -jnp.inf)

    kbase = pl.multiple_of(jnp.maximum(start - WINDOW, 0), NSA_TQ)
    kw = kw_ref[0, pl.ds(kbase, WIN_KEYS), :]
    lag = tok - (kbase + lax.broadcasted_iota(jnp.int32, (1, WIN_KEYS), 1))
    bias_w = jnp.where((lag >= 0) & (lag < WINDOW), 0.0, NEG_INF)
    sw = [_dot_t(qh[hh], kw) for hh in heads]

    bits = lax.bitcast_convert_type(score.T, jnp.int32)
    key = bits ^ ((bits >> 31) & 0x7FFFFFFF)
    n_sel = min(SLC_TOPK, nb)

    def enough(c):
        return jnp.sum(jnp.where(key >= c, 1.0, 0.0), axis=0, keepdims=True) >= n_sel

    def enough3(c1, c2, c3):
        packed = jnp.where(key >= c3, 65793.0, jnp.where(key >= c2, 257.0, jnp.where(key >= c1, 1.0, 0.0)))
        tot = jnp.sum(packed, axis=0, keepdims=True).astype(jnp.int32)
        return (tot & 255) >= n_sel, ((tot >> 8) & 255) >= n_sel, (tot >> 16) >= n_sel

    zero_row = jnp.zeros((1, NSA_TQ), jnp.int32)
    thr = jnp.where(enough(zero_row), zero_row, INT32_MIN)
    for hi in range(30, 0, -2):
        c1 = thr + (1 << (hi - 1))
        c2 = thr + (1 << hi)
        c3 = c2 + (1 << (hi - 1))
        e1, e2, e3 = enough3(c1, c2, c3)
        thr = jnp.where(e3, c3, jnp.where(e2, c2, jnp.where(e1, c1, thr)))
    c1 = thr + 1
    thr = jnp.where(enough(c1), c1, thr)
    above = key > thr
    tied = key == thr
    need = n_sel - jnp.sum(jnp.where(above, 1.0, 0.0), axis=0, keepdims=True)
    tied_before = _dot(ltri_ref[...], jnp.where(tied, 1.0, 0.0).astype(BF16))
    sel_t = above | (tied & (tied_before < need))
    sel = jnp.where(sel_t, 1.0, 0.0).T
    if nb < LANES:
        sel = jnp.concatenate([sel, jnp.zeros((NSA_TQ, LANES - nb), F32)], axis=1)
    lane_q = lax.broadcasted_iota(jnp.int32, (NSA_TQ, LANES), 1)
    sel_sw = pltpu.roll(sel, NSA_DH, axis=1)
    sel_lo = jnp.where(lane_q >= NSA_DH, sel_sw, 0.0)
    sel_hi = jnp.where(lane_q < NSA_DH, sel_sw, 0.0)
    qx = [jnp.concatenate([qpad_ref[hh].astype(F32) + sel_lo, sel_hi], axis=1).astype(BF16) for hh in heads]

    sw = [sw[hh] + bias_w for hh in heads]
    ew = [jnp.exp((s - jnp.max(s, axis=-1, keepdims=True)).astype(BF16)) for s in sw]
    acc_w = [_dot(ew[hh], vw_ref[0, pl.ds(kbase, WIN_KEYS), :]) for hh in heads]

    n_tiles = (start + NSA_TQ + SLC_KT - 1) // SLC_KT

    def sel_tile(j, carry, causal, width=SLC_KT):
        ms, accs = carry
        k0 = pl.multiple_of(j * SLC_KT, SLC_KT)
        kt = ks_ref[0, pl.ds(k0, width), :]
        vt = vs_ref[0, pl.ds(k0, width), :]
        sj = [_dot_t(qx[hh], kt) for hh in heads]
        if causal:
            future = (k0 + lax.broadcasted_iota(jnp.int32, (1, width), 1)) > tok
            sj = [jnp.where(future, NEG_INF, s) for s in sj]
        m_new = [jnp.maximum(ms[hh], jnp.max(sj[hh], axis=-1, keepdims=True)) for hh in heads]
        pj = [jnp.exp((sj[hh] - m_new[hh]).astype(BF16)) for hh in heads]
        accs = [jnp.exp(ms[hh] - m_new[hh]) * accs[hh] + _dot(pj[hh], vt) for hh in heads]
        return tuple(m_new), tuple(accs)

    m0 = tuple(jnp.full((NSA_TQ, 1), NEG_INF, F32) for _ in heads)
    a0 = tuple(jnp.zeros((NSA_TQ, LANES), F32) for _ in heads)
    carry = lax.fori_loop(0, n_tiles - 1, functools.partial(sel_tile, causal=False), (m0, a0))
    per_tile = SLC_KT // NSA_TQ
    last = [functools.partial(sel_tile, n_tiles - 1, causal=True, width=(r + 1) * NSA_TQ) for r in range(per_tile)]
    _, acc_s = lax.switch(qi % per_tile, last, carry)

    gt = jax.nn.sigmoid(gate_ref[0])
    outs = []
    for hh in heads:
        g0, g1, g2 = (gt[:, 3 * hh + br:3 * hh + br + 1] for br in range(3))
        scale_s = g1 / acc_s[hh][:, NSA_DH:NSA_DH + 1]
        scale_w = g2 / acc_w[hh][:, NSA_DH:NSA_DH + 1]
        outs.append(g0 * o_c[hh] + scale_s * acc_s[hh][:, :NSA_DH] + scale_w * acc_w[hh][:, :NSA_DH])
    o_ref[...] = jnp.concatenate(outs, axis=1).astype(o_ref.dtype)


def _nsa_attn(qk_hm, qpad, ksx, gates, cmp_kv, v1, cis, ltri):
    t = qk_hm.shape[1]
    nq = t // NSA_TQ
    nc = cmp_kv.shape[2]
    nb = cis.shape[1]
    ks_spec = pl.BlockSpec((1, t, 2 * LANES), lambda g, i: (g, 0, 0))
    kw_spec = pl.BlockSpec((1, t, NSA_DH), lambda g, i: (NSA_HEADS + NSA_G + g, 0, 0))
    vs_spec = pl.BlockSpec((1, t, LANES), lambda g, i: (g, 0, 0))
    vw_spec = pl.BlockSpec((1, t, LANES), lambda g, i: (NSA_G + g, 0, 0))
    return pl.pallas_call(
        _nsa_attn_kernel,
        out_shape=jax.ShapeDtypeStruct((t, NSA_W), BF16),
        grid=(NSA_G, nq),
        in_specs=[pl.BlockSpec((NSA_HG, NSA_TQ, NSA_DH), lambda g, i: (g, i, 0)),
                  pl.BlockSpec((NSA_HG, NSA_TQ, LANES), lambda g, i: (g, i, 0)),
                  pl.BlockSpec((1, NSA_TQ, LANES), lambda g, i: (g, i, 0)),
                  pl.BlockSpec((1, 1, nc, NSA_DH), lambda g, i: (0, g, 0, 0)),
                  pl.BlockSpec((1, 1, nc, NSA_DH), lambda g, i: (1, g, 0, 0)),
                  ks_spec, vs_spec, kw_spec, vw_spec,
                  pl.BlockSpec((nc, nb), lambda g, i: (0, 0)),
                  pl.BlockSpec((nb, nb), lambda g, i: (0, 0))],
        out_specs=pl.BlockSpec((NSA_TQ, NSA_HG * NSA_DH), lambda g, i: (i, g)),
        compiler_params=_cparams(("parallel", "arbitrary")),
        name="nsa_attention",
    )(qk_hm, qpad, gates, cmp_kv, cmp_kv, ksx, v1, qk_hm, v1, cis, ltri)


def _rope_tables(pos, dh, reps):
    half = dh // 2
    inv_freq = ROPE_THETA ** (-np.arange(half, dtype=np.float64) / half)
    ang = np.asarray(pos, np.float64)[:, None] * inv_freq[None, :]
    cos = np.cos(ang)
    sin = np.sin(ang)
    cos_t = np.tile(np.concatenate([cos, cos], axis=1), (1, reps))
    sin_t = np.tile(np.concatenate([-sin, sin], axis=1), (1, reps))
    return jnp.asarray(cos_t, F32), jnp.asarray(sin_t, F32)


def _pad_cols(a, width):
    return jnp.pad(a, ((0, 0), (0, width - a.shape[1])))


def _pad_rows(a, height):
    return jnp.pad(a, ((0, height - a.shape[0]), (0, 0)))


def _pack_w_in_kernel(w_ref, xv_ref, o_ref):
    o_ref[...] = _pack_w_in(w_ref[...], xv_ref[...]).astype(o_ref.dtype)


def _pack_w_in_call(w_in, layer, xv):
    _, d, n_in = w_in.shape
    tk = 256
    return pl.pallas_call(
        _pack_w_in_kernel,
        out_shape=jax.ShapeDtypeStruct((d, PROJ_W), BF16),
        grid=(d // tk,),
        in_specs=[pl.BlockSpec((None, tk, n_in), lambda i: (layer, i, 0)),
                  pl.BlockSpec((tk, LANES), lambda i: (i, 0))],
        out_specs=pl.BlockSpec((tk, PROJ_W), lambda i: (i, 0)),
        compiler_params=_cparams(("parallel",)),
        name="pack_w_in",
    )(w_in, xv)


def _pack_w_in(w_l, xv):
    d = w_l.shape[0]
    o = 0
    ret = w_l[:, o:o + 4 * RET_W]; o += 4 * RET_W
    rkv = w_l[:, o:o + 3 * RW_W]; o += 3 * RW_W
    xw = w_l[:, o:o + RW_DECAY_RANK]; o += RW_DECAY_RANK
    xa = w_l[:, o:o + RW_A_RANK]; o += RW_A_RANK
    xg = w_l[:, o:o + RW_GATE_RANK]; o += RW_GATE_RANK
    q = w_l[:, o:o + NSA_W]; o += NSA_W
    kc, vc, ks, vs, kw, vw = (w_l[:, o + i * NSA_KV_W:o + (i + 1) * NSA_KV_W] for i in range(6))
    o += 6 * NSA_KV_W
    gates = w_l[:, o:o + 3 * NSA_HEADS]
    return jnp.concatenate([ret, rkv, _pad_cols(xw, LANES), _pad_cols(xa, LANES), xg, xv,
                            q, ks, kw, kc, vc, vs, vw, _pad_cols(gates, LANES)], axis=1)


def _pack_mix(mix, vres_mix):
    o = 3 * RW_W
    xw = mix[o:o + RW_DECAY_RANK]; o += RW_DECAY_RANK
    xa = mix[o:o + RW_A_RANK]; o += RW_A_RANK
    xg = mix[o:o + RW_GATE_RANK]
    z = lambda n: jnp.zeros((n,), mix.dtype)
    xv = z(LANES) if vres_mix is None else jnp.concatenate([vres_mix, z(LANES - RW_V_RANK)])
    return jnp.concatenate([mix[:3 * RW_W], xw, z(LANES - RW_DECAY_RANK), xa, z(LANES - RW_A_RANK), xg, xv])[None, :]


def kernel(x, ln1_g, w_in, w_in_vres, rwkv_mix, rwkv_vres_mix, rwkv_w0, rwkv_w2, rwkv_a0, rwkv_a2, rwkv_v0, rwkv_v2, rwkv_g2, rwkv_k_k, rwkv_k_a, rwkv_r_k, rwkv_ln_g, rwkv_ln_b, nsa_q_norm, nsa_k_norm, nsa_cmp_pe, nsa_cmp_k_w1, nsa_cmp_k_w2, nsa_cmp_v_w1, nsa_cmp_v_w2, w_out, ln2_g, w_up, w_down):
    bsz, t, d = x.shape
    assert bsz == 1
    depth = w_in.shape[0]
    nc = t // CMP_STRIDE
    nb = t // SLC_BLOCK
    xs = x.reshape(t, d)

    pos = np.arange(t)
    cos_r, sin_r = _rope_tables(pos, RET_DH, 1)
    cos_n, sin_n = _rope_tables(pos, NSA_DH, 2)
    cos_c, sin_c = _rope_tables(np.arange(nc) * CMP_STRIDE + (CMP_BLOCK - 1), NSA_DH, 1)
    lane_head = np.arange(RW_W) // RW_N
    gsum = jnp.asarray(lane_head[:, None] == lane_head[None, :], BF16)
    lane_h2 = np.arange(LANES) // NSA_DH
    gmean = jnp.asarray((lane_h2[:, None] == lane_h2[None, :]) / float(NSA_DH), BF16)
    cstart = np.arange(nc) * CMP_STRIDE
    sstart = np.arange(nb) * SLC_BLOCK
    cis = jnp.asarray((cstart[:, None] <= sstart[None, :] + SLC_BLOCK - 1)
                      & (cstart[:, None] + CMP_BLOCK - 1 >= sstart[None, :]), BF16)

    ltri = jnp.asarray(np.tril(np.ones((nb, nb), np.float32), -1), BF16)

    v_first = jnp.zeros((t, RW_W), F32)
    for l in range(depth):
        vres = l > 0
        xv_w = _pad_cols(w_in_vres[l - 1], LANES) if vres else jnp.zeros((d, LANES), F32)
        w_cat = _pack_w_in_call(w_in, l, xv_w)
        proj = _norm_matmul(xs, ln1_g[l], w_cat, name="in_proj")

        o_ret = _retention(proj, cos_r, sin_r)

        row = lambda a: a.reshape(1, -1)
        mix = _pack_mix(rwkv_mix[l], rwkv_vres_mix[l - 1] if vres else None)
        v0 = row(rwkv_v0[l - 1]) if vres else jnp.zeros((1, RW_W), F32)
        v2 = _pad_rows(rwkv_v2[l - 1], LANES) if vres else jnp.zeros((LANES, RW_W), F32)
        r_, lw_, k_, v_, an_, b_, g_ = _rwkv_pre(
            proj, mix, row(rwkv_w0[l]), _pad_rows(rwkv_w2[l], LANES), row(rwkv_a0[l]), _pad_rows(rwkv_a2[l], LANES),
            v0, v2, rwkv_g2[l], row(rwkv_k_k[l]), row(rwkv_k_a[l]), gsum, v_first, vres)
        if not vres:
            v_first = v_
        o_rwkv = _wkv(r_, lw_, k_, v_, an_, b_, g_, row(rwkv_ln_g[l]), row(rwkv_ln_b[l]), row(rwkv_r_k[l]), gsum)

        n_q = NSA_W // LANES
        n_k = NSA_KV_W // LANES
        w_slabs = jnp.stack([jnp.tile(nsa_q_norm[l] * (NSA_DH ** -0.5), 2)] * n_q
                            + [jnp.tile(nsa_k_norm[l, 1], 2)] * n_k + [jnp.tile(nsa_k_norm[l, 2], 2)] * n_k)[:, None, :]
        s_max = 1.02 * jnp.max(jnp.abs(nsa_q_norm[l])) * jnp.max(jnp.abs(nsa_k_norm[l, 1])) * (NSA_DH ** 0.5)
        big = jnp.full((1, LANES), jnp.exp2(jnp.ceil(jnp.log2(2.0 * s_max + 128.0))), F32)
        qk_hm, hmat, v1, gates, qpad, ksx = _nsa_prep(proj, w_slabs, cos_n, sin_n, gmean, big)
        w1 = jnp.stack([nsa_cmp_k_w1[l], nsa_cmp_v_w1[l]]).astype(BF16)
        w2 = jnp.stack([nsa_cmp_k_w2[l], nsa_cmp_v_w2[l]]).astype(BF16)
        pe = jnp.broadcast_to(nsa_cmp_pe[l].reshape(2, 1, CMP_BLOCK * NSA_DH), (2, 8, CMP_BLOCK * NSA_DH)).astype(BF16)
        cmp_kv = _compress(hmat, w1, w2, pe, row(nsa_k_norm[l, 0]), cos_c, sin_c)
        o_nsa = _nsa_attn(qk_hm, qpad, ksx, gates, cmp_kv, v1, cis, ltri)

        xs = _out_proj(o_ret, o_rwkv, o_nsa, w_out, l, xs)

        up = _norm_matmul(xs, ln2_g[l], w_up, layer=l, act="relu2", out_dtype=BF16, name="mlp_up")
        xs = _matmul(up, w_down, layer=l, res=xs, name="mlp_down")
    return xs.reshape(bsz, t, d)
```

```python
import functools

import numpy as np
import jax
import jax.numpy as jnp
from jax import lax
from jax.experimental import pallas as pl
from jax.experimental.pallas import tpu as pltpu

F32 = jnp.float32
BF16 = jnp.bfloat16

D_MODEL = 2048
RET_HEADS, RET_DH = 4, 128
RET_W = RET_HEADS * RET_DH
RW_HEADS, RW_N = 8, 64
RW_W = RW_HEADS * RW_N
RW_DECAY_RANK, RW_A_RANK, RW_V_RANK, RW_GATE_RANK = 96, 96, 64, 256
RW_GN_EPS = 64e-5
NSA_HEADS, NSA_G, NSA_DH = 16, 4, 64
NSA_HG = NSA_HEADS // NSA_G
NSA_W = NSA_HEADS * NSA_DH
NSA_KV_W = NSA_G * NSA_DH
CMP_BLOCK, CMP_STRIDE, CMP_HIDDEN = 32, 16, 256
SLC_BLOCK, SLC_TOPK, WINDOW = 64, 16, 512
NSA_TQ = 512
ROPE_THETA = 10000.0
MLP_HIDDEN = 4 * D_MODEL
RMS_EPS = 1e-6
NEG_INF = -1e30
FORCED_SCORE = 1e9

LANES = 128
VMEM_LIMIT = 56 * 1024 * 1024

C_RET = 0
C_RW_RKV = 2048
C_RW_XW = 3584
C_RW_XA = 3712
C_RW_XG = 3840
C_RW_XV = 4096
C_NSA_Q = 4224
C_NSA_KS = 5248
C_NSA_KW = 5504
C_NSA_KC = 5760
C_NSA_VC = 6016
C_NSA_VS = 6272
C_NSA_VW = 6528
C_NSA_GATE = 6784
PROJ_W = 6912


def _cparams(sem):
    return pltpu.CompilerParams(dimension_semantics=sem, vmem_limit_bytes=VMEM_LIMIT)


def _dot(a, b):
    return jnp.dot(a, b, preferred_element_type=F32)


def _dot_t(a, b):
    return lax.dot_general(a, b, (((1,), (1,)), ((), ())), preferred_element_type=F32)


def _dot_tl(a, b):
    return lax.dot_general(a, b, (((0,), (0,)), ((), ())), preferred_element_type=F32)


def _split_dot(x, m_bf16, passes):
    hi = x.astype(BF16)
    acc = _dot(hi, m_bf16)
    rem = x - hi.astype(F32)
    for _ in range(passes - 1):
        piece = rem.astype(BF16)
        acc = acc + _dot(piece, m_bf16)
        rem = rem - piece.astype(F32)
    return acc


def _mm_kernel(*refs, nk, act, has_res):
    a_ref, b_ref = refs[:2]
    r_ref = refs[2] if has_res else None
    o_ref = refs[3] if has_res else refs[2]

    def finish(y):
        if act == "relu2":
            y = jnp.square(jnp.maximum(y, 0.0))
        if has_res:
            y = y + r_ref[...]
        o_ref[...] = y.astype(o_ref.dtype)

    part = _dot(a_ref[...], b_ref[...].astype(BF16))
    if nk == 1:
        finish(part)
        return
    acc_ref = refs[-1]
    k = pl.program_id(2)

    @pl.when(k == 0)
    def _():
        acc_ref[...] = part

    @pl.when(k > 0)
    def _():
        acc_ref[...] += part

    @pl.when(k == nk - 1)
    def _():
        finish(acc_ref[...])


def _pick(n, cands):
    for c in cands:
        if n % c == 0:
            return c
    raise ValueError(f"no tile for {n}")


def _weight_spec(b, layer, block, index_map):
    if layer is None:
        return pl.BlockSpec(block, index_map)
    return pl.BlockSpec((None,) + block, lambda *idx: (layer,) + index_map(*idx))


def _matmul(a, b, *, layer=None, res=None, act=None, out_dtype=F32, name="matmul"):
    m, kd = a.shape
    n = b.shape[-1]
    tm = _pick(m, (1024, 512, 256, 128))
    tn = _pick(n, (1024, 768, 512, 256, 128))
    tk = _pick(kd, (2048, 1024, 512))
    nk = kd // tk
    in_specs = [pl.BlockSpec((tm, tk), lambda i, j, k: (i, k)),
                _weight_spec(b, layer, (tk, tn), lambda i, j, k: (k, j))]
    args = [a, b]
    if res is not None:
        in_specs.append(pl.BlockSpec((tm, tn), lambda i, j, k: (i, j)))
        args.append(res)
    return pl.pallas_call(
        functools.partial(_mm_kernel, nk=nk, act=act, has_res=res is not None),
        out_shape=jax.ShapeDtypeStruct((m, n), out_dtype),
        grid=(m // tm, n // tn, nk),
        in_specs=in_specs,
        out_specs=pl.BlockSpec((tm, tn), lambda i, j, k: (i, j)),
        scratch_shapes=[pltpu.VMEM((tm, tn), F32)] if nk > 1 else [],
        compiler_params=_cparams(("parallel", "parallel", "arbitrary")),
        name=name,
    )(*args)


def _norm_mm_kernel(x_ref, g_ref, b_ref, o_ref, h_ref, *, act):
    @pl.when(pl.program_id(1) == 0)
    def _():
        x = x_ref[...]
        ms = jnp.mean(x * x, axis=-1, keepdims=True)
        h_ref[...] = (x * lax.rsqrt(ms + RMS_EPS) * g_ref[...]).astype(BF16)

    y = _dot(h_ref[...], b_ref[...].astype(BF16))
    if act == "relu2":
        y = jnp.square(jnp.maximum(y, 0.0))
    o_ref[...] = y.astype(o_ref.dtype)


def _norm_matmul(x, g, b, *, layer=None, act=None, out_dtype=F32, name="norm_matmul"):
    m, kd = x.shape
    n = b.shape[-1]
    tm = _pick(m, (1024, 512, 256, 128))
    tn = _pick(n, (1024, 768, 512, 256, 128))
    return pl.pallas_call(
        functools.partial(_norm_mm_kernel, act=act),
        out_shape=jax.ShapeDtypeStruct((m, n), out_dtype),
        grid=(m // tm, n // tn),
        in_specs=[pl.BlockSpec((tm, kd), lambda i, j: (i, 0)),
                  pl.BlockSpec((1, kd), lambda i, j: (0, 0)),
                  _weight_spec(b, layer, (kd, tn), lambda i, j: (0, j))],
        out_specs=pl.BlockSpec((tm, tn), lambda i, j: (i, j)),
        scratch_shapes=[pltpu.VMEM((tm, kd), BF16)],
        compiler_params=_cparams(("parallel", "arbitrary")),
        name=name,
    )(x, g.reshape(1, kd), b)


def _out_proj_kernel(ret_ref, rw_ref, nsa_ref, w_ref, res_ref, o_ref, wb_ref):
    @pl.when(pl.program_id(1) == 0)
    def _():
        wb_ref[...] = w_ref[...].astype(BF16)

    k0 = ret_ref.shape[1]
    k1 = k0 + rw_ref.shape[1]
    acc = _dot(ret_ref[...], wb_ref[:k0, :])
    acc = acc + _dot(rw_ref[...], wb_ref[k0:k1, :])
    acc = acc + _dot(nsa_ref[...], wb_ref[k1:, :])
    o_ref[...] = acc + res_ref[...]


def _out_proj(o_ret, o_rwkv, o_nsa, w, layer, res):
    m = o_ret.shape[0]
    kd, n = w.shape[-2:]
    assert kd == o_ret.shape[1] + o_rwkv.shape[1] + o_nsa.shape[1]
    tm = _pick(m, (1024, 512, 256, 128))
    tn = _pick(n, (1024, 512, 256, 128))

    def rows(a):
        return pl.BlockSpec((tm, a.shape[1]), lambda j, i: (i, 0))

    return pl.pallas_call(
        _out_proj_kernel,
        out_shape=jax.ShapeDtypeStruct((m, n), F32),
        grid=(n // tn, m // tm),
        in_specs=[rows(o_ret), rows(o_rwkv), rows(o_nsa),
                  _weight_spec(w, layer, (kd, tn), lambda j, i: (0, j)),
                  pl.BlockSpec((tm, tn), lambda j, i: (i, j))],
        out_specs=pl.BlockSpec((tm, tn), lambda j, i: (i, j)),
        scratch_shapes=[pltpu.VMEM((kd, tn), BF16)],
        compiler_params=_cparams(("parallel", "arbitrary")),
        name="out_proj",
    )(o_ret, o_rwkv, o_nsa, w, res)


RET_CHUNK = 256


def _retention_kernel(q_ref, k_ref, v_ref, g_ref, cos_ref, sin_ref, dmat_ref, qd_ref, kd_ref, cd_ref, o_ref, state_ref):
    @pl.when(pl.program_id(0) == 0)
    def _():
        state_ref[...] = jnp.zeros_like(state_ref)

    cos = cos_ref[...]
    sin = sin_ref[...]

    def rope(x):
        return x * cos + pltpu.roll(x, RET_DH // 2, axis=1) * sin

    heads = range(RET_HEADS)
    hs = [slice(h * RET_DH, (h + 1) * RET_DH) for h in heads]
    q_all, k_all, v_all, gate = q_ref[...], k_ref[...], v_ref[...], g_ref[...]
    qb = [rope(q_all[:, s]).astype(BF16) for s in hs]
    kf = [rope(k_all[:, s]) * (RET_DH ** -0.5) for s in hs]
    kb = [kf[h].astype(BF16) for h in heads]
    vb = [v_all[:, s].astype(BF16) for s in hs]
    inner = [(_dot_t(qb[h], kb[h]) * dmat_ref[h]).astype(BF16) for h in heads]
    cross = [_dot(qb[h], state_ref[h].astype(BF16)) * qd_ref[h] for h in heads]
    kv = [_dot_tl((kf[h] * kd_ref[h]).astype(BF16), vb[h]) for h in heads]
    outs = []
    for h in heads:
        out = _dot(inner[h], vb[h]) + cross[h]
        state_ref[h] = cd_ref[h] * state_ref[h] + kv[h]
        outs.append(out * lax.rsqrt(jnp.mean(out * out, axis=-1, keepdims=True) + RMS_EPS))
    o_ref[...] = (gate * jax.nn.sigmoid(gate) * jnp.concatenate(outs, axis=1)).astype(o_ref.dtype)


def _retention(proj, cos, sin):
    t = proj.shape[0]
    chunk = RET_CHUNK
    log_gamma = np.log(1.0 - 2.0 ** (-5.0 - np.arange(RET_HEADS, dtype=np.float64)))[:, None, None]
    n = np.arange(chunk, dtype=np.float64)
    lag = n[:, None] - n[None, :]
    dmat = np.where(lag >= 0, np.exp(np.maximum(lag, 0.0)[None] * log_gamma), 0.0)
    ones = np.ones((1, 1, RET_DH))
    qd = np.exp((n + 1.0)[None, :, None] * log_gamma) * ones
    kd = np.exp((chunk - 1.0 - n)[None, :, None] * log_gamma) * ones
    cd = np.exp(chunk * log_gamma) * ones
    tables = [jnp.asarray(a, F32) for a in (dmat, qd, kd, cd)]
    base = C_RET // RET_W

    def col(off):
        return pl.BlockSpec((chunk, RET_W), lambda c, off=off: (c, base + off))

    def full(a):
        return pl.BlockSpec(a.shape, lambda c: (0, 0, 0))

    tab = pl.BlockSpec((chunk, RET_DH), lambda c: (c, 0))
    return pl.pallas_call(
        _retention_kernel,
        out_shape=jax.ShapeDtypeStruct((t, RET_W), BF16),
        grid=(t // chunk,),
        in_specs=[col(0), col(1), col(2), col(3), tab, tab] + [full(a) for a in tables],
        out_specs=pl.BlockSpec((chunk, RET_W), lambda c: (c, 0)),
        scratch_shapes=[pltpu.VMEM((RET_HEADS, RET_DH, RET_DH), F32)],
        compiler_params=_cparams(("arbitrary",)),
        name="retention",
    )(proj, proj, proj, proj, cos, sin, *tables)


def _dot_f32(a, b):
    a_hi = a.astype(BF16)
    b_hi = b.astype(BF16)
    a_lo = (a - a_hi.astype(F32)).astype(BF16)
    b_lo = (b - b_hi.astype(F32)).astype(BF16)
    return _dot(a_hi, b_hi) + (_dot(a_hi, b_lo) + _dot(a_lo, b_hi))


RW_PRE_TM = 256
_MIX_OFF = (0, 512, 1024, 1536, 1664, 1792, 2048, 2176)


def _rwkv_pre_kernel(r_ref, k_ref, v_ref, xw_ref, xa_ref, xg_ref, xv_ref,
                     rp_ref, kp_ref, vp_ref, xwp_ref, xap_ref, xgp_ref, xvp_ref,
                     mix_ref, w0_ref, w2_ref, a0_ref, a2_ref, v0_ref, v2_ref, g2_ref, kk_ref, ka_ref,
                     gsum_ref, vfirst_ref,
                     r_out, lw_out, k_out, v_out, an_out, b_out, g_out, *, use_vres):
    i = pl.program_id(0)

    def mixed(cur_ref, prev_ref, seg):
        cur = cur_ref[...]
        mix = mix_ref[:, _MIX_OFF[seg]:_MIX_OFF[seg + 1]]
        prev_row = jnp.where(i > 0, prev_ref[7:8, :], 0.0)
        rid = lax.broadcasted_iota(jnp.int32, cur.shape, 0)
        shifted = jnp.where(rid == 0, prev_row, pltpu.roll(cur, 1, axis=0))
        return cur + mix * (shifted - cur)

    r = mixed(r_ref, rp_ref, 0)
    k = mixed(k_ref, kp_ref, 1)
    v = mixed(v_ref, vp_ref, 2)
    xw = mixed(xw_ref, xwp_ref, 3)
    xa = mixed(xa_ref, xap_ref, 4)
    xg = mixed(xg_ref, xgp_ref, 5)

    z = -(w0_ref[...] + _dot_f32(jnp.tanh(xw), w2_ref[...]))
    softplus = jnp.maximum(z, 0.0) + jnp.log(1.0 + jnp.exp(-jnp.abs(z)))
    w = -softplus - 0.5
    lw_out[...] = -jnp.exp(w)
    a = jax.nn.sigmoid(a0_ref[...] + _dot_f32(xa, a2_ref[...]))
    g_out[...] = _dot_f32(jax.nn.sigmoid(xg), g2_ref[...])
    if use_vres:
        xv = mixed(xv_ref, xvp_ref, 6)
        v = v + (vfirst_ref[...] - v) * jax.nn.sigmoid(v0_ref[...] + _dot_f32(xv, v2_ref[...]))
    kk = k * kk_ref[...]
    ss = _split_dot(kk * kk, gsum_ref[...], 3)
    kk = kk / jnp.maximum(jnp.sqrt(ss), 1e-12)
    r_out[...] = r
    k_out[...] = k * (1.0 + (a - 1.0) * ka_ref[...])
    v_out[...] = v
    an_out[...] = -kk
    b_out[...] = kk * a


def _rwkv_pre(proj, mix, w0, w2, a0, a2, v0, v2, g2, k_k, k_a, gsum, v_first, use_vres):
    t = proj.shape[0]
    tm = RW_PRE_TM

    def cur(width, off):
        return pl.BlockSpec((tm, width), lambda i: (i, off // width))

    def prev(width, off):
        return pl.BlockSpec((8, width), lambda i: (jnp.maximum(i * (tm // 8) - 1, 0), off // width))

    def full(a):
        return pl.BlockSpec(a.shape, lambda i: (0,) * a.ndim)

    segs = [(RW_W, C_RW_RKV), (RW_W, C_RW_RKV + RW_W), (RW_W, C_RW_RKV + 2 * RW_W),
            (128, C_RW_XW), (128, C_RW_XA), (256, C_RW_XG), (128, C_RW_XV)]
    params = [mix, w0, w2, a0, a2, v0, v2, g2, k_k, k_a, gsum]
    row = pl.BlockSpec((tm, RW_W), lambda i: (i, 0))
    outs = pl.pallas_call(
        functools.partial(_rwkv_pre_kernel, use_vres=use_vres),
        out_shape=[jax.ShapeDtypeStruct((t, RW_W), F32)] * 7,
        grid=(t // tm,),
        in_specs=[cur(w, o) for w, o in segs] + [prev(w, o) for w, o in segs] + [full(p) for p in params] + [row],
        out_specs=[row] * 7,
        compiler_params=_cparams(("parallel",)),
        name="rwkv_pre",
    )(*([proj] * 14), *params, v_first)
    return outs


RW_CHUNK = 128


def _wkv_kernel(r_ref, lw_ref, k_ref, v_ref, an_ref, b_ref, g_ref, lng_ref, lnb_ref, rk_ref,
                gsum_ref, tril_ref, o_ref, s_ref):
    c = pl.program_id(0)
    C = RW_CHUNK
    N = RW_N

    @pl.when(c == 0)
    def _():
        s_ref[...] = jnp.zeros_like(s_ref)

    r = r_ref[...]
    lw = lw_ref[...]
    k = k_ref[...]
    v = v_ref[...]
    lw_hi = lw.astype(BF16)
    rem = lw - lw_hi.astype(F32)
    lw_mid = rem.astype(BF16)
    lw_lo = (rem - lw_mid.astype(F32)).astype(BF16)
    tril = tril_ref[...]
    lg = _dot(tril, lw_hi) + (_dot(tril, lw_mid) + _dot(tril, lw_lo))
    gam = jnp.exp(lg)
    ginv = jnp.exp(-lg)
    at = (an_ref[...] * jnp.exp(lg - lw)).astype(BF16)
    bt = (b_ref[...] * ginv).astype(BF16)
    kt = (k * ginv).astype(BF16)
    rt_f = r * gam
    rt = rt_f.astype(BF16)
    vb = v.astype(BF16)
    g_last = gam[C - 1:C, :]

    rowi = lax.broadcasted_iota(jnp.int32, (C, C), 0)
    coli = lax.broadcasted_iota(jnp.int32, (C, C), 1)
    strict = rowi > coli
    incl = rowi >= coli

    heads = range(RW_HEADS)
    hs = [slice(h * N, (h + 1) * N) for h in heads]
    bth = [bt[:, s] for s in hs]
    kth = [kt[:, s] for s in hs]
    vh = [vb[:, s] for s in hs]
    big = [_dot_t(jnp.concatenate([at[:, s], rt[:, s]], axis=0), jnp.concatenate([bth[h], kth[h]], axis=0))
           for h, s in zip(heads, hs)]
    a_ab = [jnp.where(strict, big[h][:C, :C], 0.0) for h in heads]
    a_ak = [jnp.where(strict, big[h][:C, C:], 0.0).astype(BF16) for h in heads]
    a_rb = [jnp.where(incl, big[h][C:, :C], 0.0).astype(BF16) for h in heads]
    a_rk = [jnp.where(incl, big[h][C:, C:], 0.0).astype(BF16) for h in heads]
    akv = [_dot(a_ak[h], vh[h]) for h in heads]
    def same_block(size):
        shift = size.bit_length() - 1
        return (rowi >> shift) == (coli >> shift)

    tinv = [jnp.where(same_block(2), a_ab[h], 0.0) + jnp.where(rowi == coli, 1.0, 0.0) for h in heads]
    size = 2
    while size < C:
        lower_left = same_block(2 * size) & jnp.logical_not(same_block(size))
        off = [jnp.where(lower_left, a_ab[h], 0.0).astype(BF16) for h in heads]
        tb = [tinv[h].astype(BF16) for h in heads]
        half = [_dot(tb[h], off[h]).astype(BF16) for h in heads]
        tinv = [tinv[h] + _dot(half[h], tb[h]) for h in heads]
        size *= 2
    xb = [_dot(tinv[h].astype(BF16), jnp.concatenate([at[:, hs[h]], akv[h].astype(BF16)], axis=1)).astype(BF16)
          for h in heads]
    yx = [_dot(a_rb[h], xb[h]) for h in heads]
    ykv = [_dot(a_rk[h], vh[h]) for h in heads]
    xtb = [_dot_tl(xb[h], bth[h]) for h in heads]
    vtk = [_dot_tl(vh[h], kth[h]) for h in heads]
    ys = []
    for h in heads:
        gl = g_last[:, hs[h]]
        y1 = (rt_f[:, hs[h]] + yx[h][:, :N]).astype(BF16)
        s0 = s_ref[h]
        s0b = s0.astype(BF16)
        ys.append(_dot_t(y1, s0b) + (yx[h][:, N:] + ykv[h]))
        s_ref[h] = (s0 + _dot(s0b, xtb[h][:N].astype(BF16)) + (xtb[h][N:] + vtk[h])) * gl
    y = jnp.concatenate(ys, axis=1)

    gsum = gsum_ref[...]
    inv_n = 1.0 / N
    mu = _split_dot(y, gsum, 3) * inv_n
    yc = y - mu
    var = _split_dot(yc * yc, gsum, 3) * inv_n
    yn = yc * lax.rsqrt(var + RW_GN_EPS) * lng_ref[...] + lnb_ref[...]
    bonus = _split_dot(r * k * rk_ref[...], gsum, 3)
    o_ref[...] = ((yn + bonus * v) * g_ref[...]).astype(o_ref.dtype)


def _wkv(r, lw, k, v, an, b, g, ln_g, ln_b, r_k, gsum):
    t = r.shape[0]
    C = RW_CHUNK
    tril = jnp.asarray(np.tril(np.ones((C, C), np.float32)), BF16)
    row = pl.BlockSpec((C, RW_W), lambda c: (c, 0))
    vec = pl.BlockSpec((1, RW_W), lambda c: (0, 0))
    return pl.pallas_call(
        _wkv_kernel,
        out_shape=jax.ShapeDtypeStruct((t, RW_W), BF16),
        grid=(t // C,),
        in_specs=[row] * 7 + [vec] * 3 + [pl.BlockSpec((RW_W, RW_W), lambda c: (0, 0)),
                                          pl.BlockSpec((C, C), lambda c: (0, 0))],
        out_specs=row,
        scratch_shapes=[pltpu.VMEM((RW_HEADS, RW_N, RW_N), F32)],
        compiler_params=_cparams(("arbitrary",)),
        name="wkv7",
    )(r, lw, k, v, an, b, g, ln_g, ln_b, r_k, gsum, tril)


NSA_PREP_TM = 512


N_NORM_SLABS = (NSA_W + 2 * NSA_KV_W) // LANES
N_RAW_SLABS = 4 * NSA_KV_W // LANES
N_NSA_SLABS = N_NORM_SLABS + N_RAW_SLABS + 1


def _nsa_prep_kernel(*refs):
    x_refs = refs[:N_NSA_SLABS]
    (cos_ref, sin_ref, w_ref, gmean_ref, big_ref,
     qk_ref, hmat_ref, v1_ref, gate_ref, qpad_ref, ksx_ref) = refs[N_NSA_SLABS:]
    cos = cos_ref[...]
    sin = sin_ref[...]
    gmean = gmean_ref[...]
    half = NSA_DH // 2
    tm = cos.shape[0]
    lane = lax.broadcasted_iota(jnp.int32, cos.shape, 1)
    first_half = (lane % NSA_DH) < half
    low = lane < NSA_DH
    blk_row = (pl.program_id(0) * tm + lax.broadcasted_iota(jnp.int32, cos.shape, 0)) // SLC_BLOCK
    big = big_ref[...]
    mark_lo = jnp.where(blk_row == lane - NSA_DH, big, 0.0)
    mark_hi = jnp.where(low & (blk_row == lane + NSA_DH), big, 0.0).astype(BF16)
    n_q = NSA_W // LANES
    n_k = NSA_KV_W // LANES
    for s in range(N_NORM_SLABS):
        x = x_refs[s][...]
        y = x * lax.rsqrt(_split_dot(x * x, gmean, 3) + RMS_EPS) * w_ref[s]
        rot = jnp.where(first_half, pltpu.roll(y, LANES - half, axis=1), pltpu.roll(y, half, axis=1))
        res32 = y * cos + rot * sin
        res = res32.astype(qk_ref.dtype)
        qk_ref[2 * s] = res[:, :NSA_DH]
        qk_ref[2 * s + 1] = res[:, NSA_DH:]
        if s >= n_q + n_k:
            continue
        swapped = pltpu.roll(res32, NSA_DH, axis=1)
        if s < n_q:
            qpad_ref[2 * s] = jnp.where(low, res32, 0.0).astype(BF16)
            qpad_ref[2 * s + 1] = jnp.where(low, swapped, 0.0).astype(BF16)
        else:
            g0 = 2 * (s - n_q)
            ksx_ref[g0, :, :LANES] = jnp.where(low, res32, mark_lo).astype(BF16)
            ksx_ref[g0 + 1, :, :LANES] = jnp.where(low, swapped, mark_lo).astype(BF16)
            ksx_ref[g0, :, LANES:] = mark_hi
            ksx_ref[g0 + 1, :, LANES:] = mark_hi

    rows = hmat_ref.shape[2]
    low_r = lax.broadcasted_iota(jnp.int32, (rows, LANES), 1) < NSA_DH
    for s in range(NSA_KV_W // LANES * 2):
        kind, pair = divmod(s, NSA_KV_W // LANES)
        x_ref = x_refs[N_NORM_SLABS + s]
        for t2 in range(CMP_STRIDE // 2):
            r0 = x_ref[pl.ds(2 * t2, rows, stride=CMP_STRIDE), :]
            r1 = x_ref[pl.ds(2 * t2 + 1, rows, stride=CMP_STRIDE), :]
            cols = slice(t2 * LANES, (t2 + 1) * LANES)
            hmat_ref[kind, 2 * pair, :, cols] = jnp.where(low_r, r0, pltpu.roll(r1, NSA_DH, axis=1)).astype(BF16)
            hmat_ref[kind, 2 * pair + 1, :, cols] = jnp.where(low_r, pltpu.roll(r0, NSA_DH, axis=1), r1).astype(BF16)

    one_col = jnp.where(lane == NSA_DH, 1.0, 0.0)
    for s in range(NSA_KV_W // LANES * 2):
        x = x_refs[N_NORM_SLABS + N_RAW_SLABS // 2 + s][...]
        v1_ref[2 * s] = jnp.where(low, x, one_col).astype(BF16)
        v1_ref[2 * s + 1] = jnp.where(low, pltpu.roll(x, NSA_DH, axis=1), one_col).astype(BF16)

    xg = x_refs[N_NSA_SLABS - 1][...]
    per_group = 3 * NSA_HG
    for g in range(NSA_G):
        shifted = xg if g == 0 else pltpu.roll(xg, LANES - per_group * g, axis=1)
        gate_ref[g] = jnp.where(lane < per_group, shifted, 0.0)


def _nsa_prep(proj, w_slabs, cos, sin, gmean, big):
    t = proj.shape[0]
    assert t // SLC_BLOCK <= LANES
    tm = NSA_PREP_TM
    base = C_NSA_Q // LANES
    tab = pl.BlockSpec((tm, LANES), lambda i: (i, 0))
    n_heads_out = 2 * N_NORM_SLABS
    return pl.pallas_call(
        _nsa_prep_kernel,
        out_shape=[jax.ShapeDtypeStruct((n_heads_out, t, NSA_DH), BF16),
                   jax.ShapeDtypeStruct((2, NSA_G, t // CMP_STRIDE, CMP_STRIDE * NSA_DH), BF16),
                   jax.ShapeDtypeStruct((2 * NSA_G, t, LANES), BF16),
                   jax.ShapeDtypeStruct((NSA_G, t, LANES), F32),
                   jax.ShapeDtypeStruct((NSA_HEADS, t, LANES), BF16),
                   jax.ShapeDtypeStruct((NSA_G, t, 2 * LANES), BF16)],
        grid=(t // tm,),
        in_specs=[pl.BlockSpec((tm, LANES), lambda i, s=s: (i, base + s)) for s in range(N_NSA_SLABS)]
        + [tab, tab, pl.BlockSpec((N_NORM_SLABS, 1, LANES), lambda i: (0, 0, 0)),
           pl.BlockSpec((LANES, LANES), lambda i: (0, 0)), pl.BlockSpec((1, LANES), lambda i: (0, 0))],
        out_specs=[pl.BlockSpec((n_heads_out, tm, NSA_DH), lambda i: (0, i, 0)),
                   pl.BlockSpec((2, NSA_G, tm // CMP_STRIDE, CMP_STRIDE * NSA_DH), lambda i: (0, 0, i, 0)),
                   pl.BlockSpec((2 * NSA_G, tm, LANES), lambda i: (0, i, 0)),
                   pl.BlockSpec((NSA_G, tm, LANES), lambda i: (0, i, 0)),
                   pl.BlockSpec((NSA_HEADS, tm, LANES), lambda i: (0, i, 0)),
                   pl.BlockSpec((NSA_G, tm, 2 * LANES), lambda i: (0, i, 0))],
        compiler_params=_cparams(("parallel",)),
        name="nsa_prep",
    )(*([proj] * N_NSA_SLABS), cos, sin, w_slabs, gmean, big)


def _compress_kernel(h_ref, w1_ref, w2_ref, pe_ref, nw_ref, cos_ref, sin_ref, o_ref):
    kind = pl.program_id(0)
    nc = h_ref.shape[2]
    half_in = CMP_STRIDE * NSA_DH
    hm = h_ref[0, 0]
    w1 = w1_ref[0]
    first = _dot(hm, w1[:half_in])
    second = _dot(hm, w1[half_in:])
    const = _dot(pe_ref[0], w1)[0:1, :]
    pre = first + pltpu.roll(second, nc - 1, axis=0) + const
    hid = 0.5 * pre * (1.0 + jnp.tanh(0.7978845608028654 * (pre + 0.044715 * pre * pre * pre)))
    out = _dot(hid.astype(BF16), w2_ref[0])
    y = out * lax.rsqrt(jnp.mean(out * out, axis=-1, keepdims=True) + RMS_EPS) * nw_ref[...]
    half = NSA_DH // 2
    rot = jnp.concatenate([y[:, half:], y[:, :half]], axis=1)
    roped = y * cos_ref[...] + rot * sin_ref[...]
    o_ref[0, 0] = jnp.where(kind == 0, roped, out).astype(o_ref.dtype)


def _compress(hmat, w1, w2, pe, nw, cos_c, sin_c):
    _, g, nc, width = hmat.shape
    return pl.pallas_call(
        _compress_kernel,
        out_shape=jax.ShapeDtypeStruct((2, g, nc, NSA_DH), BF16),
        grid=(2, g),
        in_specs=[pl.BlockSpec((1, 1, nc, width), lambda a, b: (a, b, 0, 0)),
                  pl.BlockSpec((1, 2 * width, CMP_HIDDEN), lambda a, b: (a, 0, 0)),
                  pl.BlockSpec((1, CMP_HIDDEN, NSA_DH), lambda a, b: (a, 0, 0)),
                  pl.BlockSpec((1, 8, 2 * width), lambda a, b: (a, 0, 0)),
                  pl.BlockSpec((1, NSA_DH), lambda a, b: (0, 0)),
                  pl.BlockSpec((nc, NSA_DH), lambda a, b: (0, 0)),
                  pl.BlockSpec((nc, NSA_DH), lambda a, b: (0, 0))],
        out_specs=pl.BlockSpec((1, 1, nc, NSA_DH), lambda a, b: (a, b, 0, 0)),
        compiler_params=_cparams(("parallel", "parallel")),
        name="nsa_compress",
    )(hmat, w1, w2, pe, nw, cos_c, sin_c)


SLC_KT = 1024
WIN_ROWS = 128
WIN_KEYS = WINDOW + WIN_ROWS


INT32_MIN = -2 ** 31


def _nsa_attn_kernel(q_ref, qpad_ref, gate_ref, kc_ref, vc_ref, ks_ref, vs_ref, kw_ref, vw_ref, cis_ref, ltri_ref,
                     o_ref):
    qi = pl.program_id(1)
    nc = kc_ref.shape[2]
    nb = cis_ref.shape[1]
    heads = range(NSA_HG)
    start = qi * NSA_TQ
    qh = [q_ref[hh] for hh in heads]
    tok = start + lax.broadcasted_iota(jnp.int32, (NSA_TQ, 1), 0)

    def softmax_terms(scores):
        return [jnp.exp(s - jnp.max(s, axis=-1, keepdims=True)) for s in scores]

    def cmp_branch(width):
        kc = kc_ref[0, 0, :width, :]
        vc = vc_ref[0, 0, :width, :]
        cend = lax.broadcasted_iota(jnp.int32, (1, width), 1) * CMP_STRIDE + (CMP_BLOCK - 1)
        bias_c = jnp.where(cend <= tok, 0.0, NEG_INF)
        ec = softmax_terms([_dot_t(qh[hh], kc) + bias_c for hh in heads])
        sees_any = tok >= CMP_BLOCK - 1
        pc = [ec[hh] * jnp.where(sees_any, 1.0 / jnp.sum(ec[hh], axis=-1, keepdims=True), 0.0) for hh in heads]
        outs_c = tuple(_dot(pc[hh].astype(BF16), vc) for hh in heads)
        psum = pc[0]
        for hh in range(1, NSA_HG):
            psum = psum + pc[hh]
        return outs_c, _split_dot(psum, cis_ref[:width, :], 3)

    n_quarters = 4
    quarter = nc // n_quarters
    n_visible = (start + NSA_TQ - CMP_BLOCK) // CMP_STRIDE + 1
    which = jnp.clip((n_visible - 1) // quarter, 0, n_quarters - 1)
    o_c, score = lax.switch(which, [functools.partial(cmp_branch, (r + 1) * quarter) for r in range(n_quarters)])
    blk = lax.broadcasted_iota(jnp.int32, (1, nb), 1)
    cur = tok // SLC_BLOCK
    forced = (blk == 0) | (blk == cur) | (blk == cur - 1)
    score = jnp.where(forced, FORCED_SCORE, score)
    score = jnp.where(blk <= cur, score, -jnp.inf)


    bits = lax.bitcast_convert_type(score.T, jnp.int32)
    key = bits ^ ((bits >> 31) & 0x7FFFFFFF)
    n_sel = min(SLC_TOPK, nb)

    def enough(c):
        return jnp.sum(jnp.where(key >= c, 1.0, 0.0), axis=0, keepdims=True) >= n_sel

    def enough3(c1, c2, c3):
        packed = jnp.where(key >= c3, 65793.0, jnp.where(key >= c2, 257.0, jnp.where(key >= c1, 1.0, 0.0)))
        tot = jnp.sum(packed, axis=0, keepdims=True).astype(jnp.int32)
        return (tot & 255) >= n_sel, ((tot >> 8) & 255) >= n_sel, (tot >> 16) >= n_sel

    zero_row = jnp.zeros((1, NSA_TQ), jnp.int32)
    thr = jnp.where(enough(zero_row), zero_row, INT32_MIN)
    for hi in range(30, 0, -2):
        c1 = thr + (1 << (hi - 1))
        c2 = thr + (1 << hi)
        c3 = c2 + (1 << (hi - 1))
        e1, e2, e3 = enough3(c1, c2, c3)
        thr = jnp.where(e3, c3, jnp.where(e2, c2, jnp.where(e1, c1, thr)))
    c1 = thr + 1
    thr = jnp.where(enough(c1), c1, thr)
    above = key > thr
    tied = key == thr
    need = n_sel - jnp.sum(jnp.where(above, 1.0, 0.0), axis=0, keepdims=True)
    tied_before = _dot(ltri_ref[...], jnp.where(tied, 1.0, 0.0).astype(BF16))
    sel_t = above | (tied & (tied_before < need))
    sel = jnp.where(sel_t, 1.0, 0.0).T
    if nb < LANES:
        sel = jnp.concatenate([sel, jnp.zeros((NSA_TQ, LANES - nb), F32)], axis=1)
    lane_q = lax.broadcasted_iota(jnp.int32, (NSA_TQ, LANES), 1)
    sel_sw = pltpu.roll(sel, NSA_DH, axis=1)
    sel_lo = jnp.where(lane_q >= NSA_DH, sel_sw, 0.0)
    sel_hi = jnp.where(lane_q < NSA_DH, sel_sw, 0.0)
    qx = [jnp.concatenate([qpad_ref[hh].astype(F32) + sel_lo, sel_hi], axis=1).astype(BF16) for hh in heads]

    def window_rows(r0):
        kbase = pl.multiple_of(jnp.maximum(start + r0 - WINDOW, 0), WIN_ROWS)
        kw = kw_ref[0, pl.ds(kbase, WIN_KEYS), :]
        vw = vw_ref[0, pl.ds(kbase, WIN_KEYS), :]
        lag = tok[r0:r0 + WIN_ROWS] - (kbase + lax.broadcasted_iota(jnp.int32, (1, WIN_KEYS), 1))
        bias_w = jnp.where((lag >= 0) & (lag < WINDOW), 0.0, NEG_INF)
        sw = [_dot_t(qh[hh][r0:r0 + WIN_ROWS], kw) + bias_w for hh in heads]
        ew = [jnp.exp((s - jnp.max(s, axis=-1, keepdims=True)).astype(BF16)) for s in sw]
        return [_dot(ew[hh], vw) for hh in heads]

    win_parts = [window_rows(r0) for r0 in range(0, NSA_TQ, WIN_ROWS)]
    acc_w = [jnp.concatenate([part[hh] for part in win_parts], axis=0) for hh in heads]

    n_tiles = (start + NSA_TQ + SLC_KT - 1) // SLC_KT

    def sel_tile(j, carry, causal, width=SLC_KT):
        ms, accs = carry
        k0 = pl.multiple_of(j * SLC_KT, SLC_KT)
        kt = ks_ref[0, pl.ds(k0, width), :]
        vt = vs_ref[0, pl.ds(k0, width), :]
        sj = [_dot_t(qx[hh], kt) for hh in heads]
        if causal:
            future = (k0 + lax.broadcasted_iota(jnp.int32, (1, width), 1)) > tok
            sj = [jnp.where(future, NEG_INF, s) for s in sj]
        m_new = [jnp.maximum(ms[hh], jnp.max(sj[hh], axis=-1, keepdims=True)) for hh in heads]
        pj = [jnp.exp((sj[hh] - m_new[hh]).astype(BF16)) for hh in heads]
        accs = [jnp.exp(ms[hh] - m_new[hh]) * accs[hh] + _dot(pj[hh], vt) for hh in heads]
        return tuple(m_new), tuple(accs)

    m0 = tuple(jnp.full((NSA_TQ, 1), NEG_INF, F32) for _ in heads)
    a0 = tuple(jnp.zeros((NSA_TQ, LANES), F32) for _ in heads)
    carry = lax.fori_loop(0, n_tiles - 1, functools.partial(sel_tile, causal=False), (m0, a0))
    per_tile = SLC_KT // NSA_TQ
    last = [functools.partial(sel_tile, n_tiles - 1, causal=True, width=(r + 1) * NSA_TQ) for r in range(per_tile)]
    _, acc_s = lax.switch(qi % per_tile, last, carry)

    gt = jax.nn.sigmoid(gate_ref[0])
    outs = []
    for hh in heads:
        g0, g1, g2 = (gt[:, 3 * hh + br:3 * hh + br + 1] for br in range(3))
        scale_s = g1 / acc_s[hh][:, NSA_DH:NSA_DH + 1]
        scale_w = g2 / acc_w[hh][:, NSA_DH:NSA_DH + 1]
        outs.append(g0 * o_c[hh] + scale_s * acc_s[hh][:, :NSA_DH] + scale_w * acc_w[hh][:, :NSA_DH])
    o_ref[...] = jnp.concatenate(outs, axis=1).astype(o_ref.dtype)


def _nsa_attn(qk_hm, qpad, ksx, gates, cmp_kv, v1, cis, ltri):
    t = qk_hm.shape[1]
    nq = t // NSA_TQ
    nc = cmp_kv.shape[2]
    nb = cis.shape[1]
    ks_spec = pl.BlockSpec((1, t, 2 * LANES), lambda g, i: (g, 0, 0))
    kw_spec = pl.BlockSpec((1, t, NSA_DH), lambda g, i: (NSA_HEADS + NSA_G + g, 0, 0))
    vs_spec = pl.BlockSpec((1, t, LANES), lambda g, i: (g, 0, 0))
    vw_spec = pl.BlockSpec((1, t, LANES), lambda g, i: (NSA_G + g, 0, 0))
    return pl.pallas_call(
        _nsa_attn_kernel,
        out_shape=jax.ShapeDtypeStruct((t, NSA_W), BF16),
        grid=(NSA_G, nq),
        in_specs=[pl.BlockSpec((NSA_HG, NSA_TQ, NSA_DH), lambda g, i: (g, i, 0)),
                  pl.BlockSpec((NSA_HG, NSA_TQ, LANES), lambda g, i: (g, i, 0)),
                  pl.BlockSpec((1, NSA_TQ, LANES), lambda g, i: (g, i, 0)),
                  pl.BlockSpec((1, 1, nc, NSA_DH), lambda g, i: (0, g, 0, 0)),
                  pl.BlockSpec((1, 1, nc, NSA_DH), lambda g, i: (1, g, 0, 0)),
                  ks_spec, vs_spec, kw_spec, vw_spec,
                  pl.BlockSpec((nc, nb), lambda g, i: (0, 0)),
                  pl.BlockSpec((nb, nb), lambda g, i: (0, 0))],
        out_specs=pl.BlockSpec((NSA_TQ, NSA_HG * NSA_DH), lambda g, i: (i, g)),
        compiler_params=_cparams(("parallel", "arbitrary")),
        name="nsa_attention",
    )(qk_hm, qpad, gates, cmp_kv, cmp_kv, ksx, v1, qk_hm, v1, cis, ltri)


def _rope_tables(pos, dh, reps):
    half = dh // 2
    inv_freq = ROPE_THETA ** (-np.arange(half, dtype=np.float64) / half)
    ang = np.asarray(pos, np.float64)[:, None] * inv_freq[None, :]
    cos = np.cos(ang)
    sin = np.sin(ang)
    cos_t = np.tile(np.concatenate([cos, cos], axis=1), (1, reps))
    sin_t = np.tile(np.concatenate([-sin, sin], axis=1), (1, reps))
    return jnp.asarray(cos_t, F32), jnp.asarray(sin_t, F32)


def _pad_cols(a, width):
    return jnp.pad(a, ((0, 0), (0, width - a.shape[1])))


def _pad_rows(a, height):
    return jnp.pad(a, ((0, height - a.shape[0]), (0, 0)))


def _pack_w_in_kernel(w_ref, xv_ref, o_ref):
    o_ref[...] = _pack_w_in(w_ref[...], xv_ref[...]).astype(o_ref.dtype)


def _pack_w_in_call(w_in, layer, xv):
    _, d, n_in = w_in.shape
    tk = 256
    return pl.pallas_call(
        _pack_w_in_kernel,
        out_shape=jax.ShapeDtypeStruct((d, PROJ_W), BF16),
        grid=(d // tk,),
        in_specs=[pl.BlockSpec((None, tk, n_in), lambda i: (layer, i, 0)),
                  pl.BlockSpec((tk, LANES), lambda i: (i, 0))],
        out_specs=pl.BlockSpec((tk, PROJ_W), lambda i: (i, 0)),
        compiler_params=_cparams(("parallel",)),
        name="pack_w_in",
    )(w_in, xv)


def _pack_w_in(w_l, xv):
    d = w_l.shape[0]
    o = 0
    ret = w_l[:, o:o + 4 * RET_W]; o += 4 * RET_W
    rkv = w_l[:, o:o + 3 * RW_W]; o += 3 * RW_W
    xw = w_l[:, o:o + RW_DECAY_RANK]; o += RW_DECAY_RANK
    xa = w_l[:, o:o + RW_A_RANK]; o += RW_A_RANK
    xg = w_l[:, o:o + RW_GATE_RANK]; o += RW_GATE_RANK
    q = w_l[:, o:o + NSA_W]; o += NSA_W
    kc, vc, ks, vs, kw, vw = (w_l[:, o + i * NSA_KV_W:o + (i + 1) * NSA_KV_W] for i in range(6))
    o += 6 * NSA_KV_W
    gates = w_l[:, o:o + 3 * NSA_HEADS]
    return jnp.concatenate([ret, rkv, _pad_cols(xw, LANES), _pad_cols(xa, LANES), xg, xv,
                            q, ks, kw, kc, vc, vs, vw, _pad_cols(gates, LANES)], axis=1)


def _pack_mix(mix, vres_mix):
    o = 3 * RW_W
    xw = mix[o:o + RW_DECAY_RANK]; o += RW_DECAY_RANK
    xa = mix[o:o + RW_A_RANK]; o += RW_A_RANK
    xg = mix[o:o + RW_GATE_RANK]
    z = lambda n: jnp.zeros((n,), mix.dtype)
    xv = z(LANES) if vres_mix is None else jnp.concatenate([vres_mix, z(LANES - RW_V_RANK)])
    return jnp.concatenate([mix[:3 * RW_W], xw, z(LANES - RW_DECAY_RANK), xa, z(LANES - RW_A_RANK), xg, xv])[None, :]


def kernel(x, ln1_g, w_in, w_in_vres, rwkv_mix, rwkv_vres_mix, rwkv_w0, rwkv_w2, rwkv_a0, rwkv_a2, rwkv_v0, rwkv_v2, rwkv_g2, rwkv_k_k, rwkv_k_a, rwkv_r_k, rwkv_ln_g, rwkv_ln_b, nsa_q_norm, nsa_k_norm, nsa_cmp_pe, nsa_cmp_k_w1, nsa_cmp_k_w2, nsa_cmp_v_w1, nsa_cmp_v_w2, w_out, ln2_g, w_up, w_down):
    bsz, t, d = x.shape
    assert bsz == 1
    depth = w_in.shape[0]
    nc = t // CMP_STRIDE
    nb = t // SLC_BLOCK
    xs = x.reshape(t, d)

    pos = np.arange(t)
    cos_r, sin_r = _rope_tables(pos, RET_DH, 1)
    cos_n, sin_n = _rope_tables(pos, NSA_DH, 2)
    cos_c, sin_c = _rope_tables(np.arange(nc) * CMP_STRIDE + (CMP_BLOCK - 1), NSA_DH, 1)
    lane_head = np.arange(RW_W) // RW_N
    gsum = jnp.asarray(lane_head[:, None] == lane_head[None, :], BF16)
    lane_h2 = np.arange(LANES) // NSA_DH
    gmean = jnp.asarray((lane_h2[:, None] == lane_h2[None, :]) / float(NSA_DH), BF16)
    cstart = np.arange(nc) * CMP_STRIDE
    sstart = np.arange(nb) * SLC_BLOCK
    cis = jnp.asarray((cstart[:, None] <= sstart[None, :] + SLC_BLOCK - 1)
                      & (cstart[:, None] + CMP_BLOCK - 1 >= sstart[None, :]), BF16)

    ltri = jnp.asarray(np.tril(np.ones((nb, nb), np.float32), -1), BF16)

    v_first = jnp.zeros((t, RW_W), F32)
    for l in range(depth):
        vres = l > 0
        xv_w = _pad_cols(w_in_vres[l - 1], LANES) if vres else jnp.zeros((d, LANES), F32)
        w_cat = _pack_w_in_call(w_in, l, xv_w)
        proj = _norm_matmul(xs, ln1_g[l], w_cat, name="in_proj")

        o_ret = _retention(proj, cos_r, sin_r)

        row = lambda a: a.reshape(1, -1)
        mix = _pack_mix(rwkv_mix[l], rwkv_vres_mix[l - 1] if vres else None)
        v0 = row(rwkv_v0[l - 1]) if vres else jnp.zeros((1, RW_W), F32)
        v2 = _pad_rows(rwkv_v2[l - 1], LANES) if vres else jnp.zeros((LANES, RW_W), F32)
        r_, lw_, k_, v_, an_, b_, g_ = _rwkv_pre(
            proj, mix, row(rwkv_w0[l]), _pad_rows(rwkv_w2[l], LANES), row(rwkv_a0[l]), _pad_rows(rwkv_a2[l], LANES),
            v0, v2, rwkv_g2[l], row(rwkv_k_k[l]), row(rwkv_k_a[l]), gsum, v_first, vres)
        if not vres:
            v_first = v_
        o_rwkv = _wkv(r_, lw_, k_, v_, an_, b_, g_, row(rwkv_ln_g[l]), row(rwkv_ln_b[l]), row(rwkv_r_k[l]), gsum)

        n_q = NSA_W // LANES
        n_k = NSA_KV_W // LANES
        w_slabs = jnp.stack([jnp.tile(nsa_q_norm[l] * (NSA_DH ** -0.5), 2)] * n_q
                            + [jnp.tile(nsa_k_norm[l, 1], 2)] * n_k + [jnp.tile(nsa_k_norm[l, 2], 2)] * n_k)[:, None, :]
        s_max = 1.02 * jnp.max(jnp.abs(nsa_q_norm[l])) * jnp.max(jnp.abs(nsa_k_norm[l, 1])) * (NSA_DH ** 0.5)
        big = jnp.full((1, LANES), jnp.exp2(jnp.ceil(jnp.log2(2.0 * s_max + 128.0))), F32)
        qk_hm, hmat, v1, gates, qpad, ksx = _nsa_prep(proj, w_slabs, cos_n, sin_n, gmean, big)
        w1 = jnp.stack([nsa_cmp_k_w1[l], nsa_cmp_v_w1[l]]).astype(BF16)
        w2 = jnp.stack([nsa_cmp_k_w2[l], nsa_cmp_v_w2[l]]).astype(BF16)
        pe = jnp.broadcast_to(nsa_cmp_pe[l].reshape(2, 1, CMP_BLOCK * NSA_DH), (2, 8, CMP_BLOCK * NSA_DH)).astype(BF16)
        cmp_kv = _compress(hmat, w1, w2, pe, row(nsa_k_norm[l, 0]), cos_c, sin_c)
        o_nsa = _nsa_attn(qk_hm, qpad, ksx, gates, cmp_kv, v1, cis, ltri)

        xs = _out_proj(o_ret, o_rwkv, o_nsa, w_out, l, xs)

        up = _norm_matmul(xs, ln2_g[l], w_up, layer=l, act="relu2", out_dtype=BF16, name="mlp_up")
        xs = _matmul(up, w_down, layer=l, res=xs, name="mlp_down")
    return xs.reshape(bsz, t, d)
```

```python
import functools

import numpy as np
import jax
import jax.numpy as jnp
from jax import lax
from jax.experimental import pallas as pl
from jax.experimental.pallas import tpu as pltpu

F32 = jnp.float32
BF16 = jnp.bfloat16

D_MODEL = 2048
RET_HEADS, RET_DH = 4, 128
RET_W = RET_HEADS * RET_DH
RW_HEADS, RW_N = 8, 64
RW_W = RW_HEADS * RW_N
RW_DECAY_RANK, RW_A_RANK, RW_V_RANK, RW_GATE_RANK = 96, 96, 64, 256
RW_GN_EPS = 64e-5
NSA_HEADS, NSA_G, NSA_DH = 16, 4, 64
NSA_HG = NSA_HEADS // NSA_G
NSA_W = NSA_HEADS * NSA_DH
NSA_KV_W = NSA_G * NSA_DH
CMP_BLOCK, CMP_STRIDE, CMP_HIDDEN = 32, 16, 256
SLC_BLOCK, SLC_TOPK, WINDOW = 64, 16, 512
NSA_TQ = 512
ROPE_THETA = 10000.0
MLP_HIDDEN = 4 * D_MODEL
RMS_EPS = 1e-6
NEG_INF = -1e30
FORCED_SCORE = 1e9

LANES = 128
VMEM_LIMIT = 56 * 1024 * 1024

C_RET = 0
C_RW_RKV = 2048
C_RW_XW = 3584
C_RW_XA = 3712
C_RW_XG = 3840
C_RW_XV = 4096
C_NSA_Q = 4224
C_NSA_KS = 5248
C_NSA_KW = 5504
C_NSA_KC = 5760
C_NSA_VC = 6016
C_NSA_VS = 6272
C_NSA_VW = 6528
C_NSA_GATE = 6784
PROJ_W = 6912


def _cparams(sem):
    return pltpu.CompilerParams(dimension_semantics=sem, vmem_limit_bytes=VMEM_LIMIT)


def _dot(a, b):
    return jnp.dot(a, b, preferred_element_type=F32)


def _dot_t(a, b):
    return lax.dot_general(a, b, (((1,), (1,)), ((), ())), preferred_element_type=F32)


def _dot_tl(a, b):
    return lax.dot_general(a, b, (((0,), (0,)), ((), ())), preferred_element_type=F32)


def _split_dot(x, m_bf16, passes):
    hi = x.astype(BF16)
    acc = _dot(hi, m_bf16)
    rem = x - hi.astype(F32)
    for _ in range(passes - 1):
        piece = rem.astype(BF16)
        acc = acc + _dot(piece, m_bf16)
        rem = rem - piece.astype(F32)
    return acc


def _mm_kernel(*refs, nk, act, has_res):
    a_ref, b_ref = refs[:2]
    r_ref = refs[2] if has_res else None
    o_ref = refs[3] if has_res else refs[2]

    def finish(y):
        if act == "relu2":
            y = jnp.square(jnp.maximum(y, 0.0))
        if has_res:
            y = y + r_ref[...]
        o_ref[...] = y.astype(o_ref.dtype)

    part = _dot(a_ref[...], b_ref[...].astype(BF16))
    if nk == 1:
        finish(part)
        return
    acc_ref = refs[-1]
    k = pl.program_id(2)

    @pl.when(k == 0)
    def _():
        acc_ref[...] = part

    @pl.when(k > 0)
    def _():
        acc_ref[...] += part

    @pl.when(k == nk - 1)
    def _():
        finish(acc_ref[...])


def _pick(n, cands):
    for c in cands:
        if n % c == 0:
            return c
    raise ValueError(f"no tile for {n}")


def _weight_spec(b, layer, block, index_map):
    if layer is None:
        return pl.BlockSpec(block, index_map)
    return pl.BlockSpec((None,) + block, lambda *idx: (layer,) + index_map(*idx))


def _matmul(a, b, *, layer=None, res=None, act=None, out_dtype=F32, name="matmul"):
    m, kd = a.shape
    n = b.shape[-1]
    tm = _pick(m, (1024, 512, 256, 128))
    tn = _pick(n, (1024, 768, 512, 256, 128))
    tk = _pick(kd, (2048, 1024, 512))
    nk = kd // tk
    in_specs = [pl.BlockSpec((tm, tk), lambda i, j, k: (i, k)),
                _weight_spec(b, layer, (tk, tn), lambda i, j, k: (k, j))]
    args = [a, b]
    if res is not None:
        in_specs.append(pl.BlockSpec((tm, tn), lambda i, j, k: (i, j)))
        args.append(res)
    return pl.pallas_call(
        functools.partial(_mm_kernel, nk=nk, act=act, has_res=res is not None),
        out_shape=jax.ShapeDtypeStruct((m, n), out_dtype),
        grid=(m // tm, n // tn, nk),
        in_specs=in_specs,
        out_specs=pl.BlockSpec((tm, tn), lambda i, j, k: (i, j)),
        scratch_shapes=[pltpu.VMEM((tm, tn), F32)] if nk > 1 else [],
        compiler_params=_cparams(("parallel", "parallel", "arbitrary")),
        name=name,
    )(*args)


def _norm_mm_kernel(x_ref, g_ref, b_ref, o_ref, h_ref, *, act):
    @pl.when(pl.program_id(1) == 0)
    def _():
        x = x_ref[...]
        ms = jnp.mean(x * x, axis=-1, keepdims=True)
        h_ref[...] = (x * lax.rsqrt(ms + RMS_EPS) * g_ref[...]).astype(BF16)

    y = _dot(h_ref[...], b_ref[...].astype(BF16))
    if act == "relu2":
        y = jnp.square(jnp.maximum(y, 0.0))
    o_ref[...] = y.astype(o_ref.dtype)


def _norm_matmul(x, g, b, *, layer=None, act=None, out_dtype=F32, name="norm_matmul"):
    m, kd = x.shape
    n = b.shape[-1]
    tm = _pick(m, (1024, 512, 256, 128))
    tn = _pick(n, (1024, 768, 512, 256, 128))
    return pl.pallas_call(
        functools.partial(_norm_mm_kernel, act=act),
        out_shape=jax.ShapeDtypeStruct((m, n), out_dtype),
        grid=(m // tm, n // tn),
        in_specs=[pl.BlockSpec((tm, kd), lambda i, j: (i, 0)),
                  pl.BlockSpec((1, kd), lambda i, j: (0, 0)),
                  _weight_spec(b, layer, (kd, tn), lambda i, j: (0, j))],
        out_specs=pl.BlockSpec((tm, tn), lambda i, j: (i, j)),
        scratch_shapes=[pltpu.VMEM((tm, kd), BF16)],
        compiler_params=_cparams(("parallel", "arbitrary")),
        name=name,
    )(x, g.reshape(1, kd), b)


def _out_proj_kernel(ret_ref, rw_ref, nsa_ref, w_ref, res_ref, o_ref, wb_ref):
    @pl.when(pl.program_id(1) == 0)
    def _():
        wb_ref[...] = w_ref[...].astype(BF16)

    k0 = ret_ref.shape[1]
    k1 = k0 + rw_ref.shape[1]
    acc = _dot(ret_ref[...], wb_ref[:k0, :])
    acc = acc + _dot(rw_ref[...], wb_ref[k0:k1, :])
    acc = acc + _dot(nsa_ref[...], wb_ref[k1:, :])
    o_ref[...] = acc + res_ref[...]


def _out_proj(o_ret, o_rwkv, o_nsa, w, layer, res):
    m = o_ret.shape[0]
    kd, n = w.shape[-2:]
    assert kd == o_ret.shape[1] + o_rwkv.shape[1] + o_nsa.shape[1]
    tm = _pick(m, (1024, 512, 256, 128))
    tn = _pick(n, (1024, 512, 256, 128))

    def rows(a):
        return pl.BlockSpec((tm, a.shape[1]), lambda j, i: (i, 0))

    return pl.pallas_call(
        _out_proj_kernel,
        out_shape=jax.ShapeDtypeStruct((m, n), F32),
        grid=(n // tn, m // tm),
        in_specs=[rows(o_ret), rows(o_rwkv), rows(o_nsa),
                  _weight_spec(w, layer, (kd, tn), lambda j, i: (0, j)),
                  pl.BlockSpec((tm, tn), lambda j, i: (i, j))],
        out_specs=pl.BlockSpec((tm, tn), lambda j, i: (i, j)),
        scratch_shapes=[pltpu.VMEM((kd, tn), BF16)],
        compiler_params=_cparams(("parallel", "arbitrary")),
        name="out_proj",
    )(o_ret, o_rwkv, o_nsa, w, res)


RET_CHUNK = 512


def _retention_kernel(q_ref, k_ref, v_ref, g_ref, cos_ref, sin_ref, dmat_ref, qd_ref, kd_ref, cd_ref, o_ref, state_ref):
    @pl.when(pl.program_id(0) == 0)
    def _():
        state_ref[...] = jnp.zeros_like(state_ref)

    cos = cos_ref[...]
    sin = sin_ref[...]

    def rope(x):
        return x * cos + pltpu.roll(x, RET_DH // 2, axis=1) * sin

    heads = range(RET_HEADS)
    hs = [slice(h * RET_DH, (h + 1) * RET_DH) for h in heads]
    q_all, k_all, v_all, gate = q_ref[...], k_ref[...], v_ref[...], g_ref[...]
    qb = [rope(q_all[:, s]).astype(BF16) for s in hs]
    kf = [rope(k_all[:, s]) * (RET_DH ** -0.5) for s in hs]
    kb = [kf[h].astype(BF16) for h in heads]
    vb = [v_all[:, s].astype(BF16) for s in hs]
    inner = [(_dot_t(qb[h], kb[h]) * dmat_ref[h]).astype(BF16) for h in heads]
    cross = [_dot(qb[h], state_ref[h].astype(BF16)) * qd_ref[h] for h in heads]
    kv = [_dot_tl((kf[h] * kd_ref[h]).astype(BF16), vb[h]) for h in heads]
    outs = []
    for h in heads:
        out = _dot(inner[h], vb[h]) + cross[h]
        state_ref[h] = cd_ref[h] * state_ref[h] + kv[h]
        outs.append(out * lax.rsqrt(jnp.mean(out * out, axis=-1, keepdims=True) + RMS_EPS))
    o_ref[...] = (gate * jax.nn.sigmoid(gate) * jnp.concatenate(outs, axis=1)).astype(o_ref.dtype)


def _retention(proj, cos, sin):
    t = proj.shape[0]
    chunk = RET_CHUNK
    log_gamma = np.log(1.0 - 2.0 ** (-5.0 - np.arange(RET_HEADS, dtype=np.float64)))[:, None, None]
    n = np.arange(chunk, dtype=np.float64)
    lag = n[:, None] - n[None, :]
    dmat = np.where(lag >= 0, np.exp(np.maximum(lag, 0.0)[None] * log_gamma), 0.0)
    ones = np.ones((1, 1, RET_DH))
    qd = np.exp((n + 1.0)[None, :, None] * log_gamma) * ones
    kd = np.exp((chunk - 1.0 - n)[None, :, None] * log_gamma) * ones
    cd = np.exp(chunk * log_gamma) * ones
    tables = [jnp.asarray(a, F32) for a in (dmat, qd, kd, cd)]
    base = C_RET // RET_W

    def col(off):
        return pl.BlockSpec((chunk, RET_W), lambda c, off=off: (c, base + off))

    def full(a):
        return pl.BlockSpec(a.shape, lambda c: (0, 0, 0))

    tab = pl.BlockSpec((chunk, RET_DH), lambda c: (c, 0))
    return pl.pallas_call(
        _retention_kernel,
        out_shape=jax.ShapeDtypeStruct((t, RET_W), BF16),
        grid=(t // chunk,),
        in_specs=[col(0), col(1), col(2), col(3), tab, tab] + [full(a) for a in tables],
        out_specs=pl.BlockSpec((chunk, RET_W), lambda c: (c, 0)),
        scratch_shapes=[pltpu.VMEM((RET_HEADS, RET_DH, RET_DH), F32)],
        compiler_params=_cparams(("arbitrary",)),
        name="retention",
    )(proj, proj, proj, proj, cos, sin, *tables)


def _dot_f32(a, b):
    a_hi = a.astype(BF16)
    b_hi = b.astype(BF16)
    a_lo = (a - a_hi.astype(F32)).astype(BF16)
    b_lo = (b - b_hi.astype(F32)).astype(BF16)
    return _dot(a_hi, b_hi) + (_dot(a_hi, b_lo) + _dot(a_lo, b_hi))


RW_PRE_TM = 512
_MIX_OFF = (0, 512, 1024, 1536, 1664, 1792, 2048, 2176)


def _rwkv_pre_kernel(r_ref, k_ref, v_ref, xw_ref, xa_ref, xg_ref, xv_ref,
                     rp_ref, kp_ref, vp_ref, xwp_ref, xap_ref, xgp_ref, xvp_ref,
                     mix_ref, w0_ref, w2_ref, a0_ref, a2_ref, v0_ref, v2_ref, g2_ref, kk_ref, ka_ref,
                     gsum_ref, vfirst_ref,
                     r_out, lw_out, k_out, v_out, an_out, b_out, g_out, *, use_vres):
    i = pl.program_id(0)

    def mixed(cur_ref, prev_ref, seg):
        cur = cur_ref[...]
        mix = mix_ref[:, _MIX_OFF[seg]:_MIX_OFF[seg + 1]]
        prev_row = jnp.where(i > 0, prev_ref[7:8, :], 0.0)
        rid = lax.broadcasted_iota(jnp.int32, cur.shape, 0)
        shifted = jnp.where(rid == 0, prev_row, pltpu.roll(cur, 1, axis=0))
        return cur + mix * (shifted - cur)

    r = mixed(r_ref, rp_ref, 0)
    k = mixed(k_ref, kp_ref, 1)
    v = mixed(v_ref, vp_ref, 2)
    xw = mixed(xw_ref, xwp_ref, 3)
    xa = mixed(xa_ref, xap_ref, 4)
    xg = mixed(xg_ref, xgp_ref, 5)

    z = -(w0_ref[...] + _dot_f32(jnp.tanh(xw), w2_ref[...]))
    softplus = jnp.maximum(z, 0.0) + jnp.log(1.0 + jnp.exp(-jnp.abs(z)))
    w = -softplus - 0.5
    lw_out[...] = -jnp.exp(w)
    a = jax.nn.sigmoid(a0_ref[...] + _dot_f32(xa, a2_ref[...]))
    g_out[...] = _dot_f32(jax.nn.sigmoid(xg), g2_ref[...])
    if use_vres:
        xv = mixed(xv_ref, xvp_ref, 6)
        v = v + (vfirst_ref[...] - v) * jax.nn.sigmoid(v0_ref[...] + _dot_f32(xv, v2_ref[...]))
    kk = k * kk_ref[...]
    ss = _split_dot(kk * kk, gsum_ref[...], 2)
    kk = kk / jnp.maximum(jnp.sqrt(ss), 1e-12)
    r_out[...] = r
    k_out[...] = k * (1.0 + (a - 1.0) * ka_ref[...])
    v_out[...] = v
    an_out[...] = -kk
    b_out[...] = kk * a


def _rwkv_pre(proj, mix, w0, w2, a0, a2, v0, v2, g2, k_k, k_a, gsum, v_first, use_vres):
    t = proj.shape[0]
    tm = RW_PRE_TM

    def cur(width, off):
        return pl.BlockSpec((tm, width), lambda i: (i, off // width))

    def prev(width, off):
        return pl.BlockSpec((8, width), lambda i: (jnp.maximum(i * (tm // 8) - 1, 0), off // width))

    def full(a):
        return pl.BlockSpec(a.shape, lambda i: (0,) * a.ndim)

    segs = [(RW_W, C_RW_RKV), (RW_W, C_RW_RKV + RW_W), (RW_W, C_RW_RKV + 2 * RW_W),
            (128, C_RW_XW), (128, C_RW_XA), (256, C_RW_XG), (128, C_RW_XV)]
    params = [mix, w0, w2, a0, a2, v0, v2, g2, k_k, k_a, gsum]
    row = pl.BlockSpec((tm, RW_W), lambda i: (i, 0))
    outs = pl.pallas_call(
        functools.partial(_rwkv_pre_kernel, use_vres=use_vres),
        out_shape=[jax.ShapeDtypeStruct((t, RW_W), F32)] * 7,
        grid=(t // tm,),
        in_specs=[cur(w, o) for w, o in segs] + [prev(w, o) for w, o in segs] + [full(p) for p in params] + [row],
        out_specs=[row] * 7,
        compiler_params=_cparams(("parallel",)),
        name="rwkv_pre",
    )(*([proj] * 14), *params, v_first)
    return outs


RW_CHUNK = 128


def _wkv_kernel(r_ref, lw_ref, k_ref, v_ref, an_ref, b_ref, g_ref, lng_ref, lnb_ref, rk_ref,
                gsum_ref, tril_ref, o_ref, s_ref):
    c = pl.program_id(0)
    C = RW_CHUNK
    N = RW_N

    @pl.when(c == 0)
    def _():
        s_ref[...] = jnp.zeros_like(s_ref)

    r = r_ref[...]
    lw = lw_ref[...]
    k = k_ref[...]
    v = v_ref[...]
    lw_hi = lw.astype(BF16)
    rem = lw - lw_hi.astype(F32)
    lw_mid = rem.astype(BF16)
    lw_lo = (rem - lw_mid.astype(F32)).astype(BF16)
    tril = tril_ref[...]
    lg = _dot(tril, lw_hi) + (_dot(tril, lw_mid) + _dot(tril, lw_lo))
    gam = jnp.exp(lg)
    ginv = jnp.exp(-lg)
    at = (an_ref[...] * jnp.exp(lg - lw)).astype(BF16)
    bt = (b_ref[...] * ginv).astype(BF16)
    kt = (k * ginv).astype(BF16)
    rt_f = r * gam
    rt = rt_f.astype(BF16)
    vb = v.astype(BF16)
    g_last = gam[C - 1:C, :]

    rowi = lax.broadcasted_iota(jnp.int32, (C, C), 0)
    coli = lax.broadcasted_iota(jnp.int32, (C, C), 1)
    strict = rowi > coli
    incl = rowi >= coli

    heads = range(RW_HEADS)
    hs = [slice(h * N, (h + 1) * N) for h in heads]
    bth = [bt[:, s] for s in hs]
    kth = [kt[:, s] for s in hs]
    vh = [vb[:, s] for s in hs]
    big = [_dot_t(jnp.concatenate([at[:, s], rt[:, s]], axis=0), jnp.concatenate([bth[h], kth[h]], axis=0))
           for h, s in zip(heads, hs)]
    a_ab = [jnp.where(strict, big[h][:C, :C], 0.0) for h in heads]
    a_ak = [jnp.where(strict, big[h][:C, C:], 0.0).astype(BF16) for h in heads]
    a_rb = [jnp.where(incl, big[h][C:, :C], 0.0).astype(BF16) for h in heads]
    a_rk = [jnp.where(incl, big[h][C:, C:], 0.0).astype(BF16) for h in heads]
    akv = [_dot(a_ak[h], vh[h]) for h in heads]
    def same_block(size):
        shift = size.bit_length() - 1
        return (rowi >> shift) == (coli >> shift)

    tinv = [jnp.where(same_block(2), a_ab[h], 0.0) + jnp.where(rowi == coli, 1.0, 0.0) for h in heads]
    size = 2
    while size < C:
        lower_left = same_block(2 * size) & jnp.logical_not(same_block(size))
        off = [jnp.where(lower_left, a_ab[h], 0.0).astype(BF16) for h in heads]
        tb = [tinv[h].astype(BF16) for h in heads]
        half = [_dot(tb[h], off[h]).astype(BF16) for h in heads]
        tinv = [tinv[h] + _dot(half[h], tb[h]) for h in heads]
        size *= 2
    xb = [_dot(tinv[h].astype(BF16), jnp.concatenate([at[:, hs[h]], akv[h].astype(BF16)], axis=1)).astype(BF16)
          for h in heads]
    yx = [_dot(a_rb[h], xb[h]) for h in heads]
    ykv = [_dot(a_rk[h], vh[h]) for h in heads]
    xtb = [_dot_tl(xb[h], bth[h]) for h in heads]
    vtk = [_dot_tl(vh[h], kth[h]) for h in heads]
    ys = []
    for h in heads:
        gl = g_last[:, hs[h]]
        y1 = (rt_f[:, hs[h]] + yx[h][:, :N]).astype(BF16)
        s0 = s_ref[h]
        s0b = s0.astype(BF16)
        ys.append(_dot_t(y1, s0b) + (yx[h][:, N:] + ykv[h]))
        s_ref[h] = (s0 + _dot(s0b, xtb[h][:N].astype(BF16)) + (xtb[h][N:] + vtk[h])) * gl
    y = jnp.concatenate(ys, axis=1)

    gsum = gsum_ref[...]
    inv_n = 1.0 / N
    mu = _split_dot(y, gsum, 2) * inv_n
    yc = y - mu
    var = _split_dot(yc * yc, gsum, 2) * inv_n
    yn = yc * lax.rsqrt(var + RW_GN_EPS) * lng_ref[...] + lnb_ref[...]
    bonus = _split_dot(r * k * rk_ref[...], gsum, 2)
    o_ref[...] = ((yn + bonus * v) * g_ref[...]).astype(o_ref.dtype)


def _wkv(r, lw, k, v, an, b, g, ln_g, ln_b, r_k, gsum):
    t = r.shape[0]
    C = RW_CHUNK
    tril = jnp.asarray(np.tril(np.ones((C, C), np.float32)), BF16)
    row = pl.BlockSpec((C, RW_W), lambda c: (c, 0))
    vec = pl.BlockSpec((1, RW_W), lambda c: (0, 0))
    return pl.pallas_call(
        _wkv_kernel,
        out_shape=jax.ShapeDtypeStruct((t, RW_W), BF16),
        grid=(t // C,),
        in_specs=[row] * 7 + [vec] * 3 + [pl.BlockSpec((RW_W, RW_W), lambda c: (0, 0)),
                                          pl.BlockSpec((C, C), lambda c: (0, 0))],
        out_specs=row,
        scratch_shapes=[pltpu.VMEM((RW_HEADS, RW_N, RW_N), F32)],
        compiler_params=_cparams(("arbitrary",)),
        name="wkv7",
    )(r, lw, k, v, an, b, g, ln_g, ln_b, r_k, gsum, tril)


NSA_PREP_TM = 512


N_NORM_SLABS = (NSA_W + 2 * NSA_KV_W) // LANES
N_RAW_SLABS = 4 * NSA_KV_W // LANES
N_NSA_SLABS = N_NORM_SLABS + N_RAW_SLABS + 1


def _nsa_prep_kernel(*refs):
    x_refs = refs[:N_NSA_SLABS]
    (cos_ref, sin_ref, w_ref, gmean_ref, big_ref,
     qk_ref, hmat_ref, v1_ref, gate_ref, qpad_ref, ksx_ref) = refs[N_NSA_SLABS:]
    cos = cos_ref[...]
    sin = sin_ref[...]
    gmean = gmean_ref[...]
    half = NSA_DH // 2
    tm = cos.shape[0]
    lane = lax.broadcasted_iota(jnp.int32, cos.shape, 1)
    first_half = (lane % NSA_DH) < half
    low = lane < NSA_DH
    blk_row = (pl.program_id(0) * tm + lax.broadcasted_iota(jnp.int32, cos.shape, 0)) // SLC_BLOCK
    big = big_ref[...]
    mark_lo = jnp.where(blk_row == lane - NSA_DH, big, 0.0)
    mark_hi = jnp.where(low & (blk_row == lane + NSA_DH), big, 0.0).astype(BF16)
    n_q = NSA_W // LANES
    n_k = NSA_KV_W // LANES
    for s in range(N_NORM_SLABS):
        x = x_refs[s][...]
        y = x * lax.rsqrt(_split_dot(x * x, gmean, 2) + RMS_EPS) * w_ref[s]
        rot = jnp.where(first_half, pltpu.roll(y, LANES - half, axis=1), pltpu.roll(y, half, axis=1))
        res32 = y * cos + rot * sin
        res = res32.astype(qk_ref.dtype)
        qk_ref[2 * s] = res[:, :NSA_DH]
        qk_ref[2 * s + 1] = res[:, NSA_DH:]
        if s >= n_q + n_k:
            continue
        swapped = pltpu.roll(res32, NSA_DH, axis=1)
        if s < n_q:
            qpad_ref[2 * s] = jnp.where(low, res32, 0.0).astype(BF16)
            qpad_ref[2 * s + 1] = jnp.where(low, swapped, 0.0).astype(BF16)
        else:
            g0 = 2 * (s - n_q)
            ksx_ref[g0, :, :LANES] = jnp.where(low, res32, mark_lo).astype(BF16)
            ksx_ref[g0 + 1, :, :LANES] = jnp.where(low, swapped, mark_lo).astype(BF16)
            ksx_ref[g0, :, LANES:] = mark_hi
            ksx_ref[g0 + 1, :, LANES:] = mark_hi

    rows = hmat_ref.shape[2]
    low_r = lax.broadcasted_iota(jnp.int32, (rows, LANES), 1) < NSA_DH
    for s in range(NSA_KV_W // LANES * 2):
        kind, pair = divmod(s, NSA_KV_W // LANES)
        x_ref = x_refs[N_NORM_SLABS + s]
        for t2 in range(CMP_STRIDE // 2):
            r0 = x_ref[pl.ds(2 * t2, rows, stride=CMP_STRIDE), :]
            r1 = x_ref[pl.ds(2 * t2 + 1, rows, stride=CMP_STRIDE), :]
            cols = slice(t2 * LANES, (t2 + 1) * LANES)
            hmat_ref[kind, 2 * pair, :, cols] = jnp.where(low_r, r0, pltpu.roll(r1, NSA_DH, axis=1)).astype(BF16)
            hmat_ref[kind, 2 * pair + 1, :, cols] = jnp.where(low_r, pltpu.roll(r0, NSA_DH, axis=1), r1).astype(BF16)

    one_col = jnp.where(lane == NSA_DH, 1.0, 0.0)
    for s in range(NSA_KV_W // LANES * 2):
        x = x_refs[N_NORM_SLABS + N_RAW_SLABS // 2 + s][...]
        v1_ref[2 * s] = jnp.where(low, x, one_col).astype(BF16)
        v1_ref[2 * s + 1] = jnp.where(low, pltpu.roll(x, NSA_DH, axis=1), one_col).astype(BF16)

    xg = x_refs[N_NSA_SLABS - 1][...]
    per_group = 3 * NSA_HG
    for g in range(NSA_G):
        shifted = xg if g == 0 else pltpu.roll(xg, LANES - per_group * g, axis=1)
        gate_ref[g] = jnp.where(lane < per_group, shifted, 0.0)


def _nsa_prep(proj, w_slabs, cos, sin, gmean, big):
    t = proj.shape[0]
    assert t // SLC_BLOCK <= LANES
    tm = NSA_PREP_TM
    base = C_NSA_Q // LANES
    tab = pl.BlockSpec((tm, LANES), lambda i: (i, 0))
    n_heads_out = 2 * N_NORM_SLABS
    return pl.pallas_call(
        _nsa_prep_kernel,
        out_shape=[jax.ShapeDtypeStruct((n_heads_out, t, NSA_DH), BF16),
                   jax.ShapeDtypeStruct((2, NSA_G, t // CMP_STRIDE, CMP_STRIDE * NSA_DH), BF16),
                   jax.ShapeDtypeStruct((2 * NSA_G, t, LANES), BF16),
                   jax.ShapeDtypeStruct((NSA_G, t, LANES), F32),
                   jax.ShapeDtypeStruct((NSA_HEADS, t, LANES), BF16),
                   jax.ShapeDtypeStruct((NSA_G, t, 2 * LANES), BF16)],
        grid=(t // tm,),
        in_specs=[pl.BlockSpec((tm, LANES), lambda i, s=s: (i, base + s)) for s in range(N_NSA_SLABS)]
        + [tab, tab, pl.BlockSpec((N_NORM_SLABS, 1, LANES), lambda i: (0, 0, 0)),
           pl.BlockSpec((LANES, LANES), lambda i: (0, 0)), pl.BlockSpec((1, LANES), lambda i: (0, 0))],
        out_specs=[pl.BlockSpec((n_heads_out, tm, NSA_DH), lambda i: (0, i, 0)),
                   pl.BlockSpec((2, NSA_G, tm // CMP_STRIDE, CMP_STRIDE * NSA_DH), lambda i: (0, 0, i, 0)),
                   pl.BlockSpec((2 * NSA_G, tm, LANES), lambda i: (0, i, 0)),
                   pl.BlockSpec((NSA_G, tm, LANES), lambda i: (0, i, 0)),
                   pl.BlockSpec((NSA_HEADS, tm, LANES), lambda i: (0, i, 0)),
                   pl.BlockSpec((NSA_G, tm, 2 * LANES), lambda i: (0, i, 0))],
        compiler_params=_cparams(("parallel",)),
        name="nsa_prep",
    )(*([proj] * N_NSA_SLABS), cos, sin, w_slabs, gmean, big)


def _compress_kernel(h_ref, w1_ref, w2_ref, pe_ref, nw_ref, cos_ref, sin_ref, o_ref):
    kind = pl.program_id(0)
    nc = h_ref.shape[2]
    half_in = CMP_STRIDE * NSA_DH
    hm = h_ref[0, 0]
    w1 = w1_ref[0]
    first = _dot(hm, w1[:half_in])
    second = _dot(hm, w1[half_in:])
    const = _dot(pe_ref[0], w1)[0:1, :]
    pre = first + pltpu.roll(second, nc - 1, axis=0) + const
    hid = 0.5 * pre * (1.0 + jnp.tanh(0.7978845608028654 * (pre + 0.044715 * pre * pre * pre)))
    out = _dot(hid.astype(BF16), w2_ref[0])
    y = out * lax.rsqrt(jnp.mean(out * out, axis=-1, keepdims=True) + RMS_EPS) * nw_ref[...]
    half = NSA_DH // 2
    rot = jnp.concatenate([y[:, half:], y[:, :half]], axis=1)
    roped = y * cos_ref[...] + rot * sin_ref[...]
    o_ref[0, 0] = jnp.where(kind == 0, roped, out).astype(o_ref.dtype)


def _compress(hmat, w1, w2, pe, nw, cos_c, sin_c):
    _, g, nc, width = hmat.shape
    return pl.pallas_call(
        _compress_kernel,
        out_shape=jax.ShapeDtypeStruct((2, g, nc, NSA_DH), BF16),
        grid=(2, g),
        in_specs=[pl.BlockSpec((1, 1, nc, width), lambda a, b: (a, b, 0, 0)),
                  pl.BlockSpec((1, 2 * width, CMP_HIDDEN), lambda a, b: (a, 0, 0)),
                  pl.BlockSpec((1, CMP_HIDDEN, NSA_DH), lambda a, b: (a, 0, 0)),
                  pl.BlockSpec((1, 8, 2 * width), lambda a, b: (a, 0, 0)),
                  pl.BlockSpec((1, NSA_DH), lambda a, b: (0, 0)),
                  pl.BlockSpec((nc, NSA_DH), lambda a, b: (0, 0)),
                  pl.BlockSpec((nc, NSA_DH), lambda a, b: (0, 0))],
        out_specs=pl.BlockSpec((1, 1, nc, NSA_DH), lambda a, b: (a, b, 0, 0)),
        compiler_params=_cparams(("parallel", "parallel")),
        name="nsa_compress",
    )(hmat, w1, w2, pe, nw, cos_c, sin_c)


SLC_KT = 1024
WIN_ROWS = 128
WIN_KEYS = WINDOW + WIN_ROWS


INT32_MIN = -2 ** 31


def _nsa_attn_kernel(q_ref, qpad_ref, gate_ref, kc_ref, vc_ref, ks_ref, vs_ref, kw_ref, vw_ref, cis_ref, ltri_ref,
                     o_ref):
    qi = pl.program_id(1)
    nc = kc_ref.shape[2]
    nb = cis_ref.shape[1]
    heads = range(NSA_HG)
    start = qi * NSA_TQ
    qh = [q_ref[hh] for hh in heads]
    tok = start + lax.broadcasted_iota(jnp.int32, (NSA_TQ, 1), 0)

    def softmax_terms(scores):
        return [jnp.exp(s - jnp.max(s, axis=-1, keepdims=True)) for s in scores]

    def cmp_branch(width):
        kc = kc_ref[0, 0, :width, :]
        vc = vc_ref[0, 0, :width, :]
        cend = lax.broadcasted_iota(jnp.int32, (1, width), 1) * CMP_STRIDE + (CMP_BLOCK - 1)
        bias_c = jnp.where(cend <= tok, 0.0, NEG_INF)
        ec = softmax_terms([_dot_t(qh[hh], kc) + bias_c for hh in heads])
        sees_any = tok >= CMP_BLOCK - 1
        pc = [ec[hh] * jnp.where(sees_any, 1.0 / jnp.sum(ec[hh], axis=-1, keepdims=True), 0.0) for hh in heads]
        outs_c = tuple(_dot(pc[hh].astype(BF16), vc) for hh in heads)
        psum = pc[0]
        for hh in range(1, NSA_HG):
            psum = psum + pc[hh]
        return outs_c, _split_dot(psum, cis_ref[:width, :], 3)

    n_quarters = 4
    quarter = nc // n_quarters
    n_visible = (start + NSA_TQ - CMP_BLOCK) // CMP_STRIDE + 1
    which = jnp.clip((n_visible - 1) // quarter, 0, n_quarters - 1)
    o_c, score = lax.switch(which, [functools.partial(cmp_branch, (r + 1) * quarter) for r in range(n_quarters)])
    blk = lax.broadcasted_iota(jnp.int32, (1, nb), 1)
    cur = tok // SLC_BLOCK
    forced = (blk == 0) | (blk == cur) | (blk == cur - 1)
    score = jnp.where(forced, FORCED_SCORE, score)
    score = jnp.where(blk <= cur, score, -jnp.inf)


    bits = lax.bitcast_convert_type(score.T, jnp.int32)
    key = bits ^ ((bits >> 31) & 0x7FFFFFFF)
    n_sel = min(SLC_TOPK, nb)

    def enough(c):
        return jnp.sum(jnp.where(key >= c, 1.0, 0.0), axis=0, keepdims=True) >= n_sel

    def enough3(c1, c2, c3):
        packed = jnp.where(key >= c3, 65793.0, jnp.where(key >= c2, 257.0, jnp.where(key >= c1, 1.0, 0.0)))
        tot = jnp.sum(packed, axis=0, keepdims=True).astype(jnp.int32)
        return (tot & 255) >= n_sel, ((tot >> 8) & 255) >= n_sel, (tot >> 16) >= n_sel

    zero_row = jnp.zeros((1, NSA_TQ), jnp.int32)
    thr = jnp.where(enough(zero_row), zero_row, INT32_MIN)
    for hi in range(30, 0, -2):
        c1 = thr + (1 << (hi - 1))
        c2 = thr + (1 << hi)
        c3 = c2 + (1 << (hi - 1))
        e1, e2, e3 = enough3(c1, c2, c3)
        thr = jnp.where(e3, c3, jnp.where(e2, c2, jnp.where(e1, c1, thr)))
    c1 = thr + 1
    thr = jnp.where(enough(c1), c1, thr)
    above = key > thr
    tied = key == thr
    need = n_sel - jnp.sum(jnp.where(above, 1.0, 0.0), axis=0, keepdims=True)
    tied_before = _dot(ltri_ref[...], jnp.where(tied, 1.0, 0.0).astype(BF16))
    sel_t = above | (tied & (tied_before < need))
    sel = jnp.where(sel_t, 1.0, 0.0).T
    if nb < LANES:
        sel = jnp.concatenate([sel, jnp.zeros((NSA_TQ, LANES - nb), F32)], axis=1)
    lane_q = lax.broadcasted_iota(jnp.int32, (NSA_TQ, LANES), 1)
    sel_sw = pltpu.roll(sel, NSA_DH, axis=1)
    sel_lo = jnp.where(lane_q >= NSA_DH, sel_sw, 0.0)
    sel_hi = jnp.where(lane_q < NSA_DH, sel_sw, 0.0)
    qx = [jnp.concatenate([qpad_ref[hh].astype(F32) + sel_lo, sel_hi], axis=1).astype(BF16) for hh in heads]

    def window_rows(r0):
        kbase = pl.multiple_of(jnp.maximum(start + r0 - WINDOW, 0), WIN_ROWS)
        kw = kw_ref[0, pl.ds(kbase, WIN_KEYS), :]
        vw = vw_ref[0, pl.ds(kbase, WIN_KEYS), :]
        lag = tok[r0:r0 + WIN_ROWS] - (kbase + lax.broadcasted_iota(jnp.int32, (1, WIN_KEYS), 1))
        bias_w = jnp.where((lag >= 0) & (lag < WINDOW), 0.0, NEG_INF)
        sw = [_dot_t(qh[hh][r0:r0 + WIN_ROWS], kw) + bias_w for hh in heads]
        ew = [jnp.exp((s - jnp.max(s, axis=-1, keepdims=True)).astype(BF16)) for s in sw]
        return [_dot(ew[hh], vw) for hh in heads]

    win_parts = [window_rows(r0) for r0 in range(0, NSA_TQ, WIN_ROWS)]
    acc_w = [jnp.concatenate([part[hh] for part in win_parts], axis=0) for hh in heads]

    n_tiles = (start + NSA_TQ + SLC_KT - 1) // SLC_KT

    def sel_tile(j, carry, causal, width=SLC_KT):
        ms, accs = carry
        k0 = pl.multiple_of(j * SLC_KT, SLC_KT)
        kt = ks_ref[0, pl.ds(k0, width), :]
        vt = vs_ref[0, pl.ds(k0, width), :]
        sj = [_dot_t(qx[hh], kt) for hh in heads]
        if causal:
            future = (k0 + lax.broadcasted_iota(jnp.int32, (1, width), 1)) > tok
            sj = [jnp.where(future, NEG_INF, s) for s in sj]
        m_new = [jnp.maximum(ms[hh], jnp.max(sj[hh], axis=-1, keepdims=True)) for hh in heads]
        pj = [jnp.exp((sj[hh] - m_new[hh]).astype(BF16)) for hh in heads]
        accs = [jnp.exp(ms[hh] - m_new[hh]) * accs[hh] + _dot(pj[hh], vt) for hh in heads]
        return tuple(m_new), tuple(accs)

    m0 = tuple(jnp.full((NSA_TQ, 1), NEG_INF, F32) for _ in heads)
    a0 = tuple(jnp.zeros((NSA_TQ, LANES), F32) for _ in heads)
    carry = lax.fori_loop(0, n_tiles - 1, functools.partial(sel_tile, causal=False), (m0, a0))
    per_tile = SLC_KT // NSA_TQ
    last = [functools.partial(sel_tile, n_tiles - 1, causal=True, width=(r + 1) * NSA_TQ) for r in range(per_tile)]
    _, acc_s = lax.switch(qi % per_tile, last, carry)

    gt = jax.nn.sigmoid(gate_ref[0])
    outs = []
    for hh in heads:
        g0, g1, g2 = (gt[:, 3 * hh + br:3 * hh + br + 1] for br in range(3))
        scale_s = g1 / acc_s[hh][:, NSA_DH:NSA_DH + 1]
        scale_w = g2 / acc_w[hh][:, NSA_DH:NSA_DH + 1]
        outs.append(g0 * o_c[hh] + scale_s * acc_s[hh][:, :NSA_DH] + scale_w * acc_w[hh][:, :NSA_DH])
    o_ref[...] = jnp.concatenate(outs, axis=1).astype(o_ref.dtype)


def _nsa_attn(qk_hm, qpad, ksx, gates, cmp_kv, v1, cis, ltri):
    t = qk_hm.shape[1]
    nq = t // NSA_TQ
    nc = cmp_kv.shape[2]
    nb = cis.shape[1]
    ks_spec = pl.BlockSpec((1, t, 2 * LANES), lambda g, i: (g, 0, 0))
    kw_spec = pl.BlockSpec((1, t, NSA_DH), lambda g, i: (NSA_HEADS + NSA_G + g, 0, 0))
    vs_spec = pl.BlockSpec((1, t, LANES), lambda g, i: (g, 0, 0))
    vw_spec = pl.BlockSpec((1, t, LANES), lambda g, i: (NSA_G + g, 0, 0))
    return pl.pallas_call(
        _nsa_attn_kernel,
        out_shape=jax.ShapeDtypeStruct((t, NSA_W), BF16),
        grid=(NSA_G, nq),
        in_specs=[pl.BlockSpec((NSA_HG, NSA_TQ, NSA_DH), lambda g, i: (g, i, 0)),
                  pl.BlockSpec((NSA_HG, NSA_TQ, LANES), lambda g, i: (g, i, 0)),
                  pl.BlockSpec((1, NSA_TQ, LANES), lambda g, i: (g, i, 0)),
                  pl.BlockSpec((1, 1, nc, NSA_DH), lambda g, i: (0, g, 0, 0)),
                  pl.BlockSpec((1, 1, nc, NSA_DH), lambda g, i: (1, g, 0, 0)),
                  ks_spec, vs_spec, kw_spec, vw_spec,
                  pl.BlockSpec((nc, nb), lambda g, i: (0, 0)),
                  pl.BlockSpec((nb, nb), lambda g, i: (0, 0))],
        out_specs=pl.BlockSpec((NSA_TQ, NSA_HG * NSA_DH), lambda g, i: (i, g)),
        compiler_params=_cparams(("parallel", "arbitrary")),
        name="nsa_attention",
    )(qk_hm, qpad, gates, cmp_kv, cmp_kv, ksx, v1, qk_hm, v1, cis, ltri)


def _rope_tables(pos, dh, reps):
    half = dh // 2
    inv_freq = ROPE_THETA ** (-np.arange(half, dtype=np.float64) / half)
    ang = np.asarray(pos, np.float64)[:, None] * inv_freq[None, :]
    cos = np.cos(ang)
    sin = np.sin(ang)
    cos_t = np.tile(np.concatenate([cos, cos], axis=1), (1, reps))
    sin_t = np.tile(np.concatenate([-sin, sin], axis=1), (1, reps))
    return jnp.asarray(cos_t, F32), jnp.asarray(sin_t, F32)


def _pad_cols(a, width):
    return jnp.pad(a, ((0, 0), (0, width - a.shape[1])))


def _pad_rows(a, height):
    return jnp.pad(a, ((0, height - a.shape[0]), (0, 0)))


def _pack_w_in_kernel(w_ref, xv_ref, o_ref):
    o_ref[...] = _pack_w_in(w_ref[...], xv_ref[...]).astype(o_ref.dtype)


def _pack_w_in_call(w_in, layer, xv):
    _, d, n_in = w_in.shape
    tk = 256
    return pl.pallas_call(
        _pack_w_in_kernel,
        out_shape=jax.ShapeDtypeStruct((d, PROJ_W), BF16),
        grid=(d // tk,),
        in_specs=[pl.BlockSpec((None, tk, n_in), lambda i: (layer, i, 0)),
                  pl.BlockSpec((tk, LANES), lambda i: (i, 0))],
        out_specs=pl.BlockSpec((tk, PROJ_W), lambda i: (i, 0)),
        compiler_params=_cparams(("parallel",)),
        name="pack_w_in",
    )(w_in, xv)


def _pack_w_in(w_l, xv):
    d = w_l.shape[0]
    o = 0
    ret = w_l[:, o:o + 4 * RET_W]; o += 4 * RET_W
    rkv = w_l[:, o:o + 3 * RW_W]; o += 3 * RW_W
    xw = w_l[:, o:o + RW_DECAY_RANK]; o += RW_DECAY_RANK
    xa = w_l[:, o:o + RW_A_RANK]; o += RW_A_RANK
    xg = w_l[:, o:o + RW_GATE_RANK]; o += RW_GATE_RANK
    q = w_l[:, o:o + NSA_W]; o += NSA_W
    kc, vc, ks, vs, kw, vw = (w_l[:, o + i * NSA_KV_W:o + (i + 1) * NSA_KV_W] for i in range(6))
    o += 6 * NSA_KV_W
    gates = w_l[:, o:o + 3 * NSA_HEADS]
    return jnp.concatenate([ret, rkv, _pad_cols(xw, LANES), _pad_cols(xa, LANES), xg, xv,
                            q, ks, kw, kc, vc, vs, vw, _pad_cols(gates, LANES)], axis=1)


def _pack_mix(mix, vres_mix):
    o = 3 * RW_W
    xw = mix[o:o + RW_DECAY_RANK]; o += RW_DECAY_RANK
    xa = mix[o:o + RW_A_RANK]; o += RW_A_RANK
    xg = mix[o:o + RW_GATE_RANK]
    z = lambda n: jnp.zeros((n,), mix.dtype)
    xv = z(LANES) if vres_mix is None else jnp.concatenate([vres_mix, z(LANES - RW_V_RANK)])
    return jnp.concatenate([mix[:3 * RW_W], xw, z(LANES - RW_DECAY_RANK), xa, z(LANES - RW_A_RANK), xg, xv])[None, :]


def kernel(x, ln1_g, w_in, w_in_vres, rwkv_mix, rwkv_vres_mix, rwkv_w0, rwkv_w2, rwkv_a0, rwkv_a2, rwkv_v0, rwkv_v2, rwkv_g2, rwkv_k_k, rwkv_k_a, rwkv_r_k, rwkv_ln_g, rwkv_ln_b, nsa_q_norm, nsa_k_norm, nsa_cmp_pe, nsa_cmp_k_w1, nsa_cmp_k_w2, nsa_cmp_v_w1, nsa_cmp_v_w2, w_out, ln2_g, w_up, w_down):
    bsz, t, d = x.shape
    assert bsz == 1
    depth = w_in.shape[0]
    nc = t // CMP_STRIDE
    nb = t // SLC_BLOCK
    xs = x.reshape(t, d)

    pos = np.arange(t)
    cos_r, sin_r = _rope_tables(pos, RET_DH, 1)
    cos_n, sin_n = _rope_tables(pos, NSA_DH, 2)
    cos_c, sin_c = _rope_tables(np.arange(nc) * CMP_STRIDE + (CMP_BLOCK - 1), NSA_DH, 1)
    lane_head = np.arange(RW_W) // RW_N
    gsum = jnp.asarray(lane_head[:, None] == lane_head[None, :], BF16)
    lane_h2 = np.arange(LANES) // NSA_DH
    gmean = jnp.asarray((lane_h2[:, None] == lane_h2[None, :]) / float(NSA_DH), BF16)
    cstart = np.arange(nc) * CMP_STRIDE
    sstart = np.arange(nb) * SLC_BLOCK
    cis = jnp.asarray((cstart[:, None] <= sstart[None, :] + SLC_BLOCK - 1)
                      & (cstart[:, None] + CMP_BLOCK - 1 >= sstart[None, :]), BF16)

    ltri = jnp.asarray(np.tril(np.ones((nb, nb), np.float32), -1), BF16)

    v_first = jnp.zeros((t, RW_W), F32)
    for l in range(depth):
        vres = l > 0
        xv_w = _pad_cols(w_in_vres[l - 1], LANES) if vres else jnp.zeros((d, LANES), F32)
        w_cat = _pack_w_in_call(w_in, l, xv_w)
        proj = _norm_matmul(xs, ln1_g[l], w_cat, name="in_proj")

        o_ret = _retention(proj, cos_r, sin_r)

        row = lambda a: a.reshape(1, -1)
        mix = _pack_mix(rwkv_mix[l], rwkv_vres_mix[l - 1] if vres else None)
        v0 = row(rwkv_v0[l - 1]) if vres else jnp.zeros((1, RW_W), F32)
        v2 = _pad_rows(rwkv_v2[l - 1], LANES) if vres else jnp.zeros((LANES, RW_W), F32)
        r_, lw_, k_, v_, an_, b_, g_ = _rwkv_pre(
            proj, mix, row(rwkv_w0[l]), _pad_rows(rwkv_w2[l], LANES), row(rwkv_a0[l]), _pad_rows(rwkv_a2[l], LANES),
            v0, v2, rwkv_g2[l], row(rwkv_k_k[l]), row(rwkv_k_a[l]), gsum, v_first, vres)
        if not vres:
            v_first = v_
        o_rwkv = _wkv(r_, lw_, k_, v_, an_, b_, g_, row(rwkv_ln_g[l]), row(rwkv_ln_b[l]), row(rwkv_r_k[l]), gsum)

        n_q = NSA_W // LANES
        n_k = NSA_KV_W // LANES
        w_slabs = jnp.stack([jnp.tile(nsa_q_norm[l] * (NSA_DH ** -0.5), 2)] * n_q
                            + [jnp.tile(nsa_k_norm[l, 1], 2)] * n_k + [jnp.tile(nsa_k_norm[l, 2], 2)] * n_k)[:, None, :]
        s_max = 1.02 * jnp.max(jnp.abs(nsa_q_norm[l])) * jnp.max(jnp.abs(nsa_k_norm[l, 1])) * (NSA_DH ** 0.5)
        big = jnp.full((1, LANES), jnp.exp2(jnp.ceil(jnp.log2(2.0 * s_max + 128.0))), F32)
        qk_hm, hmat, v1, gates, qpad, ksx = _nsa_prep(proj, w_slabs, cos_n, sin_n, gmean, big)
        w1 = jnp.stack([nsa_cmp_k_w1[l], nsa_cmp_v_w1[l]]).astype(BF16)
        w2 = jnp.stack([nsa_cmp_k_w2[l], nsa_cmp_v_w2[l]]).astype(BF16)
        pe = jnp.broadcast_to(nsa_cmp_pe[l].reshape(2, 1, CMP_BLOCK * NSA_DH), (2, 8, CMP_BLOCK * NSA_DH)).astype(BF16)
        cmp_kv = _compress(hmat, w1, w2, pe, row(nsa_k_norm[l, 0]), cos_c, sin_c)
        o_nsa = _nsa_attn(qk_hm, qpad, ksx, gates, cmp_kv, v1, cis, ltri)

        xs = _out_proj(o_ret, o_rwkv, o_nsa, w_out, l, xs)

        up = _norm_matmul(xs, ln2_g[l], w_up, layer=l, act="relu2", out_dtype=BF16, name="mlp_up")
        xs = _matmul(up, w_down, layer=l, res=xs, name="mlp_down")
    return xs.reshape(bsz, t, d)
```

```python
import functools

import numpy as np
import jax
import jax.numpy as jnp
from jax import lax
from jax.experimental import pallas as pl
from jax.experimental.pallas import tpu as pltpu

F32 = jnp.float32
BF16 = jnp.bfloat16

D_MODEL = 2048
RET_HEADS, RET_DH = 4, 128
RET_W = RET_HEADS * RET_DH
RW_HEADS, RW_N = 8, 64
RW_W = RW_HEADS * RW_N
RW_DECAY_RANK, RW_A_RANK, RW_V_RANK, RW_GATE_RANK = 96, 96, 64, 256
RW_GN_EPS = 64e-5
NSA_HEADS, NSA_G, NSA_DH = 16, 4, 64
NSA_HG = NSA_HEADS // NSA_G
NSA_W = NSA_HEADS * NSA_DH
NSA_KV_W = NSA_G * NSA_DH
CMP_BLOCK, CMP_STRIDE, CMP_HIDDEN = 32, 16, 256
SLC_BLOCK, SLC_TOPK, WINDOW = 64, 16, 512
NSA_TQ = 512
ROPE_THETA = 10000.0
MLP_HIDDEN = 4 * D_MODEL
RMS_EPS = 1e-6
NEG_INF = -1e30
FORCED_SCORE = 1e9

LANES = 128
VMEM_LIMIT = 56 * 1024 * 1024

C_RET = 0
C_RW_RKV = 2048
C_RW_XW = 3584
C_RW_XA = 3712
C_RW_XG = 3840
C_RW_XV = 4096
C_NSA_Q = 4224
C_NSA_KS = 5248
C_NSA_KW = 5504
C_NSA_KC = 5760
C_NSA_VC = 6016
C_NSA_VS = 6272
C_NSA_VW = 6528
C_NSA_GATE = 6784
PROJ_W = 6912


def _cparams(sem):
    return pltpu.CompilerParams(dimension_semantics=sem, vmem_limit_bytes=VMEM_LIMIT)


def _dot(a, b):
    return jnp.dot(a, b, preferred_element_type=F32)


def _dot_t(a, b):
    return lax.dot_general(a, b, (((1,), (1,)), ((), ())), preferred_element_type=F32)


def _dot_tl(a, b):
    return lax.dot_general(a, b, (((0,), (0,)), ((), ())), preferred_element_type=F32)


def _split_dot(x, m_bf16, passes):
    hi = x.astype(BF16)
    acc = _dot(hi, m_bf16)
    rem = x - hi.astype(F32)
    for _ in range(passes - 1):
        piece = rem.astype(BF16)
        acc = acc + _dot(piece, m_bf16)
        rem = rem - piece.astype(F32)
    return acc


def _mm_kernel(*refs, nk, act, has_res):
    a_ref, b_ref = refs[:2]
    r_ref = refs[2] if has_res else None
    o_ref = refs[3] if has_res else refs[2]

    def finish(y):
        if act == "relu2":
            y = jnp.square(jnp.maximum(y, 0.0))
        if has_res:
            y = y + r_ref[...]
        o_ref[...] = y.astype(o_ref.dtype)

    part = _dot(a_ref[...], b_ref[...].astype(BF16))
    if nk == 1:
        finish(part)
        return
    acc_ref = refs[-1]
    k = pl.program_id(2)

    @pl.when(k == 0)
    def _():
        acc_ref[...] = part

    @pl.when(k > 0)
    def _():
        acc_ref[...] += part

    @pl.when(k == nk - 1)
    def _():
        finish(acc_ref[...])


def _pick(n, cands):
    for c in cands:
        if n % c == 0:
            return c
    raise ValueError(f"no tile for {n}")


def _weight_spec(b, layer, block, index_map):
    if layer is None:
        return pl.BlockSpec(block, index_map)
    return pl.BlockSpec((None,) + block, lambda *idx: (layer,) + index_map(*idx))


def _matmul(a, b, *, layer=None, res=None, act=None, out_dtype=F32, name="matmul"):
    m, kd = a.shape
    n = b.shape[-1]
    tm = _pick(m, (2048, 1024, 512, 256, 128))
    tn = _pick(n, (1024, 768, 512, 256, 128))
    tk = _pick(kd, (512,))
    nk = kd // tk
    in_specs = [pl.BlockSpec((tm, tk), lambda i, j, k: (i, k)),
                _weight_spec(b, layer, (tk, tn), lambda i, j, k: (k, j))]
    args = [a, b]
    if res is not None:
        in_specs.append(pl.BlockSpec((tm, tn), lambda i, j, k: (i, j), pipeline_mode=pl.Buffered(1)))
        args.append(res)
    return pl.pallas_call(
        functools.partial(_mm_kernel, nk=nk, act=act, has_res=res is not None),
        out_shape=jax.ShapeDtypeStruct((m, n), out_dtype),
        grid=(m // tm, n // tn, nk),
        in_specs=in_specs,
        out_specs=pl.BlockSpec((tm, tn), lambda i, j, k: (i, j)),
        scratch_shapes=[pltpu.VMEM((tm, tn), F32)] if nk > 1 else [],
        compiler_params=_cparams(("parallel", "parallel", "arbitrary")),
        name=name,
    )(*args)


def _norm_mm_kernel(x_ref, g_ref, b_ref, o_ref, h_ref, *, act):
    @pl.when(pl.program_id(1) == 0)
    def _():
        x = x_ref[...]
        ms = jnp.mean(x * x, axis=-1, keepdims=True)
        h_ref[...] = (x * lax.rsqrt(ms + RMS_EPS) * g_ref[...]).astype(BF16)

    y = _dot(h_ref[...], b_ref[...].astype(BF16))
    if act == "relu2":
        y = jnp.square(jnp.maximum(y, 0.0))
    o_ref[...] = y.astype(o_ref.dtype)


def _norm_matmul(x, g, b, *, layer=None, act=None, out_dtype=F32, name="norm_matmul"):
    m, kd = x.shape
    n = b.shape[-1]
    tm = _pick(m, (1024, 512, 256, 128))
    tn = _pick(n, (1024, 768, 512, 256, 128))
    return pl.pallas_call(
        functools.partial(_norm_mm_kernel, act=act),
        out_shape=jax.ShapeDtypeStruct((m, n), out_dtype),
        grid=(m // tm, n // tn),
        in_specs=[pl.BlockSpec((tm, kd), lambda i, j: (i, 0)),
                  pl.BlockSpec((1, kd), lambda i, j: (0, 0)),
                  _weight_spec(b, layer, (kd, tn), lambda i, j: (0, j))],
        out_specs=pl.BlockSpec((tm, tn), lambda i, j: (i, j)),
        scratch_shapes=[pltpu.VMEM((tm, kd), BF16)],
        compiler_params=_cparams(("parallel", "arbitrary")),
        name=name,
    )(x, g.reshape(1, kd), b)


def _out_proj_kernel(ret_ref, rw_ref, nsa_ref, w_ref, res_ref, o_ref, wb_ref):
    @pl.when(pl.program_id(1) == 0)
    def _():
        wb_ref[...] = w_ref[...].astype(BF16)

    k0 = ret_ref.shape[1]
    k1 = k0 + rw_ref.shape[1]
    acc = _dot(ret_ref[...], wb_ref[:k0, :])
    acc = acc + _dot(rw_ref[...], wb_ref[k0:k1, :])
    acc = acc + _dot(nsa_ref[...], wb_ref[k1:, :])
    o_ref[...] = acc + res_ref[...]


def _out_proj(o_ret, o_rwkv, o_nsa, w, layer, res):
    m = o_ret.shape[0]
    kd, n = w.shape[-2:]
    assert kd == o_ret.shape[1] + o_rwkv.shape[1] + o_nsa.shape[1]
    tm = _pick(m, (1024, 512, 256, 128))
    tn = _pick(n, (1024, 512, 256, 128))

    def rows(a):
        return pl.BlockSpec((tm, a.shape[1]), lambda j, i: (i, 0))

    return pl.pallas_call(
        _out_proj_kernel,
        out_shape=jax.ShapeDtypeStruct((m, n), F32),
        grid=(n // tn, m // tm),
        in_specs=[rows(o_ret), rows(o_rwkv), rows(o_nsa),
                  _weight_spec(w, layer, (kd, tn), lambda j, i: (0, j)),
                  pl.BlockSpec((tm, tn), lambda j, i: (i, j))],
        out_specs=pl.BlockSpec((tm, tn), lambda j, i: (i, j)),
        scratch_shapes=[pltpu.VMEM((kd, tn), BF16)],
        compiler_params=_cparams(("parallel", "arbitrary")),
        name="out_proj",
    )(o_ret, o_rwkv, o_nsa, w, res)


RET_CHUNK = 512


def _retention_kernel(q_ref, k_ref, v_ref, g_ref, cos_ref, sin_ref, dmat_ref, qd_ref, kd_ref, cd_ref, o_ref, state_ref):
    @pl.when(pl.program_id(0) == 0)
    def _():
        state_ref[...] = jnp.zeros_like(state_ref)

    cos = cos_ref[...]
    sin = sin_ref[...]

    def rope(x):
        return x * cos + pltpu.roll(x, RET_DH // 2, axis=1) * sin

    heads = range(RET_HEADS)
    hs = [slice(h * RET_DH, (h + 1) * RET_DH) for h in heads]
    q_all, k_all, v_all, gate = q_ref[...], k_ref[...], v_ref[...], g_ref[...]
    qb = [rope(q_all[:, s]).astype(BF16) for s in hs]
    kf = [rope(k_all[:, s]) * (RET_DH ** -0.5) for s in hs]
    kb = [kf[h].astype(BF16) for h in heads]
    vb = [v_all[:, s].astype(BF16) for s in hs]
    inner = [(_dot_t(qb[h], kb[h]) * dmat_ref[h]).astype(BF16) for h in heads]
    cross = [_dot(qb[h], state_ref[h].astype(BF16)) * qd_ref[h] for h in heads]
    kv = [_dot_tl((kf[h] * kd_ref[h]).astype(BF16), vb[h]) for h in heads]
    outs = []
    for h in heads:
        out = _dot(inner[h], vb[h]) + cross[h]
        state_ref[h] = cd_ref[h] * state_ref[h] + kv[h]
        outs.append(out * lax.rsqrt(jnp.mean(out * out, axis=-1, keepdims=True) + RMS_EPS))
    o_ref[...] = (gate * jax.nn.sigmoid(gate) * jnp.concatenate(outs, axis=1)).astype(o_ref.dtype)


def _retention(proj, cos, sin):
    t = proj.shape[0]
    chunk = RET_CHUNK
    log_gamma = np.log(1.0 - 2.0 ** (-5.0 - np.arange(RET_HEADS, dtype=np.float64)))[:, None, None]
    n = np.arange(chunk, dtype=np.float64)
    lag = n[:, None] - n[None, :]
    dmat = np.where(lag >= 0, np.exp(np.maximum(lag, 0.0)[None] * log_gamma), 0.0)
    ones = np.ones((1, 1, RET_DH))
    qd = np.exp((n + 1.0)[None, :, None] * log_gamma) * ones
    kd = np.exp((chunk - 1.0 - n)[None, :, None] * log_gamma) * ones
    cd = np.exp(chunk * log_gamma) * ones
    tables = [jnp.asarray(a, F32) for a in (dmat, qd, kd, cd)]
    base = C_RET // RET_W

    def col(off):
        return pl.BlockSpec((chunk, RET_W), lambda c, off=off: (c, base + off))

    def full(a):
        return pl.BlockSpec(a.shape, lambda c: (0, 0, 0))

    tab = pl.BlockSpec((chunk, RET_DH), lambda c: (c, 0))
    return pl.pallas_call(
        _retention_kernel,
        out_shape=jax.ShapeDtypeStruct((t, RET_W), BF16),
        grid=(t // chunk,),
        in_specs=[col(0), col(1), col(2), col(3), tab, tab] + [full(a) for a in tables],
        out_specs=pl.BlockSpec((chunk, RET_W), lambda c: (c, 0)),
        scratch_shapes=[pltpu.VMEM((RET_HEADS, RET_DH, RET_DH), F32)],
        compiler_params=_cparams(("arbitrary",)),
        name="retention",
    )(proj, proj, proj, proj, cos, sin, *tables)


def _dot_f32(a, b):
    a_hi = a.astype(BF16)
    b_hi = b.astype(BF16)
    a_lo = (a - a_hi.astype(F32)).astype(BF16)
    b_lo = (b - b_hi.astype(F32)).astype(BF16)
    return _dot(a_hi, b_hi) + (_dot(a_hi, b_lo) + _dot(a_lo, b_hi))


RW_PRE_TM = 512
_MIX_OFF = (0, 512, 1024, 1536, 1664, 1792, 2048, 2176)


def _rwkv_pre_kernel(r_ref, k_ref, v_ref, xw_ref, xa_ref, xg_ref, xv_ref,
                     rp_ref, kp_ref, vp_ref, xwp_ref, xap_ref, xgp_ref, xvp_ref,
                     mix_ref, w0_ref, w2_ref, a0_ref, a2_ref, v0_ref, v2_ref, g2_ref, kk_ref, ka_ref,
                     gsum_ref, vfirst_ref,
                     r_out, lw_out, k_out, v_out, an_out, b_out, g_out, *, use_vres):
    i = pl.program_id(0)

    def mixed(cur_ref, prev_ref, seg):
        cur = cur_ref[...]
        mix = mix_ref[:, _MIX_OFF[seg]:_MIX_OFF[seg + 1]]
        prev_row = jnp.where(i > 0, prev_ref[7:8, :], 0.0)
        rid = lax.broadcasted_iota(jnp.int32, cur.shape, 0)
        shifted = jnp.where(rid == 0, prev_row, pltpu.roll(cur, 1, axis=0))
        return cur + mix * (shifted - cur)

    r = mixed(r_ref, rp_ref, 0)
    k = mixed(k_ref, kp_ref, 1)
    v = mixed(v_ref, vp_ref, 2)
    xw = mixed(xw_ref, xwp_ref, 3)
    xa = mixed(xa_ref, xap_ref, 4)
    xg = mixed(xg_ref, xgp_ref, 5)

    z = -(w0_ref[...] + _dot_f32(jnp.tanh(xw), w2_ref[...]))
    softplus = jnp.maximum(z, 0.0) + jnp.log(1.0 + jnp.exp(-jnp.abs(z)))
    w = -softplus - 0.5
    lw_out[...] = -jnp.exp(w)
    a = jax.nn.sigmoid(a0_ref[...] + _dot_f32(xa, a2_ref[...]))
    g_out[...] = _dot_f32(jax.nn.sigmoid(xg), g2_ref[...])
    if use_vres:
        xv = mixed(xv_ref, xvp_ref, 6)
        v = v + (vfirst_ref[...] - v) * jax.nn.sigmoid(v0_ref[...] + _dot_f32(xv, v2_ref[...]))
    kk = k * kk_ref[...]
    ss = _split_dot(kk * kk, gsum_ref[...], 2)
    kk = kk / jnp.maximum(jnp.sqrt(ss), 1e-12)
    r_out[...] = r
    k_out[...] = k * (1.0 + (a - 1.0) * ka_ref[...])
    v_out[...] = v
    an_out[...] = -kk
    b_out[...] = kk * a


def _rwkv_pre(proj, mix, w0, w2, a0, a2, v0, v2, g2, k_k, k_a, gsum, v_first, use_vres):
    t = proj.shape[0]
    tm = RW_PRE_TM

    def cur(width, off):
        return pl.BlockSpec((tm, width), lambda i: (i, off // width))

    def prev(width, off):
        return pl.BlockSpec((8, width), lambda i: (jnp.maximum(i * (tm // 8) - 1, 0), off // width))

    def full(a):
        return pl.BlockSpec(a.shape, lambda i: (0,) * a.ndim)

    segs = [(RW_W, C_RW_RKV), (RW_W, C_RW_RKV + RW_W), (RW_W, C_RW_RKV + 2 * RW_W),
            (128, C_RW_XW), (128, C_RW_XA), (256, C_RW_XG), (128, C_RW_XV)]
    params = [mix, w0, w2, a0, a2, v0, v2, g2, k_k, k_a, gsum]
    row = pl.BlockSpec((tm, RW_W), lambda i: (i, 0))
    outs = pl.pallas_call(
        functools.partial(_rwkv_pre_kernel, use_vres=use_vres),
        out_shape=[jax.ShapeDtypeStruct((t, RW_W), F32)] * 7,
        grid=(t // tm,),
        in_specs=[cur(w, o) for w, o in segs] + [prev(w, o) for w, o in segs] + [full(p) for p in params] + [row],
        out_specs=[row] * 7,
        compiler_params=_cparams(("parallel",)),
        name="rwkv_pre",
    )(*([proj] * 14), *params, v_first)
    return outs


RW_CHUNK = 128


def _wkv_kernel(r_ref, lw_ref, k_ref, v_ref, an_ref, b_ref, g_ref, lng_ref, lnb_ref, rk_ref,
                gsum_ref, tril_ref, o_ref, s_ref):
    c = pl.program_id(0)
    C = RW_CHUNK
    N = RW_N

    @pl.when(c == 0)
    def _():
        s_ref[...] = jnp.zeros_like(s_ref)

    r = r_ref[...]
    lw = lw_ref[...]
    k = k_ref[...]
    v = v_ref[...]
    lw_hi = lw.astype(BF16)
    rem = lw - lw_hi.astype(F32)
    lw_mid = rem.astype(BF16)
    lw_lo = (rem - lw_mid.astype(F32)).astype(BF16)
    tril = tril_ref[...]
    lg = _dot(tril, lw_hi) + (_dot(tril, lw_mid) + _dot(tril, lw_lo))
    gam = jnp.exp(lg)
    ginv = jnp.exp(-lg)
    at = (an_ref[...] * jnp.exp(lg - lw)).astype(BF16)
    bt = (b_ref[...] * ginv).astype(BF16)
    kt = (k * ginv).astype(BF16)
    rt_f = r * gam
    rt = rt_f.astype(BF16)
    vb = v.astype(BF16)
    g_last = gam[C - 1:C, :]

    rowi = lax.broadcasted_iota(jnp.int32, (C, C), 0)
    coli = lax.broadcasted_iota(jnp.int32, (C, C), 1)
    strict = rowi > coli
    incl = rowi >= coli

    heads = range(RW_HEADS)
    hs = [slice(h * N, (h + 1) * N) for h in heads]
    bth = [bt[:, s] for s in hs]
    kth = [kt[:, s] for s in hs]
    vh = [vb[:, s] for s in hs]
    big = [_dot_t(jnp.concatenate([at[:, s], rt[:, s]], axis=0), jnp.concatenate([bth[h], kth[h]], axis=0))
           for h, s in zip(heads, hs)]
    a_ab = [jnp.where(strict, big[h][:C, :C], 0.0) for h in heads]
    a_ak = [jnp.where(strict, big[h][:C, C:], 0.0).astype(BF16) for h in heads]
    a_rb = [jnp.where(incl, big[h][C:, :C], 0.0).astype(BF16) for h in heads]
    a_rk = [jnp.where(incl, big[h][C:, C:], 0.0).astype(BF16) for h in heads]
    akv = [_dot(a_ak[h], vh[h]) for h in heads]
    def same_block(size):
        shift = size.bit_length() - 1
        return (rowi >> shift) == (coli >> shift)

    tinv = [jnp.where(same_block(2), a_ab[h], 0.0) + jnp.where(rowi == coli, 1.0, 0.0) for h in heads]
    size = 2
    while size < C:
        lower_left = same_block(2 * size) & jnp.logical_not(same_block(size))
        off = [jnp.where(lower_left, a_ab[h], 0.0).astype(BF16) for h in heads]
        tb = [tinv[h].astype(BF16) for h in heads]
        half = [_dot(tb[h], off[h]).astype(BF16) for h in heads]
        tinv = [tinv[h] + _dot(half[h], tb[h]) for h in heads]
        size *= 2
    xb = [_dot(tinv[h].astype(BF16), jnp.concatenate([at[:, hs[h]], akv[h].astype(BF16)], axis=1)).astype(BF16)
          for h in heads]
    yx = [_dot(a_rb[h], xb[h]) for h in heads]
    ykv = [_dot(a_rk[h], vh[h]) for h in heads]
    xtb = [_dot_tl(xb[h], bth[h]) for h in heads]
    vtk = [_dot_tl(vh[h], kth[h]) for h in heads]
    ys = []
    for h in heads:
        gl = g_last[:, hs[h]]
        y1 = (rt_f[:, hs[h]] + yx[h][:, :N]).astype(BF16)
        s0 = s_ref[h]
        s0b = s0.astype(BF16)
        ys.append(_dot_t(y1, s0b) + (yx[h][:, N:] + ykv[h]))
        s_ref[h] = (s0 + _dot(s0b, xtb[h][:N].astype(BF16)) + (xtb[h][N:] + vtk[h])) * gl
    y = jnp.concatenate(ys, axis=1)

    gsum = gsum_ref[...]
    inv_n = 1.0 / N
    mu = _split_dot(y, gsum, 2) * inv_n
    yc = y - mu
    var = _split_dot(yc * yc, gsum, 2) * inv_n
    yn = yc * lax.rsqrt(var + RW_GN_EPS) * lng_ref[...] + lnb_ref[...]
    bonus = _split_dot(r * k * rk_ref[...], gsum, 2)
    o_ref[...] = ((yn + bonus * v) * g_ref[...]).astype(o_ref.dtype)


def _wkv(r, lw, k, v, an, b, g, ln_g, ln_b, r_k, gsum):
    t = r.shape[0]
    C = RW_CHUNK
    tril = jnp.asarray(np.tril(np.ones((C, C), np.float32)), BF16)
    row = pl.BlockSpec((C, RW_W), lambda c: (c, 0))
    vec = pl.BlockSpec((1, RW_W), lambda c: (0, 0))
    return pl.pallas_call(
        _wkv_kernel,
        out_shape=jax.ShapeDtypeStruct((t, RW_W), BF16),
        grid=(t // C,),
        in_specs=[row] * 7 + [vec] * 3 + [pl.BlockSpec((RW_W, RW_W), lambda c: (0, 0)),
                                          pl.BlockSpec((C, C), lambda c: (0, 0))],
        out_specs=row,
        scratch_shapes=[pltpu.VMEM((RW_HEADS, RW_N, RW_N), F32)],
        compiler_params=_cparams(("arbitrary",)),
        name="wkv7",
    )(r, lw, k, v, an, b, g, ln_g, ln_b, r_k, gsum, tril)


NSA_PREP_TM = 512


N_NORM_SLABS = (NSA_W + 2 * NSA_KV_W) // LANES
N_RAW_SLABS = 4 * NSA_KV_W // LANES
N_NSA_SLABS = N_NORM_SLABS + N_RAW_SLABS + 1


def _nsa_prep_kernel(*refs):
    x_refs = refs[:N_NSA_SLABS]
    (cos_ref, sin_ref, w_ref, gmean_ref, big_ref,
     qk_ref, hmat_ref, v1_ref, gate_ref, qpad_ref, ksx_ref) = refs[N_NSA_SLABS:]
    cos = cos_ref[...]
    sin = sin_ref[...]
    gmean = gmean_ref[...]
    half = NSA_DH // 2
    tm = cos.shape[0]
    lane = lax.broadcasted_iota(jnp.int32, cos.shape, 1)
    first_half = (lane % NSA_DH) < half
    low = lane < NSA_DH
    blk_row = (pl.program_id(0) * tm + lax.broadcasted_iota(jnp.int32, cos.shape, 0)) // SLC_BLOCK
    big = big_ref[...]
    mark_lo = jnp.where(blk_row == lane - NSA_DH, big, 0.0)
    mark_hi = jnp.where(low & (blk_row == lane + NSA_DH), big, 0.0).astype(BF16)
    n_q = NSA_W // LANES
    n_k = NSA_KV_W // LANES
    for s in range(N_NORM_SLABS):
        x = x_refs[s][...]
        y = x * lax.rsqrt(_split_dot(x * x, gmean, 2) + RMS_EPS) * w_ref[s]
        rot = jnp.where(first_half, pltpu.roll(y, LANES - half, axis=1), pltpu.roll(y, half, axis=1))
        res32 = y * cos + rot * sin
        res = res32.astype(qk_ref.dtype)
        qk_ref[2 * s] = res[:, :NSA_DH]
        qk_ref[2 * s + 1] = res[:, NSA_DH:]
        if s >= n_q + n_k:
            continue
        swapped = pltpu.roll(res32, NSA_DH, axis=1)
        if s < n_q:
            qpad_ref[2 * s] = jnp.where(low, res32, 0.0).astype(BF16)
            qpad_ref[2 * s + 1] = jnp.where(low, swapped, 0.0).astype(BF16)
        else:
            g0 = 2 * (s - n_q)
            ksx_ref[g0, :, :LANES] = jnp.where(low, res32, mark_lo).astype(BF16)
            ksx_ref[g0 + 1, :, :LANES] = jnp.where(low, swapped, mark_lo).astype(BF16)
            ksx_ref[g0, :, LANES:] = mark_hi
            ksx_ref[g0 + 1, :, LANES:] = mark_hi

    rows = hmat_ref.shape[2]
    low_r = lax.broadcasted_iota(jnp.int32, (rows, LANES), 1) < NSA_DH
    for s in range(NSA_KV_W // LANES * 2):
        kind, pair = divmod(s, NSA_KV_W // LANES)
        x_ref = x_refs[N_NORM_SLABS + s]
        for t2 in range(CMP_STRIDE // 2):
            r0 = x_ref[pl.ds(2 * t2, rows, stride=CMP_STRIDE), :]
            r1 = x_ref[pl.ds(2 * t2 + 1, rows, stride=CMP_STRIDE), :]
            cols = slice(t2 * LANES, (t2 + 1) * LANES)
            hmat_ref[kind, 2 * pair, :, cols] = jnp.where(low_r, r0, pltpu.roll(r1, NSA_DH, axis=1)).astype(BF16)
            hmat_ref[kind, 2 * pair + 1, :, cols] = jnp.where(low_r, pltpu.roll(r0, NSA_DH, axis=1), r1).astype(BF16)

    one_col = jnp.where(lane == NSA_DH, 1.0, 0.0)
    for s in range(NSA_KV_W // LANES * 2):
        x = x_refs[N_NORM_SLABS + N_RAW_SLABS // 2 + s][...]
        v1_ref[2 * s] = jnp.where(low, x, one_col).astype(BF16)
        v1_ref[2 * s + 1] = jnp.where(low, pltpu.roll(x, NSA_DH, axis=1), one_col).astype(BF16)

    xg = x_refs[N_NSA_SLABS - 1][...]
    per_group = 3 * NSA_HG
    for g in range(NSA_G):
        shifted = xg if g == 0 else pltpu.roll(xg, LANES - per_group * g, axis=1)
        gate_ref[g] = jnp.where(lane < per_group, shifted, 0.0)


def _nsa_prep(proj, w_slabs, cos, sin, gmean, big):
    t = proj.shape[0]
    assert t // SLC_BLOCK <= LANES
    tm = NSA_PREP_TM
    base = C_NSA_Q // LANES
    tab = pl.BlockSpec((tm, LANES), lambda i: (i, 0))
    n_heads_out = 2 * N_NORM_SLABS
    return pl.pallas_call(
        _nsa_prep_kernel,
        out_shape=[jax.ShapeDtypeStruct((n_heads_out, t, NSA_DH), BF16),
                   jax.ShapeDtypeStruct((2, NSA_G, t // CMP_STRIDE, CMP_STRIDE * NSA_DH), BF16),
                   jax.ShapeDtypeStruct((2 * NSA_G, t, LANES), BF16),
                   jax.ShapeDtypeStruct((NSA_G, t, LANES), F32),
                   jax.ShapeDtypeStruct((NSA_HEADS, t, LANES), BF16),
                   jax.ShapeDtypeStruct((NSA_G, t, 2 * LANES), BF16)],
        grid=(t // tm,),
        in_specs=[pl.BlockSpec((tm, LANES), lambda i, s=s: (i, base + s)) for s in range(N_NSA_SLABS)]
        + [tab, tab, pl.BlockSpec((N_NORM_SLABS, 1, LANES), lambda i: (0, 0, 0)),
           pl.BlockSpec((LANES, LANES), lambda i: (0, 0)), pl.BlockSpec((1, LANES), lambda i: (0, 0))],
        out_specs=[pl.BlockSpec((n_heads_out, tm, NSA_DH), lambda i: (0, i, 0)),
                   pl.BlockSpec((2, NSA_G, tm // CMP_STRIDE, CMP_STRIDE * NSA_DH), lambda i: (0, 0, i, 0)),
                   pl.BlockSpec((2 * NSA_G, tm, LANES), lambda i: (0, i, 0)),
                   pl.BlockSpec((NSA_G, tm, LANES), lambda i: (0, i, 0)),
                   pl.BlockSpec((NSA_HEADS, tm, LANES), lambda i: (0, i, 0)),
                   pl.BlockSpec((NSA_G, tm, 2 * LANES), lambda i: (0, i, 0))],
        compiler_params=_cparams(("parallel",)),
        name="nsa_prep",
    )(*([proj] * N_NSA_SLABS), cos, sin, w_slabs, gmean, big)


def _compress_kernel(h_ref, w1_ref, w2_ref, pe_ref, nw_ref, cos_ref, sin_ref, o_ref):
    kind = pl.program_id(0)
    nc = h_ref.shape[2]
    half_in = CMP_STRIDE * NSA_DH
    hm = h_ref[0, 0]
    w1 = w1_ref[0]
    first = _dot(hm, w1[:half_in])
    second = _dot(hm, w1[half_in:])
    const = _dot(pe_ref[0], w1)[0:1, :]
    pre = first + pltpu.roll(second, nc - 1, axis=0) + const
    hid = 0.5 * pre * (1.0 + jnp.tanh(0.7978845608028654 * (pre + 0.044715 * pre * pre * pre)))
    out = _dot(hid.astype(BF16), w2_ref[0])
    y = out * lax.rsqrt(jnp.mean(out * out, axis=-1, keepdims=True) + RMS_EPS) * nw_ref[...]
    half = NSA_DH // 2
    rot = jnp.concatenate([y[:, half:], y[:, :half]], axis=1)
    roped = y * cos_ref[...] + rot * sin_ref[...]
    o_ref[0, 0] = jnp.where(kind == 0, roped, out).astype(o_ref.dtype)


def _compress(hmat, w1, w2, pe, nw, cos_c, sin_c):
    _, g, nc, width = hmat.shape
    return pl.pallas_call(
        _compress_kernel,
        out_shape=jax.ShapeDtypeStruct((2, g, nc, NSA_DH), BF16),
        grid=(2, g),
        in_specs=[pl.BlockSpec((1, 1, nc, width), lambda a, b: (a, b, 0, 0)),
                  pl.BlockSpec((1, 2 * width, CMP_HIDDEN), lambda a, b: (a, 0, 0)),
                  pl.BlockSpec((1, CMP_HIDDEN, NSA_DH), lambda a, b: (a, 0, 0)),
                  pl.BlockSpec((1, 8, 2 * width), lambda a, b: (a, 0, 0)),
                  pl.BlockSpec((1, NSA_DH), lambda a, b: (0, 0)),
                  pl.BlockSpec((nc, NSA_DH), lambda a, b: (0, 0)),
                  pl.BlockSpec((nc, NSA_DH), lambda a, b: (0, 0))],
        out_specs=pl.BlockSpec((1, 1, nc, NSA_DH), lambda a, b: (a, b, 0, 0)),
        compiler_params=_cparams(("parallel", "parallel")),
        name="nsa_compress",
    )(hmat, w1, w2, pe, nw, cos_c, sin_c)


SLC_KT = 1024
WIN_ROWS = 128
WIN_KEYS = WINDOW + WIN_ROWS


INT32_MIN = -2 ** 31


def _nsa_attn_kernel(q_ref, qpad_ref, gate_ref, kc_ref, vc_ref, ks_ref, vs_ref, kw_ref, vw_ref, cis_ref, ltri_ref,
                     o_ref):
    qi = pl.program_id(1)
    nc = kc_ref.shape[2]
    nb = cis_ref.shape[1]
    heads = range(NSA_HG)
    start = qi * NSA_TQ
    qh = [q_ref[hh] for hh in heads]
    tok = start + lax.broadcasted_iota(jnp.int32, (NSA_TQ, 1), 0)

    def softmax_terms(scores):
        return [jnp.exp(s - jnp.max(s, axis=-1, keepdims=True)) for s in scores]

    def cmp_branch(width):
        kc = kc_ref[0, 0, :width, :]
        vc = vc_ref[0, 0, :width, :]
        cend = lax.broadcasted_iota(jnp.int32, (1, width), 1) * CMP_STRIDE + (CMP_BLOCK - 1)
        bias_c = jnp.where(cend <= tok, 0.0, NEG_INF)
        ec = softmax_terms([_dot_t(qh[hh], kc) + bias_c for hh in heads])
        sees_any = tok >= CMP_BLOCK - 1
        pc = [ec[hh] * jnp.where(sees_any, 1.0 / jnp.sum(ec[hh], axis=-1, keepdims=True), 0.0) for hh in heads]
        outs_c = tuple(_dot(pc[hh].astype(BF16), vc) for hh in heads)
        psum = pc[0]
        for hh in range(1, NSA_HG):
            psum = psum + pc[hh]
        return outs_c, _split_dot(psum, cis_ref[:width, :], 2)

    n_quarters = 4
    quarter = nc // n_quarters
    n_visible = (start + NSA_TQ - CMP_BLOCK) // CMP_STRIDE + 1
    which = jnp.clip((n_visible - 1) // quarter, 0, n_quarters - 1)
    o_c, score = lax.switch(which, [functools.partial(cmp_branch, (r + 1) * quarter) for r in range(n_quarters)])
    blk = lax.broadcasted_iota(jnp.int32, (1, nb), 1)
    cur = tok // SLC_BLOCK
    forced = (blk == 0) | (blk == cur) | (blk == cur - 1)
    score = jnp.where(forced, FORCED_SCORE, score)
    score = jnp.where(blk <= cur, score, -jnp.inf)


    bits = lax.bitcast_convert_type(score.T, jnp.int32)
    key = bits ^ ((bits >> 31) & 0x7FFFFFFF)
    n_sel = min(SLC_TOPK, nb)

    def enough(c):
        return jnp.sum(jnp.where(key >= c, 1.0, 0.0), axis=0, keepdims=True) >= n_sel

    def enough3(c1, c2, c3):
        packed = jnp.where(key >= c3, 65793.0, jnp.where(key >= c2, 257.0, jnp.where(key >= c1, 1.0, 0.0)))
        tot = jnp.sum(packed, axis=0, keepdims=True).astype(jnp.int32)
        return (tot & 255) >= n_sel, ((tot >> 8) & 255) >= n_sel, (tot >> 16) >= n_sel

    zero_row = jnp.zeros((1, NSA_TQ), jnp.int32)
    thr = jnp.where(enough(zero_row), zero_row, INT32_MIN)
    for hi in range(30, 0, -2):
        c1 = thr + (1 << (hi - 1))
        c2 = thr + (1 << hi)
        c3 = c2 + (1 << (hi - 1))
        e1, e2, e3 = enough3(c1, c2, c3)
        thr = jnp.where(e3, c3, jnp.where(e2, c2, jnp.where(e1, c1, thr)))
    c1 = thr + 1
    thr = jnp.where(enough(c1), c1, thr)
    above = key > thr
    tied = key == thr
    need = n_sel - jnp.sum(jnp.where(above, 1.0, 0.0), axis=0, keepdims=True)
    tied_before = _dot(ltri_ref[...], jnp.where(tied, 1.0, 0.0).astype(BF16))
    sel_t = above | (tied & (tied_before < need))
    sel = jnp.where(sel_t, 1.0, 0.0).T
    if nb < LANES:
        sel = jnp.concatenate([sel, jnp.zeros((NSA_TQ, LANES - nb), F32)], axis=1)
    lane_q = lax.broadcasted_iota(jnp.int32, (NSA_TQ, LANES), 1)
    sel_sw = pltpu.roll(sel, NSA_DH, axis=1)
    sel_lo = jnp.where(lane_q >= NSA_DH, sel_sw, 0.0)
    sel_hi = jnp.where(lane_q < NSA_DH, sel_sw, 0.0)
    qx = [jnp.concatenate([qpad_ref[hh].astype(F32) + sel_lo, sel_hi], axis=1).astype(BF16) for hh in heads]

    def window_rows(r0):
        kbase = pl.multiple_of(jnp.maximum(start + r0 - WINDOW, 0), WIN_ROWS)
        kw = kw_ref[0, pl.ds(kbase, WIN_KEYS), :]
        vw = vw_ref[0, pl.ds(kbase, WIN_KEYS), :]
        lag = tok[r0:r0 + WIN_ROWS] - (kbase + lax.broadcasted_iota(jnp.int32, (1, WIN_KEYS), 1))
        bias_w = jnp.where((lag >= 0) & (lag < WINDOW), 0.0, NEG_INF)
        sw = [_dot_t(qh[hh][r0:r0 + WIN_ROWS], kw) + bias_w for hh in heads]
        ew = [jnp.exp((s - jnp.max(s, axis=-1, keepdims=True)).astype(BF16)) for s in sw]
        return [_dot(ew[hh], vw) for hh in heads]

    win_parts = [window_rows(r0) for r0 in range(0, NSA_TQ, WIN_ROWS)]
    acc_w = [jnp.concatenate([part[hh] for part in win_parts], axis=0) for hh in heads]

    n_tiles = (start + NSA_TQ + SLC_KT - 1) // SLC_KT

    def sel_tile(j, carry, causal, width=SLC_KT):
        ms, accs = carry
        k0 = pl.multiple_of(j * SLC_KT, SLC_KT)
        kt = ks_ref[0, pl.ds(k0, width), :]
        vt = vs_ref[0, pl.ds(k0, width), :]
        sj = [_dot_t(qx[hh], kt) for hh in heads]
        if causal:
            future = (k0 + lax.broadcasted_iota(jnp.int32, (1, width), 1)) > tok
            sj = [jnp.where(future, NEG_INF, s) for s in sj]
        m_new = [jnp.maximum(ms[hh], jnp.max(sj[hh], axis=-1, keepdims=True)) for hh in heads]
        pj = [jnp.exp((sj[hh] - m_new[hh]).astype(BF16)) for hh in heads]
        accs = [jnp.exp(ms[hh] - m_new[hh]) * accs[hh] + _dot(pj[hh], vt) for hh in heads]
        return tuple(m_new), tuple(accs)

    m0 = tuple(jnp.full((NSA_TQ, 1), NEG_INF, F32) for _ in heads)
    a0 = tuple(jnp.zeros((NSA_TQ, LANES), F32) for _ in heads)
    carry = lax.fori_loop(0, n_tiles - 1, functools.partial(sel_tile, causal=False), (m0, a0))
    per_tile = SLC_KT // NSA_TQ
    last = [functools.partial(sel_tile, n_tiles - 1, causal=True, width=(r + 1) * NSA_TQ) for r in range(per_tile)]
    _, acc_s = lax.switch(qi % per_tile, last, carry)

    gt = jax.nn.sigmoid(gate_ref[0])
    outs = []
    for hh in heads:
        g0, g1, g2 = (gt[:, 3 * hh + br:3 * hh + br + 1] for br in range(3))
        scale_s = g1 / acc_s[hh][:, NSA_DH:NSA_DH + 1]
        scale_w = g2 / acc_w[hh][:, NSA_DH:NSA_DH + 1]
        outs.append(g0 * o_c[hh] + scale_s * acc_s[hh][:, :NSA_DH] + scale_w * acc_w[hh][:, :NSA_DH])
    o_ref[...] = jnp.concatenate(outs, axis=1).astype(o_ref.dtype)


def _nsa_attn(qk_hm, qpad, ksx, gates, cmp_kv, v1, cis, ltri):
    t = qk_hm.shape[1]
    nq = t // NSA_TQ
    nc = cmp_kv.shape[2]
    nb = cis.shape[1]
    ks_spec = pl.BlockSpec((1, t, 2 * LANES), lambda g, i: (g, 0, 0))
    kw_spec = pl.BlockSpec((1, t, NSA_DH), lambda g, i: (NSA_HEADS + NSA_G + g, 0, 0))
    vs_spec = pl.BlockSpec((1, t, LANES), lambda g, i: (g, 0, 0))
    vw_spec = pl.BlockSpec((1, t, LANES), lambda g, i: (NSA_G + g, 0, 0))
    return pl.pallas_call(
        _nsa_attn_kernel,
        out_shape=jax.ShapeDtypeStruct((t, NSA_W), BF16),
        grid=(NSA_G, nq),
        in_specs=[pl.BlockSpec((NSA_HG, NSA_TQ, NSA_DH), lambda g, i: (g, i, 0)),
                  pl.BlockSpec((NSA_HG, NSA_TQ, LANES), lambda g, i: (g, i, 0)),
                  pl.BlockSpec((1, NSA_TQ, LANES), lambda g, i: (g, i, 0)),
                  pl.BlockSpec((1, 1, nc, NSA_DH), lambda g, i: (0, g, 0, 0)),
                  pl.BlockSpec((1, 1, nc, NSA_DH), lambda g, i: (1, g, 0, 0)),
                  ks_spec, vs_spec, kw_spec, vw_spec,
                  pl.BlockSpec((nc, nb), lambda g, i: (0, 0)),
                  pl.BlockSpec((nb, nb), lambda g, i: (0, 0))],
        out_specs=pl.BlockSpec((NSA_TQ, NSA_HG * NSA_DH), lambda g, i: (i, g)),
        compiler_params=_cparams(("parallel", "arbitrary")),
        name="nsa_attention",
    )(qk_hm, qpad, gates, cmp_kv, cmp_kv, ksx, v1, qk_hm, v1, cis, ltri)


def _rope_tables(pos, dh, reps):
    half = dh // 2
    inv_freq = ROPE_THETA ** (-np.arange(half, dtype=np.float64) / half)
    ang = np.asarray(pos, np.float64)[:, None] * inv_freq[None, :]
    cos = np.cos(ang)
    sin = np.sin(ang)
    cos_t = np.tile(np.concatenate([cos, cos], axis=1), (1, reps))
    sin_t = np.tile(np.concatenate([-sin, sin], axis=1), (1, reps))
    return jnp.asarray(cos_t, F32), jnp.asarray(sin_t, F32)


def _pad_cols(a, width):
    return jnp.pad(a, ((0, 0), (0, width - a.shape[1])))


def _pad_rows(a, height):
    return jnp.pad(a, ((0, height - a.shape[0]), (0, 0)))


def _pack_w_in_kernel(w_ref, xv_ref, o_ref):
    o_ref[...] = _pack_w_in(w_ref[...], xv_ref[...]).astype(o_ref.dtype)


def _pack_w_in_call(w_in, layer, xv):
    _, d, n_in = w_in.shape
    tk = 256
    return pl.pallas_call(
        _pack_w_in_kernel,
        out_shape=jax.ShapeDtypeStruct((d, PROJ_W), BF16),
        grid=(d // tk,),
        in_specs=[pl.BlockSpec((None, tk, n_in), lambda i: (layer, i, 0)),
                  pl.BlockSpec((tk, LANES), lambda i: (i, 0))],
        out_specs=pl.BlockSpec((tk, PROJ_W), lambda i: (i, 0)),
        compiler_params=_cparams(("parallel",)),
        name="pack_w_in",
    )(w_in, xv)


def _pack_w_in(w_l, xv):
    d = w_l.shape[0]
    o = 0
    ret = w_l[:, o:o + 4 * RET_W]; o += 4 * RET_W
    rkv = w_l[:, o:o + 3 * RW_W]; o += 3 * RW_W
    xw = w_l[:, o:o + RW_DECAY_RANK]; o += RW_DECAY_RANK
    xa = w_l[:, o:o + RW_A_RANK]; o += RW_A_RANK
    xg = w_l[:, o:o + RW_GATE_RANK]; o += RW_GATE_RANK
    q = w_l[:, o:o + NSA_W]; o += NSA_W
    kc, vc, ks, vs, kw, vw = (w_l[:, o + i * NSA_KV_W:o + (i + 1) * NSA_KV_W] for i in range(6))
    o += 6 * NSA_KV_W
    gates = w_l[:, o:o + 3 * NSA_HEADS]
    return jnp.concatenate([ret, rkv, _pad_cols(xw, LANES), _pad_cols(xa, LANES), xg, xv,
                            q, ks, kw, kc, vc, vs, vw, _pad_cols(gates, LANES)], axis=1)


def _pack_mix(mix, vres_mix):
    o = 3 * RW_W
    xw = mix[o:o + RW_DECAY_RANK]; o += RW_DECAY_RANK
    xa = mix[o:o + RW_A_RANK]; o += RW_A_RANK
    xg = mix[o:o + RW_GATE_RANK]
    z = lambda n: jnp.zeros((n,), mix.dtype)
    xv = z(LANES) if vres_mix is None else jnp.concatenate([vres_mix, z(LANES - RW_V_RANK)])
    return jnp.concatenate([mix[:3 * RW_W], xw, z(LANES - RW_DECAY_RANK), xa, z(LANES - RW_A_RANK), xg, xv])[None, :]


def kernel(x, ln1_g, w_in, w_in_vres, rwkv_mix, rwkv_vres_mix, rwkv_w0, rwkv_w2, rwkv_a0, rwkv_a2, rwkv_v0, rwkv_v2, rwkv_g2, rwkv_k_k, rwkv_k_a, rwkv_r_k, rwkv_ln_g, rwkv_ln_b, nsa_q_norm, nsa_k_norm, nsa_cmp_pe, nsa_cmp_k_w1, nsa_cmp_k_w2, nsa_cmp_v_w1, nsa_cmp_v_w2, w_out, ln2_g, w_up, w_down):
    bsz, t, d = x.shape
    assert bsz == 1
    depth = w_in.shape[0]
    nc = t // CMP_STRIDE
    nb = t // SLC_BLOCK
    xs = x.reshape(t, d)

    pos = np.arange(t)
    cos_r, sin_r = _rope_tables(pos, RET_DH, 1)
    cos_n, sin_n = _rope_tables(pos, NSA_DH, 2)
    cos_c, sin_c = _rope_tables(np.arange(nc) * CMP_STRIDE + (CMP_BLOCK - 1), NSA_DH, 1)
    lane_head = np.arange(RW_W) // RW_N
    gsum = jnp.asarray(lane_head[:, None] == lane_head[None, :], BF16)
    lane_h2 = np.arange(LANES) // NSA_DH
    gmean = jnp.asarray((lane_h2[:, None] == lane_h2[None, :]) / float(NSA_DH), BF16)
    cstart = np.arange(nc) * CMP_STRIDE
    sstart = np.arange(nb) * SLC_BLOCK
    cis = jnp.asarray((cstart[:, None] <= sstart[None, :] + SLC_BLOCK - 1)
                      & (cstart[:, None] + CMP_BLOCK - 1 >= sstart[None, :]), BF16)

    ltri = jnp.asarray(np.tril(np.ones((nb, nb), np.float32), -1), BF16)

    v_first = jnp.zeros((t, RW_W), F32)
    for l in range(depth):
        vres = l > 0
        xv_w = _pad_cols(w_in_vres[l - 1], LANES) if vres else jnp.zeros((d, LANES), F32)
        w_cat = _pack_w_in_call(w_in, l, xv_w)
        proj = _norm_matmul(xs, ln1_g[l], w_cat, name="in_proj")

        o_ret = _retention(proj, cos_r, sin_r)

        row = lambda a: a.reshape(1, -1)
        mix = _pack_mix(rwkv_mix[l], rwkv_vres_mix[l - 1] if vres else None)
        v0 = row(rwkv_v0[l - 1]) if vres else jnp.zeros((1, RW_W), F32)
        v2 = _pad_rows(rwkv_v2[l - 1], LANES) if vres else jnp.zeros((LANES, RW_W), F32)
        r_, lw_, k_, v_, an_, b_, g_ = _rwkv_pre(
            proj, mix, row(rwkv_w0[l]), _pad_rows(rwkv_w2[l], LANES), row(rwkv_a0[l]), _pad_rows(rwkv_a2[l], LANES),
            v0, v2, rwkv_g2[l], row(rwkv_k_k[l]), row(rwkv_k_a[l]), gsum, v_first, vres)
        if not vres:
            v_first = v_
        o_rwkv = _wkv(r_, lw_, k_, v_, an_, b_, g_, row(rwkv_ln_g[l]), row(rwkv_ln_b[l]), row(rwkv_r_k[l]), gsum)

        n_q = NSA_W // LANES
        n_k = NSA_KV_W // LANES
        w_slabs = jnp.stack([jnp.tile(nsa_q_norm[l] * (NSA_DH ** -0.5), 2)] * n_q
                            + [jnp.tile(nsa_k_norm[l, 1], 2)] * n_k + [jnp.tile(nsa_k_norm[l, 2], 2)] * n_k)[:, None, :]
        s_max = 1.02 * jnp.max(jnp.abs(nsa_q_norm[l])) * jnp.max(jnp.abs(nsa_k_norm[l, 1])) * (NSA_DH ** 0.5)
        big = jnp.full((1, LANES), jnp.exp2(jnp.ceil(jnp.log2(2.0 * s_max + 128.0))), F32)
        qk_hm, hmat, v1, gates, qpad, ksx = _nsa_prep(proj, w_slabs, cos_n, sin_n, gmean, big)
        w1 = jnp.stack([nsa_cmp_k_w1[l], nsa_cmp_v_w1[l]]).astype(BF16)
        w2 = jnp.stack([nsa_cmp_k_w2[l], nsa_cmp_v_w2[l]]).astype(BF16)
        pe = jnp.broadcast_to(nsa_cmp_pe[l].reshape(2, 1, CMP_BLOCK * NSA_DH), (2, 8, CMP_BLOCK * NSA_DH)).astype(BF16)
        cmp_kv = _compress(hmat, w1, w2, pe, row(nsa_k_norm[l, 0]), cos_c, sin_c)
        o_nsa = _nsa_attn(qk_hm, qpad, ksx, gates, cmp_kv, v1, cis, ltri)

        xs = _out_proj(o_ret, o_rwkv, o_nsa, w_out, l, xs)

        up = _norm_matmul(xs, ln2_g[l], w_up, layer=l, act="relu2", out_dtype=BF16, name="mlp_up")
        xs = _matmul(up, w_down, layer=l, res=xs, name="mlp_down")
    return xs.reshape(bsz, t, d)
```

```python
import functools

import numpy as np
import jax
import jax.numpy as jnp
from jax import lax
from jax.experimental import pallas as pl
from jax.experimental.pallas import tpu as pltpu

F32 = jnp.float32
BF16 = jnp.bfloat16

D_MODEL = 2048
RET_HEADS, RET_DH = 4, 128
RET_W = RET_HEADS * RET_DH
RW_HEADS, RW_N = 8, 64
RW_W = RW_HEADS * RW_N
RW_DECAY_RANK, RW_A_RANK, RW_V_RANK, RW_GATE_RANK = 96, 96, 64, 256
RW_GN_EPS = 64e-5
NSA_HEADS, NSA_G, NSA_DH = 16, 4, 64
NSA_HG = NSA_HEADS // NSA_G
NSA_W = NSA_HEADS * NSA_DH
NSA_KV_W = NSA_G * NSA_DH
CMP_BLOCK, CMP_STRIDE, CMP_HIDDEN = 32, 16, 256
SLC_BLOCK, SLC_TOPK, WINDOW = 64, 16, 512
NSA_TQ = 512
ROPE_THETA = 10000.0
MLP_HIDDEN = 4 * D_MODEL
RMS_EPS = 1e-6
NEG_INF = -1e30
FORCED_SCORE = 1e9

LANES = 128
VMEM_LIMIT = 56 * 1024 * 1024

C_RET = 0
C_RW_RKV = 2048
C_RW_XW = 3584
C_RW_XA = 3712
C_RW_XG = 3840
C_RW_XV = 4096
C_NSA_Q = 4224
C_NSA_KS = 5248
C_NSA_KW = 5504
C_NSA_KC = 5760
C_NSA_VC = 6016
C_NSA_VS = 6272
C_NSA_VW = 6528
C_NSA_GATE = 6784
PROJ_W = 6912


def _cparams(sem):
    return pltpu.CompilerParams(dimension_semantics=sem, vmem_limit_bytes=VMEM_LIMIT)


def _dot(a, b):
    return jnp.dot(a, b, preferred_element_type=F32)


def _dot_t(a, b):
    return lax.dot_general(a, b, (((1,), (1,)), ((), ())), preferred_element_type=F32)


def _dot_tl(a, b):
    return lax.dot_general(a, b, (((0,), (0,)), ((), ())), preferred_element_type=F32)


def _split_dot(x, m_bf16, passes):
    hi = x.astype(BF16)
    acc = _dot(hi, m_bf16)
    rem = x - hi.astype(F32)
    for _ in range(passes - 1):
        piece = rem.astype(BF16)
        acc = acc + _dot(piece, m_bf16)
        rem = rem - piece.astype(F32)
    return acc


def _mm_kernel(*refs, nk, act, has_res):
    a_ref, b_ref = refs[:2]
    r_ref = refs[2] if has_res else None
    o_ref = refs[3] if has_res else refs[2]

    def finish(y):
        if act == "relu2":
            y = jnp.square(jnp.maximum(y, 0.0))
        if has_res:
            y = y + r_ref[...]
        o_ref[...] = y.astype(o_ref.dtype)

    part = _dot(a_ref[...], b_ref[...].astype(BF16))
    if nk == 1:
        finish(part)
        return
    acc_ref = refs[-1]
    k = pl.program_id(2)

    @pl.when(k == 0)
    def _():
        acc_ref[...] = part

    @pl.when(k > 0)
    def _():
        acc_ref[...] += part

    @pl.when(k == nk - 1)
    def _():
        finish(acc_ref[...])


def _pick(n, cands):
    for c in cands:
        if n % c == 0:
            return c
    raise ValueError(f"no tile for {n}")


def _weight_spec(b, layer, block, index_map):
    if layer is None:
        return pl.BlockSpec(block, index_map)
    return pl.BlockSpec((None,) + block, lambda *idx: (layer,) + index_map(*idx))


def _matmul(a, b, *, layer=None, res=None, act=None, out_dtype=F32, name="matmul"):
    m, kd = a.shape
    n = b.shape[-1]
    tm = _pick(m, (1024, 512, 256, 128))
    tn = _pick(n, (1024, 768, 512, 256, 128))
    tk = _pick(kd, (2048, 1024, 512))
    nk = kd // tk
    in_specs = [pl.BlockSpec((tm, tk), lambda i, j, k: (i, k)),
                _weight_spec(b, layer, (tk, tn), lambda i, j, k: (k, j))]
    args = [a, b]
    if res is not None:
        in_specs.append(pl.BlockSpec((tm, tn), lambda i, j, k: (i, j)))
        args.append(res)
    return pl.pallas_call(
        functools.partial(_mm_kernel, nk=nk, act=act, has_res=res is not None),
        out_shape=jax.ShapeDtypeStruct((m, n), out_dtype),
        grid=(m // tm, n // tn, nk),
        in_specs=in_specs,
        out_specs=pl.BlockSpec((tm, tn), lambda i, j, k: (i, j)),
        scratch_shapes=[pltpu.VMEM((tm, tn), F32)] if nk > 1 else [],
        compiler_params=_cparams(("parallel", "parallel", "arbitrary")),
        name=name,
    )(*args)


def _norm_mm_kernel(x_ref, g_ref, b_ref, o_ref, h_ref, *, act):
    @pl.when(pl.program_id(1) == 0)
    def _():
        x = x_ref[...]
        ms = jnp.mean(x * x, axis=-1, keepdims=True)
        h_ref[...] = (x * lax.rsqrt(ms + RMS_EPS) * g_ref[...]).astype(BF16)

    y = _dot(h_ref[...], b_ref[...].astype(BF16))
    if act == "relu2":
        y = jnp.square(jnp.maximum(y, 0.0))
    o_ref[...] = y.astype(o_ref.dtype)


def _norm_matmul(x, g, b, *, layer=None, act=None, out_dtype=F32, name="norm_matmul"):
    m, kd = x.shape
    n = b.shape[-1]
    tm = _pick(m, (1024, 512, 256, 128))
    tn = _pick(n, (1024, 768, 512, 256, 128))
    return pl.pallas_call(
        functools.partial(_norm_mm_kernel, act=act),
        out_shape=jax.ShapeDtypeStruct((m, n), out_dtype),
        grid=(m // tm, n // tn),
        in_specs=[pl.BlockSpec((tm, kd), lambda i, j: (i, 0)),
                  pl.BlockSpec((1, kd), lambda i, j: (0, 0)),
                  _weight_spec(b, layer, (kd, tn), lambda i, j: (0, j))],
        out_specs=pl.BlockSpec((tm, tn), lambda i, j: (i, j)),
        scratch_shapes=[pltpu.VMEM((tm, kd), BF16)],
        compiler_params=_cparams(("parallel", "arbitrary")),
        name=name,
    )(x, g.reshape(1, kd), b)


def _out_proj_kernel(ret_ref, rw_ref, nsa_ref, w_ref, res_ref, o_ref, wb_ref):
    @pl.when(pl.program_id(1) == 0)
    def _():
        wb_ref[...] = w_ref[...].astype(BF16)

    k0 = ret_ref.shape[1]
    k1 = k0 + rw_ref.shape[1]
    acc = _dot(ret_ref[...], wb_ref[:k0, :])
    acc = acc + _dot(rw_ref[...], wb_ref[k0:k1, :])
    acc = acc + _dot(nsa_ref[...], wb_ref[k1:, :])
    o_ref[...] = acc + res_ref[...]


def _out_proj(o_ret, o_rwkv, o_nsa, w, layer, res):
    m = o_ret.shape[0]
    kd, n = w.shape[-2:]
    assert kd == o_ret.shape[1] + o_rwkv.shape[1] + o_nsa.shape[1]
    tm = _pick(m, (1024, 512, 256, 128))
    tn = _pick(n, (1024, 512, 256, 128))

    def rows(a):
        return pl.BlockSpec((tm, a.shape[1]), lambda j, i: (i, 0))

    return pl.pallas_call(
        _out_proj_kernel,
        out_shape=jax.ShapeDtypeStruct((m, n), F32),
        grid=(n // tn, m // tm),
        in_specs=[rows(o_ret), rows(o_rwkv), rows(o_nsa),
                  _weight_spec(w, layer, (kd, tn), lambda j, i: (0, j)),
                  pl.BlockSpec((tm, tn), lambda j, i: (i, j))],
        out_specs=pl.BlockSpec((tm, tn), lambda j, i: (i, j)),
        scratch_shapes=[pltpu.VMEM((kd, tn), BF16)],
        compiler_params=_cparams(("parallel", "arbitrary")),
        name="out_proj",
    )(o_ret, o_rwkv, o_nsa, w, res)


RET_CHUNK = 512


def _retention_kernel(q_ref, k_ref, v_ref, g_ref, cos_ref, sin_ref, dmat_ref, qd_ref, kd_ref, cd_ref, o_ref, state_ref):
    @pl.when(pl.program_id(0) == 0)
    def _():
        state_ref[...] = jnp.zeros_like(state_ref)

    cos = cos_ref[...]
    sin = sin_ref[...]

    def rope(x):
        return x * cos + pltpu.roll(x, RET_DH // 2, axis=1) * sin

    heads = range(RET_HEADS)
    hs = [slice(h * RET_DH, (h + 1) * RET_DH) for h in heads]
    q_all, k_all, v_all, gate = q_ref[...], k_ref[...], v_ref[...], g_ref[...]
    qb = [rope(q_all[:, s]).astype(BF16) for s in hs]
    kf = [rope(k_all[:, s]) * (RET_DH ** -0.5) for s in hs]
    kb = [kf[h].astype(BF16) for h in heads]
    vb = [v_all[:, s].astype(BF16) for s in hs]
    inner = [(_dot_t(qb[h], kb[h]) * dmat_ref[h]).astype(BF16) for h in heads]
    cross = [_dot(qb[h], state_ref[h].astype(BF16)) * qd_ref[h] for h in heads]
    kv = [_dot_tl((kf[h] * kd_ref[h]).astype(BF16), vb[h]) for h in heads]
    outs = []
    for h in heads:
        out = _dot(inner[h], vb[h]) + cross[h]
        state_ref[h] = cd_ref[h] * state_ref[h] + kv[h]
        outs.append(out * lax.rsqrt(jnp.mean(out * out, axis=-1, keepdims=True) + RMS_EPS))
    o_ref[...] = (gate * jax.nn.sigmoid(gate) * jnp.concatenate(outs, axis=1)).astype(o_ref.dtype)


def _retention(proj, cos, sin):
    t = proj.shape[0]
    chunk = RET_CHUNK
    log_gamma = np.log(1.0 - 2.0 ** (-5.0 - np.arange(RET_HEADS, dtype=np.float64)))[:, None, None]
    n = np.arange(chunk, dtype=np.float64)
    lag = n[:, None] - n[None, :]
    dmat = np.where(lag >= 0, np.exp(np.maximum(lag, 0.0)[None] * log_gamma), 0.0)
    ones = np.ones((1, 1, RET_DH))
    qd = np.exp((n + 1.0)[None, :, None] * log_gamma) * ones
    kd = np.exp((chunk - 1.0 - n)[None, :, None] * log_gamma) * ones
    cd = np.exp(chunk * log_gamma) * ones
    tables = [jnp.asarray(a, F32) for a in (dmat, qd, kd, cd)]
    base = C_RET // RET_W

    def col(off):
        return pl.BlockSpec((chunk, RET_W), lambda c, off=off: (c, base + off))

    def full(a):
        return pl.BlockSpec(a.shape, lambda c: (0, 0, 0))

    tab = pl.BlockSpec((chunk, RET_DH), lambda c: (c, 0))
    return pl.pallas_call(
        _retention_kernel,
        out_shape=jax.ShapeDtypeStruct((t, RET_W), BF16),
        grid=(t // chunk,),
        in_specs=[col(0), col(1), col(2), col(3), tab, tab] + [full(a) for a in tables],
        out_specs=pl.BlockSpec((chunk, RET_W), lambda c: (c, 0)),
        scratch_shapes=[pltpu.VMEM((RET_HEADS, RET_DH, RET_DH), F32)],
        compiler_params=_cparams(("arbitrary",)),
        name="retention",
    )(proj, proj, proj, proj, cos, sin, *tables)


def _dot_f32(a, b):
    a_hi = a.astype(BF16)
    b_hi = b.astype(BF16)
    a_lo = (a - a_hi.astype(F32)).astype(BF16)
    b_lo = (b - b_hi.astype(F32)).astype(BF16)
    return _dot(a_hi, b_hi) + (_dot(a_hi, b_lo) + _dot(a_lo, b_hi))


RW_PRE_TM = 512
_MIX_OFF = (0, 512, 1024, 1536, 1664, 1792, 2048, 2176)


def _rwkv_pre_kernel(r_ref, k_ref, v_ref, xw_ref, xa_ref, xg_ref, xv_ref,
                     rp_ref, kp_ref, vp_ref, xwp_ref, xap_ref, xgp_ref, xvp_ref,
                     mix_ref, w0_ref, w2_ref, a0_ref, a2_ref, v0_ref, v2_ref, g2_ref, kk_ref, ka_ref,
                     gsum_ref, vfirst_ref,
                     r_out, lw_out, k_out, v_out, an_out, b_out, g_out, *, use_vres):
    i = pl.program_id(0)

    def mixed(cur_ref, prev_ref, seg):
        cur = cur_ref[...]
        mix = mix_ref[:, _MIX_OFF[seg]:_MIX_OFF[seg + 1]]
        prev_row = jnp.where(i > 0, prev_ref[7:8, :], 0.0)
        rid = lax.broadcasted_iota(jnp.int32, cur.shape, 0)
        shifted = jnp.where(rid == 0, prev_row, pltpu.roll(cur, 1, axis=0))
        return cur + mix * (shifted - cur)

    r = mixed(r_ref, rp_ref, 0)
    k = mixed(k_ref, kp_ref, 1)
    v = mixed(v_ref, vp_ref, 2)
    xw = mixed(xw_ref, xwp_ref, 3)
    xa = mixed(xa_ref, xap_ref, 4)
    xg = mixed(xg_ref, xgp_ref, 5)

    z = -(w0_ref[...] + _dot_f32(jnp.tanh(xw), w2_ref[...]))
    softplus = jnp.maximum(z, 0.0) + jnp.log(1.0 + jnp.exp(-jnp.abs(z)))
    w = -softplus - 0.5
    lw_out[...] = -jnp.exp(w)
    a = jax.nn.sigmoid(a0_ref[...] + _dot_f32(xa, a2_ref[...]))
    g_out[...] = _dot_f32(jax.nn.sigmoid(xg), g2_ref[...])
    if use_vres:
        xv = mixed(xv_ref, xvp_ref, 6)
        v = v + (vfirst_ref[...] - v) * jax.nn.sigmoid(v0_ref[...] + _dot_f32(xv, v2_ref[...]))
    kk = k * kk_ref[...]
    ss = _split_dot(kk * kk, gsum_ref[...], 2)
    kk = kk / jnp.maximum(jnp.sqrt(ss), 1e-12)
    r_out[...] = r
    k_out[...] = k * (1.0 + (a - 1.0) * ka_ref[...])
    v_out[...] = v
    an_out[...] = -kk
    b_out[...] = kk * a


def _rwkv_pre(proj, mix, w0, w2, a0, a2, v0, v2, g2, k_k, k_a, gsum, v_first, use_vres):
    t = proj.shape[0]
    tm = RW_PRE_TM

    def cur(width, off):
        return pl.BlockSpec((tm, width), lambda i: (i, off // width))

    def prev(width, off):
        return pl.BlockSpec((8, width), lambda i: (jnp.maximum(i * (tm // 8) - 1, 0), off // width))

    def full(a):
        return pl.BlockSpec(a.shape, lambda i: (0,) * a.ndim)

    segs = [(RW_W, C_RW_RKV), (RW_W, C_RW_RKV + RW_W), (RW_W, C_RW_RKV + 2 * RW_W),
            (128, C_RW_XW), (128, C_RW_XA), (256, C_RW_XG), (128, C_RW_XV)]
    params = [mix, w0, w2, a0, a2, v0, v2, g2, k_k, k_a, gsum]
    row = pl.BlockSpec((tm, RW_W), lambda i: (i, 0))
    outs = pl.pallas_call(
        functools.partial(_rwkv_pre_kernel, use_vres=use_vres),
        out_shape=[jax.ShapeDtypeStruct((t, RW_W), F32)] * 7,
        grid=(t // tm,),
        in_specs=[cur(w, o) for w, o in segs] + [prev(w, o) for w, o in segs] + [full(p) for p in params] + [row],
        out_specs=[row] * 7,
        compiler_params=_cparams(("parallel",)),
        name="rwkv_pre",
    )(*([proj] * 14), *params, v_first)
    return outs


RW_CHUNK = 128


def _wkv_kernel(r_ref, lw_ref, k_ref, v_ref, an_ref, b_ref, g_ref, lng_ref, lnb_ref, rk_ref,
                gsum_ref, tril_ref, o_ref, s_ref):
    c = pl.program_id(0)
    C = RW_CHUNK
    N = RW_N

    @pl.when(c == 0)
    def _():
        s_ref[...] = jnp.zeros_like(s_ref)

    r = r_ref[...]
    lw = lw_ref[...]
    k = k_ref[...]
    v = v_ref[...]
    lw_hi = lw.astype(BF16)
    rem = lw - lw_hi.astype(F32)
    lw_mid = rem.astype(BF16)
    lw_lo = (rem - lw_mid.astype(F32)).astype(BF16)
    tril = tril_ref[...]
    lg = _dot(tril, lw_hi) + (_dot(tril, lw_mid) + _dot(tril, lw_lo))
    gam = jnp.exp(lg)
    ginv = jnp.exp(-lg)
    at = (an_ref[...] * jnp.exp(lg - lw)).astype(BF16)
    bt = (b_ref[...] * ginv).astype(BF16)
    kt = (k * ginv).astype(BF16)
    rt_f = r * gam
    rt = rt_f.astype(BF16)
    vb = v.astype(BF16)
    g_last = gam[C - 1:C, :]

    rowi = lax.broadcasted_iota(jnp.int32, (C, C), 0)
    coli = lax.broadcasted_iota(jnp.int32, (C, C), 1)
    strict = rowi > coli
    incl = rowi >= coli

    heads = range(RW_HEADS)
    hs = [slice(h * N, (h + 1) * N) for h in heads]
    bth = [bt[:, s] for s in hs]
    kth = [kt[:, s] for s in hs]
    vh = [vb[:, s] for s in hs]
    big = [_dot_t(jnp.concatenate([at[:, s], rt[:, s]], axis=0), jnp.concatenate([bth[h], kth[h]], axis=0))
           for h, s in zip(heads, hs)]
    a_ab = [jnp.where(strict, big[h][:C, :C], 0.0) for h in heads]
    a_ak = [jnp.where(strict, big[h][:C, C:], 0.0).astype(BF16) for h in heads]
    a_rb = [jnp.where(incl, big[h][C:, :C], 0.0).astype(BF16) for h in heads]
    a_rk = [jnp.where(incl, big[h][C:, C:], 0.0).astype(BF16) for h in heads]
    akv = [_dot(a_ak[h], vh[h]) for h in heads]
    def same_block(size):
        shift = size.bit_length() - 1
        return (rowi >> shift) == (coli >> shift)

    tinv = [jnp.where(same_block(2), a_ab[h], 0.0) + jnp.where(rowi == coli, 1.0, 0.0) for h in heads]
    size = 2
    while size < C:
        lower_left = same_block(2 * size) & jnp.logical_not(same_block(size))
        off = [jnp.where(lower_left, a_ab[h], 0.0).astype(BF16) for h in heads]
        tb = [tinv[h].astype(BF16) for h in heads]
        half = [_dot(tb[h], off[h]).astype(BF16) for h in heads]
        tinv = [tinv[h] + _dot(half[h], tb[h]) for h in heads]
        size *= 2
    xb = [_dot(tinv[h].astype(BF16), jnp.concatenate([at[:, hs[h]], akv[h].astype(BF16)], axis=1)).astype(BF16)
          for h in heads]
    yx = [_dot(a_rb[h], xb[h]) for h in heads]
    ykv = [_dot(a_rk[h], vh[h]) for h in heads]
    xtb = [_dot_tl(xb[h], bth[h]) for h in heads]
    vtk = [_dot_tl(vh[h], kth[h]) for h in heads]
    ys = []
    for h in heads:
        gl = g_last[:, hs[h]]
        y1 = (rt_f[:, hs[h]] + yx[h][:, :N]).astype(BF16)
        s0 = s_ref[h]
        s0b = s0.astype(BF16)
        ys.append(_dot_t(y1, s0b) + (yx[h][:, N:] + ykv[h]))
        s_ref[h] = (s0 + _dot(s0b, xtb[h][:N].astype(BF16)) + (xtb[h][N:] + vtk[h])) * gl
    y = jnp.concatenate(ys, axis=1)

    gsum = gsum_ref[...]
    inv_n = 1.0 / N
    mu = _split_dot(y, gsum, 2) * inv_n
    yc = y - mu
    var = _split_dot(yc * yc, gsum, 2) * inv_n
    yn = yc * lax.rsqrt(var + RW_GN_EPS) * lng_ref[...] + lnb_ref[...]
    bonus = _split_dot(r * k * rk_ref[...], gsum, 2)
    o_ref[...] = ((yn + bonus * v) * g_ref[...]).astype(o_ref.dtype)


def _wkv(r, lw, k, v, an, b, g, ln_g, ln_b, r_k, gsum):
    t = r.shape[0]
    C = RW_CHUNK
    tril = jnp.asarray(np.tril(np.ones((C, C), np.float32)), BF16)
    row = pl.BlockSpec((C, RW_W), lambda c: (c, 0))
    vec = pl.BlockSpec((1, RW_W), lambda c: (0, 0))
    return pl.pallas_call(
        _wkv_kernel,
        out_shape=jax.ShapeDtypeStruct((t, RW_W), BF16),
        grid=(t // C,),
        in_specs=[row] * 7 + [vec] * 3 + [pl.BlockSpec((RW_W, RW_W), lambda c: (0, 0)),
                                          pl.BlockSpec((C, C), lambda c: (0, 0))],
        out_specs=row,
        scratch_shapes=[pltpu.VMEM((RW_HEADS, RW_N, RW_N), F32)],
        compiler_params=_cparams(("arbitrary",)),
        name="wkv7",
    )(r, lw, k, v, an, b, g, ln_g, ln_b, r_k, gsum, tril)


NSA_PREP_TM = 512


N_NORM_SLABS = (NSA_W + 2 * NSA_KV_W) // LANES
N_RAW_SLABS = 4 * NSA_KV_W // LANES
N_NSA_SLABS = N_NORM_SLABS + N_RAW_SLABS + 1


def _nsa_prep_kernel(*refs):
    x_refs = refs[:N_NSA_SLABS]
    (cos_ref, sin_ref, w_ref, gmean_ref, big_ref,
     qk_ref, hmat_ref, v1_ref, gate_ref, qpad_ref, ksx_ref) = refs[N_NSA_SLABS:]
    cos = cos_ref[...]
    sin = sin_ref[...]
    gmean = gmean_ref[...]
    half = NSA_DH // 2
    tm = cos.shape[0]
    lane = lax.broadcasted_iota(jnp.int32, cos.shape, 1)
    first_half = (lane % NSA_DH) < half
    low = lane < NSA_DH
    blk_row = (pl.program_id(0) * tm + lax.broadcasted_iota(jnp.int32, cos.shape, 0)) // SLC_BLOCK
    big = big_ref[...]
    mark_lo = jnp.where(blk_row == lane - NSA_DH, big, 0.0)
    mark_hi = jnp.where(low & (blk_row == lane + NSA_DH), big, 0.0).astype(BF16)
    n_q = NSA_W // LANES
    n_k = NSA_KV_W // LANES
    for s in range(N_NORM_SLABS):
        x = x_refs[s][...]
        y = x * lax.rsqrt(_split_dot(x * x, gmean, 2) + RMS_EPS) * w_ref[s]
        rot = jnp.where(first_half, pltpu.roll(y, LANES - half, axis=1), pltpu.roll(y, half, axis=1))
        res32 = y * cos + rot * sin
        res = res32.astype(qk_ref.dtype)
        qk_ref[2 * s] = res[:, :NSA_DH]
        qk_ref[2 * s + 1] = res[:, NSA_DH:]
        if s >= n_q + n_k:
            continue
        swapped = pltpu.roll(res32, NSA_DH, axis=1)
        if s < n_q:
            qpad_ref[2 * s] = jnp.where(low, res32, 0.0).astype(BF16)
            qpad_ref[2 * s + 1] = jnp.where(low, swapped, 0.0).astype(BF16)
        else:
            g0 = 2 * (s - n_q)
            ksx_ref[g0, :, :LANES] = jnp.where(low, res32, mark_lo).astype(BF16)
            ksx_ref[g0 + 1, :, :LANES] = jnp.where(low, swapped, mark_lo).astype(BF16)
            ksx_ref[g0, :, LANES:] = mark_hi
            ksx_ref[g0 + 1, :, LANES:] = mark_hi

    rows = hmat_ref.shape[2]
    low_r = lax.broadcasted_iota(jnp.int32, (rows, LANES), 1) < NSA_DH
    for s in range(NSA_KV_W // LANES * 2):
        kind, pair = divmod(s, NSA_KV_W // LANES)
        x_ref = x_refs[N_NORM_SLABS + s]
        for t2 in range(CMP_STRIDE // 2):
            r0 = x_ref[pl.ds(2 * t2, rows, stride=CMP_STRIDE), :]
            r1 = x_ref[pl.ds(2 * t2 + 1, rows, stride=CMP_STRIDE), :]
            cols = slice(t2 * LANES, (t2 + 1) * LANES)
            hmat_ref[kind, 2 * pair, :, cols] = jnp.where(low_r, r0, pltpu.roll(r1, NSA_DH, axis=1)).astype(BF16)
            hmat_ref[kind, 2 * pair + 1, :, cols] = jnp.where(low_r, pltpu.roll(r0, NSA_DH, axis=1), r1).astype(BF16)

    one_col = jnp.where(lane == NSA_DH, 1.0, 0.0)
    for s in range(NSA_KV_W // LANES * 2):
        x = x_refs[N_NORM_SLABS + N_RAW_SLABS // 2 + s][...]
        v1_ref[2 * s] = jnp.where(low, x, one_col).astype(BF16)
        v1_ref[2 * s + 1] = jnp.where(low, pltpu.roll(x, NSA_DH, axis=1), one_col).astype(BF16)

    xg = x_refs[N_NSA_SLABS - 1][...]
    per_group = 3 * NSA_HG
    for g in range(NSA_G):
        shifted = xg if g == 0 else pltpu.roll(xg, LANES - per_group * g, axis=1)
        gate_ref[g] = jnp.where(lane < per_group, shifted, 0.0)


def _nsa_prep(proj, w_slabs, cos, sin, gmean, big):
    t = proj.shape[0]
    assert t // SLC_BLOCK <= LANES
    tm = NSA_PREP_TM
    base = C_NSA_Q // LANES
    tab = pl.BlockSpec((tm, LANES), lambda i: (i, 0))
    n_heads_out = 2 * N_NORM_SLABS
    return pl.pallas_call(
        _nsa_prep_kernel,
        out_shape=[jax.ShapeDtypeStruct((n_heads_out, t, NSA_DH), BF16),
                   jax.ShapeDtypeStruct((2, NSA_G, t // CMP_STRIDE, CMP_STRIDE * NSA_DH), BF16),
                   jax.ShapeDtypeStruct((2 * NSA_G, t, LANES), BF16),
                   jax.ShapeDtypeStruct((NSA_G, t, LANES), F32),
                   jax.ShapeDtypeStruct((NSA_HEADS, t, LANES), BF16),
                   jax.ShapeDtypeStruct((NSA_G, t, 2 * LANES), BF16)],
        grid=(t // tm,),
        in_specs=[pl.BlockSpec((tm, LANES), lambda i, s=s: (i, base + s)) for s in range(N_NSA_SLABS)]
        + [tab, tab, pl.BlockSpec((N_NORM_SLABS, 1, LANES), lambda i: (0, 0, 0)),
           pl.BlockSpec((LANES, LANES), lambda i: (0, 0)), pl.BlockSpec((1, LANES), lambda i: (0, 0))],
        out_specs=[pl.BlockSpec((n_heads_out, tm, NSA_DH), lambda i: (0, i, 0)),
                   pl.BlockSpec((2, NSA_G, tm // CMP_STRIDE, CMP_STRIDE * NSA_DH), lambda i: (0, 0, i, 0)),
                   pl.BlockSpec((2 * NSA_G, tm, LANES), lambda i: (0, i, 0)),
                   pl.BlockSpec((NSA_G, tm, LANES), lambda i: (0, i, 0)),
                   pl.BlockSpec((NSA_HEADS, tm, LANES), lambda i: (0, i, 0)),
                   pl.BlockSpec((NSA_G, tm, 2 * LANES), lambda i: (0, i, 0))],
        compiler_params=_cparams(("parallel",)),
        name="nsa_prep",
    )(*([proj] * N_NSA_SLABS), cos, sin, w_slabs, gmean, big)


def _compress_kernel(h_ref, w1_ref, w2_ref, pe_ref, nw_ref, cos_ref, sin_ref, o_ref):
    kind = pl.program_id(0)
    nc = h_ref.shape[2]
    half_in = CMP_STRIDE * NSA_DH
    hm = h_ref[0, 0]
    w1 = w1_ref[0]
    first = _dot(hm, w1[:half_in])
    second = _dot(hm, w1[half_in:])
    const = _dot(pe_ref[0], w1)[0:1, :]
    pre = first + pltpu.roll(second, nc - 1, axis=0) + const
    hid = 0.5 * pre * (1.0 + jnp.tanh(0.7978845608028654 * (pre + 0.044715 * pre * pre * pre)))
    out = _dot(hid.astype(BF16), w2_ref[0])
    y = out * lax.rsqrt(jnp.mean(out * out, axis=-1, keepdims=True) + RMS_EPS) * nw_ref[...]
    half = NSA_DH // 2
    rot = jnp.concatenate([y[:, half:], y[:, :half]], axis=1)
    roped = y * cos_ref[...] + rot * sin_ref[...]
    o_ref[0, 0] = jnp.where(kind == 0, roped, out).astype(o_ref.dtype)


def _compress(hmat, w1, w2, pe, nw, cos_c, sin_c):
    _, g, nc, width = hmat.shape
    return pl.pallas_call(
        _compress_kernel,
        out_shape=jax.ShapeDtypeStruct((2, g, nc, NSA_DH), BF16),
        grid=(2, g),
        in_specs=[pl.BlockSpec((1, 1, nc, width), lambda a, b: (a, b, 0, 0)),
                  pl.BlockSpec((1, 2 * width, CMP_HIDDEN), lambda a, b: (a, 0, 0)),
                  pl.BlockSpec((1, CMP_HIDDEN, NSA_DH), lambda a, b: (a, 0, 0)),
                  pl.BlockSpec((1, 8, 2 * width), lambda a, b: (a, 0, 0)),
                  pl.BlockSpec((1, NSA_DH), lambda a, b: (0, 0)),
                  pl.BlockSpec((nc, NSA_DH), lambda a, b: (0, 0)),
                  pl.BlockSpec((nc, NSA_DH), lambda a, b: (0, 0))],
        out_specs=pl.BlockSpec((1, 1, nc, NSA_DH), lambda a, b: (a, b, 0, 0)),
        compiler_params=_cparams(("parallel", "parallel")),
        name="nsa_compress",
    )(hmat, w1, w2, pe, nw, cos_c, sin_c)


SLC_KT = 1024
WIN_ROWS = 128
WIN_KEYS = WINDOW + WIN_ROWS


INT32_MIN = -2 ** 31


def _nsa_attn_kernel(q_ref, qpad_ref, gate_ref, kc_ref, vc_ref, ks_ref, vs_ref, kw_ref, vw_ref, cis_ref, ltri_ref,
                     o_ref):
    qi = pl.program_id(1)
    nc = kc_ref.shape[2]
    nb = cis_ref.shape[1]
    heads = range(NSA_HG)
    start = qi * NSA_TQ
    qh = [q_ref[hh] for hh in heads]
    tok = start + lax.broadcasted_iota(jnp.int32, (NSA_TQ, 1), 0)

    def softmax_terms(scores):
        return [jnp.exp(s - jnp.max(s, axis=-1, keepdims=True)) for s in scores]

    def cmp_branch(width):
        kc = kc_ref[0, 0, :width, :]
        vc = vc_ref[0, 0, :width, :]
        cend = lax.broadcasted_iota(jnp.int32, (1, width), 1) * CMP_STRIDE + (CMP_BLOCK - 1)
        bias_c = jnp.where(cend <= tok, 0.0, NEG_INF)
        ec = softmax_terms([_dot_t(qh[hh], kc) + bias_c for hh in heads])
        sees_any = tok >= CMP_BLOCK - 1
        pc = [ec[hh] * jnp.where(sees_any, 1.0 / jnp.sum(ec[hh], axis=-1, keepdims=True), 0.0) for hh in heads]
        outs_c = tuple(_dot(pc[hh].astype(BF16), vc) for hh in heads)
        psum = pc[0]
        for hh in range(1, NSA_HG):
            psum = psum + pc[hh]
        return outs_c, _split_dot(psum, cis_ref[:width, :], 2)

    n_quarters = 4
    quarter = nc // n_quarters
    n_visible = (start + NSA_TQ - CMP_BLOCK) // CMP_STRIDE + 1
    which = jnp.clip((n_visible - 1) // quarter, 0, n_quarters - 1)
    o_c, score = lax.switch(which, [functools.partial(cmp_branch, (r + 1) * quarter) for r in range(n_quarters)])
    blk = lax.broadcasted_iota(jnp.int32, (1, nb), 1)
    cur = tok // SLC_BLOCK
    forced = (blk == 0) | (blk == cur) | (blk == cur - 1)
    score = jnp.where(forced, FORCED_SCORE, score)
    score = jnp.where(blk <= cur, score, -jnp.inf)


    bits = lax.bitcast_convert_type(score.T, jnp.int32)
    key = bits ^ ((bits >> 31) & 0x7FFFFFFF)
    n_sel = min(SLC_TOPK, nb)

    def enough(c):
        return jnp.sum(jnp.where(key >= c, 1.0, 0.0), axis=0, keepdims=True) >= n_sel

    def enough3(c1, c2, c3):
        packed = jnp.where(key >= c3, 65793.0, jnp.where(key >= c2, 257.0, jnp.where(key >= c1, 1.0, 0.0)))
        tot = jnp.sum(packed, axis=0, keepdims=True).astype(jnp.int32)
        return (tot & 255) >= n_sel, ((tot >> 8) & 255) >= n_sel, (tot >> 16) >= n_sel

    zero_row = jnp.zeros((1, NSA_TQ), jnp.int32)
    thr = jnp.where(enough(zero_row), zero_row, INT32_MIN)
    for hi in range(30, 0, -2):
        c1 = thr + (1 << (hi - 1))
        c2 = thr + (1 << hi)
        c3 = c2 + (1 << (hi - 1))
        e1, e2, e3 = enough3(c1, c2, c3)
        thr = jnp.where(e3, c3, jnp.where(e2, c2, jnp.where(e1, c1, thr)))
    c1 = thr + 1
    thr = jnp.where(enough(c1), c1, thr)
    above = key > thr
    tied = key == thr
    need = n_sel - jnp.sum(jnp.where(above, 1.0, 0.0), axis=0, keepdims=True)
    tied_before = _dot(ltri_ref[...], jnp.where(tied, 1.0, 0.0).astype(BF16))
    sel_t = above | (tied & (tied_before < need))
    sel = jnp.where(sel_t, 1.0, 0.0).T
    if nb < LANES:
        sel = jnp.concatenate([sel, jnp.zeros((NSA_TQ, LANES - nb), F32)], axis=1)
    lane_q = lax.broadcasted_iota(jnp.int32, (NSA_TQ, LANES), 1)
    sel_sw = pltpu.roll(sel, NSA_DH, axis=1)
    sel_lo = jnp.where(lane_q >= NSA_DH, sel_sw, 0.0)
    sel_hi = jnp.where(lane_q < NSA_DH, sel_sw, 0.0)
    qx = [jnp.concatenate([qpad_ref[hh].astype(F32) + sel_lo, sel_hi], axis=1).astype(BF16) for hh in heads]

    def window_rows(r0):
        kbase = pl.multiple_of(jnp.maximum(start + r0 - WINDOW, 0), WIN_ROWS)
        kw = kw_ref[0, pl.ds(kbase, WIN_KEYS), :]
        vw = vw_ref[0, pl.ds(kbase, WIN_KEYS), :]
        lag = tok[r0:r0 + WIN_ROWS] - (kbase + lax.broadcasted_iota(jnp.int32, (1, WIN_KEYS), 1))
        bias_w = jnp.where((lag >= 0) & (lag < WINDOW), 0.0, NEG_INF)
        sw = [_dot_t(qh[hh][r0:r0 + WIN_ROWS], kw) + bias_w for hh in heads]
        ew = [jnp.exp((s - jnp.max(s, axis=-1, keepdims=True)).astype(BF16)) for s in sw]
        return [_dot(ew[hh], vw) for hh in heads]

    win_parts = [window_rows(r0) for r0 in range(0, NSA_TQ, WIN_ROWS)]
    acc_w = [jnp.concatenate([part[hh] for part in win_parts], axis=0) for hh in heads]

    n_tiles = (start + NSA_TQ + SLC_KT - 1) // SLC_KT

    def sel_tile(j, carry, causal, width=SLC_KT):
        ms, accs = carry
        k0 = pl.multiple_of(j * SLC_KT, SLC_KT)
        kt = ks_ref[0, pl.ds(k0, width), :]
        vt = vs_ref[0, pl.ds(k0, width), :]
        sj = [_dot_t(qx[hh], kt) for hh in heads]
        if causal:
            future = (k0 + lax.broadcasted_iota(jnp.int32, (1, width), 1)) > tok
            sj = [jnp.where(future, NEG_INF, s) for s in sj]
        m_new = [jnp.maximum(ms[hh], jnp.max(sj[hh], axis=-1, keepdims=True)) for hh in heads]
        pj = [jnp.exp((sj[hh] - m_new[hh]).astype(BF16)) for hh in heads]
        accs = [jnp.exp(ms[hh] - m_new[hh]) * accs[hh] + _dot(pj[hh], vt) for hh in heads]
        return tuple(m_new), tuple(accs)

    m0 = tuple(jnp.full((NSA_TQ, 1), NEG_INF, F32) for _ in heads)
    a0 = tuple(jnp.zeros((NSA_TQ, LANES), F32) for _ in heads)
    carry = lax.fori_loop(0, n_tiles - 1, functools.partial(sel_tile, causal=False), (m0, a0))
    per_tile = SLC_KT // NSA_TQ
    last = [functools.partial(sel_tile, n_tiles - 1, causal=True, width=(r + 1) * NSA_TQ) for r in range(per_tile)]
    _, acc_s = lax.switch(qi % per_tile, last, carry)

    gt = jax.nn.sigmoid(gate_ref[0])
    outs = []
    for hh in heads:
        g0, g1, g2 = (gt[:, 3 * hh + br:3 * hh + br + 1] for br in range(3))
        scale_s = g1 / acc_s[hh][:, NSA_DH:NSA_DH + 1]
        scale_w = g2 / acc_w[hh][:, NSA_DH:NSA_DH + 1]
        outs.append(g0 * o_c[hh] + scale_s * acc_s[hh][:, :NSA_DH] + scale_w * acc_w[hh][:, :NSA_DH])
    o_ref[...] = jnp.concatenate(outs, axis=1).astype(o_ref.dtype)


def _nsa_attn(qk_hm, qpad, ksx, gates, cmp_kv, v1, cis, ltri):
    t = qk_hm.shape[1]
    nq = t // NSA_TQ
    nc = cmp_kv.shape[2]
    nb = cis.shape[1]
    ks_spec = pl.BlockSpec((1, t, 2 * LANES), lambda g, i: (g, 0, 0))
    kw_spec = pl.BlockSpec((1, t, NSA_DH), lambda g, i: (NSA_HEADS + NSA_G + g, 0, 0))
    vs_spec = pl.BlockSpec((1, t, LANES), lambda g, i: (g, 0, 0))
    vw_spec = pl.BlockSpec((1, t, LANES), lambda g, i: (NSA_G + g, 0, 0))
    return pl.pallas_call(
        _nsa_attn_kernel,
        out_shape=jax.ShapeDtypeStruct((t, NSA_W), BF16),
        grid=(NSA_G, nq),
        in_specs=[pl.BlockSpec((NSA_HG, NSA_TQ, NSA_DH), lambda g, i: (g, i, 0)),
                  pl.BlockSpec((NSA_HG, NSA_TQ, LANES), lambda g, i: (g, i, 0)),
                  pl.BlockSpec((1, NSA_TQ, LANES), lambda g, i: (g, i, 0)),
                  pl.BlockSpec((1, 1, nc, NSA_DH), lambda g, i: (0, g, 0, 0)),
                  pl.BlockSpec((1, 1, nc, NSA_DH), lambda g, i: (1, g, 0, 0)),
                  ks_spec, vs_spec, kw_spec, vw_spec,
                  pl.BlockSpec((nc, nb), lambda g, i: (0, 0)),
                  pl.BlockSpec((nb, nb), lambda g, i: (0, 0))],
        out_specs=pl.BlockSpec((NSA_TQ, NSA_HG * NSA_DH), lambda g, i: (i, g)),
        compiler_params=_cparams(("parallel", "arbitrary")),
        name="nsa_attention",
    )(qk_hm, qpad, gates, cmp_kv, cmp_kv, ksx, v1, qk_hm, v1, cis, ltri)


def _rope_tables(pos, dh, reps):
    half = dh // 2
    inv_freq = ROPE_THETA ** (-np.arange(half, dtype=np.float64) / half)
    ang = np.asarray(pos, np.float64)[:, None] * inv_freq[None, :]
    cos = np.cos(ang)
    sin = np.sin(ang)
    cos_t = np.tile(np.concatenate([cos, cos], axis=1), (1, reps))
    sin_t = np.tile(np.concatenate([-sin, sin], axis=1), (1, reps))
    return jnp.asarray(cos_t, F32), jnp.asarray(sin_t, F32)


def _pad_cols(a, width):
    return jnp.pad(a, ((0, 0), (0, width - a.shape[1])))


def _pad_rows(a, height):
    return jnp.pad(a, ((0, height - a.shape[0]), (0, 0)))


def _pack_w_in_kernel(w_ref, xv_ref, o_ref):
    o_ref[...] = _pack_w_in(w_ref[...], xv_ref[...]).astype(o_ref.dtype)


def _pack_w_in_call(w_in, layer, xv):
    _, d, n_in = w_in.shape
    tk = 256
    return pl.pallas_call(
        _pack_w_in_kernel,
        out_shape=jax.ShapeDtypeStruct((d, PROJ_W), BF16),
        grid=(d // tk,),
        in_specs=[pl.BlockSpec((None, tk, n_in), lambda i: (layer, i, 0)),
                  pl.BlockSpec((tk, LANES), lambda i: (i, 0))],
        out_specs=pl.BlockSpec((tk, PROJ_W), lambda i: (i, 0)),
        compiler_params=_cparams(("parallel",)),
        name="pack_w_in",
    )(w_in, xv)


def _pack_w_in(w_l, xv):
    d = w_l.shape[0]
    o = 0
    ret = w_l[:, o:o + 4 * RET_W]; o += 4 * RET_W
    rkv = w_l[:, o:o + 3 * RW_W]; o += 3 * RW_W
    xw = w_l[:, o:o + RW_DECAY_RANK]; o += RW_DECAY_RANK
    xa = w_l[:, o:o + RW_A_RANK]; o += RW_A_RANK
    xg = w_l[:, o:o + RW_GATE_RANK]; o += RW_GATE_RANK
    q = w_l[:, o:o + NSA_W]; o += NSA_W
    kc, vc, ks, vs, kw, vw = (w_l[:, o + i * NSA_KV_W:o + (i + 1) * NSA_KV_W] for i in range(6))
    o += 6 * NSA_KV_W
    gates = w_l[:, o:o + 3 * NSA_HEADS]
    return jnp.concatenate([ret, rkv, _pad_cols(xw, LANES), _pad_cols(xa, LANES), xg, xv,
                            q, ks, kw, kc, vc, vs, vw, _pad_cols(gates, LANES)], axis=1)


def _pack_mix(mix, vres_mix):
    o = 3 * RW_W
    xw = mix[o:o + RW_DECAY_RANK]; o += RW_DECAY_RANK
    xa = mix[o:o + RW_A_RANK]; o += RW_A_RANK
    xg = mix[o:o + RW_GATE_RANK]
    z = lambda n: jnp.zeros((n,), mix.dtype)
    xv = z(LANES) if vres_mix is None else jnp.concatenate([vres_mix, z(LANES - RW_V_RANK)])
    return jnp.concatenate([mix[:3 * RW_W], xw, z(LANES - RW_DECAY_RANK), xa, z(LANES - RW_A_RANK), xg, xv])[None, :]


def kernel(x, ln1_g, w_in, w_in_vres, rwkv_mix, rwkv_vres_mix, rwkv_w0, rwkv_w2, rwkv_a0, rwkv_a2, rwkv_v0, rwkv_v2, rwkv_g2, rwkv_k_k, rwkv_k_a, rwkv_r_k, rwkv_ln_g, rwkv_ln_b, nsa_q_norm, nsa_k_norm, nsa_cmp_pe, nsa_cmp_k_w1, nsa_cmp_k_w2, nsa_cmp_v_w1, nsa_cmp_v_w2, w_out, ln2_g, w_up, w_down):
    bsz, t, d = x.shape
    assert bsz == 1
    depth = w_in.shape[0]
    nc = t // CMP_STRIDE
    nb = t // SLC_BLOCK
    xs = x.reshape(t, d)

    pos = np.arange(t)
    cos_r, sin_r = _rope_tables(pos, RET_DH, 1)
    cos_n, sin_n = _rope_tables(pos, NSA_DH, 2)
    cos_c, sin_c = _rope_tables(np.arange(nc) * CMP_STRIDE + (CMP_BLOCK - 1), NSA_DH, 1)
    lane_head = np.arange(RW_W) // RW_N
    gsum = jnp.asarray(lane_head[:, None] == lane_head[None, :], BF16)
    lane_h2 = np.arange(LANES) // NSA_DH
    gmean = jnp.asarray((lane_h2[:, None] == lane_h2[None, :]) / float(NSA_DH), BF16)
    cstart = np.arange(nc) * CMP_STRIDE
    sstart = np.arange(nb) * SLC_BLOCK
    cis = jnp.asarray((cstart[:, None] <= sstart[None, :] + SLC_BLOCK - 1)
                      & (cstart[:, None] + CMP_BLOCK - 1 >= sstart[None, :]), BF16)

    ltri = jnp.asarray(np.tril(np.ones((nb, nb), np.float32), -1), BF16)

    v_first = jnp.zeros((t, RW_W), F32)
    for l in range(depth):
        vres = l > 0
        xv_w = _pad_cols(w_in_vres[l - 1], LANES) if vres else jnp.zeros((d, LANES), F32)
        w_cat = _pack_w_in_call(w_in, l, xv_w)
        proj = _norm_matmul(xs, ln1_g[l], w_cat, name="in_proj")

        o_ret = _retention(proj, cos_r, sin_r)

        row = lambda a: a.reshape(1, -1)
        mix = _pack_mix(rwkv_mix[l], rwkv_vres_mix[l - 1] if vres else None)
        v0 = row(rwkv_v0[l - 1]) if vres else jnp.zeros((1, RW_W), F32)
        v2 = _pad_rows(rwkv_v2[l - 1], LANES) if vres else jnp.zeros((LANES, RW_W), F32)
        r_, lw_, k_, v_, an_, b_, g_ = _rwkv_pre(
            proj, mix, row(rwkv_w0[l]), _pad_rows(rwkv_w2[l], LANES), row(rwkv_a0[l]), _pad_rows(rwkv_a2[l], LANES),
            v0, v2, rwkv_g2[l], row(rwkv_k_k[l]), row(rwkv_k_a[l]), gsum, v_first, vres)
        if not vres:
            v_first = v_
        o_rwkv = _wkv(r_, lw_, k_, v_, an_, b_, g_, row(rwkv_ln_g[l]), row(rwkv_ln_b[l]), row(rwkv_r_k[l]), gsum)

        n_q = NSA_W // LANES
        n_k = NSA_KV_W // LANES
        w_slabs = jnp.stack([jnp.tile(nsa_q_norm[l] * (NSA_DH ** -0.5), 2)] * n_q
                            + [jnp.tile(nsa_k_norm[l, 1], 2)] * n_k + [jnp.tile(nsa_k_norm[l, 2], 2)] * n_k)[:, None, :]
        s_max = 1.02 * jnp.max(jnp.abs(nsa_q_norm[l])) * jnp.max(jnp.abs(nsa_k_norm[l, 1])) * (NSA_DH ** 0.5)
        big = jnp.full((1, LANES), jnp.exp2(jnp.ceil(jnp.log2(2.0 * s_max + 128.0))), F32)
        qk_hm, hmat, v1, gates, qpad, ksx = _nsa_prep(proj, w_slabs, cos_n, sin_n, gmean, big)
        w1 = jnp.stack([nsa_cmp_k_w1[l], nsa_cmp_v_w1[l]]).astype(BF16)
        w2 = jnp.stack([nsa_cmp_k_w2[l], nsa_cmp_v_w2[l]]).astype(BF16)
        pe = jnp.broadcast_to(nsa_cmp_pe[l].reshape(2, 1, CMP_BLOCK * NSA_DH), (2, 8, CMP_BLOCK * NSA_DH)).astype(BF16)
        cmp_kv = _compress(hmat, w1, w2, pe, row(nsa_k_norm[l, 0]), cos_c, sin_c)
        o_nsa = _nsa_attn(qk_hm, qpad, ksx, gates, cmp_kv, v1, cis, ltri)

        xs = _out_proj(o_ret, o_rwkv, o_nsa, w_out, l, xs)

        up = _norm_matmul(xs, ln2_g[l], w_up, layer=l, act="relu2", out_dtype=BF16, name="mlp_up")
        xs = _matmul(up, w_down, layer=l, res=xs, name="mlp_down")
    return xs.reshape(bsz, t, d)
```

```python
import functools

import numpy as np
import jax
import jax.numpy as jnp
from jax import lax
from jax.experimental import pallas as pl
from jax.experimental.pallas import tpu as pltpu

F32 = jnp.float32
BF16 = jnp.bfloat16

D_MODEL = 2048
RET_HEADS, RET_DH = 4, 128
RET_W = RET_HEADS * RET_DH
RW_HEADS, RW_N = 8, 64
RW_W = RW_HEADS * RW_N
RW_DECAY_RANK, RW_A_RANK, RW_V_RANK, RW_GATE_RANK = 96, 96, 64, 256
RW_GN_EPS = 64e-5
NSA_HEADS, NSA_G, NSA_DH = 16, 4, 64
NSA_HG = NSA_HEADS // NSA_G
NSA_W = NSA_HEADS * NSA_DH
NSA_KV_W = NSA_G * NSA_DH
CMP_BLOCK, CMP_STRIDE, CMP_HIDDEN = 32, 16, 256
SLC_BLOCK, SLC_TOPK, WINDOW = 64, 16, 512
NSA_TQ = 512
ROPE_THETA = 10000.0
MLP_HIDDEN = 4 * D_MODEL
RMS_EPS = 1e-6
NEG_INF = -1e30
FORCED_SCORE = 1e9

LANES = 128
VMEM_LIMIT = 56 * 1024 * 1024

C_RET = 0
C_RW_RKV = 2048
C_RW_XW = 3584
C_RW_XA = 3712
C_RW_XG = 3840
C_RW_XV = 4096
C_NSA_Q = 4224
C_NSA_KS = 5248
C_NSA_KW = 5504
C_NSA_KC = 5760
C_NSA_VC = 6016
C_NSA_VS = 6272
C_NSA_VW = 6528
C_NSA_GATE = 6784
PROJ_W = 6912


def _cparams(sem):
    return pltpu.CompilerParams(dimension_semantics=sem, vmem_limit_bytes=VMEM_LIMIT)


def _dot(a, b):
    return jnp.dot(a, b, preferred_element_type=F32)


def _dot_t(a, b):
    return lax.dot_general(a, b, (((1,), (1,)), ((), ())), preferred_element_type=F32)


def _dot_tl(a, b):
    return lax.dot_general(a, b, (((0,), (0,)), ((), ())), preferred_element_type=F32)


def _split_dot(x, m_bf16, passes):
    hi = x.astype(BF16)
    acc = _dot(hi, m_bf16)
    rem = x - hi.astype(F32)
    for _ in range(passes - 1):
        piece = rem.astype(BF16)
        acc = acc + _dot(piece, m_bf16)
        rem = rem - piece.astype(F32)
    return acc


def _mm_kernel(*refs, nk, act, has_res):
    a_ref, b_ref = refs[:2]
    r_ref = refs[2] if has_res else None
    o_ref = refs[3] if has_res else refs[2]

    def finish(y):
        if act == "relu2":
            y = jnp.square(jnp.maximum(y, 0.0))
        if has_res:
            y = y + r_ref[...]
        o_ref[...] = y.astype(o_ref.dtype)

    part = _dot(a_ref[...], b_ref[...].astype(BF16))
    if nk == 1:
        finish(part)
        return
    acc_ref = refs[-1]
    k = pl.program_id(2)

    @pl.when(k == 0)
    def _():
        acc_ref[...] = part

    @pl.when(k > 0)
    def _():
        acc_ref[...] += part

    @pl.when(k == nk - 1)
    def _():
        finish(acc_ref[...])


def _pick(n, cands):
    for c in cands:
        if n % c == 0:
            return c
    raise ValueError(f"no tile for {n}")


def _weight_spec(b, layer, block, index_map):
    if layer is None:
        return pl.BlockSpec(block, index_map)
    return pl.BlockSpec((None,) + block, lambda *idx: (layer,) + index_map(*idx))


def _matmul(a, b, *, layer=None, res=None, act=None, out_dtype=F32, name="matmul"):
    m, kd = a.shape
    n = b.shape[-1]
    tm = _pick(m, (1024, 512, 256, 128))
    tn = _pick(n, (1024, 768, 512, 256, 128))
    tk = _pick(kd, (2048, 1024, 512))
    nk = kd // tk
    in_specs = [pl.BlockSpec((tm, tk), lambda i, j, k: (i, k)),
                _weight_spec(b, layer, (tk, tn), lambda i, j, k: (k, j))]
    args = [a, b]
    if res is not None:
        in_specs.append(pl.BlockSpec((tm, tn), lambda i, j, k: (i, j)))
        args.append(res)
    return pl.pallas_call(
        functools.partial(_mm_kernel, nk=nk, act=act, has_res=res is not None),
        out_shape=jax.ShapeDtypeStruct((m, n), out_dtype),
        grid=(m // tm, n // tn, nk),
        in_specs=in_specs,
        out_specs=pl.BlockSpec((tm, tn), lambda i, j, k: (i, j)),
        scratch_shapes=[pltpu.VMEM((tm, tn), F32)] if nk > 1 else [],
        compiler_params=_cparams(("parallel", "parallel", "arbitrary")),
        name=name,
    )(*args)


def _norm_mm_kernel(x_ref, g_ref, b_ref, o_ref, h_ref, *, act):
    @pl.when(pl.program_id(1) == 0)
    def _():
        x = x_ref[...]
        ms = jnp.mean(x * x, axis=-1, keepdims=True)
        h_ref[...] = (x * lax.rsqrt(ms + RMS_EPS) * g_ref[...]).astype(BF16)

    y = _dot(h_ref[...], b_ref[...].astype(BF16))
    if act == "relu2":
        y = jnp.square(jnp.maximum(y, 0.0))
    o_ref[...] = y.astype(o_ref.dtype)


def _norm_matmul(x, g, b, *, layer=None, act=None, out_dtype=F32, name="norm_matmul"):
    m, kd = x.shape
    n = b.shape[-1]
    tm = _pick(m, (1024, 512, 256, 128))
    tn = _pick(n, (1024, 768, 512, 256, 128))
    return pl.pallas_call(
        functools.partial(_norm_mm_kernel, act=act),
        out_shape=jax.ShapeDtypeStruct((m, n), out_dtype),
        grid=(m // tm, n // tn),
        in_specs=[pl.BlockSpec((tm, kd), lambda i, j: (i, 0)),
                  pl.BlockSpec((1, kd), lambda i, j: (0, 0)),
                  _weight_spec(b, layer, (kd, tn), lambda i, j: (0, j))],
        out_specs=pl.BlockSpec((tm, tn), lambda i, j: (i, j)),
        scratch_shapes=[pltpu.VMEM((tm, kd), BF16)],
        compiler_params=_cparams(("parallel", "arbitrary")),
        name=name,
    )(x, g.reshape(1, kd), b)


def _out_proj_kernel(ret_ref, rw_ref, nsa_ref, w_ref, res_ref, o_ref, wb_ref):
    @pl.when(pl.program_id(1) == 0)
    def _():
        wb_ref[...] = w_ref[...].astype(BF16)

    k0 = ret_ref.shape[1]
    k1 = k0 + rw_ref.shape[1]
    acc = _dot(ret_ref[...], wb_ref[:k0, :])
    acc = acc + _dot(rw_ref[...], wb_ref[k0:k1, :])
    acc = acc + _dot(nsa_ref[...], wb_ref[k1:, :])
    o_ref[...] = acc + res_ref[...]


def _out_proj(o_ret, o_rwkv, o_nsa, w, layer, res):
    m = o_ret.shape[0]
    kd, n = w.shape[-2:]
    assert kd == o_ret.shape[1] + o_rwkv.shape[1] + o_nsa.shape[1]
    tm = _pick(m, (1024, 512, 256, 128))
    tn = _pick(n, (1024, 512, 256, 128))

    def rows(a):
        return pl.BlockSpec((tm, a.shape[1]), lambda j, i: (i, 0))

    return pl.pallas_call(
        _out_proj_kernel,
        out_shape=jax.ShapeDtypeStruct((m, n), F32),
        grid=(n // tn, m // tm),
        in_specs=[rows(o_ret), rows(o_rwkv), rows(o_nsa),
                  _weight_spec(w, layer, (kd, tn), lambda j, i: (0, j)),
                  pl.BlockSpec((tm, tn), lambda j, i: (i, j))],
        out_specs=pl.BlockSpec((tm, tn), lambda j, i: (i, j)),
        scratch_shapes=[pltpu.VMEM((kd, tn), BF16)],
        compiler_params=_cparams(("parallel", "arbitrary")),
        name="out_proj",
    )(o_ret, o_rwkv, o_nsa, w, res)


RET_CHUNK = 512


def _retention_kernel(q_ref, k_ref, v_ref, g_ref, cos_ref, sin_ref, dmat_ref, qd_ref, kd_ref, cd_ref, o_ref, state_ref):
    @pl.when(pl.program_id(0) == 0)
    def _():
        state_ref[...] = jnp.zeros_like(state_ref)

    cos = cos_ref[...]
    sin = sin_ref[...]

    def rope(x):
        return x * cos + pltpu.roll(x, RET_DH // 2, axis=1) * sin

    heads = range(RET_HEADS)
    hs = [slice(h * RET_DH, (h + 1) * RET_DH) for h in heads]
    q_all, k_all, v_all, gate = q_ref[...], k_ref[...], v_ref[...], g_ref[...]
    qb = [rope(q_all[:, s]).astype(BF16) for s in hs]
    kf = [rope(k_all[:, s]) * (RET_DH ** -0.5) for s in hs]
    kb = [kf[h].astype(BF16) for h in heads]
    vb = [v_all[:, s].astype(BF16) for s in hs]
    inner = [(_dot_t(qb[h], kb[h]) * dmat_ref[h]).astype(BF16) for h in heads]
    cross = [_dot(qb[h], state_ref[h].astype(BF16)) * qd_ref[h] for h in heads]
    kv = [_dot_tl((kf[h] * kd_ref[h]).astype(BF16), vb[h]) for h in heads]
    outs = []
    for h in heads:
        out = _dot(inner[h], vb[h]) + cross[h]
        state_ref[h] = cd_ref[h] * state_ref[h] + kv[h]
        outs.append(out * lax.rsqrt(jnp.mean(out * out, axis=-1, keepdims=True) + RMS_EPS))
    o_ref[...] = (gate * jax.nn.sigmoid(gate) * jnp.concatenate(outs, axis=1)).astype(o_ref.dtype)


def _retention(proj, cos, sin):
    t = proj.shape[0]
    chunk = RET_CHUNK
    log_gamma = np.log(1.0 - 2.0 ** (-5.0 - np.arange(RET_HEADS, dtype=np.float64)))[:, None, None]
    n = np.arange(chunk, dtype=np.float64)
    lag = n[:, None] - n[None, :]
    dmat = np.where(lag >= 0, np.exp(np.maximum(lag, 0.0)[None] * log_gamma), 0.0)
    ones = np.ones((1, 1, RET_DH))
    qd = np.exp((n + 1.0)[None, :, None] * log_gamma) * ones
    kd = np.exp((chunk - 1.0 - n)[None, :, None] * log_gamma) * ones
    cd = np.exp(chunk * log_gamma) * ones
    tables = [jnp.asarray(a, F32) for a in (dmat, qd, kd, cd)]
    base = C_RET // RET_W

    def col(off):
        return pl.BlockSpec((chunk, RET_W), lambda c, off=off: (c, base + off))

    def full(a):
        return pl.BlockSpec(a.shape, lambda c: (0, 0, 0))

    tab = pl.BlockSpec((chunk, RET_DH), lambda c: (c, 0))
    return pl.pallas_call(
        _retention_kernel,
        out_shape=jax.ShapeDtypeStruct((t, RET_W), BF16),
        grid=(t // chunk,),
        in_specs=[col(0), col(1), col(2), col(3), tab, tab] + [full(a) for a in tables],
        out_specs=pl.BlockSpec((chunk, RET_W), lambda c: (c, 0)),
        scratch_shapes=[pltpu.VMEM((RET_HEADS, RET_DH, RET_DH), F32)],
        compiler_params=_cparams(("arbitrary",)),
        name="retention",
    )(proj, proj, proj, proj, cos, sin, *tables)


def _dot_f32(a, b):
    a_hi = a.astype(BF16)
    b_hi = b.astype(BF16)
    a_lo = (a - a_hi.astype(F32)).astype(BF16)
    b_lo = (b - b_hi.astype(F32)).astype(BF16)
    return _dot(a_hi, b_hi) + (_dot(a_hi, b_lo) + _dot(a_lo, b_hi))


RW_PRE_TM = 512
_MIX_OFF = (0, 512, 1024, 1536, 1664, 1792, 2048, 2176)


def _rwkv_pre_kernel(r_ref, k_ref, v_ref, xw_ref, xa_ref, xg_ref, xv_ref,
                     rp_ref, kp_ref, vp_ref, xwp_ref, xap_ref, xgp_ref, xvp_ref,
                     mix_ref, w0_ref, w2_ref, a0_ref, a2_ref, v0_ref, v2_ref, g2_ref, kk_ref, ka_ref,
                     gsum_ref, vfirst_ref,
                     r_out, lw_out, k_out, v_out, an_out, b_out, g_out, *, use_vres):
    i = pl.program_id(0)

    def mixed(cur_ref, prev_ref, seg):
        cur = cur_ref[...]
        mix = mix_ref[:, _MIX_OFF[seg]:_MIX_OFF[seg + 1]]
        prev_row = jnp.where(i > 0, prev_ref[7:8, :], 0.0)
        rid = lax.broadcasted_iota(jnp.int32, cur.shape, 0)
        shifted = jnp.where(rid == 0, prev_row, pltpu.roll(cur, 1, axis=0))
        return cur + mix * (shifted - cur)

    r = mixed(r_ref, rp_ref, 0)
    k = mixed(k_ref, kp_ref, 1)
    v = mixed(v_ref, vp_ref, 2)
    xw = mixed(xw_ref, xwp_ref, 3)
    xa = mixed(xa_ref, xap_ref, 4)
    xg = mixed(xg_ref, xgp_ref, 5)

    z = -(w0_ref[...] + _dot_f32(jnp.tanh(xw), w2_ref[...]))
    softplus = jnp.maximum(z, 0.0) + jnp.log(1.0 + jnp.exp(-jnp.abs(z)))
    w = -softplus - 0.5
    lw_out[...] = -jnp.exp(w)
    a = jax.nn.sigmoid(a0_ref[...] + _dot_f32(xa, a2_ref[...]))
    g_out[...] = _dot_f32(jax.nn.sigmoid(xg), g2_ref[...])
    if use_vres:
        xv = mixed(xv_ref, xvp_ref, 6)
        v = v + (vfirst_ref[...] - v) * jax.nn.sigmoid(v0_ref[...] + _dot_f32(xv, v2_ref[...]))
    kk = k * kk_ref[...]
    ss = _split_dot(kk * kk, gsum_ref[...], 2)
    kk = kk / jnp.maximum(jnp.sqrt(ss), 1e-12)
    r_out[...] = r
    k_out[...] = k * (1.0 + (a - 1.0) * ka_ref[...])
    v_out[...] = v
    an_out[...] = -kk
    b_out[...] = kk * a


def _rwkv_pre(proj, mix, w0, w2, a0, a2, v0, v2, g2, k_k, k_a, gsum, v_first, use_vres):
    t = proj.shape[0]
    tm = RW_PRE_TM

    def cur(width, off):
        return pl.BlockSpec((tm, width), lambda i: (i, off // width))

    def prev(width, off):
        return pl.BlockSpec((8, width), lambda i: (jnp.maximum(i * (tm // 8) - 1, 0), off // width))

    def full(a):
        return pl.BlockSpec(a.shape, lambda i: (0,) * a.ndim)

    segs = [(RW_W, C_RW_RKV), (RW_W, C_RW_RKV + RW_W), (RW_W, C_RW_RKV + 2 * RW_W),
            (128, C_RW_XW), (128, C_RW_XA), (256, C_RW_XG), (128, C_RW_XV)]
    params = [mix, w0, w2, a0, a2, v0, v2, g2, k_k, k_a, gsum]
    row = pl.BlockSpec((tm, RW_W), lambda i: (i, 0))
    outs = pl.pallas_call(
        functools.partial(_rwkv_pre_kernel, use_vres=use_vres),
        out_shape=[jax.ShapeDtypeStruct((t, RW_W), F32)] * 7,
        grid=(t // tm,),
        in_specs=[cur(w, o) for w, o in segs] + [prev(w, o) for w, o in segs] + [full(p) for p in params] + [row],
        out_specs=[row] * 7,
        compiler_params=_cparams(("parallel",)),
        name="rwkv_pre",
    )(*([proj] * 14), *params, v_first)
    return outs


RW_CHUNK = 128


def _wkv_kernel(r_ref, lw_ref, k_ref, v_ref, an_ref, b_ref, g_ref, lng_ref, lnb_ref, rk_ref,
                gsum_ref, tril_ref, o_ref, s_ref):
    c = pl.program_id(0)
    C = RW_CHUNK
    N = RW_N

    @pl.when(c == 0)
    def _():
        s_ref[...] = jnp.zeros_like(s_ref)

    r = r_ref[...]
    lw = lw_ref[...]
    k = k_ref[...]
    v = v_ref[...]
    lw_hi = lw.astype(BF16)
    rem = lw - lw_hi.astype(F32)
    lw_mid = rem.astype(BF16)
    lw_lo = (rem - lw_mid.astype(F32)).astype(BF16)
    tril = tril_ref[...]
    lg = _dot(tril, lw_hi) + (_dot(tril, lw_mid) + _dot(tril, lw_lo))
    gam = jnp.exp(lg)
    ginv = jnp.exp(-lg)
    at = (an_ref[...] * jnp.exp(lg - lw)).astype(BF16)
    bt = (b_ref[...] * ginv).astype(BF16)
    kt = (k * ginv).astype(BF16)
    rt_f = r * gam
    rt = rt_f.astype(BF16)
    vb = v.astype(BF16)
    g_last = gam[C - 1:C, :]

    rowi = lax.broadcasted_iota(jnp.int32, (C, C), 0)
    coli = lax.broadcasted_iota(jnp.int32, (C, C), 1)
    strict = rowi > coli
    incl = rowi >= coli

    heads = range(RW_HEADS)
    hs = [slice(h * N, (h + 1) * N) for h in heads]
    bth = [bt[:, s] for s in hs]
    kth = [kt[:, s] for s in hs]
    vh = [vb[:, s] for s in hs]
    big = [_dot_t(jnp.concatenate([at[:, s], rt[:, s]], axis=0), jnp.concatenate([bth[h], kth[h]], axis=0))
           for h, s in zip(heads, hs)]
    a_ab = [jnp.where(strict, big[h][:C, :C], 0.0) for h in heads]
    a_ak = [jnp.where(strict, big[h][:C, C:], 0.0).astype(BF16) for h in heads]
    a_rb = [jnp.where(incl, big[h][C:, :C], 0.0).astype(BF16) for h in heads]
    a_rk = [jnp.where(incl, big[h][C:, C:], 0.0).astype(BF16) for h in heads]
    akv = [_dot(a_ak[h], vh[h]) for h in heads]
    def same_block(size):
        shift = size.bit_length() - 1
        return (rowi >> shift) == (coli >> shift)

    tinv = [jnp.where(same_block(2), a_ab[h], 0.0) + jnp.where(rowi == coli, 1.0, 0.0) for h in heads]
    size = 2
    while size < C:
        lower_left = same_block(2 * size) & jnp.logical_not(same_block(size))
        off = [jnp.where(lower_left, a_ab[h], 0.0).astype(BF16) for h in heads]
        tb = [tinv[h].astype(BF16) for h in heads]
        half = [_dot(tb[h], off[h]).astype(BF16) for h in heads]
        tinv = [tinv[h] + _dot(half[h], tb[h]) for h in heads]
        size *= 2
    xb = [_dot(tinv[h].astype(BF16), jnp.concatenate([at[:, hs[h]], akv[h].astype(BF16)], axis=1)).astype(BF16)
          for h in heads]
    yx = [_dot(a_rb[h], xb[h]) for h in heads]
    ykv = [_dot(a_rk[h], vh[h]) for h in heads]
    xtb = [_dot_tl(xb[h], bth[h]) for h in heads]
    vtk = [_dot_tl(vh[h], kth[h]) for h in heads]
    ys = []
    for h in heads:
        gl = g_last[:, hs[h]]
        y1 = (rt_f[:, hs[h]] + yx[h][:, :N]).astype(BF16)
        s0 = s_ref[h]
        s0b = s0.astype(BF16)
        ys.append(_dot_t(y1, s0b) + (yx[h][:, N:] + ykv[h]))
        s_ref[h] = (s0 + _dot(s0b, xtb[h][:N].astype(BF16)) + (xtb[h][N:] + vtk[h])) * gl
    y = jnp.concatenate(ys, axis=1)

    gsum = gsum_ref[...]
    inv_n = 1.0 / N
    mu = _split_dot(y, gsum, 2) * inv_n
    yc = y - mu
    var = _split_dot(yc * yc, gsum, 2) * inv_n
    yn = yc * lax.rsqrt(var + RW_GN_EPS) * lng_ref[...] + lnb_ref[...]
    bonus = _split_dot(r * k * rk_ref[...], gsum, 2)
    o_ref[...] = ((yn + bonus * v) * g_ref[...]).astype(o_ref.dtype)


def _wkv(r, lw, k, v, an, b, g, ln_g, ln_b, r_k, gsum):
    t = r.shape[0]
    C = RW_CHUNK
    tril = jnp.asarray(np.tril(np.ones((C, C), np.float32)), BF16)
    row = pl.BlockSpec((C, RW_W), lambda c: (c, 0))
    vec = pl.BlockSpec((1, RW_W), lambda c: (0, 0))
    return pl.pallas_call(
        _wkv_kernel,
        out_shape=jax.ShapeDtypeStruct((t, RW_W), BF16),
        grid=(t // C,),
        in_specs=[row] * 7 + [vec] * 3 + [pl.BlockSpec((RW_W, RW_W), lambda c: (0, 0)),
                                          pl.BlockSpec((C, C), lambda c: (0, 0))],
        out_specs=row,
        scratch_shapes=[pltpu.VMEM((RW_HEADS, RW_N, RW_N), F32)],
        compiler_params=_cparams(("arbitrary",)),
        name="wkv7",
    )(r, lw, k, v, an, b, g, ln_g, ln_b, r_k, gsum, tril)


NSA_PREP_TM = 512


N_NORM_SLABS = (NSA_W + 2 * NSA_KV_W) // LANES
N_RAW_SLABS = 4 * NSA_KV_W // LANES
N_NSA_SLABS = N_NORM_SLABS + N_RAW_SLABS + 1


def _nsa_prep_kernel(*refs):
    x_refs = refs[:N_NSA_SLABS]
    (cos_ref, sin_ref, w_ref, gmean_ref, big_ref,
     qk_ref, hmat_ref, v1_ref, gate_ref, qpad_ref, ksx_ref) = refs[N_NSA_SLABS:]
    cos = cos_ref[...]
    sin = sin_ref[...]
    gmean = gmean_ref[...]
    half = NSA_DH // 2
    tm = cos.shape[0]
    lane = lax.broadcasted_iota(jnp.int32, cos.shape, 1)
    first_half = (lane % NSA_DH) < half
    low = lane < NSA_DH
    blk_row = (pl.program_id(0) * tm + lax.broadcasted_iota(jnp.int32, cos.shape, 0)) // SLC_BLOCK
    big = big_ref[...]
    mark_lo = jnp.where(blk_row == lane - NSA_DH, big, 0.0)
    mark_hi = jnp.where(low & (blk_row == lane + NSA_DH), big, 0.0).astype(BF16)
    n_q = NSA_W // LANES
    n_k = NSA_KV_W // LANES
    for s in range(N_NORM_SLABS):
        x = x_refs[s][...]
        y = x * lax.rsqrt(_split_dot(x * x, gmean, 2) + RMS_EPS) * w_ref[s]
        rot = jnp.where(first_half, pltpu.roll(y, LANES - half, axis=1), pltpu.roll(y, half, axis=1))
        res32 = y * cos + rot * sin
        res = res32.astype(qk_ref.dtype)
        qk_ref[2 * s] = res[:, :NSA_DH]
        qk_ref[2 * s + 1] = res[:, NSA_DH:]
        if s >= n_q + n_k:
            continue
        swapped = pltpu.roll(res32, NSA_DH, axis=1)
        if s < n_q:
            qpad_ref[2 * s] = jnp.where(low, res32, 0.0).astype(BF16)
            qpad_ref[2 * s + 1] = jnp.where(low, swapped, 0.0).astype(BF16)
        else:
            g0 = 2 * (s - n_q)
            ksx_ref[g0, :, :LANES] = jnp.where(low, res32, mark_lo).astype(BF16)
            ksx_ref[g0 + 1, :, :LANES] = jnp.where(low, swapped, mark_lo).astype(BF16)
            ksx_ref[g0, :, LANES:] = mark_hi
            ksx_ref[g0 + 1, :, LANES:] = mark_hi

    rows = hmat_ref.shape[2]
    low_r = lax.broadcasted_iota(jnp.int32, (rows, LANES), 1) < NSA_DH
    for s in range(NSA_KV_W // LANES * 2):
        kind, pair = divmod(s, NSA_KV_W // LANES)
        x_ref = x_refs[N_NORM_SLABS + s]
        for t2 in range(CMP_STRIDE // 2):
            r0 = x_ref[pl.ds(2 * t2, rows, stride=CMP_STRIDE), :]
            r1 = x_ref[pl.ds(2 * t2 + 1, rows, stride=CMP_STRIDE), :]
            cols = slice(t2 * LANES, (t2 + 1) * LANES)
            hmat_ref[kind, 2 * pair, :, cols] = jnp.where(low_r, r0, pltpu.roll(r1, NSA_DH, axis=1)).astype(BF16)
            hmat_ref[kind, 2 * pair + 1, :, cols] = jnp.where(low_r, pltpu.roll(r0, NSA_DH, axis=1), r1).astype(BF16)

    one_col = jnp.where(lane == NSA_DH, 1.0, 0.0)
    for s in range(NSA_KV_W // LANES * 2):
        x = x_refs[N_NORM_SLABS + N_RAW_SLABS // 2 + s][...]
        v1_ref[2 * s] = jnp.where(low, x, one_col).astype(BF16)
        v1_ref[2 * s + 1] = jnp.where(low, pltpu.roll(x, NSA_DH, axis=1), one_col).astype(BF16)

    xg = x_refs[N_NSA_SLABS - 1][...]
    per_group = 3 * NSA_HG
    for g in range(NSA_G):
        shifted = xg if g == 0 else pltpu.roll(xg, LANES - per_group * g, axis=1)
        gate_ref[g] = jnp.where(lane < per_group, shifted, 0.0)


def _nsa_prep(proj, w_slabs, cos, sin, gmean, big):
    t = proj.shape[0]
    assert t // SLC_BLOCK <= LANES
    tm = NSA_PREP_TM
    base = C_NSA_Q // LANES
    tab = pl.BlockSpec((tm, LANES), lambda i: (i, 0))
    n_heads_out = 2 * N_NORM_SLABS
    return pl.pallas_call(
        _nsa_prep_kernel,
        out_shape=[jax.ShapeDtypeStruct((n_heads_out, t, NSA_DH), BF16),
                   jax.ShapeDtypeStruct((2, NSA_G, t // CMP_STRIDE, CMP_STRIDE * NSA_DH), BF16),
                   jax.ShapeDtypeStruct((2 * NSA_G, t, LANES), BF16),
                   jax.ShapeDtypeStruct((NSA_G, t, LANES), F32),
                   jax.ShapeDtypeStruct((NSA_HEADS, t, LANES), BF16),
                   jax.ShapeDtypeStruct((NSA_G, t, 2 * LANES), BF16)],
        grid=(t // tm,),
        in_specs=[pl.BlockSpec((tm, LANES), lambda i, s=s: (i, base + s)) for s in range(N_NSA_SLABS)]
        + [tab, tab, pl.BlockSpec((N_NORM_SLABS, 1, LANES), lambda i: (0, 0, 0)),
           pl.BlockSpec((LANES, LANES), lambda i: (0, 0)), pl.BlockSpec((1, LANES), lambda i: (0, 0))],
        out_specs=[pl.BlockSpec((n_heads_out, tm, NSA_DH), lambda i: (0, i, 0)),
                   pl.BlockSpec((2, NSA_G, tm // CMP_STRIDE, CMP_STRIDE * NSA_DH), lambda i: (0, 0, i, 0)),
                   pl.BlockSpec((2 * NSA_G, tm, LANES), lambda i: (0, i, 0)),
                   pl.BlockSpec((NSA_G, tm, LANES), lambda i: (0, i, 0)),
                   pl.BlockSpec((NSA_HEADS, tm, LANES), lambda i: (0, i, 0)),
                   pl.BlockSpec((NSA_G, tm, 2 * LANES), lambda i: (0, i, 0))],
        compiler_params=_cparams(("parallel",)),
        name="nsa_prep",
    )(*([proj] * N_NSA_SLABS), cos, sin, w_slabs, gmean, big)


def _compress_kernel(h_ref, w1_ref, w2_ref, pe_ref, nw_ref, cos_ref, sin_ref, o_ref):
    kind = pl.program_id(0)
    nc = h_ref.shape[2]
    half_in = CMP_STRIDE * NSA_DH
    hm = h_ref[0, 0]
    w1 = w1_ref[0]
    first = _dot(hm, w1[:half_in])
    second = _dot(hm, w1[half_in:])
    const = _dot(pe_ref[0], w1)[0:1, :]
    pre = first + pltpu.roll(second, nc - 1, axis=0) + const
    hid = 0.5 * pre * (1.0 + jnp.tanh(0.7978845608028654 * (pre + 0.044715 * pre * pre * pre)))
    out = _dot(hid.astype(BF16), w2_ref[0])
    y = out * lax.rsqrt(jnp.mean(out * out, axis=-1, keepdims=True) + RMS_EPS) * nw_ref[...]
    half = NSA_DH // 2
    rot = jnp.concatenate([y[:, half:], y[:, :half]], axis=1)
    roped = y * cos_ref[...] + rot * sin_ref[...]
    o_ref[0, 0] = jnp.where(kind == 0, roped, out).astype(o_ref.dtype)


def _compress(hmat, w1, w2, pe, nw, cos_c, sin_c):
    _, g, nc, width = hmat.shape
    return pl.pallas_call(
        _compress_kernel,
        out_shape=jax.ShapeDtypeStruct((2, g, nc, NSA_DH), BF16),
        grid=(2, g),
        in_specs=[pl.BlockSpec((1, 1, nc, width), lambda a, b: (a, b, 0, 0)),
                  pl.BlockSpec((1, 2 * width, CMP_HIDDEN), lambda a, b: (a, 0, 0)),
                  pl.BlockSpec((1, CMP_HIDDEN, NSA_DH), lambda a, b: (a, 0, 0)),
                  pl.BlockSpec((1, 8, 2 * width), lambda a, b: (a, 0, 0)),
                  pl.BlockSpec((1, NSA_DH), lambda a, b: (0, 0)),
                  pl.BlockSpec((nc, NSA_DH), lambda a, b: (0, 0)),
                  pl.BlockSpec((nc, NSA_DH), lambda a, b: (0, 0))],
        out_specs=pl.BlockSpec((1, 1, nc, NSA_DH), lambda a, b: (a, b, 0, 0)),
        compiler_params=_cparams(("parallel", "parallel")),
        name="nsa_compress",
    )(hmat, w1, w2, pe, nw, cos_c, sin_c)


SLC_KT = 1024
WIN_ROWS = 128
WIN_KEYS = WINDOW + WIN_ROWS


INT32_MIN = -2 ** 31


def _nsa_attn_kernel(q_ref, qpad_ref, gate_ref, kc_ref, vc_ref, ks_ref, vs_ref, kw_ref, vw_ref, cis_ref, ltri_ref,
                     o_ref):
    qi = pl.program_id(1)
    nc = kc_ref.shape[2]
    nb = cis_ref.shape[1]
    heads = range(NSA_HG)
    start = qi * NSA_TQ
    qh = [q_ref[hh] for hh in heads]
    tok = start + lax.broadcasted_iota(jnp.int32, (NSA_TQ, 1), 0)

    def softmax_terms(scores):
        return [jnp.exp(s - jnp.max(s, axis=-1, keepdims=True)) for s in scores]

    def cmp_branch(width):
        kc = kc_ref[0, 0, :width, :]
        vc = vc_ref[0, 0, :width, :]
        cend = lax.broadcasted_iota(jnp.int32, (1, width), 1) * CMP_STRIDE + (CMP_BLOCK - 1)
        bias_c = jnp.where(cend <= tok, 0.0, NEG_INF)
        ec = softmax_terms([_dot_t(qh[hh], kc) + bias_c for hh in heads])
        sees_any = tok >= CMP_BLOCK - 1
        pc = [ec[hh] * jnp.where(sees_any, 1.0 / jnp.sum(ec[hh], axis=-1, keepdims=True), 0.0) for hh in heads]
        outs_c = tuple(_dot(pc[hh].astype(BF16), vc) for hh in heads)
        psum = pc[0]
        for hh in range(1, NSA_HG):
            psum = psum + pc[hh]
        return outs_c, _split_dot(psum, cis_ref[:width, :], 2)

    n_quarters = 4
    quarter = nc // n_quarters
    n_visible = (start + NSA_TQ - CMP_BLOCK) // CMP_STRIDE + 1
    which = jnp.clip((n_visible - 1) // quarter, 0, n_quarters - 1)
    o_c, score = lax.switch(which, [functools.partial(cmp_branch, (r + 1) * quarter) for r in range(n_quarters)])
    blk = lax.broadcasted_iota(jnp.int32, (1, nb), 1)
    cur = tok // SLC_BLOCK
    forced = (blk == 0) | (blk == cur) | (blk == cur - 1)
    score = jnp.where(forced, FORCED_SCORE, score)
    score = jnp.where(blk <= cur, score, -jnp.inf)


    bits = lax.bitcast_convert_type(score.T, jnp.int32)
    key_all = bits ^ ((bits >> 31) & 0x7FFFFFFF)

    def select_blocks(nbv):
        key = key_all[:nbv]
        n_sel = min(SLC_TOPK, nbv)

        def enough(c):
            return jnp.sum(jnp.where(key >= c, 1.0, 0.0), axis=0, keepdims=True) >= n_sel

        def enough3(c1, c2, c3):
            packed = jnp.where(key >= c3, 65793.0, jnp.where(key >= c2, 257.0, jnp.where(key >= c1, 1.0, 0.0)))
            tot = jnp.sum(packed, axis=0, keepdims=True).astype(jnp.int32)
            return (tot & 255) >= n_sel, ((tot >> 8) & 255) >= n_sel, (tot >> 16) >= n_sel

        zero_row = jnp.zeros((1, NSA_TQ), jnp.int32)
        thr = jnp.where(enough(zero_row), zero_row, INT32_MIN)
        for hi in range(30, 0, -2):
            c1 = thr + (1 << (hi - 1))
            c2 = thr + (1 << hi)
            c3 = c2 + (1 << (hi - 1))
            e1, e2, e3 = enough3(c1, c2, c3)
            thr = jnp.where(e3, c3, jnp.where(e2, c2, jnp.where(e1, c1, thr)))
        c1 = thr + 1
        thr = jnp.where(enough(c1), c1, thr)
        above = key > thr
        tied = key == thr
        need = n_sel - jnp.sum(jnp.where(above, 1.0, 0.0), axis=0, keepdims=True)
        tied_before = _dot(ltri_ref[:nbv, :nbv], jnp.where(tied, 1.0, 0.0).astype(BF16))
        chosen = jnp.where(above | (tied & (tied_before < need)), 1.0, 0.0)
        if nbv < LANES:
            chosen = jnp.concatenate([chosen, jnp.zeros((LANES - nbv, NSA_TQ), F32)], axis=0)
        return chosen.T

    assert nb <= LANES
    n_steps = 4
    blocks_visible = (start + NSA_TQ - 1) // SLC_BLOCK + 1
    widths = sorted({max(-(-nb * (r + 1) // n_steps // 8) * 8, min(nb, SLC_TOPK)) for r in range(n_steps)})
    which_sel = sum((blocks_visible > w).astype(jnp.int32) for w in widths[:-1])
    sel = lax.switch(which_sel, [functools.partial(select_blocks, w) for w in widths])
    lane_q = lax.broadcasted_iota(jnp.int32, (NSA_TQ, LANES), 1)
    sel_sw = pltpu.roll(sel, NSA_DH, axis=1)
    sel_lo = jnp.where(lane_q >= NSA_DH, sel_sw, 0.0)
    sel_hi = jnp.where(lane_q < NSA_DH, sel_sw, 0.0)
    qx = [jnp.concatenate([qpad_ref[hh].astype(F32) + sel_lo, sel_hi], axis=1).astype(BF16) for hh in heads]

    def window_rows(r0):
        kbase = pl.multiple_of(jnp.maximum(start + r0 - WINDOW, 0), WIN_ROWS)
        kw = kw_ref[0, pl.ds(kbase, WIN_KEYS), :]
        vw = vw_ref[0, pl.ds(kbase, WIN_KEYS), :]
        lag = tok[r0:r0 + WIN_ROWS] - (kbase + lax.broadcasted_iota(jnp.int32, (1, WIN_KEYS), 1))
        bias_w = jnp.where((lag >= 0) & (lag < WINDOW), 0.0, NEG_INF)
        sw = [_dot_t(qh[hh][r0:r0 + WIN_ROWS], kw) + bias_w for hh in heads]
        ew = [jnp.exp((s - jnp.max(s, axis=-1, keepdims=True)).astype(BF16)) for s in sw]
        return [_dot(ew[hh], vw) for hh in heads]

    win_parts = [window_rows(r0) for r0 in range(0, NSA_TQ, WIN_ROWS)]
    acc_w = [jnp.concatenate([part[hh] for part in win_parts], axis=0) for hh in heads]

    n_tiles = (start + NSA_TQ + SLC_KT - 1) // SLC_KT

    def sel_tile(j, carry, causal, width=SLC_KT):
        ms, accs = carry
        k0 = pl.multiple_of(j * SLC_KT, SLC_KT)
        kt = ks_ref[0, pl.ds(k0, width), :]
        vt = vs_ref[0, pl.ds(k0, width), :]
        sj = [_dot_t(qx[hh], kt) for hh in heads]
        if causal:
            future = (k0 + lax.broadcasted_iota(jnp.int32, (1, width), 1)) > tok
            sj = [jnp.where(future, NEG_INF, s) for s in sj]
        m_new = [jnp.maximum(ms[hh], jnp.max(sj[hh], axis=-1, keepdims=True)) for hh in heads]
        pj = [jnp.exp((sj[hh] - m_new[hh]).astype(BF16)) for hh in heads]
        accs = [jnp.exp(ms[hh] - m_new[hh]) * accs[hh] + _dot(pj[hh], vt) for hh in heads]
        return tuple(m_new), tuple(accs)

    m0 = tuple(jnp.full((NSA_TQ, 1), NEG_INF, F32) for _ in heads)
    a0 = tuple(jnp.zeros((NSA_TQ, LANES), F32) for _ in heads)
    carry = lax.fori_loop(0, n_tiles - 1, functools.partial(sel_tile, causal=False), (m0, a0))
    per_tile = SLC_KT // NSA_TQ
    last = [functools.partial(sel_tile, n_tiles - 1, causal=True, width=(r + 1) * NSA_TQ) for r in range(per_tile)]
    _, acc_s = lax.switch(qi % per_tile, last, carry)

    gt = jax.nn.sigmoid(gate_ref[0])
    outs = []
    for hh in heads:
        g0, g1, g2 = (gt[:, 3 * hh + br:3 * hh + br + 1] for br in range(3))
        scale_s = g1 / acc_s[hh][:, NSA_DH:NSA_DH + 1]
        scale_w = g2 / acc_w[hh][:, NSA_DH:NSA_DH + 1]
        outs.append(g0 * o_c[hh] + scale_s * acc_s[hh][:, :NSA_DH] + scale_w * acc_w[hh][:, :NSA_DH])
    o_ref[...] = jnp.concatenate(outs, axis=1).astype(o_ref.dtype)


def _nsa_attn(qk_hm, qpad, ksx, gates, cmp_kv, v1, cis, ltri):
    t = qk_hm.shape[1]
    nq = t // NSA_TQ
    nc = cmp_kv.shape[2]
    nb = cis.shape[1]
    ks_spec = pl.BlockSpec((1, t, 2 * LANES), lambda g, i: (g, 0, 0))
    kw_spec = pl.BlockSpec((1, t, NSA_DH), lambda g, i: (NSA_HEADS + NSA_G + g, 0, 0))
    vs_spec = pl.BlockSpec((1, t, LANES), lambda g, i: (g, 0, 0))
    vw_spec = pl.BlockSpec((1, t, LANES), lambda g, i: (NSA_G + g, 0, 0))
    return pl.pallas_call(
        _nsa_attn_kernel,
        out_shape=jax.ShapeDtypeStruct((t, NSA_W), BF16),
        grid=(NSA_G, nq),
        in_specs=[pl.BlockSpec((NSA_HG, NSA_TQ, NSA_DH), lambda g, i: (g, i, 0)),
                  pl.BlockSpec((NSA_HG, NSA_TQ, LANES), lambda g, i: (g, i, 0)),
                  pl.BlockSpec((1, NSA_TQ, LANES), lambda g, i: (g, i, 0)),
                  pl.BlockSpec((1, 1, nc, NSA_DH), lambda g, i: (0, g, 0, 0)),
                  pl.BlockSpec((1, 1, nc, NSA_DH), lambda g, i: (1, g, 0, 0)),
                  ks_spec, vs_spec, kw_spec, vw_spec,
                  pl.BlockSpec((nc, nb), lambda g, i: (0, 0)),
                  pl.BlockSpec((nb, nb), lambda g, i: (0, 0))],
        out_specs=pl.BlockSpec((NSA_TQ, NSA_HG * NSA_DH), lambda g, i: (i, g)),
        compiler_params=_cparams(("parallel", "arbitrary")),
        name="nsa_attention",
    )(qk_hm, qpad, gates, cmp_kv, cmp_kv, ksx, v1, qk_hm, v1, cis, ltri)


def _rope_tables(pos, dh, reps):
    half = dh // 2
    inv_freq = ROPE_THETA ** (-np.arange(half, dtype=np.float64) / half)
    ang = np.asarray(pos, np.float64)[:, None] * inv_freq[None, :]
    cos = np.cos(ang)
    sin = np.sin(ang)
    cos_t = np.tile(np.concatenate([cos, cos], axis=1), (1, reps))
    sin_t = np.tile(np.concatenate([-sin, sin], axis=1), (1, reps))
    return jnp.asarray(cos_t, F32), jnp.asarray(sin_t, F32)


def _pad_cols(a, width):
    return jnp.pad(a, ((0, 0), (0, width - a.shape[1])))


def _pad_rows(a, height):
    return jnp.pad(a, ((0, height - a.shape[0]), (0, 0)))


def _pack_w_in_kernel(w_ref, xv_ref, o_ref):
    o_ref[...] = _pack_w_in(w_ref[...], xv_ref[...]).astype(o_ref.dtype)


def _pack_w_in_call(w_in, layer, xv):
    _, d, n_in = w_in.shape
    tk = 256
    return pl.pallas_call(
        _pack_w_in_kernel,
        out_shape=jax.ShapeDtypeStruct((d, PROJ_W), BF16),
        grid=(d // tk,),
        in_specs=[pl.BlockSpec((None, tk, n_in), lambda i: (layer, i, 0)),
                  pl.BlockSpec((tk, LANES), lambda i: (i, 0))],
        out_specs=pl.BlockSpec((tk, PROJ_W), lambda i: (i, 0)),
        compiler_params=_cparams(("parallel",)),
        name="pack_w_in",
    )(w_in, xv)


def _pack_w_in(w_l, xv):
    d = w_l.shape[0]
    o = 0
    ret = w_l[:, o:o + 4 * RET_W]; o += 4 * RET_W
    rkv = w_l[:, o:o + 3 * RW_W]; o += 3 * RW_W
    xw = w_l[:, o:o + RW_DECAY_RANK]; o += RW_DECAY_RANK
    xa = w_l[:, o:o + RW_A_RANK]; o += RW_A_RANK
    xg = w_l[:, o:o + RW_GATE_RANK]; o += RW_GATE_RANK
    q = w_l[:, o:o + NSA_W]; o += NSA_W
    kc, vc, ks, vs, kw, vw = (w_l[:, o + i * NSA_KV_W:o + (i + 1) * NSA_KV_W] for i in range(6))
    o += 6 * NSA_KV_W
    gates = w_l[:, o:o + 3 * NSA_HEADS]
    return jnp.concatenate([ret, rkv, _pad_cols(xw, LANES), _pad_cols(xa, LANES), xg, xv,
                            q, ks, kw, kc, vc, vs, vw, _pad_cols(gates, LANES)], axis=1)


def _pack_mix(mix, vres_mix):
    o = 3 * RW_W
    xw = mix[o:o + RW_DECAY_RANK]; o += RW_DECAY_RANK
    xa = mix[o:o + RW_A_RANK]; o += RW_A_RANK
    xg = mix[o:o + RW_GATE_RANK]
    z = lambda n: jnp.zeros((n,), mix.dtype)
    xv = z(LANES) if vres_mix is None else jnp.concatenate([vres_mix, z(LANES - RW_V_RANK)])
    return jnp.concatenate([mix[:3 * RW_W], xw, z(LANES - RW_DECAY_RANK), xa, z(LANES - RW_A_RANK), xg, xv])[None, :]


def kernel(x, ln1_g, w_in, w_in_vres, rwkv_mix, rwkv_vres_mix, rwkv_w0, rwkv_w2, rwkv_a0, rwkv_a2, rwkv_v0, rwkv_v2, rwkv_g2, rwkv_k_k, rwkv_k_a, rwkv_r_k, rwkv_ln_g, rwkv_ln_b, nsa_q_norm, nsa_k_norm, nsa_cmp_pe, nsa_cmp_k_w1, nsa_cmp_k_w2, nsa_cmp_v_w1, nsa_cmp_v_w2, w_out, ln2_g, w_up, w_down):
    bsz, t, d = x.shape
    assert bsz == 1
    depth = w_in.shape[0]
    nc = t // CMP_STRIDE
    nb = t // SLC_BLOCK
    xs = x.reshape(t, d)

    pos = np.arange(t)
    cos_r, sin_r = _rope_tables(pos, RET_DH, 1)
    cos_n, sin_n = _rope_tables(pos, NSA_DH, 2)
    cos_c, sin_c = _rope_tables(np.arange(nc) * CMP_STRIDE + (CMP_BLOCK - 1), NSA_DH, 1)
    lane_head = np.arange(RW_W) // RW_N
    gsum = jnp.asarray(lane_head[:, None] == lane_head[None, :], BF16)
    lane_h2 = np.arange(LANES) // NSA_DH
    gmean = jnp.asarray((lane_h2[:, None] == lane_h2[None, :]) / float(NSA_DH), BF16)
    cstart = np.arange(nc) * CMP_STRIDE
    sstart = np.arange(nb) * SLC_BLOCK
    cis = jnp.asarray((cstart[:, None] <= sstart[None, :] + SLC_BLOCK - 1)
                      & (cstart[:, None] + CMP_BLOCK - 1 >= sstart[None, :]), BF16)

    ltri = jnp.asarray(np.tril(np.ones((nb, nb), np.float32), -1), BF16)

    v_first = jnp.zeros((t, RW_W), F32)
    for l in range(depth):
        vres = l > 0
        xv_w = _pad_cols(w_in_vres[l - 1], LANES) if vres else jnp.zeros((d, LANES), F32)
        w_cat = _pack_w_in_call(w_in, l, xv_w)
        proj = _norm_matmul(xs, ln1_g[l], w_cat, name="in_proj")

        o_ret = _retention(proj, cos_r, sin_r)

        row = lambda a: a.reshape(1, -1)
        mix = _pack_mix(rwkv_mix[l], rwkv_vres_mix[l - 1] if vres else None)
        v0 = row(rwkv_v0[l - 1]) if vres else jnp.zeros((1, RW_W), F32)
        v2 = _pad_rows(rwkv_v2[l - 1], LANES) if vres else jnp.zeros((LANES, RW_W), F32)
        r_, lw_, k_, v_, an_, b_, g_ = _rwkv_pre(
            proj, mix, row(rwkv_w0[l]), _pad_rows(rwkv_w2[l], LANES), row(rwkv_a0[l]), _pad_rows(rwkv_a2[l], LANES),
            v0, v2, rwkv_g2[l], row(rwkv_k_k[l]), row(rwkv_k_a[l]), gsum, v_first, vres)
        if not vres:
            v_first = v_
        o_rwkv = _wkv(r_, lw_, k_, v_, an_, b_, g_, row(rwkv_ln_g[l]), row(rwkv_ln_b[l]), row(rwkv_r_k[l]), gsum)

        n_q = NSA_W // LANES
        n_k = NSA_KV_W // LANES
        w_slabs = jnp.stack([jnp.tile(nsa_q_norm[l] * (NSA_DH ** -0.5), 2)] * n_q
                            + [jnp.tile(nsa_k_norm[l, 1], 2)] * n_k + [jnp.tile(nsa_k_norm[l, 2], 2)] * n_k)[:, None, :]
        s_max = 1.02 * jnp.max(jnp.abs(nsa_q_norm[l])) * jnp.max(jnp.abs(nsa_k_norm[l, 1])) * (NSA_DH ** 0.5)
        big = jnp.full((1, LANES), jnp.exp2(jnp.ceil(jnp.log2(2.0 * s_max + 128.0))), F32)
        qk_hm, hmat, v1, gates, qpad, ksx = _nsa_prep(proj, w_slabs, cos_n, sin_n, gmean, big)
        w1 = jnp.stack([nsa_cmp_k_w1[l], nsa_cmp_v_w1[l]]).astype(BF16)
        w2 = jnp.stack([nsa_cmp_k_w2[l], nsa_cmp_v_w2[l]]).astype(BF16)
        pe = jnp.broadcast_to(nsa_cmp_pe[l].reshape(2, 1, CMP_BLOCK * NSA_DH), (2, 8, CMP_BLOCK * NSA_DH)).astype(BF16)
        cmp_kv = _compress(hmat, w1, w2, pe, row(nsa_k_norm[l, 0]), cos_c, sin_c)
        o_nsa = _nsa_attn(qk_hm, qpad, ksx, gates, cmp_kv, v1, cis, ltri)

        xs = _out_proj(o_ret, o_rwkv, o_nsa, w_out, l, xs)

        up = _norm_matmul(xs, ln2_g[l], w_up, layer=l, act="relu2", out_dtype=BF16, name="mlp_up")
        xs = _matmul(up, w_down, layer=l, res=xs, name="mlp_down")
    return xs.reshape(bsz, t, d)
```

```python
import functools

import numpy as np
import jax
import jax.numpy as jnp
from jax import lax
from jax.experimental import pallas as pl
from jax.experimental.pallas import tpu as pltpu

F32 = jnp.float32
BF16 = jnp.bfloat16

D_MODEL = 2048
RET_HEADS, RET_DH = 4, 128
RET_W = RET_HEADS * RET_DH
RW_HEADS, RW_N = 8, 64
RW_W = RW_HEADS * RW_N
RW_DECAY_RANK, RW_A_RANK, RW_V_RANK, RW_GATE_RANK = 96, 96, 64, 256
RW_GN_EPS = 64e-5
NSA_HEADS, NSA_G, NSA_DH = 16, 4, 64
NSA_HG = NSA_HEADS // NSA_G
NSA_W = NSA_HEADS * NSA_DH
NSA_KV_W = NSA_G * NSA_DH
CMP_BLOCK, CMP_STRIDE, CMP_HIDDEN = 32, 16, 256
SLC_BLOCK, SLC_TOPK, WINDOW = 64, 16, 512
NSA_TQ = 512
ROPE_THETA = 10000.0
MLP_HIDDEN = 4 * D_MODEL
RMS_EPS = 1e-6
NEG_INF = -1e30
FORCED_SCORE = 1e9

LANES = 128
VMEM_LIMIT = 56 * 1024 * 1024

C_RET = 0
C_RW_RKV = 2048
C_RW_XW = 3584
C_RW_XA = 3712
C_RW_XG = 3840
C_RW_XV = 4096
C_NSA_Q = 4224
C_NSA_KS = 5248
C_NSA_KW = 5504
C_NSA_KC = 5760
C_NSA_VC = 6016
C_NSA_VS = 6272
C_NSA_VW = 6528
C_NSA_GATE = 6784
PROJ_W = 6912


def _cparams(sem):
    return pltpu.CompilerParams(dimension_semantics=sem, vmem_limit_bytes=VMEM_LIMIT)


def _dot(a, b):
    return jnp.dot(a, b, preferred_element_type=F32)


def _dot_t(a, b):
    return lax.dot_general(a, b, (((1,), (1,)), ((), ())), preferred_element_type=F32)


def _dot_tl(a, b):
    return lax.dot_general(a, b, (((0,), (0,)), ((), ())), preferred_element_type=F32)


def _split_dot(x, m_bf16, passes):
    hi = x.astype(BF16)
    acc = _dot(hi, m_bf16)
    rem = x - hi.astype(F32)
    for _ in range(passes - 1):
        piece = rem.astype(BF16)
        acc = acc + _dot(piece, m_bf16)
        rem = rem - piece.astype(F32)
    return acc


def _mm_kernel(*refs, nk, act, has_res):
    a_ref, b_ref = refs[:2]
    r_ref = refs[2] if has_res else None
    o_ref = refs[3] if has_res else refs[2]

    def finish(y):
        if act == "relu2":
            y = jnp.square(jnp.maximum(y, 0.0))
        if has_res:
            y = y + r_ref[...]
        o_ref[...] = y.astype(o_ref.dtype)

    part = _dot(a_ref[...], b_ref[...].astype(BF16))
    if nk == 1:
        finish(part)
        return
    acc_ref = refs[-1]
    k = pl.program_id(2)

    @pl.when(k == 0)
    def _():
        acc_ref[...] = part

    @pl.when(k > 0)
    def _():
        acc_ref[...] += part

    @pl.when(k == nk - 1)
    def _():
        finish(acc_ref[...])


def _pick(n, cands):
    for c in cands:
        if n % c == 0:
            return c
    raise ValueError(f"no tile for {n}")


def _weight_spec(b, layer, block, index_map):
    if layer is None:
        return pl.BlockSpec(block, index_map)
    return pl.BlockSpec((None,) + block, lambda *idx: (layer,) + index_map(*idx))


def _matmul(a, b, *, layer=None, res=None, act=None, out_dtype=F32, name="matmul"):
    m, kd = a.shape
    n = b.shape[-1]
    tm = _pick(m, (1024, 512, 256, 128))
    tn = _pick(n, (1024, 768, 512, 256, 128))
    tk = _pick(kd, (2048, 1024, 512))
    nk = kd // tk
    in_specs = [pl.BlockSpec((tm, tk), lambda i, j, k: (i, k)),
                _weight_spec(b, layer, (tk, tn), lambda i, j, k: (k, j))]
    args = [a, b]
    if res is not None:
        in_specs.append(pl.BlockSpec((tm, tn), lambda i, j, k: (i, j)))
        args.append(res)
    return pl.pallas_call(
        functools.partial(_mm_kernel, nk=nk, act=act, has_res=res is not None),
        out_shape=jax.ShapeDtypeStruct((m, n), out_dtype),
        grid=(m // tm, n // tn, nk),
        in_specs=in_specs,
        out_specs=pl.BlockSpec((tm, tn), lambda i, j, k: (i, j)),
        scratch_shapes=[pltpu.VMEM((tm, tn), F32)] if nk > 1 else [],
        compiler_params=_cparams(("parallel", "parallel", "arbitrary")),
        name=name,
    )(*args)


def _norm_mm_kernel(x_ref, g_ref, b_ref, o_ref, h_ref, *, act):
    @pl.when(pl.program_id(1) == 0)
    def _():
        x = x_ref[...]
        ms = jnp.mean(x * x, axis=-1, keepdims=True)
        h_ref[...] = (x * lax.rsqrt(ms + RMS_EPS) * g_ref[...]).astype(BF16)

    y = _dot(h_ref[...], b_ref[...].astype(BF16))
    if act == "relu2":
        y = jnp.square(jnp.maximum(y, 0.0))
    o_ref[...] = y.astype(o_ref.dtype)


def _norm_matmul(x, g, b, *, layer=None, act=None, out_dtype=F32, name="norm_matmul"):
    m, kd = x.shape
    n = b.shape[-1]
    tm = _pick(m, (1024, 512, 256, 128))
    tn = _pick(n, (1024, 768, 512, 256, 128))
    return pl.pallas_call(
        functools.partial(_norm_mm_kernel, act=act),
        out_shape=jax.ShapeDtypeStruct((m, n), out_dtype),
        grid=(m // tm, n // tn),
        in_specs=[pl.BlockSpec((tm, kd), lambda i, j: (i, 0)),
                  pl.BlockSpec((1, kd), lambda i, j: (0, 0)),
                  _weight_spec(b, layer, (kd, tn), lambda i, j: (0, j))],
        out_specs=pl.BlockSpec((tm, tn), lambda i, j: (i, j)),
        scratch_shapes=[pltpu.VMEM((tm, kd), BF16)],
        compiler_params=_cparams(("parallel", "arbitrary")),
        name=name,
    )(x, g.reshape(1, kd), b)


def _out_proj_kernel(ret_ref, rw_ref, nsa_ref, w_ref, res_ref, o_ref, wb_ref):
    @pl.when(pl.program_id(1) == 0)
    def _():
        wb_ref[...] = w_ref[...].astype(BF16)

    k0 = ret_ref.shape[1]
    k1 = k0 + rw_ref.shape[1]
    acc = _dot(ret_ref[...], wb_ref[:k0, :])
    acc = acc + _dot(rw_ref[...], wb_ref[k0:k1, :])
    acc = acc + _dot(nsa_ref[...], wb_ref[k1:, :])
    o_ref[...] = acc + res_ref[...]


def _out_proj(o_ret, o_rwkv, o_nsa, w, layer, res):
    m = o_ret.shape[0]
    kd, n = w.shape[-2:]
    assert kd == o_ret.shape[1] + o_rwkv.shape[1] + o_nsa.shape[1]
    tm = _pick(m, (1024, 512, 256, 128))
    tn = _pick(n, (1024, 512, 256, 128))

    def rows(a):
        return pl.BlockSpec((tm, a.shape[1]), lambda j, i: (i, 0))

    return pl.pallas_call(
        _out_proj_kernel,
        out_shape=jax.ShapeDtypeStruct((m, n), F32),
        grid=(n // tn, m // tm),
        in_specs=[rows(o_ret), rows(o_rwkv), rows(o_nsa),
                  _weight_spec(w, layer, (kd, tn), lambda j, i: (0, j)),
                  pl.BlockSpec((tm, tn), lambda j, i: (i, j))],
        out_specs=pl.BlockSpec((tm, tn), lambda j, i: (i, j)),
        scratch_shapes=[pltpu.VMEM((kd, tn), BF16)],
        compiler_params=_cparams(("parallel", "arbitrary")),
        name="out_proj",
    )(o_ret, o_rwkv, o_nsa, w, res)


RET_CHUNK = 512


def _retention_kernel(q_ref, k_ref, v_ref, g_ref, cos_ref, sin_ref, dmat_ref, qd_ref, kd_ref, cd_ref, o_ref, state_ref):
    @pl.when(pl.program_id(0) == 0)
    def _():
        state_ref[...] = jnp.zeros_like(state_ref)

    cos = cos_ref[...]
    sin = sin_ref[...]

    def rope(x):
        return x * cos + pltpu.roll(x, RET_DH // 2, axis=1) * sin

    heads = range(RET_HEADS)
    hs = [slice(h * RET_DH, (h + 1) * RET_DH) for h in heads]
    q_all, k_all, v_all, gate = q_ref[...], k_ref[...], v_ref[...], g_ref[...]
    qb = [rope(q_all[:, s]).astype(BF16) for s in hs]
    kf = [rope(k_all[:, s]) * (RET_DH ** -0.5) for s in hs]
    kb = [kf[h].astype(BF16) for h in heads]
    vb = [v_all[:, s].astype(BF16) for s in hs]
    inner = [(_dot_t(qb[h], kb[h]) * dmat_ref[h]).astype(BF16) for h in heads]
    cross = [_dot(qb[h], state_ref[h].astype(BF16)) * qd_ref[h] for h in heads]
    kv = [_dot_tl((kf[h] * kd_ref[h]).astype(BF16), vb[h]) for h in heads]
    outs = []
    for h in heads:
        out = _dot(inner[h], vb[h]) + cross[h]
        state_ref[h] = cd_ref[h] * state_ref[h] + kv[h]
        outs.append(out * lax.rsqrt(jnp.mean(out * out, axis=-1, keepdims=True) + RMS_EPS))
    o_ref[...] = (gate * jax.nn.sigmoid(gate) * jnp.concatenate(outs, axis=1)).astype(o_ref.dtype)


def _retention(proj, cos, sin):
    t = proj.shape[0]
    chunk = RET_CHUNK
    log_gamma = np.log(1.0 - 2.0 ** (-5.0 - np.arange(RET_HEADS, dtype=np.float64)))[:, None, None]
    n = np.arange(chunk, dtype=np.float64)
    lag = n[:, None] - n[None, :]
    dmat = np.where(lag >= 0, np.exp(np.maximum(lag, 0.0)[None] * log_gamma), 0.0)
    ones = np.ones((1, 1, RET_DH))
    qd = np.exp((n + 1.0)[None, :, None] * log_gamma) * ones
    kd = np.exp((chunk - 1.0 - n)[None, :, None] * log_gamma) * ones
    cd = np.exp(chunk * log_gamma) * ones
    tables = [jnp.asarray(a, F32) for a in (dmat, qd, kd, cd)]
    base = C_RET // RET_W

    def col(off):
        return pl.BlockSpec((chunk, RET_W), lambda c, off=off: (c, base + off))

    def full(a):
        return pl.BlockSpec(a.shape, lambda c: (0, 0, 0))

    tab = pl.BlockSpec((chunk, RET_DH), lambda c: (c, 0))
    return pl.pallas_call(
        _retention_kernel,
        out_shape=jax.ShapeDtypeStruct((t, RET_W), BF16),
        grid=(t // chunk,),
        in_specs=[col(0), col(1), col(2), col(3), tab, tab] + [full(a) for a in tables],
        out_specs=pl.BlockSpec((chunk, RET_W), lambda c: (c, 0)),
        scratch_shapes=[pltpu.VMEM((RET_HEADS, RET_DH, RET_DH), F32)],
        compiler_params=_cparams(("arbitrary",)),
        name="retention",
    )(proj, proj, proj, proj, cos, sin, *tables)


def _dot_f32(a, b):
    a_hi = a.astype(BF16)
    b_hi = b.astype(BF16)
    a_lo = (a - a_hi.astype(F32)).astype(BF16)
    b_lo = (b - b_hi.astype(F32)).astype(BF16)
    return _dot(a_hi, b_hi) + (_dot(a_hi, b_lo) + _dot(a_lo, b_hi))


RW_PRE_TM = 512
_MIX_OFF = (0, 512, 1024, 1536, 1664, 1792, 2048, 2176)


def _rwkv_pre_kernel(r_ref, k_ref, v_ref, xw_ref, xa_ref, xg_ref, xv_ref,
                     rp_ref, kp_ref, vp_ref, xwp_ref, xap_ref, xgp_ref, xvp_ref,
                     mix_ref, w0_ref, w2_ref, a0_ref, a2_ref, v0_ref, v2_ref, g2_ref, kk_ref, ka_ref,
                     gsum_ref, vfirst_ref,
                     r_out, lw_out, k_out, v_out, an_out, b_out, g_out, *, use_vres):
    i = pl.program_id(0)

    def mixed(cur_ref, prev_ref, seg):
        cur = cur_ref[...]
        mix = mix_ref[:, _MIX_OFF[seg]:_MIX_OFF[seg + 1]]
        prev_row = jnp.where(i > 0, prev_ref[7:8, :], 0.0)
        rid = lax.broadcasted_iota(jnp.int32, cur.shape, 0)
        shifted = jnp.where(rid == 0, prev_row, pltpu.roll(cur, 1, axis=0))
        return cur + mix * (shifted - cur)

    r = mixed(r_ref, rp_ref, 0)
    k = mixed(k_ref, kp_ref, 1)
    v = mixed(v_ref, vp_ref, 2)
    xw = mixed(xw_ref, xwp_ref, 3)
    xa = mixed(xa_ref, xap_ref, 4)
    xg = mixed(xg_ref, xgp_ref, 5)

    z = -(w0_ref[...] + _dot_f32(jnp.tanh(xw), w2_ref[...]))
    softplus = jnp.maximum(z, 0.0) + jnp.log(1.0 + jnp.exp(-jnp.abs(z)))
    w = -softplus - 0.5
    lw_out[...] = -jnp.exp(w)
    a = jax.nn.sigmoid(a0_ref[...] + _dot_f32(xa, a2_ref[...]))
    g_out[...] = _dot_f32(jax.nn.sigmoid(xg), g2_ref[...])
    if use_vres:
        xv = mixed(xv_ref, xvp_ref, 6)
        v = v + (vfirst_ref[...] - v) * jax.nn.sigmoid(v0_ref[...] + _dot_f32(xv, v2_ref[...]))
    kk = k * kk_ref[...]
    ss = _split_dot(kk * kk, gsum_ref[...], 2)
    kk = kk / jnp.maximum(jnp.sqrt(ss), 1e-12)
    r_out[...] = r
    k_out[...] = k * (1.0 + (a - 1.0) * ka_ref[...])
    v_out[...] = v
    an_out[...] = -kk
    b_out[...] = kk * a


def _rwkv_pre(proj, mix, w0, w2, a0, a2, v0, v2, g2, k_k, k_a, gsum, v_first, use_vres):
    t = proj.shape[0]
    tm = RW_PRE_TM

    def cur(width, off):
        return pl.BlockSpec((tm, width), lambda i: (i, off // width))

    def prev(width, off):
        return pl.BlockSpec((8, width), lambda i: (jnp.maximum(i * (tm // 8) - 1, 0), off // width))

    def full(a):
        return pl.BlockSpec(a.shape, lambda i: (0,) * a.ndim)

    segs = [(RW_W, C_RW_RKV), (RW_W, C_RW_RKV + RW_W), (RW_W, C_RW_RKV + 2 * RW_W),
            (128, C_RW_XW), (128, C_RW_XA), (256, C_RW_XG), (128, C_RW_XV)]
    params = [mix, w0, w2, a0, a2, v0, v2, g2, k_k, k_a, gsum]
    row = pl.BlockSpec((tm, RW_W), lambda i: (i, 0))
    outs = pl.pallas_call(
        functools.partial(_rwkv_pre_kernel, use_vres=use_vres),
        out_shape=[jax.ShapeDtypeStruct((t, RW_W), F32)] * 7,
        grid=(t // tm,),
        in_specs=[cur(w, o) for w, o in segs] + [prev(w, o) for w, o in segs] + [full(p) for p in params] + [row],
        out_specs=[row] * 7,
        compiler_params=_cparams(("parallel",)),
        name="rwkv_pre",
    )(*([proj] * 14), *params, v_first)
    return outs


RW_CHUNK = 128


def _wkv_kernel(r_ref, lw_ref, k_ref, v_ref, an_ref, b_ref, g_ref, lng_ref, lnb_ref, rk_ref,
                gsum_ref, tril_ref, o_ref, s_ref):
    c = pl.program_id(0)
    C = RW_CHUNK
    N = RW_N

    @pl.when(c == 0)
    def _():
        s_ref[...] = jnp.zeros_like(s_ref)

    r = r_ref[...]
    lw = lw_ref[...]
    k = k_ref[...]
    v = v_ref[...]
    lw_hi = lw.astype(BF16)
    rem = lw - lw_hi.astype(F32)
    lw_mid = rem.astype(BF16)
    lw_lo = (rem - lw_mid.astype(F32)).astype(BF16)
    tril = tril_ref[...]
    lg = _dot(tril, lw_hi) + (_dot(tril, lw_mid) + _dot(tril, lw_lo))
    gam = jnp.exp(lg)
    ginv = jnp.exp(-lg)
    at = (an_ref[...] * jnp.exp(lg - lw)).astype(BF16)
    bt = (b_ref[...] * ginv).astype(BF16)
    kt = (k * ginv).astype(BF16)
    rt_f = r * gam
    rt = rt_f.astype(BF16)
    vb = v.astype(BF16)
    g_last = gam[C - 1:C, :]

    rowi = lax.broadcasted_iota(jnp.int32, (C, C), 0)
    coli = lax.broadcasted_iota(jnp.int32, (C, C), 1)
    strict = rowi > coli
    incl = rowi >= coli

    heads = range(RW_HEADS)
    hs = [slice(h * N, (h + 1) * N) for h in heads]
    bth = [bt[:, s] for s in hs]
    kth = [kt[:, s] for s in hs]
    vh = [vb[:, s] for s in hs]
    big = [_dot_t(jnp.concatenate([at[:, s], rt[:, s]], axis=0), jnp.concatenate([bth[h], kth[h]], axis=0))
           for h, s in zip(heads, hs)]
    a_ab = [jnp.where(strict, big[h][:C, :C], 0.0) for h in heads]
    a_ak = [jnp.where(strict, big[h][:C, C:], 0.0).astype(BF16) for h in heads]
    a_rb = [jnp.where(incl, big[h][C:, :C], 0.0).astype(BF16) for h in heads]
    a_rk = [jnp.where(incl, big[h][C:, C:], 0.0).astype(BF16) for h in heads]
    akv = [_dot(a_ak[h], vh[h]) for h in heads]
    def same_block(size):
        shift = size.bit_length() - 1
        return (rowi >> shift) == (coli >> shift)

    tinv = [jnp.where(same_block(2), a_ab[h], 0.0) + jnp.where(rowi == coli, 1.0, 0.0) for h in heads]
    size = 2
    while size < C:
        lower_left = same_block(2 * size) & jnp.logical_not(same_block(size))
        off = [jnp.where(lower_left, a_ab[h], 0.0).astype(BF16) for h in heads]
        tb = [tinv[h].astype(BF16) for h in heads]
        half = [_dot(tb[h], off[h]).astype(BF16) for h in heads]
        tinv = [tinv[h] + _dot(half[h], tb[h]) for h in heads]
        size *= 2
    xb = [_dot(tinv[h].astype(BF16), jnp.concatenate([at[:, hs[h]], akv[h].astype(BF16)], axis=1)).astype(BF16)
          for h in heads]
    yx = [_dot(a_rb[h], xb[h]) for h in heads]
    ykv = [_dot(a_rk[h], vh[h]) for h in heads]
    xtb = [_dot_tl(xb[h], bth[h]) for h in heads]
    vtk = [_dot_tl(vh[h], kth[h]) for h in heads]
    ys = []
    for h in heads:
        gl = g_last[:, hs[h]]
        y1 = (rt_f[:, hs[h]] + yx[h][:, :N]).astype(BF16)
        s0 = s_ref[h]
        s0b = s0.astype(BF16)
        ys.append(_dot_t(y1, s0b) + (yx[h][:, N:] + ykv[h]))
        s_ref[h] = (s0 + _dot(s0b, xtb[h][:N].astype(BF16)) + (xtb[h][N:] + vtk[h])) * gl
    y = jnp.concatenate(ys, axis=1)

    gsum = gsum_ref[...]
    inv_n = 1.0 / N
    mu = _split_dot(y, gsum, 2) * inv_n
    yc = y - mu
    var = _split_dot(yc * yc, gsum, 2) * inv_n
    yn = yc * lax.rsqrt(var + RW_GN_EPS) * lng_ref[...] + lnb_ref[...]
    bonus = _split_dot(r * k * rk_ref[...], gsum, 2)
    o_ref[...] = ((yn + bonus * v) * g_ref[...]).astype(o_ref.dtype)


def _wkv(r, lw, k, v, an, b, g, ln_g, ln_b, r_k, gsum):
    t = r.shape[0]
    C = RW_CHUNK
    tril = jnp.asarray(np.tril(np.ones((C, C), np.float32)), BF16)
    row = pl.BlockSpec((C, RW_W), lambda c: (c, 0))
    vec = pl.BlockSpec((1, RW_W), lambda c: (0, 0))
    return pl.pallas_call(
        _wkv_kernel,
        out_shape=jax.ShapeDtypeStruct((t, RW_W), BF16),
        grid=(t // C,),
        in_specs=[row] * 7 + [vec] * 3 + [pl.BlockSpec((RW_W, RW_W), lambda c: (0, 0)),
                                          pl.BlockSpec((C, C), lambda c: (0, 0))],
        out_specs=row,
        scratch_shapes=[pltpu.VMEM((RW_HEADS, RW_N, RW_N), F32)],
        compiler_params=_cparams(("arbitrary",)),
        name="wkv7",
    )(r, lw, k, v, an, b, g, ln_g, ln_b, r_k, gsum, tril)


NSA_PREP_TM = 512


N_NORM_SLABS = (NSA_W + 2 * NSA_KV_W) // LANES
N_RAW_SLABS = 4 * NSA_KV_W // LANES
N_NSA_SLABS = N_NORM_SLABS + N_RAW_SLABS + 1


def _nsa_prep_kernel(*refs):
    x_refs = refs[:N_NSA_SLABS]
    (cos_ref, sin_ref, w_ref, gmean_ref, big_ref,
     qk_ref, hmat_ref, v1_ref, gate_ref, qpad_ref, ksx_ref) = refs[N_NSA_SLABS:]
    cos = cos_ref[...]
    sin = sin_ref[...]
    gmean = gmean_ref[...]
    half = NSA_DH // 2
    tm = cos.shape[0]
    lane = lax.broadcasted_iota(jnp.int32, cos.shape, 1)
    first_half = (lane % NSA_DH) < half
    low = lane < NSA_DH
    blk_row = (pl.program_id(0) * tm + lax.broadcasted_iota(jnp.int32, cos.shape, 0)) // SLC_BLOCK
    big = big_ref[...]
    mark_lo = jnp.where(blk_row == lane - NSA_DH, big, 0.0)
    mark_hi = jnp.where(low & (blk_row == lane + NSA_DH), big, 0.0).astype(BF16)
    n_q = NSA_W // LANES
    n_k = NSA_KV_W // LANES
    for s in range(N_NORM_SLABS):
        x = x_refs[s][...]
        y = x * lax.rsqrt(_split_dot(x * x, gmean, 2) + RMS_EPS) * w_ref[s]
        rot = jnp.where(first_half, pltpu.roll(y, LANES - half, axis=1), pltpu.roll(y, half, axis=1))
        res32 = y * cos + rot * sin
        res = res32.astype(qk_ref.dtype)
        qk_ref[2 * s] = res[:, :NSA_DH]
        qk_ref[2 * s + 1] = res[:, NSA_DH:]
        if s >= n_q + n_k:
            continue
        swapped = pltpu.roll(res32, NSA_DH, axis=1)
        if s < n_q:
            qpad_ref[2 * s] = jnp.where(low, res32, 0.0).astype(BF16)
            qpad_ref[2 * s + 1] = jnp.where(low, swapped, 0.0).astype(BF16)
        else:
            g0 = 2 * (s - n_q)
            ksx_ref[g0, :, :LANES] = jnp.where(low, res32, mark_lo).astype(BF16)
            ksx_ref[g0 + 1, :, :LANES] = jnp.where(low, swapped, mark_lo).astype(BF16)
            ksx_ref[g0, :, LANES:] = mark_hi
            ksx_ref[g0 + 1, :, LANES:] = mark_hi

    rows = hmat_ref.shape[2]
    low_r = lax.broadcasted_iota(jnp.int32, (rows, LANES), 1) < NSA_DH
    for s in range(NSA_KV_W // LANES * 2):
        kind, pair = divmod(s, NSA_KV_W // LANES)
        x_ref = x_refs[N_NORM_SLABS + s]
        for t2 in range(CMP_STRIDE // 2):
            r0 = x_ref[pl.ds(2 * t2, rows, stride=CMP_STRIDE), :]
            r1 = x_ref[pl.ds(2 * t2 + 1, rows, stride=CMP_STRIDE), :]
            cols = slice(t2 * LANES, (t2 + 1) * LANES)
            hmat_ref[kind, 2 * pair, :, cols] = jnp.where(low_r, r0, pltpu.roll(r1, NSA_DH, axis=1)).astype(BF16)
            hmat_ref[kind, 2 * pair + 1, :, cols] = jnp.where(low_r, pltpu.roll(r0, NSA_DH, axis=1), r1).astype(BF16)

    one_col = jnp.where(lane == NSA_DH, 1.0, 0.0)
    for s in range(NSA_KV_W // LANES * 2):
        x = x_refs[N_NORM_SLABS + N_RAW_SLABS // 2 + s][...]
        v1_ref[2 * s] = jnp.where(low, x, one_col).astype(BF16)
        v1_ref[2 * s + 1] = jnp.where(low, pltpu.roll(x, NSA_DH, axis=1), one_col).astype(BF16)

    xg = x_refs[N_NSA_SLABS - 1][...]
    per_group = 3 * NSA_HG
    for g in range(NSA_G):
        shifted = xg if g == 0 else pltpu.roll(xg, LANES - per_group * g, axis=1)
        gate_ref[g] = jnp.where(lane < per_group, shifted, 0.0)


def _nsa_prep(proj, w_slabs, cos, sin, gmean, big):
    t = proj.shape[0]
    assert t // SLC_BLOCK <= LANES
    tm = NSA_PREP_TM
    base = C_NSA_Q // LANES
    tab = pl.BlockSpec((tm, LANES), lambda i: (i, 0))
    n_heads_out = 2 * N_NORM_SLABS
    return pl.pallas_call(
        _nsa_prep_kernel,
        out_shape=[jax.ShapeDtypeStruct((n_heads_out, t, NSA_DH), BF16),
                   jax.ShapeDtypeStruct((2, NSA_G, t // CMP_STRIDE, CMP_STRIDE * NSA_DH), BF16),
                   jax.ShapeDtypeStruct((2 * NSA_G, t, LANES), BF16),
                   jax.ShapeDtypeStruct((NSA_G, t, LANES), F32),
                   jax.ShapeDtypeStruct((NSA_HEADS, t, LANES), BF16),
                   jax.ShapeDtypeStruct((NSA_G, t, 2 * LANES), BF16)],
        grid=(t // tm,),
        in_specs=[pl.BlockSpec((tm, LANES), lambda i, s=s: (i, base + s)) for s in range(N_NSA_SLABS)]
        + [tab, tab, pl.BlockSpec((N_NORM_SLABS, 1, LANES), lambda i: (0, 0, 0)),
           pl.BlockSpec((LANES, LANES), lambda i: (0, 0)), pl.BlockSpec((1, LANES), lambda i: (0, 0))],
        out_specs=[pl.BlockSpec((n_heads_out, tm, NSA_DH), lambda i: (0, i, 0)),
                   pl.BlockSpec((2, NSA_G, tm // CMP_STRIDE, CMP_STRIDE * NSA_DH), lambda i: (0, 0, i, 0)),
                   pl.BlockSpec((2 * NSA_G, tm, LANES), lambda i: (0, i, 0)),
                   pl.BlockSpec((NSA_G, tm, LANES), lambda i: (0, i, 0)),
                   pl.BlockSpec((NSA_HEADS, tm, LANES), lambda i: (0, i, 0)),
                   pl.BlockSpec((NSA_G, tm, 2 * LANES), lambda i: (0, i, 0))],
        compiler_params=_cparams(("parallel",)),
        name="nsa_prep",
    )(*([proj] * N_NSA_SLABS), cos, sin, w_slabs, gmean, big)


def _compress_kernel(h_ref, w1_ref, w2_ref, pe_ref, nw_ref, cos_ref, sin_ref, o_ref):
    kind = pl.program_id(0)
    nc = h_ref.shape[2]
    half_in = CMP_STRIDE * NSA_DH
    hm = h_ref[0, 0]
    w1 = w1_ref[0]
    first = _dot(hm, w1[:half_in])
    second = _dot(hm, w1[half_in:])
    const = _dot(pe_ref[0], w1)[0:1, :]
    pre = first + pltpu.roll(second, nc - 1, axis=0) + const
    hid = 0.5 * pre * (1.0 + jnp.tanh(0.7978845608028654 * (pre + 0.044715 * pre * pre * pre)))
    out = _dot(hid.astype(BF16), w2_ref[0])
    y = out * lax.rsqrt(jnp.mean(out * out, axis=-1, keepdims=True) + RMS_EPS) * nw_ref[...]
    half = NSA_DH // 2
    rot = jnp.concatenate([y[:, half:], y[:, :half]], axis=1)
    roped = y * cos_ref[...] + rot * sin_ref[...]
    o_ref[0, 0] = jnp.where(kind == 0, roped, out).astype(o_ref.dtype)


def _compress(hmat, w1, w2, pe, nw, cos_c, sin_c):
    _, g, nc, width = hmat.shape
    return pl.pallas_call(
        _compress_kernel,
        out_shape=jax.ShapeDtypeStruct((2, g, nc, NSA_DH), BF16),
        grid=(2, g),
        in_specs=[pl.BlockSpec((1, 1, nc, width), lambda a, b: (a, b, 0, 0)),
                  pl.BlockSpec((1, 2 * width, CMP_HIDDEN), lambda a, b: (a, 0, 0)),
                  pl.BlockSpec((1, CMP_HIDDEN, NSA_DH), lambda a, b: (a, 0, 0)),
                  pl.BlockSpec((1, 8, 2 * width), lambda a, b: (a, 0, 0)),
                  pl.BlockSpec((1, NSA_DH), lambda a, b: (0, 0)),
                  pl.BlockSpec((nc, NSA_DH), lambda a, b: (0, 0)),
                  pl.BlockSpec((nc, NSA_DH), lambda a, b: (0, 0))],
        out_specs=pl.BlockSpec((1, 1, nc, NSA_DH), lambda a, b: (a, b, 0, 0)),
        compiler_params=_cparams(("parallel", "parallel")),
        name="nsa_compress",
    )(hmat, w1, w2, pe, nw, cos_c, sin_c)


SLC_KT = 1024
WIN_ROWS = 256
WIN_KEYS = WINDOW + WIN_ROWS


INT32_MIN = -2 ** 31


def _nsa_attn_kernel(q_ref, qpad_ref, gate_ref, kc_ref, vc_ref, ks_ref, vs_ref, kw_ref, vw_ref, cis_ref, ltri_ref,
                     o_ref):
    qi = pl.program_id(1)
    nc = kc_ref.shape[2]
    nb = cis_ref.shape[1]
    heads = range(NSA_HG)
    start = qi * NSA_TQ
    qh = [q_ref[hh] for hh in heads]
    tok = start + lax.broadcasted_iota(jnp.int32, (NSA_TQ, 1), 0)

    def softmax_terms(scores):
        return [jnp.exp(s - jnp.max(s, axis=-1, keepdims=True)) for s in scores]

    def cmp_branch(width):
        kc = kc_ref[0, 0, :width, :]
        vc = vc_ref[0, 0, :width, :]
        cend = lax.broadcasted_iota(jnp.int32, (1, width), 1) * CMP_STRIDE + (CMP_BLOCK - 1)
        bias_c = jnp.where(cend <= tok, 0.0, NEG_INF)
        ec = softmax_terms([_dot_t(qh[hh], kc) + bias_c for hh in heads])
        sees_any = tok >= CMP_BLOCK - 1
        pc = [ec[hh] * jnp.where(sees_any, 1.0 / jnp.sum(ec[hh], axis=-1, keepdims=True), 0.0) for hh in heads]
        outs_c = tuple(_dot(pc[hh].astype(BF16), vc) for hh in heads)
        psum = pc[0]
        for hh in range(1, NSA_HG):
            psum = psum + pc[hh]
        return outs_c, _split_dot(psum, cis_ref[:width, :], 2)

    n_quarters = 4
    quarter = nc // n_quarters
    n_visible = (start + NSA_TQ - CMP_BLOCK) // CMP_STRIDE + 1
    which = jnp.clip((n_visible - 1) // quarter, 0, n_quarters - 1)
    o_c, score = lax.switch(which, [functools.partial(cmp_branch, (r + 1) * quarter) for r in range(n_quarters)])
    blk = lax.broadcasted_iota(jnp.int32, (1, nb), 1)
    cur = tok // SLC_BLOCK
    forced = (blk == 0) | (blk == cur) | (blk == cur - 1)
    score = jnp.where(forced, FORCED_SCORE, score)
    score = jnp.where(blk <= cur, score, -jnp.inf)


    bits = lax.bitcast_convert_type(score.T, jnp.int32)
    key_all = bits ^ ((bits >> 31) & 0x7FFFFFFF)

    def select_blocks(nbv):
        key = key_all[:nbv]
        n_sel = min(SLC_TOPK, nbv)

        def enough(c):
            return jnp.sum(jnp.where(key >= c, 1.0, 0.0), axis=0, keepdims=True) >= n_sel

        def enough3(c1, c2, c3):
            packed = jnp.where(key >= c3, 65793.0, jnp.where(key >= c2, 257.0, jnp.where(key >= c1, 1.0, 0.0)))
            tot = jnp.sum(packed, axis=0, keepdims=True).astype(jnp.int32)
            return (tot & 255) >= n_sel, ((tot >> 8) & 255) >= n_sel, (tot >> 16) >= n_sel

        zero_row = jnp.zeros((1, NSA_TQ), jnp.int32)
        thr = jnp.where(enough(zero_row), zero_row, INT32_MIN)
        for hi in range(30, 0, -2):
            c1 = thr + (1 << (hi - 1))
            c2 = thr + (1 << hi)
            c3 = c2 + (1 << (hi - 1))
            e1, e2, e3 = enough3(c1, c2, c3)
            thr = jnp.where(e3, c3, jnp.where(e2, c2, jnp.where(e1, c1, thr)))
        c1 = thr + 1
        thr = jnp.where(enough(c1), c1, thr)
        above = key > thr
        tied = key == thr
        need = n_sel - jnp.sum(jnp.where(above, 1.0, 0.0), axis=0, keepdims=True)
        tied_before = _dot(ltri_ref[:nbv, :nbv], jnp.where(tied, 1.0, 0.0).astype(BF16))
        chosen = jnp.where(above | (tied & (tied_before < need)), 1.0, 0.0)
        if nbv < LANES:
            chosen = jnp.concatenate([chosen, jnp.zeros((LANES - nbv, NSA_TQ), F32)], axis=0)
        return chosen.T

    assert nb <= LANES
    n_steps = 4
    blocks_visible = (start + NSA_TQ - 1) // SLC_BLOCK + 1
    widths = sorted({max(-(-nb * (r + 1) // n_steps // 8) * 8, min(nb, SLC_TOPK)) for r in range(n_steps)})
    which_sel = sum((blocks_visible > w).astype(jnp.int32) for w in widths[:-1])
    sel = lax.switch(which_sel, [functools.partial(select_blocks, w) for w in widths])
    lane_q = lax.broadcasted_iota(jnp.int32, (NSA_TQ, LANES), 1)
    sel_sw = pltpu.roll(sel, NSA_DH, axis=1)
    sel_lo = jnp.where(lane_q >= NSA_DH, sel_sw, 0.0)
    sel_hi = jnp.where(lane_q < NSA_DH, sel_sw, 0.0)
    qx = [jnp.concatenate([qpad_ref[hh].astype(F32) + sel_lo, sel_hi], axis=1).astype(BF16) for hh in heads]

    def window_rows(r0):
        kbase = pl.multiple_of(jnp.maximum(start + r0 - WINDOW, 0), WIN_ROWS)
        kw = kw_ref[0, pl.ds(kbase, WIN_KEYS), :]
        vw = vw_ref[0, pl.ds(kbase, WIN_KEYS), :]
        lag = tok[r0:r0 + WIN_ROWS] - (kbase + lax.broadcasted_iota(jnp.int32, (1, WIN_KEYS), 1))
        bias_w = jnp.where((lag >= 0) & (lag < WINDOW), 0.0, NEG_INF)
        sw = [_dot_t(qh[hh][r0:r0 + WIN_ROWS], kw) + bias_w for hh in heads]
        ew = [jnp.exp((s - jnp.max(s, axis=-1, keepdims=True)).astype(BF16)) for s in sw]
        return [_dot(ew[hh], vw) for hh in heads]

    win_parts = [window_rows(r0) for r0 in range(0, NSA_TQ, WIN_ROWS)]
    acc_w = [jnp.concatenate([part[hh] for part in win_parts], axis=0) for hh in heads]

    n_tiles = (start + NSA_TQ + SLC_KT - 1) // SLC_KT

    def sel_tile(j, carry, causal, width=SLC_KT):
        ms, accs = carry
        k0 = pl.multiple_of(j * SLC_KT, SLC_KT)
        kt = ks_ref[0, pl.ds(k0, width), :]
        vt = vs_ref[0, pl.ds(k0, width), :]
        sj = [_dot_t(qx[hh], kt) for hh in heads]
        if causal:
            future = (k0 + lax.broadcasted_iota(jnp.int32, (1, width), 1)) > tok
            sj = [jnp.where(future, NEG_INF, s) for s in sj]
        m_new = [jnp.maximum(ms[hh], jnp.max(sj[hh], axis=-1, keepdims=True)) for hh in heads]
        pj = [jnp.exp((sj[hh] - m_new[hh]).astype(BF16)) for hh in heads]
        accs = [jnp.exp(ms[hh] - m_new[hh]) * accs[hh] + _dot(pj[hh], vt) for hh in heads]
        return tuple(m_new), tuple(accs)

    m0 = tuple(jnp.full((NSA_TQ, 1), NEG_INF, F32) for _ in heads)
    a0 = tuple(jnp.zeros((NSA_TQ, LANES), F32) for _ in heads)
    carry = lax.fori_loop(0, n_tiles - 1, functools.partial(sel_tile, causal=False), (m0, a0))
    per_tile = SLC_KT // NSA_TQ
    last = [functools.partial(sel_tile, n_tiles - 1, causal=True, width=(r + 1) * NSA_TQ) for r in range(per_tile)]
    _, acc_s = lax.switch(qi % per_tile, last, carry)

    gt = jax.nn.sigmoid(gate_ref[0])
    outs = []
    for hh in heads:
        g0, g1, g2 = (gt[:, 3 * hh + br:3 * hh + br + 1] for br in range(3))
        scale_s = g1 / acc_s[hh][:, NSA_DH:NSA_DH + 1]
        scale_w = g2 / acc_w[hh][:, NSA_DH:NSA_DH + 1]
        outs.append(g0 * o_c[hh] + scale_s * acc_s[hh][:, :NSA_DH] + scale_w * acc_w[hh][:, :NSA_DH])
    o_ref[...] = jnp.concatenate(outs, axis=1).astype(o_ref.dtype)


def _nsa_attn(qk_hm, qpad, ksx, gates, cmp_kv, v1, cis, ltri):
    t = qk_hm.shape[1]
    nq = t // NSA_TQ
    nc = cmp_kv.shape[2]
    nb = cis.shape[1]
    ks_spec = pl.BlockSpec((1, t, 2 * LANES), lambda g, i: (g, 0, 0))
    kw_spec = pl.BlockSpec((1, t, NSA_DH), lambda g, i: (NSA_HEADS + NSA_G + g, 0, 0))
    vs_spec = pl.BlockSpec((1, t, LANES), lambda g, i: (g, 0, 0))
    vw_spec = pl.BlockSpec((1, t, LANES), lambda g, i: (NSA_G + g, 0, 0))
    return pl.pallas_call(
        _nsa_attn_kernel,
        out_shape=jax.ShapeDtypeStruct((t, NSA_W), BF16),
        grid=(NSA_G, nq),
        in_specs=[pl.BlockSpec((NSA_HG, NSA_TQ, NSA_DH), lambda g, i: (g, i, 0)),
                  pl.BlockSpec((NSA_HG, NSA_TQ, LANES), lambda g, i: (g, i, 0)),
                  pl.BlockSpec((1, NSA_TQ, LANES), lambda g, i: (g, i, 0)),
                  pl.BlockSpec((1, 1, nc, NSA_DH), lambda g, i: (0, g, 0, 0)),
                  pl.BlockSpec((1, 1, nc, NSA_DH), lambda g, i: (1, g, 0, 0)),
                  ks_spec, vs_spec, kw_spec, vw_spec,
                  pl.BlockSpec((nc, nb), lambda g, i: (0, 0)),
                  pl.BlockSpec((nb, nb), lambda g, i: (0, 0))],
        out_specs=pl.BlockSpec((NSA_TQ, NSA_HG * NSA_DH), lambda g, i: (i, g)),
        compiler_params=_cparams(("parallel", "arbitrary")),
        name="nsa_attention",
    )(qk_hm, qpad, gates, cmp_kv, cmp_kv, ksx, v1, qk_hm, v1, cis, ltri)


def _rope_tables(pos, dh, reps):
    half = dh // 2
    inv_freq = ROPE_THETA ** (-np.arange(half, dtype=np.float64) / half)
    ang = np.asarray(pos, np.float64)[:, None] * inv_freq[None, :]
    cos = np.cos(ang)
    sin = np.sin(ang)
    cos_t = np.tile(np.concatenate([cos, cos], axis=1), (1, reps))
    sin_t = np.tile(np.concatenate([-sin, sin], axis=1), (1, reps))
    return jnp.asarray(cos_t, F32), jnp.asarray(sin_t, F32)


def _pad_cols(a, width):
    return jnp.pad(a, ((0, 0), (0, width - a.shape[1])))


def _pad_rows(a, height):
    return jnp.pad(a, ((0, height - a.shape[0]), (0, 0)))


def _pack_w_in_kernel(w_ref, xv_ref, o_ref):
    o_ref[...] = _pack_w_in(w_ref[...], xv_ref[...]).astype(o_ref.dtype)


def _pack_w_in_call(w_in, layer, xv):
    _, d, n_in = w_in.shape
    tk = 256
    return pl.pallas_call(
        _pack_w_in_kernel,
        out_shape=jax.ShapeDtypeStruct((d, PROJ_W), BF16),
        grid=(d // tk,),
        in_specs=[pl.BlockSpec((None, tk, n_in), lambda i: (layer, i, 0)),
                  pl.BlockSpec((tk, LANES), lambda i: (i, 0))],
        out_specs=pl.BlockSpec((tk, PROJ_W), lambda i: (i, 0)),
        compiler_params=_cparams(("parallel",)),
        name="pack_w_in",
    )(w_in, xv)


def _pack_w_in(w_l, xv):
    d = w_l.shape[0]
    o = 0
    ret = w_l[:, o:o + 4 * RET_W]; o += 4 * RET_W
    rkv = w_l[:, o:o + 3 * RW_W]; o += 3 * RW_W
    xw = w_l[:, o:o + RW_DECAY_RANK]; o += RW_DECAY_RANK
    xa = w_l[:, o:o + RW_A_RANK]; o += RW_A_RANK
    xg = w_l[:, o:o + RW_GATE_RANK]; o += RW_GATE_RANK
    q = w_l[:, o:o + NSA_W]; o += NSA_W
    kc, vc, ks, vs, kw, vw = (w_l[:, o + i * NSA_KV_W:o + (i + 1) * NSA_KV_W] for i in range(6))
    o += 6 * NSA_KV_W
    gates = w_l[:, o:o + 3 * NSA_HEADS]
    return jnp.concatenate([ret, rkv, _pad_cols(xw, LANES), _pad_cols(xa, LANES), xg, xv,
                            q, ks, kw, kc, vc, vs, vw, _pad_cols(gates, LANES)], axis=1)


def _pack_mix(mix, vres_mix):
    o = 3 * RW_W
    xw = mix[o:o + RW_DECAY_RANK]; o += RW_DECAY_RANK
    xa = mix[o:o + RW_A_RANK]; o += RW_A_RANK
    xg = mix[o:o + RW_GATE_RANK]
    z = lambda n: jnp.zeros((n,), mix.dtype)
    xv = z(LANES) if vres_mix is None else jnp.concatenate([vres_mix, z(LANES - RW_V_RANK)])
    return jnp.concatenate([mix[:3 * RW_W], xw, z(LANES - RW_DECAY_RANK), xa, z(LANES - RW_A_RANK), xg, xv])[None, :]


def kernel(x, ln1_g, w_in, w_in_vres, rwkv_mix, rwkv_vres_mix, rwkv_w0, rwkv_w2, rwkv_a0, rwkv_a2, rwkv_v0, rwkv_v2, rwkv_g2, rwkv_k_k, rwkv_k_a, rwkv_r_k, rwkv_ln_g, rwkv_ln_b, nsa_q_norm, nsa_k_norm, nsa_cmp_pe, nsa_cmp_k_w1, nsa_cmp_k_w2, nsa_cmp_v_w1, nsa_cmp_v_w2, w_out, ln2_g, w_up, w_down):
    bsz, t, d = x.shape
    assert bsz == 1
    depth = w_in.shape[0]
    nc = t // CMP_STRIDE
    nb = t // SLC_BLOCK
    xs = x.reshape(t, d)

    pos = np.arange(t)
    cos_r, sin_r = _rope_tables(pos, RET_DH, 1)
    cos_n, sin_n = _rope_tables(pos, NSA_DH, 2)
    cos_c, sin_c = _rope_tables(np.arange(nc) * CMP_STRIDE + (CMP_BLOCK - 1), NSA_DH, 1)
    lane_head = np.arange(RW_W) // RW_N
    gsum = jnp.asarray(lane_head[:, None] == lane_head[None, :], BF16)
    lane_h2 = np.arange(LANES) // NSA_DH
    gmean = jnp.asarray((lane_h2[:, None] == lane_h2[None, :]) / float(NSA_DH), BF16)
    cstart = np.arange(nc) * CMP_STRIDE
    sstart = np.arange(nb) * SLC_BLOCK
    cis = jnp.asarray((cstart[:, None] <= sstart[None, :] + SLC_BLOCK - 1)
                      & (cstart[:, None] + CMP_BLOCK - 1 >= sstart[None, :]), BF16)

    ltri = jnp.asarray(np.tril(np.ones((nb, nb), np.float32), -1), BF16)

    v_first = jnp.zeros((t, RW_W), F32)
    for l in range(depth):
        vres = l > 0
        xv_w = _pad_cols(w_in_vres[l - 1], LANES) if vres else jnp.zeros((d, LANES), F32)
        w_cat = _pack_w_in_call(w_in, l, xv_w)
        proj = _norm_matmul(xs, ln1_g[l], w_cat, name="in_proj")

        o_ret = _retention(proj, cos_r, sin_r)

        row = lambda a: a.reshape(1, -1)
        mix = _pack_mix(rwkv_mix[l], rwkv_vres_mix[l - 1] if vres else None)
        v0 = row(rwkv_v0[l - 1]) if vres else jnp.zeros((1, RW_W), F32)
        v2 = _pad_rows(rwkv_v2[l - 1], LANES) if vres else jnp.zeros((LANES, RW_W), F32)
        r_, lw_, k_, v_, an_, b_, g_ = _rwkv_pre(
            proj, mix, row(rwkv_w0[l]), _pad_rows(rwkv_w2[l], LANES), row(rwkv_a0[l]), _pad_rows(rwkv_a2[l], LANES),
            v0, v2, rwkv_g2[l], row(rwkv_k_k[l]), row(rwkv_k_a[l]), gsum, v_first, vres)
        if not vres:
            v_first = v_
        o_rwkv = _wkv(r_, lw_, k_, v_, an_, b_, g_, row(rwkv_ln_g[l]), row(rwkv_ln_b[l]), row(rwkv_r_k[l]), gsum)

        n_q = NSA_W // LANES
        n_k = NSA_KV_W // LANES
        w_slabs = jnp.stack([jnp.tile(nsa_q_norm[l] * (NSA_DH ** -0.5), 2)] * n_q
                            + [jnp.tile(nsa_k_norm[l, 1], 2)] * n_k + [jnp.tile(nsa_k_norm[l, 2], 2)] * n_k)[:, None, :]
        s_max = 1.02 * jnp.max(jnp.abs(nsa_q_norm[l])) * jnp.max(jnp.abs(nsa_k_norm[l, 1])) * (NSA_DH ** 0.5)
        big = jnp.full((1, LANES), jnp.exp2(jnp.ceil(jnp.log2(2.0 * s_max + 128.0))), F32)
        qk_hm, hmat, v1, gates, qpad, ksx = _nsa_prep(proj, w_slabs, cos_n, sin_n, gmean, big)
        w1 = jnp.stack([nsa_cmp_k_w1[l], nsa_cmp_v_w1[l]]).astype(BF16)
        w2 = jnp.stack([nsa_cmp_k_w2[l], nsa_cmp_v_w2[l]]).astype(BF16)
        pe = jnp.broadcast_to(nsa_cmp_pe[l].reshape(2, 1, CMP_BLOCK * NSA_DH), (2, 8, CMP_BLOCK * NSA_DH)).astype(BF16)
        cmp_kv = _compress(hmat, w1, w2, pe, row(nsa_k_norm[l, 0]), cos_c, sin_c)
        o_nsa = _nsa_attn(qk_hm, qpad, ksx, gates, cmp_kv, v1, cis, ltri)

        xs = _out_proj(o_ret, o_rwkv, o_nsa, w_out, l, xs)

        up = _norm_matmul(xs, ln2_g[l], w_up, layer=l, act="relu2", out_dtype=BF16, name="mlp_up")
        xs = _matmul(up, w_down, layer=l, res=xs, name="mlp_down")
    return xs.reshape(bsz, t, d)
```

```python
import functools

import numpy as np
import jax
import jax.numpy as jnp
from jax import lax
from jax.experimental import pallas as pl
from jax.experimental.pallas import tpu as pltpu

F32 = jnp.float32
BF16 = jnp.bfloat16

D_MODEL = 2048
RET_HEADS, RET_DH = 4, 128
RET_W = RET_HEADS * RET_DH
RW_HEADS, RW_N = 8, 64
RW_W = RW_HEADS * RW_N
RW_DECAY_RANK, RW_A_RANK, RW_V_RANK, RW_GATE_RANK = 96, 96, 64, 256
RW_GN_EPS = 64e-5
NSA_HEADS, NSA_G, NSA_DH = 16, 4, 64
NSA_HG = NSA_HEADS // NSA_G
NSA_W = NSA_HEADS * NSA_DH
NSA_KV_W = NSA_G * NSA_DH
CMP_BLOCK, CMP_STRIDE, CMP_HIDDEN = 32, 16, 256
SLC_BLOCK, SLC_TOPK, WINDOW = 64, 16, 512
NSA_TQ = 512
ROPE_THETA = 10000.0
MLP_HIDDEN = 4 * D_MODEL
RMS_EPS = 1e-6
NEG_INF = -1e30
FORCED_SCORE = 1e9

LANES = 128
VMEM_LIMIT = 56 * 1024 * 1024

C_RET = 0
C_RW_RKV = 2048
C_RW_XW = 3584
C_RW_XA = 3712
C_RW_XG = 3840
C_RW_XV = 4096
C_NSA_Q = 4224
C_NSA_KS = 5248
C_NSA_KW = 5504
C_NSA_KC = 5760
C_NSA_VC = 6016
C_NSA_VS = 6272
C_NSA_VW = 6528
C_NSA_GATE = 6784
PROJ_W = 6912


def _cparams(sem):
    return pltpu.CompilerParams(dimension_semantics=sem, vmem_limit_bytes=VMEM_LIMIT)


def _dot(a, b):
    return jnp.dot(a, b, preferred_element_type=F32)


def _dot_t(a, b):
    return lax.dot_general(a, b, (((1,), (1,)), ((), ())), preferred_element_type=F32)


def _dot_tl(a, b):
    return lax.dot_general(a, b, (((0,), (0,)), ((), ())), preferred_element_type=F32)


def _split_dot(x, m_bf16, passes):
    hi = x.astype(BF16)
    acc = _dot(hi, m_bf16)
    rem = x - hi.astype(F32)
    for _ in range(passes - 1):
        piece = rem.astype(BF16)
        acc = acc + _dot(piece, m_bf16)
        rem = rem - piece.astype(F32)
    return acc


def _mm_kernel(*refs, nk, act, has_res):
    a_ref, b_ref = refs[:2]
    r_ref = refs[2] if has_res else None
    o_ref = refs[3] if has_res else refs[2]

    def finish(y):
        if act == "relu2":
            y = jnp.square(jnp.maximum(y, 0.0))
        if has_res:
            y = y + r_ref[...]
        o_ref[...] = y.astype(o_ref.dtype)

    part = _dot(a_ref[...], b_ref[...].astype(BF16))
    if nk == 1:
        finish(part)
        return
    k = pl.program_id(2)
    if act is None and o_ref.dtype == F32:
        @pl.when(k == 0)
        def _():
            o_ref[...] = part + r_ref[...] if has_res else part

        @pl.when(k > 0)
        def _():
            o_ref[...] += part

        return
    acc_ref = refs[-1]

    @pl.when(k == 0)
    def _():
        acc_ref[...] = part

    @pl.when(k > 0)
    def _():
        acc_ref[...] += part

    @pl.when(k == nk - 1)
    def _():
        finish(acc_ref[...])


def _pick(n, cands):
    for c in cands:
        if n % c == 0:
            return c
    raise ValueError(f"no tile for {n}")


def _weight_spec(b, layer, block, index_map):
    if layer is None:
        return pl.BlockSpec(block, index_map)
    return pl.BlockSpec((None,) + block, lambda *idx: (layer,) + index_map(*idx))


def _matmul(a, b, *, layer=None, res=None, act=None, out_dtype=F32, name="matmul"):
    m, kd = a.shape
    n = b.shape[-1]
    tm = _pick(m, (1024, 512, 256, 128))
    tn = _pick(n, (1024, 768, 512, 256, 128))
    tk = _pick(kd, (2048, 1024, 512))
    nk = kd // tk
    in_specs = [pl.BlockSpec((tm, tk), lambda i, j, k: (i, k)),
                _weight_spec(b, layer, (tk, tn), lambda i, j, k: (k, j))]
    args = [a, b]
    if res is not None:
        in_specs.append(pl.BlockSpec((tm, tn), lambda i, j, k: (i, j)))
        args.append(res)
    return pl.pallas_call(
        functools.partial(_mm_kernel, nk=nk, act=act, has_res=res is not None),
        out_shape=jax.ShapeDtypeStruct((m, n), out_dtype),
        grid=(m // tm, n // tn, nk),
        in_specs=in_specs,
        out_specs=pl.BlockSpec((tm, tn), lambda i, j, k: (i, j)),
        scratch_shapes=[pltpu.VMEM((tm, tn), F32)] if nk > 1 and (act is not None or out_dtype != F32) else [],
        compiler_params=_cparams(("parallel", "parallel", "arbitrary")),
        name=name,
    )(*args)


def _norm_mm_kernel(x_ref, g_ref, b_ref, o_ref, h_ref, *, act):
    @pl.when(pl.program_id(1) == 0)
    def _():
        x = x_ref[...]
        ms = jnp.mean(x * x, axis=-1, keepdims=True)
        h_ref[...] = (x * lax.rsqrt(ms + RMS_EPS) * g_ref[...]).astype(BF16)

    y = _dot(h_ref[...], b_ref[...].astype(BF16))
    if act == "relu2":
        y = jnp.square(jnp.maximum(y, 0.0))
    o_ref[...] = y.astype(o_ref.dtype)


def _norm_matmul(x, g, b, *, layer=None, act=None, out_dtype=F32, name="norm_matmul"):
    m, kd = x.shape
    n = b.shape[-1]
    tm = _pick(m, (1024, 512, 256, 128))
    tn = _pick(n, (1024, 768, 512, 256, 128))
    return pl.pallas_call(
        functools.partial(_norm_mm_kernel, act=act),
        out_shape=jax.ShapeDtypeStruct((m, n), out_dtype),
        grid=(m // tm, n // tn),
        in_specs=[pl.BlockSpec((tm, kd), lambda i, j: (i, 0)),
                  pl.BlockSpec((1, kd), lambda i, j: (0, 0)),
                  _weight_spec(b, layer, (kd, tn), lambda i, j: (0, j))],
        out_specs=pl.BlockSpec((tm, tn), lambda i, j: (i, j)),
        scratch_shapes=[pltpu.VMEM((tm, kd), BF16)],
        compiler_params=_cparams(("parallel", "arbitrary")),
        name=name,
    )(x, g.reshape(1, kd), b)


def _out_proj_kernel(ret_ref, rw_ref, nsa_ref, w_ref, res_ref, o_ref, wb_ref):
    @pl.when(pl.program_id(1) == 0)
    def _():
        wb_ref[...] = w_ref[...].astype(BF16)

    k0 = ret_ref.shape[1]
    k1 = k0 + rw_ref.shape[1]
    acc = _dot(ret_ref[...], wb_ref[:k0, :])
    acc = acc + _dot(rw_ref[...], wb_ref[k0:k1, :])
    acc = acc + _dot(nsa_ref[...], wb_ref[k1:, :])
    o_ref[...] = acc + res_ref[...]


def _out_proj(o_ret, o_rwkv, o_nsa, w, layer, res):
    m = o_ret.shape[0]
    kd, n = w.shape[-2:]
    assert kd == o_ret.shape[1] + o_rwkv.shape[1] + o_nsa.shape[1]
    tm = _pick(m, (1024, 512, 256, 128))
    tn = _pick(n, (1024, 512, 256, 128))

    def rows(a):
        return pl.BlockSpec((tm, a.shape[1]), lambda j, i: (i, 0))

    return pl.pallas_call(
        _out_proj_kernel,
        out_shape=jax.ShapeDtypeStruct((m, n), F32),
        grid=(n // tn, m // tm),
        in_specs=[rows(o_ret), rows(o_rwkv), rows(o_nsa),
                  _weight_spec(w, layer, (kd, tn), lambda j, i: (0, j)),
                  pl.BlockSpec((tm, tn), lambda j, i: (i, j))],
        out_specs=pl.BlockSpec((tm, tn), lambda j, i: (i, j)),
        scratch_shapes=[pltpu.VMEM((kd, tn), BF16)],
        compiler_params=_cparams(("parallel", "arbitrary")),
        name="out_proj",
    )(o_ret, o_rwkv, o_nsa, w, res)


RET_CHUNK = 512


def _retention_kernel(q_ref, k_ref, v_ref, g_ref, cos_ref, sin_ref, dmat_ref, qd_ref, kd_ref, cd_ref, o_ref, state_ref):
    @pl.when(pl.program_id(0) == 0)
    def _():
        state_ref[...] = jnp.zeros_like(state_ref)

    cos = cos_ref[...]
    sin = sin_ref[...]

    def rope(x):
        return x * cos + pltpu.roll(x, RET_DH // 2, axis=1) * sin

    heads = range(RET_HEADS)
    hs = [slice(h * RET_DH, (h + 1) * RET_DH) for h in heads]
    q_all, k_all, v_all, gate = q_ref[...], k_ref[...], v_ref[...], g_ref[...]
    qb = [rope(q_all[:, s]).astype(BF16) for s in hs]
    kf = [rope(k_all[:, s]) * (RET_DH ** -0.5) for s in hs]
    kb = [kf[h].astype(BF16) for h in heads]
    vb = [v_all[:, s].astype(BF16) for s in hs]
    inner = [(_dot_t(qb[h], kb[h]) * dmat_ref[h]).astype(BF16) for h in heads]
    cross = [_dot(qb[h], state_ref[h].astype(BF16)) * qd_ref[h] for h in heads]
    kv = [_dot_tl((kf[h] * kd_ref[h]).astype(BF16), vb[h]) for h in heads]
    outs = []
    for h in heads:
        out = _dot(inner[h], vb[h]) + cross[h]
        state_ref[h] = cd_ref[h] * state_ref[h] + kv[h]
        outs.append(out * lax.rsqrt(jnp.mean(out * out, axis=-1, keepdims=True) + RMS_EPS))
    o_ref[...] = (gate * jax.nn.sigmoid(gate) * jnp.concatenate(outs, axis=1)).astype(o_ref.dtype)


def _retention(proj, cos, sin):
    t = proj.shape[0]
    chunk = RET_CHUNK
    log_gamma = np.log(1.0 - 2.0 ** (-5.0 - np.arange(RET_HEADS, dtype=np.float64)))[:, None, None]
    n = np.arange(chunk, dtype=np.float64)
    lag = n[:, None] - n[None, :]
    dmat = np.where(lag >= 0, np.exp(np.maximum(lag, 0.0)[None] * log_gamma), 0.0)
    ones = np.ones((1, 1, RET_DH))
    qd = np.exp((n + 1.0)[None, :, None] * log_gamma) * ones
    kd = np.exp((chunk - 1.0 - n)[None, :, None] * log_gamma) * ones
    cd = np.exp(chunk * log_gamma) * ones
    tables = [jnp.asarray(a, F32) for a in (dmat, qd, kd, cd)]
    base = C_RET // RET_W

    def col(off):
        return pl.BlockSpec((chunk, RET_W), lambda c, off=off: (c, base + off))

    def full(a):
        return pl.BlockSpec(a.shape, lambda c: (0, 0, 0))

    tab = pl.BlockSpec((chunk, RET_DH), lambda c: (c, 0))
    return pl.pallas_call(
        _retention_kernel,
        out_shape=jax.ShapeDtypeStruct((t, RET_W), BF16),
        grid=(t // chunk,),
        in_specs=[col(0), col(1), col(2), col(3), tab, tab] + [full(a) for a in tables],
        out_specs=pl.BlockSpec((chunk, RET_W), lambda c: (c, 0)),
        scratch_shapes=[pltpu.VMEM((RET_HEADS, RET_DH, RET_DH), F32)],
        compiler_params=_cparams(("arbitrary",)),
        name="retention",
    )(proj, proj, proj, proj, cos, sin, *tables)


def _dot_f32(a, b):
    a_hi = a.astype(BF16)
    b_hi = b.astype(BF16)
    a_lo = (a - a_hi.astype(F32)).astype(BF16)
    b_lo = (b - b_hi.astype(F32)).astype(BF16)
    return _dot(a_hi, b_hi) + (_dot(a_hi, b_lo) + _dot(a_lo, b_hi))


RW_PRE_TM = 512
_MIX_OFF = (0, 512, 1024, 1536, 1664, 1792, 2048, 2176)


def _rwkv_pre_kernel(r_ref, k_ref, v_ref, xw_ref, xa_ref, xg_ref, xv_ref,
                     rp_ref, kp_ref, vp_ref, xwp_ref, xap_ref, xgp_ref, xvp_ref,
                     mix_ref, w0_ref, w2_ref, a0_ref, a2_ref, v0_ref, v2_ref, g2_ref, kk_ref, ka_ref,
                     gsum_ref, vfirst_ref,
                     r_out, lw_out, k_out, v_out, an_out, b_out, g_out, *, use_vres):
    i = pl.program_id(0)

    def mixed(cur_ref, prev_ref, seg):
        cur = cur_ref[...]
        mix = mix_ref[:, _MIX_OFF[seg]:_MIX_OFF[seg + 1]]
        prev_row = jnp.where(i > 0, prev_ref[7:8, :], 0.0)
        rid = lax.broadcasted_iota(jnp.int32, cur.shape, 0)
        shifted = jnp.where(rid == 0, prev_row, pltpu.roll(cur, 1, axis=0))
        return cur + mix * (shifted - cur)

    r = mixed(r_ref, rp_ref, 0)
    k = mixed(k_ref, kp_ref, 1)
    v = mixed(v_ref, vp_ref, 2)
    xw = mixed(xw_ref, xwp_ref, 3)
    xa = mixed(xa_ref, xap_ref, 4)
    xg = mixed(xg_ref, xgp_ref, 5)

    z = -(w0_ref[...] + _dot_f32(jnp.tanh(xw), w2_ref[...]))
    softplus = jnp.maximum(z, 0.0) + jnp.log(1.0 + jnp.exp(-jnp.abs(z)))
    w = -softplus - 0.5
    lw_out[...] = -jnp.exp(w)
    a = jax.nn.sigmoid(a0_ref[...] + _dot_f32(xa, a2_ref[...]))
    g_out[...] = _dot_f32(jax.nn.sigmoid(xg), g2_ref[...])
    if use_vres:
        xv = mixed(xv_ref, xvp_ref, 6)
        v = v + (vfirst_ref[...] - v) * jax.nn.sigmoid(v0_ref[...] + _dot_f32(xv, v2_ref[...]))
    kk = k * kk_ref[...]
    ss = _split_dot(kk * kk, gsum_ref[...], 2)
    kk = kk / jnp.maximum(jnp.sqrt(ss), 1e-12)
    r_out[...] = r
    k_out[...] = k * (1.0 + (a - 1.0) * ka_ref[...])
    v_out[...] = v
    an_out[...] = -kk
    b_out[...] = kk * a


def _rwkv_pre(proj, mix, w0, w2, a0, a2, v0, v2, g2, k_k, k_a, gsum, v_first, use_vres):
    t = proj.shape[0]
    tm = RW_PRE_TM

    def cur(width, off):
        return pl.BlockSpec((tm, width), lambda i: (i, off // width))

    def prev(width, off):
        return pl.BlockSpec((8, width), lambda i: (jnp.maximum(i * (tm // 8) - 1, 0), off // width))

    def full(a):
        return pl.BlockSpec(a.shape, lambda i: (0,) * a.ndim)

    segs = [(RW_W, C_RW_RKV), (RW_W, C_RW_RKV + RW_W), (RW_W, C_RW_RKV + 2 * RW_W),
            (128, C_RW_XW), (128, C_RW_XA), (256, C_RW_XG), (128, C_RW_XV)]
    params = [mix, w0, w2, a0, a2, v0, v2, g2, k_k, k_a, gsum]
    row = pl.BlockSpec((tm, RW_W), lambda i: (i, 0))
    outs = pl.pallas_call(
        functools.partial(_rwkv_pre_kernel, use_vres=use_vres),
        out_shape=[jax.ShapeDtypeStruct((t, RW_W), F32)] * 7,
        grid=(t // tm,),
        in_specs=[cur(w, o) for w, o in segs] + [prev(w, o) for w, o in segs] + [full(p) for p in params] + [row],
        out_specs=[row] * 7,
        compiler_params=_cparams(("parallel",)),
        name="rwkv_pre",
    )(*([proj] * 14), *params, v_first)
    return outs


RW_CHUNK = 128


def _wkv_kernel(r_ref, lw_ref, k_ref, v_ref, an_ref, b_ref, g_ref, lng_ref, lnb_ref, rk_ref,
                gsum_ref, tril_ref, o_ref, s_ref):
    c = pl.program_id(0)
    C = RW_CHUNK
    N = RW_N

    @pl.when(c == 0)
    def _():
        s_ref[...] = jnp.zeros_like(s_ref)

    r = r_ref[...]
    lw = lw_ref[...]
    k = k_ref[...]
    v = v_ref[...]
    lw_hi = lw.astype(BF16)
    rem = lw - lw_hi.astype(F32)
    lw_mid = rem.astype(BF16)
    lw_lo = (rem - lw_mid.astype(F32)).astype(BF16)
    tril = tril_ref[...]
    lg = _dot(tril, lw_hi) + (_dot(tril, lw_mid) + _dot(tril, lw_lo))
    gam = jnp.exp(lg)
    ginv = jnp.exp(-lg)
    at = (an_ref[...] * jnp.exp(lg - lw)).astype(BF16)
    bt = (b_ref[...] * ginv).astype(BF16)
    kt = (k * ginv).astype(BF16)
    rt_f = r * gam
    rt = rt_f.astype(BF16)
    vb = v.astype(BF16)
    g_last = gam[C - 1:C, :]

    rowi = lax.broadcasted_iota(jnp.int32, (C, C), 0)
    coli = lax.broadcasted_iota(jnp.int32, (C, C), 1)
    strict = rowi > coli
    incl = rowi >= coli

    heads = range(RW_HEADS)
    hs = [slice(h * N, (h + 1) * N) for h in heads]
    bth = [bt[:, s] for s in hs]
    kth = [kt[:, s] for s in hs]
    vh = [vb[:, s] for s in hs]
    big = [_dot_t(jnp.concatenate([at[:, s], rt[:, s]], axis=0), jnp.concatenate([bth[h], kth[h]], axis=0))
           for h, s in zip(heads, hs)]
    a_ab = [jnp.where(strict, big[h][:C, :C], 0.0) for h in heads]
    a_ak = [jnp.where(strict, big[h][:C, C:], 0.0).astype(BF16) for h in heads]
    a_rb = [jnp.where(incl, big[h][C:, :C], 0.0).astype(BF16) for h in heads]
    a_rk = [jnp.where(incl, big[h][C:, C:], 0.0).astype(BF16) for h in heads]
    akv = [_dot(a_ak[h], vh[h]) for h in heads]
    def same_block(size):
        shift = size.bit_length() - 1
        return (rowi >> shift) == (coli >> shift)

    tinv = [jnp.where(same_block(2), a_ab[h], 0.0) + jnp.where(rowi == coli, 1.0, 0.0) for h in heads]
    size = 2
    while size < C:
        lower_left = same_block(2 * size) & jnp.logical_not(same_block(size))
        off = [jnp.where(lower_left, a_ab[h], 0.0).astype(BF16) for h in heads]
        tb = [tinv[h].astype(BF16) for h in heads]
        half = [_dot(tb[h], off[h]).astype(BF16) for h in heads]
        tinv = [tinv[h] + _dot(half[h], tb[h]) for h in heads]
        size *= 2
    xb = [_dot(tinv[h].astype(BF16), jnp.concatenate([at[:, hs[h]], akv[h].astype(BF16)], axis=1)).astype(BF16)
          for h in heads]
    yx = [_dot(a_rb[h], xb[h]) for h in heads]
    ykv = [_dot(a_rk[h], vh[h]) for h in heads]
    xtb = [_dot_tl(xb[h], bth[h]) for h in heads]
    vtk = [_dot_tl(vh[h], kth[h]) for h in heads]
    ys = []
    for h in heads:
        gl = g_last[:, hs[h]]
        y1 = (rt_f[:, hs[h]] + yx[h][:, :N]).astype(BF16)
        s0 = s_ref[h]
        s0b = s0.astype(BF16)
        ys.append(_dot_t(y1, s0b) + (yx[h][:, N:] + ykv[h]))
        s_ref[h] = (s0 + _dot(s0b, xtb[h][:N].astype(BF16)) + (xtb[h][N:] + vtk[h])) * gl
    y = jnp.concatenate(ys, axis=1)

    gsum = gsum_ref[...]
    inv_n = 1.0 / N
    mu = _split_dot(y, gsum, 2) * inv_n
    yc = y - mu
    var = _split_dot(yc * yc, gsum, 2) * inv_n
    yn = yc * lax.rsqrt(var + RW_GN_EPS) * lng_ref[...] + lnb_ref[...]
    bonus = _split_dot(r * k * rk_ref[...], gsum, 2)
    o_ref[...] = ((yn + bonus * v) * g_ref[...]).astype(o_ref.dtype)


def _wkv(r, lw, k, v, an, b, g, ln_g, ln_b, r_k, gsum):
    t = r.shape[0]
    C = RW_CHUNK
    tril = jnp.asarray(np.tril(np.ones((C, C), np.float32)), BF16)
    row = pl.BlockSpec((C, RW_W), lambda c: (c, 0))
    vec = pl.BlockSpec((1, RW_W), lambda c: (0, 0))
    return pl.pallas_call(
        _wkv_kernel,
        out_shape=jax.ShapeDtypeStruct((t, RW_W), BF16),
        grid=(t // C,),
        in_specs=[row] * 7 + [vec] * 3 + [pl.BlockSpec((RW_W, RW_W), lambda c: (0, 0)),
                                          pl.BlockSpec((C, C), lambda c: (0, 0))],
        out_specs=row,
        scratch_shapes=[pltpu.VMEM((RW_HEADS, RW_N, RW_N), F32)],
        compiler_params=_cparams(("arbitrary",)),
        name="wkv7",
    )(r, lw, k, v, an, b, g, ln_g, ln_b, r_k, gsum, tril)


NSA_PREP_TM = 512


N_NORM_SLABS = (NSA_W + 2 * NSA_KV_W) // LANES
N_RAW_SLABS = 4 * NSA_KV_W // LANES
N_NSA_SLABS = N_NORM_SLABS + N_RAW_SLABS + 1


def _nsa_prep_kernel(*refs):
    x_refs = refs[:N_NSA_SLABS]
    (cos_ref, sin_ref, w_ref, gmean_ref, big_ref,
     qk_ref, hmat_ref, v1_ref, gate_ref, qpad_ref, ksx_ref) = refs[N_NSA_SLABS:]
    cos = cos_ref[...]
    sin = sin_ref[...]
    gmean = gmean_ref[...]
    half = NSA_DH // 2
    tm = cos.shape[0]
    lane = lax.broadcasted_iota(jnp.int32, cos.shape, 1)
    first_half = (lane % NSA_DH) < half
    low = lane < NSA_DH
    blk_row = (pl.program_id(0) * tm + lax.broadcasted_iota(jnp.int32, cos.shape, 0)) // SLC_BLOCK
    big = big_ref[...]
    mark_lo = jnp.where(blk_row == lane - NSA_DH, big, 0.0)
    mark_hi = jnp.where(low & (blk_row == lane + NSA_DH), big, 0.0).astype(BF16)
    n_q = NSA_W // LANES
    n_k = NSA_KV_W // LANES
    for s in range(N_NORM_SLABS):
        x = x_refs[s][...]
        y = x * lax.rsqrt(_split_dot(x * x, gmean, 2) + RMS_EPS) * w_ref[s]
        rot = jnp.where(first_half, pltpu.roll(y, LANES - half, axis=1), pltpu.roll(y, half, axis=1))
        res32 = y * cos + rot * sin
        res = res32.astype(qk_ref.dtype)
        qk_ref[2 * s] = res[:, :NSA_DH]
        qk_ref[2 * s + 1] = res[:, NSA_DH:]
        if s >= n_q + n_k:
            continue
        swapped = pltpu.roll(res32, NSA_DH, axis=1)
        if s < n_q:
            qpad_ref[2 * s] = jnp.where(low, res32, 0.0).astype(BF16)
            qpad_ref[2 * s + 1] = jnp.where(low, swapped, 0.0).astype(BF16)
        else:
            g0 = 2 * (s - n_q)
            ksx_ref[g0, :, :LANES] = jnp.where(low, res32, mark_lo).astype(BF16)
            ksx_ref[g0 + 1, :, :LANES] = jnp.where(low, swapped, mark_lo).astype(BF16)
            ksx_ref[g0, :, LANES:] = mark_hi
            ksx_ref[g0 + 1, :, LANES:] = mark_hi

    rows = hmat_ref.shape[2]
    low_r = lax.broadcasted_iota(jnp.int32, (rows, LANES), 1) < NSA_DH
    for s in range(NSA_KV_W // LANES * 2):
        kind, pair = divmod(s, NSA_KV_W // LANES)
        x_ref = x_refs[N_NORM_SLABS + s]
        for t2 in range(CMP_STRIDE // 2):
            r0 = x_ref[pl.ds(2 * t2, rows, stride=CMP_STRIDE), :]
            r1 = x_ref[pl.ds(2 * t2 + 1, rows, stride=CMP_STRIDE), :]
            cols = slice(t2 * LANES, (t2 + 1) * LANES)
            hmat_ref[kind, 2 * pair, :, cols] = jnp.where(low_r, r0, pltpu.roll(r1, NSA_DH, axis=1)).astype(BF16)
            hmat_ref[kind, 2 * pair + 1, :, cols] = jnp.where(low_r, pltpu.roll(r0, NSA_DH, axis=1), r1).astype(BF16)

    one_col = jnp.where(lane == NSA_DH, 1.0, 0.0)
    for s in range(NSA_KV_W // LANES * 2):
        x = x_refs[N_NORM_SLABS + N_RAW_SLABS // 2 + s][...]
        v1_ref[2 * s] = jnp.where(low, x, one_col).astype(BF16)
        v1_ref[2 * s + 1] = jnp.where(low, pltpu.roll(x, NSA_DH, axis=1), one_col).astype(BF16)

    xg = x_refs[N_NSA_SLABS - 1][...]
    per_group = 3 * NSA_HG
    for g in range(NSA_G):
        shifted = xg if g == 0 else pltpu.roll(xg, LANES - per_group * g, axis=1)
        gate_ref[g] = jnp.where(lane < per_group, shifted, 0.0)


def _nsa_prep(proj, w_slabs, cos, sin, gmean, big):
    t = proj.shape[0]
    assert t // SLC_BLOCK <= LANES
    tm = NSA_PREP_TM
    base = C_NSA_Q // LANES
    tab = pl.BlockSpec((tm, LANES), lambda i: (i, 0))
    n_heads_out = 2 * N_NORM_SLABS
    return pl.pallas_call(
        _nsa_prep_kernel,
        out_shape=[jax.ShapeDtypeStruct((n_heads_out, t, NSA_DH), BF16),
                   jax.ShapeDtypeStruct((2, NSA_G, t // CMP_STRIDE, CMP_STRIDE * NSA_DH), BF16),
                   jax.ShapeDtypeStruct((2 * NSA_G, t, LANES), BF16),
                   jax.ShapeDtypeStruct((NSA_G, t, LANES), F32),
                   jax.ShapeDtypeStruct((NSA_HEADS, t, LANES), BF16),
                   jax.ShapeDtypeStruct((NSA_G, t, 2 * LANES), BF16)],
        grid=(t // tm,),
        in_specs=[pl.BlockSpec((tm, LANES), lambda i, s=s: (i, base + s)) for s in range(N_NSA_SLABS)]
        + [tab, tab, pl.BlockSpec((N_NORM_SLABS, 1, LANES), lambda i: (0, 0, 0)),
           pl.BlockSpec((LANES, LANES), lambda i: (0, 0)), pl.BlockSpec((1, LANES), lambda i: (0, 0))],
        out_specs=[pl.BlockSpec((n_heads_out, tm, NSA_DH), lambda i: (0, i, 0)),
                   pl.BlockSpec((2, NSA_G, tm // CMP_STRIDE, CMP_STRIDE * NSA_DH), lambda i: (0, 0, i, 0)),
                   pl.BlockSpec((2 * NSA_G, tm, LANES), lambda i: (0, i, 0)),
                   pl.BlockSpec((NSA_G, tm, LANES), lambda i: (0, i, 0)),
                   pl.BlockSpec((NSA_HEADS, tm, LANES), lambda i: (0, i, 0)),
                   pl.BlockSpec((NSA_G, tm, 2 * LANES), lambda i: (0, i, 0))],
        compiler_params=_cparams(("parallel",)),
        name="nsa_prep",
    )(*([proj] * N_NSA_SLABS), cos, sin, w_slabs, gmean, big)


def _compress_kernel(h_ref, w1_ref, w2_ref, pe_ref, nw_ref, cos_ref, sin_ref, o_ref):
    kind = pl.program_id(0)
    nc = h_ref.shape[2]
    half_in = CMP_STRIDE * NSA_DH
    hm = h_ref[0, 0]
    w1 = w1_ref[0]
    first = _dot(hm, w1[:half_in])
    second = _dot(hm, w1[half_in:])
    const = _dot(pe_ref[0], w1)[0:1, :]
    pre = first + pltpu.roll(second, nc - 1, axis=0) + const
    hid = 0.5 * pre * (1.0 + jnp.tanh(0.7978845608028654 * (pre + 0.044715 * pre * pre * pre)))
    out = _dot(hid.astype(BF16), w2_ref[0])
    y = out * lax.rsqrt(jnp.mean(out * out, axis=-1, keepdims=True) + RMS_EPS) * nw_ref[...]
    half = NSA_DH // 2
    rot = jnp.concatenate([y[:, half:], y[:, :half]], axis=1)
    roped = y * cos_ref[...] + rot * sin_ref[...]
    o_ref[0, 0] = jnp.where(kind == 0, roped, out).astype(o_ref.dtype)


def _compress(hmat, w1, w2, pe, nw, cos_c, sin_c):
    _, g, nc, width = hmat.shape
    return pl.pallas_call(
        _compress_kernel,
        out_shape=jax.ShapeDtypeStruct((2, g, nc, NSA_DH), BF16),
        grid=(2, g),
        in_specs=[pl.BlockSpec((1, 1, nc, width), lambda a, b: (a, b, 0, 0)),
                  pl.BlockSpec((1, 2 * width, CMP_HIDDEN), lambda a, b: (a, 0, 0)),
                  pl.BlockSpec((1, CMP_HIDDEN, NSA_DH), lambda a, b: (a, 0, 0)),
                  pl.BlockSpec((1, 8, 2 * width), lambda a, b: (a, 0, 0)),
                  pl.BlockSpec((1, NSA_DH), lambda a, b: (0, 0)),
                  pl.BlockSpec((nc, NSA_DH), lambda a, b: (0, 0)),
                  pl.BlockSpec((nc, NSA_DH), lambda a, b: (0, 0))],
        out_specs=pl.BlockSpec((1, 1, nc, NSA_DH), lambda a, b: (a, b, 0, 0)),
        compiler_params=_cparams(("parallel", "parallel")),
        name="nsa_compress",
    )(hmat, w1, w2, pe, nw, cos_c, sin_c)


SLC_KT = 1024
WIN_ROWS = 128
WIN_KEYS = WINDOW + WIN_ROWS


INT32_MIN = -2 ** 31


def _nsa_attn_kernel(q_ref, qpad_ref, gate_ref, kc_ref, vc_ref, ks_ref, vs_ref, kw_ref, vw_ref, cis_ref, ltri_ref,
                     o_ref):
    qi = pl.program_id(1)
    nc = kc_ref.shape[2]
    nb = cis_ref.shape[1]
    heads = range(NSA_HG)
    start = qi * NSA_TQ
    qh = [q_ref[hh] for hh in heads]
    tok = start + lax.broadcasted_iota(jnp.int32, (NSA_TQ, 1), 0)

    def softmax_terms(scores):
        return [jnp.exp(s - jnp.max(s, axis=-1, keepdims=True)) for s in scores]

    def cmp_branch(width):
        kc = kc_ref[0, 0, :width, :]
        vc = vc_ref[0, 0, :width, :]
        cend = lax.broadcasted_iota(jnp.int32, (1, width), 1) * CMP_STRIDE + (CMP_BLOCK - 1)
        bias_c = jnp.where(cend <= tok, 0.0, NEG_INF)
        ec = softmax_terms([_dot_t(qh[hh], kc) + bias_c for hh in heads])
        sees_any = tok >= CMP_BLOCK - 1
        pc = [ec[hh] * jnp.where(sees_any, 1.0 / jnp.sum(ec[hh], axis=-1, keepdims=True), 0.0) for hh in heads]
        outs_c = tuple(_dot(pc[hh].astype(BF16), vc) for hh in heads)
        psum = pc[0]
        for hh in range(1, NSA_HG):
            psum = psum + pc[hh]
        return outs_c, _split_dot(psum, cis_ref[:width, :], 2)

    n_quarters = 4
    quarter = nc // n_quarters
    n_visible = (start + NSA_TQ - CMP_BLOCK) // CMP_STRIDE + 1
    which = jnp.clip((n_visible - 1) // quarter, 0, n_quarters - 1)
    o_c, score = lax.switch(which, [functools.partial(cmp_branch, (r + 1) * quarter) for r in range(n_quarters)])
    blk = lax.broadcasted_iota(jnp.int32, (1, nb), 1)
    cur = tok // SLC_BLOCK
    forced = (blk == 0) | (blk == cur) | (blk == cur - 1)
    score = jnp.where(forced, FORCED_SCORE, score)
    score = jnp.where(blk <= cur, score, -jnp.inf)


    bits = lax.bitcast_convert_type(score.T, jnp.int32)
    key_all = bits ^ ((bits >> 31) & 0x7FFFFFFF)

    def select_blocks(nbv):
        key = key_all[:nbv]
        n_sel = min(SLC_TOPK, nbv)

        def enough(c):
            return jnp.sum(jnp.where(key >= c, 1.0, 0.0), axis=0, keepdims=True) >= n_sel

        def enough3(c1, c2, c3):
            packed = jnp.where(key >= c3, 65793.0, jnp.where(key >= c2, 257.0, jnp.where(key >= c1, 1.0, 0.0)))
            tot = jnp.sum(packed, axis=0, keepdims=True).astype(jnp.int32)
            return (tot & 255) >= n_sel, ((tot >> 8) & 255) >= n_sel, (tot >> 16) >= n_sel

        zero_row = jnp.zeros((1, NSA_TQ), jnp.int32)
        thr = jnp.where(enough(zero_row), zero_row, INT32_MIN)
        for hi in range(30, 0, -2):
            c1 = thr + (1 << (hi - 1))
            c2 = thr + (1 << hi)
            c3 = c2 + (1 << (hi - 1))
            e1, e2, e3 = enough3(c1, c2, c3)
            thr = jnp.where(e3, c3, jnp.where(e2, c2, jnp.where(e1, c1, thr)))
        c1 = thr + 1
        thr = jnp.where(enough(c1), c1, thr)
        above = key > thr
        tied = key == thr
        need = n_sel - jnp.sum(jnp.where(above, 1.0, 0.0), axis=0, keepdims=True)
        tied_before = _dot(ltri_ref[:nbv, :nbv], jnp.where(tied, 1.0, 0.0).astype(BF16))
        chosen = jnp.where(above | (tied & (tied_before < need)), 1.0, 0.0)
        if nbv < LANES:
            chosen = jnp.concatenate([chosen, jnp.zeros((LANES - nbv, NSA_TQ), F32)], axis=0)
        return chosen.T

    assert nb <= LANES
    n_steps = 4
    blocks_visible = (start + NSA_TQ - 1) // SLC_BLOCK + 1
    widths = sorted({max(-(-nb * (r + 1) // n_steps // 8) * 8, min(nb, SLC_TOPK)) for r in range(n_steps)})
    which_sel = sum((blocks_visible > w).astype(jnp.int32) for w in widths[:-1])
    sel = lax.switch(which_sel, [functools.partial(select_blocks, w) for w in widths])
    lane_q = lax.broadcasted_iota(jnp.int32, (NSA_TQ, LANES), 1)
    sel_sw = pltpu.roll(sel, NSA_DH, axis=1)
    sel_lo = jnp.where(lane_q >= NSA_DH, sel_sw, 0.0)
    sel_hi = jnp.where(lane_q < NSA_DH, sel_sw, 0.0)
    qx = [jnp.concatenate([qpad_ref[hh].astype(F32) + sel_lo, sel_hi], axis=1).astype(BF16) for hh in heads]

    def window_rows(r0):
        kbase = pl.multiple_of(jnp.maximum(start + r0 - WINDOW, 0), WIN_ROWS)
        kw = kw_ref[0, pl.ds(kbase, WIN_KEYS), :]
        vw = vw_ref[0, pl.ds(kbase, WIN_KEYS), :]
        lag = tok[r0:r0 + WIN_ROWS] - (kbase + lax.broadcasted_iota(jnp.int32, (1, WIN_KEYS), 1))
        bias_w = jnp.where((lag >= 0) & (lag < WINDOW), 0.0, NEG_INF)
        sw = [_dot_t(qh[hh][r0:r0 + WIN_ROWS], kw) + bias_w for hh in heads]
        ew = [jnp.exp((s - jnp.max(s, axis=-1, keepdims=True)).astype(BF16)) for s in sw]
        return [_dot(ew[hh], vw) for hh in heads]

    win_parts = [window_rows(r0) for r0 in range(0, NSA_TQ, WIN_ROWS)]
    acc_w = [jnp.concatenate([part[hh] for part in win_parts], axis=0) for hh in heads]

    n_tiles = (start + NSA_TQ + SLC_KT - 1) // SLC_KT

    def sel_tile(j, carry, causal, width=SLC_KT):
        ms, accs = carry
        k0 = pl.multiple_of(j * SLC_KT, SLC_KT)
        kt = ks_ref[0, pl.ds(k0, width), :]
        vt = vs_ref[0, pl.ds(k0, width), :]
        sj = [_dot_t(qx[hh], kt) for hh in heads]
        if causal:
            future = (k0 + lax.broadcasted_iota(jnp.int32, (1, width), 1)) > tok
            sj = [jnp.where(future, NEG_INF, s) for s in sj]
        m_new = [jnp.maximum(ms[hh], jnp.max(sj[hh], axis=-1, keepdims=True)) for hh in heads]
        pj = [jnp.exp((sj[hh] - m_new[hh]).astype(BF16)) for hh in heads]
        accs = [jnp.exp(ms[hh] - m_new[hh]) * accs[hh] + _dot(pj[hh], vt) for hh in heads]
        return tuple(m_new), tuple(accs)

    m0 = tuple(jnp.full((NSA_TQ, 1), NEG_INF, F32) for _ in heads)
    a0 = tuple(jnp.zeros((NSA_TQ, LANES), F32) for _ in heads)
    carry = lax.fori_loop(0, n_tiles - 1, functools.partial(sel_tile, causal=False), (m0, a0))
    per_tile = SLC_KT // NSA_TQ
    last = [functools.partial(sel_tile, n_tiles - 1, causal=True, width=(r + 1) * NSA_TQ) for r in range(per_tile)]
    _, acc_s = lax.switch(qi % per_tile, last, carry)

    gt = jax.nn.sigmoid(gate_ref[0])
    outs = []
    for hh in heads:
        g0, g1, g2 = (gt[:, 3 * hh + br:3 * hh + br + 1] for br in range(3))
        scale_s = g1 / acc_s[hh][:, NSA_DH:NSA_DH + 1]
        scale_w = g2 / acc_w[hh][:, NSA_DH:NSA_DH + 1]
        outs.append(g0 * o_c[hh] + scale_s * acc_s[hh][:, :NSA_DH] + scale_w * acc_w[hh][:, :NSA_DH])
    o_ref[...] = jnp.concatenate(outs, axis=1).astype(o_ref.dtype)


def _nsa_attn(qk_hm, qpad, ksx, gates, cmp_kv, v1, cis, ltri):
    t = qk_hm.shape[1]
    nq = t // NSA_TQ
    nc = cmp_kv.shape[2]
    nb = cis.shape[1]
    ks_spec = pl.BlockSpec((1, t, 2 * LANES), lambda g, i: (g, 0, 0))
    kw_spec = pl.BlockSpec((1, t, NSA_DH), lambda g, i: (NSA_HEADS + NSA_G + g, 0, 0))
    vs_spec = pl.BlockSpec((1, t, LANES), lambda g, i: (g, 0, 0))
    vw_spec = pl.BlockSpec((1, t, LANES), lambda g, i: (NSA_G + g, 0, 0))
    return pl.pallas_call(
        _nsa_attn_kernel,
        out_shape=jax.ShapeDtypeStruct((t, NSA_W), BF16),
        grid=(NSA_G, nq),
        in_specs=[pl.BlockSpec((NSA_HG, NSA_TQ, NSA_DH), lambda g, i: (g, i, 0)),
                  pl.BlockSpec((NSA_HG, NSA_TQ, LANES), lambda g, i: (g, i, 0)),
                  pl.BlockSpec((1, NSA_TQ, LANES), lambda g, i: (g, i, 0)),
                  pl.BlockSpec((1, 1, nc, NSA_DH), lambda g, i: (0, g, 0, 0)),
                  pl.BlockSpec((1, 1, nc, NSA_DH), lambda g, i: (1, g, 0, 0)),
                  ks_spec, vs_spec, kw_spec, vw_spec,
                  pl.BlockSpec((nc, nb), lambda g, i: (0, 0)),
                  pl.BlockSpec((nb, nb), lambda g, i: (0, 0))],
        out_specs=pl.BlockSpec((NSA_TQ, NSA_HG * NSA_DH), lambda g, i: (i, g)),
        compiler_params=_cparams(("parallel", "arbitrary")),
        name="nsa_attention",
    )(qk_hm, qpad, gates, cmp_kv, cmp_kv, ksx, v1, qk_hm, v1, cis, ltri)


def _rope_tables(pos, dh, reps):
    half = dh // 2
    inv_freq = ROPE_THETA ** (-np.arange(half, dtype=np.float64) / half)
    ang = np.asarray(pos, np.float64)[:, None] * inv_freq[None, :]
    cos = np.cos(ang)
    sin = np.sin(ang)
    cos_t = np.tile(np.concatenate([cos, cos], axis=1), (1, reps))
    sin_t = np.tile(np.concatenate([-sin, sin], axis=1), (1, reps))
    return jnp.asarray(cos_t, F32), jnp.asarray(sin_t, F32)


def _pad_cols(a, width):
    return jnp.pad(a, ((0, 0), (0, width - a.shape[1])))


def _pad_rows(a, height):
    return jnp.pad(a, ((0, height - a.shape[0]), (0, 0)))


def _pack_w_in_kernel(w_ref, xv_ref, o_ref):
    o_ref[...] = _pack_w_in(w_ref[...], xv_ref[...]).astype(o_ref.dtype)


def _pack_w_in_call(w_in, layer, xv):
    _, d, n_in = w_in.shape
    tk = 256
    return pl.pallas_call(
        _pack_w_in_kernel,
        out_shape=jax.ShapeDtypeStruct((d, PROJ_W), BF16),
        grid=(d // tk,),
        in_specs=[pl.BlockSpec((None, tk, n_in), lambda i: (layer, i, 0)),
                  pl.BlockSpec((tk, LANES), lambda i: (i, 0))],
        out_specs=pl.BlockSpec((tk, PROJ_W), lambda i: (i, 0)),
        compiler_params=_cparams(("parallel",)),
        name="pack_w_in",
    )(w_in, xv)


def _pack_w_in(w_l, xv):
    d = w_l.shape[0]
    o = 0
    ret = w_l[:, o:o + 4 * RET_W]; o += 4 * RET_W
    rkv = w_l[:, o:o + 3 * RW_W]; o += 3 * RW_W
    xw = w_l[:, o:o + RW_DECAY_RANK]; o += RW_DECAY_RANK
    xa = w_l[:, o:o + RW_A_RANK]; o += RW_A_RANK
    xg = w_l[:, o:o + RW_GATE_RANK]; o += RW_GATE_RANK
    q = w_l[:, o:o + NSA_W]; o += NSA_W
    kc, vc, ks, vs, kw, vw = (w_l[:, o + i * NSA_KV_W:o + (i + 1) * NSA_KV_W] for i in range(6))
    o += 6 * NSA_KV_W
    gates = w_l[:, o:o + 3 * NSA_HEADS]
    return jnp.concatenate([ret, rkv, _pad_cols(xw, LANES), _pad_cols(xa, LANES), xg, xv,
                            q, ks, kw, kc, vc, vs, vw, _pad_cols(gates, LANES)], axis=1)


def _pack_mix(mix, vres_mix):
    o = 3 * RW_W
    xw = mix[o:o + RW_DECAY_RANK]; o += RW_DECAY_RANK
    xa = mix[o:o + RW_A_RANK]; o += RW_A_RANK
    xg = mix[o:o + RW_GATE_RANK]
    z = lambda n: jnp.zeros((n,), mix.dtype)
    xv = z(LANES) if vres_mix is None else jnp.concatenate([vres_mix, z(LANES - RW_V_RANK)])
    return jnp.concatenate([mix[:3 * RW_W], xw, z(LANES - RW_DECAY_RANK), xa, z(LANES - RW_A_RANK), xg, xv])[None, :]


def kernel(x, ln1_g, w_in, w_in_vres, rwkv_mix, rwkv_vres_mix, rwkv_w0, rwkv_w2, rwkv_a0, rwkv_a2, rwkv_v0, rwkv_v2, rwkv_g2, rwkv_k_k, rwkv_k_a, rwkv_r_k, rwkv_ln_g, rwkv_ln_b, nsa_q_norm, nsa_k_norm, nsa_cmp_pe, nsa_cmp_k_w1, nsa_cmp_k_w2, nsa_cmp_v_w1, nsa_cmp_v_w2, w_out, ln2_g, w_up, w_down):
    bsz, t, d = x.shape
    assert bsz == 1
    depth = w_in.shape[0]
    nc = t // CMP_STRIDE
    nb = t // SLC_BLOCK
    xs = x.reshape(t, d)

    pos = np.arange(t)
    cos_r, sin_r = _rope_tables(pos, RET_DH, 1)
    cos_n, sin_n = _rope_tables(pos, NSA_DH, 2)
    cos_c, sin_c = _rope_tables(np.arange(nc) * CMP_STRIDE + (CMP_BLOCK - 1), NSA_DH, 1)
    lane_head = np.arange(RW_W) // RW_N
    gsum = jnp.asarray(lane_head[:, None] == lane_head[None, :], BF16)
    lane_h2 = np.arange(LANES) // NSA_DH
    gmean = jnp.asarray((lane_h2[:, None] == lane_h2[None, :]) / float(NSA_DH), BF16)
    cstart = np.arange(nc) * CMP_STRIDE
    sstart = np.arange(nb) * SLC_BLOCK
    cis = jnp.asarray((cstart[:, None] <= sstart[None, :] + SLC_BLOCK - 1)
                      & (cstart[:, None] + CMP_BLOCK - 1 >= sstart[None, :]), BF16)

    ltri = jnp.asarray(np.tril(np.ones((nb, nb), np.float32), -1), BF16)

    v_first = jnp.zeros((t, RW_W), F32)
    for l in range(depth):
        vres = l > 0
        xv_w = _pad_cols(w_in_vres[l - 1], LANES) if vres else jnp.zeros((d, LANES), F32)
        w_cat = _pack_w_in_call(w_in, l, xv_w)
        proj = _norm_matmul(xs, ln1_g[l], w_cat, name="in_proj")

        o_ret = _retention(proj, cos_r, sin_r)

        row = lambda a: a.reshape(1, -1)
        mix = _pack_mix(rwkv_mix[l], rwkv_vres_mix[l - 1] if vres else None)
        v0 = row(rwkv_v0[l - 1]) if vres else jnp.zeros((1, RW_W), F32)
        v2 = _pad_rows(rwkv_v2[l - 1], LANES) if vres else jnp.zeros((LANES, RW_W), F32)
        r_, lw_, k_, v_, an_, b_, g_ = _rwkv_pre(
            proj, mix, row(rwkv_w0[l]), _pad_rows(rwkv_w2[l], LANES), row(rwkv_a0[l]), _pad_rows(rwkv_a2[l], LANES),
            v0, v2, rwkv_g2[l], row(rwkv_k_k[l]), row(rwkv_k_a[l]), gsum, v_first, vres)
        if not vres:
            v_first = v_
        o_rwkv = _wkv(r_, lw_, k_, v_, an_, b_, g_, row(rwkv_ln_g[l]), row(rwkv_ln_b[l]), row(rwkv_r_k[l]), gsum)

        n_q = NSA_W // LANES
        n_k = NSA_KV_W // LANES
        w_slabs = jnp.stack([jnp.tile(nsa_q_norm[l] * (NSA_DH ** -0.5), 2)] * n_q
                            + [jnp.tile(nsa_k_norm[l, 1], 2)] * n_k + [jnp.tile(nsa_k_norm[l, 2], 2)] * n_k)[:, None, :]
        s_max = 1.02 * jnp.max(jnp.abs(nsa_q_norm[l])) * jnp.max(jnp.abs(nsa_k_norm[l, 1])) * (NSA_DH ** 0.5)
        big = jnp.full((1, LANES), jnp.exp2(jnp.ceil(jnp.log2(2.0 * s_max + 128.0))), F32)
        qk_hm, hmat, v1, gates, qpad, ksx = _nsa_prep(proj, w_slabs, cos_n, sin_n, gmean, big)
        w1 = jnp.stack([nsa_cmp_k_w1[l], nsa_cmp_v_w1[l]]).astype(BF16)
        w2 = jnp.stack([nsa_cmp_k_w2[l], nsa_cmp_v_w2[l]]).astype(BF16)
        pe = jnp.broadcast_to(nsa_cmp_pe[l].reshape(2, 1, CMP_BLOCK * NSA_DH), (2, 8, CMP_BLOCK * NSA_DH)).astype(BF16)
        cmp_kv = _compress(hmat, w1, w2, pe, row(nsa_k_norm[l, 0]), cos_c, sin_c)
        o_nsa = _nsa_attn(qk_hm, qpad, ksx, gates, cmp_kv, v1, cis, ltri)

        xs = _out_proj(o_ret, o_rwkv, o_nsa, w_out, l, xs)

        up = _norm_matmul(xs, ln2_g[l], w_up, layer=l, act="relu2", out_dtype=BF16, name="mlp_up")
        xs = _matmul(up, w_down, layer=l, res=xs, name="mlp_down")
    return xs.reshape(bsz, t, d)
```

```python
import functools

import numpy as np
import jax
import jax.numpy as jnp
from jax import lax
from jax.experimental import pallas as pl
from jax.experimental.pallas import tpu as pltpu

F32 = jnp.float32
BF16 = jnp.bfloat16

RET_HEADS, RET_DH = 4, 128
RET_W = RET_HEADS * RET_DH
RW_HEADS, RW_N = 8, 64
RW_W = RW_HEADS * RW_N
RW_DECAY_RANK, RW_A_RANK, RW_V_RANK, RW_GATE_RANK = 96, 96, 64, 256
RW_GN_EPS = 64e-5
NSA_HEADS, NSA_G, NSA_DH = 16, 4, 64
NSA_HG = NSA_HEADS // NSA_G
NSA_W = NSA_HEADS * NSA_DH
NSA_KV_W = NSA_G * NSA_DH
CMP_BLOCK, CMP_STRIDE, CMP_HIDDEN = 32, 16, 256
SLC_BLOCK, SLC_TOPK, WINDOW = 64, 16, 512
NSA_TQ = 512
ROPE_THETA = 10000.0
RMS_EPS = 1e-6
NEG_INF = -1e30
FORCED_SCORE = 1e9

LANES = 128
VMEM_LIMIT = 56 * 1024 * 1024

C_RET = 0
C_RW_RKV = 2048
C_RW_XW = 3584
C_RW_XA = 3712
C_RW_XG = 3840
C_RW_XV = 4096
C_NSA_Q = 4224
PROJ_W = 6912


def _cparams(sem):
    return pltpu.CompilerParams(dimension_semantics=sem, vmem_limit_bytes=VMEM_LIMIT)


def _dot(a, b):
    return jnp.dot(a, b, preferred_element_type=F32)


def _dot_t(a, b):
    return lax.dot_general(a, b, (((1,), (1,)), ((), ())), preferred_element_type=F32)


def _dot_tl(a, b):
    return lax.dot_general(a, b, (((0,), (0,)), ((), ())), preferred_element_type=F32)


def _split_dot(x, m_bf16, passes):
    hi = x.astype(BF16)
    acc = _dot(hi, m_bf16)
    rem = x - hi.astype(F32)
    for _ in range(passes - 1):
        piece = rem.astype(BF16)
        acc = acc + _dot(piece, m_bf16)
        rem = rem - piece.astype(F32)
    return acc


def _mm_kernel(*refs, nk, act, has_res):
    a_ref, b_ref = refs[:2]
    r_ref = refs[2] if has_res else None
    o_ref = refs[3] if has_res else refs[2]

    def finish(y):
        if act == "relu2":
            y = jnp.square(jnp.maximum(y, 0.0))
        if has_res:
            y = y + r_ref[...]
        o_ref[...] = y.astype(o_ref.dtype)

    part = _dot(a_ref[...], b_ref[...].astype(BF16))
    if nk == 1:
        finish(part)
        return
    k = pl.program_id(2)
    if act is None and o_ref.dtype == F32:
        @pl.when(k == 0)
        def _():
            o_ref[...] = part + r_ref[...] if has_res else part

        @pl.when(k > 0)
        def _():
            o_ref[...] += part

        return
    acc_ref = refs[-1]

    @pl.when(k == 0)
    def _():
        acc_ref[...] = part

    @pl.when(k > 0)
    def _():
        acc_ref[...] += part

    @pl.when(k == nk - 1)
    def _():
        finish(acc_ref[...])


def _pick(n, cands):
    for c in cands:
        if n % c == 0:
            return c
    raise ValueError(f"no tile for {n}")


def _weight_spec(b, layer, block, index_map):
    if layer is None:
        return pl.BlockSpec(block, index_map)
    return pl.BlockSpec((None,) + block, lambda *idx: (layer,) + index_map(*idx))


def _matmul(a, b, *, layer=None, res=None, act=None, out_dtype=F32, name="matmul"):
    m, kd = a.shape
    n = b.shape[-1]
    tm = _pick(m, (1024, 512, 256, 128))
    tn = _pick(n, (1024, 768, 512, 256, 128))
    tk = _pick(kd, (2048, 1024, 512))
    nk = kd // tk
    in_specs = [pl.BlockSpec((tm, tk), lambda i, j, k: (i, k)),
                _weight_spec(b, layer, (tk, tn), lambda i, j, k: (k, j))]
    args = [a, b]
    if res is not None:
        in_specs.append(pl.BlockSpec((tm, tn), lambda i, j, k: (i, j)))
        args.append(res)
    return pl.pallas_call(
        functools.partial(_mm_kernel, nk=nk, act=act, has_res=res is not None),
        out_shape=jax.ShapeDtypeStruct((m, n), out_dtype),
        grid=(m // tm, n // tn, nk),
        in_specs=in_specs,
        out_specs=pl.BlockSpec((tm, tn), lambda i, j, k: (i, j)),
        scratch_shapes=[pltpu.VMEM((tm, tn), F32)] if nk > 1 and (act is not None or out_dtype != F32) else [],
        compiler_params=_cparams(("parallel", "parallel", "arbitrary")),
        name=name,
    )(*args)


def _norm_mm_kernel(x_ref, g_ref, b_ref, o_ref, h_ref, *, act):
    @pl.when(pl.program_id(1) == 0)
    def _():
        x = x_ref[...]
        ms = jnp.mean(x * x, axis=-1, keepdims=True)
        h_ref[...] = (x * lax.rsqrt(ms + RMS_EPS) * g_ref[...]).astype(BF16)

    y = _dot(h_ref[...], b_ref[...].astype(BF16))
    if act == "relu2":
        y = jnp.square(jnp.maximum(y, 0.0))
    o_ref[...] = y.astype(o_ref.dtype)


def _norm_matmul(x, g, b, *, layer=None, act=None, out_dtype=F32, name="norm_matmul"):
    m, kd = x.shape
    n = b.shape[-1]
    tm = _pick(m, (1024, 512, 256, 128))
    tn = _pick(n, (1024, 768, 512, 256, 128))
    return pl.pallas_call(
        functools.partial(_norm_mm_kernel, act=act),
        out_shape=jax.ShapeDtypeStruct((m, n), out_dtype),
        grid=(m // tm, n // tn),
        in_specs=[pl.BlockSpec((tm, kd), lambda i, j: (i, 0)),
                  pl.BlockSpec((1, kd), lambda i, j: (0, 0)),
                  _weight_spec(b, layer, (kd, tn), lambda i, j: (0, j))],
        out_specs=pl.BlockSpec((tm, tn), lambda i, j: (i, j)),
        scratch_shapes=[pltpu.VMEM((tm, kd), BF16)],
        compiler_params=_cparams(("parallel", "arbitrary")),
        name=name,
    )(x, g.reshape(1, kd), b)


def _out_proj_kernel(ret_ref, rw_ref, nsa_ref, w_ref, res_ref, o_ref, wb_ref):
    @pl.when(pl.program_id(1) == 0)
    def _():
        wb_ref[...] = w_ref[...].astype(BF16)

    k0 = ret_ref.shape[1]
    k1 = k0 + rw_ref.shape[1]
    acc = _dot(ret_ref[...], wb_ref[:k0, :])
    acc = acc + _dot(rw_ref[...], wb_ref[k0:k1, :])
    acc = acc + _dot(nsa_ref[...], wb_ref[k1:, :])
    o_ref[...] = acc + res_ref[...]


def _out_proj(o_ret, o_rwkv, o_nsa, w, layer, res):
    m = o_ret.shape[0]
    kd, n = w.shape[-2:]
    assert kd == o_ret.shape[1] + o_rwkv.shape[1] + o_nsa.shape[1]
    tm = _pick(m, (1024, 512, 256, 128))
    tn = _pick(n, (1024, 512, 256, 128))

    def rows(a):
        return pl.BlockSpec((tm, a.shape[1]), lambda j, i: (i, 0))

    return pl.pallas_call(
        _out_proj_kernel,
        out_shape=jax.ShapeDtypeStruct((m, n), F32),
        grid=(n // tn, m // tm),
        in_specs=[rows(o_ret), rows(o_rwkv), rows(o_nsa),
                  _weight_spec(w, layer, (kd, tn), lambda j, i: (0, j)),
                  pl.BlockSpec((tm, tn), lambda j, i: (i, j))],
        out_specs=pl.BlockSpec((tm, tn), lambda j, i: (i, j)),
        scratch_shapes=[pltpu.VMEM((kd, tn), BF16)],
        compiler_params=_cparams(("parallel", "arbitrary")),
        name="out_proj",
    )(o_ret, o_rwkv, o_nsa, w, res)


RET_CHUNK = 512


def _retention_kernel(q_ref, k_ref, v_ref, g_ref, cos_ref, sin_ref, dmat_ref, qd_ref, kd_ref, cd_ref, o_ref, state_ref):
    @pl.when(pl.program_id(0) == 0)
    def _():
        state_ref[...] = jnp.zeros_like(state_ref)

    cos = cos_ref[...]
    sin = sin_ref[...]

    def rope(x):
        return x * cos + pltpu.roll(x, RET_DH // 2, axis=1) * sin

    heads = range(RET_HEADS)
    hs = [slice(h * RET_DH, (h + 1) * RET_DH) for h in heads]
    q_all, k_all, v_all, gate = q_ref[...], k_ref[...], v_ref[...], g_ref[...]
    qb = [rope(q_all[:, s]).astype(BF16) for s in hs]
    kf = [rope(k_all[:, s]) * (RET_DH ** -0.5) for s in hs]
    kb = [kf[h].astype(BF16) for h in heads]
    vb = [v_all[:, s].astype(BF16) for s in hs]
    inner = [(_dot_t(qb[h], kb[h]) * dmat_ref[h]).astype(BF16) for h in heads]
    cross = [_dot(qb[h], state_ref[h].astype(BF16)) * qd_ref[h] for h in heads]
    kv = [_dot_tl((kf[h] * kd_ref[h]).astype(BF16), vb[h]) for h in heads]
    outs = []
    for h in heads:
        out = _dot(inner[h], vb[h]) + cross[h]
        state_ref[h] = cd_ref[h] * state_ref[h] + kv[h]
        outs.append(out * lax.rsqrt(jnp.mean(out * out, axis=-1, keepdims=True) + RMS_EPS))
    o_ref[...] = (gate * jax.nn.sigmoid(gate) * jnp.concatenate(outs, axis=1)).astype(o_ref.dtype)


def _retention(proj, cos, sin):
    t = proj.shape[0]
    chunk = RET_CHUNK
    log_gamma = np.log(1.0 - 2.0 ** (-5.0 - np.arange(RET_HEADS, dtype=np.float64)))[:, None, None]
    n = np.arange(chunk, dtype=np.float64)
    lag = n[:, None] - n[None, :]
    dmat = np.where(lag >= 0, np.exp(np.maximum(lag, 0.0)[None] * log_gamma), 0.0)
    ones = np.ones((1, 1, RET_DH))
    qd = np.exp((n + 1.0)[None, :, None] * log_gamma) * ones
    kd = np.exp((chunk - 1.0 - n)[None, :, None] * log_gamma) * ones
    cd = np.exp(chunk * log_gamma) * ones
    tables = [jnp.asarray(a, F32) for a in (dmat, qd, kd, cd)]
    base = C_RET // RET_W

    def col(off):
        return pl.BlockSpec((chunk, RET_W), lambda c, off=off: (c, base + off))

    def full(a):
        return pl.BlockSpec(a.shape, lambda c: (0, 0, 0))

    tab = pl.BlockSpec((chunk, RET_DH), lambda c: (c, 0))
    return pl.pallas_call(
        _retention_kernel,
        out_shape=jax.ShapeDtypeStruct((t, RET_W), BF16),
        grid=(t // chunk,),
        in_specs=[col(0), col(1), col(2), col(3), tab, tab] + [full(a) for a in tables],
        out_specs=pl.BlockSpec((chunk, RET_W), lambda c: (c, 0)),
        scratch_shapes=[pltpu.VMEM((RET_HEADS, RET_DH, RET_DH), F32)],
        compiler_params=_cparams(("arbitrary",)),
        name="retention",
    )(proj, proj, proj, proj, cos, sin, *tables)


def _dot_f32(a, b):
    a_hi = a.astype(BF16)
    b_hi = b.astype(BF16)
    a_lo = (a - a_hi.astype(F32)).astype(BF16)
    b_lo = (b - b_hi.astype(F32)).astype(BF16)
    return _dot(a_hi, b_hi) + (_dot(a_hi, b_lo) + _dot(a_lo, b_hi))


RW_PRE_TM = 512
_MIX_OFF = (0, 512, 1024, 1536, 1664, 1792, 2048, 2176)


def _rwkv_pre_kernel(r_ref, k_ref, v_ref, xw_ref, xa_ref, xg_ref, xv_ref,
                     rp_ref, kp_ref, vp_ref, xwp_ref, xap_ref, xgp_ref, xvp_ref,
                     mix_ref, w0_ref, w2_ref, a0_ref, a2_ref, v0_ref, v2_ref, g2_ref, kk_ref, ka_ref,
                     gsum_ref, vfirst_ref,
                     r_out, lw_out, k_out, v_out, an_out, b_out, g_out, *, use_vres):
    i = pl.program_id(0)

    def mixed(cur_ref, prev_ref, seg):
        cur = cur_ref[...]
        mix = mix_ref[:, _MIX_OFF[seg]:_MIX_OFF[seg + 1]]
        prev_row = jnp.where(i > 0, prev_ref[7:8, :], 0.0)
        rid = lax.broadcasted_iota(jnp.int32, cur.shape, 0)
        shifted = jnp.where(rid == 0, prev_row, pltpu.roll(cur, 1, axis=0))
        return cur + mix * (shifted - cur)

    r = mixed(r_ref, rp_ref, 0)
    k = mixed(k_ref, kp_ref, 1)
    v = mixed(v_ref, vp_ref, 2)
    xw = mixed(xw_ref, xwp_ref, 3)
    xa = mixed(xa_ref, xap_ref, 4)
    xg = mixed(xg_ref, xgp_ref, 5)

    z = -(w0_ref[...] + _dot_f32(jnp.tanh(xw), w2_ref[...]))
    softplus = jnp.maximum(z, 0.0) + jnp.log(1.0 + jnp.exp(-jnp.abs(z)))
    w = -softplus - 0.5
    lw_out[...] = -jnp.exp(w)
    a = jax.nn.sigmoid(a0_ref[...] + _dot_f32(xa, a2_ref[...]))
    g_out[...] = _dot_f32(jax.nn.sigmoid(xg), g2_ref[...])
    if use_vres:
        xv = mixed(xv_ref, xvp_ref, 6)
        v = v + (vfirst_ref[...] - v) * jax.nn.sigmoid(v0_ref[...] + _dot_f32(xv, v2_ref[...]))
    kk = k * kk_ref[...]
    ss = _split_dot(kk * kk, gsum_ref[...], 2)
    kk = kk / jnp.maximum(jnp.sqrt(ss), 1e-12)
    r_out[...] = r
    k_out[...] = k * (1.0 + (a - 1.0) * ka_ref[...])
    v_out[...] = v
    an_out[...] = -kk
    b_out[...] = kk * a


def _rwkv_pre(proj, mix, w0, w2, a0, a2, v0, v2, g2, k_k, k_a, gsum, v_first, use_vres):
    t = proj.shape[0]
    tm = RW_PRE_TM

    def cur(width, off):
        return pl.BlockSpec((tm, width), lambda i: (i, off // width))

    def prev(width, off):
        return pl.BlockSpec((8, width), lambda i: (jnp.maximum(i * (tm // 8) - 1, 0), off // width))

    def full(a):
        return pl.BlockSpec(a.shape, lambda i: (0,) * a.ndim)

    segs = [(RW_W, C_RW_RKV), (RW_W, C_RW_RKV + RW_W), (RW_W, C_RW_RKV + 2 * RW_W),
            (128, C_RW_XW), (128, C_RW_XA), (256, C_RW_XG), (128, C_RW_XV)]
    params = [mix, w0, w2, a0, a2, v0, v2, g2, k_k, k_a, gsum]
    row = pl.BlockSpec((tm, RW_W), lambda i: (i, 0))
    outs = pl.pallas_call(
        functools.partial(_rwkv_pre_kernel, use_vres=use_vres),
        out_shape=[jax.ShapeDtypeStruct((t, RW_W), F32)] * 7,
        grid=(t // tm,),
        in_specs=[cur(w, o) for w, o in segs] + [prev(w, o) for w, o in segs] + [full(p) for p in params] + [row],
        out_specs=[row] * 7,
        compiler_params=_cparams(("parallel",)),
        name="rwkv_pre",
    )(*([proj] * 14), *params, v_first)
    return outs


RW_CHUNK = 128


def _wkv_kernel(r_ref, lw_ref, k_ref, v_ref, an_ref, b_ref, g_ref, lng_ref, lnb_ref, rk_ref,
                gsum_ref, tril_ref, o_ref, s_ref):
    c = pl.program_id(0)
    C = RW_CHUNK
    N = RW_N

    @pl.when(c == 0)
    def _():
        s_ref[...] = jnp.zeros_like(s_ref)

    r = r_ref[...]
    lw = lw_ref[...]
    k = k_ref[...]
    v = v_ref[...]
    lw_hi = lw.astype(BF16)
    rem = lw - lw_hi.astype(F32)
    lw_mid = rem.astype(BF16)
    lw_lo = (rem - lw_mid.astype(F32)).astype(BF16)
    tril = tril_ref[...]
    lg = _dot(tril, lw_hi) + (_dot(tril, lw_mid) + _dot(tril, lw_lo))
    gam = jnp.exp(lg)
    ginv = jnp.exp(-lg)
    at = (an_ref[...] * jnp.exp(lg - lw)).astype(BF16)
    bt = (b_ref[...] * ginv).astype(BF16)
    kt = (k * ginv).astype(BF16)
    rt_f = r * gam
    rt = rt_f.astype(BF16)
    vb = v.astype(BF16)
    g_last = gam[C - 1:C, :]

    rowi = lax.broadcasted_iota(jnp.int32, (C, C), 0)
    coli = lax.broadcasted_iota(jnp.int32, (C, C), 1)
    strict = rowi > coli
    incl = rowi >= coli

    heads = range(RW_HEADS)
    hs = [slice(h * N, (h + 1) * N) for h in heads]
    bth = [bt[:, s] for s in hs]
    kth = [kt[:, s] for s in hs]
    vh = [vb[:, s] for s in hs]
    big = [_dot_t(jnp.concatenate([at[:, s], rt[:, s]], axis=0), jnp.concatenate([bth[h], kth[h]], axis=0))
           for h, s in zip(heads, hs)]
    a_ab = [jnp.where(strict, big[h][:C, :C], 0.0) for h in heads]
    a_ak = [jnp.where(strict, big[h][:C, C:], 0.0).astype(BF16) for h in heads]
    a_rb = [jnp.where(incl, big[h][C:, :C], 0.0).astype(BF16) for h in heads]
    a_rk = [jnp.where(incl, big[h][C:, C:], 0.0).astype(BF16) for h in heads]
    akv = [_dot(a_ak[h], vh[h]) for h in heads]
    def same_block(size):
        shift = size.bit_length() - 1
        return (rowi >> shift) == (coli >> shift)

    tinv = [jnp.where(same_block(2), a_ab[h], 0.0) + jnp.where(rowi == coli, 1.0, 0.0) for h in heads]
    size = 2
    while size < C:
        lower_left = same_block(2 * size) & jnp.logical_not(same_block(size))
        off = [jnp.where(lower_left, a_ab[h], 0.0).astype(BF16) for h in heads]
        tb = [tinv[h].astype(BF16) for h in heads]
        half = [_dot(tb[h], off[h]).astype(BF16) for h in heads]
        tinv = [tinv[h] + _dot(half[h], tb[h]) for h in heads]
        size *= 2
    xb = [_dot(tinv[h].astype(BF16), jnp.concatenate([at[:, hs[h]], akv[h].astype(BF16)], axis=1)).astype(BF16)
          for h in heads]
    yx = [_dot(a_rb[h], xb[h]) for h in heads]
    ykv = [_dot(a_rk[h], vh[h]) for h in heads]
    xtb = [_dot_tl(xb[h], bth[h]) for h in heads]
    vtk = [_dot_tl(vh[h], kth[h]) for h in heads]
    ys = []
    for h in heads:
        gl = g_last[:, hs[h]]
        y1 = (rt_f[:, hs[h]] + yx[h][:, :N]).astype(BF16)
        s0 = s_ref[h]
        s0b = s0.astype(BF16)
        ys.append(_dot_t(y1, s0b) + (yx[h][:, N:] + ykv[h]))
        s_ref[h] = (s0 + _dot(s0b, xtb[h][:N].astype(BF16)) + (xtb[h][N:] + vtk[h])) * gl
    y = jnp.concatenate(ys, axis=1)

    gsum = gsum_ref[...]
    inv_n = 1.0 / N
    mu = _split_dot(y, gsum, 2) * inv_n
    yc = y - mu
    var = _split_dot(yc * yc, gsum, 2) * inv_n
    yn = yc * lax.rsqrt(var + RW_GN_EPS) * lng_ref[...] + lnb_ref[...]
    bonus = _split_dot(r * k * rk_ref[...], gsum, 2)
    o_ref[...] = ((yn + bonus * v) * g_ref[...]).astype(o_ref.dtype)


def _wkv(r, lw, k, v, an, b, g, ln_g, ln_b, r_k, gsum):
    t = r.shape[0]
    C = RW_CHUNK
    tril = jnp.asarray(np.tril(np.ones((C, C), np.float32)), BF16)
    row = pl.BlockSpec((C, RW_W), lambda c: (c, 0))
    vec = pl.BlockSpec((1, RW_W), lambda c: (0, 0))
    return pl.pallas_call(
        _wkv_kernel,
        out_shape=jax.ShapeDtypeStruct((t, RW_W), BF16),
        grid=(t // C,),
        in_specs=[row] * 7 + [vec] * 3 + [pl.BlockSpec((RW_W, RW_W), lambda c: (0, 0)),
                                          pl.BlockSpec((C, C), lambda c: (0, 0))],
        out_specs=row,
        scratch_shapes=[pltpu.VMEM((RW_HEADS, RW_N, RW_N), F32)],
        compiler_params=_cparams(("arbitrary",)),
        name="wkv7",
    )(r, lw, k, v, an, b, g, ln_g, ln_b, r_k, gsum, tril)


NSA_PREP_TM = 512


N_NORM_SLABS = (NSA_W + 2 * NSA_KV_W) // LANES
N_RAW_SLABS = 4 * NSA_KV_W // LANES
N_NSA_SLABS = N_NORM_SLABS + N_RAW_SLABS + 1


def _nsa_prep_kernel(*refs):
    x_refs = refs[:N_NSA_SLABS]
    (cos_ref, sin_ref, w_ref, gmean_ref, big_ref,
     qk_ref, hmat_ref, v1_ref, gate_ref, qpad_ref, ksx_ref) = refs[N_NSA_SLABS:]
    cos = cos_ref[...]
    sin = sin_ref[...]
    gmean = gmean_ref[...]
    half = NSA_DH // 2
    tm = cos.shape[0]
    lane = lax.broadcasted_iota(jnp.int32, cos.shape, 1)
    first_half = (lane % NSA_DH) < half
    low = lane < NSA_DH
    blk_row = (pl.program_id(0) * tm + lax.broadcasted_iota(jnp.int32, cos.shape, 0)) // SLC_BLOCK
    big = big_ref[...]
    mark_lo = jnp.where(blk_row == lane - NSA_DH, big, 0.0)
    mark_hi = jnp.where(low & (blk_row == lane + NSA_DH), big, 0.0).astype(BF16)
    n_q = NSA_W // LANES
    n_k = NSA_KV_W // LANES
    for s in range(N_NORM_SLABS):
        x = x_refs[s][...]
        y = x * lax.rsqrt(_split_dot(x * x, gmean, 2) + RMS_EPS) * w_ref[s]
        rot = jnp.where(first_half, pltpu.roll(y, LANES - half, axis=1), pltpu.roll(y, half, axis=1))
        res32 = y * cos + rot * sin
        res = res32.astype(qk_ref.dtype)
        qk_ref[2 * s] = res[:, :NSA_DH]
        qk_ref[2 * s + 1] = res[:, NSA_DH:]
        if s >= n_q + n_k:
            continue
        swapped = pltpu.roll(res32, NSA_DH, axis=1)
        if s < n_q:
            qpad_ref[2 * s] = jnp.where(low, res32, 0.0).astype(BF16)
            qpad_ref[2 * s + 1] = jnp.where(low, swapped, 0.0).astype(BF16)
        else:
            g0 = 2 * (s - n_q)
            ksx_ref[g0, :, :LANES] = jnp.where(low, res32, mark_lo).astype(BF16)
            ksx_ref[g0 + 1, :, :LANES] = jnp.where(low, swapped, mark_lo).astype(BF16)
            ksx_ref[g0, :, LANES:] = mark_hi
            ksx_ref[g0 + 1, :, LANES:] = mark_hi

    rows = hmat_ref.shape[2]
    low_r = lax.broadcasted_iota(jnp.int32, (rows, LANES), 1) < NSA_DH
    for s in range(NSA_KV_W // LANES * 2):
        kind, pair = divmod(s, NSA_KV_W // LANES)
        x_ref = x_refs[N_NORM_SLABS + s]
        for t2 in range(CMP_STRIDE // 2):
            r0 = x_ref[pl.ds(2 * t2, rows, stride=CMP_STRIDE), :]
            r1 = x_ref[pl.ds(2 * t2 + 1, rows, stride=CMP_STRIDE), :]
            cols = slice(t2 * LANES, (t2 + 1) * LANES)
            hmat_ref[kind, 2 * pair, :, cols] = jnp.where(low_r, r0, pltpu.roll(r1, NSA_DH, axis=1)).astype(BF16)
            hmat_ref[kind, 2 * pair + 1, :, cols] = jnp.where(low_r, pltpu.roll(r0, NSA_DH, axis=1), r1).astype(BF16)

    one_col = jnp.where(lane == NSA_DH, 1.0, 0.0)
    for s in range(NSA_KV_W // LANES * 2):
        x = x_refs[N_NORM_SLABS + N_RAW_SLABS // 2 + s][...]
        v1_ref[2 * s] = jnp.where(low, x, one_col).astype(BF16)
        v1_ref[2 * s + 1] = jnp.where(low, pltpu.roll(x, NSA_DH, axis=1), one_col).astype(BF16)

    xg = x_refs[N_NSA_SLABS - 1][...]
    per_group = 3 * NSA_HG
    for g in range(NSA_G):
        shifted = xg if g == 0 else pltpu.roll(xg, LANES - per_group * g, axis=1)
        gate_ref[g] = jnp.where(lane < per_group, shifted, 0.0)


def _nsa_prep(proj, w_slabs, cos, sin, gmean, big):
    t = proj.shape[0]
    assert t // SLC_BLOCK <= LANES
    tm = NSA_PREP_TM
    base = C_NSA_Q // LANES
    tab = pl.BlockSpec((tm, LANES), lambda i: (i, 0))
    n_heads_out = 2 * N_NORM_SLABS
    return pl.pallas_call(
        _nsa_prep_kernel,
        out_shape=[jax.ShapeDtypeStruct((n_heads_out, t, NSA_DH), BF16),
                   jax.ShapeDtypeStruct((2, NSA_G, t // CMP_STRIDE, CMP_STRIDE * NSA_DH), BF16),
                   jax.ShapeDtypeStruct((2 * NSA_G, t, LANES), BF16),
                   jax.ShapeDtypeStruct((NSA_G, t, LANES), F32),
                   jax.ShapeDtypeStruct((NSA_HEADS, t, LANES), BF16),
                   jax.ShapeDtypeStruct((NSA_G, t, 2 * LANES), BF16)],
        grid=(t // tm,),
        in_specs=[pl.BlockSpec((tm, LANES), lambda i, s=s: (i, base + s)) for s in range(N_NSA_SLABS)]
        + [tab, tab, pl.BlockSpec((N_NORM_SLABS, 1, LANES), lambda i: (0, 0, 0)),
           pl.BlockSpec((LANES, LANES), lambda i: (0, 0)), pl.BlockSpec((1, LANES), lambda i: (0, 0))],
        out_specs=[pl.BlockSpec((n_heads_out, tm, NSA_DH), lambda i: (0, i, 0)),
                   pl.BlockSpec((2, NSA_G, tm // CMP_STRIDE, CMP_STRIDE * NSA_DH), lambda i: (0, 0, i, 0)),
                   pl.BlockSpec((2 * NSA_G, tm, LANES), lambda i: (0, i, 0)),
                   pl.BlockSpec((NSA_G, tm, LANES), lambda i: (0, i, 0)),
                   pl.BlockSpec((NSA_HEADS, tm, LANES), lambda i: (0, i, 0)),
                   pl.BlockSpec((NSA_G, tm, 2 * LANES), lambda i: (0, i, 0))],
        compiler_params=_cparams(("parallel",)),
        name="nsa_prep",
    )(*([proj] * N_NSA_SLABS), cos, sin, w_slabs, gmean, big)


def _compress_kernel(h_ref, w1_ref, w2_ref, pe_ref, nw_ref, cos_ref, sin_ref, o_ref):
    kind = pl.program_id(0)
    nc = h_ref.shape[2]
    half_in = CMP_STRIDE * NSA_DH
    hm = h_ref[0, 0]
    w1 = w1_ref[0]
    first = _dot(hm, w1[:half_in])
    second = _dot(hm, w1[half_in:])
    const = _dot(pe_ref[0], w1)[0:1, :]
    pre = first + pltpu.roll(second, nc - 1, axis=0) + const
    hid = 0.5 * pre * (1.0 + jnp.tanh(0.7978845608028654 * (pre + 0.044715 * pre * pre * pre)))
    out = _dot(hid.astype(BF16), w2_ref[0])
    y = out * lax.rsqrt(jnp.mean(out * out, axis=-1, keepdims=True) + RMS_EPS) * nw_ref[...]
    half = NSA_DH // 2
    rot = jnp.concatenate([y[:, half:], y[:, :half]], axis=1)
    roped = y * cos_ref[...] + rot * sin_ref[...]
    o_ref[0, 0] = jnp.where(kind == 0, roped, out).astype(o_ref.dtype)


def _compress(hmat, w1, w2, pe, nw, cos_c, sin_c):
    _, g, nc, width = hmat.shape
    return pl.pallas_call(
        _compress_kernel,
        out_shape=jax.ShapeDtypeStruct((2, g, nc, NSA_DH), BF16),
        grid=(2, g),
        in_specs=[pl.BlockSpec((1, 1, nc, width), lambda a, b: (a, b, 0, 0)),
                  pl.BlockSpec((1, 2 * width, CMP_HIDDEN), lambda a, b: (a, 0, 0)),
                  pl.BlockSpec((1, CMP_HIDDEN, NSA_DH), lambda a, b: (a, 0, 0)),
                  pl.BlockSpec((1, 8, 2 * width), lambda a, b: (a, 0, 0)),
                  pl.BlockSpec((1, NSA_DH), lambda a, b: (0, 0)),
                  pl.BlockSpec((nc, NSA_DH), lambda a, b: (0, 0)),
                  pl.BlockSpec((nc, NSA_DH), lambda a, b: (0, 0))],
        out_specs=pl.BlockSpec((1, 1, nc, NSA_DH), lambda a, b: (a, b, 0, 0)),
        compiler_params=_cparams(("parallel", "parallel")),
        name="nsa_compress",
    )(hmat, w1, w2, pe, nw, cos_c, sin_c)


SLC_KT = 1024
WIN_ROWS = 128
WIN_KEYS = WINDOW + WIN_ROWS


INT32_MIN = -2 ** 31


def _nsa_attn_kernel(q_ref, qpad_ref, gate_ref, kc_ref, vc_ref, ks_ref, vs_ref, kw_ref, vw_ref, cis_ref, ltri_ref,
                     o_ref):
    qi = pl.program_id(1)
    nc = kc_ref.shape[2]
    nb = cis_ref.shape[1]
    heads = range(NSA_HG)
    start = qi * NSA_TQ
    qh = [q_ref[hh] for hh in heads]
    tok = start + lax.broadcasted_iota(jnp.int32, (NSA_TQ, 1), 0)

    def softmax_terms(scores):
        return [jnp.exp(s - jnp.max(s, axis=-1, keepdims=True)) for s in scores]

    def cmp_branch(width):
        kc = kc_ref[0, 0, :width, :]
        vc = vc_ref[0, 0, :width, :]
        cend = lax.broadcasted_iota(jnp.int32, (1, width), 1) * CMP_STRIDE + (CMP_BLOCK - 1)
        bias_c = jnp.where(cend <= tok, 0.0, NEG_INF)
        ec = softmax_terms([_dot_t(qh[hh], kc) + bias_c for hh in heads])
        sees_any = tok >= CMP_BLOCK - 1
        pc = [ec[hh] * jnp.where(sees_any, 1.0 / jnp.sum(ec[hh], axis=-1, keepdims=True), 0.0) for hh in heads]
        outs_c = tuple(_dot(pc[hh].astype(BF16), vc) for hh in heads)
        psum = pc[0]
        for hh in range(1, NSA_HG):
            psum = psum + pc[hh]
        return outs_c, _split_dot(psum, cis_ref[:width, :], 2)

    n_quarters = 4
    quarter = nc // n_quarters
    n_visible = (start + NSA_TQ - CMP_BLOCK) // CMP_STRIDE + 1
    which = jnp.clip((n_visible - 1) // quarter, 0, n_quarters - 1)
    o_c, score = lax.switch(which, [functools.partial(cmp_branch, (r + 1) * quarter) for r in range(n_quarters)])
    blk = lax.broadcasted_iota(jnp.int32, (1, nb), 1)
    cur = tok // SLC_BLOCK
    forced = (blk == 0) | (blk == cur) | (blk == cur - 1)
    score = jnp.where(forced, FORCED_SCORE, score)
    score = jnp.where(blk <= cur, score, -jnp.inf)

    bits = lax.bitcast_convert_type(score.T, jnp.int32)
    key_all = bits ^ ((bits >> 31) & 0x7FFFFFFF)

    def select_blocks(nbv):
        key = key_all[:nbv]
        n_sel = min(SLC_TOPK, nbv)

        def enough(c):
            return jnp.sum(jnp.where(key >= c, 1.0, 0.0), axis=0, keepdims=True) >= n_sel

        def enough3(c1, c2, c3):
            packed = jnp.where(key >= c3, 65793.0, jnp.where(key >= c2, 257.0, jnp.where(key >= c1, 1.0, 0.0)))
            tot = jnp.sum(packed, axis=0, keepdims=True).astype(jnp.int32)
            return (tot & 255) >= n_sel, ((tot >> 8) & 255) >= n_sel, (tot >> 16) >= n_sel

        zero_row = jnp.zeros((1, NSA_TQ), jnp.int32)
        thr = jnp.where(enough(zero_row), zero_row, INT32_MIN)
        for hi in range(30, 0, -2):
            c1 = thr + (1 << (hi - 1))
            c2 = thr + (1 << hi)
            c3 = c2 + (1 << (hi - 1))
            e1, e2, e3 = enough3(c1, c2, c3)
            thr = jnp.where(e3, c3, jnp.where(e2, c2, jnp.where(e1, c1, thr)))
        c1 = thr + 1
        thr = jnp.where(enough(c1), c1, thr)
        above = key > thr
        tied = key == thr
        need = n_sel - jnp.sum(jnp.where(above, 1.0, 0.0), axis=0, keepdims=True)
        tied_before = _dot(ltri_ref[:nbv, :nbv], jnp.where(tied, 1.0, 0.0).astype(BF16))
        chosen = jnp.where(above | (tied & (tied_before < need)), 1.0, 0.0)
        if nbv < LANES:
            chosen = jnp.concatenate([chosen, jnp.zeros((LANES - nbv, NSA_TQ), F32)], axis=0)
        return chosen.T

    assert nb <= LANES
    n_steps = 4
    blocks_visible = (start + NSA_TQ - 1) // SLC_BLOCK + 1
    widths = sorted({max(-(-nb * (r + 1) // n_steps // 8) * 8, min(nb, SLC_TOPK)) for r in range(n_steps)})
    which_sel = sum((blocks_visible > w).astype(jnp.int32) for w in widths[:-1])
    sel = lax.switch(which_sel, [functools.partial(select_blocks, w) for w in widths])
    lane_q = lax.broadcasted_iota(jnp.int32, (NSA_TQ, LANES), 1)
    sel_sw = pltpu.roll(sel, NSA_DH, axis=1)
    sel_lo = jnp.where(lane_q >= NSA_DH, sel_sw, 0.0)
    sel_hi = jnp.where(lane_q < NSA_DH, sel_sw, 0.0)
    qx = [jnp.concatenate([qpad_ref[hh].astype(F32) + sel_lo, sel_hi], axis=1).astype(BF16) for hh in heads]

    def window_rows(r0):
        kbase = pl.multiple_of(jnp.maximum(start + r0 - WINDOW, 0), WIN_ROWS)
        kw = kw_ref[0, pl.ds(kbase, WIN_KEYS), :]
        vw = vw_ref[0, pl.ds(kbase, WIN_KEYS), :]
        lag = tok[r0:r0 + WIN_ROWS] - (kbase + lax.broadcasted_iota(jnp.int32, (1, WIN_KEYS), 1))
        bias_w = jnp.where((lag >= 0) & (lag < WINDOW), 0.0, NEG_INF)
        sw = [_dot_t(qh[hh][r0:r0 + WIN_ROWS], kw) + bias_w for hh in heads]
        ew = [jnp.exp((s - jnp.max(s, axis=-1, keepdims=True)).astype(BF16)) for s in sw]
        return [_dot(ew[hh], vw) for hh in heads]

    win_parts = [window_rows(r0) for r0 in range(0, NSA_TQ, WIN_ROWS)]
    acc_w = [jnp.concatenate([part[hh] for part in win_parts], axis=0) for hh in heads]

    n_tiles = (start + NSA_TQ + SLC_KT - 1) // SLC_KT

    def sel_tile(j, carry, causal, width=SLC_KT):
        ms, accs = carry
        k0 = pl.multiple_of(j * SLC_KT, SLC_KT)
        kt = ks_ref[0, pl.ds(k0, width), :]
        vt = vs_ref[0, pl.ds(k0, width), :]
        sj = [_dot_t(qx[hh], kt) for hh in heads]
        if causal:
            future = (k0 + lax.broadcasted_iota(jnp.int32, (1, width), 1)) > tok
            sj = [jnp.where(future, NEG_INF, s) for s in sj]
        m_new = [jnp.maximum(ms[hh], jnp.max(sj[hh], axis=-1, keepdims=True)) for hh in heads]
        pj = [jnp.exp((sj[hh] - m_new[hh]).astype(BF16)) for hh in heads]
        accs = [jnp.exp(ms[hh] - m_new[hh]) * accs[hh] + _dot(pj[hh], vt) for hh in heads]
        return tuple(m_new), tuple(accs)

    m0 = tuple(jnp.full((NSA_TQ, 1), NEG_INF, F32) for _ in heads)
    a0 = tuple(jnp.zeros((NSA_TQ, LANES), F32) for _ in heads)
    carry = lax.fori_loop(0, n_tiles - 1, functools.partial(sel_tile, causal=False), (m0, a0))
    per_tile = SLC_KT // NSA_TQ
    last = [functools.partial(sel_tile, n_tiles - 1, causal=True, width=(r + 1) * NSA_TQ) for r in range(per_tile)]
    _, acc_s = lax.switch(qi % per_tile, last, carry)

    gt = jax.nn.sigmoid(gate_ref[0])
    outs = []
    for hh in heads:
        g0, g1, g2 = (gt[:, 3 * hh + br:3 * hh + br + 1] for br in range(3))
        scale_s = g1 / acc_s[hh][:, NSA_DH:NSA_DH + 1]
        scale_w = g2 / acc_w[hh][:, NSA_DH:NSA_DH + 1]
        outs.append(g0 * o_c[hh] + scale_s * acc_s[hh][:, :NSA_DH] + scale_w * acc_w[hh][:, :NSA_DH])
    o_ref[...] = jnp.concatenate(outs, axis=1).astype(o_ref.dtype)


def _nsa_attn(qk_hm, qpad, ksx, gates, cmp_kv, v1, cis, ltri):
    t = qk_hm.shape[1]
    nq = t // NSA_TQ
    nc = cmp_kv.shape[2]
    nb = cis.shape[1]
    ks_spec = pl.BlockSpec((1, t, 2 * LANES), lambda g, i: (g, 0, 0))
    kw_spec = pl.BlockSpec((1, t, NSA_DH), lambda g, i: (NSA_HEADS + NSA_G + g, 0, 0))
    vs_spec = pl.BlockSpec((1, t, LANES), lambda g, i: (g, 0, 0))
    vw_spec = pl.BlockSpec((1, t, LANES), lambda g, i: (NSA_G + g, 0, 0))
    return pl.pallas_call(
        _nsa_attn_kernel,
        out_shape=jax.ShapeDtypeStruct((t, NSA_W), BF16),
        grid=(NSA_G, nq),
        in_specs=[pl.BlockSpec((NSA_HG, NSA_TQ, NSA_DH), lambda g, i: (g, i, 0)),
                  pl.BlockSpec((NSA_HG, NSA_TQ, LANES), lambda g, i: (g, i, 0)),
                  pl.BlockSpec((1, NSA_TQ, LANES), lambda g, i: (g, i, 0)),
                  pl.BlockSpec((1, 1, nc, NSA_DH), lambda g, i: (0, g, 0, 0)),
                  pl.BlockSpec((1, 1, nc, NSA_DH), lambda g, i: (1, g, 0, 0)),
                  ks_spec, vs_spec, kw_spec, vw_spec,
                  pl.BlockSpec((nc, nb), lambda g, i: (0, 0)),
                  pl.BlockSpec((nb, nb), lambda g, i: (0, 0))],
        out_specs=pl.BlockSpec((NSA_TQ, NSA_HG * NSA_DH), lambda g, i: (i, g)),
        compiler_params=_cparams(("parallel", "arbitrary")),
        name="nsa_attention",
    )(qk_hm, qpad, gates, cmp_kv, cmp_kv, ksx, v1, qk_hm, v1, cis, ltri)


def _rope_tables(pos, dh, reps):
    half = dh // 2
    inv_freq = ROPE_THETA ** (-np.arange(half, dtype=np.float64) / half)
    ang = np.asarray(pos, np.float64)[:, None] * inv_freq[None, :]
    cos = np.cos(ang)
    sin = np.sin(ang)
    cos_t = np.tile(np.concatenate([cos, cos], axis=1), (1, reps))
    sin_t = np.tile(np.concatenate([-sin, sin], axis=1), (1, reps))
    return jnp.asarray(cos_t, F32), jnp.asarray(sin_t, F32)


def _pad_cols(a, width):
    return jnp.pad(a, ((0, 0), (0, width - a.shape[1])))


def _pad_rows(a, height):
    return jnp.pad(a, ((0, height - a.shape[0]), (0, 0)))


def _pack_w_in_kernel(w_ref, xv_ref, o_ref):
    o_ref[...] = _pack_w_in(w_ref[...], xv_ref[...]).astype(o_ref.dtype)


def _pack_w_in_call(w_in, layer, xv):
    _, d, n_in = w_in.shape
    tk = 256
    return pl.pallas_call(
        _pack_w_in_kernel,
        out_shape=jax.ShapeDtypeStruct((d, PROJ_W), BF16),
        grid=(d // tk,),
        in_specs=[pl.BlockSpec((None, tk, n_in), lambda i: (layer, i, 0)),
                  pl.BlockSpec((tk, LANES), lambda i: (i, 0))],
        out_specs=pl.BlockSpec((tk, PROJ_W), lambda i: (i, 0)),
        compiler_params=_cparams(("parallel",)),
        name="pack_w_in",
    )(w_in, xv)


def _pack_w_in(w_l, xv):
    d = w_l.shape[0]
    o = 0
    ret = w_l[:, o:o + 4 * RET_W]; o += 4 * RET_W
    rkv = w_l[:, o:o + 3 * RW_W]; o += 3 * RW_W
    xw = w_l[:, o:o + RW_DECAY_RANK]; o += RW_DECAY_RANK
    xa = w_l[:, o:o + RW_A_RANK]; o += RW_A_RANK
    xg = w_l[:, o:o + RW_GATE_RANK]; o += RW_GATE_RANK
    q = w_l[:, o:o + NSA_W]; o += NSA_W
    kc, vc, ks, vs, kw, vw = (w_l[:, o + i * NSA_KV_W:o + (i + 1) * NSA_KV_W] for i in range(6))
    o += 6 * NSA_KV_W
    gates = w_l[:, o:o + 3 * NSA_HEADS]
    return jnp.concatenate([ret, rkv, _pad_cols(xw, LANES), _pad_cols(xa, LANES), xg, xv,
                            q, ks, kw, kc, vc, vs, vw, _pad_cols(gates, LANES)], axis=1)


def _pack_mix(mix, vres_mix):
    o = 3 * RW_W
    xw = mix[o:o + RW_DECAY_RANK]; o += RW_DECAY_RANK
    xa = mix[o:o + RW_A_RANK]; o += RW_A_RANK
    xg = mix[o:o + RW_GATE_RANK]
    z = lambda n: jnp.zeros((n,), mix.dtype)
    xv = z(LANES) if vres_mix is None else jnp.concatenate([vres_mix, z(LANES - RW_V_RANK)])
    return jnp.concatenate([mix[:3 * RW_W], xw, z(LANES - RW_DECAY_RANK), xa, z(LANES - RW_A_RANK), xg, xv])[None, :]


def kernel(x, ln1_g, w_in, w_in_vres, rwkv_mix, rwkv_vres_mix, rwkv_w0, rwkv_w2, rwkv_a0, rwkv_a2, rwkv_v0, rwkv_v2, rwkv_g2, rwkv_k_k, rwkv_k_a, rwkv_r_k, rwkv_ln_g, rwkv_ln_b, nsa_q_norm, nsa_k_norm, nsa_cmp_pe, nsa_cmp_k_w1, nsa_cmp_k_w2, nsa_cmp_v_w1, nsa_cmp_v_w2, w_out, ln2_g, w_up, w_down):
    bsz, t, d = x.shape
    assert bsz == 1
    depth = w_in.shape[0]
    nc = t // CMP_STRIDE
    nb = t // SLC_BLOCK
    xs = x.reshape(t, d)

    pos = np.arange(t)
    cos_r, sin_r = _rope_tables(pos, RET_DH, 1)
    cos_n, sin_n = _rope_tables(pos, NSA_DH, 2)
    cos_c, sin_c = _rope_tables(np.arange(nc) * CMP_STRIDE + (CMP_BLOCK - 1), NSA_DH, 1)
    lane_head = np.arange(RW_W) // RW_N
    gsum = jnp.asarray(lane_head[:, None] == lane_head[None, :], BF16)
    lane_h2 = np.arange(LANES) // NSA_DH
    gmean = jnp.asarray((lane_h2[:, None] == lane_h2[None, :]) / float(NSA_DH), BF16)
    cstart = np.arange(nc) * CMP_STRIDE
    sstart = np.arange(nb) * SLC_BLOCK
    cis = jnp.asarray((cstart[:, None] <= sstart[None, :] + SLC_BLOCK - 1)
                      & (cstart[:, None] + CMP_BLOCK - 1 >= sstart[None, :]), BF16)

    ltri = jnp.asarray(np.tril(np.ones((nb, nb), np.float32), -1), BF16)

    v_first = jnp.zeros((t, RW_W), F32)
    for l in range(depth):
        vres = l > 0
        xv_w = _pad_cols(w_in_vres[l - 1], LANES) if vres else jnp.zeros((d, LANES), F32)
        w_cat = _pack_w_in_call(w_in, l, xv_w)
        proj = _norm_matmul(xs, ln1_g[l], w_cat, name="in_proj")

        o_ret = _retention(proj, cos_r, sin_r)

        row = lambda a: a.reshape(1, -1)
        mix = _pack_mix(rwkv_mix[l], rwkv_vres_mix[l - 1] if vres else None)
        v0 = row(rwkv_v0[l - 1]) if vres else jnp.zeros((1, RW_W), F32)
        v2 = _pad_rows(rwkv_v2[l - 1], LANES) if vres else jnp.zeros((LANES, RW_W), F32)
        r_, lw_, k_, v_, an_, b_, g_ = _rwkv_pre(
            proj, mix, row(rwkv_w0[l]), _pad_rows(rwkv_w2[l], LANES), row(rwkv_a0[l]), _pad_rows(rwkv_a2[l], LANES),
            v0, v2, rwkv_g2[l], row(rwkv_k_k[l]), row(rwkv_k_a[l]), gsum, v_first, vres)
        if not vres:
            v_first = v_
        o_rwkv = _wkv(r_, lw_, k_, v_, an_, b_, g_, row(rwkv_ln_g[l]), row(rwkv_ln_b[l]), row(rwkv_r_k[l]), gsum)

        n_q = NSA_W // LANES
        n_k = NSA_KV_W // LANES
        w_slabs = jnp.stack([jnp.tile(nsa_q_norm[l] * (NSA_DH ** -0.5), 2)] * n_q
                            + [jnp.tile(nsa_k_norm[l, 1], 2)] * n_k + [jnp.tile(nsa_k_norm[l, 2], 2)] * n_k)[:, None, :]
        s_max = 1.02 * jnp.max(jnp.abs(nsa_q_norm[l])) * jnp.max(jnp.abs(nsa_k_norm[l, 1])) * (NSA_DH ** 0.5)
        big = jnp.full((1, LANES), jnp.exp2(jnp.ceil(jnp.log2(2.0 * s_max + 128.0))), F32)
        qk_hm, hmat, v1, gates, qpad, ksx = _nsa_prep(proj, w_slabs, cos_n, sin_n, gmean, big)
        w1 = jnp.stack([nsa_cmp_k_w1[l], nsa_cmp_v_w1[l]]).astype(BF16)
        w2 = jnp.stack([nsa_cmp_k_w2[l], nsa_cmp_v_w2[l]]).astype(BF16)
        pe = jnp.broadcast_to(nsa_cmp_pe[l].reshape(2, 1, CMP_BLOCK * NSA_DH), (2, 8, CMP_BLOCK * NSA_DH)).astype(BF16)
        cmp_kv = _compress(hmat, w1, w2, pe, row(nsa_k_norm[l, 0]), cos_c, sin_c)
        o_nsa = _nsa_attn(qk_hm, qpad, ksx, gates, cmp_kv, v1, cis, ltri)

        xs = _out_proj(o_ret, o_rwkv, o_nsa, w_out, l, xs)

        up = _norm_matmul(xs, ln2_g[l], w_up, layer=l, act="relu2", out_dtype=BF16, name="mlp_up")
        xs = _matmul(up, w_down, layer=l, res=xs, name="mlp_down")
    return xs.reshape(bsz, t, d)
```

```python
import functools

import numpy as np
import jax
import jax.numpy as jnp
from jax import lax
from jax.experimental import pallas as pl
from jax.experimental.pallas import tpu as pltpu

F32 = jnp.float32
BF16 = jnp.bfloat16

RET_HEADS, RET_DH = 4, 128
RET_W = RET_HEADS * RET_DH
RW_HEADS, RW_N = 8, 64
RW_W = RW_HEADS * RW_N
RW_DECAY_RANK, RW_A_RANK, RW_V_RANK, RW_GATE_RANK = 96, 96, 64, 256
RW_GN_EPS = 64e-5
NSA_HEADS, NSA_G, NSA_DH = 16, 4, 64
NSA_HG = NSA_HEADS // NSA_G
NSA_W = NSA_HEADS * NSA_DH
NSA_KV_W = NSA_G * NSA_DH
CMP_BLOCK, CMP_STRIDE, CMP_HIDDEN = 32, 16, 256
SLC_BLOCK, SLC_TOPK, WINDOW = 64, 16, 512
NSA_TQ = 512
ROPE_THETA = 10000.0
RMS_EPS = 1e-6
NEG_INF = -1e30
FORCED_SCORE = 1e9

LANES = 128
VMEM_LIMIT = 56 * 1024 * 1024

C_RET = 0
C_RW_RKV = 2048
C_RW_XW = 3584
C_RW_XA = 3712
C_RW_XG = 3840
C_RW_XV = 4096
C_NSA_Q = 4224
PROJ_W = 6912


def _cparams(sem):
    return pltpu.CompilerParams(dimension_semantics=sem, vmem_limit_bytes=VMEM_LIMIT)


def _dot(a, b):
    return jnp.dot(a, b, preferred_element_type=F32)


def _dot_t(a, b):
    return lax.dot_general(a, b, (((1,), (1,)), ((), ())), preferred_element_type=F32)


def _dot_tl(a, b):
    return lax.dot_general(a, b, (((0,), (0,)), ((), ())), preferred_element_type=F32)


def _split_dot(x, m_bf16, passes):
    hi = x.astype(BF16)
    acc = _dot(hi, m_bf16)
    rem = x - hi.astype(F32)
    for _ in range(passes - 1):
        piece = rem.astype(BF16)
        acc = acc + _dot(piece, m_bf16)
        rem = rem - piece.astype(F32)
    return acc


def _mm_kernel(*refs, nk, act, has_res):
    a_ref, b_ref = refs[:2]
    r_ref = refs[2] if has_res else None
    o_ref = refs[3] if has_res else refs[2]

    def finish(y):
        if act == "relu2":
            y = jnp.square(jnp.maximum(y, 0.0))
        if has_res:
            y = y + r_ref[...]
        o_ref[...] = y.astype(o_ref.dtype)

    part = _dot(a_ref[...], b_ref[...].astype(BF16))
    if nk == 1:
        finish(part)
        return
    k = pl.program_id(2)
    if act is None and o_ref.dtype == F32:
        @pl.when(k == 0)
        def _():
            o_ref[...] = part + r_ref[...] if has_res else part

        @pl.when(k > 0)
        def _():
            o_ref[...] += part

        return
    acc_ref = refs[-1]

    @pl.when(k == 0)
    def _():
        acc_ref[...] = part

    @pl.when(k > 0)
    def _():
        acc_ref[...] += part

    @pl.when(k == nk - 1)
    def _():
        finish(acc_ref[...])


def _pick(n, cands):
    for c in cands:
        if n % c == 0:
            return c
    raise ValueError(f"no tile for {n}")


def _weight_spec(b, layer, block, index_map):
    if layer is None:
        return pl.BlockSpec(block, index_map)
    return pl.BlockSpec((None,) + block, lambda *idx: (layer,) + index_map(*idx))


def _matmul(a, b, *, layer=None, res=None, act=None, out_dtype=F32, name="matmul"):
    m, kd = a.shape
    n = b.shape[-1]
    tm = _pick(m, (1024, 512, 256, 128))
    tn = _pick(n, (1024, 768, 512, 256, 128))
    tk = _pick(kd, (2048, 1024, 512))
    nk = kd // tk
    in_specs = [pl.BlockSpec((tm, tk), lambda i, j, k: (i, k)),
                _weight_spec(b, layer, (tk, tn), lambda i, j, k: (k, j))]
    args = [a, b]
    if res is not None:
        in_specs.append(pl.BlockSpec((tm, tn), lambda i, j, k: (i, j)))
        args.append(res)
    return pl.pallas_call(
        functools.partial(_mm_kernel, nk=nk, act=act, has_res=res is not None),
        out_shape=jax.ShapeDtypeStruct((m, n), out_dtype),
        grid=(m // tm, n // tn, nk),
        in_specs=in_specs,
        out_specs=pl.BlockSpec((tm, tn), lambda i, j, k: (i, j)),
        scratch_shapes=[pltpu.VMEM((tm, tn), F32)] if nk > 1 and (act is not None or out_dtype != F32) else [],
        compiler_params=_cparams(("parallel", "parallel", "arbitrary")),
        name=name,
    )(*args)


def _norm_mm_kernel(x_ref, g_ref, b_ref, o_ref, h_ref, *, act):
    @pl.when(pl.program_id(1) == 0)
    def _():
        x = x_ref[...]
        ms = jnp.mean(x * x, axis=-1, keepdims=True)
        h_ref[...] = (x * lax.rsqrt(ms + RMS_EPS) * g_ref[...]).astype(BF16)

    y = _dot(h_ref[...], b_ref[...].astype(BF16))
    if act == "relu2":
        y = jnp.square(jnp.maximum(y, 0.0))
    o_ref[...] = y.astype(o_ref.dtype)


def _norm_matmul(x, g, b, *, layer=None, act=None, out_dtype=F32, name="norm_matmul"):
    m, kd = x.shape
    n = b.shape[-1]
    tm = _pick(m, (1024, 512, 256, 128))
    tn = _pick(n, (1024, 768, 512, 256, 128))
    return pl.pallas_call(
        functools.partial(_norm_mm_kernel, act=act),
        out_shape=jax.ShapeDtypeStruct((m, n), out_dtype),
        grid=(m // tm, n // tn),
        in_specs=[pl.BlockSpec((tm, kd), lambda i, j: (i, 0)),
                  pl.BlockSpec((1, kd), lambda i, j: (0, 0)),
                  _weight_spec(b, layer, (kd, tn), lambda i, j: (0, j))],
        out_specs=pl.BlockSpec((tm, tn), lambda i, j: (i, j)),
        scratch_shapes=[pltpu.VMEM((tm, kd), BF16)],
        compiler_params=_cparams(("parallel", "arbitrary")),
        name=name,
    )(x, g.reshape(1, kd), b)


def _out_proj_kernel(ret_ref, rw_ref, nsa_ref, w_ref, res_ref, o_ref, wb_ref):
    @pl.when(pl.program_id(1) == 0)
    def _():
        wb_ref[...] = w_ref[...].astype(BF16)

    k0 = ret_ref.shape[1]
    k1 = k0 + rw_ref.shape[1]
    acc = _dot(ret_ref[...], wb_ref[:k0, :])
    acc = acc + _dot(rw_ref[...], wb_ref[k0:k1, :])
    acc = acc + _dot(nsa_ref[...], wb_ref[k1:, :])
    o_ref[...] = acc + res_ref[...]


def _out_proj(o_ret, o_rwkv, o_nsa, w, layer, res):
    m = o_ret.shape[0]
    kd, n = w.shape[-2:]
    assert kd == o_ret.shape[1] + o_rwkv.shape[1] + o_nsa.shape[1]
    tm = _pick(m, (1024, 512, 256, 128))
    tn = _pick(n, (1024, 512, 256, 128))

    def rows(a):
        return pl.BlockSpec((tm, a.shape[1]), lambda j, i: (i, 0))

    return pl.pallas_call(
        _out_proj_kernel,
        out_shape=jax.ShapeDtypeStruct((m, n), F32),
        grid=(n // tn, m // tm),
        in_specs=[rows(o_ret), rows(o_rwkv), rows(o_nsa),
                  _weight_spec(w, layer, (kd, tn), lambda j, i: (0, j)),
                  pl.BlockSpec((tm, tn), lambda j, i: (i, j))],
        out_specs=pl.BlockSpec((tm, tn), lambda j, i: (i, j)),
        scratch_shapes=[pltpu.VMEM((kd, tn), BF16)],
        compiler_params=_cparams(("parallel", "arbitrary")),
        name="out_proj",
    )(o_ret, o_rwkv, o_nsa, w, res)


RET_CHUNK = 512


def _retention_kernel(q_ref, k_ref, v_ref, g_ref, cos_ref, sin_ref, dmat_ref, qd_ref, kd_ref, cd_ref, o_ref, state_ref):
    @pl.when(pl.program_id(0) == 0)
    def _():
        state_ref[...] = jnp.zeros_like(state_ref)

    cos = cos_ref[...]
    sin = sin_ref[...]

    def rope(x):
        return x * cos + pltpu.roll(x, RET_DH // 2, axis=1) * sin

    heads = range(RET_HEADS)
    hs = [slice(h * RET_DH, (h + 1) * RET_DH) for h in heads]
    q_all, k_all, v_all, gate = q_ref[...], k_ref[...], v_ref[...], g_ref[...]
    qb = [rope(q_all[:, s]).astype(BF16) for s in hs]
    kf = [rope(k_all[:, s]) * (RET_DH ** -0.5) for s in hs]
    kb = [kf[h].astype(BF16) for h in heads]
    vb = [v_all[:, s].astype(BF16) for s in hs]
    inner = [(_dot_t(qb[h], kb[h]) * dmat_ref[h]).astype(BF16) for h in heads]
    cross = [_dot(qb[h], state_ref[h].astype(BF16)) * qd_ref[h] for h in heads]
    kv = [_dot_tl((kf[h] * kd_ref[h]).astype(BF16), vb[h]) for h in heads]
    outs = []
    for h in heads:
        out = _dot(inner[h], vb[h]) + cross[h]
        state_ref[h] = cd_ref[h] * state_ref[h] + kv[h]
        outs.append(out * lax.rsqrt(jnp.mean(out * out, axis=-1, keepdims=True) + RMS_EPS))
    o_ref[...] = (gate * jax.nn.sigmoid(gate) * jnp.concatenate(outs, axis=1)).astype(o_ref.dtype)


def _retention(proj, cos, sin):
    t = proj.shape[0]
    chunk = RET_CHUNK
    log_gamma = np.log(1.0 - 2.0 ** (-5.0 - np.arange(RET_HEADS, dtype=np.float64)))[:, None, None]
    n = np.arange(chunk, dtype=np.float64)
    lag = n[:, None] - n[None, :]
    dmat = np.where(lag >= 0, np.exp(np.maximum(lag, 0.0)[None] * log_gamma), 0.0)
    ones = np.ones((1, 1, RET_DH))
    qd = np.exp((n + 1.0)[None, :, None] * log_gamma) * ones
    kd = np.exp((chunk - 1.0 - n)[None, :, None] * log_gamma) * ones
    cd = np.exp(chunk * log_gamma) * ones
    tables = [jnp.asarray(a, F32) for a in (dmat, qd, kd, cd)]
    base = C_RET // RET_W

    def col(off):
        return pl.BlockSpec((chunk, RET_W), lambda c, off=off: (c, base + off))

    def full(a):
        return pl.BlockSpec(a.shape, lambda c: (0, 0, 0))

    tab = pl.BlockSpec((chunk, RET_DH), lambda c: (c, 0))
    return pl.pallas_call(
        _retention_kernel,
        out_shape=jax.ShapeDtypeStruct((t, RET_W), BF16),
        grid=(t // chunk,),
        in_specs=[col(0), col(1), col(2), col(3), tab, tab] + [full(a) for a in tables],
        out_specs=pl.BlockSpec((chunk, RET_W), lambda c: (c, 0)),
        scratch_shapes=[pltpu.VMEM((RET_HEADS, RET_DH, RET_DH), F32)],
        compiler_params=_cparams(("arbitrary",)),
        name="retention",
    )(proj, proj, proj, proj, cos, sin, *tables)


def _dot_f32(a, b):
    a_hi = a.astype(BF16)
    b_hi = b.astype(BF16)
    a_lo = (a - a_hi.astype(F32)).astype(BF16)
    b_lo = (b - b_hi.astype(F32)).astype(BF16)
    return _dot(a_hi, b_hi) + (_dot(a_hi, b_lo) + _dot(a_lo, b_hi))


RW_PRE_TM = 512
_MIX_OFF = (0, 512, 1024, 1536, 1664, 1792, 2048, 2176)


def _rwkv_pre_kernel(r_ref, k_ref, v_ref, xw_ref, xa_ref, xg_ref, xv_ref,
                     rp_ref, kp_ref, vp_ref, xwp_ref, xap_ref, xgp_ref, xvp_ref,
                     mix_ref, w0_ref, w2_ref, a0_ref, a2_ref, v0_ref, v2_ref, g2_ref, kk_ref, ka_ref,
                     gsum_ref, vfirst_ref,
                     r_out, lw_out, k_out, v_out, an_out, b_out, g_out, *, use_vres):
    i = pl.program_id(0)

    def mixed(cur_ref, prev_ref, seg):
        cur = cur_ref[...]
        mix = mix_ref[:, _MIX_OFF[seg]:_MIX_OFF[seg + 1]]
        prev_row = jnp.where(i > 0, prev_ref[7:8, :], 0.0)
        rid = lax.broadcasted_iota(jnp.int32, cur.shape, 0)
        shifted = jnp.where(rid == 0, prev_row, pltpu.roll(cur, 1, axis=0))
        return cur + mix * (shifted - cur)

    r = mixed(r_ref, rp_ref, 0)
    k = mixed(k_ref, kp_ref, 1)
    v = mixed(v_ref, vp_ref, 2)
    xw = mixed(xw_ref, xwp_ref, 3)
    xa = mixed(xa_ref, xap_ref, 4)
    xg = mixed(xg_ref, xgp_ref, 5)

    z = -(w0_ref[...] + _dot_f32(jnp.tanh(xw), w2_ref[...]))
    softplus = jnp.maximum(z, 0.0) + jnp.log(1.0 + jnp.exp(-jnp.abs(z)))
    w = -softplus - 0.5
    lw_out[...] = -jnp.exp(w)
    a = jax.nn.sigmoid(a0_ref[...] + _dot_f32(xa, a2_ref[...]))
    g_out[...] = _dot_f32(jax.nn.sigmoid(xg), g2_ref[...])
    if use_vres:
        xv = mixed(xv_ref, xvp_ref, 6)
        v = v + (vfirst_ref[...] - v) * jax.nn.sigmoid(v0_ref[...] + _dot_f32(xv, v2_ref[...]))
    kk = k * kk_ref[...]
    ss = _split_dot(kk * kk, gsum_ref[...], 2)
    kk = kk / jnp.maximum(jnp.sqrt(ss), 1e-12)
    r_out[...] = r
    k_out[...] = k * (1.0 + (a - 1.0) * ka_ref[...])
    v_out[...] = v
    an_out[...] = -kk
    b_out[...] = kk * a


def _rwkv_pre(proj, mix, w0, w2, a0, a2, v0, v2, g2, k_k, k_a, gsum, v_first, use_vres):
    t = proj.shape[0]
    tm = RW_PRE_TM

    def cur(width, off):
        return pl.BlockSpec((tm, width), lambda i: (i, off // width))

    def prev(width, off):
        return pl.BlockSpec((8, width), lambda i: (jnp.maximum(i * (tm // 8) - 1, 0), off // width))

    def full(a):
        return pl.BlockSpec(a.shape, lambda i: (0,) * a.ndim)

    segs = [(RW_W, C_RW_RKV), (RW_W, C_RW_RKV + RW_W), (RW_W, C_RW_RKV + 2 * RW_W),
            (128, C_RW_XW), (128, C_RW_XA), (256, C_RW_XG), (128, C_RW_XV)]
    params = [mix, w0, w2, a0, a2, v0, v2, g2, k_k, k_a, gsum]
    row = pl.BlockSpec((tm, RW_W), lambda i: (i, 0))
    outs = pl.pallas_call(
        functools.partial(_rwkv_pre_kernel, use_vres=use_vres),
        out_shape=[jax.ShapeDtypeStruct((t, RW_W), F32)] * 7,
        grid=(t // tm,),
        in_specs=[cur(w, o) for w, o in segs] + [prev(w, o) for w, o in segs] + [full(p) for p in params] + [row],
        out_specs=[row] * 7,
        compiler_params=_cparams(("parallel",)),
        name="rwkv_pre",
    )(*([proj] * 14), *params, v_first)
    return outs


RW_CHUNK = 128
RW_CHUNKS_PER_STEP = 4


def _wkv_kernel(*refs):
    s_ref = refs[-1]

    @pl.when(pl.program_id(0) == 0)
    def _():
        s_ref[...] = jnp.zeros_like(s_ref)

    for sub in range(RW_CHUNKS_PER_STEP):
        _wkv_chunk(slice(sub * RW_CHUNK, (sub + 1) * RW_CHUNK), *refs)


def _wkv_chunk(rows, r_ref, lw_ref, k_ref, v_ref, an_ref, b_ref, g_ref, lng_ref, lnb_ref, rk_ref,
               gsum_ref, tril_ref, o_ref, s_ref):
    C = RW_CHUNK
    N = RW_N
    r = r_ref[rows, :]
    lw = lw_ref[rows, :]
    k = k_ref[rows, :]
    v = v_ref[rows, :]
    lw_hi = lw.astype(BF16)
    rem = lw - lw_hi.astype(F32)
    lw_mid = rem.astype(BF16)
    lw_lo = (rem - lw_mid.astype(F32)).astype(BF16)
    tril = tril_ref[...]
    lg = _dot(tril, lw_hi) + (_dot(tril, lw_mid) + _dot(tril, lw_lo))
    gam = jnp.exp(lg)
    ginv = jnp.exp(-lg)
    at = (an_ref[rows, :] * jnp.exp(lg - lw)).astype(BF16)
    bt = (b_ref[rows, :] * ginv).astype(BF16)
    kt = (k * ginv).astype(BF16)
    rt_f = r * gam
    rt = rt_f.astype(BF16)
    vb = v.astype(BF16)
    g_last = gam[C - 1:C, :]

    rowi = lax.broadcasted_iota(jnp.int32, (C, C), 0)
    coli = lax.broadcasted_iota(jnp.int32, (C, C), 1)
    strict = rowi > coli
    incl = rowi >= coli

    heads = range(RW_HEADS)
    hs = [slice(h * N, (h + 1) * N) for h in heads]
    bth = [bt[:, s] for s in hs]
    kth = [kt[:, s] for s in hs]
    vh = [vb[:, s] for s in hs]
    big = [_dot_t(jnp.concatenate([at[:, s], rt[:, s]], axis=0), jnp.concatenate([bth[h], kth[h]], axis=0))
           for h, s in zip(heads, hs)]
    a_ab = [jnp.where(strict, big[h][:C, :C], 0.0) for h in heads]
    a_ak = [jnp.where(strict, big[h][:C, C:], 0.0).astype(BF16) for h in heads]
    a_rb = [jnp.where(incl, big[h][C:, :C], 0.0).astype(BF16) for h in heads]
    a_rk = [jnp.where(incl, big[h][C:, C:], 0.0).astype(BF16) for h in heads]
    akv = [_dot(a_ak[h], vh[h]) for h in heads]
    def same_block(size):
        shift = size.bit_length() - 1
        return (rowi >> shift) == (coli >> shift)

    tinv = [jnp.where(same_block(2), a_ab[h], 0.0) + jnp.where(rowi == coli, 1.0, 0.0) for h in heads]
    size = 2
    while size < C:
        lower_left = same_block(2 * size) & jnp.logical_not(same_block(size))
        off = [jnp.where(lower_left, a_ab[h], 0.0).astype(BF16) for h in heads]
        tb = [tinv[h].astype(BF16) for h in heads]
        half = [_dot(tb[h], off[h]).astype(BF16) for h in heads]
        tinv = [tinv[h] + _dot(half[h], tb[h]) for h in heads]
        size *= 2
    xb = [_dot(tinv[h].astype(BF16), jnp.concatenate([at[:, hs[h]], akv[h].astype(BF16)], axis=1)).astype(BF16)
          for h in heads]
    yx = [_dot(a_rb[h], xb[h]) for h in heads]
    ykv = [_dot(a_rk[h], vh[h]) for h in heads]
    xtb = [_dot_tl(xb[h], bth[h]) for h in heads]
    vtk = [_dot_tl(vh[h], kth[h]) for h in heads]
    ys = []
    for h in heads:
        gl = g_last[:, hs[h]]
        y1 = (rt_f[:, hs[h]] + yx[h][:, :N]).astype(BF16)
        s0 = s_ref[h]
        s0b = s0.astype(BF16)
        ys.append(_dot_t(y1, s0b) + (yx[h][:, N:] + ykv[h]))
        s_ref[h] = (s0 + _dot(s0b, xtb[h][:N].astype(BF16)) + (xtb[h][N:] + vtk[h])) * gl
    y = jnp.concatenate(ys, axis=1)

    gsum = gsum_ref[...]
    inv_n = 1.0 / N
    mu = _split_dot(y, gsum, 2) * inv_n
    yc = y - mu
    var = _split_dot(yc * yc, gsum, 2) * inv_n
    yn = yc * lax.rsqrt(var + RW_GN_EPS) * lng_ref[...] + lnb_ref[...]
    bonus = _split_dot(r * k * rk_ref[...], gsum, 2)
    o_ref[rows, :] = ((yn + bonus * v) * g_ref[rows, :]).astype(o_ref.dtype)


def _wkv(r, lw, k, v, an, b, g, ln_g, ln_b, r_k, gsum):
    t = r.shape[0]
    C = RW_CHUNK
    tril = jnp.asarray(np.tril(np.ones((C, C), np.float32)), BF16)
    rows = RW_CHUNKS_PER_STEP * C
    row = pl.BlockSpec((rows, RW_W), lambda c: (c, 0))
    vec = pl.BlockSpec((1, RW_W), lambda c: (0, 0))
    return pl.pallas_call(
        _wkv_kernel,
        out_shape=jax.ShapeDtypeStruct((t, RW_W), BF16),
        grid=(t // rows,),
        in_specs=[row] * 7 + [vec] * 3 + [pl.BlockSpec((RW_W, RW_W), lambda c: (0, 0)),
                                          pl.BlockSpec((C, C), lambda c: (0, 0))],
        out_specs=row,
        scratch_shapes=[pltpu.VMEM((RW_HEADS, RW_N, RW_N), F32)],
        compiler_params=_cparams(("arbitrary",)),
        name="wkv7",
    )(r, lw, k, v, an, b, g, ln_g, ln_b, r_k, gsum, tril)


NSA_PREP_TM = 512


N_NORM_SLABS = (NSA_W + 2 * NSA_KV_W) // LANES
N_RAW_SLABS = 4 * NSA_KV_W // LANES
N_NSA_SLABS = N_NORM_SLABS + N_RAW_SLABS + 1


def _nsa_prep_kernel(*refs):
    x_refs = refs[:N_NSA_SLABS]
    (cos_ref, sin_ref, w_ref, gmean_ref, big_ref,
     kw_ref, hmat_ref, v1_ref, gate_ref, qpad_ref, ksx_ref) = refs[N_NSA_SLABS:]
    cos = cos_ref[...]
    sin = sin_ref[...]
    gmean = gmean_ref[...]
    half = NSA_DH // 2
    tm = cos.shape[0]
    lane = lax.broadcasted_iota(jnp.int32, cos.shape, 1)
    first_half = (lane % NSA_DH) < half
    low = lane < NSA_DH
    blk_row = (pl.program_id(0) * tm + lax.broadcasted_iota(jnp.int32, cos.shape, 0)) // SLC_BLOCK
    big = big_ref[...]
    mark_lo = jnp.where(blk_row == lane - NSA_DH, big, 0.0)
    mark_hi = jnp.where(low & (blk_row == lane + NSA_DH), big, 0.0).astype(BF16)
    n_q = NSA_W // LANES
    n_k = NSA_KV_W // LANES
    for s in range(N_NORM_SLABS):
        x = x_refs[s][...]
        y = x * lax.rsqrt(_split_dot(x * x, gmean, 2) + RMS_EPS) * w_ref[s]
        rot = jnp.where(first_half, pltpu.roll(y, LANES - half, axis=1), pltpu.roll(y, half, axis=1))
        res32 = y * cos + rot * sin
        if s >= n_q + n_k:
            res = res32.astype(kw_ref.dtype)
            kw_ref[2 * (s - n_q - n_k)] = res[:, :NSA_DH]
            kw_ref[2 * (s - n_q - n_k) + 1] = res[:, NSA_DH:]
            continue
        swapped = pltpu.roll(res32, NSA_DH, axis=1)
        if s < n_q:
            qpad_ref[2 * s] = jnp.where(low, res32, 0.0).astype(BF16)
            qpad_ref[2 * s + 1] = jnp.where(low, swapped, 0.0).astype(BF16)
        else:
            g0 = 2 * (s - n_q)
            ksx_ref[g0, :, :LANES] = jnp.where(low, res32, mark_lo).astype(BF16)
            ksx_ref[g0 + 1, :, :LANES] = jnp.where(low, swapped, mark_lo).astype(BF16)
            ksx_ref[g0, :, LANES:] = mark_hi
            ksx_ref[g0 + 1, :, LANES:] = mark_hi

    rows = hmat_ref.shape[2]
    low_r = lax.broadcasted_iota(jnp.int32, (rows, LANES), 1) < NSA_DH
    for s in range(NSA_KV_W // LANES * 2):
        kind, pair = divmod(s, NSA_KV_W // LANES)
        x_ref = x_refs[N_NORM_SLABS + s]
        for t2 in range(CMP_STRIDE // 2):
            r0 = x_ref[pl.ds(2 * t2, rows, stride=CMP_STRIDE), :]
            r1 = x_ref[pl.ds(2 * t2 + 1, rows, stride=CMP_STRIDE), :]
            cols = slice(t2 * LANES, (t2 + 1) * LANES)
            hmat_ref[kind, 2 * pair, :, cols] = jnp.where(low_r, r0, pltpu.roll(r1, NSA_DH, axis=1)).astype(BF16)
            hmat_ref[kind, 2 * pair + 1, :, cols] = jnp.where(low_r, pltpu.roll(r0, NSA_DH, axis=1), r1).astype(BF16)

    one_col = jnp.where(lane == NSA_DH, 1.0, 0.0)
    for s in range(NSA_KV_W // LANES * 2):
        x = x_refs[N_NORM_SLABS + N_RAW_SLABS // 2 + s][...]
        v1_ref[2 * s] = jnp.where(low, x, one_col).astype(BF16)
        v1_ref[2 * s + 1] = jnp.where(low, pltpu.roll(x, NSA_DH, axis=1), one_col).astype(BF16)

    xg = x_refs[N_NSA_SLABS - 1][...]
    per_group = 3 * NSA_HG
    for g in range(NSA_G):
        shifted = xg if g == 0 else pltpu.roll(xg, LANES - per_group * g, axis=1)
        gate_ref[g] = jnp.where(lane < per_group, shifted, 0.0)


def _nsa_prep(proj, w_slabs, cos, sin, gmean, big):
    t = proj.shape[0]
    assert t // SLC_BLOCK <= LANES
    tm = NSA_PREP_TM
    base = C_NSA_Q // LANES
    tab = pl.BlockSpec((tm, LANES), lambda i: (i, 0))
    return pl.pallas_call(
        _nsa_prep_kernel,
        out_shape=[jax.ShapeDtypeStruct((NSA_G, t, NSA_DH), BF16),
                   jax.ShapeDtypeStruct((2, NSA_G, t // CMP_STRIDE, CMP_STRIDE * NSA_DH), BF16),
                   jax.ShapeDtypeStruct((2 * NSA_G, t, LANES), BF16),
                   jax.ShapeDtypeStruct((NSA_G, t, LANES), F32),
                   jax.ShapeDtypeStruct((NSA_HEADS, t, LANES), BF16),
                   jax.ShapeDtypeStruct((NSA_G, t, 2 * LANES), BF16)],
        grid=(t // tm,),
        in_specs=[pl.BlockSpec((tm, LANES), lambda i, s=s: (i, base + s)) for s in range(N_NSA_SLABS)]
        + [tab, tab, pl.BlockSpec((N_NORM_SLABS, 1, LANES), lambda i: (0, 0, 0)),
           pl.BlockSpec((LANES, LANES), lambda i: (0, 0)), pl.BlockSpec((1, LANES), lambda i: (0, 0))],
        out_specs=[pl.BlockSpec((NSA_G, tm, NSA_DH), lambda i: (0, i, 0)),
                   pl.BlockSpec((2, NSA_G, tm // CMP_STRIDE, CMP_STRIDE * NSA_DH), lambda i: (0, 0, i, 0)),
                   pl.BlockSpec((2 * NSA_G, tm, LANES), lambda i: (0, i, 0)),
                   pl.BlockSpec((NSA_G, tm, LANES), lambda i: (0, i, 0)),
                   pl.BlockSpec((NSA_HEADS, tm, LANES), lambda i: (0, i, 0)),
                   pl.BlockSpec((NSA_G, tm, 2 * LANES), lambda i: (0, i, 0))],
        compiler_params=_cparams(("parallel",)),
        name="nsa_prep",
    )(*([proj] * N_NSA_SLABS), cos, sin, w_slabs, gmean, big)


def _compress_kernel(h_ref, w1_ref, w2_ref, pe_ref, nw_ref, cos_ref, sin_ref, o_ref):
    kind = pl.program_id(0)
    nc = h_ref.shape[2]
    half_in = CMP_STRIDE * NSA_DH
    hm = h_ref[0, 0]
    w1 = w1_ref[0]
    first = _dot(hm, w1[:half_in])
    second = _dot(hm, w1[half_in:])
    const = _dot(pe_ref[0], w1)[0:1, :]
    pre = first + pltpu.roll(second, nc - 1, axis=0) + const
    hid = 0.5 * pre * (1.0 + jnp.tanh(0.7978845608028654 * (pre + 0.044715 * pre * pre * pre)))
    out = _dot(hid.astype(BF16), w2_ref[0])
    y = out * lax.rsqrt(jnp.mean(out * out, axis=-1, keepdims=True) + RMS_EPS) * nw_ref[...]
    half = NSA_DH // 2
    rot = jnp.concatenate([y[:, half:], y[:, :half]], axis=1)
    roped = y * cos_ref[...] + rot * sin_ref[...]
    o_ref[0, 0] = jnp.where(kind == 0, roped, out).astype(o_ref.dtype)


def _compress(hmat, w1, w2, pe, nw, cos_c, sin_c):
    _, g, nc, width = hmat.shape
    return pl.pallas_call(
        _compress_kernel,
        out_shape=jax.ShapeDtypeStruct((2, g, nc, NSA_DH), BF16),
        grid=(2, g),
        in_specs=[pl.BlockSpec((1, 1, nc, width), lambda a, b: (a, b, 0, 0)),
                  pl.BlockSpec((1, 2 * width, CMP_HIDDEN), lambda a, b: (a, 0, 0)),
                  pl.BlockSpec((1, CMP_HIDDEN, NSA_DH), lambda a, b: (a, 0, 0)),
                  pl.BlockSpec((1, 8, 2 * width), lambda a, b: (a, 0, 0)),
                  pl.BlockSpec((1, NSA_DH), lambda a, b: (0, 0)),
                  pl.BlockSpec((nc, NSA_DH), lambda a, b: (0, 0)),
                  pl.BlockSpec((nc, NSA_DH), lambda a, b: (0, 0))],
        out_specs=pl.BlockSpec((1, 1, nc, NSA_DH), lambda a, b: (a, b, 0, 0)),
        compiler_params=_cparams(("parallel", "parallel")),
        name="nsa_compress",
    )(hmat, w1, w2, pe, nw, cos_c, sin_c)


SLC_KT = 1024
WIN_ROWS = 128
WIN_KEYS = WINDOW + WIN_ROWS


INT32_MIN = -2 ** 31


def _nsa_attn_kernel(qpad_ref, gate_ref, kc_ref, vc_ref, ks_ref, vs_ref, kw_ref, vw_ref, cis_ref, ltri_ref, o_ref):
    qi = pl.program_id(1)
    nc = kc_ref.shape[2]
    nb = cis_ref.shape[1]
    heads = range(NSA_HG)
    start = qi * NSA_TQ
    qh = [qpad_ref[hh][:, :NSA_DH] for hh in heads]
    tok = start + lax.broadcasted_iota(jnp.int32, (NSA_TQ, 1), 0)

    def softmax_terms(scores):
        return [jnp.exp(s - jnp.max(s, axis=-1, keepdims=True)) for s in scores]

    def cmp_branch(width):
        kc = kc_ref[0, 0, :width, :]
        vc = vc_ref[0, 0, :width, :]
        cend = lax.broadcasted_iota(jnp.int32, (1, width), 1) * CMP_STRIDE + (CMP_BLOCK - 1)
        bias_c = jnp.where(cend <= tok, 0.0, NEG_INF)
        ec = softmax_terms([_dot_t(qh[hh], kc) + bias_c for hh in heads])
        sees_any = tok >= CMP_BLOCK - 1
        pc = [ec[hh] * jnp.where(sees_any, 1.0 / jnp.sum(ec[hh], axis=-1, keepdims=True), 0.0) for hh in heads]
        outs_c = tuple(_dot(pc[hh].astype(BF16), vc) for hh in heads)
        psum = pc[0]
        for hh in range(1, NSA_HG):
            psum = psum + pc[hh]
        return outs_c, _split_dot(psum, cis_ref[:width, :], 2)

    n_quarters = 4
    quarter = nc // n_quarters
    n_visible = (start + NSA_TQ - CMP_BLOCK) // CMP_STRIDE + 1
    which = jnp.clip((n_visible - 1) // quarter, 0, n_quarters - 1)
    o_c, score = lax.switch(which, [functools.partial(cmp_branch, (r + 1) * quarter) for r in range(n_quarters)])
    blk = lax.broadcasted_iota(jnp.int32, (1, nb), 1)
    cur = tok // SLC_BLOCK
    forced = (blk == 0) | (blk == cur) | (blk == cur - 1)
    score = jnp.where(forced, FORCED_SCORE, score)
    score = jnp.where(blk <= cur, score, -jnp.inf)

    bits = lax.bitcast_convert_type(score.T, jnp.int32)
    key_all = bits ^ ((bits >> 31) & 0x7FFFFFFF)

    def select_blocks(nbv):
        key = key_all[:nbv]
        n_sel = min(SLC_TOPK, nbv)

        def enough(c):
            return jnp.sum(jnp.where(key >= c, 1.0, 0.0), axis=0, keepdims=True) >= n_sel

        def enough3(c1, c2, c3):
            packed = jnp.where(key >= c3, 65793.0, jnp.where(key >= c2, 257.0, jnp.where(key >= c1, 1.0, 0.0)))
            tot = jnp.sum(packed, axis=0, keepdims=True).astype(jnp.int32)
            return (tot & 255) >= n_sel, ((tot >> 8) & 255) >= n_sel, (tot >> 16) >= n_sel

        zero_row = jnp.zeros((1, NSA_TQ), jnp.int32)
        thr = jnp.where(enough(zero_row), zero_row, INT32_MIN)
        for hi in range(30, 0, -2):
            c1 = thr + (1 << (hi - 1))
            c2 = thr + (1 << hi)
            c3 = c2 + (1 << (hi - 1))
            e1, e2, e3 = enough3(c1, c2, c3)
            thr = jnp.where(e3, c3, jnp.where(e2, c2, jnp.where(e1, c1, thr)))
        c1 = thr + 1
        thr = jnp.where(enough(c1), c1, thr)
        above = key > thr
        tied = key == thr
        need = n_sel - jnp.sum(jnp.where(above, 1.0, 0.0), axis=0, keepdims=True)
        tied_before = _dot(ltri_ref[:nbv, :nbv], jnp.where(tied, 1.0, 0.0).astype(BF16))
        chosen = jnp.where(above | (tied & (tied_before < need)), 1.0, 0.0)
        if nbv < LANES:
            chosen = jnp.concatenate([chosen, jnp.zeros((LANES - nbv, NSA_TQ), F32)], axis=0)
        return chosen.T

    assert nb <= LANES
    n_steps = 4
    blocks_visible = (start + NSA_TQ - 1) // SLC_BLOCK + 1
    widths = sorted({max(-(-nb * (r + 1) // n_steps // 8) * 8, min(nb, SLC_TOPK)) for r in range(n_steps)})
    which_sel = sum((blocks_visible > w).astype(jnp.int32) for w in widths[:-1])
    sel = lax.switch(which_sel, [functools.partial(select_blocks, w) for w in widths])
    lane_q = lax.broadcasted_iota(jnp.int32, (NSA_TQ, LANES), 1)
    sel_sw = pltpu.roll(sel, NSA_DH, axis=1)
    sel_lo = jnp.where(lane_q >= NSA_DH, sel_sw, 0.0)
    sel_hi = jnp.where(lane_q < NSA_DH, sel_sw, 0.0)
    qx = [jnp.concatenate([qpad_ref[hh].astype(F32) + sel_lo, sel_hi], axis=1).astype(BF16) for hh in heads]

    def window_rows(r0):
        kbase = pl.multiple_of(jnp.maximum(start + r0 - WINDOW, 0), WIN_ROWS)
        kw = kw_ref[0, pl.ds(kbase, WIN_KEYS), :]
        vw = vw_ref[0, pl.ds(kbase, WIN_KEYS), :]
        lag = tok[r0:r0 + WIN_ROWS] - (kbase + lax.broadcasted_iota(jnp.int32, (1, WIN_KEYS), 1))
        bias_w = jnp.where((lag >= 0) & (lag < WINDOW), 0.0, NEG_INF)
        sw = [_dot_t(qh[hh][r0:r0 + WIN_ROWS], kw) + bias_w for hh in heads]
        ew = [jnp.exp((s - jnp.max(s, axis=-1, keepdims=True)).astype(BF16)) for s in sw]
        return [_dot(ew[hh], vw) for hh in heads]

    win_parts = [window_rows(r0) for r0 in range(0, NSA_TQ, WIN_ROWS)]
    acc_w = [jnp.concatenate([part[hh] for part in win_parts], axis=0) for hh in heads]

    n_tiles = (start + NSA_TQ + SLC_KT - 1) // SLC_KT

    def sel_tile(j, carry, causal, width=SLC_KT):
        ms, accs = carry
        k0 = pl.multiple_of(j * SLC_KT, SLC_KT)
        kt = ks_ref[0, pl.ds(k0, width), :]
        vt = vs_ref[0, pl.ds(k0, width), :]
        sj = [_dot_t(qx[hh], kt) for hh in heads]
        if causal:
            future = (k0 + lax.broadcasted_iota(jnp.int32, (1, width), 1)) > tok
            sj = [jnp.where(future, NEG_INF, s) for s in sj]
        m_new = [jnp.maximum(ms[hh], jnp.max(sj[hh], axis=-1, keepdims=True)) for hh in heads]
        pj = [jnp.exp((sj[hh] - m_new[hh]).astype(BF16)) for hh in heads]
        accs = [jnp.exp(ms[hh] - m_new[hh]) * accs[hh] + _dot(pj[hh], vt) for hh in heads]
        return tuple(m_new), tuple(accs)

    m0 = tuple(jnp.full((NSA_TQ, 1), NEG_INF, F32) for _ in heads)
    a0 = tuple(jnp.zeros((NSA_TQ, LANES), F32) for _ in heads)
    carry = lax.fori_loop(0, n_tiles - 1, functools.partial(sel_tile, causal=False), (m0, a0))
    per_tile = SLC_KT // NSA_TQ
    last = [functools.partial(sel_tile, n_tiles - 1, causal=True, width=(r + 1) * NSA_TQ) for r in range(per_tile)]
    _, acc_s = lax.switch(qi % per_tile, last, carry)

    gt = jax.nn.sigmoid(gate_ref[0])
    outs = []
    for hh in heads:
        g0, g1, g2 = (gt[:, 3 * hh + br:3 * hh + br + 1] for br in range(3))
        scale_s = g1 / acc_s[hh][:, NSA_DH:NSA_DH + 1]
        scale_w = g2 / acc_w[hh][:, NSA_DH:NSA_DH + 1]
        outs.append(g0 * o_c[hh] + scale_s * acc_s[hh][:, :NSA_DH] + scale_w * acc_w[hh][:, :NSA_DH])
    o_ref[...] = jnp.concatenate(outs, axis=1).astype(o_ref.dtype)


def _nsa_attn(kw_hm, qpad, ksx, gates, cmp_kv, v1, cis, ltri):
    t = kw_hm.shape[1]
    nq = t // NSA_TQ
    nc = cmp_kv.shape[2]
    nb = cis.shape[1]
    ks_spec = pl.BlockSpec((1, t, 2 * LANES), lambda g, i: (g, 0, 0))
    kw_spec = pl.BlockSpec((1, t, NSA_DH), lambda g, i: (g, 0, 0))
    vs_spec = pl.BlockSpec((1, t, LANES), lambda g, i: (g, 0, 0))
    vw_spec = pl.BlockSpec((1, t, LANES), lambda g, i: (NSA_G + g, 0, 0))
    return pl.pallas_call(
        _nsa_attn_kernel,
        out_shape=jax.ShapeDtypeStruct((t, NSA_W), BF16),
        grid=(NSA_G, nq),
        in_specs=[pl.BlockSpec((NSA_HG, NSA_TQ, LANES), lambda g, i: (g, i, 0)),
                  pl.BlockSpec((1, NSA_TQ, LANES), lambda g, i: (g, i, 0)),
                  pl.BlockSpec((1, 1, nc, NSA_DH), lambda g, i: (0, g, 0, 0)),
                  pl.BlockSpec((1, 1, nc, NSA_DH), lambda g, i: (1, g, 0, 0)),
                  ks_spec, vs_spec, kw_spec, vw_spec,
                  pl.BlockSpec((nc, nb), lambda g, i: (0, 0)),
                  pl.BlockSpec((nb, nb), lambda g, i: (0, 0))],
        out_specs=pl.BlockSpec((NSA_TQ, NSA_HG * NSA_DH), lambda g, i: (i, g)),
        compiler_params=_cparams(("parallel", "arbitrary")),
        name="nsa_attention",
    )(qpad, gates, cmp_kv, cmp_kv, ksx, v1, kw_hm, v1, cis, ltri)


def _rope_tables(pos, dh, reps):
    half = dh // 2
    inv_freq = ROPE_THETA ** (-np.arange(half, dtype=np.float64) / half)
    ang = np.asarray(pos, np.float64)[:, None] * inv_freq[None, :]
    cos = np.cos(ang)
    sin = np.sin(ang)
    cos_t = np.tile(np.concatenate([cos, cos], axis=1), (1, reps))
    sin_t = np.tile(np.concatenate([-sin, sin], axis=1), (1, reps))
    return jnp.asarray(cos_t, F32), jnp.asarray(sin_t, F32)


def _pad_cols(a, width):
    return jnp.pad(a, ((0, 0), (0, width - a.shape[1])))


def _pad_rows(a, height):
    return jnp.pad(a, ((0, height - a.shape[0]), (0, 0)))


def _pack_w_in_kernel(w_ref, xv_ref, o_ref):
    o_ref[...] = _pack_w_in(w_ref[...], xv_ref[...]).astype(o_ref.dtype)


def _pack_w_in_call(w_in, layer, xv):
    _, d, n_in = w_in.shape
    tk = 256
    return pl.pallas_call(
        _pack_w_in_kernel,
        out_shape=jax.ShapeDtypeStruct((d, PROJ_W), BF16),
        grid=(d // tk,),
        in_specs=[pl.BlockSpec((None, tk, n_in), lambda i: (layer, i, 0)),
                  pl.BlockSpec((tk, LANES), lambda i: (i, 0))],
        out_specs=pl.BlockSpec((tk, PROJ_W), lambda i: (i, 0)),
        compiler_params=_cparams(("parallel",)),
        name="pack_w_in",
    )(w_in, xv)


def _pack_w_in(w_l, xv):
    d = w_l.shape[0]
    o = 0
    ret = w_l[:, o:o + 4 * RET_W]; o += 4 * RET_W
    rkv = w_l[:, o:o + 3 * RW_W]; o += 3 * RW_W
    xw = w_l[:, o:o + RW_DECAY_RANK]; o += RW_DECAY_RANK
    xa = w_l[:, o:o + RW_A_RANK]; o += RW_A_RANK
    xg = w_l[:, o:o + RW_GATE_RANK]; o += RW_GATE_RANK
    q = w_l[:, o:o + NSA_W]; o += NSA_W
    kc, vc, ks, vs, kw, vw = (w_l[:, o + i * NSA_KV_W:o + (i + 1) * NSA_KV_W] for i in range(6))
    o += 6 * NSA_KV_W
    gates = w_l[:, o:o + 3 * NSA_HEADS]
    return jnp.concatenate([ret, rkv, _pad_cols(xw, LANES), _pad_cols(xa, LANES), xg, xv,
                            q, ks, kw, kc, vc, vs, vw, _pad_cols(gates, LANES)], axis=1)


def _pack_mix(mix, vres_mix):
    o = 3 * RW_W
    xw = mix[o:o + RW_DECAY_RANK]; o += RW_DECAY_RANK
    xa = mix[o:o + RW_A_RANK]; o += RW_A_RANK
    xg = mix[o:o + RW_GATE_RANK]
    z = lambda n: jnp.zeros((n,), mix.dtype)
    xv = z(LANES) if vres_mix is None else jnp.concatenate([vres_mix, z(LANES - RW_V_RANK)])
    return jnp.concatenate([mix[:3 * RW_W], xw, z(LANES - RW_DECAY_RANK), xa, z(LANES - RW_A_RANK), xg, xv])[None, :]


def kernel(x, ln1_g, w_in, w_in_vres, rwkv_mix, rwkv_vres_mix, rwkv_w0, rwkv_w2, rwkv_a0, rwkv_a2, rwkv_v0, rwkv_v2, rwkv_g2, rwkv_k_k, rwkv_k_a, rwkv_r_k, rwkv_ln_g, rwkv_ln_b, nsa_q_norm, nsa_k_norm, nsa_cmp_pe, nsa_cmp_k_w1, nsa_cmp_k_w2, nsa_cmp_v_w1, nsa_cmp_v_w2, w_out, ln2_g, w_up, w_down):
    bsz, t, d = x.shape
    assert bsz == 1
    depth = w_in.shape[0]
    nc = t // CMP_STRIDE
    nb = t // SLC_BLOCK
    xs = x.reshape(t, d)

    pos = np.arange(t)
    cos_r, sin_r = _rope_tables(pos, RET_DH, 1)
    cos_n, sin_n = _rope_tables(pos, NSA_DH, 2)
    cos_c, sin_c = _rope_tables(np.arange(nc) * CMP_STRIDE + (CMP_BLOCK - 1), NSA_DH, 1)
    lane_head = np.arange(RW_W) // RW_N
    gsum = jnp.asarray(lane_head[:, None] == lane_head[None, :], BF16)
    lane_h2 = np.arange(LANES) // NSA_DH
    gmean = jnp.asarray((lane_h2[:, None] == lane_h2[None, :]) / float(NSA_DH), BF16)
    cstart = np.arange(nc) * CMP_STRIDE
    sstart = np.arange(nb) * SLC_BLOCK
    cis = jnp.asarray((cstart[:, None] <= sstart[None, :] + SLC_BLOCK - 1)
                      & (cstart[:, None] + CMP_BLOCK - 1 >= sstart[None, :]), BF16)

    ltri = jnp.asarray(np.tril(np.ones((nb, nb), np.float32), -1), BF16)

    v_first = jnp.zeros((t, RW_W), F32)
    for l in range(depth):
        vres = l > 0
        xv_w = _pad_cols(w_in_vres[l - 1], LANES) if vres else jnp.zeros((d, LANES), F32)
        w_cat = _pack_w_in_call(w_in, l, xv_w)
        proj = _norm_matmul(xs, ln1_g[l], w_cat, name="in_proj")

        o_ret = _retention(proj, cos_r, sin_r)

        row = lambda a: a.reshape(1, -1)
        mix = _pack_mix(rwkv_mix[l], rwkv_vres_mix[l - 1] if vres else None)
        v0 = row(rwkv_v0[l - 1]) if vres else jnp.zeros((1, RW_W), F32)
        v2 = _pad_rows(rwkv_v2[l - 1], LANES) if vres else jnp.zeros((LANES, RW_W), F32)
        r_, lw_, k_, v_, an_, b_, g_ = _rwkv_pre(
            proj, mix, row(rwkv_w0[l]), _pad_rows(rwkv_w2[l], LANES), row(rwkv_a0[l]), _pad_rows(rwkv_a2[l], LANES),
            v0, v2, rwkv_g2[l], row(rwkv_k_k[l]), row(rwkv_k_a[l]), gsum, v_first, vres)
        if not vres:
            v_first = v_
        o_rwkv = _wkv(r_, lw_, k_, v_, an_, b_, g_, row(rwkv_ln_g[l]), row(rwkv_ln_b[l]), row(rwkv_r_k[l]), gsum)

        n_q = NSA_W // LANES
        n_k = NSA_KV_W // LANES
        w_slabs = jnp.stack([jnp.tile(nsa_q_norm[l] * (NSA_DH ** -0.5), 2)] * n_q
                            + [jnp.tile(nsa_k_norm[l, 1], 2)] * n_k + [jnp.tile(nsa_k_norm[l, 2], 2)] * n_k)[:, None, :]
        s_max = 1.02 * jnp.max(jnp.abs(nsa_q_norm[l])) * jnp.max(jnp.abs(nsa_k_norm[l, 1])) * (NSA_DH ** 0.5)
        big = jnp.full((1, LANES), jnp.exp2(jnp.ceil(jnp.log2(2.0 * s_max + 128.0))), F32)
        kw_hm, hmat, v1, gates, qpad, ksx = _nsa_prep(proj, w_slabs, cos_n, sin_n, gmean, big)
        w1 = jnp.stack([nsa_cmp_k_w1[l], nsa_cmp_v_w1[l]]).astype(BF16)
        w2 = jnp.stack([nsa_cmp_k_w2[l], nsa_cmp_v_w2[l]]).astype(BF16)
        pe = jnp.broadcast_to(nsa_cmp_pe[l].reshape(2, 1, CMP_BLOCK * NSA_DH), (2, 8, CMP_BLOCK * NSA_DH)).astype(BF16)
        cmp_kv = _compress(hmat, w1, w2, pe, row(nsa_k_norm[l, 0]), cos_c, sin_c)
        o_nsa = _nsa_attn(kw_hm, qpad, ksx, gates, cmp_kv, v1, cis, ltri)

        xs = _out_proj(o_ret, o_rwkv, o_nsa, w_out, l, xs)

        up = _norm_matmul(xs, ln2_g[l], w_up, layer=l, act="relu2", out_dtype=BF16, name="mlp_up")
        xs = _matmul(up, w_down, layer=l, res=xs, name="mlp_down")
    return xs.reshape(bsz, t, d)
```

```python
import functools

import numpy as np
import jax
import jax.numpy as jnp
from jax import lax
from jax.experimental import pallas as pl
from jax.experimental.pallas import tpu as pltpu

F32 = jnp.float32
BF16 = jnp.bfloat16

RET_HEADS, RET_DH = 4, 128
RET_W = RET_HEADS * RET_DH
RW_HEADS, RW_N = 8, 64
RW_W = RW_HEADS * RW_N
RW_DECAY_RANK, RW_A_RANK, RW_V_RANK, RW_GATE_RANK = 96, 96, 64, 256
RW_GN_EPS = 64e-5
NSA_HEADS, NSA_G, NSA_DH = 16, 4, 64
NSA_HG = NSA_HEADS // NSA_G
NSA_W = NSA_HEADS * NSA_DH
NSA_KV_W = NSA_G * NSA_DH
CMP_BLOCK, CMP_STRIDE, CMP_HIDDEN = 32, 16, 256
SLC_BLOCK, SLC_TOPK, WINDOW = 64, 16, 512
NSA_TQ = 512
ROPE_THETA = 10000.0
RMS_EPS = 1e-6
NEG_INF = -1e30
FORCED_SCORE = 1e9

LANES = 128
VMEM_LIMIT = 56 * 1024 * 1024

C_RET = 0
C_RW_RKV = 2048
C_RW_XW = 3584
C_RW_XA = 3712
C_RW_XG = 3840
C_RW_XV = 4096
C_NSA_Q = 4224
PROJ_W = 6912


def _cparams(sem):
    return pltpu.CompilerParams(dimension_semantics=sem, vmem_limit_bytes=VMEM_LIMIT)


def _dot(a, b):
    return jnp.dot(a, b, preferred_element_type=F32)


def _dot_t(a, b):
    return lax.dot_general(a, b, (((1,), (1,)), ((), ())), preferred_element_type=F32)


def _dot_tl(a, b):
    return lax.dot_general(a, b, (((0,), (0,)), ((), ())), preferred_element_type=F32)


def _split_dot(x, m_bf16, passes):
    hi = x.astype(BF16)
    acc = _dot(hi, m_bf16)
    rem = x - hi.astype(F32)
    for _ in range(passes - 1):
        piece = rem.astype(BF16)
        acc = acc + _dot(piece, m_bf16)
        rem = rem - piece.astype(F32)
    return acc


def _mm_kernel(*refs, nk, act, has_res):
    a_ref, b_ref = refs[:2]
    r_ref = refs[2] if has_res else None
    o_ref = refs[3] if has_res else refs[2]

    def finish(y):
        if act == "relu2":
            y = jnp.square(jnp.maximum(y, 0.0))
        if has_res:
            y = y + r_ref[...]
        o_ref[...] = y.astype(o_ref.dtype)

    part = _dot(a_ref[...], b_ref[...].astype(BF16))
    if nk == 1:
        finish(part)
        return
    k = pl.program_id(2)
    if act is None and o_ref.dtype == F32:
        @pl.when(k == 0)
        def _():
            o_ref[...] = part + r_ref[...] if has_res else part

        @pl.when(k > 0)
        def _():
            o_ref[...] += part

        return
    acc_ref = refs[-1]

    @pl.when(k == 0)
    def _():
        acc_ref[...] = part

    @pl.when(k > 0)
    def _():
        acc_ref[...] += part

    @pl.when(k == nk - 1)
    def _():
        finish(acc_ref[...])


def _pick(n, cands):
    for c in cands:
        if n % c == 0:
            return c
    raise ValueError(f"no tile for {n}")


def _weight_spec(b, layer, block, index_map):
    if layer is None:
        return pl.BlockSpec(block, index_map)
    return pl.BlockSpec((None,) + block, lambda *idx: (layer,) + index_map(*idx))


def _matmul(a, b, *, layer=None, res=None, act=None, out_dtype=F32, name="matmul"):
    m, kd = a.shape
    n = b.shape[-1]
    tm = _pick(m, (1024, 512, 256, 128))
    tn = _pick(n, (1024, 768, 512, 256, 128))
    tk = _pick(kd, (2048, 1024, 512))
    nk = kd // tk
    in_specs = [pl.BlockSpec((tm, tk), lambda i, j, k: (i, k)),
                _weight_spec(b, layer, (tk, tn), lambda i, j, k: (k, j))]
    args = [a, b]
    if res is not None:
        in_specs.append(pl.BlockSpec((tm, tn), lambda i, j, k: (i, j)))
        args.append(res)
    return pl.pallas_call(
        functools.partial(_mm_kernel, nk=nk, act=act, has_res=res is not None),
        out_shape=jax.ShapeDtypeStruct((m, n), out_dtype),
        grid=(m // tm, n // tn, nk),
        in_specs=in_specs,
        out_specs=pl.BlockSpec((tm, tn), lambda i, j, k: (i, j)),
        scratch_shapes=[pltpu.VMEM((tm, tn), F32)] if nk > 1 and (act is not None or out_dtype != F32) else [],
        compiler_params=_cparams(("parallel", "parallel", "arbitrary")),
        name=name,
    )(*args)


def _norm_mm_kernel(x_ref, g_ref, b_ref, o_ref, h_ref, *, act):
    @pl.when(pl.program_id(1) == 0)
    def _():
        x = x_ref[...]
        ms = jnp.mean(x * x, axis=-1, keepdims=True)
        h_ref[...] = (x * lax.rsqrt(ms + RMS_EPS) * g_ref[...]).astype(BF16)

    y = _dot(h_ref[...], b_ref[...].astype(BF16))
    if act == "relu2":
        y = jnp.square(jnp.maximum(y, 0.0))
    o_ref[...] = y.astype(o_ref.dtype)


def _norm_matmul(x, g, b, *, layer=None, act=None, out_dtype=F32, name="norm_matmul"):
    m, kd = x.shape
    n = b.shape[-1]
    tm = _pick(m, (1024, 512, 256, 128))
    tn = _pick(n, (1024, 768, 512, 256, 128))
    return pl.pallas_call(
        functools.partial(_norm_mm_kernel, act=act),
        out_shape=jax.ShapeDtypeStruct((m, n), out_dtype),
        grid=(m // tm, n // tn),
        in_specs=[pl.BlockSpec((tm, kd), lambda i, j: (i, 0)),
                  pl.BlockSpec((1, kd), lambda i, j: (0, 0)),
                  _weight_spec(b, layer, (kd, tn), lambda i, j: (0, j))],
        out_specs=pl.BlockSpec((tm, tn), lambda i, j: (i, j)),
        scratch_shapes=[pltpu.VMEM((tm, kd), BF16)],
        compiler_params=_cparams(("parallel", "arbitrary")),
        name=name,
    )(x, g.reshape(1, kd), b)


def _out_proj_kernel(ret_ref, rw_ref, nsa_ref, w_ref, res_ref, o_ref, wb_ref):
    @pl.when(pl.program_id(1) == 0)
    def _():
        wb_ref[...] = w_ref[...].astype(BF16)

    k0 = ret_ref.shape[1]
    k1 = k0 + rw_ref.shape[1]
    acc = _dot(ret_ref[...], wb_ref[:k0, :])
    acc = acc + _dot(rw_ref[...], wb_ref[k0:k1, :])
    acc = acc + _dot(nsa_ref[...], wb_ref[k1:, :])
    o_ref[...] = acc + res_ref[...]


def _out_proj(o_ret, o_rwkv, o_nsa, w, layer, res):
    m = o_ret.shape[0]
    kd, n = w.shape[-2:]
    assert kd == o_ret.shape[1] + o_rwkv.shape[1] + o_nsa.shape[1]
    tm = _pick(m, (1024, 512, 256, 128))
    tn = _pick(n, (1024, 512, 256, 128))

    def rows(a):
        return pl.BlockSpec((tm, a.shape[1]), lambda j, i: (i, 0))

    return pl.pallas_call(
        _out_proj_kernel,
        out_shape=jax.ShapeDtypeStruct((m, n), F32),
        grid=(n // tn, m // tm),
        in_specs=[rows(o_ret), rows(o_rwkv), rows(o_nsa),
                  _weight_spec(w, layer, (kd, tn), lambda j, i: (0, j)),
                  pl.BlockSpec((tm, tn), lambda j, i: (i, j))],
        out_specs=pl.BlockSpec((tm, tn), lambda j, i: (i, j)),
        scratch_shapes=[pltpu.VMEM((kd, tn), BF16)],
        compiler_params=_cparams(("parallel", "arbitrary")),
        name="out_proj",
    )(o_ret, o_rwkv, o_nsa, w, res)


RET_CHUNK = 512


def _retention_kernel(q_ref, k_ref, v_ref, g_ref, cos_ref, sin_ref, dmat_ref, qd_ref, kd_ref, cd_ref, o_ref, state_ref):
    @pl.when(pl.program_id(0) == 0)
    def _():
        state_ref[...] = jnp.zeros_like(state_ref)

    cos = cos_ref[...]
    sin = sin_ref[...]

    def rope(x):
        return x * cos + pltpu.roll(x, RET_DH // 2, axis=1) * sin

    heads = range(RET_HEADS)
    hs = [slice(h * RET_DH, (h + 1) * RET_DH) for h in heads]
    q_all, k_all, v_all, gate = q_ref[...], k_ref[...], v_ref[...], g_ref[...]
    qb = [rope(q_all[:, s]).astype(BF16) for s in hs]
    kf = [rope(k_all[:, s]) * (RET_DH ** -0.5) for s in hs]
    kb = [kf[h].astype(BF16) for h in heads]
    vb = [v_all[:, s].astype(BF16) for s in hs]
    inner = [(_dot_t(qb[h], kb[h]) * dmat_ref[h]).astype(BF16) for h in heads]
    cross = [_dot(qb[h], state_ref[h].astype(BF16)) * qd_ref[h] for h in heads]
    kv = [_dot_tl((kf[h] * kd_ref[h]).astype(BF16), vb[h]) for h in heads]
    outs = []
    for h in heads:
        out = _dot(inner[h], vb[h]) + cross[h]
        state_ref[h] = cd_ref[h] * state_ref[h] + kv[h]
        outs.append(out * lax.rsqrt(jnp.mean(out * out, axis=-1, keepdims=True) + RMS_EPS))
    o_ref[...] = (gate * jax.nn.sigmoid(gate) * jnp.concatenate(outs, axis=1)).astype(o_ref.dtype)


def _retention(proj, cos, sin):
    t = proj.shape[0]
    chunk = RET_CHUNK
    log_gamma = np.log(1.0 - 2.0 ** (-5.0 - np.arange(RET_HEADS, dtype=np.float64)))[:, None, None]
    n = np.arange(chunk, dtype=np.float64)
    lag = n[:, None] - n[None, :]
    dmat = np.where(lag >= 0, np.exp(np.maximum(lag, 0.0)[None] * log_gamma), 0.0)
    ones = np.ones((1, 1, RET_DH))
    qd = np.exp((n + 1.0)[None, :, None] * log_gamma) * ones
    kd = np.exp((chunk - 1.0 - n)[None, :, None] * log_gamma) * ones
    cd = np.exp(chunk * log_gamma) * ones
    tables = [jnp.asarray(a, F32) for a in (dmat, qd, kd, cd)]
    base = C_RET // RET_W

    def col(off):
        return pl.BlockSpec((chunk, RET_W), lambda c, off=off: (c, base + off))

    def full(a):
        return pl.BlockSpec(a.shape, lambda c: (0, 0, 0))

    tab = pl.BlockSpec((chunk, RET_DH), lambda c: (c, 0))
    return pl.pallas_call(
        _retention_kernel,
        out_shape=jax.ShapeDtypeStruct((t, RET_W), BF16),
        grid=(t // chunk,),
        in_specs=[col(0), col(1), col(2), col(3), tab, tab] + [full(a) for a in tables],
        out_specs=pl.BlockSpec((chunk, RET_W), lambda c: (c, 0)),
        scratch_shapes=[pltpu.VMEM((RET_HEADS, RET_DH, RET_DH), F32)],
        compiler_params=_cparams(("arbitrary",)),
        name="retention",
    )(proj, proj, proj, proj, cos, sin, *tables)


def _dot_f32(a, b):
    a_hi = a.astype(BF16)
    b_hi = b.astype(BF16)
    a_lo = (a - a_hi.astype(F32)).astype(BF16)
    b_lo = (b - b_hi.astype(F32)).astype(BF16)
    return _dot(a_hi, b_hi) + (_dot(a_hi, b_lo) + _dot(a_lo, b_hi))


RW_PRE_TM = 512
_MIX_OFF = (0, 512, 1024, 1536, 1664, 1792, 2048, 2176)


def _rwkv_pre_kernel(r_ref, k_ref, v_ref, xw_ref, xa_ref, xg_ref, xv_ref,
                     rp_ref, kp_ref, vp_ref, xwp_ref, xap_ref, xgp_ref, xvp_ref,
                     mix_ref, w0_ref, w2_ref, a0_ref, a2_ref, v0_ref, v2_ref, g2_ref, kk_ref, ka_ref,
                     gsum_ref, vfirst_ref,
                     r_out, lw_out, k_out, v_out, an_out, b_out, g_out, *, use_vres):
    i = pl.program_id(0)

    def mixed(cur_ref, prev_ref, seg):
        cur = cur_ref[...]
        mix = mix_ref[:, _MIX_OFF[seg]:_MIX_OFF[seg + 1]]
        prev_row = jnp.where(i > 0, prev_ref[7:8, :], 0.0)
        rid = lax.broadcasted_iota(jnp.int32, cur.shape, 0)
        shifted = jnp.where(rid == 0, prev_row, pltpu.roll(cur, 1, axis=0))
        return cur + mix * (shifted - cur)

    r = mixed(r_ref, rp_ref, 0)
    k = mixed(k_ref, kp_ref, 1)
    v = mixed(v_ref, vp_ref, 2)
    xw = mixed(xw_ref, xwp_ref, 3)
    xa = mixed(xa_ref, xap_ref, 4)
    xg = mixed(xg_ref, xgp_ref, 5)

    z = -(w0_ref[...] + _dot_f32(jnp.tanh(xw), w2_ref[...]))
    softplus = jnp.maximum(z, 0.0) + jnp.log(1.0 + jnp.exp(-jnp.abs(z)))
    w = -softplus - 0.5
    lw_out[...] = -jnp.exp(w)
    a = jax.nn.sigmoid(a0_ref[...] + _dot_f32(xa, a2_ref[...]))
    g_out[...] = _dot_f32(jax.nn.sigmoid(xg), g2_ref[...])
    if use_vres:
        xv = mixed(xv_ref, xvp_ref, 6)
        v = v + (vfirst_ref[...] - v) * jax.nn.sigmoid(v0_ref[...] + _dot_f32(xv, v2_ref[...]))
    kk = k * kk_ref[...]
    ss = _split_dot(kk * kk, gsum_ref[...], 2)
    kk = kk / jnp.maximum(jnp.sqrt(ss), 1e-12)
    r_out[...] = r
    k_out[...] = k * (1.0 + (a - 1.0) * ka_ref[...])
    v_out[...] = v
    an_out[...] = (-kk).astype(an_out.dtype)
    b_out[...] = (kk * a).astype(b_out.dtype)


def _rwkv_pre(proj, mix, w0, w2, a0, a2, v0, v2, g2, k_k, k_a, gsum, v_first, use_vres):
    t = proj.shape[0]
    tm = RW_PRE_TM

    def cur(width, off):
        return pl.BlockSpec((tm, width), lambda i: (i, off // width))

    def prev(width, off):
        return pl.BlockSpec((8, width), lambda i: (jnp.maximum(i * (tm // 8) - 1, 0), off // width))

    def full(a):
        return pl.BlockSpec(a.shape, lambda i: (0,) * a.ndim)

    segs = [(RW_W, C_RW_RKV), (RW_W, C_RW_RKV + RW_W), (RW_W, C_RW_RKV + 2 * RW_W),
            (128, C_RW_XW), (128, C_RW_XA), (256, C_RW_XG), (128, C_RW_XV)]
    params = [mix, w0, w2, a0, a2, v0, v2, g2, k_k, k_a, gsum]
    row = pl.BlockSpec((tm, RW_W), lambda i: (i, 0))
    outs = pl.pallas_call(
        functools.partial(_rwkv_pre_kernel, use_vres=use_vres),
        out_shape=[jax.ShapeDtypeStruct((t, RW_W), dt) for dt in (F32, F32, F32, F32, BF16, BF16, F32)],
        grid=(t // tm,),
        in_specs=[cur(w, o) for w, o in segs] + [prev(w, o) for w, o in segs] + [full(p) for p in params] + [row],
        out_specs=[row] * 7,
        compiler_params=_cparams(("parallel",)),
        name="rwkv_pre",
    )(*([proj] * 14), *params, v_first)
    return outs


RW_CHUNK = 128
RW_CHUNKS_PER_STEP = 4


def _wkv_kernel(*refs):
    s_ref = refs[-1]

    @pl.when(pl.program_id(0) == 0)
    def _():
        s_ref[...] = jnp.zeros_like(s_ref)

    for sub in range(RW_CHUNKS_PER_STEP):
        _wkv_chunk(slice(sub * RW_CHUNK, (sub + 1) * RW_CHUNK), *refs)


def _wkv_chunk(rows, r_ref, lw_ref, k_ref, v_ref, an_ref, b_ref, g_ref, lng_ref, lnb_ref, rk_ref,
               gsum_ref, tril_ref, o_ref, s_ref):
    C = RW_CHUNK
    N = RW_N
    r = r_ref[rows, :]
    lw = lw_ref[rows, :]
    k = k_ref[rows, :]
    v = v_ref[rows, :]
    lw_hi = lw.astype(BF16)
    rem = lw - lw_hi.astype(F32)
    lw_mid = rem.astype(BF16)
    lw_lo = (rem - lw_mid.astype(F32)).astype(BF16)
    tril = tril_ref[...]
    lg = _dot(tril, lw_hi) + (_dot(tril, lw_mid) + _dot(tril, lw_lo))
    gam = jnp.exp(lg)
    ginv = jnp.exp(-lg)
    at = (an_ref[rows, :].astype(F32) * jnp.exp(lg - lw)).astype(BF16)
    bt = (b_ref[rows, :].astype(F32) * ginv).astype(BF16)
    kt = (k * ginv).astype(BF16)
    rt_f = r * gam
    rt = rt_f.astype(BF16)
    vb = v.astype(BF16)
    g_last = gam[C - 1:C, :]

    rowi = lax.broadcasted_iota(jnp.int32, (C, C), 0)
    coli = lax.broadcasted_iota(jnp.int32, (C, C), 1)
    strict = rowi > coli
    incl = rowi >= coli

    heads = range(RW_HEADS)
    hs = [slice(h * N, (h + 1) * N) for h in heads]
    bth = [bt[:, s] for s in hs]
    kth = [kt[:, s] for s in hs]
    vh = [vb[:, s] for s in hs]
    big = [_dot_t(jnp.concatenate([at[:, s], rt[:, s]], axis=0), jnp.concatenate([bth[h], kth[h]], axis=0))
           for h, s in zip(heads, hs)]
    a_ab = [jnp.where(strict, big[h][:C, :C], 0.0) for h in heads]
    a_ak = [jnp.where(strict, big[h][:C, C:], 0.0).astype(BF16) for h in heads]
    a_rb = [jnp.where(incl, big[h][C:, :C], 0.0).astype(BF16) for h in heads]
    a_rk = [jnp.where(incl, big[h][C:, C:], 0.0).astype(BF16) for h in heads]
    akv = [_dot(a_ak[h], vh[h]) for h in heads]
    def same_block(size):
        shift = size.bit_length() - 1
        return (rowi >> shift) == (coli >> shift)

    tinv = [jnp.where(same_block(2), a_ab[h], 0.0) + jnp.where(rowi == coli, 1.0, 0.0) for h in heads]
    size = 2
    while size < C:
        lower_left = same_block(2 * size) & jnp.logical_not(same_block(size))
        off = [jnp.where(lower_left, a_ab[h], 0.0).astype(BF16) for h in heads]
        tb = [tinv[h].astype(BF16) for h in heads]
        half = [_dot(tb[h], off[h]).astype(BF16) for h in heads]
        tinv = [tinv[h] + _dot(half[h], tb[h]) for h in heads]
        size *= 2
    xb = [_dot(tinv[h].astype(BF16), jnp.concatenate([at[:, hs[h]], akv[h].astype(BF16)], axis=1)).astype(BF16)
          for h in heads]
    yx = [_dot(a_rb[h], xb[h]) for h in heads]
    ykv = [_dot(a_rk[h], vh[h]) for h in heads]
    xtb = [_dot_tl(xb[h], bth[h]) for h in heads]
    vtk = [_dot_tl(vh[h], kth[h]) for h in heads]
    ys = []
    for h in heads:
        gl = g_last[:, hs[h]]
        y1 = (rt_f[:, hs[h]] + yx[h][:, :N]).astype(BF16)
        s0 = s_ref[h]
        s0b = s0.astype(BF16)
        ys.append(_dot_t(y1, s0b) + (yx[h][:, N:] + ykv[h]))
        s_ref[h] = (s0 + _dot(s0b, xtb[h][:N].astype(BF16)) + (xtb[h][N:] + vtk[h])) * gl
    y = jnp.concatenate(ys, axis=1)

    gsum = gsum_ref[...]
    inv_n = 1.0 / N
    mu = _split_dot(y, gsum, 2) * inv_n
    yc = y - mu
    var = _split_dot(yc * yc, gsum, 2) * inv_n
    yn = yc * lax.rsqrt(var + RW_GN_EPS) * lng_ref[...] + lnb_ref[...]
    bonus = _split_dot(r * k * rk_ref[...], gsum, 2)
    o_ref[rows, :] = ((yn + bonus * v) * g_ref[rows, :]).astype(o_ref.dtype)


def _wkv(r, lw, k, v, an, b, g, ln_g, ln_b, r_k, gsum):
    t = r.shape[0]
    C = RW_CHUNK
    tril = jnp.asarray(np.tril(np.ones((C, C), np.float32)), BF16)
    rows = RW_CHUNKS_PER_STEP * C
    row = pl.BlockSpec((rows, RW_W), lambda c: (c, 0))
    vec = pl.BlockSpec((1, RW_W), lambda c: (0, 0))
    return pl.pallas_call(
        _wkv_kernel,
        out_shape=jax.ShapeDtypeStruct((t, RW_W), BF16),
        grid=(t // rows,),
        in_specs=[row] * 7 + [vec] * 3 + [pl.BlockSpec((RW_W, RW_W), lambda c: (0, 0)),
                                          pl.BlockSpec((C, C), lambda c: (0, 0))],
        out_specs=row,
        scratch_shapes=[pltpu.VMEM((RW_HEADS, RW_N, RW_N), F32)],
        compiler_params=_cparams(("arbitrary",)),
        name="wkv7",
    )(r, lw, k, v, an, b, g, ln_g, ln_b, r_k, gsum, tril)


NSA_PREP_TM = 512


N_NORM_SLABS = (NSA_W + 2 * NSA_KV_W) // LANES
N_RAW_SLABS = 4 * NSA_KV_W // LANES
N_NSA_SLABS = N_NORM_SLABS + N_RAW_SLABS + 1


def _nsa_prep_kernel(*refs):
    x_refs = refs[:N_NSA_SLABS]
    (cos_ref, sin_ref, w_ref, gmean_ref, big_ref,
     kw_ref, hmat_ref, v1_ref, gate_ref, qpad_ref, ksx_ref) = refs[N_NSA_SLABS:]
    cos = cos_ref[...]
    sin = sin_ref[...]
    gmean = gmean_ref[...]
    half = NSA_DH // 2
    tm = cos.shape[0]
    lane = lax.broadcasted_iota(jnp.int32, cos.shape, 1)
    first_half = (lane % NSA_DH) < half
    low = lane < NSA_DH
    blk_row = (pl.program_id(0) * tm + lax.broadcasted_iota(jnp.int32, cos.shape, 0)) // SLC_BLOCK
    big = big_ref[...]
    mark_lo = jnp.where(blk_row == lane - NSA_DH, big, 0.0)
    mark_hi = jnp.where(low & (blk_row == lane + NSA_DH), big, 0.0).astype(BF16)
    n_q = NSA_W // LANES
    n_k = NSA_KV_W // LANES
    for s in range(N_NORM_SLABS):
        x = x_refs[s][...]
        y = x * lax.rsqrt(_split_dot(x * x, gmean, 2) + RMS_EPS) * w_ref[s]
        rot = jnp.where(first_half, pltpu.roll(y, LANES - half, axis=1), pltpu.roll(y, half, axis=1))
        res32 = y * cos + rot * sin
        if s >= n_q + n_k:
            res = res32.astype(kw_ref.dtype)
            kw_ref[2 * (s - n_q - n_k)] = res[:, :NSA_DH]
            kw_ref[2 * (s - n_q - n_k) + 1] = res[:, NSA_DH:]
            continue
        swapped = pltpu.roll(res32, NSA_DH, axis=1)
        if s < n_q:
            qpad_ref[2 * s] = jnp.where(low, res32, 0.0).astype(BF16)
            qpad_ref[2 * s + 1] = jnp.where(low, swapped, 0.0).astype(BF16)
        else:
            g0 = 2 * (s - n_q)
            ksx_ref[g0, :, :LANES] = jnp.where(low, res32, mark_lo).astype(BF16)
            ksx_ref[g0 + 1, :, :LANES] = jnp.where(low, swapped, mark_lo).astype(BF16)
            ksx_ref[g0, :, LANES:] = mark_hi
            ksx_ref[g0 + 1, :, LANES:] = mark_hi

    rows = hmat_ref.shape[2]
    low_r = lax.broadcasted_iota(jnp.int32, (rows, LANES), 1) < NSA_DH
    for s in range(NSA_KV_W // LANES * 2):
        kind, pair = divmod(s, NSA_KV_W // LANES)
        x_ref = x_refs[N_NORM_SLABS + s]
        for t2 in range(CMP_STRIDE // 2):
            r0 = x_ref[pl.ds(2 * t2, rows, stride=CMP_STRIDE), :]
            r1 = x_ref[pl.ds(2 * t2 + 1, rows, stride=CMP_STRIDE), :]
            cols = slice(t2 * LANES, (t2 + 1) * LANES)
            hmat_ref[kind, 2 * pair, :, cols] = jnp.where(low_r, r0, pltpu.roll(r1, NSA_DH, axis=1)).astype(BF16)
            hmat_ref[kind, 2 * pair + 1, :, cols] = jnp.where(low_r, pltpu.roll(r0, NSA_DH, axis=1), r1).astype(BF16)

    one_col = jnp.where(lane == NSA_DH, 1.0, 0.0)
    for s in range(NSA_KV_W // LANES * 2):
        x = x_refs[N_NORM_SLABS + N_RAW_SLABS // 2 + s][...]
        v1_ref[2 * s] = jnp.where(low, x, one_col).astype(BF16)
        v1_ref[2 * s + 1] = jnp.where(low, pltpu.roll(x, NSA_DH, axis=1), one_col).astype(BF16)

    xg = x_refs[N_NSA_SLABS - 1][...]
    per_group = 3 * NSA_HG
    for g in range(NSA_G):
        shifted = xg if g == 0 else pltpu.roll(xg, LANES - per_group * g, axis=1)
        gate_ref[g] = jnp.where(lane < per_group, shifted, 0.0)


def _nsa_prep(proj, w_slabs, cos, sin, gmean, big):
    t = proj.shape[0]
    assert t // SLC_BLOCK <= LANES
    tm = NSA_PREP_TM
    base = C_NSA_Q // LANES
    tab = pl.BlockSpec((tm, LANES), lambda i: (i, 0))
    return pl.pallas_call(
        _nsa_prep_kernel,
        out_shape=[jax.ShapeDtypeStruct((NSA_G, t, NSA_DH), BF16),
                   jax.ShapeDtypeStruct((2, NSA_G, t // CMP_STRIDE, CMP_STRIDE * NSA_DH), BF16),
                   jax.ShapeDtypeStruct((2 * NSA_G, t, LANES), BF16),
                   jax.ShapeDtypeStruct((NSA_G, t, LANES), F32),
                   jax.ShapeDtypeStruct((NSA_HEADS, t, LANES), BF16),
                   jax.ShapeDtypeStruct((NSA_G, t, 2 * LANES), BF16)],
        grid=(t // tm,),
        in_specs=[pl.BlockSpec((tm, LANES), lambda i, s=s: (i, base + s)) for s in range(N_NSA_SLABS)]
        + [tab, tab, pl.BlockSpec((N_NORM_SLABS, 1, LANES), lambda i: (0, 0, 0)),
           pl.BlockSpec((LANES, LANES), lambda i: (0, 0)), pl.BlockSpec((1, LANES), lambda i: (0, 0))],
        out_specs=[pl.BlockSpec((NSA_G, tm, NSA_DH), lambda i: (0, i, 0)),
                   pl.BlockSpec((2, NSA_G, tm // CMP_STRIDE, CMP_STRIDE * NSA_DH), lambda i: (0, 0, i, 0)),
                   pl.BlockSpec((2 * NSA_G, tm, LANES), lambda i: (0, i, 0)),
                   pl.BlockSpec((NSA_G, tm, LANES), lambda i: (0, i, 0)),
                   pl.BlockSpec((NSA_HEADS, tm, LANES), lambda i: (0, i, 0)),
                   pl.BlockSpec((NSA_G, tm, 2 * LANES), lambda i: (0, i, 0))],
        compiler_params=_cparams(("parallel",)),
        name="nsa_prep",
    )(*([proj] * N_NSA_SLABS), cos, sin, w_slabs, gmean, big)


def _compress_kernel(h_ref, w1_ref, w2_ref, pe_ref, nw_ref, cos_ref, sin_ref, o_ref):
    kind = pl.program_id(0)
    nc = h_ref.shape[2]
    half_in = CMP_STRIDE * NSA_DH
    hm = h_ref[0, 0]
    w1 = w1_ref[0]
    first = _dot(hm, w1[:half_in])
    second = _dot(hm, w1[half_in:])
    const = _dot(pe_ref[0], w1)[0:1, :]
    pre = first + pltpu.roll(second, nc - 1, axis=0) + const
    hid = 0.5 * pre * (1.0 + jnp.tanh(0.7978845608028654 * (pre + 0.044715 * pre * pre * pre)))
    out = _dot(hid.astype(BF16), w2_ref[0])
    y = out * lax.rsqrt(jnp.mean(out * out, axis=-1, keepdims=True) + RMS_EPS) * nw_ref[...]
    half = NSA_DH // 2
    rot = jnp.concatenate([y[:, half:], y[:, :half]], axis=1)
    roped = y * cos_ref[...] + rot * sin_ref[...]
    o_ref[0, 0] = jnp.where(kind == 0, roped, out).astype(o_ref.dtype)


def _compress(hmat, w1, w2, pe, nw, cos_c, sin_c):
    _, g, nc, width = hmat.shape
    return pl.pallas_call(
        _compress_kernel,
        out_shape=jax.ShapeDtypeStruct((2, g, nc, NSA_DH), BF16),
        grid=(2, g),
        in_specs=[pl.BlockSpec((1, 1, nc, width), lambda a, b: (a, b, 0, 0)),
                  pl.BlockSpec((1, 2 * width, CMP_HIDDEN), lambda a, b: (a, 0, 0)),
                  pl.BlockSpec((1, CMP_HIDDEN, NSA_DH), lambda a, b: (a, 0, 0)),
                  pl.BlockSpec((1, 8, 2 * width), lambda a, b: (a, 0, 0)),
                  pl.BlockSpec((1, NSA_DH), lambda a, b: (0, 0)),
                  pl.BlockSpec((nc, NSA_DH), lambda a, b: (0, 0)),
                  pl.BlockSpec((nc, NSA_DH), lambda a, b: (0, 0))],
        out_specs=pl.BlockSpec((1, 1, nc, NSA_DH), lambda a, b: (a, b, 0, 0)),
        compiler_params=_cparams(("parallel", "parallel")),
        name="nsa_compress",
    )(hmat, w1, w2, pe, nw, cos_c, sin_c)


SLC_KT = 1024
WIN_ROWS = 128
WIN_KEYS = WINDOW + WIN_ROWS


INT32_MIN = -2 ** 31


def _nsa_attn_kernel(qpad_ref, gate_ref, kc_ref, vc_ref, ks_ref, vs_ref, kw_ref, vw_ref, cis_ref, ltri_ref, o_ref):
    qi = pl.program_id(1)
    nc = kc_ref.shape[2]
    nb = cis_ref.shape[1]
    heads = range(NSA_HG)
    start = qi * NSA_TQ
    qh = [qpad_ref[hh][:, :NSA_DH] for hh in heads]
    tok = start + lax.broadcasted_iota(jnp.int32, (NSA_TQ, 1), 0)

    def softmax_terms(scores):
        return [jnp.exp(s - jnp.max(s, axis=-1, keepdims=True)) for s in scores]

    def cmp_branch(width):
        kc = kc_ref[0, 0, :width, :]
        vc = vc_ref[0, 0, :width, :]
        cend = lax.broadcasted_iota(jnp.int32, (1, width), 1) * CMP_STRIDE + (CMP_BLOCK - 1)
        bias_c = jnp.where(cend <= tok, 0.0, NEG_INF)
        ec = softmax_terms([_dot_t(qh[hh], kc) + bias_c for hh in heads])
        sees_any = tok >= CMP_BLOCK - 1
        pc = [ec[hh] * jnp.where(sees_any, 1.0 / jnp.sum(ec[hh], axis=-1, keepdims=True), 0.0) for hh in heads]
        outs_c = tuple(_dot(pc[hh].astype(BF16), vc) for hh in heads)
        psum = pc[0]
        for hh in range(1, NSA_HG):
            psum = psum + pc[hh]
        return outs_c, _split_dot(psum, cis_ref[:width, :], 2)

    n_quarters = 4
    quarter = nc // n_quarters
    n_visible = (start + NSA_TQ - CMP_BLOCK) // CMP_STRIDE + 1
    which = jnp.clip((n_visible - 1) // quarter, 0, n_quarters - 1)
    o_c, score = lax.switch(which, [functools.partial(cmp_branch, (r + 1) * quarter) for r in range(n_quarters)])
    blk = lax.broadcasted_iota(jnp.int32, (1, nb), 1)
    cur = tok // SLC_BLOCK
    forced = (blk == 0) | (blk == cur) | (blk == cur - 1)
    score = jnp.where(forced, FORCED_SCORE, score)
    score = jnp.where(blk <= cur, score, -jnp.inf)

    bits = lax.bitcast_convert_type(score.T, jnp.int32)
    key_all = bits ^ ((bits >> 31) & 0x7FFFFFFF)

    def select_blocks(nbv):
        key = key_all[:nbv]
        n_sel = min(SLC_TOPK, nbv)

        def enough(c):
            return jnp.sum(jnp.where(key >= c, 1.0, 0.0), axis=0, keepdims=True) >= n_sel

        def enough3(c1, c2, c3):
            packed = jnp.where(key >= c3, 65793.0, jnp.where(key >= c2, 257.0, jnp.where(key >= c1, 1.0, 0.0)))
            tot = jnp.sum(packed, axis=0, keepdims=True).astype(jnp.int32)
            return (tot & 255) >= n_sel, ((tot >> 8) & 255) >= n_sel, (tot >> 16) >= n_sel

        zero_row = jnp.zeros((1, NSA_TQ), jnp.int32)
        thr = jnp.where(enough(zero_row), zero_row, INT32_MIN)
        for hi in range(30, 0, -2):
            c1 = thr + (1 << (hi - 1))
            c2 = thr + (1 << hi)
            c3 = c2 + (1 << (hi - 1))
            e1, e2, e3 = enough3(c1, c2, c3)
            thr = jnp.where(e3, c3, jnp.where(e2, c2, jnp.where(e1, c1, thr)))
        c1 = thr + 1
        thr = jnp.where(enough(c1), c1, thr)
        above = key > thr
        tied = key == thr
        need = n_sel - jnp.sum(jnp.where(above, 1.0, 0.0), axis=0, keepdims=True)
        tied_before = _dot(ltri_ref[:nbv, :nbv], jnp.where(tied, 1.0, 0.0).astype(BF16))
        chosen = jnp.where(above | (tied & (tied_before < need)), 1.0, 0.0)
        if nbv < LANES:
            chosen = jnp.concatenate([chosen, jnp.zeros((LANES - nbv, NSA_TQ), F32)], axis=0)
        return chosen.T

    assert nb <= LANES
    n_steps = 4
    blocks_visible = (start + NSA_TQ - 1) // SLC_BLOCK + 1
    widths = sorted({max(-(-nb * (r + 1) // n_steps // 8) * 8, min(nb, SLC_TOPK)) for r in range(n_steps)})
    which_sel = sum((blocks_visible > w).astype(jnp.int32) for w in widths[:-1])
    sel = lax.switch(which_sel, [functools.partial(select_blocks, w) for w in widths])
    lane_q = lax.broadcasted_iota(jnp.int32, (NSA_TQ, LANES), 1)
    sel_sw = pltpu.roll(sel, NSA_DH, axis=1)
    sel_lo = jnp.where(lane_q >= NSA_DH, sel_sw, 0.0)
    sel_hi = jnp.where(lane_q < NSA_DH, sel_sw, 0.0)
    qx = [jnp.concatenate([qpad_ref[hh].astype(F32) + sel_lo, sel_hi], axis=1).astype(BF16) for hh in heads]

    def window_rows(r0):
        kbase = pl.multiple_of(jnp.maximum(start + r0 - WINDOW, 0), WIN_ROWS)
        kw = kw_ref[0, pl.ds(kbase, WIN_KEYS), :]
        vw = vw_ref[0, pl.ds(kbase, WIN_KEYS), :]
        lag = tok[r0:r0 + WIN_ROWS] - (kbase + lax.broadcasted_iota(jnp.int32, (1, WIN_KEYS), 1))
        bias_w = jnp.where((lag >= 0) & (lag < WINDOW), 0.0, NEG_INF)
        sw = [_dot_t(qh[hh][r0:r0 + WIN_ROWS], kw) + bias_w for hh in heads]
        ew = [jnp.exp((s - jnp.max(s, axis=-1, keepdims=True)).astype(BF16)) for s in sw]
        return [_dot(ew[hh], vw) for hh in heads]

    win_parts = [window_rows(r0) for r0 in range(0, NSA_TQ, WIN_ROWS)]
    acc_w = [jnp.concatenate([part[hh] for part in win_parts], axis=0) for hh in heads]

    n_tiles = (start + NSA_TQ + SLC_KT - 1) // SLC_KT

    def sel_tile(j, carry, causal, width=SLC_KT):
        ms, accs = carry
        k0 = pl.multiple_of(j * SLC_KT, SLC_KT)
        kt = ks_ref[0, pl.ds(k0, width), :]
        vt = vs_ref[0, pl.ds(k0, width), :]
        sj = [_dot_t(qx[hh], kt) for hh in heads]
        if causal:
            future = (k0 + lax.broadcasted_iota(jnp.int32, (1, width), 1)) > tok
            sj = [jnp.where(future, NEG_INF, s) for s in sj]
        m_new = [jnp.maximum(ms[hh], jnp.max(sj[hh], axis=-1, keepdims=True)) for hh in heads]
        pj = [jnp.exp((sj[hh] - m_new[hh]).astype(BF16)) for hh in heads]
        accs = [jnp.exp(ms[hh] - m_new[hh]) * accs[hh] + _dot(pj[hh], vt) for hh in heads]
        return tuple(m_new), tuple(accs)

    m0 = tuple(jnp.full((NSA_TQ, 1), NEG_INF, F32) for _ in heads)
    a0 = tuple(jnp.zeros((NSA_TQ, LANES), F32) for _ in heads)
    carry = lax.fori_loop(0, n_tiles - 1, functools.partial(sel_tile, causal=False), (m0, a0))
    per_tile = SLC_KT // NSA_TQ
    last = [functools.partial(sel_tile, n_tiles - 1, causal=True, width=(r + 1) * NSA_TQ) for r in range(per_tile)]
    _, acc_s = lax.switch(qi % per_tile, last, carry)

    gt = jax.nn.sigmoid(gate_ref[0])
    outs = []
    for hh in heads:
        g0, g1, g2 = (gt[:, 3 * hh + br:3 * hh + br + 1] for br in range(3))
        scale_s = g1 / acc_s[hh][:, NSA_DH:NSA_DH + 1]
        scale_w = g2 / acc_w[hh][:, NSA_DH:NSA_DH + 1]
        outs.append(g0 * o_c[hh] + scale_s * acc_s[hh][:, :NSA_DH] + scale_w * acc_w[hh][:, :NSA_DH])
    o_ref[...] = jnp.concatenate(outs, axis=1).astype(o_ref.dtype)


def _nsa_attn(kw_hm, qpad, ksx, gates, cmp_kv, v1, cis, ltri):
    t = kw_hm.shape[1]
    nq = t // NSA_TQ
    nc = cmp_kv.shape[2]
    nb = cis.shape[1]
    ks_spec = pl.BlockSpec((1, t, 2 * LANES), lambda g, i: (g, 0, 0))
    kw_spec = pl.BlockSpec((1, t, NSA_DH), lambda g, i: (g, 0, 0))
    vs_spec = pl.BlockSpec((1, t, LANES), lambda g, i: (g, 0, 0))
    vw_spec = pl.BlockSpec((1, t, LANES), lambda g, i: (NSA_G + g, 0, 0))
    return pl.pallas_call(
        _nsa_attn_kernel,
        out_shape=jax.ShapeDtypeStruct((t, NSA_W), BF16),
        grid=(NSA_G, nq),
        in_specs=[pl.BlockSpec((NSA_HG, NSA_TQ, LANES), lambda g, i: (g, i, 0)),
                  pl.BlockSpec((1, NSA_TQ, LANES), lambda g, i: (g, i, 0)),
                  pl.BlockSpec((1, 1, nc, NSA_DH), lambda g, i: (0, g, 0, 0)),
                  pl.BlockSpec((1, 1, nc, NSA_DH), lambda g, i: (1, g, 0, 0)),
                  ks_spec, vs_spec, kw_spec, vw_spec,
                  pl.BlockSpec((nc, nb), lambda g, i: (0, 0)),
                  pl.BlockSpec((nb, nb), lambda g, i: (0, 0))],
        out_specs=pl.BlockSpec((NSA_TQ, NSA_HG * NSA_DH), lambda g, i: (i, g)),
        compiler_params=_cparams(("parallel", "arbitrary")),
        name="nsa_attention",
    )(qpad, gates, cmp_kv, cmp_kv, ksx, v1, kw_hm, v1, cis, ltri)


def _rope_tables(pos, dh, reps):
    half = dh // 2
    inv_freq = ROPE_THETA ** (-np.arange(half, dtype=np.float64) / half)
    ang = np.asarray(pos, np.float64)[:, None] * inv_freq[None, :]
    cos = np.cos(ang)
    sin = np.sin(ang)
    cos_t = np.tile(np.concatenate([cos, cos], axis=1), (1, reps))
    sin_t = np.tile(np.concatenate([-sin, sin], axis=1), (1, reps))
    return jnp.asarray(cos_t, F32), jnp.asarray(sin_t, F32)


def _pad_cols(a, width):
    return jnp.pad(a, ((0, 0), (0, width - a.shape[1])))


def _pad_rows(a, height):
    return jnp.pad(a, ((0, height - a.shape[0]), (0, 0)))


def _pack_w_in_kernel(w_ref, xv_ref, o_ref):
    o_ref[...] = _pack_w_in(w_ref[...], xv_ref[...]).astype(o_ref.dtype)


def _pack_w_in_call(w_in, layer, xv):
    _, d, n_in = w_in.shape
    tk = 256
    return pl.pallas_call(
        _pack_w_in_kernel,
        out_shape=jax.ShapeDtypeStruct((d, PROJ_W), BF16),
        grid=(d // tk,),
        in_specs=[pl.BlockSpec((None, tk, n_in), lambda i: (layer, i, 0)),
                  pl.BlockSpec((tk, LANES), lambda i: (i, 0))],
        out_specs=pl.BlockSpec((tk, PROJ_W), lambda i: (i, 0)),
        compiler_params=_cparams(("parallel",)),
        name="pack_w_in",
    )(w_in, xv)


def _pack_w_in(w_l, xv):
    d = w_l.shape[0]
    o = 0
    ret = w_l[:, o:o + 4 * RET_W]; o += 4 * RET_W
    rkv = w_l[:, o:o + 3 * RW_W]; o += 3 * RW_W
    xw = w_l[:, o:o + RW_DECAY_RANK]; o += RW_DECAY_RANK
    xa = w_l[:, o:o + RW_A_RANK]; o += RW_A_RANK
    xg = w_l[:, o:o + RW_GATE_RANK]; o += RW_GATE_RANK
    q = w_l[:, o:o + NSA_W]; o += NSA_W
    kc, vc, ks, vs, kw, vw = (w_l[:, o + i * NSA_KV_W:o + (i + 1) * NSA_KV_W] for i in range(6))
    o += 6 * NSA_KV_W
    gates = w_l[:, o:o + 3 * NSA_HEADS]
    return jnp.concatenate([ret, rkv, _pad_cols(xw, LANES), _pad_cols(xa, LANES), xg, xv,
                            q, ks, kw, kc, vc, vs, vw, _pad_cols(gates, LANES)], axis=1)


def _pack_mix(mix, vres_mix):
    o = 3 * RW_W
    xw = mix[o:o + RW_DECAY_RANK]; o += RW_DECAY_RANK
    xa = mix[o:o + RW_A_RANK]; o += RW_A_RANK
    xg = mix[o:o + RW_GATE_RANK]
    z = lambda n: jnp.zeros((n,), mix.dtype)
    xv = z(LANES) if vres_mix is None else jnp.concatenate([vres_mix, z(LANES - RW_V_RANK)])
    return jnp.concatenate([mix[:3 * RW_W], xw, z(LANES - RW_DECAY_RANK), xa, z(LANES - RW_A_RANK), xg, xv])[None, :]


def kernel(x, ln1_g, w_in, w_in_vres, rwkv_mix, rwkv_vres_mix, rwkv_w0, rwkv_w2, rwkv_a0, rwkv_a2, rwkv_v0, rwkv_v2, rwkv_g2, rwkv_k_k, rwkv_k_a, rwkv_r_k, rwkv_ln_g, rwkv_ln_b, nsa_q_norm, nsa_k_norm, nsa_cmp_pe, nsa_cmp_k_w1, nsa_cmp_k_w2, nsa_cmp_v_w1, nsa_cmp_v_w2, w_out, ln2_g, w_up, w_down):
    bsz, t, d = x.shape
    assert bsz == 1
    depth = w_in.shape[0]
    nc = t // CMP_STRIDE
    nb = t // SLC_BLOCK
    xs = x.reshape(t, d)

    pos = np.arange(t)
    cos_r, sin_r = _rope_tables(pos, RET_DH, 1)
    cos_n, sin_n = _rope_tables(pos, NSA_DH, 2)
    cos_c, sin_c = _rope_tables(np.arange(nc) * CMP_STRIDE + (CMP_BLOCK - 1), NSA_DH, 1)
    lane_head = np.arange(RW_W) // RW_N
    gsum = jnp.asarray(lane_head[:, None] == lane_head[None, :], BF16)
    lane_h2 = np.arange(LANES) // NSA_DH
    gmean = jnp.asarray((lane_h2[:, None] == lane_h2[None, :]) / float(NSA_DH), BF16)
    cstart = np.arange(nc) * CMP_STRIDE
    sstart = np.arange(nb) * SLC_BLOCK
    cis = jnp.asarray((cstart[:, None] <= sstart[None, :] + SLC_BLOCK - 1)
                      & (cstart[:, None] + CMP_BLOCK - 1 >= sstart[None, :]), BF16)

    ltri = jnp.asarray(np.tril(np.ones((nb, nb), np.float32), -1), BF16)

    v_first = jnp.zeros((t, RW_W), F32)
    for l in range(depth):
        vres = l > 0
        xv_w = _pad_cols(w_in_vres[l - 1], LANES) if vres else jnp.zeros((d, LANES), F32)
        w_cat = _pack_w_in_call(w_in, l, xv_w)
        proj = _norm_matmul(xs, ln1_g[l], w_cat, name="in_proj")

        o_ret = _retention(proj, cos_r, sin_r)

        row = lambda a: a.reshape(1, -1)
        mix = _pack_mix(rwkv_mix[l], rwkv_vres_mix[l - 1] if vres else None)
        v0 = row(rwkv_v0[l - 1]) if vres else jnp.zeros((1, RW_W), F32)
        v2 = _pad_rows(rwkv_v2[l - 1], LANES) if vres else jnp.zeros((LANES, RW_W), F32)
        r_, lw_, k_, v_, an_, b_, g_ = _rwkv_pre(
            proj, mix, row(rwkv_w0[l]), _pad_rows(rwkv_w2[l], LANES), row(rwkv_a0[l]), _pad_rows(rwkv_a2[l], LANES),
            v0, v2, rwkv_g2[l], row(rwkv_k_k[l]), row(rwkv_k_a[l]), gsum, v_first, vres)
        if not vres:
            v_first = v_
        o_rwkv = _wkv(r_, lw_, k_, v_, an_, b_, g_, row(rwkv_ln_g[l]), row(rwkv_ln_b[l]), row(rwkv_r_k[l]), gsum)

        n_q = NSA_W // LANES
        n_k = NSA_KV_W // LANES
        w_slabs = jnp.stack([jnp.tile(nsa_q_norm[l] * (NSA_DH ** -0.5), 2)] * n_q
                            + [jnp.tile(nsa_k_norm[l, 1], 2)] * n_k + [jnp.tile(nsa_k_norm[l, 2], 2)] * n_k)[:, None, :]
        s_max = 1.02 * jnp.max(jnp.abs(nsa_q_norm[l])) * jnp.max(jnp.abs(nsa_k_norm[l, 1])) * (NSA_DH ** 0.5)
        big = jnp.full((1, LANES), jnp.exp2(jnp.ceil(jnp.log2(2.0 * s_max + 128.0))), F32)
        kw_hm, hmat, v1, gates, qpad, ksx = _nsa_prep(proj, w_slabs, cos_n, sin_n, gmean, big)
        w1 = jnp.stack([nsa_cmp_k_w1[l], nsa_cmp_v_w1[l]]).astype(BF16)
        w2 = jnp.stack([nsa_cmp_k_w2[l], nsa_cmp_v_w2[l]]).astype(BF16)
        pe = jnp.broadcast_to(nsa_cmp_pe[l].reshape(2, 1, CMP_BLOCK * NSA_DH), (2, 8, CMP_BLOCK * NSA_DH)).astype(BF16)
        cmp_kv = _compress(hmat, w1, w2, pe, row(nsa_k_norm[l, 0]), cos_c, sin_c)
        o_nsa = _nsa_attn(kw_hm, qpad, ksx, gates, cmp_kv, v1, cis, ltri)

        xs = _out_proj(o_ret, o_rwkv, o_nsa, w_out, l, xs)

        up = _norm_matmul(xs, ln2_g[l], w_up, layer=l, act="relu2", out_dtype=BF16, name="mlp_up")
        xs = _matmul(up, w_down, layer=l, res=xs, name="mlp_down")
    return xs.reshape(bsz, t, d)
```

```python
import functools

import numpy as np
import jax
import jax.numpy as jnp
from jax import lax
from jax.experimental import pallas as pl
from jax.experimental.pallas import tpu as pltpu

F32 = jnp.float32
BF16 = jnp.bfloat16

RET_HEADS, RET_DH = 4, 128
RET_W = RET_HEADS * RET_DH
RW_HEADS, RW_N = 8, 64
RW_W = RW_HEADS * RW_N
RW_DECAY_RANK, RW_A_RANK, RW_V_RANK, RW_GATE_RANK = 96, 96, 64, 256
RW_GN_EPS = 64e-5
NSA_HEADS, NSA_G, NSA_DH = 16, 4, 64
NSA_HG = NSA_HEADS // NSA_G
NSA_W = NSA_HEADS * NSA_DH
NSA_KV_W = NSA_G * NSA_DH
CMP_BLOCK, CMP_STRIDE, CMP_HIDDEN = 32, 16, 256
SLC_BLOCK, SLC_TOPK, WINDOW = 64, 16, 512
NSA_TQ = 512
ROPE_THETA = 10000.0
RMS_EPS = 1e-6
NEG_INF = -1e30
FORCED_SCORE = 1e9

LANES = 128
VMEM_LIMIT = 56 * 1024 * 1024

C_RET = 0
C_RW_RKV = 2048
C_RW_XW = 3584
C_RW_XA = 3712
C_RW_XG = 3840
C_RW_XV = 4096
C_NSA_Q = 4224
PROJ_W = 6912


def _cparams(sem):
    return pltpu.CompilerParams(dimension_semantics=sem, vmem_limit_bytes=VMEM_LIMIT)


def _dot(a, b):
    return jnp.dot(a, b, preferred_element_type=F32)


def _dot_t(a, b):
    return lax.dot_general(a, b, (((1,), (1,)), ((), ())), preferred_element_type=F32)


def _dot_tl(a, b):
    return lax.dot_general(a, b, (((0,), (0,)), ((), ())), preferred_element_type=F32)


def _split_dot(x, m_bf16, passes):
    hi = x.astype(BF16)
    acc = _dot(hi, m_bf16)
    rem = x - hi.astype(F32)
    for _ in range(passes - 1):
        piece = rem.astype(BF16)
        acc = acc + _dot(piece, m_bf16)
        rem = rem - piece.astype(F32)
    return acc


def _mm_kernel(*refs, nk, act, has_res):
    a_ref, b_ref = refs[:2]
    r_ref = refs[2] if has_res else None
    o_ref = refs[3] if has_res else refs[2]

    def finish(y):
        if act == "relu2":
            y = jnp.square(jnp.maximum(y, 0.0))
        if has_res:
            y = y + r_ref[...]
        o_ref[...] = y.astype(o_ref.dtype)

    part = _dot(a_ref[...], b_ref[...].astype(BF16))
    if nk == 1:
        finish(part)
        return
    k = pl.program_id(2)
    if act is None and o_ref.dtype == F32:
        @pl.when(k == 0)
        def _():
            o_ref[...] = part + r_ref[...] if has_res else part

        @pl.when(k > 0)
        def _():
            o_ref[...] += part

        return
    acc_ref = refs[-1]

    @pl.when(k == 0)
    def _():
        acc_ref[...] = part

    @pl.when(k > 0)
    def _():
        acc_ref[...] += part

    @pl.when(k == nk - 1)
    def _():
        finish(acc_ref[...])


def _pick(n, cands):
    for c in cands:
        if n % c == 0:
            return c
    raise ValueError(f"no tile for {n}")


def _weight_spec(b, layer, block, index_map):
    if layer is None:
        return pl.BlockSpec(block, index_map)
    return pl.BlockSpec((None,) + block, lambda *idx: (layer,) + index_map(*idx))


def _matmul(a, b, *, layer=None, res=None, act=None, out_dtype=F32, name="matmul"):
    m, kd = a.shape
    n = b.shape[-1]
    tm = _pick(m, (1024, 512, 256, 128))
    tn = _pick(n, (1024, 768, 512, 256, 128))
    tk = _pick(kd, (2048, 1024, 512))
    nk = kd // tk
    in_specs = [pl.BlockSpec((tm, tk), lambda i, j, k: (i, k)),
                _weight_spec(b, layer, (tk, tn), lambda i, j, k: (k, j))]
    args = [a, b]
    if res is not None:
        in_specs.append(pl.BlockSpec((tm, tn), lambda i, j, k: (i, j)))
        args.append(res)
    return pl.pallas_call(
        functools.partial(_mm_kernel, nk=nk, act=act, has_res=res is not None),
        out_shape=jax.ShapeDtypeStruct((m, n), out_dtype),
        grid=(m // tm, n // tn, nk),
        in_specs=in_specs,
        out_specs=pl.BlockSpec((tm, tn), lambda i, j, k: (i, j)),
        scratch_shapes=[pltpu.VMEM((tm, tn), F32)] if nk > 1 and (act is not None or out_dtype != F32) else [],
        compiler_params=_cparams(("parallel", "parallel", "arbitrary")),
        name=name,
    )(*args)


def _norm_mm_kernel(x_ref, g_ref, b_ref, o_ref, h_ref, *, act):
    @pl.when(pl.program_id(1) == 0)
    def _():
        x = x_ref[...]
        ms = jnp.mean(x * x, axis=-1, keepdims=True)
        h_ref[...] = (x * lax.rsqrt(ms + RMS_EPS) * g_ref[...]).astype(BF16)

    y = _dot(h_ref[...], b_ref[...].astype(BF16))
    if act == "relu2":
        y = jnp.square(jnp.maximum(y, 0.0))
    o_ref[...] = y.astype(o_ref.dtype)


def _norm_matmul(x, g, b, *, layer=None, act=None, out_dtype=F32, name="norm_matmul"):
    m, kd = x.shape
    n = b.shape[-1]
    tm = _pick(m, (1024, 512, 256, 128))
    tn = _pick(n, (1024, 768, 512, 256, 128))
    return pl.pallas_call(
        functools.partial(_norm_mm_kernel, act=act),
        out_shape=jax.ShapeDtypeStruct((m, n), out_dtype),
        grid=(m // tm, n // tn),
        in_specs=[pl.BlockSpec((tm, kd), lambda i, j: (i, 0)),
                  pl.BlockSpec((1, kd), lambda i, j: (0, 0)),
                  _weight_spec(b, layer, (kd, tn), lambda i, j: (0, j))],
        out_specs=pl.BlockSpec((tm, tn), lambda i, j: (i, j)),
        scratch_shapes=[pltpu.VMEM((tm, kd), BF16)],
        compiler_params=_cparams(("parallel", "arbitrary")),
        name=name,
    )(x, g.reshape(1, kd), b)


def _out_proj_kernel(ret_ref, rw_ref, nsa_ref, w_ref, res_ref, o_ref, wb_ref):
    @pl.when(pl.program_id(1) == 0)
    def _():
        wb_ref[...] = w_ref[...].astype(BF16)

    k0 = ret_ref.shape[1]
    k1 = k0 + rw_ref.shape[1]
    acc = _dot(ret_ref[...], wb_ref[:k0, :])
    acc = acc + _dot(rw_ref[...], wb_ref[k0:k1, :])
    acc = acc + _dot(nsa_ref[...], wb_ref[k1:, :])
    o_ref[...] = acc + res_ref[...]


def _out_proj(o_ret, o_rwkv, o_nsa, w, layer, res):
    m = o_ret.shape[0]
    kd, n = w.shape[-2:]
    assert kd == o_ret.shape[1] + o_rwkv.shape[1] + o_nsa.shape[1]
    tm = _pick(m, (1024, 512, 256, 128))
    tn = _pick(n, (1024, 512, 256, 128))

    def rows(a):
        return pl.BlockSpec((tm, a.shape[1]), lambda j, i: (i, 0))

    return pl.pallas_call(
        _out_proj_kernel,
        out_shape=jax.ShapeDtypeStruct((m, n), F32),
        grid=(n // tn, m // tm),
        in_specs=[rows(o_ret), rows(o_rwkv), rows(o_nsa),
                  _weight_spec(w, layer, (kd, tn), lambda j, i: (0, j)),
                  pl.BlockSpec((tm, tn), lambda j, i: (i, j))],
        out_specs=pl.BlockSpec((tm, tn), lambda j, i: (i, j)),
        scratch_shapes=[pltpu.VMEM((kd, tn), BF16)],
        compiler_params=_cparams(("parallel", "arbitrary")),
        name="out_proj",
    )(o_ret, o_rwkv, o_nsa, w, res)


RET_CHUNK = 512


def _retention_kernel(q_ref, k_ref, v_ref, g_ref, cos_ref, sin_ref, dmat_ref, qd_ref, kd_ref, cd_ref, o_ref, state_ref):
    @pl.when(pl.program_id(0) == 0)
    def _():
        state_ref[...] = jnp.zeros_like(state_ref)

    cos = cos_ref[...]
    sin = sin_ref[...]

    def rope(x):
        return x * cos + pltpu.roll(x, RET_DH // 2, axis=1) * sin

    heads = range(RET_HEADS)
    hs = [slice(h * RET_DH, (h + 1) * RET_DH) for h in heads]
    q_all, k_all, v_all, gate = q_ref[...], k_ref[...], v_ref[...], g_ref[...]
    qb = [rope(q_all[:, s]).astype(BF16) for s in hs]
    kf = [rope(k_all[:, s]) * (RET_DH ** -0.5) for s in hs]
    kb = [kf[h].astype(BF16) for h in heads]
    vb = [v_all[:, s].astype(BF16) for s in hs]
    inner = [(_dot_t(qb[h], kb[h]) * dmat_ref[h]).astype(BF16) for h in heads]
    cross = [_dot(qb[h], state_ref[h].astype(BF16)) * qd_ref[h] for h in heads]
    kv = [_dot_tl((kf[h] * kd_ref[h]).astype(BF16), vb[h]) for h in heads]
    outs = []
    for h in heads:
        out = _dot(inner[h], vb[h]) + cross[h]
        state_ref[h] = cd_ref[h] * state_ref[h] + kv[h]
        outs.append(out * lax.rsqrt(jnp.mean(out * out, axis=-1, keepdims=True) + RMS_EPS))
    o_ref[...] = (gate * jax.nn.sigmoid(gate) * jnp.concatenate(outs, axis=1)).astype(o_ref.dtype)


def _retention(proj, cos, sin):
    t = proj.shape[0]
    chunk = RET_CHUNK
    log_gamma = np.log(1.0 - 2.0 ** (-5.0 - np.arange(RET_HEADS, dtype=np.float64)))[:, None, None]
    n = np.arange(chunk, dtype=np.float64)
    lag = n[:, None] - n[None, :]
    dmat = np.where(lag >= 0, np.exp(np.maximum(lag, 0.0)[None] * log_gamma), 0.0)
    ones = np.ones((1, 1, RET_DH))
    qd = np.exp((n + 1.0)[None, :, None] * log_gamma) * ones
    kd = np.exp((chunk - 1.0 - n)[None, :, None] * log_gamma) * ones
    cd = np.exp(chunk * log_gamma) * ones
    tables = [jnp.asarray(a, F32) for a in (dmat, qd, kd, cd)]
    base = C_RET // RET_W

    def col(off):
        return pl.BlockSpec((chunk, RET_W), lambda c, off=off: (c, base + off))

    def full(a):
        return pl.BlockSpec(a.shape, lambda c: (0, 0, 0))

    tab = pl.BlockSpec((chunk, RET_DH), lambda c: (c, 0))
    return pl.pallas_call(
        _retention_kernel,
        out_shape=jax.ShapeDtypeStruct((t, RET_W), BF16),
        grid=(t // chunk,),
        in_specs=[col(0), col(1), col(2), col(3), tab, tab] + [full(a) for a in tables],
        out_specs=pl.BlockSpec((chunk, RET_W), lambda c: (c, 0)),
        scratch_shapes=[pltpu.VMEM((RET_HEADS, RET_DH, RET_DH), F32)],
        compiler_params=_cparams(("arbitrary",)),
        name="retention",
    )(proj, proj, proj, proj, cos, sin, *tables)


def _dot_f32(a, b):
    a_hi = a.astype(BF16)
    b_hi = b.astype(BF16)
    a_lo = (a - a_hi.astype(F32)).astype(BF16)
    b_lo = (b - b_hi.astype(F32)).astype(BF16)
    return _dot(a_hi, b_hi) + (_dot(a_hi, b_lo) + _dot(a_lo, b_hi))


RW_PRE_TM = 512
_MIX_OFF = (0, 512, 1024, 1536, 1664, 1792, 2048, 2176)


def _rwkv_pre_kernel(r_ref, k_ref, v_ref, xw_ref, xa_ref, xg_ref, xv_ref,
                     rp_ref, kp_ref, vp_ref, xwp_ref, xap_ref, xgp_ref, xvp_ref,
                     mix_ref, w0_ref, w2_ref, a0_ref, a2_ref, v0_ref, v2_ref, g2_ref, kk_ref, ka_ref,
                     gsum_ref, vfirst_ref,
                     r_out, lw_out, k_out, v_out, an_out, b_out, g_out, *, use_vres):
    i = pl.program_id(0)

    def mixed(cur_ref, prev_ref, seg):
        cur = cur_ref[...]
        mix = mix_ref[:, _MIX_OFF[seg]:_MIX_OFF[seg + 1]]
        prev_row = jnp.where(i > 0, prev_ref[7:8, :], 0.0)
        rid = lax.broadcasted_iota(jnp.int32, cur.shape, 0)
        shifted = jnp.where(rid == 0, prev_row, pltpu.roll(cur, 1, axis=0))
        return cur + mix * (shifted - cur)

    r = mixed(r_ref, rp_ref, 0)
    k = mixed(k_ref, kp_ref, 1)
    v = mixed(v_ref, vp_ref, 2)
    xw = mixed(xw_ref, xwp_ref, 3)
    xa = mixed(xa_ref, xap_ref, 4)
    xg = mixed(xg_ref, xgp_ref, 5)

    z = -(w0_ref[...] + _dot_f32(jnp.tanh(xw), w2_ref[...]))
    softplus = jnp.maximum(z, 0.0) + jnp.log(1.0 + jnp.exp(-jnp.abs(z)))
    w = -softplus - 0.5
    lw_out[...] = -jnp.exp(w)
    a = jax.nn.sigmoid(a0_ref[...] + _dot_f32(xa, a2_ref[...]))
    g_out[...] = _dot_f32(jax.nn.sigmoid(xg), g2_ref[...])
    if use_vres:
        xv = mixed(xv_ref, xvp_ref, 6)
        v = v + (vfirst_ref[...] - v) * jax.nn.sigmoid(v0_ref[...] + _dot_f32(xv, v2_ref[...]))
    kk = k * kk_ref[...]
    ss = _split_dot(kk * kk, gsum_ref[...], 2)
    kk = kk / jnp.maximum(jnp.sqrt(ss), 1e-12)
    r_out[...] = r
    k_out[...] = k * (1.0 + (a - 1.0) * ka_ref[...])
    v_out[...] = v
    an_out[...] = -kk
    b_out[...] = kk * a


def _rwkv_pre(proj, mix, w0, w2, a0, a2, v0, v2, g2, k_k, k_a, gsum, v_first, use_vres):
    t = proj.shape[0]
    tm = RW_PRE_TM

    def cur(width, off):
        return pl.BlockSpec((tm, width), lambda i: (i, off // width))

    def prev(width, off):
        return pl.BlockSpec((8, width), lambda i: (jnp.maximum(i * (tm // 8) - 1, 0), off // width))

    def full(a):
        return pl.BlockSpec(a.shape, lambda i: (0,) * a.ndim)

    segs = [(RW_W, C_RW_RKV), (RW_W, C_RW_RKV + RW_W), (RW_W, C_RW_RKV + 2 * RW_W),
            (128, C_RW_XW), (128, C_RW_XA), (256, C_RW_XG), (128, C_RW_XV)]
    params = [mix, w0, w2, a0, a2, v0, v2, g2, k_k, k_a, gsum]
    row = pl.BlockSpec((tm, RW_W), lambda i: (i, 0))
    outs = pl.pallas_call(
        functools.partial(_rwkv_pre_kernel, use_vres=use_vres),
        out_shape=[jax.ShapeDtypeStruct((t, RW_W), F32)] * 7,
        grid=(t // tm,),
        in_specs=[cur(w, o) for w, o in segs] + [prev(w, o) for w, o in segs] + [full(p) for p in params] + [row],
        out_specs=[row] * 7,
        compiler_params=_cparams(("parallel",)),
        name="rwkv_pre",
    )(*([proj] * 14), *params, v_first)
    return outs


RW_CHUNK = 128
RW_CHUNKS_PER_STEP = 8


def _wkv_kernel(*refs):
    s_ref = refs[-1]

    @pl.when(pl.program_id(0) == 0)
    def _():
        s_ref[...] = jnp.zeros_like(s_ref)

    for sub in range(RW_CHUNKS_PER_STEP):
        _wkv_chunk(slice(sub * RW_CHUNK, (sub + 1) * RW_CHUNK), *refs)


def _wkv_chunk(rows, r_ref, lw_ref, k_ref, v_ref, an_ref, b_ref, g_ref, lng_ref, lnb_ref, rk_ref,
               gsum_ref, tril_ref, o_ref, s_ref):
    C = RW_CHUNK
    N = RW_N
    r = r_ref[rows, :]
    lw = lw_ref[rows, :]
    k = k_ref[rows, :]
    v = v_ref[rows, :]
    lw_hi = lw.astype(BF16)
    rem = lw - lw_hi.astype(F32)
    lw_mid = rem.astype(BF16)
    lw_lo = (rem - lw_mid.astype(F32)).astype(BF16)
    tril = tril_ref[...]
    lg = _dot(tril, lw_hi) + (_dot(tril, lw_mid) + _dot(tril, lw_lo))
    gam = jnp.exp(lg)
    ginv = jnp.exp(-lg)
    at = (an_ref[rows, :] * jnp.exp(lg - lw)).astype(BF16)
    bt = (b_ref[rows, :] * ginv).astype(BF16)
    kt = (k * ginv).astype(BF16)
    rt_f = r * gam
    rt = rt_f.astype(BF16)
    vb = v.astype(BF16)
    g_last = gam[C - 1:C, :]

    rowi = lax.broadcasted_iota(jnp.int32, (C, C), 0)
    coli = lax.broadcasted_iota(jnp.int32, (C, C), 1)
    strict = rowi > coli
    incl = rowi >= coli

    heads = range(RW_HEADS)
    hs = [slice(h * N, (h + 1) * N) for h in heads]
    bth = [bt[:, s] for s in hs]
    kth = [kt[:, s] for s in hs]
    vh = [vb[:, s] for s in hs]
    big = [_dot_t(jnp.concatenate([at[:, s], rt[:, s]], axis=0), jnp.concatenate([bth[h], kth[h]], axis=0))
           for h, s in zip(heads, hs)]
    a_ab = [jnp.where(strict, big[h][:C, :C], 0.0) for h in heads]
    a_ak = [jnp.where(strict, big[h][:C, C:], 0.0).astype(BF16) for h in heads]
    a_rb = [jnp.where(incl, big[h][C:, :C], 0.0).astype(BF16) for h in heads]
    a_rk = [jnp.where(incl, big[h][C:, C:], 0.0).astype(BF16) for h in heads]
    akv = [_dot(a_ak[h], vh[h]) for h in heads]
    def same_block(size):
        shift = size.bit_length() - 1
        return (rowi >> shift) == (coli >> shift)

    tinv = [jnp.where(same_block(2), a_ab[h], 0.0) + jnp.where(rowi == coli, 1.0, 0.0) for h in heads]
    size = 2
    while size < C:
        lower_left = same_block(2 * size) & jnp.logical_not(same_block(size))
        off = [jnp.where(lower_left, a_ab[h], 0.0).astype(BF16) for h in heads]
        tb = [tinv[h].astype(BF16) for h in heads]
        half = [_dot(tb[h], off[h]).astype(BF16) for h in heads]
        tinv = [tinv[h] + _dot(half[h], tb[h]) for h in heads]
        size *= 2
    xb = [_dot(tinv[h].astype(BF16), jnp.concatenate([at[:, hs[h]], akv[h].astype(BF16)], axis=1)).astype(BF16)
          for h in heads]
    yx = [_dot(a_rb[h], xb[h]) for h in heads]
    ykv = [_dot(a_rk[h], vh[h]) for h in heads]
    xtb = [_dot_tl(xb[h], bth[h]) for h in heads]
    vtk = [_dot_tl(vh[h], kth[h]) for h in heads]
    ys = []
    for h in heads:
        gl = g_last[:, hs[h]]
        y1 = (rt_f[:, hs[h]] + yx[h][:, :N]).astype(BF16)
        s0 = s_ref[h]
        s0b = s0.astype(BF16)
        ys.append(_dot_t(y1, s0b) + (yx[h][:, N:] + ykv[h]))
        s_ref[h] = (s0 + _dot(s0b, xtb[h][:N].astype(BF16)) + (xtb[h][N:] + vtk[h])) * gl
    y = jnp.concatenate(ys, axis=1)

    gsum = gsum_ref[...]
    inv_n = 1.0 / N
    mu = _split_dot(y, gsum, 2) * inv_n
    yc = y - mu
    var = _split_dot(yc * yc, gsum, 2) * inv_n
    yn = yc * lax.rsqrt(var + RW_GN_EPS) * lng_ref[...] + lnb_ref[...]
    bonus = _split_dot(r * k * rk_ref[...], gsum, 2)
    o_ref[rows, :] = ((yn + bonus * v) * g_ref[rows, :]).astype(o_ref.dtype)


def _wkv(r, lw, k, v, an, b, g, ln_g, ln_b, r_k, gsum):
    t = r.shape[0]
    C = RW_CHUNK
    tril = jnp.asarray(np.tril(np.ones((C, C), np.float32)), BF16)
    rows = RW_CHUNKS_PER_STEP * C
    row = pl.BlockSpec((rows, RW_W), lambda c: (c, 0))
    vec = pl.BlockSpec((1, RW_W), lambda c: (0, 0))
    return pl.pallas_call(
        _wkv_kernel,
        out_shape=jax.ShapeDtypeStruct((t, RW_W), BF16),
        grid=(t // rows,),
        in_specs=[row] * 7 + [vec] * 3 + [pl.BlockSpec((RW_W, RW_W), lambda c: (0, 0)),
                                          pl.BlockSpec((C, C), lambda c: (0, 0))],
        out_specs=row,
        scratch_shapes=[pltpu.VMEM((RW_HEADS, RW_N, RW_N), F32)],
        compiler_params=_cparams(("arbitrary",)),
        name="wkv7",
    )(r, lw, k, v, an, b, g, ln_g, ln_b, r_k, gsum, tril)


NSA_PREP_TM = 512


N_NORM_SLABS = (NSA_W + 2 * NSA_KV_W) // LANES
N_RAW_SLABS = 4 * NSA_KV_W // LANES
N_NSA_SLABS = N_NORM_SLABS + N_RAW_SLABS + 1


def _nsa_prep_kernel(*refs):
    x_refs = refs[:N_NSA_SLABS]
    (cos_ref, sin_ref, w_ref, gmean_ref, big_ref,
     kw_ref, hmat_ref, v1_ref, gate_ref, qpad_ref, ksx_ref) = refs[N_NSA_SLABS:]
    cos = cos_ref[...]
    sin = sin_ref[...]
    gmean = gmean_ref[...]
    half = NSA_DH // 2
    tm = cos.shape[0]
    lane = lax.broadcasted_iota(jnp.int32, cos.shape, 1)
    first_half = (lane % NSA_DH) < half
    low = lane < NSA_DH
    blk_row = (pl.program_id(0) * tm + lax.broadcasted_iota(jnp.int32, cos.shape, 0)) // SLC_BLOCK
    big = big_ref[...]
    mark_lo = jnp.where(blk_row == lane - NSA_DH, big, 0.0)
    mark_hi = jnp.where(low & (blk_row == lane + NSA_DH), big, 0.0).astype(BF16)
    n_q = NSA_W // LANES
    n_k = NSA_KV_W // LANES
    for s in range(N_NORM_SLABS):
        x = x_refs[s][...]
        y = x * lax.rsqrt(_split_dot(x * x, gmean, 2) + RMS_EPS) * w_ref[s]
        rot = jnp.where(first_half, pltpu.roll(y, LANES - half, axis=1), pltpu.roll(y, half, axis=1))
        res32 = y * cos + rot * sin
        if s >= n_q + n_k:
            res = res32.astype(kw_ref.dtype)
            kw_ref[2 * (s - n_q - n_k)] = res[:, :NSA_DH]
            kw_ref[2 * (s - n_q - n_k) + 1] = res[:, NSA_DH:]
            continue
        swapped = pltpu.roll(res32, NSA_DH, axis=1)
        if s < n_q:
            qpad_ref[2 * s] = jnp.where(low, res32, 0.0).astype(BF16)
            qpad_ref[2 * s + 1] = jnp.where(low, swapped, 0.0).astype(BF16)
        else:
            g0 = 2 * (s - n_q)
            ksx_ref[g0, :, :LANES] = jnp.where(low, res32, mark_lo).astype(BF16)
            ksx_ref[g0 + 1, :, :LANES] = jnp.where(low, swapped, mark_lo).astype(BF16)
            ksx_ref[g0, :, LANES:] = mark_hi
            ksx_ref[g0 + 1, :, LANES:] = mark_hi

    rows = hmat_ref.shape[2]
    low_r = lax.broadcasted_iota(jnp.int32, (rows, LANES), 1) < NSA_DH
    for s in range(NSA_KV_W // LANES * 2):
        kind, pair = divmod(s, NSA_KV_W // LANES)
        x_ref = x_refs[N_NORM_SLABS + s]
        for t2 in range(CMP_STRIDE // 2):
            r0 = x_ref[pl.ds(2 * t2, rows, stride=CMP_STRIDE), :]
            r1 = x_ref[pl.ds(2 * t2 + 1, rows, stride=CMP_STRIDE), :]
            cols = slice(t2 * LANES, (t2 + 1) * LANES)
            hmat_ref[kind, 2 * pair, :, cols] = jnp.where(low_r, r0, pltpu.roll(r1, NSA_DH, axis=1)).astype(BF16)
            hmat_ref[kind, 2 * pair + 1, :, cols] = jnp.where(low_r, pltpu.roll(r0, NSA_DH, axis=1), r1).astype(BF16)

    one_col = jnp.where(lane == NSA_DH, 1.0, 0.0)
    for s in range(NSA_KV_W // LANES * 2):
        x = x_refs[N_NORM_SLABS + N_RAW_SLABS // 2 + s][...]
        v1_ref[2 * s] = jnp.where(low, x, one_col).astype(BF16)
        v1_ref[2 * s + 1] = jnp.where(low, pltpu.roll(x, NSA_DH, axis=1), one_col).astype(BF16)

    xg = x_refs[N_NSA_SLABS - 1][...]
    per_group = 3 * NSA_HG
    for g in range(NSA_G):
        shifted = xg if g == 0 else pltpu.roll(xg, LANES - per_group * g, axis=1)
        gate_ref[g] = jnp.where(lane < per_group, shifted, 0.0)


def _nsa_prep(proj, w_slabs, cos, sin, gmean, big):
    t = proj.shape[0]
    assert t // SLC_BLOCK <= LANES
    tm = NSA_PREP_TM
    base = C_NSA_Q // LANES
    tab = pl.BlockSpec((tm, LANES), lambda i: (i, 0))
    return pl.pallas_call(
        _nsa_prep_kernel,
        out_shape=[jax.ShapeDtypeStruct((NSA_G, t, NSA_DH), BF16),
                   jax.ShapeDtypeStruct((2, NSA_G, t // CMP_STRIDE, CMP_STRIDE * NSA_DH), BF16),
                   jax.ShapeDtypeStruct((2 * NSA_G, t, LANES), BF16),
                   jax.ShapeDtypeStruct((NSA_G, t, LANES), F32),
                   jax.ShapeDtypeStruct((NSA_HEADS, t, LANES), BF16),
                   jax.ShapeDtypeStruct((NSA_G, t, 2 * LANES), BF16)],
        grid=(t // tm,),
        in_specs=[pl.BlockSpec((tm, LANES), lambda i, s=s: (i, base + s)) for s in range(N_NSA_SLABS)]
        + [tab, tab, pl.BlockSpec((N_NORM_SLABS, 1, LANES), lambda i: (0, 0, 0)),
           pl.BlockSpec((LANES, LANES), lambda i: (0, 0)), pl.BlockSpec((1, LANES), lambda i: (0, 0))],
        out_specs=[pl.BlockSpec((NSA_G, tm, NSA_DH), lambda i: (0, i, 0)),
                   pl.BlockSpec((2, NSA_G, tm // CMP_STRIDE, CMP_STRIDE * NSA_DH), lambda i: (0, 0, i, 0)),
                   pl.BlockSpec((2 * NSA_G, tm, LANES), lambda i: (0, i, 0)),
                   pl.BlockSpec((NSA_G, tm, LANES), lambda i: (0, i, 0)),
                   pl.BlockSpec((NSA_HEADS, tm, LANES), lambda i: (0, i, 0)),
                   pl.BlockSpec((NSA_G, tm, 2 * LANES), lambda i: (0, i, 0))],
        compiler_params=_cparams(("parallel",)),
        name="nsa_prep",
    )(*([proj] * N_NSA_SLABS), cos, sin, w_slabs, gmean, big)


def _compress_kernel(h_ref, w1_ref, w2_ref, pe_ref, nw_ref, cos_ref, sin_ref, o_ref):
    kind = pl.program_id(0)
    nc = h_ref.shape[2]
    half_in = CMP_STRIDE * NSA_DH
    hm = h_ref[0, 0]
    w1 = w1_ref[0]
    first = _dot(hm, w1[:half_in])
    second = _dot(hm, w1[half_in:])
    const = _dot(pe_ref[0], w1)[0:1, :]
    pre = first + pltpu.roll(second, nc - 1, axis=0) + const
    hid = 0.5 * pre * (1.0 + jnp.tanh(0.7978845608028654 * (pre + 0.044715 * pre * pre * pre)))
    out = _dot(hid.astype(BF16), w2_ref[0])
    y = out * lax.rsqrt(jnp.mean(out * out, axis=-1, keepdims=True) + RMS_EPS) * nw_ref[...]
    half = NSA_DH // 2
    rot = jnp.concatenate([y[:, half:], y[:, :half]], axis=1)
    roped = y * cos_ref[...] + rot * sin_ref[...]
    o_ref[0, 0] = jnp.where(kind == 0, roped, out).astype(o_ref.dtype)


def _compress(hmat, w1, w2, pe, nw, cos_c, sin_c):
    _, g, nc, width = hmat.shape
    return pl.pallas_call(
        _compress_kernel,
        out_shape=jax.ShapeDtypeStruct((2, g, nc, NSA_DH), BF16),
        grid=(2, g),
        in_specs=[pl.BlockSpec((1, 1, nc, width), lambda a, b: (a, b, 0, 0)),
                  pl.BlockSpec((1, 2 * width, CMP_HIDDEN), lambda a, b: (a, 0, 0)),
                  pl.BlockSpec((1, CMP_HIDDEN, NSA_DH), lambda a, b: (a, 0, 0)),
                  pl.BlockSpec((1, 8, 2 * width), lambda a, b: (a, 0, 0)),
                  pl.BlockSpec((1, NSA_DH), lambda a, b: (0, 0)),
                  pl.BlockSpec((nc, NSA_DH), lambda a, b: (0, 0)),
                  pl.BlockSpec((nc, NSA_DH), lambda a, b: (0, 0))],
        out_specs=pl.BlockSpec((1, 1, nc, NSA_DH), lambda a, b: (a, b, 0, 0)),
        compiler_params=_cparams(("parallel", "parallel")),
        name="nsa_compress",
    )(hmat, w1, w2, pe, nw, cos_c, sin_c)


SLC_KT = 1024
WIN_ROWS = 128
WIN_KEYS = WINDOW + WIN_ROWS


INT32_MIN = -2 ** 31


def _nsa_attn_kernel(qpad_ref, gate_ref, kc_ref, vc_ref, ks_ref, vs_ref, kw_ref, vw_ref, cis_ref, ltri_ref, o_ref):
    qi = pl.program_id(1)
    nc = kc_ref.shape[2]
    nb = cis_ref.shape[1]
    heads = range(NSA_HG)
    start = qi * NSA_TQ
    qh = [qpad_ref[hh][:, :NSA_DH] for hh in heads]
    tok = start + lax.broadcasted_iota(jnp.int32, (NSA_TQ, 1), 0)

    def softmax_terms(scores):
        return [jnp.exp(s - jnp.max(s, axis=-1, keepdims=True)) for s in scores]

    def cmp_branch(width):
        kc = kc_ref[0, 0, :width, :]
        vc = vc_ref[0, 0, :width, :]
        cend = lax.broadcasted_iota(jnp.int32, (1, width), 1) * CMP_STRIDE + (CMP_BLOCK - 1)
        bias_c = jnp.where(cend <= tok, 0.0, NEG_INF)
        ec = softmax_terms([_dot_t(qh[hh], kc) + bias_c for hh in heads])
        sees_any = tok >= CMP_BLOCK - 1
        pc = [ec[hh] * jnp.where(sees_any, 1.0 / jnp.sum(ec[hh], axis=-1, keepdims=True), 0.0) for hh in heads]
        outs_c = tuple(_dot(pc[hh].astype(BF16), vc) for hh in heads)
        psum = pc[0]
        for hh in range(1, NSA_HG):
            psum = psum + pc[hh]
        return outs_c, _split_dot(psum, cis_ref[:width, :], 2)

    n_quarters = 4
    quarter = nc // n_quarters
    n_visible = (start + NSA_TQ - CMP_BLOCK) // CMP_STRIDE + 1
    which = jnp.clip((n_visible - 1) // quarter, 0, n_quarters - 1)
    o_c, score = lax.switch(which, [functools.partial(cmp_branch, (r + 1) * quarter) for r in range(n_quarters)])
    blk = lax.broadcasted_iota(jnp.int32, (1, nb), 1)
    cur = tok // SLC_BLOCK
    forced = (blk == 0) | (blk == cur) | (blk == cur - 1)
    score = jnp.where(forced, FORCED_SCORE, score)
    score = jnp.where(blk <= cur, score, -jnp.inf)

    bits = lax.bitcast_convert_type(score.T, jnp.int32)
    key_all = bits ^ ((bits >> 31) & 0x7FFFFFFF)

    def select_blocks(nbv):
        key = key_all[:nbv]
        n_sel = min(SLC_TOPK, nbv)

        def enough(c):
            return jnp.sum(jnp.where(key >= c, 1.0, 0.0), axis=0, keepdims=True) >= n_sel

        def enough3(c1, c2, c3):
            packed = jnp.where(key >= c3, 65793.0, jnp.where(key >= c2, 257.0, jnp.where(key >= c1, 1.0, 0.0)))
            tot = jnp.sum(packed, axis=0, keepdims=True).astype(jnp.int32)
            return (tot & 255) >= n_sel, ((tot >> 8) & 255) >= n_sel, (tot >> 16) >= n_sel

        zero_row = jnp.zeros((1, NSA_TQ), jnp.int32)
        thr = jnp.where(enough(zero_row), zero_row, INT32_MIN)
        for hi in range(30, 0, -2):
            c1 = thr + (1 << (hi - 1))
            c2 = thr + (1 << hi)
            c3 = c2 + (1 << (hi - 1))
            e1, e2, e3 = enough3(c1, c2, c3)
            thr = jnp.where(e3, c3, jnp.where(e2, c2, jnp.where(e1, c1, thr)))
        c1 = thr + 1
        thr = jnp.where(enough(c1), c1, thr)
        above = key > thr
        tied = key == thr
        need = n_sel - jnp.sum(jnp.where(above, 1.0, 0.0), axis=0, keepdims=True)
        tied_before = _dot(ltri_ref[:nbv, :nbv], jnp.where(tied, 1.0, 0.0).astype(BF16))
        chosen = jnp.where(above | (tied & (tied_before < need)), 1.0, 0.0)
        if nbv < LANES:
            chosen = jnp.concatenate([chosen, jnp.zeros((LANES - nbv, NSA_TQ), F32)], axis=0)
        return chosen.T

    assert nb <= LANES
    n_steps = 4
    blocks_visible = (start + NSA_TQ - 1) // SLC_BLOCK + 1
    widths = sorted({max(-(-nb * (r + 1) // n_steps // 8) * 8, min(nb, SLC_TOPK)) for r in range(n_steps)})
    which_sel = sum((blocks_visible > w).astype(jnp.int32) for w in widths[:-1])
    sel = lax.switch(which_sel, [functools.partial(select_blocks, w) for w in widths])
    lane_q = lax.broadcasted_iota(jnp.int32, (NSA_TQ, LANES), 1)
    sel_sw = pltpu.roll(sel, NSA_DH, axis=1)
    sel_lo = jnp.where(lane_q >= NSA_DH, sel_sw, 0.0)
    sel_hi = jnp.where(lane_q < NSA_DH, sel_sw, 0.0)
    qx = [jnp.concatenate([qpad_ref[hh].astype(F32) + sel_lo, sel_hi], axis=1).astype(BF16) for hh in heads]

    def window_rows(r0):
        kbase = pl.multiple_of(jnp.maximum(start + r0 - WINDOW, 0), WIN_ROWS)
        kw = kw_ref[0, pl.ds(kbase, WIN_KEYS), :]
        vw = vw_ref[0, pl.ds(kbase, WIN_KEYS), :]
        lag = tok[r0:r0 + WIN_ROWS] - (kbase + lax.broadcasted_iota(jnp.int32, (1, WIN_KEYS), 1))
        bias_w = jnp.where((lag >= 0) & (lag < WINDOW), 0.0, NEG_INF)
        sw = [_dot_t(qh[hh][r0:r0 + WIN_ROWS], kw) + bias_w for hh in heads]
        ew = [jnp.exp((s - jnp.max(s, axis=-1, keepdims=True)).astype(BF16)) for s in sw]
        return [_dot(ew[hh], vw) for hh in heads]

    win_parts = [window_rows(r0) for r0 in range(0, NSA_TQ, WIN_ROWS)]
    acc_w = [jnp.concatenate([part[hh] for part in win_parts], axis=0) for hh in heads]

    n_tiles = (start + NSA_TQ + SLC_KT - 1) // SLC_KT

    def sel_tile(j, carry, causal, width=SLC_KT):
        ms, accs = carry
        k0 = pl.multiple_of(j * SLC_KT, SLC_KT)
        kt = ks_ref[0, pl.ds(k0, width), :]
        vt = vs_ref[0, pl.ds(k0, width), :]
        sj = [_dot_t(qx[hh], kt) for hh in heads]
        if causal:
            future = (k0 + lax.broadcasted_iota(jnp.int32, (1, width), 1)) > tok
            sj = [jnp.where(future, NEG_INF, s) for s in sj]
        m_new = [jnp.maximum(ms[hh], jnp.max(sj[hh], axis=-1, keepdims=True)) for hh in heads]
        pj = [jnp.exp((sj[hh] - m_new[hh]).astype(BF16)) for hh in heads]
        accs = [jnp.exp(ms[hh] - m_new[hh]) * accs[hh] + _dot(pj[hh], vt) for hh in heads]
        return tuple(m_new), tuple(accs)

    m0 = tuple(jnp.full((NSA_TQ, 1), NEG_INF, F32) for _ in heads)
    a0 = tuple(jnp.zeros((NSA_TQ, LANES), F32) for _ in heads)
    carry = lax.fori_loop(0, n_tiles - 1, functools.partial(sel_tile, causal=False), (m0, a0))
    per_tile = SLC_KT // NSA_TQ
    last = [functools.partial(sel_tile, n_tiles - 1, causal=True, width=(r + 1) * NSA_TQ) for r in range(per_tile)]
    _, acc_s = lax.switch(qi % per_tile, last, carry)

    gt = jax.nn.sigmoid(gate_ref[0])
    outs = []
    for hh in heads:
        g0, g1, g2 = (gt[:, 3 * hh + br:3 * hh + br + 1] for br in range(3))
        scale_s = g1 / acc_s[hh][:, NSA_DH:NSA_DH + 1]
        scale_w = g2 / acc_w[hh][:, NSA_DH:NSA_DH + 1]
        outs.append(g0 * o_c[hh] + scale_s * acc_s[hh][:, :NSA_DH] + scale_w * acc_w[hh][:, :NSA_DH])
    o_ref[...] = jnp.concatenate(outs, axis=1).astype(o_ref.dtype)


def _nsa_attn(kw_hm, qpad, ksx, gates, cmp_kv, v1, cis, ltri):
    t = kw_hm.shape[1]
    nq = t // NSA_TQ
    nc = cmp_kv.shape[2]
    nb = cis.shape[1]
    ks_spec = pl.BlockSpec((1, t, 2 * LANES), lambda g, i: (g, 0, 0))
    kw_spec = pl.BlockSpec((1, t, NSA_DH), lambda g, i: (g, 0, 0))
    vs_spec = pl.BlockSpec((1, t, LANES), lambda g, i: (g, 0, 0))
    vw_spec = pl.BlockSpec((1, t, LANES), lambda g, i: (NSA_G + g, 0, 0))
    return pl.pallas_call(
        _nsa_attn_kernel,
        out_shape=jax.ShapeDtypeStruct((t, NSA_W), BF16),
        grid=(NSA_G, nq),
        in_specs=[pl.BlockSpec((NSA_HG, NSA_TQ, LANES), lambda g, i: (g, i, 0)),
                  pl.BlockSpec((1, NSA_TQ, LANES), lambda g, i: (g, i, 0)),
                  pl.BlockSpec((1, 1, nc, NSA_DH), lambda g, i: (0, g, 0, 0)),
                  pl.BlockSpec((1, 1, nc, NSA_DH), lambda g, i: (1, g, 0, 0)),
                  ks_spec, vs_spec, kw_spec, vw_spec,
                  pl.BlockSpec((nc, nb), lambda g, i: (0, 0)),
                  pl.BlockSpec((nb, nb), lambda g, i: (0, 0))],
        out_specs=pl.BlockSpec((NSA_TQ, NSA_HG * NSA_DH), lambda g, i: (i, g)),
        compiler_params=_cparams(("parallel", "arbitrary")),
        name="nsa_attention",
    )(qpad, gates, cmp_kv, cmp_kv, ksx, v1, kw_hm, v1, cis, ltri)


def _rope_tables(pos, dh, reps):
    half = dh // 2
    inv_freq = ROPE_THETA ** (-np.arange(half, dtype=np.float64) / half)
    ang = np.asarray(pos, np.float64)[:, None] * inv_freq[None, :]
    cos = np.cos(ang)
    sin = np.sin(ang)
    cos_t = np.tile(np.concatenate([cos, cos], axis=1), (1, reps))
    sin_t = np.tile(np.concatenate([-sin, sin], axis=1), (1, reps))
    return jnp.asarray(cos_t, F32), jnp.asarray(sin_t, F32)


def _pad_cols(a, width):
    return jnp.pad(a, ((0, 0), (0, width - a.shape[1])))


def _pad_rows(a, height):
    return jnp.pad(a, ((0, height - a.shape[0]), (0, 0)))


def _pack_w_in_kernel(w_ref, xv_ref, o_ref):
    o_ref[...] = _pack_w_in(w_ref[...], xv_ref[...]).astype(o_ref.dtype)


def _pack_w_in_call(w_in, layer, xv):
    _, d, n_in = w_in.shape
    tk = 256
    return pl.pallas_call(
        _pack_w_in_kernel,
        out_shape=jax.ShapeDtypeStruct((d, PROJ_W), BF16),
        grid=(d // tk,),
        in_specs=[pl.BlockSpec((None, tk, n_in), lambda i: (layer, i, 0)),
                  pl.BlockSpec((tk, LANES), lambda i: (i, 0))],
        out_specs=pl.BlockSpec((tk, PROJ_W), lambda i: (i, 0)),
        compiler_params=_cparams(("parallel",)),
        name="pack_w_in",
    )(w_in, xv)


def _pack_w_in(w_l, xv):
    d = w_l.shape[0]
    o = 0
    ret = w_l[:, o:o + 4 * RET_W]; o += 4 * RET_W
    rkv = w_l[:, o:o + 3 * RW_W]; o += 3 * RW_W
    xw = w_l[:, o:o + RW_DECAY_RANK]; o += RW_DECAY_RANK
    xa = w_l[:, o:o + RW_A_RANK]; o += RW_A_RANK
    xg = w_l[:, o:o + RW_GATE_RANK]; o += RW_GATE_RANK
    q = w_l[:, o:o + NSA_W]; o += NSA_W
    kc, vc, ks, vs, kw, vw = (w_l[:, o + i * NSA_KV_W:o + (i + 1) * NSA_KV_W] for i in range(6))
    o += 6 * NSA_KV_W
    gates = w_l[:, o:o + 3 * NSA_HEADS]
    return jnp.concatenate([ret, rkv, _pad_cols(xw, LANES), _pad_cols(xa, LANES), xg, xv,
                            q, ks, kw, kc, vc, vs, vw, _pad_cols(gates, LANES)], axis=1)


def _pack_mix(mix, vres_mix):
    o = 3 * RW_W
    xw = mix[o:o + RW_DECAY_RANK]; o += RW_DECAY_RANK
    xa = mix[o:o + RW_A_RANK]; o += RW_A_RANK
    xg = mix[o:o + RW_GATE_RANK]
    z = lambda n: jnp.zeros((n,), mix.dtype)
    xv = z(LANES) if vres_mix is None else jnp.concatenate([vres_mix, z(LANES - RW_V_RANK)])
    return jnp.concatenate([mix[:3 * RW_W], xw, z(LANES - RW_DECAY_RANK), xa, z(LANES - RW_A_RANK), xg, xv])[None, :]


def kernel(x, ln1_g, w_in, w_in_vres, rwkv_mix, rwkv_vres_mix, rwkv_w0, rwkv_w2, rwkv_a0, rwkv_a2, rwkv_v0, rwkv_v2, rwkv_g2, rwkv_k_k, rwkv_k_a, rwkv_r_k, rwkv_ln_g, rwkv_ln_b, nsa_q_norm, nsa_k_norm, nsa_cmp_pe, nsa_cmp_k_w1, nsa_cmp_k_w2, nsa_cmp_v_w1, nsa_cmp_v_w2, w_out, ln2_g, w_up, w_down):
    bsz, t, d = x.shape
    assert bsz == 1
    depth = w_in.shape[0]
    nc = t // CMP_STRIDE
    nb = t // SLC_BLOCK
    xs = x.reshape(t, d)

    pos = np.arange(t)
    cos_r, sin_r = _rope_tables(pos, RET_DH, 1)
    cos_n, sin_n = _rope_tables(pos, NSA_DH, 2)
    cos_c, sin_c = _rope_tables(np.arange(nc) * CMP_STRIDE + (CMP_BLOCK - 1), NSA_DH, 1)
    lane_head = np.arange(RW_W) // RW_N
    gsum = jnp.asarray(lane_head[:, None] == lane_head[None, :], BF16)
    lane_h2 = np.arange(LANES) // NSA_DH
    gmean = jnp.asarray((lane_h2[:, None] == lane_h2[None, :]) / float(NSA_DH), BF16)
    cstart = np.arange(nc) * CMP_STRIDE
    sstart = np.arange(nb) * SLC_BLOCK
    cis = jnp.asarray((cstart[:, None] <= sstart[None, :] + SLC_BLOCK - 1)
                      & (cstart[:, None] + CMP_BLOCK - 1 >= sstart[None, :]), BF16)

    ltri = jnp.asarray(np.tril(np.ones((nb, nb), np.float32), -1), BF16)

    v_first = jnp.zeros((t, RW_W), F32)
    for l in range(depth):
        vres = l > 0
        xv_w = _pad_cols(w_in_vres[l - 1], LANES) if vres else jnp.zeros((d, LANES), F32)
        w_cat = _pack_w_in_call(w_in, l, xv_w)
        proj = _norm_matmul(xs, ln1_g[l], w_cat, name="in_proj")

        o_ret = _retention(proj, cos_r, sin_r)

        row = lambda a: a.reshape(1, -1)
        mix = _pack_mix(rwkv_mix[l], rwkv_vres_mix[l - 1] if vres else None)
        v0 = row(rwkv_v0[l - 1]) if vres else jnp.zeros((1, RW_W), F32)
        v2 = _pad_rows(rwkv_v2[l - 1], LANES) if vres else jnp.zeros((LANES, RW_W), F32)
        r_, lw_, k_, v_, an_, b_, g_ = _rwkv_pre(
            proj, mix, row(rwkv_w0[l]), _pad_rows(rwkv_w2[l], LANES), row(rwkv_a0[l]), _pad_rows(rwkv_a2[l], LANES),
            v0, v2, rwkv_g2[l], row(rwkv_k_k[l]), row(rwkv_k_a[l]), gsum, v_first, vres)
        if not vres:
            v_first = v_
        o_rwkv = _wkv(r_, lw_, k_, v_, an_, b_, g_, row(rwkv_ln_g[l]), row(rwkv_ln_b[l]), row(rwkv_r_k[l]), gsum)

        n_q = NSA_W // LANES
        n_k = NSA_KV_W // LANES
        w_slabs = jnp.stack([jnp.tile(nsa_q_norm[l] * (NSA_DH ** -0.5), 2)] * n_q
                            + [jnp.tile(nsa_k_norm[l, 1], 2)] * n_k + [jnp.tile(nsa_k_norm[l, 2], 2)] * n_k)[:, None, :]
        s_max = 1.02 * jnp.max(jnp.abs(nsa_q_norm[l])) * jnp.max(jnp.abs(nsa_k_norm[l, 1])) * (NSA_DH ** 0.5)
        big = jnp.full((1, LANES), jnp.exp2(jnp.ceil(jnp.log2(2.0 * s_max + 128.0))), F32)
        kw_hm, hmat, v1, gates, qpad, ksx = _nsa_prep(proj, w_slabs, cos_n, sin_n, gmean, big)
        w1 = jnp.stack([nsa_cmp_k_w1[l], nsa_cmp_v_w1[l]]).astype(BF16)
        w2 = jnp.stack([nsa_cmp_k_w2[l], nsa_cmp_v_w2[l]]).astype(BF16)
        pe = jnp.broadcast_to(nsa_cmp_pe[l].reshape(2, 1, CMP_BLOCK * NSA_DH), (2, 8, CMP_BLOCK * NSA_DH)).astype(BF16)
        cmp_kv = _compress(hmat, w1, w2, pe, row(nsa_k_norm[l, 0]), cos_c, sin_c)
        o_nsa = _nsa_attn(kw_hm, qpad, ksx, gates, cmp_kv, v1, cis, ltri)

        xs = _out_proj(o_ret, o_rwkv, o_nsa, w_out, l, xs)

        up = _norm_matmul(xs, ln2_g[l], w_up, layer=l, act="relu2", out_dtype=BF16, name="mlp_up")
        xs = _matmul(up, w_down, layer=l, res=xs, name="mlp_down")
    return xs.reshape(bsz, t, d)
```

```python
import functools

import numpy as np
import jax
import jax.numpy as jnp
from jax import lax
from jax.experimental import pallas as pl
from jax.experimental.pallas import tpu as pltpu

F32 = jnp.float32
BF16 = jnp.bfloat16

RET_HEADS, RET_DH = 4, 128
RET_W = RET_HEADS * RET_DH
RW_HEADS, RW_N = 8, 64
RW_W = RW_HEADS * RW_N
RW_DECAY_RANK, RW_A_RANK, RW_V_RANK, RW_GATE_RANK = 96, 96, 64, 256
RW_GN_EPS = 64e-5
NSA_HEADS, NSA_G, NSA_DH = 16, 4, 64
NSA_HG = NSA_HEADS // NSA_G
NSA_W = NSA_HEADS * NSA_DH
NSA_KV_W = NSA_G * NSA_DH
CMP_BLOCK, CMP_STRIDE, CMP_HIDDEN = 32, 16, 256
SLC_BLOCK, SLC_TOPK, WINDOW = 64, 16, 512
NSA_TQ = 512
ROPE_THETA = 10000.0
RMS_EPS = 1e-6
NEG_INF = -1e30
FORCED_SCORE = 1e9

LANES = 128
VMEM_LIMIT = 56 * 1024 * 1024

C_RET = 0
C_RW_RKV = 2048
C_RW_XW = 3584
C_RW_XA = 3712
C_RW_XG = 3840
C_RW_XV = 4096
C_NSA_Q = 4224
PROJ_W = 6912


def _cparams(sem):
    return pltpu.CompilerParams(dimension_semantics=sem, vmem_limit_bytes=VMEM_LIMIT)


def _dot(a, b):
    return jnp.dot(a, b, preferred_element_type=F32)


def _dot_t(a, b):
    return lax.dot_general(a, b, (((1,), (1,)), ((), ())), preferred_element_type=F32)


def _dot_tl(a, b):
    return lax.dot_general(a, b, (((0,), (0,)), ((), ())), preferred_element_type=F32)


def _split_dot(x, m_bf16, passes):
    hi = x.astype(BF16)
    acc = _dot(hi, m_bf16)
    rem = x - hi.astype(F32)
    for _ in range(passes - 1):
        piece = rem.astype(BF16)
        acc = acc + _dot(piece, m_bf16)
        rem = rem - piece.astype(F32)
    return acc


def _mm_kernel(*refs, nk, act, has_res):
    a_ref, b_ref = refs[:2]
    r_ref = refs[2] if has_res else None
    o_ref = refs[3] if has_res else refs[2]

    def finish(y):
        if act == "relu2":
            y = jnp.square(jnp.maximum(y, 0.0))
        if has_res:
            y = y + r_ref[...]
        o_ref[...] = y.astype(o_ref.dtype)

    part = _dot(a_ref[...], b_ref[...].astype(BF16))
    if nk == 1:
        finish(part)
        return
    k = pl.program_id(2)
    if act is None and o_ref.dtype == F32:
        @pl.when(k == 0)
        def _():
            o_ref[...] = part + r_ref[...] if has_res else part

        @pl.when(k > 0)
        def _():
            o_ref[...] += part

        return
    acc_ref = refs[-1]

    @pl.when(k == 0)
    def _():
        acc_ref[...] = part

    @pl.when(k > 0)
    def _():
        acc_ref[...] += part

    @pl.when(k == nk - 1)
    def _():
        finish(acc_ref[...])


def _pick(n, cands):
    for c in cands:
        if n % c == 0:
            return c
    raise ValueError(f"no tile for {n}")


def _weight_spec(b, layer, block, index_map):
    if layer is None:
        return pl.BlockSpec(block, index_map)
    return pl.BlockSpec((None,) + block, lambda *idx: (layer,) + index_map(*idx))


def _matmul(a, b, *, layer=None, res=None, act=None, out_dtype=F32, name="matmul"):
    m, kd = a.shape
    n = b.shape[-1]
    tm = _pick(m, (1024, 512, 256, 128))
    tn = _pick(n, (1024, 768, 512, 256, 128))
    tk = _pick(kd, (2048, 1024, 512))
    nk = kd // tk
    in_specs = [pl.BlockSpec((tm, tk), lambda i, j, k: (i, k)),
                _weight_spec(b, layer, (tk, tn), lambda i, j, k: (k, j))]
    args = [a, b]
    if res is not None:
        in_specs.append(pl.BlockSpec((tm, tn), lambda i, j, k: (i, j)))
        args.append(res)
    return pl.pallas_call(
        functools.partial(_mm_kernel, nk=nk, act=act, has_res=res is not None),
        out_shape=jax.ShapeDtypeStruct((m, n), out_dtype),
        grid=(m // tm, n // tn, nk),
        in_specs=in_specs,
        out_specs=pl.BlockSpec((tm, tn), lambda i, j, k: (i, j)),
        scratch_shapes=[pltpu.VMEM((tm, tn), F32)] if nk > 1 and (act is not None or out_dtype != F32) else [],
        compiler_params=_cparams(("parallel", "parallel", "arbitrary")),
        name=name,
    )(*args)


def _norm_mm_kernel(x_ref, g_ref, b_ref, o_ref, h_ref, *, act):
    @pl.when(pl.program_id(1) == 0)
    def _():
        x = x_ref[...]
        ms = jnp.mean(x * x, axis=-1, keepdims=True)
        h_ref[...] = (x * lax.rsqrt(ms + RMS_EPS) * g_ref[...]).astype(BF16)

    y = _dot(h_ref[...], b_ref[...].astype(BF16))
    if act == "relu2":
        y = jnp.square(jnp.maximum(y, 0.0))
    o_ref[...] = y.astype(o_ref.dtype)


def _norm_matmul(x, g, b, *, layer=None, act=None, out_dtype=F32, name="norm_matmul"):
    m, kd = x.shape
    n = b.shape[-1]
    tm = _pick(m, (1024, 512, 256, 128))
    tn = _pick(n, (1024, 768, 512, 256, 128))
    return pl.pallas_call(
        functools.partial(_norm_mm_kernel, act=act),
        out_shape=jax.ShapeDtypeStruct((m, n), out_dtype),
        grid=(m // tm, n // tn),
        in_specs=[pl.BlockSpec((tm, kd), lambda i, j: (i, 0)),
                  pl.BlockSpec((1, kd), lambda i, j: (0, 0)),
                  _weight_spec(b, layer, (kd, tn), lambda i, j: (0, j))],
        out_specs=pl.BlockSpec((tm, tn), lambda i, j: (i, j)),
        scratch_shapes=[pltpu.VMEM((tm, kd), BF16)],
        compiler_params=_cparams(("parallel", "arbitrary")),
        name=name,
    )(x, g.reshape(1, kd), b)


def _out_proj_kernel(ret_ref, rw_ref, nsa_ref, w_ref, res_ref, o_ref, wb_ref):
    @pl.when(pl.program_id(1) == 0)
    def _():
        wb_ref[...] = w_ref[...].astype(BF16)

    k0 = ret_ref.shape[1]
    k1 = k0 + rw_ref.shape[1]
    acc = _dot(ret_ref[...], wb_ref[:k0, :])
    acc = acc + _dot(rw_ref[...], wb_ref[k0:k1, :])
    acc = acc + _dot(nsa_ref[...], wb_ref[k1:, :])
    o_ref[...] = acc + res_ref[...]


def _out_proj(o_ret, o_rwkv, o_nsa, w, layer, res):
    m = o_ret.shape[0]
    kd, n = w.shape[-2:]
    assert kd == o_ret.shape[1] + o_rwkv.shape[1] + o_nsa.shape[1]
    tm = _pick(m, (1024, 512, 256, 128))
    tn = _pick(n, (1024, 512, 256, 128))

    def rows(a):
        return pl.BlockSpec((tm, a.shape[1]), lambda j, i: (i, 0))

    return pl.pallas_call(
        _out_proj_kernel,
        out_shape=jax.ShapeDtypeStruct((m, n), F32),
        grid=(n // tn, m // tm),
        in_specs=[rows(o_ret), rows(o_rwkv), rows(o_nsa),
                  _weight_spec(w, layer, (kd, tn), lambda j, i: (0, j)),
                  pl.BlockSpec((tm, tn), lambda j, i: (i, j))],
        out_specs=pl.BlockSpec((tm, tn), lambda j, i: (i, j)),
        scratch_shapes=[pltpu.VMEM((kd, tn), BF16)],
        compiler_params=_cparams(("parallel", "arbitrary")),
        name="out_proj",
    )(o_ret, o_rwkv, o_nsa, w, res)


RET_CHUNK = 512


def _retention_kernel(q_ref, k_ref, v_ref, g_ref, cos_ref, sin_ref, dmat_ref, qd_ref, kd_ref, cd_ref, o_ref, state_ref):
    @pl.when(pl.program_id(0) == 0)
    def _():
        state_ref[...] = jnp.zeros_like(state_ref)

    cos = cos_ref[...]
    sin = sin_ref[...]

    def rope(x):
        return x * cos + pltpu.roll(x, RET_DH // 2, axis=1) * sin

    heads = range(RET_HEADS)
    hs = [slice(h * RET_DH, (h + 1) * RET_DH) for h in heads]
    q_all, k_all, v_all, gate = q_ref[...], k_ref[...], v_ref[...], g_ref[...]
    qb = [rope(q_all[:, s]).astype(BF16) for s in hs]
    kf = [rope(k_all[:, s]) * (RET_DH ** -0.5) for s in hs]
    kb = [kf[h].astype(BF16) for h in heads]
    vb = [v_all[:, s].astype(BF16) for s in hs]
    inner = [(_dot_t(qb[h], kb[h]) * dmat_ref[h]).astype(BF16) for h in heads]
    cross = [_dot(qb[h], state_ref[h].astype(BF16)) * qd_ref[h] for h in heads]
    kv = [_dot_tl((kf[h] * kd_ref[h]).astype(BF16), vb[h]) for h in heads]
    outs = []
    for h in heads:
        out = _dot(inner[h], vb[h]) + cross[h]
        state_ref[h] = cd_ref[h] * state_ref[h] + kv[h]
        outs.append(out * lax.rsqrt(jnp.mean(out * out, axis=-1, keepdims=True) + RMS_EPS))
    o_ref[...] = (gate * jax.nn.sigmoid(gate) * jnp.concatenate(outs, axis=1)).astype(o_ref.dtype)


def _retention(proj, cos, sin):
    t = proj.shape[0]
    chunk = RET_CHUNK
    log_gamma = np.log(1.0 - 2.0 ** (-5.0 - np.arange(RET_HEADS, dtype=np.float64)))[:, None, None]
    n = np.arange(chunk, dtype=np.float64)
    lag = n[:, None] - n[None, :]
    dmat = np.where(lag >= 0, np.exp(np.maximum(lag, 0.0)[None] * log_gamma), 0.0)
    ones = np.ones((1, 1, RET_DH))
    qd = np.exp((n + 1.0)[None, :, None] * log_gamma) * ones
    kd = np.exp((chunk - 1.0 - n)[None, :, None] * log_gamma) * ones
    cd = np.exp(chunk * log_gamma) * ones
    tables = [jnp.asarray(a, F32) for a in (dmat, qd, kd, cd)]
    base = C_RET // RET_W

    def col(off):
        return pl.BlockSpec((chunk, RET_W), lambda c, off=off: (c, base + off))

    def full(a):
        return pl.BlockSpec(a.shape, lambda c: (0, 0, 0))

    tab = pl.BlockSpec((chunk, RET_DH), lambda c: (c, 0))
    return pl.pallas_call(
        _retention_kernel,
        out_shape=jax.ShapeDtypeStruct((t, RET_W), BF16),
        grid=(t // chunk,),
        in_specs=[col(0), col(1), col(2), col(3), tab, tab] + [full(a) for a in tables],
        out_specs=pl.BlockSpec((chunk, RET_W), lambda c: (c, 0)),
        scratch_shapes=[pltpu.VMEM((RET_HEADS, RET_DH, RET_DH), F32)],
        compiler_params=_cparams(("arbitrary",)),
        name="retention",
    )(proj, proj, proj, proj, cos, sin, *tables)


def _dot_f32(a, b):
    a_hi = a.astype(BF16)
    b_hi = b.astype(BF16)
    a_lo = (a - a_hi.astype(F32)).astype(BF16)
    b_lo = (b - b_hi.astype(F32)).astype(BF16)
    return _dot(a_hi, b_hi) + (_dot(a_hi, b_lo) + _dot(a_lo, b_hi))


RW_PRE_TM = 512
_MIX_OFF = (0, 512, 1024, 1536, 1664, 1792, 2048, 2176)


def _rwkv_pre_kernel(r_ref, k_ref, v_ref, xw_ref, xa_ref, xg_ref, xv_ref,
                     rp_ref, kp_ref, vp_ref, xwp_ref, xap_ref, xgp_ref, xvp_ref,
                     mix_ref, w0_ref, w2_ref, a0_ref, a2_ref, v0_ref, v2_ref, g2_ref, kk_ref, ka_ref,
                     gsum_ref, vfirst_ref,
                     r_out, lw_out, k_out, v_out, an_out, b_out, g_out, *, use_vres):
    i = pl.program_id(0)

    def mixed(cur_ref, prev_ref, seg):
        cur = cur_ref[...]
        mix = mix_ref[:, _MIX_OFF[seg]:_MIX_OFF[seg + 1]]
        prev_row = jnp.where(i > 0, prev_ref[7:8, :], 0.0)
        rid = lax.broadcasted_iota(jnp.int32, cur.shape, 0)
        shifted = jnp.where(rid == 0, prev_row, pltpu.roll(cur, 1, axis=0))
        return cur + mix * (shifted - cur)

    r = mixed(r_ref, rp_ref, 0)
    k = mixed(k_ref, kp_ref, 1)
    v = mixed(v_ref, vp_ref, 2)
    xw = mixed(xw_ref, xwp_ref, 3)
    xa = mixed(xa_ref, xap_ref, 4)
    xg = mixed(xg_ref, xgp_ref, 5)

    z = -(w0_ref[...] + _dot_f32(jnp.tanh(xw), w2_ref[...]))
    softplus = jnp.maximum(z, 0.0) + jnp.log(1.0 + jnp.exp(-jnp.abs(z)))
    w = -softplus - 0.5
    lw_out[...] = -jnp.exp(w)
    a = jax.nn.sigmoid(a0_ref[...] + _dot_f32(xa, a2_ref[...]))
    g_out[...] = _dot_f32(jax.nn.sigmoid(xg), g2_ref[...])
    if use_vres:
        xv = mixed(xv_ref, xvp_ref, 6)
        v = v + (vfirst_ref[...] - v) * jax.nn.sigmoid(v0_ref[...] + _dot_f32(xv, v2_ref[...]))
    kk = k * kk_ref[...]
    ss = _split_dot(kk * kk, gsum_ref[...], 2)
    kk = kk / jnp.maximum(jnp.sqrt(ss), 1e-12)
    r_out[...] = r
    k_out[...] = k * (1.0 + (a - 1.0) * ka_ref[...])
    v_out[...] = v
    an_out[...] = -kk
    b_out[...] = kk * a


def _rwkv_pre(proj, mix, w0, w2, a0, a2, v0, v2, g2, k_k, k_a, gsum, v_first, use_vres):
    t = proj.shape[0]
    tm = RW_PRE_TM

    def cur(width, off):
        return pl.BlockSpec((tm, width), lambda i: (i, off // width))

    def prev(width, off):
        return pl.BlockSpec((8, width), lambda i: (jnp.maximum(i * (tm // 8) - 1, 0), off // width))

    def full(a):
        return pl.BlockSpec(a.shape, lambda i: (0,) * a.ndim)

    segs = [(RW_W, C_RW_RKV), (RW_W, C_RW_RKV + RW_W), (RW_W, C_RW_RKV + 2 * RW_W),
            (128, C_RW_XW), (128, C_RW_XA), (256, C_RW_XG), (128, C_RW_XV)]
    params = [mix, w0, w2, a0, a2, v0, v2, g2, k_k, k_a, gsum]
    row = pl.BlockSpec((tm, RW_W), lambda i: (i, 0))
    outs = pl.pallas_call(
        functools.partial(_rwkv_pre_kernel, use_vres=use_vres),
        out_shape=[jax.ShapeDtypeStruct((t, RW_W), F32)] * 7,
        grid=(t // tm,),
        in_specs=[cur(w, o) for w, o in segs] + [prev(w, o) for w, o in segs] + [full(p) for p in params] + [row],
        out_specs=[row] * 7,
        compiler_params=_cparams(("parallel",)),
        name="rwkv_pre",
    )(*([proj] * 14), *params, v_first)
    return outs


RW_CHUNK = 128
RW_CHUNKS_PER_STEP = 4


def _wkv_kernel(*refs):
    s_ref = refs[-1]

    @pl.when(pl.program_id(0) == 0)
    def _():
        s_ref[...] = jnp.zeros_like(s_ref)

    for sub in range(RW_CHUNKS_PER_STEP):
        _wkv_chunk(slice(sub * RW_CHUNK, (sub + 1) * RW_CHUNK), *refs)


def _wkv_chunk(rows, r_ref, lw_ref, k_ref, v_ref, an_ref, b_ref, g_ref, lng_ref, lnb_ref, rk_ref,
               gsum_ref, tril_ref, o_ref, s_ref):
    C = RW_CHUNK
    N = RW_N
    r = r_ref[rows, :]
    lw = lw_ref[rows, :]
    k = k_ref[rows, :]
    v = v_ref[rows, :]
    lw_hi = lw.astype(BF16)
    rem = lw - lw_hi.astype(F32)
    lw_mid = rem.astype(BF16)
    lw_lo = (rem - lw_mid.astype(F32)).astype(BF16)
    tril = tril_ref[...]
    lg = _dot(tril, lw_hi) + (_dot(tril, lw_mid) + _dot(tril, lw_lo))
    gam = jnp.exp(lg)
    ginv = jnp.exp(-lg)
    at = (an_ref[rows, :] * jnp.exp(lg - lw)).astype(BF16)
    bt = (b_ref[rows, :] * ginv).astype(BF16)
    kt = (k * ginv).astype(BF16)
    rt_f = r * gam
    rt = rt_f.astype(BF16)
    vb = v.astype(BF16)
    g_last = gam[C - 1:C, :]

    rowi = lax.broadcasted_iota(jnp.int32, (C, C), 0)
    coli = lax.broadcasted_iota(jnp.int32, (C, C), 1)
    strict = rowi > coli
    incl = rowi >= coli

    heads = range(RW_HEADS)
    hs = [slice(h * N, (h + 1) * N) for h in heads]
    bth = [bt[:, s] for s in hs]
    kth = [kt[:, s] for s in hs]
    vh = [vb[:, s] for s in hs]
    big = [_dot_t(jnp.concatenate([at[:, s], rt[:, s]], axis=0), jnp.concatenate([bth[h], kth[h]], axis=0))
           for h, s in zip(heads, hs)]
    a_ab = [jnp.where(strict, big[h][:C, :C], 0.0) for h in heads]
    a_ak = [jnp.where(strict, big[h][:C, C:], 0.0).astype(BF16) for h in heads]
    a_rb = [jnp.where(incl, big[h][C:, :C], 0.0).astype(BF16) for h in heads]
    a_rk = [jnp.where(incl, big[h][C:, C:], 0.0).astype(BF16) for h in heads]
    akv = [_dot(a_ak[h], vh[h]) for h in heads]
    def same_block(size):
        shift = size.bit_length() - 1
        return (rowi >> shift) == (coli >> shift)

    tinv = [jnp.where(same_block(2), a_ab[h], 0.0) + jnp.where(rowi == coli, 1.0, 0.0) for h in heads]
    size = 2
    while size < C:
        lower_left = same_block(2 * size) & jnp.logical_not(same_block(size))
        off = [jnp.where(lower_left, a_ab[h], 0.0).astype(BF16) for h in heads]
        tb = [tinv[h].astype(BF16) for h in heads]
        half = [_dot(tb[h], off[h]).astype(BF16) for h in heads]
        tinv = [tinv[h] + _dot(half[h], tb[h]) for h in heads]
        size *= 2
    xb = [_dot(tinv[h].astype(BF16), jnp.concatenate([at[:, hs[h]], akv[h].astype(BF16)], axis=1)).astype(BF16)
          for h in heads]
    yx = [_dot(a_rb[h], xb[h]) for h in heads]
    ykv = [_dot(a_rk[h], vh[h]) for h in heads]
    xtb = [_dot_tl(xb[h], bth[h]) for h in heads]
    vtk = [_dot_tl(vh[h], kth[h]) for h in heads]
    ys = []
    for h in heads:
        gl = g_last[:, hs[h]]
        y1 = (rt_f[:, hs[h]] + yx[h][:, :N]).astype(BF16)
        s0 = s_ref[h]
        s0b = s0.astype(BF16)
        ys.append(_dot_t(y1, s0b) + (yx[h][:, N:] + ykv[h]))
        s_ref[h] = (s0 + _dot(s0b, xtb[h][:N].astype(BF16)) + (xtb[h][N:] + vtk[h])) * gl
    y = jnp.concatenate(ys, axis=1)

    gsum = gsum_ref[...]
    inv_n = 1.0 / N
    mu = _split_dot(y, gsum, 2) * inv_n
    yc = y - mu
    var = _split_dot(yc * yc, gsum, 2) * inv_n
    yn = yc * lax.rsqrt(var + RW_GN_EPS) * lng_ref[...] + lnb_ref[...]
    bonus = _split_dot(r * k * rk_ref[...], gsum, 2)
    o_ref[rows, :] = ((yn + bonus * v) * g_ref[rows, :]).astype(o_ref.dtype)


def _wkv(r, lw, k, v, an, b, g, ln_g, ln_b, r_k, gsum):
    t = r.shape[0]
    C = RW_CHUNK
    tril = jnp.asarray(np.tril(np.ones((C, C), np.float32)), BF16)
    rows = RW_CHUNKS_PER_STEP * C
    row = pl.BlockSpec((rows, RW_W), lambda c: (c, 0))
    vec = pl.BlockSpec((1, RW_W), lambda c: (0, 0))
    return pl.pallas_call(
        _wkv_kernel,
        out_shape=jax.ShapeDtypeStruct((t, RW_W), BF16),
        grid=(t // rows,),
        in_specs=[row] * 7 + [vec] * 3 + [pl.BlockSpec((RW_W, RW_W), lambda c: (0, 0)),
                                          pl.BlockSpec((C, C), lambda c: (0, 0))],
        out_specs=row,
        scratch_shapes=[pltpu.VMEM((RW_HEADS, RW_N, RW_N), F32)],
        compiler_params=_cparams(("arbitrary",)),
        name="wkv7",
    )(r, lw, k, v, an, b, g, ln_g, ln_b, r_k, gsum, tril)


NSA_PREP_TM = 512


N_NORM_SLABS = (NSA_W + 2 * NSA_KV_W) // LANES
N_RAW_SLABS = 4 * NSA_KV_W // LANES
N_NSA_SLABS = N_NORM_SLABS + N_RAW_SLABS + 1


def _nsa_prep_kernel(*refs):
    x_refs = refs[:N_NSA_SLABS]
    (cos_ref, sin_ref, w_ref, gmean_ref, big_ref,
     kw_ref, hmat_ref, v1_ref, gate_ref, qpad_ref, ksx_ref) = refs[N_NSA_SLABS:]
    cos = cos_ref[...]
    sin = sin_ref[...]
    gmean = gmean_ref[...]
    half = NSA_DH // 2
    tm = cos.shape[0]
    lane = lax.broadcasted_iota(jnp.int32, cos.shape, 1)
    first_half = (lane % NSA_DH) < half
    low = lane < NSA_DH
    blk_row = (pl.program_id(0) * tm + lax.broadcasted_iota(jnp.int32, cos.shape, 0)) // SLC_BLOCK
    big = big_ref[...]
    mark_lo = jnp.where(blk_row == lane - NSA_DH, big, 0.0)
    mark_hi = jnp.where(low & (blk_row == lane + NSA_DH), big, 0.0).astype(BF16)
    n_q = NSA_W // LANES
    n_k = NSA_KV_W // LANES
    for s in range(N_NORM_SLABS):
        x = x_refs[s][...]
        y = x * lax.rsqrt(_split_dot(x * x, gmean, 2) + RMS_EPS) * w_ref[s]
        rot = jnp.where(first_half, pltpu.roll(y, LANES - half, axis=1), pltpu.roll(y, half, axis=1))
        res32 = y * cos + rot * sin
        if s >= n_q + n_k:
            res = res32.astype(kw_ref.dtype)
            kw_ref[2 * (s - n_q - n_k)] = res[:, :NSA_DH]
            kw_ref[2 * (s - n_q - n_k) + 1] = res[:, NSA_DH:]
            continue
        swapped = pltpu.roll(res32, NSA_DH, axis=1)
        if s < n_q:
            qpad_ref[2 * s] = jnp.where(low, res32, 0.0).astype(BF16)
            qpad_ref[2 * s + 1] = jnp.where(low, swapped, 0.0).astype(BF16)
        else:
            g0 = 2 * (s - n_q)
            ksx_ref[g0, :, :LANES] = jnp.where(low, res32, mark_lo).astype(BF16)
            ksx_ref[g0 + 1, :, :LANES] = jnp.where(low, swapped, mark_lo).astype(BF16)
            ksx_ref[g0, :, LANES:] = mark_hi
            ksx_ref[g0 + 1, :, LANES:] = mark_hi

    rows = hmat_ref.shape[2]
    low_r = lax.broadcasted_iota(jnp.int32, (rows, LANES), 1) < NSA_DH
    for s in range(NSA_KV_W // LANES * 2):
        kind, pair = divmod(s, NSA_KV_W // LANES)
        x_ref = x_refs[N_NORM_SLABS + s]
        for t2 in range(CMP_STRIDE // 2):
            r0 = x_ref[pl.ds(2 * t2, rows, stride=CMP_STRIDE), :]
            r1 = x_ref[pl.ds(2 * t2 + 1, rows, stride=CMP_STRIDE), :]
            cols = slice(t2 * LANES, (t2 + 1) * LANES)
            hmat_ref[kind, 2 * pair, :, cols] = jnp.where(low_r, r0, pltpu.roll(r1, NSA_DH, axis=1)).astype(BF16)
            hmat_ref[kind, 2 * pair + 1, :, cols] = jnp.where(low_r, pltpu.roll(r0, NSA_DH, axis=1), r1).astype(BF16)

    one_col = jnp.where(lane == NSA_DH, 1.0, 0.0)
    for s in range(NSA_KV_W // LANES * 2):
        x = x_refs[N_NORM_SLABS + N_RAW_SLABS // 2 + s][...]
        v1_ref[2 * s] = jnp.where(low, x, one_col).astype(BF16)
        v1_ref[2 * s + 1] = jnp.where(low, pltpu.roll(x, NSA_DH, axis=1), one_col).astype(BF16)

    xg = x_refs[N_NSA_SLABS - 1][...]
    per_group = 3 * NSA_HG
    for g in range(NSA_G):
        shifted = xg if g == 0 else pltpu.roll(xg, LANES - per_group * g, axis=1)
        gate_ref[g] = jnp.where(lane < per_group, shifted, 0.0)


def _nsa_prep(proj, w_slabs, cos, sin, gmean, big):
    t = proj.shape[0]
    assert t // SLC_BLOCK <= LANES
    tm = NSA_PREP_TM
    base = C_NSA_Q // LANES
    tab = pl.BlockSpec((tm, LANES), lambda i: (i, 0))
    return pl.pallas_call(
        _nsa_prep_kernel,
        out_shape=[jax.ShapeDtypeStruct((NSA_G, t, NSA_DH), BF16),
                   jax.ShapeDtypeStruct((2, NSA_G, t // CMP_STRIDE, CMP_STRIDE * NSA_DH), BF16),
                   jax.ShapeDtypeStruct((2 * NSA_G, t, LANES), BF16),
                   jax.ShapeDtypeStruct((NSA_G, t, LANES), F32),
                   jax.ShapeDtypeStruct((NSA_HEADS, t, LANES), BF16),
                   jax.ShapeDtypeStruct((NSA_G, t, 2 * LANES), BF16)],
        grid=(t // tm,),
        in_specs=[pl.BlockSpec((tm, LANES), lambda i, s=s: (i, base + s)) for s in range(N_NSA_SLABS)]
        + [tab, tab, pl.BlockSpec((N_NORM_SLABS, 1, LANES), lambda i: (0, 0, 0)),
           pl.BlockSpec((LANES, LANES), lambda i: (0, 0)), pl.BlockSpec((1, LANES), lambda i: (0, 0))],
        out_specs=[pl.BlockSpec((NSA_G, tm, NSA_DH), lambda i: (0, i, 0)),
                   pl.BlockSpec((2, NSA_G, tm // CMP_STRIDE, CMP_STRIDE * NSA_DH), lambda i: (0, 0, i, 0)),
                   pl.BlockSpec((2 * NSA_G, tm, LANES), lambda i: (0, i, 0)),
                   pl.BlockSpec((NSA_G, tm, LANES), lambda i: (0, i, 0)),
                   pl.BlockSpec((NSA_HEADS, tm, LANES), lambda i: (0, i, 0)),
                   pl.BlockSpec((NSA_G, tm, 2 * LANES), lambda i: (0, i, 0))],
        compiler_params=_cparams(("parallel",)),
        name="nsa_prep",
    )(*([proj] * N_NSA_SLABS), cos, sin, w_slabs, gmean, big)


def _compress_kernel(h_ref, w1_ref, w2_ref, pe_ref, nw_ref, cos_ref, sin_ref, o_ref):
    kind = pl.program_id(0)
    nc = h_ref.shape[2]
    half_in = CMP_STRIDE * NSA_DH
    hm = h_ref[0, 0]
    w1 = w1_ref[0]
    first = _dot(hm, w1[:half_in])
    second = _dot(hm, w1[half_in:])
    const = _dot(pe_ref[0], w1)[0:1, :]
    pre = first + pltpu.roll(second, nc - 1, axis=0) + const
    hid = 0.5 * pre * (1.0 + jnp.tanh(0.7978845608028654 * (pre + 0.044715 * pre * pre * pre)))
    out = _dot(hid.astype(BF16), w2_ref[0])
    y = out * lax.rsqrt(jnp.mean(out * out, axis=-1, keepdims=True) + RMS_EPS) * nw_ref[...]
    half = NSA_DH // 2
    rot = jnp.concatenate([y[:, half:], y[:, :half]], axis=1)
    roped = y * cos_ref[...] + rot * sin_ref[...]
    o_ref[0, 0] = jnp.where(kind == 0, roped, out).astype(o_ref.dtype)


def _compress(hmat, w1, w2, pe, nw, cos_c, sin_c):
    _, g, nc, width = hmat.shape
    return pl.pallas_call(
        _compress_kernel,
        out_shape=jax.ShapeDtypeStruct((2, g, nc, NSA_DH), BF16),
        grid=(2, g),
        in_specs=[pl.BlockSpec((1, 1, nc, width), lambda a, b: (a, b, 0, 0)),
                  pl.BlockSpec((1, 2 * width, CMP_HIDDEN), lambda a, b: (a, 0, 0)),
                  pl.BlockSpec((1, CMP_HIDDEN, NSA_DH), lambda a, b: (a, 0, 0)),
                  pl.BlockSpec((1, 8, 2 * width), lambda a, b: (a, 0, 0)),
                  pl.BlockSpec((1, NSA_DH), lambda a, b: (0, 0)),
                  pl.BlockSpec((nc, NSA_DH), lambda a, b: (0, 0)),
                  pl.BlockSpec((nc, NSA_DH), lambda a, b: (0, 0))],
        out_specs=pl.BlockSpec((1, 1, nc, NSA_DH), lambda a, b: (a, b, 0, 0)),
        compiler_params=_cparams(("parallel", "parallel")),
        name="nsa_compress",
    )(hmat, w1, w2, pe, nw, cos_c, sin_c)


SLC_KT = 1024
WIN_ROWS = 128
WIN_KEYS = WINDOW + WIN_ROWS


INT32_MIN = -2 ** 31
LOG2_E = 1.4426950408889634


def _nsa_attn_kernel(qpad_ref, gate_ref, kc_ref, vc_ref, ks_ref, vs_ref, kw_ref, vw_ref, cis_ref, ltri_ref, o_ref):
    qi = pl.program_id(1)
    nc = kc_ref.shape[2]
    nb = cis_ref.shape[1]
    heads = range(NSA_HG)
    start = qi * NSA_TQ
    qh = [qpad_ref[hh][:, :NSA_DH] for hh in heads]
    tok = start + lax.broadcasted_iota(jnp.int32, (NSA_TQ, 1), 0)

    def softmax_terms(scores):
        return [jnp.exp2(s - jnp.max(s, axis=-1, keepdims=True)) for s in scores]

    def cmp_branch(width):
        kc = kc_ref[0, 0, :width, :]
        vc = vc_ref[0, 0, :width, :]
        cend = lax.broadcasted_iota(jnp.int32, (1, width), 1) * CMP_STRIDE + (CMP_BLOCK - 1)
        bias_c = jnp.where(cend <= tok, 0.0, NEG_INF)
        ec = softmax_terms([_dot_t(qh[hh], kc) + bias_c for hh in heads])
        sees_any = tok >= CMP_BLOCK - 1
        pc = [ec[hh] * jnp.where(sees_any, 1.0 / jnp.sum(ec[hh], axis=-1, keepdims=True), 0.0) for hh in heads]
        outs_c = tuple(_dot(pc[hh].astype(BF16), vc) for hh in heads)
        psum = pc[0]
        for hh in range(1, NSA_HG):
            psum = psum + pc[hh]
        return outs_c, _split_dot(psum, cis_ref[:width, :], 2)

    n_quarters = 4
    quarter = nc // n_quarters
    n_visible = (start + NSA_TQ - CMP_BLOCK) // CMP_STRIDE + 1
    which = jnp.clip((n_visible - 1) // quarter, 0, n_quarters - 1)
    o_c, score = lax.switch(which, [functools.partial(cmp_branch, (r + 1) * quarter) for r in range(n_quarters)])
    blk = lax.broadcasted_iota(jnp.int32, (1, nb), 1)
    cur = tok // SLC_BLOCK
    forced = (blk == 0) | (blk == cur) | (blk == cur - 1)
    score = jnp.where(forced, FORCED_SCORE, score)
    score = jnp.where(blk <= cur, score, -jnp.inf)

    bits = lax.bitcast_convert_type(score.T, jnp.int32)
    key_all = bits ^ ((bits >> 31) & 0x7FFFFFFF)

    def select_blocks(nbv):
        key = key_all[:nbv]
        n_sel = min(SLC_TOPK, nbv)

        def enough(c):
            return jnp.sum(jnp.where(key >= c, 1.0, 0.0), axis=0, keepdims=True) >= n_sel

        def enough3(c1, c2, c3):
            packed = jnp.where(key >= c3, 65793.0, jnp.where(key >= c2, 257.0, jnp.where(key >= c1, 1.0, 0.0)))
            tot = jnp.sum(packed, axis=0, keepdims=True).astype(jnp.int32)
            return (tot & 255) >= n_sel, ((tot >> 8) & 255) >= n_sel, (tot >> 16) >= n_sel

        zero_row = jnp.zeros((1, NSA_TQ), jnp.int32)
        thr = jnp.where(enough(zero_row), zero_row, INT32_MIN)
        for hi in range(30, 0, -2):
            c1 = thr + (1 << (hi - 1))
            c2 = thr + (1 << hi)
            c3 = c2 + (1 << (hi - 1))
            e1, e2, e3 = enough3(c1, c2, c3)
            thr = jnp.where(e3, c3, jnp.where(e2, c2, jnp.where(e1, c1, thr)))
        c1 = thr + 1
        thr = jnp.where(enough(c1), c1, thr)
        above = key > thr
        tied = key == thr
        need = n_sel - jnp.sum(jnp.where(above, 1.0, 0.0), axis=0, keepdims=True)
        tied_before = _dot(ltri_ref[:nbv, :nbv], jnp.where(tied, 1.0, 0.0).astype(BF16))
        chosen = jnp.where(above | (tied & (tied_before < need)), 1.0, 0.0)
        if nbv < LANES:
            chosen = jnp.concatenate([chosen, jnp.zeros((LANES - nbv, NSA_TQ), F32)], axis=0)
        return chosen.T

    assert nb <= LANES
    n_steps = 4
    blocks_visible = (start + NSA_TQ - 1) // SLC_BLOCK + 1
    widths = sorted({max(-(-nb * (r + 1) // n_steps // 8) * 8, min(nb, SLC_TOPK)) for r in range(n_steps)})
    which_sel = sum((blocks_visible > w).astype(jnp.int32) for w in widths[:-1])
    sel = lax.switch(which_sel, [functools.partial(select_blocks, w) for w in widths])
    lane_q = lax.broadcasted_iota(jnp.int32, (NSA_TQ, LANES), 1)
    sel_sw = pltpu.roll(sel, NSA_DH, axis=1)
    sel_lo = jnp.where(lane_q >= NSA_DH, sel_sw, 0.0)
    sel_hi = jnp.where(lane_q < NSA_DH, sel_sw, 0.0)
    qx = [jnp.concatenate([qpad_ref[hh].astype(F32) + sel_lo, sel_hi], axis=1).astype(BF16) for hh in heads]

    def window_rows(r0):
        kbase = pl.multiple_of(jnp.maximum(start + r0 - WINDOW, 0), WIN_ROWS)
        kw = kw_ref[0, pl.ds(kbase, WIN_KEYS), :]
        vw = vw_ref[0, pl.ds(kbase, WIN_KEYS), :]
        lag = tok[r0:r0 + WIN_ROWS] - (kbase + lax.broadcasted_iota(jnp.int32, (1, WIN_KEYS), 1))
        bias_w = jnp.where((lag >= 0) & (lag < WINDOW), 0.0, NEG_INF)
        sw = [_dot_t(qh[hh][r0:r0 + WIN_ROWS], kw) + bias_w for hh in heads]
        ew = [jnp.exp2((s - jnp.max(s, axis=-1, keepdims=True)).astype(BF16)) for s in sw]
        return [_dot(ew[hh], vw) for hh in heads]

    win_parts = [window_rows(r0) for r0 in range(0, NSA_TQ, WIN_ROWS)]
    acc_w = [jnp.concatenate([part[hh] for part in win_parts], axis=0) for hh in heads]

    n_tiles = (start + NSA_TQ + SLC_KT - 1) // SLC_KT

    def sel_tile(j, carry, causal, width=SLC_KT):
        ms, accs = carry
        k0 = pl.multiple_of(j * SLC_KT, SLC_KT)
        kt = ks_ref[0, pl.ds(k0, width), :]
        vt = vs_ref[0, pl.ds(k0, width), :]
        sj = [_dot_t(qx[hh], kt) for hh in heads]
        if causal:
            future = (k0 + lax.broadcasted_iota(jnp.int32, (1, width), 1)) > tok
            sj = [jnp.where(future, NEG_INF, s) for s in sj]
        m_new = [jnp.maximum(ms[hh], jnp.max(sj[hh], axis=-1, keepdims=True)) for hh in heads]
        pj = [jnp.exp2((sj[hh] - m_new[hh]).astype(BF16)) for hh in heads]
        accs = [jnp.exp2(ms[hh] - m_new[hh]) * accs[hh] + _dot(pj[hh], vt) for hh in heads]
        return tuple(m_new), tuple(accs)

    m0 = tuple(jnp.full((NSA_TQ, 1), NEG_INF, F32) for _ in heads)
    a0 = tuple(jnp.zeros((NSA_TQ, LANES), F32) for _ in heads)
    carry = lax.fori_loop(0, n_tiles - 1, functools.partial(sel_tile, causal=False), (m0, a0))
    per_tile = SLC_KT // NSA_TQ
    last = [functools.partial(sel_tile, n_tiles - 1, causal=True, width=(r + 1) * NSA_TQ) for r in range(per_tile)]
    _, acc_s = lax.switch(qi % per_tile, last, carry)

    gt = jax.nn.sigmoid(gate_ref[0])
    outs = []
    for hh in heads:
        g0, g1, g2 = (gt[:, 3 * hh + br:3 * hh + br + 1] for br in range(3))
        scale_s = g1 / acc_s[hh][:, NSA_DH:NSA_DH + 1]
        scale_w = g2 / acc_w[hh][:, NSA_DH:NSA_DH + 1]
        outs.append(g0 * o_c[hh] + scale_s * acc_s[hh][:, :NSA_DH] + scale_w * acc_w[hh][:, :NSA_DH])
    o_ref[...] = jnp.concatenate(outs, axis=1).astype(o_ref.dtype)


def _nsa_attn(kw_hm, qpad, ksx, gates, cmp_kv, v1, cis, ltri):
    t = kw_hm.shape[1]
    nq = t // NSA_TQ
    nc = cmp_kv.shape[2]
    nb = cis.shape[1]
    ks_spec = pl.BlockSpec((1, t, 2 * LANES), lambda g, i: (g, 0, 0))
    kw_spec = pl.BlockSpec((1, t, NSA_DH), lambda g, i: (g, 0, 0))
    vs_spec = pl.BlockSpec((1, t, LANES), lambda g, i: (g, 0, 0))
    vw_spec = pl.BlockSpec((1, t, LANES), lambda g, i: (NSA_G + g, 0, 0))
    return pl.pallas_call(
        _nsa_attn_kernel,
        out_shape=jax.ShapeDtypeStruct((t, NSA_W), BF16),
        grid=(NSA_G, nq),
        in_specs=[pl.BlockSpec((NSA_HG, NSA_TQ, LANES), lambda g, i: (g, i, 0)),
                  pl.BlockSpec((1, NSA_TQ, LANES), lambda g, i: (g, i, 0)),
                  pl.BlockSpec((1, 1, nc, NSA_DH), lambda g, i: (0, g, 0, 0)),
                  pl.BlockSpec((1, 1, nc, NSA_DH), lambda g, i: (1, g, 0, 0)),
                  ks_spec, vs_spec, kw_spec, vw_spec,
                  pl.BlockSpec((nc, nb), lambda g, i: (0, 0)),
                  pl.BlockSpec((nb, nb), lambda g, i: (0, 0))],
        out_specs=pl.BlockSpec((NSA_TQ, NSA_HG * NSA_DH), lambda g, i: (i, g)),
        compiler_params=_cparams(("parallel", "arbitrary")),
        name="nsa_attention",
    )(qpad, gates, cmp_kv, cmp_kv, ksx, v1, kw_hm, v1, cis, ltri)


def _rope_tables(pos, dh, reps):
    half = dh // 2
    inv_freq = ROPE_THETA ** (-np.arange(half, dtype=np.float64) / half)
    ang = np.asarray(pos, np.float64)[:, None] * inv_freq[None, :]
    cos = np.cos(ang)
    sin = np.sin(ang)
    cos_t = np.tile(np.concatenate([cos, cos], axis=1), (1, reps))
    sin_t = np.tile(np.concatenate([-sin, sin], axis=1), (1, reps))
    return jnp.asarray(cos_t, F32), jnp.asarray(sin_t, F32)


def _pad_cols(a, width):
    return jnp.pad(a, ((0, 0), (0, width - a.shape[1])))


def _pad_rows(a, height):
    return jnp.pad(a, ((0, height - a.shape[0]), (0, 0)))


def _pack_w_in_kernel(w_ref, xv_ref, o_ref):
    o_ref[...] = _pack_w_in(w_ref[...], xv_ref[...]).astype(o_ref.dtype)


def _pack_w_in_call(w_in, layer, xv):
    _, d, n_in = w_in.shape
    tk = 256
    return pl.pallas_call(
        _pack_w_in_kernel,
        out_shape=jax.ShapeDtypeStruct((d, PROJ_W), BF16),
        grid=(d // tk,),
        in_specs=[pl.BlockSpec((None, tk, n_in), lambda i: (layer, i, 0)),
                  pl.BlockSpec((tk, LANES), lambda i: (i, 0))],
        out_specs=pl.BlockSpec((tk, PROJ_W), lambda i: (i, 0)),
        compiler_params=_cparams(("parallel",)),
        name="pack_w_in",
    )(w_in, xv)


def _pack_w_in(w_l, xv):
    d = w_l.shape[0]
    o = 0
    ret = w_l[:, o:o + 4 * RET_W]; o += 4 * RET_W
    rkv = w_l[:, o:o + 3 * RW_W]; o += 3 * RW_W
    xw = w_l[:, o:o + RW_DECAY_RANK]; o += RW_DECAY_RANK
    xa = w_l[:, o:o + RW_A_RANK]; o += RW_A_RANK
    xg = w_l[:, o:o + RW_GATE_RANK]; o += RW_GATE_RANK
    q = w_l[:, o:o + NSA_W]; o += NSA_W
    kc, vc, ks, vs, kw, vw = (w_l[:, o + i * NSA_KV_W:o + (i + 1) * NSA_KV_W] for i in range(6))
    o += 6 * NSA_KV_W
    gates = w_l[:, o:o + 3 * NSA_HEADS]
    return jnp.concatenate([ret, rkv, _pad_cols(xw, LANES), _pad_cols(xa, LANES), xg, xv,
                            q, ks, kw, kc, vc, vs, vw, _pad_cols(gates, LANES)], axis=1)


def _pack_mix(mix, vres_mix):
    o = 3 * RW_W
    xw = mix[o:o + RW_DECAY_RANK]; o += RW_DECAY_RANK
    xa = mix[o:o + RW_A_RANK]; o += RW_A_RANK
    xg = mix[o:o + RW_GATE_RANK]
    z = lambda n: jnp.zeros((n,), mix.dtype)
    xv = z(LANES) if vres_mix is None else jnp.concatenate([vres_mix, z(LANES - RW_V_RANK)])
    return jnp.concatenate([mix[:3 * RW_W], xw, z(LANES - RW_DECAY_RANK), xa, z(LANES - RW_A_RANK), xg, xv])[None, :]


def kernel(x, ln1_g, w_in, w_in_vres, rwkv_mix, rwkv_vres_mix, rwkv_w0, rwkv_w2, rwkv_a0, rwkv_a2, rwkv_v0, rwkv_v2, rwkv_g2, rwkv_k_k, rwkv_k_a, rwkv_r_k, rwkv_ln_g, rwkv_ln_b, nsa_q_norm, nsa_k_norm, nsa_cmp_pe, nsa_cmp_k_w1, nsa_cmp_k_w2, nsa_cmp_v_w1, nsa_cmp_v_w2, w_out, ln2_g, w_up, w_down):
    bsz, t, d = x.shape
    assert bsz == 1
    depth = w_in.shape[0]
    nc = t // CMP_STRIDE
    nb = t // SLC_BLOCK
    xs = x.reshape(t, d)

    pos = np.arange(t)
    cos_r, sin_r = _rope_tables(pos, RET_DH, 1)
    cos_n, sin_n = _rope_tables(pos, NSA_DH, 2)
    cos_c, sin_c = _rope_tables(np.arange(nc) * CMP_STRIDE + (CMP_BLOCK - 1), NSA_DH, 1)
    lane_head = np.arange(RW_W) // RW_N
    gsum = jnp.asarray(lane_head[:, None] == lane_head[None, :], BF16)
    lane_h2 = np.arange(LANES) // NSA_DH
    gmean = jnp.asarray((lane_h2[:, None] == lane_h2[None, :]) / float(NSA_DH), BF16)
    cstart = np.arange(nc) * CMP_STRIDE
    sstart = np.arange(nb) * SLC_BLOCK
    cis = jnp.asarray((cstart[:, None] <= sstart[None, :] + SLC_BLOCK - 1)
                      & (cstart[:, None] + CMP_BLOCK - 1 >= sstart[None, :]), BF16)

    ltri = jnp.asarray(np.tril(np.ones((nb, nb), np.float32), -1), BF16)

    v_first = jnp.zeros((t, RW_W), F32)
    for l in range(depth):
        vres = l > 0
        xv_w = _pad_cols(w_in_vres[l - 1], LANES) if vres else jnp.zeros((d, LANES), F32)
        w_cat = _pack_w_in_call(w_in, l, xv_w)
        proj = _norm_matmul(xs, ln1_g[l], w_cat, name="in_proj")

        o_ret = _retention(proj, cos_r, sin_r)

        row = lambda a: a.reshape(1, -1)
        mix = _pack_mix(rwkv_mix[l], rwkv_vres_mix[l - 1] if vres else None)
        v0 = row(rwkv_v0[l - 1]) if vres else jnp.zeros((1, RW_W), F32)
        v2 = _pad_rows(rwkv_v2[l - 1], LANES) if vres else jnp.zeros((LANES, RW_W), F32)
        r_, lw_, k_, v_, an_, b_, g_ = _rwkv_pre(
            proj, mix, row(rwkv_w0[l]), _pad_rows(rwkv_w2[l], LANES), row(rwkv_a0[l]), _pad_rows(rwkv_a2[l], LANES),
            v0, v2, rwkv_g2[l], row(rwkv_k_k[l]), row(rwkv_k_a[l]), gsum, v_first, vres)
        if not vres:
            v_first = v_
        o_rwkv = _wkv(r_, lw_, k_, v_, an_, b_, g_, row(rwkv_ln_g[l]), row(rwkv_ln_b[l]), row(rwkv_r_k[l]), gsum)

        n_q = NSA_W // LANES
        n_k = NSA_KV_W // LANES
        q_gain = nsa_q_norm[l] * (NSA_DH ** -0.5 * LOG2_E)
        w_slabs = jnp.stack([jnp.tile(q_gain, 2)] * n_q
                            + [jnp.tile(nsa_k_norm[l, 1], 2)] * n_k + [jnp.tile(nsa_k_norm[l, 2], 2)] * n_k)[:, None, :]
        s_max = 1.02 * jnp.max(jnp.abs(q_gain)) * jnp.max(jnp.abs(nsa_k_norm[l, 1])) * NSA_DH
        big = jnp.full((1, LANES), jnp.exp2(jnp.ceil(jnp.log2(2.0 * s_max + 128.0))), F32)
        kw_hm, hmat, v1, gates, qpad, ksx = _nsa_prep(proj, w_slabs, cos_n, sin_n, gmean, big)
        w1 = jnp.stack([nsa_cmp_k_w1[l], nsa_cmp_v_w1[l]]).astype(BF16)
        w2 = jnp.stack([nsa_cmp_k_w2[l], nsa_cmp_v_w2[l]]).astype(BF16)
        pe = jnp.broadcast_to(nsa_cmp_pe[l].reshape(2, 1, CMP_BLOCK * NSA_DH), (2, 8, CMP_BLOCK * NSA_DH)).astype(BF16)
        cmp_kv = _compress(hmat, w1, w2, pe, row(nsa_k_norm[l, 0]), cos_c, sin_c)
        o_nsa = _nsa_attn(kw_hm, qpad, ksx, gates, cmp_kv, v1, cis, ltri)

        xs = _out_proj(o_ret, o_rwkv, o_nsa, w_out, l, xs)

        up = _norm_matmul(xs, ln2_g[l], w_up, layer=l, act="relu2", out_dtype=BF16, name="mlp_up")
        xs = _matmul(up, w_down, layer=l, res=xs, name="mlp_down")
    return xs.reshape(bsz, t, d)
```

```python
import functools

import numpy as np
import jax
import jax.numpy as jnp
from jax import lax
from jax.experimental import pallas as pl
from jax.experimental.pallas import tpu as pltpu

F32 = jnp.float32
BF16 = jnp.bfloat16

RET_HEADS, RET_DH = 4, 128
RET_W = RET_HEADS * RET_DH
RW_HEADS, RW_N = 8, 64
RW_W = RW_HEADS * RW_N
RW_DECAY_RANK, RW_A_RANK, RW_V_RANK, RW_GATE_RANK = 96, 96, 64, 256
RW_GN_EPS = 64e-5
NSA_HEADS, NSA_G, NSA_DH = 16, 4, 64
NSA_HG = NSA_HEADS // NSA_G
NSA_W = NSA_HEADS * NSA_DH
NSA_KV_W = NSA_G * NSA_DH
CMP_BLOCK, CMP_STRIDE, CMP_HIDDEN = 32, 16, 256
SLC_BLOCK, SLC_TOPK, WINDOW = 64, 16, 512
NSA_TQ = 512
ROPE_THETA = 10000.0
RMS_EPS = 1e-6
NEG_INF = -1e30
FORCED_SCORE = 1e9

LANES = 128
VMEM_LIMIT = 56 * 1024 * 1024

C_RET = 0
C_RW_RKV = 2048
C_RW_XW = 3584
C_RW_XA = 3712
C_RW_XG = 3840
C_RW_XV = 4096
C_NSA_Q = 4224
PROJ_W = 6912


def _cparams(sem):
    return pltpu.CompilerParams(dimension_semantics=sem, vmem_limit_bytes=VMEM_LIMIT)


def _dot(a, b):
    return jnp.dot(a, b, preferred_element_type=F32)


def _dot_t(a, b):
    return lax.dot_general(a, b, (((1,), (1,)), ((), ())), preferred_element_type=F32)


def _dot_tl(a, b):
    return lax.dot_general(a, b, (((0,), (0,)), ((), ())), preferred_element_type=F32)


def _split_dot(x, m_bf16, passes):
    hi = x.astype(BF16)
    acc = _dot(hi, m_bf16)
    rem = x - hi.astype(F32)
    for _ in range(passes - 1):
        piece = rem.astype(BF16)
        acc = acc + _dot(piece, m_bf16)
        rem = rem - piece.astype(F32)
    return acc


def _mm_kernel(*refs, nk, act, has_res):
    a_ref, b_ref = refs[:2]
    r_ref = refs[2] if has_res else None
    o_ref = refs[3] if has_res else refs[2]

    def finish(y):
        if act == "relu2":
            y = jnp.square(jnp.maximum(y, 0.0))
        if has_res:
            y = y + r_ref[...]
        o_ref[...] = y.astype(o_ref.dtype)

    part = _dot(a_ref[...], b_ref[...].astype(BF16))
    if nk == 1:
        finish(part)
        return
    k = pl.program_id(2)
    if act is None and o_ref.dtype == F32:
        @pl.when(k == 0)
        def _():
            o_ref[...] = part + r_ref[...] if has_res else part

        @pl.when(k > 0)
        def _():
            o_ref[...] += part

        return
    acc_ref = refs[-1]

    @pl.when(k == 0)
    def _():
        acc_ref[...] = part

    @pl.when(k > 0)
    def _():
        acc_ref[...] += part

    @pl.when(k == nk - 1)
    def _():
        finish(acc_ref[...])


def _pick(n, cands):
    for c in cands:
        if n % c == 0:
            return c
    raise ValueError(f"no tile for {n}")


def _weight_spec(b, layer, block, index_map):
    if layer is None:
        return pl.BlockSpec(block, index_map)
    return pl.BlockSpec((None,) + block, lambda *idx: (layer,) + index_map(*idx))


def _matmul(a, b, *, layer=None, res=None, act=None, out_dtype=F32, name="matmul"):
    m, kd = a.shape
    n = b.shape[-1]
    tm = _pick(m, (1024, 512, 256, 128))
    tn = _pick(n, (1024, 768, 512, 256, 128))
    tk = _pick(kd, (2048, 1024, 512))
    nk = kd // tk
    in_specs = [pl.BlockSpec((tm, tk), lambda i, j, k: (i, k)),
                _weight_spec(b, layer, (tk, tn), lambda i, j, k: (k, j))]
    args = [a, b]
    if res is not None:
        in_specs.append(pl.BlockSpec((tm, tn), lambda i, j, k: (i, j)))
        args.append(res)
    return pl.pallas_call(
        functools.partial(_mm_kernel, nk=nk, act=act, has_res=res is not None),
        out_shape=jax.ShapeDtypeStruct((m, n), out_dtype),
        grid=(m // tm, n // tn, nk),
        in_specs=in_specs,
        out_specs=pl.BlockSpec((tm, tn), lambda i, j, k: (i, j)),
        scratch_shapes=[pltpu.VMEM((tm, tn), F32)] if nk > 1 and (act is not None or out_dtype != F32) else [],
        compiler_params=_cparams(("parallel", "parallel", "arbitrary")),
        name=name,
    )(*args)


def _norm_mm_kernel(x_ref, g_ref, b_ref, o_ref, h_ref, *, act):
    @pl.when(pl.program_id(1) == 0)
    def _():
        x = x_ref[...]
        ms = jnp.mean(x * x, axis=-1, keepdims=True)
        h_ref[...] = (x * lax.rsqrt(ms + RMS_EPS) * g_ref[...]).astype(BF16)

    y = _dot(h_ref[...], b_ref[...].astype(BF16))
    if act == "relu2":
        y = jnp.square(jnp.maximum(y, 0.0))
    o_ref[...] = y.astype(o_ref.dtype)


def _norm_matmul(x, g, b, *, layer=None, act=None, out_dtype=F32, name="norm_matmul"):
    m, kd = x.shape
    n = b.shape[-1]
    tm = _pick(m, (1024, 512, 256, 128))
    tn = _pick(n, (1024, 768, 512, 256, 128))
    return pl.pallas_call(
        functools.partial(_norm_mm_kernel, act=act),
        out_shape=jax.ShapeDtypeStruct((m, n), out_dtype),
        grid=(m // tm, n // tn),
        in_specs=[pl.BlockSpec((tm, kd), lambda i, j: (i, 0)),
                  pl.BlockSpec((1, kd), lambda i, j: (0, 0)),
                  _weight_spec(b, layer, (kd, tn), lambda i, j: (0, j))],
        out_specs=pl.BlockSpec((tm, tn), lambda i, j: (i, j)),
        scratch_shapes=[pltpu.VMEM((tm, kd), BF16)],
        compiler_params=_cparams(("parallel", "arbitrary")),
        name=name,
    )(x, g.reshape(1, kd), b)


def _out_proj_kernel(ret_ref, rw_ref, nsa_ref, w_ref, res_ref, o_ref, wb_ref):
    @pl.when(pl.program_id(1) == 0)
    def _():
        wb_ref[...] = w_ref[...].astype(BF16)

    k0 = ret_ref.shape[1]
    k1 = k0 + rw_ref.shape[1]
    acc = _dot(ret_ref[...], wb_ref[:k0, :])
    acc = acc + _dot(rw_ref[...], wb_ref[k0:k1, :])
    acc = acc + _dot(nsa_ref[...], wb_ref[k1:, :])
    o_ref[...] = acc + res_ref[...]


def _out_proj(o_ret, o_rwkv, o_nsa, w, layer, res):
    m = o_ret.shape[0]
    kd, n = w.shape[-2:]
    assert kd == o_ret.shape[1] + o_rwkv.shape[1] + o_nsa.shape[1]
    tm = _pick(m, (1024, 512, 256, 128))
    tn = _pick(n, (1024, 512, 256, 128))

    def rows(a):
        return pl.BlockSpec((tm, a.shape[1]), lambda j, i: (i, 0))

    return pl.pallas_call(
        _out_proj_kernel,
        out_shape=jax.ShapeDtypeStruct((m, n), F32),
        grid=(n // tn, m // tm),
        in_specs=[rows(o_ret), rows(o_rwkv), rows(o_nsa),
                  _weight_spec(w, layer, (kd, tn), lambda j, i: (0, j)),
                  pl.BlockSpec((tm, tn), lambda j, i: (i, j))],
        out_specs=pl.BlockSpec((tm, tn), lambda j, i: (i, j)),
        scratch_shapes=[pltpu.VMEM((kd, tn), BF16)],
        compiler_params=_cparams(("parallel", "arbitrary")),
        name="out_proj",
    )(o_ret, o_rwkv, o_nsa, w, res)


RET_CHUNK = 512


def _retention_kernel(q_ref, k_ref, v_ref, g_ref, cos_ref, sin_ref, dmat_ref, qd_ref, kd_ref, cd_ref, o_ref, state_ref):
    @pl.when(pl.program_id(0) == 0)
    def _():
        state_ref[...] = jnp.zeros_like(state_ref)

    cos = cos_ref[...]
    sin = sin_ref[...]

    def rope(x):
        return x * cos + pltpu.roll(x, RET_DH // 2, axis=1) * sin

    heads = range(RET_HEADS)
    hs = [slice(h * RET_DH, (h + 1) * RET_DH) for h in heads]
    q_all, k_all, v_all, gate = q_ref[...], k_ref[...], v_ref[...], g_ref[...]
    qb = [rope(q_all[:, s]).astype(BF16) for s in hs]
    kf = [rope(k_all[:, s]) * (RET_DH ** -0.5) for s in hs]
    kb = [kf[h].astype(BF16) for h in heads]
    vb = [v_all[:, s].astype(BF16) for s in hs]
    inner = [(_dot_t(qb[h], kb[h]) * dmat_ref[h]).astype(BF16) for h in heads]
    cross = [_dot(qb[h], state_ref[h].astype(BF16)) * qd_ref[h] for h in heads]
    kv = [_dot_tl((kf[h] * kd_ref[h]).astype(BF16), vb[h]) for h in heads]
    outs = []
    for h in heads:
        out = _dot(inner[h], vb[h]) + cross[h]
        state_ref[h] = cd_ref[h] * state_ref[h] + kv[h]
        outs.append(out * lax.rsqrt(jnp.mean(out * out, axis=-1, keepdims=True) + RMS_EPS))
    o_ref[...] = (gate * jax.nn.sigmoid(gate) * jnp.concatenate(outs, axis=1)).astype(o_ref.dtype)


def _retention(proj, cos, sin):
    t = proj.shape[0]
    chunk = RET_CHUNK
    log_gamma = np.log(1.0 - 2.0 ** (-5.0 - np.arange(RET_HEADS, dtype=np.float64)))[:, None, None]
    n = np.arange(chunk, dtype=np.float64)
    lag = n[:, None] - n[None, :]
    dmat = np.where(lag >= 0, np.exp(np.maximum(lag, 0.0)[None] * log_gamma), 0.0)
    ones = np.ones((1, 1, RET_DH))
    qd = np.exp((n + 1.0)[None, :, None] * log_gamma) * ones
    kd = np.exp((chunk - 1.0 - n)[None, :, None] * log_gamma) * ones
    cd = np.exp(chunk * log_gamma) * ones
    tables = [jnp.asarray(a, F32) for a in (dmat, qd, kd, cd)]
    base = C_RET // RET_W

    def col(off):
        return pl.BlockSpec((chunk, RET_W), lambda c, off=off: (c, base + off))

    def full(a):
        return pl.BlockSpec(a.shape, lambda c: (0, 0, 0))

    tab = pl.BlockSpec((chunk, RET_DH), lambda c: (c, 0))
    return pl.pallas_call(
        _retention_kernel,
        out_shape=jax.ShapeDtypeStruct((t, RET_W), BF16),
        grid=(t // chunk,),
        in_specs=[col(0), col(1), col(2), col(3), tab, tab] + [full(a) for a in tables],
        out_specs=pl.BlockSpec((chunk, RET_W), lambda c: (c, 0)),
        scratch_shapes=[pltpu.VMEM((RET_HEADS, RET_DH, RET_DH), F32)],
        compiler_params=_cparams(("arbitrary",)),
        name="retention",
    )(proj, proj, proj, proj, cos, sin, *tables)


def _dot_f32(a, b):
    a_hi = a.astype(BF16)
    b_hi = b.astype(BF16)
    a_lo = (a - a_hi.astype(F32)).astype(BF16)
    b_lo = (b - b_hi.astype(F32)).astype(BF16)
    return _dot(a_hi, b_hi) + (_dot(a_hi, b_lo) + _dot(a_lo, b_hi))


RW_PRE_TM = 512
_MIX_OFF = (0, 512, 1024, 1536, 1664, 1792, 2048, 2176)


def _rwkv_pre_kernel(r_ref, k_ref, v_ref, xw_ref, xa_ref, xg_ref, xv_ref,
                     rp_ref, kp_ref, vp_ref, xwp_ref, xap_ref, xgp_ref, xvp_ref,
                     mix_ref, w0_ref, w2_ref, a0_ref, a2_ref, v0_ref, v2_ref, g2_ref, kk_ref, ka_ref,
                     gsum_ref, vfirst_ref,
                     r_out, lw_out, k_out, v_out, an_out, b_out, g_out, *, use_vres):
    i = pl.program_id(0)

    def mixed(cur_ref, prev_ref, seg):
        cur = cur_ref[...]
        mix = mix_ref[:, _MIX_OFF[seg]:_MIX_OFF[seg + 1]]
        prev_row = jnp.where(i > 0, prev_ref[7:8, :], 0.0)
        rid = lax.broadcasted_iota(jnp.int32, cur.shape, 0)
        shifted = jnp.where(rid == 0, prev_row, pltpu.roll(cur, 1, axis=0))
        return cur + mix * (shifted - cur)

    r = mixed(r_ref, rp_ref, 0)
    k = mixed(k_ref, kp_ref, 1)
    v = mixed(v_ref, vp_ref, 2)
    xw = mixed(xw_ref, xwp_ref, 3)
    xa = mixed(xa_ref, xap_ref, 4)
    xg = mixed(xg_ref, xgp_ref, 5)

    z = -(w0_ref[...] + _dot_f32(jnp.tanh(xw), w2_ref[...]))
    softplus = jnp.maximum(z, 0.0) + jnp.log(1.0 + jnp.exp(-jnp.abs(z)))
    w = -softplus - 0.5
    lw_out[...] = -jnp.exp(w)
    a = jax.nn.sigmoid(a0_ref[...] + _dot_f32(xa, a2_ref[...]))
    g_out[...] = _dot_f32(jax.nn.sigmoid(xg), g2_ref[...])
    if use_vres:
        xv = mixed(xv_ref, xvp_ref, 6)
        v = v + (vfirst_ref[...] - v) * jax.nn.sigmoid(v0_ref[...] + _dot_f32(xv, v2_ref[...]))
    kk = k * kk_ref[...]
    ss = _split_dot(kk * kk, gsum_ref[...], 2)
    kk = kk / jnp.maximum(jnp.sqrt(ss), 1e-12)
    r_out[...] = r
    k_out[...] = k * (1.0 + (a - 1.0) * ka_ref[...])
    v_out[...] = v
    an_out[...] = -kk
    b_out[...] = kk * a


def _rwkv_pre(proj, mix, w0, w2, a0, a2, v0, v2, g2, k_k, k_a, gsum, v_first, use_vres):
    t = proj.shape[0]
    tm = RW_PRE_TM

    def cur(width, off):
        return pl.BlockSpec((tm, width), lambda i: (i, off // width))

    def prev(width, off):
        return pl.BlockSpec((8, width), lambda i: (jnp.maximum(i * (tm // 8) - 1, 0), off // width))

    def full(a):
        return pl.BlockSpec(a.shape, lambda i: (0,) * a.ndim)

    segs = [(RW_W, C_RW_RKV), (RW_W, C_RW_RKV + RW_W), (RW_W, C_RW_RKV + 2 * RW_W),
            (128, C_RW_XW), (128, C_RW_XA), (256, C_RW_XG), (128, C_RW_XV)]
    params = [mix, w0, w2, a0, a2, v0, v2, g2, k_k, k_a, gsum]
    row = pl.BlockSpec((tm, RW_W), lambda i: (i, 0))
    outs = pl.pallas_call(
        functools.partial(_rwkv_pre_kernel, use_vres=use_vres),
        out_shape=[jax.ShapeDtypeStruct((t, RW_W), F32)] * 7,
        grid=(t // tm,),
        in_specs=[cur(w, o) for w, o in segs] + [prev(w, o) for w, o in segs] + [full(p) for p in params] + [row],
        out_specs=[row] * 7,
        compiler_params=_cparams(("parallel",)),
        name="rwkv_pre",
    )(*([proj] * 14), *params, v_first)
    return outs


RW_CHUNK = 128
RW_CHUNKS_PER_STEP = 4


def _wkv_kernel(*refs):
    s_ref = refs[-1]

    @pl.when(pl.program_id(0) == 0)
    def _():
        s_ref[...] = jnp.zeros_like(s_ref)

    for sub in range(RW_CHUNKS_PER_STEP):
        _wkv_chunk(slice(sub * RW_CHUNK, (sub + 1) * RW_CHUNK), *refs)


def _wkv_chunk(rows, r_ref, lw_ref, k_ref, v_ref, an_ref, b_ref, g_ref, lng_ref, lnb_ref, rk_ref,
               gsum_ref, tril_ref, o_ref, s_ref):
    C = RW_CHUNK
    N = RW_N
    r = r_ref[rows, :]
    lw = lw_ref[rows, :]
    k = k_ref[rows, :]
    v = v_ref[rows, :]
    lw_hi = lw.astype(BF16)
    rem = lw - lw_hi.astype(F32)
    lw_mid = rem.astype(BF16)
    lw_lo = (rem - lw_mid.astype(F32)).astype(BF16)
    tril = tril_ref[...]
    lg = _dot(tril, lw_hi) + (_dot(tril, lw_mid) + _dot(tril, lw_lo))
    gam = jnp.exp(lg)
    ginv = jnp.exp(-lg)
    at = (an_ref[rows, :] * jnp.exp(lg - lw)).astype(BF16)
    bt = (b_ref[rows, :] * ginv).astype(BF16)
    kt = (k * ginv).astype(BF16)
    rt_f = r * gam
    rt = rt_f.astype(BF16)
    vb = v.astype(BF16)
    g_last = gam[C - 1:C, :]

    rowi = lax.broadcasted_iota(jnp.int32, (C, C), 0)
    coli = lax.broadcasted_iota(jnp.int32, (C, C), 1)
    strict = rowi > coli
    incl = rowi >= coli

    heads = range(RW_HEADS)
    hs = [slice(h * N, (h + 1) * N) for h in heads]
    bth = [bt[:, s] for s in hs]
    kth = [kt[:, s] for s in hs]
    vh = [vb[:, s] for s in hs]
    big = [_dot_t(jnp.concatenate([at[:, s], rt[:, s]], axis=0), jnp.concatenate([bth[h], kth[h]], axis=0))
           for h, s in zip(heads, hs)]
    a_ab = [jnp.where(strict, big[h][:C, :C], 0.0) for h in heads]
    a_ak = [jnp.where(strict, big[h][:C, C:], 0.0).astype(BF16) for h in heads]
    a_rb = [jnp.where(incl, big[h][C:, :C], 0.0).astype(BF16) for h in heads]
    a_rk = [jnp.where(incl, big[h][C:, C:], 0.0).astype(BF16) for h in heads]
    akv = [_dot(a_ak[h], vh[h]) for h in heads]
    def same_block(size):
        shift = size.bit_length() - 1
        return (rowi >> shift) == (coli >> shift)

    tinv = [jnp.where(same_block(2), a_ab[h], 0.0) + jnp.where(rowi == coli, 1.0, 0.0) for h in heads]
    size = 2
    while size < C:
        lower_left = same_block(2 * size) & jnp.logical_not(same_block(size))
        off = [jnp.where(lower_left, a_ab[h], 0.0).astype(BF16) for h in heads]
        tb = [tinv[h].astype(BF16) for h in heads]
        half = [_dot(tb[h], off[h]).astype(BF16) for h in heads]
        tinv = [tinv[h] + _dot(half[h], tb[h]) for h in heads]
        size *= 2
    xb = [_dot(tinv[h].astype(BF16), jnp.concatenate([at[:, hs[h]], akv[h].astype(BF16)], axis=1)).astype(BF16)
          for h in heads]
    yx = [_dot(a_rb[h], xb[h]) for h in heads]
    ykv = [_dot(a_rk[h], vh[h]) for h in heads]
    xtb = [_dot_tl(xb[h], bth[h]) for h in heads]
    vtk = [_dot_tl(vh[h], kth[h]) for h in heads]
    ys = []
    for h in heads:
        gl = g_last[:, hs[h]]
        y1 = (rt_f[:, hs[h]] + yx[h][:, :N]).astype(BF16)
        s0 = s_ref[h]
        s0b = s0.astype(BF16)
        ys.append(_dot_t(y1, s0b) + (yx[h][:, N:] + ykv[h]))
        s_ref[h] = (s0 + _dot(s0b, xtb[h][:N].astype(BF16)) + (xtb[h][N:] + vtk[h])) * gl
    y = jnp.concatenate(ys, axis=1)

    gsum = gsum_ref[...]
    inv_n = 1.0 / N
    mu = _split_dot(y, gsum, 2) * inv_n
    yc = y - mu
    var = _split_dot(yc * yc, gsum, 2) * inv_n
    yn = yc * lax.rsqrt(var + RW_GN_EPS) * lng_ref[...] + lnb_ref[...]
    bonus = _split_dot(r * k * rk_ref[...], gsum, 2)
    o_ref[rows, :] = ((yn + bonus * v) * g_ref[rows, :]).astype(o_ref.dtype)


def _wkv(r, lw, k, v, an, b, g, ln_g, ln_b, r_k, gsum):
    t = r.shape[0]
    C = RW_CHUNK
    tril = jnp.asarray(np.tril(np.ones((C, C), np.float32)), BF16)
    rows = RW_CHUNKS_PER_STEP * C
    row = pl.BlockSpec((rows, RW_W), lambda c: (c, 0))
    vec = pl.BlockSpec((1, RW_W), lambda c: (0, 0))
    return pl.pallas_call(
        _wkv_kernel,
        out_shape=jax.ShapeDtypeStruct((t, RW_W), BF16),
        grid=(t // rows,),
        in_specs=[row] * 7 + [vec] * 3 + [pl.BlockSpec((RW_W, RW_W), lambda c: (0, 0)),
                                          pl.BlockSpec((C, C), lambda c: (0, 0))],
        out_specs=row,
        scratch_shapes=[pltpu.VMEM((RW_HEADS, RW_N, RW_N), F32)],
        compiler_params=_cparams(("arbitrary",)),
        name="wkv7",
    )(r, lw, k, v, an, b, g, ln_g, ln_b, r_k, gsum, tril)


NSA_PREP_TM = 512


N_NORM_SLABS = (NSA_W + 2 * NSA_KV_W) // LANES
N_RAW_SLABS = 4 * NSA_KV_W // LANES
N_NSA_SLABS = N_NORM_SLABS + N_RAW_SLABS + 1


def _nsa_prep_kernel(*refs):
    x_refs = refs[:N_NSA_SLABS]
    (cos_ref, sin_ref, w_ref, gmean_ref, big_ref,
     kw_ref, hmat_ref, v1_ref, gate_ref, qpad_ref, ksx_ref) = refs[N_NSA_SLABS:]
    cos = cos_ref[...]
    sin = sin_ref[...]
    gmean = gmean_ref[...]
    half = NSA_DH // 2
    tm = cos.shape[0]
    lane = lax.broadcasted_iota(jnp.int32, cos.shape, 1)
    first_half = (lane % NSA_DH) < half
    low = lane < NSA_DH
    blk_row = (pl.program_id(0) * tm + lax.broadcasted_iota(jnp.int32, cos.shape, 0)) // SLC_BLOCK
    big = big_ref[...]
    mark_lo = jnp.where(blk_row == lane - NSA_DH, big, 0.0)
    mark_hi = jnp.where(low & (blk_row == lane + NSA_DH), big, 0.0).astype(BF16)
    n_q = NSA_W // LANES
    n_k = NSA_KV_W // LANES
    for s in range(N_NORM_SLABS):
        x = x_refs[s][...]
        y = x * lax.rsqrt(_split_dot(x * x, gmean, 2) + RMS_EPS) * w_ref[s]
        rot = jnp.where(first_half, pltpu.roll(y, LANES - half, axis=1), pltpu.roll(y, half, axis=1))
        res32 = y * cos + rot * sin
        if s >= n_q + n_k:
            res = res32.astype(kw_ref.dtype)
            kw_ref[2 * (s - n_q - n_k)] = res[:, :NSA_DH]
            kw_ref[2 * (s - n_q - n_k) + 1] = res[:, NSA_DH:]
            continue
        swapped = pltpu.roll(res32, NSA_DH, axis=1)
        if s < n_q:
            qpad_ref[2 * s] = jnp.where(low, res32, 0.0).astype(BF16)
            qpad_ref[2 * s + 1] = jnp.where(low, swapped, 0.0).astype(BF16)
        else:
            g0 = 2 * (s - n_q)
            ksx_ref[g0, :, :LANES] = jnp.where(low, res32, mark_lo).astype(BF16)
            ksx_ref[g0 + 1, :, :LANES] = jnp.where(low, swapped, mark_lo).astype(BF16)
            ksx_ref[g0, :, LANES:] = mark_hi
            ksx_ref[g0 + 1, :, LANES:] = mark_hi

    rows = hmat_ref.shape[2]
    low_r = lax.broadcasted_iota(jnp.int32, (rows, LANES), 1) < NSA_DH
    for s in range(NSA_KV_W // LANES * 2):
        kind, pair = divmod(s, NSA_KV_W // LANES)
        x_ref = x_refs[N_NORM_SLABS + s]
        for t2 in range(CMP_STRIDE // 2):
            r0 = x_ref[pl.ds(2 * t2, rows, stride=CMP_STRIDE), :]
            r1 = x_ref[pl.ds(2 * t2 + 1, rows, stride=CMP_STRIDE), :]
            cols = slice(t2 * LANES, (t2 + 1) * LANES)
            hmat_ref[kind, 2 * pair, :, cols] = jnp.where(low_r, r0, pltpu.roll(r1, NSA_DH, axis=1)).astype(BF16)
            hmat_ref[kind, 2 * pair + 1, :, cols] = jnp.where(low_r, pltpu.roll(r0, NSA_DH, axis=1), r1).astype(BF16)

    one_col = jnp.where(lane == NSA_DH, 1.0, 0.0)
    for s in range(NSA_KV_W // LANES * 2):
        x = x_refs[N_NORM_SLABS + N_RAW_SLABS // 2 + s][...]
        v1_ref[2 * s] = jnp.where(low, x, one_col).astype(BF16)
        v1_ref[2 * s + 1] = jnp.where(low, pltpu.roll(x, NSA_DH, axis=1), one_col).astype(BF16)

    xg = x_refs[N_NSA_SLABS - 1][...]
    per_group = 3 * NSA_HG
    for g in range(NSA_G):
        shifted = xg if g == 0 else pltpu.roll(xg, LANES - per_group * g, axis=1)
        gate_ref[g] = jnp.where(lane < per_group, shifted, 0.0)


def _nsa_prep(proj, w_slabs, cos, sin, gmean, big):
    t = proj.shape[0]
    assert t // SLC_BLOCK <= LANES
    tm = NSA_PREP_TM
    base = C_NSA_Q // LANES
    tab = pl.BlockSpec((tm, LANES), lambda i: (i, 0))
    return pl.pallas_call(
        _nsa_prep_kernel,
        out_shape=[jax.ShapeDtypeStruct((NSA_G, t, NSA_DH), BF16),
                   jax.ShapeDtypeStruct((2, NSA_G, t // CMP_STRIDE, CMP_STRIDE * NSA_DH), BF16),
                   jax.ShapeDtypeStruct((2 * NSA_G, t, LANES), BF16),
                   jax.ShapeDtypeStruct((NSA_G, t, LANES), F32),
                   jax.ShapeDtypeStruct((NSA_HEADS, t, LANES), BF16),
                   jax.ShapeDtypeStruct((NSA_G, t, 2 * LANES), BF16)],
        grid=(t // tm,),
        in_specs=[pl.BlockSpec((tm, LANES), lambda i, s=s: (i, base + s)) for s in range(N_NSA_SLABS)]
        + [tab, tab, pl.BlockSpec((N_NORM_SLABS, 1, LANES), lambda i: (0, 0, 0)),
           pl.BlockSpec((LANES, LANES), lambda i: (0, 0)), pl.BlockSpec((1, LANES), lambda i: (0, 0))],
        out_specs=[pl.BlockSpec((NSA_G, tm, NSA_DH), lambda i: (0, i, 0)),
                   pl.BlockSpec((2, NSA_G, tm // CMP_STRIDE, CMP_STRIDE * NSA_DH), lambda i: (0, 0, i, 0)),
                   pl.BlockSpec((2 * NSA_G, tm, LANES), lambda i: (0, i, 0)),
                   pl.BlockSpec((NSA_G, tm, LANES), lambda i: (0, i, 0)),
                   pl.BlockSpec((NSA_HEADS, tm, LANES), lambda i: (0, i, 0)),
                   pl.BlockSpec((NSA_G, tm, 2 * LANES), lambda i: (0, i, 0))],
        compiler_params=_cparams(("parallel",)),
        name="nsa_prep",
    )(*([proj] * N_NSA_SLABS), cos, sin, w_slabs, gmean, big)


def _compress_kernel(h_ref, w1_ref, w2_ref, pe_ref, nw_ref, cos_ref, sin_ref, o_ref):
    kind = pl.program_id(0)
    nc = h_ref.shape[2]
    half_in = CMP_STRIDE * NSA_DH
    hm = h_ref[0, 0]
    w1 = w1_ref[0]
    first = _dot(hm, w1[:half_in])
    second = _dot(hm, w1[half_in:])
    const = _dot(pe_ref[0], w1)[0:1, :]
    pre = first + pltpu.roll(second, nc - 1, axis=0) + const
    hid = 0.5 * pre * (1.0 + jnp.tanh(0.7978845608028654 * (pre + 0.044715 * pre * pre * pre)))
    out = _dot(hid.astype(BF16), w2_ref[0])
    y = out * lax.rsqrt(jnp.mean(out * out, axis=-1, keepdims=True) + RMS_EPS) * nw_ref[...]
    half = NSA_DH // 2
    rot = jnp.concatenate([y[:, half:], y[:, :half]], axis=1)
    roped = y * cos_ref[...] + rot * sin_ref[...]
    o_ref[0, 0] = jnp.where(kind == 0, roped, out).astype(o_ref.dtype)


def _compress(hmat, w1, w2, pe, nw, cos_c, sin_c):
    _, g, nc, width = hmat.shape
    return pl.pallas_call(
        _compress_kernel,
        out_shape=jax.ShapeDtypeStruct((2, g, nc, NSA_DH), BF16),
        grid=(2, g),
        in_specs=[pl.BlockSpec((1, 1, nc, width), lambda a, b: (a, b, 0, 0)),
                  pl.BlockSpec((1, 2 * width, CMP_HIDDEN), lambda a, b: (a, 0, 0)),
                  pl.BlockSpec((1, CMP_HIDDEN, NSA_DH), lambda a, b: (a, 0, 0)),
                  pl.BlockSpec((1, 8, 2 * width), lambda a, b: (a, 0, 0)),
                  pl.BlockSpec((1, NSA_DH), lambda a, b: (0, 0)),
                  pl.BlockSpec((nc, NSA_DH), lambda a, b: (0, 0)),
                  pl.BlockSpec((nc, NSA_DH), lambda a, b: (0, 0))],
        out_specs=pl.BlockSpec((1, 1, nc, NSA_DH), lambda a, b: (a, b, 0, 0)),
        compiler_params=_cparams(("parallel", "parallel")),
        name="nsa_compress",
    )(hmat, w1, w2, pe, nw, cos_c, sin_c)


SLC_KT = 1024
WIN_ROWS = 128
WIN_KEYS = WINDOW + WIN_ROWS


INT32_MIN = -2 ** 31


def _nsa_attn_kernel(qpad_ref, gate_ref, kc_ref, vc_ref, ks_ref, vs_ref, kw_ref, vw_ref, cis_ref, ltri_ref, o_ref):
    qi = pl.program_id(1)
    nc = kc_ref.shape[2]
    nb = cis_ref.shape[1]
    heads = range(NSA_HG)
    start = qi * NSA_TQ
    qh = [qpad_ref[hh][:, :NSA_DH] for hh in heads]
    tok = start + lax.broadcasted_iota(jnp.int32, (NSA_TQ, 1), 0)

    def softmax_terms(scores):
        return [jnp.exp(s - jnp.max(s, axis=-1, keepdims=True)) for s in scores]

    def cmp_branch(width):
        kc = kc_ref[0, 0, :width, :]
        vc = vc_ref[0, 0, :width, :]
        cend = lax.broadcasted_iota(jnp.int32, (1, width), 1) * CMP_STRIDE + (CMP_BLOCK - 1)
        bias_c = jnp.where(cend <= tok, 0.0, NEG_INF)
        ec = softmax_terms([_dot_t(qh[hh], kc) + bias_c for hh in heads])
        sees_any = tok >= CMP_BLOCK - 1
        pc = [ec[hh] * jnp.where(sees_any, 1.0 / jnp.sum(ec[hh], axis=-1, keepdims=True), 0.0) for hh in heads]
        outs_c = tuple(_dot(pc[hh].astype(BF16), vc) for hh in heads)
        psum = pc[0]
        for hh in range(1, NSA_HG):
            psum = psum + pc[hh]
        return outs_c, _split_dot(psum, cis_ref[:width, :], 2)

    n_quarters = 8
    quarter = nc // n_quarters
    n_visible = (start + NSA_TQ - CMP_BLOCK) // CMP_STRIDE + 1
    which = jnp.clip((n_visible - 1) // quarter, 0, n_quarters - 1)
    o_c, score = lax.switch(which, [functools.partial(cmp_branch, (r + 1) * quarter) for r in range(n_quarters)])
    blk = lax.broadcasted_iota(jnp.int32, (1, nb), 1)
    cur = tok // SLC_BLOCK
    forced = (blk == 0) | (blk == cur) | (blk == cur - 1)
    score = jnp.where(forced, FORCED_SCORE, score)
    score = jnp.where(blk <= cur, score, -jnp.inf)

    bits = lax.bitcast_convert_type(score.T, jnp.int32)
    key_all = bits ^ ((bits >> 31) & 0x7FFFFFFF)

    def select_blocks(nbv):
        key = key_all[:nbv]
        n_sel = min(SLC_TOPK, nbv)

        def enough(c):
            return jnp.sum(jnp.where(key >= c, 1.0, 0.0), axis=0, keepdims=True) >= n_sel

        def enough3(c1, c2, c3):
            packed = jnp.where(key >= c3, 65793.0, jnp.where(key >= c2, 257.0, jnp.where(key >= c1, 1.0, 0.0)))
            tot = jnp.sum(packed, axis=0, keepdims=True).astype(jnp.int32)
            return (tot & 255) >= n_sel, ((tot >> 8) & 255) >= n_sel, (tot >> 16) >= n_sel

        zero_row = jnp.zeros((1, NSA_TQ), jnp.int32)
        thr = jnp.where(enough(zero_row), zero_row, INT32_MIN)
        for hi in range(30, 0, -2):
            c1 = thr + (1 << (hi - 1))
            c2 = thr + (1 << hi)
            c3 = c2 + (1 << (hi - 1))
            e1, e2, e3 = enough3(c1, c2, c3)
            thr = jnp.where(e3, c3, jnp.where(e2, c2, jnp.where(e1, c1, thr)))
        c1 = thr + 1
        thr = jnp.where(enough(c1), c1, thr)
        above = key > thr
        tied = key == thr
        need = n_sel - jnp.sum(jnp.where(above, 1.0, 0.0), axis=0, keepdims=True)
        tied_before = _dot(ltri_ref[:nbv, :nbv], jnp.where(tied, 1.0, 0.0).astype(BF16))
        chosen = jnp.where(above | (tied & (tied_before < need)), 1.0, 0.0)
        if nbv < LANES:
            chosen = jnp.concatenate([chosen, jnp.zeros((LANES - nbv, NSA_TQ), F32)], axis=0)
        return chosen.T

    assert nb <= LANES
    n_steps = 4
    blocks_visible = (start + NSA_TQ - 1) // SLC_BLOCK + 1
    widths = sorted({max(-(-nb * (r + 1) // n_steps // 8) * 8, min(nb, SLC_TOPK)) for r in range(n_steps)})
    which_sel = sum((blocks_visible > w).astype(jnp.int32) for w in widths[:-1])
    sel = lax.switch(which_sel, [functools.partial(select_blocks, w) for w in widths])
    lane_q = lax.broadcasted_iota(jnp.int32, (NSA_TQ, LANES), 1)
    sel_sw = pltpu.roll(sel, NSA_DH, axis=1)
    sel_lo = jnp.where(lane_q >= NSA_DH, sel_sw, 0.0)
    sel_hi = jnp.where(lane_q < NSA_DH, sel_sw, 0.0)
    qx = [jnp.concatenate([qpad_ref[hh].astype(F32) + sel_lo, sel_hi], axis=1).astype(BF16) for hh in heads]

    def window_rows(r0):
        kbase = pl.multiple_of(jnp.maximum(start + r0 - WINDOW, 0), WIN_ROWS)
        kw = kw_ref[0, pl.ds(kbase, WIN_KEYS), :]
        vw = vw_ref[0, pl.ds(kbase, WIN_KEYS), :]
        lag = tok[r0:r0 + WIN_ROWS] - (kbase + lax.broadcasted_iota(jnp.int32, (1, WIN_KEYS), 1))
        bias_w = jnp.where((lag >= 0) & (lag < WINDOW), 0.0, NEG_INF)
        sw = [_dot_t(qh[hh][r0:r0 + WIN_ROWS], kw) + bias_w for hh in heads]
        ew = [jnp.exp((s - jnp.max(s, axis=-1, keepdims=True)).astype(BF16)) for s in sw]
        return [_dot(ew[hh], vw) for hh in heads]

    win_parts = [window_rows(r0) for r0 in range(0, NSA_TQ, WIN_ROWS)]
    acc_w = [jnp.concatenate([part[hh] for part in win_parts], axis=0) for hh in heads]

    n_tiles = (start + NSA_TQ + SLC_KT - 1) // SLC_KT

    def sel_tile(j, carry, causal, width=SLC_KT):
        ms, accs = carry
        k0 = pl.multiple_of(j * SLC_KT, SLC_KT)
        kt = ks_ref[0, pl.ds(k0, width), :]
        vt = vs_ref[0, pl.ds(k0, width), :]
        sj = [_dot_t(qx[hh], kt) for hh in heads]
        if causal:
            future = (k0 + lax.broadcasted_iota(jnp.int32, (1, width), 1)) > tok
            sj = [jnp.where(future, NEG_INF, s) for s in sj]
        m_new = [jnp.maximum(ms[hh], jnp.max(sj[hh], axis=-1, keepdims=True)) for hh in heads]
        pj = [jnp.exp((sj[hh] - m_new[hh]).astype(BF16)) for hh in heads]
        accs = [jnp.exp(ms[hh] - m_new[hh]) * accs[hh] + _dot(pj[hh], vt) for hh in heads]
        return tuple(m_new), tuple(accs)

    m0 = tuple(jnp.full((NSA_TQ, 1), NEG_INF, F32) for _ in heads)
    a0 = tuple(jnp.zeros((NSA_TQ, LANES), F32) for _ in heads)
    carry = lax.fori_loop(0, n_tiles - 1, functools.partial(sel_tile, causal=False), (m0, a0))
    per_tile = SLC_KT // NSA_TQ
    last = [functools.partial(sel_tile, n_tiles - 1, causal=True, width=(r + 1) * NSA_TQ) for r in range(per_tile)]
    _, acc_s = lax.switch(qi % per_tile, last, carry)

    gt = jax.nn.sigmoid(gate_ref[0])
    outs = []
    for hh in heads:
        g0, g1, g2 = (gt[:, 3 * hh + br:3 * hh + br + 1] for br in range(3))
        scale_s = g1 / acc_s[hh][:, NSA_DH:NSA_DH + 1]
        scale_w = g2 / acc_w[hh][:, NSA_DH:NSA_DH + 1]
        outs.append(g0 * o_c[hh] + scale_s * acc_s[hh][:, :NSA_DH] + scale_w * acc_w[hh][:, :NSA_DH])
    o_ref[...] = jnp.concatenate(outs, axis=1).astype(o_ref.dtype)


def _nsa_attn(kw_hm, qpad, ksx, gates, cmp_kv, v1, cis, ltri):
    t = kw_hm.shape[1]
    nq = t // NSA_TQ
    nc = cmp_kv.shape[2]
    nb = cis.shape[1]
    ks_spec = pl.BlockSpec((1, t, 2 * LANES), lambda g, i: (g, 0, 0))
    kw_spec = pl.BlockSpec((1, t, NSA_DH), lambda g, i: (g, 0, 0))
    vs_spec = pl.BlockSpec((1, t, LANES), lambda g, i: (g, 0, 0))
    vw_spec = pl.BlockSpec((1, t, LANES), lambda g, i: (NSA_G + g, 0, 0))
    return pl.pallas_call(
        _nsa_attn_kernel,
        out_shape=jax.ShapeDtypeStruct((t, NSA_W), BF16),
        grid=(NSA_G, nq),
        in_specs=[pl.BlockSpec((NSA_HG, NSA_TQ, LANES), lambda g, i: (g, i, 0)),
                  pl.BlockSpec((1, NSA_TQ, LANES), lambda g, i: (g, i, 0)),
                  pl.BlockSpec((1, 1, nc, NSA_DH), lambda g, i: (0, g, 0, 0)),
                  pl.BlockSpec((1, 1, nc, NSA_DH), lambda g, i: (1, g, 0, 0)),
                  ks_spec, vs_spec, kw_spec, vw_spec,
                  pl.BlockSpec((nc, nb), lambda g, i: (0, 0)),
                  pl.BlockSpec((nb, nb), lambda g, i: (0, 0))],
        out_specs=pl.BlockSpec((NSA_TQ, NSA_HG * NSA_DH), lambda g, i: (i, g)),
        compiler_params=_cparams(("parallel", "arbitrary")),
        name="nsa_attention",
    )(qpad, gates, cmp_kv, cmp_kv, ksx, v1, kw_hm, v1, cis, ltri)


def _rope_tables(pos, dh, reps):
    half = dh // 2
    inv_freq = ROPE_THETA ** (-np.arange(half, dtype=np.float64) / half)
    ang = np.asarray(pos, np.float64)[:, None] * inv_freq[None, :]
    cos = np.cos(ang)
    sin = np.sin(ang)
    cos_t = np.tile(np.concatenate([cos, cos], axis=1), (1, reps))
    sin_t = np.tile(np.concatenate([-sin, sin], axis=1), (1, reps))
    return jnp.asarray(cos_t, F32), jnp.asarray(sin_t, F32)


def _pad_cols(a, width):
    return jnp.pad(a, ((0, 0), (0, width - a.shape[1])))


def _pad_rows(a, height):
    return jnp.pad(a, ((0, height - a.shape[0]), (0, 0)))


def _pack_w_in_kernel(w_ref, xv_ref, o_ref):
    o_ref[...] = _pack_w_in(w_ref[...], xv_ref[...]).astype(o_ref.dtype)


def _pack_w_in_call(w_in, layer, xv):
    _, d, n_in = w_in.shape
    tk = 256
    return pl.pallas_call(
        _pack_w_in_kernel,
        out_shape=jax.ShapeDtypeStruct((d, PROJ_W), BF16),
        grid=(d // tk,),
        in_specs=[pl.BlockSpec((None, tk, n_in), lambda i: (layer, i, 0)),
                  pl.BlockSpec((tk, LANES), lambda i: (i, 0))],
        out_specs=pl.BlockSpec((tk, PROJ_W), lambda i: (i, 0)),
        compiler_params=_cparams(("parallel",)),
        name="pack_w_in",
    )(w_in, xv)


def _pack_w_in(w_l, xv):
    d = w_l.shape[0]
    o = 0
    ret = w_l[:, o:o + 4 * RET_W]; o += 4 * RET_W
    rkv = w_l[:, o:o + 3 * RW_W]; o += 3 * RW_W
    xw = w_l[:, o:o + RW_DECAY_RANK]; o += RW_DECAY_RANK
    xa = w_l[:, o:o + RW_A_RANK]; o += RW_A_RANK
    xg = w_l[:, o:o + RW_GATE_RANK]; o += RW_GATE_RANK
    q = w_l[:, o:o + NSA_W]; o += NSA_W
    kc, vc, ks, vs, kw, vw = (w_l[:, o + i * NSA_KV_W:o + (i + 1) * NSA_KV_W] for i in range(6))
    o += 6 * NSA_KV_W
    gates = w_l[:, o:o + 3 * NSA_HEADS]
    return jnp.concatenate([ret, rkv, _pad_cols(xw, LANES), _pad_cols(xa, LANES), xg, xv,
                            q, ks, kw, kc, vc, vs, vw, _pad_cols(gates, LANES)], axis=1)


def _pack_mix(mix, vres_mix):
    o = 3 * RW_W
    xw = mix[o:o + RW_DECAY_RANK]; o += RW_DECAY_RANK
    xa = mix[o:o + RW_A_RANK]; o += RW_A_RANK
    xg = mix[o:o + RW_GATE_RANK]
    z = lambda n: jnp.zeros((n,), mix.dtype)
    xv = z(LANES) if vres_mix is None else jnp.concatenate([vres_mix, z(LANES - RW_V_RANK)])
    return jnp.concatenate([mix[:3 * RW_W], xw, z(LANES - RW_DECAY_RANK), xa, z(LANES - RW_A_RANK), xg, xv])[None, :]


def kernel(x, ln1_g, w_in, w_in_vres, rwkv_mix, rwkv_vres_mix, rwkv_w0, rwkv_w2, rwkv_a0, rwkv_a2, rwkv_v0, rwkv_v2, rwkv_g2, rwkv_k_k, rwkv_k_a, rwkv_r_k, rwkv_ln_g, rwkv_ln_b, nsa_q_norm, nsa_k_norm, nsa_cmp_pe, nsa_cmp_k_w1, nsa_cmp_k_w2, nsa_cmp_v_w1, nsa_cmp_v_w2, w_out, ln2_g, w_up, w_down):
    bsz, t, d = x.shape
    assert bsz == 1
    depth = w_in.shape[0]
    nc = t // CMP_STRIDE
    nb = t // SLC_BLOCK
    xs = x.reshape(t, d)

    pos = np.arange(t)
    cos_r, sin_r = _rope_tables(pos, RET_DH, 1)
    cos_n, sin_n = _rope_tables(pos, NSA_DH, 2)
    cos_c, sin_c = _rope_tables(np.arange(nc) * CMP_STRIDE + (CMP_BLOCK - 1), NSA_DH, 1)
    lane_head = np.arange(RW_W) // RW_N
    gsum = jnp.asarray(lane_head[:, None] == lane_head[None, :], BF16)
    lane_h2 = np.arange(LANES) // NSA_DH
    gmean = jnp.asarray((lane_h2[:, None] == lane_h2[None, :]) / float(NSA_DH), BF16)
    cstart = np.arange(nc) * CMP_STRIDE
    sstart = np.arange(nb) * SLC_BLOCK
    cis = jnp.asarray((cstart[:, None] <= sstart[None, :] + SLC_BLOCK - 1)
                      & (cstart[:, None] + CMP_BLOCK - 1 >= sstart[None, :]), BF16)

    ltri = jnp.asarray(np.tril(np.ones((nb, nb), np.float32), -1), BF16)

    v_first = jnp.zeros((t, RW_W), F32)
    for l in range(depth):
        vres = l > 0
        xv_w = _pad_cols(w_in_vres[l - 1], LANES) if vres else jnp.zeros((d, LANES), F32)
        w_cat = _pack_w_in_call(w_in, l, xv_w)
        proj = _norm_matmul(xs, ln1_g[l], w_cat, name="in_proj")

        o_ret = _retention(proj, cos_r, sin_r)

        row = lambda a: a.reshape(1, -1)
        mix = _pack_mix(rwkv_mix[l], rwkv_vres_mix[l - 1] if vres else None)
        v0 = row(rwkv_v0[l - 1]) if vres else jnp.zeros((1, RW_W), F32)
        v2 = _pad_rows(rwkv_v2[l - 1], LANES) if vres else jnp.zeros((LANES, RW_W), F32)
        r_, lw_, k_, v_, an_, b_, g_ = _rwkv_pre(
            proj, mix, row(rwkv_w0[l]), _pad_rows(rwkv_w2[l], LANES), row(rwkv_a0[l]), _pad_rows(rwkv_a2[l], LANES),
            v0, v2, rwkv_g2[l], row(rwkv_k_k[l]), row(rwkv_k_a[l]), gsum, v_first, vres)
        if not vres:
            v_first = v_
        o_rwkv = _wkv(r_, lw_, k_, v_, an_, b_, g_, row(rwkv_ln_g[l]), row(rwkv_ln_b[l]), row(rwkv_r_k[l]), gsum)

        n_q = NSA_W // LANES
        n_k = NSA_KV_W // LANES
        w_slabs = jnp.stack([jnp.tile(nsa_q_norm[l] * (NSA_DH ** -0.5), 2)] * n_q
                            + [jnp.tile(nsa_k_norm[l, 1], 2)] * n_k + [jnp.tile(nsa_k_norm[l, 2], 2)] * n_k)[:, None, :]
        s_max = 1.02 * jnp.max(jnp.abs(nsa_q_norm[l])) * jnp.max(jnp.abs(nsa_k_norm[l, 1])) * (NSA_DH ** 0.5)
        big = jnp.full((1, LANES), jnp.exp2(jnp.ceil(jnp.log2(2.0 * s_max + 128.0))), F32)
        kw_hm, hmat, v1, gates, qpad, ksx = _nsa_prep(proj, w_slabs, cos_n, sin_n, gmean, big)
        w1 = jnp.stack([nsa_cmp_k_w1[l], nsa_cmp_v_w1[l]]).astype(BF16)
        w2 = jnp.stack([nsa_cmp_k_w2[l], nsa_cmp_v_w2[l]]).astype(BF16)
        pe = jnp.broadcast_to(nsa_cmp_pe[l].reshape(2, 1, CMP_BLOCK * NSA_DH), (2, 8, CMP_BLOCK * NSA_DH)).astype(BF16)
        cmp_kv = _compress(hmat, w1, w2, pe, row(nsa_k_norm[l, 0]), cos_c, sin_c)
        o_nsa = _nsa_attn(kw_hm, qpad, ksx, gates, cmp_kv, v1, cis, ltri)

        xs = _out_proj(o_ret, o_rwkv, o_nsa, w_out, l, xs)

        up = _norm_matmul(xs, ln2_g[l], w_up, layer=l, act="relu2", out_dtype=BF16, name="mlp_up")
        xs = _matmul(up, w_down, layer=l, res=xs, name="mlp_down")
    return xs.reshape(bsz, t, d)
```
